```python
import math
import jax, jax.numpy as jnp
from jax import lax
import numpy as np

D_MODEL = 1024
BATCH = 1
SEQ = 16384
DEPTH = 2

N_MEM = 256
EPS = 1e-5
NEG_INF = -1e30
H_A = 8
HD_A = 64
D_A = H_A * HD_A
WIN_DIL = ((128, 1), (512, 4), (2048, 16))
BLK = 128
D_B = D_MODEL - D_A
S5_CH = 16
S5_G = D_B // S5_CH
S5_P = 64
D_IN = 3 * D_A + D_B
NUM_BUCKETS = 32
REL_MAX_DIST = 2048
H_X = 4
HD_X = 128
D_X = H_X * HD_X
N_EXPERTS = 32
TOP_K = 4
D_FF = D_MODEL
SWIGLU_ALPHA = 1.702
SWIGLU_LIMIT = 7.0
MOE_BLOCK = 128

kernel_name = "hybrid_dilated_s5_moe_trunk"


def _rmsnorm(x, g):
    xf = x.astype(jnp.float32)
    y = xf * lax.rsqrt(jnp.mean(xf * xf, axis=-1, keepdims=True) + EPS)
    return (y * g.astype(jnp.float32)).astype(x.dtype)


def _t5_bucket(n):
    max_exact = NUM_BUCKETS // 2
    nf = jnp.maximum(n, 1).astype(jnp.float32)
    large = max_exact + (jnp.log(nf / max_exact) / math.log(REL_MAX_DIST / max_exact)
                         * (NUM_BUCKETS - max_exact)).astype(jnp.int32)
    large = jnp.minimum(large, NUM_BUCKETS - 1)
    return jnp.where(n < max_exact, n, large)


def _dilated_branch(q, k, v, rel_bias, window, dil):
    Bsz, S, H, Dh = q.shape
    steps = window // dil
    span = dil * BLK
    s_pad = -(-S // span) * span
    L = s_pad // dil
    nb = L // BLK

    def to_sub(t):
        t = jnp.pad(t, ((0, 0), (0, s_pad - S), (0, 0), (0, 0)))
        t = t.reshape(Bsz, L, dil, H, Dh).transpose(0, 2, 1, 3, 4)
        return t.reshape(Bsz, dil, nb, BLK, H, Dh)

    def with_prev(t):
        prev = jnp.pad(t[:, :, :-1], ((0, 0), (0, 0), (1, 0), (0, 0), (0, 0), (0, 0)))
        return jnp.concatenate([prev, t], axis=3)

    qs = to_sub(q)
    kb = with_prev(to_sub(k))
    vb = with_prev(to_sub(v))

    qi = jnp.arange(BLK, dtype=jnp.int32)[:, None]
    ki = jnp.arange(2 * BLK, dtype=jnp.int32)[None, :]
    dist = BLK + qi - ki
    in_win = (dist >= 0) & (dist <= steps)
    bias = rel_bias[_t5_bucket(jnp.clip(dist, 0, steps) * dil)]
    bias = bias.transpose(2, 0, 1).astype(jnp.float32)
    has_prev = jnp.arange(nb)[:, None, None] > 0
    mask = in_win[None] & (has_prev | (ki >= BLK)[None])

    s = jnp.einsum('bdnqhe,bdnkhe->bdnhqk', qs, kb,
                   preferred_element_type=jnp.float32) * (Dh ** -0.5) + bias
    s = jnp.where(mask[:, None], s, NEG_INF)
    m = jnp.max(s, axis=-1)
    e = jnp.exp(s - m[..., None])
    den = jnp.sum(e, axis=-1)
    num = jnp.einsum('bdnhqk,bdnkhe->bdnqhe', e, vb.astype(jnp.float32))

    def from_sub(t):
        t = t.reshape((Bsz, dil, L) + t.shape[4:])
        t = jnp.swapaxes(t, 1, 2)
        return t.reshape((Bsz, s_pad) + t.shape[3:])[:, :S]

    return from_sub(num), from_sub(jnp.swapaxes(den, 3, 4)), from_sub(jnp.swapaxes(m, 3, 4))


def _dilated_attention(q, k, v, rel_bias):
    nums, dens, maxs = [], [], []
    for window, dil in WIN_DIL:
        n_, d_, m_ = _dilated_branch(q, k, v, rel_bias, window, dil)
        nums.append(n_)
        dens.append(d_)
        maxs.append(m_)
    ms = jnp.stack(maxs)
    w = jnp.exp(ms - jnp.max(ms, axis=0, keepdims=True))
    num = jnp.einsum('pbsh,pbshe->bshe', w, jnp.stack(nums))
    den = jnp.einsum('pbsh,pbsh->bsh', w, jnp.stack(dens))
    return num / den[..., None]


def _linear_recurrence(left, right):
    a_l, b_l = left
    a_r, b_r = right
    return a_r * a_l, a_r * b_l + b_r


def _s5_mixer(u, a_re, a_im, b_re, b_im, c_re, c_im, log_dt, d_skip, w_glu, b_glu):
    f32 = jnp.float32
    Bsz, S, _ = u.shape
    uf = u.astype(f32).reshape(Bsz, S, S5_G, S5_CH)
    lam = lax.complex(a_re.astype(f32), a_im.astype(f32))
    dt = jnp.exp(log_dt.astype(f32))[:, None]
    a_bar = jnp.exp(lam * dt)
    b_bar = ((a_bar - 1.0) / lam)[..., None] * lax.complex(b_re.astype(f32), b_im.astype(f32))
    bu = jnp.einsum('gpc,bsgc->bsgp', b_bar, uf.astype(jnp.complex64))
    a = jnp.broadcast_to(a_bar, bu.shape)
    _, state = lax.associative_scan(_linear_recurrence, (a, bu), axis=1)
    c = lax.complex(c_re.astype(f32), c_im.astype(f32))
    y = jnp.einsum('gcp,bsgp->bsgc', c, state).real + d_skip.astype(f32).reshape(S5_G, S5_CH) * uf
    y = jax.nn.gelu(y.reshape(Bsz, S, D_B), approximate=False)
    gate = jax.nn.sigmoid(y @ w_glu.astype(f32) + b_glu.astype(f32))
    return (y * gate).astype(u.dtype)


def _memory_attention(hn, mem_n, w_q, w_kv, w_o):
    Bsz, S, _ = hn.shape
    n_mem = mem_n.shape[1]
    q = (hn @ w_q).reshape(Bsz, S, H_X, HD_X)
    kv = (mem_n @ w_kv).reshape(Bsz, n_mem, 2, H_X, HD_X)
    k, v = kv[:, :, 0], kv[:, :, 1]
    s = jnp.einsum('bshe,bmhe->bhsm', q, k, preferred_element_type=jnp.float32) * (HD_X ** -0.5)
    p = jax.nn.softmax(s, axis=-1)
    o = jnp.einsum('bhsm,bmhe->bshe', p.astype(v.dtype), v).reshape(Bsz, S, D_X)
    return o @ w_o


def _moe(hn, w_router, b_router, w1, b1, w2, b2):
    Bsz, S, D = hn.shape
    T = Bsz * S
    TK = T * TOP_K
    xt = hn.reshape(T, D)
    logits = jnp.einsum('td,de->te', xt, w_router,
                        preferred_element_type=jnp.float32) + b_router.astype(jnp.float32)
    top_val, top_idx = lax.top_k(logits, TOP_K)
    gates = jax.nn.softmax(top_val, axis=-1)
    flat_e = top_idx.reshape(TK)
    order = jnp.argsort(flat_e)
    e_sorted = flat_e[order]
    tok_sorted = order // TOP_K
    g_sorted = gates.reshape(TK)[order]
    counts = jnp.bincount(flat_e, length=N_EXPERTS)
    padded = (counts + MOE_BLOCK - 1) // MOE_BLOCK * MOE_BLOCK
    start = jnp.cumsum(counts) - counts
    pend = jnp.cumsum(padded)
    pstart = pend - padded
    dest = pstart[e_sorted] + jnp.arange(TK, dtype=jnp.int32) - start[e_sorted]
    n_rows = TK + N_EXPERTS * MOE_BLOCK
    n_blk = n_rows // MOE_BLOCK
    xpad = jnp.zeros((n_rows, D), hn.dtype).at[dest].set(xt[tok_sorted])
    blk_e = jnp.minimum(jnp.searchsorted(pend, jnp.arange(n_blk) * MOE_BLOCK, side='right'),
                        N_EXPERTS - 1)

    def expert_block(args):
        xb, e = args
        hb = xb @ w1[e] + b1[e]
        x_glu = jnp.minimum(hb[:, :D_FF], SWIGLU_LIMIT)
        x_lin = jnp.clip(hb[:, D_FF:], -SWIGLU_LIMIT, SWIGLU_LIMIT)
        act = x_glu * jax.nn.sigmoid(SWIGLU_ALPHA * x_glu) * (x_lin + 1.0)
        return act @ w2[e] + b2[e]

    out = lax.map(expert_block, (xpad.reshape(n_blk, MOE_BLOCK, D), blk_e)).reshape(n_rows, D)
    y = jax.ops.segment_sum(out[dest] * g_sorted[:, None].astype(out.dtype), tok_sorted,
                            num_segments=T)
    return y.reshape(Bsz, S, D)


def setup_inputs(seed: int = 0) -> dict:
    key = jax.random.key(seed)
    ks = iter(jax.random.split(key, 40))

    def nrm(shape, scale):
        return scale * jax.random.normal(next(ks), shape, jnp.float32)

    def gain(shape):
        return 1.0 + nrm(shape, 0.01)

    L = DEPTH
    n_idx = jnp.arange(S5_P, dtype=jnp.float32)
    return {
        "x": nrm((BATCH, SEQ, D_MODEL), 1.0),
        "mem": nrm((BATCH, N_MEM, D_MODEL), 1.0),
        "rel_bias": nrm((NUM_BUCKETS, H_A), 0.2),
        "norm_mix": gain((L, D_MODEL)),
        "w_in": nrm((L, D_MODEL, D_IN), D_MODEL ** -0.5),
        "s5_a_re": -0.5 + nrm((L, S5_G, S5_P), 0.01),
        "s5_a_im": math.pi * n_idx + nrm((L, S5_G, S5_P), 0.01),
        "s5_b_re": nrm((L, S5_G, S5_P, S5_CH), (2 * S5_CH) ** -0.5),
        "s5_b_im": nrm((L, S5_G, S5_P, S5_CH), (2 * S5_CH) ** -0.5),
        "s5_c_re": nrm((L, S5_G, S5_CH, S5_P), (2 * S5_P) ** -0.5),
        "s5_c_im": nrm((L, S5_G, S5_CH, S5_P), (2 * S5_P) ** -0.5),
        "s5_log_dt": jax.random.uniform(next(ks), (L, S5_G), jnp.float32,
                                        math.log(1e-3), math.log(1e-1)),
        "s5_d": nrm((L, D_B), 1.0),
        "w_glu": nrm((L, D_B, D_B), D_B ** -0.5),
        "b_glu": nrm((L, D_B), 0.01),
        "g_out_attn": gain((L, D_A)),
        "g_out_ssm": gain((L, D_B)),
        "w_out": nrm((L, D_MODEL, D_MODEL), D_MODEL ** -0.5),
        "norm_xattn": gain((L, D_MODEL)),
        "norm_mem": gain((L, D_MODEL)),
        "w_xq": nrm((L, D_MODEL, D_X), D_MODEL ** -0.5),
        "w_xkv": nrm((L, D_MODEL, 2 * D_X), D_MODEL ** -0.5),
        "w_xo": nrm((L, D_X, D_MODEL), D_X ** -0.5),
        "norm_moe": gain((L, D_MODEL)),
        "w_router": nrm((L, D_MODEL, N_EXPERTS), D_MODEL ** -0.5),
        "b_router": nrm((L, N_EXPERTS), 0.01),
        "w1": nrm((L, N_EXPERTS, D_MODEL, 2 * D_FF), D_MODEL ** -0.5),
        "b1": nrm((L, N_EXPERTS, 2 * D_FF), 0.01),
        "w2": nrm((L, N_EXPERTS, D_FF, D_MODEL), D_FF ** -0.5),
        "b2": nrm((L, N_EXPERTS, D_MODEL), 0.01),
        "norm_final": gain((D_MODEL,)),
    }


def reference(x, mem, rel_bias, norm_mix, w_in, s5_a_re, s5_a_im, s5_b_re, s5_b_im, s5_c_re, s5_c_im,
              s5_log_dt, s5_d, w_glu, b_glu, g_out_attn, g_out_ssm, w_out, norm_xattn, norm_mem,
              w_xq, w_xkv, w_xo, norm_moe, w_router, b_router, w1, b1, w2, b2, norm_final):
    Bsz, S, _ = x.shape
    h = x
    for l in range(DEPTH):
        xn = _rmsnorm(h, norm_mix[l])
        proj = xn @ w_in[l]
        q = proj[..., 0:D_A].reshape(Bsz, S, H_A, HD_A)
        k = proj[..., D_A:2 * D_A].reshape(Bsz, S, H_A, HD_A)
        v = proj[..., 2 * D_A:3 * D_A].reshape(Bsz, S, H_A, HD_A)
        u = proj[..., 3 * D_A:]
        y_a = _dilated_attention(q, k, v, rel_bias).reshape(Bsz, S, D_A).astype(h.dtype)
        y_b = _s5_mixer(u, s5_a_re[l], s5_a_im[l], s5_b_re[l], s5_b_im[l], s5_c_re[l], s5_c_im[l],
                        s5_log_dt[l], s5_d[l], w_glu[l], b_glu[l])
        mix = jnp.concatenate([_rmsnorm(y_a, g_out_attn[l]), _rmsnorm(y_b, g_out_ssm[l])], axis=-1)
        h = h + mix @ w_out[l]
        h = h + _memory_attention(_rmsnorm(h, norm_xattn[l]), _rmsnorm(mem, norm_mem[l]),
                                  w_xq[l], w_xkv[l], w_xo[l])
        h = h + _moe(_rmsnorm(h, norm_moe[l]), w_router[l], b_router[l], w1[l], b1[l], w2[l], b2[l])
    return _rmsnorm(h, norm_final)
```

```python
import functools
import math

import jax
import jax.numpy as jnp
from jax import lax
from jax.experimental import pallas as pl
from jax.experimental.pallas import tpu as pltpu

F32 = jnp.float32
BF16 = jnp.bfloat16

D_MODEL = 1024
DEPTH = 2
EPS = 1e-5
NEG_INF = -1e30
H_A = 8
HD_A = 64
D_A = H_A * HD_A
WIN_DIL = ((128, 1), (512, 4), (2048, 16))
BLK = 128
D_B = D_MODEL - D_A
S5_CH = 16
S5_G = D_B // S5_CH
S5_P = 64
D_IN = 3 * D_A + D_B
NUM_BUCKETS = 32
REL_MAX_DIST = 2048
H_X = 4
HD_X = 128
D_X = H_X * HD_X
N_EXPERTS = 32
TOP_K = 4
D_FF = D_MODEL
SWIGLU_ALPHA = 1.702
SWIGLU_LIMIT = 7.0

LANES = 128
S5_CHUNK = 16
S5_PAIRS = S5_G // 2
VMEM_LIMIT = 48 * 1024 * 1024

ROW_TILE = 512
ATTN_QB = 4
S5_ROWS = 128
MOE_TILE = 256


def _params(sem):
    return pltpu.CompilerParams(dimension_semantics=sem, vmem_limit_bytes=VMEM_LIMIT)


def _rms(x, g):
    return x * lax.rsqrt(jnp.mean(x * x, axis=-1, keepdims=True) + EPS) * g


def _dot(a, b):
    return jnp.dot(a, b, preferred_element_type=F32)


def _dot_nt(a, b):
    return lax.dot_general(a, b, (((1,), (1,)), ((), ())), preferred_element_type=F32)


def _inproj_body(h_ref, g_ref, w_ref, o_ref):
    xn = _rms(h_ref[...], g_ref[...]).astype(BF16)
    o_ref[...] = _dot(xn, w_ref[...]).astype(BF16)


def _inproj(h, g, w):
    s, d = h.shape
    n = w.shape[1]
    tm = min(ROW_TILE, s)
    return pl.pallas_call(
        _inproj_body,
        out_shape=jax.ShapeDtypeStruct((s, n), BF16),
        grid=(s // tm,),
        in_specs=[pl.BlockSpec((tm, d), lambda i: (i, 0)),
                  pl.BlockSpec((1, d), lambda i: (0, 0)),
                  pl.BlockSpec((d, n), lambda i: (0, 0))],
        out_specs=pl.BlockSpec((tm, n), lambda i: (i, 0)),
        compiler_params=_params(("parallel",)),
        name="inproj",
    )(h, g.reshape(1, d), w)


def _t5_bucket(n):
    max_exact = NUM_BUCKETS // 2
    nf = jnp.maximum(n, 1).astype(F32)
    large = max_exact + (jnp.log(nf / max_exact) / math.log(REL_MAX_DIST / max_exact)
                         * (NUM_BUCKETS - max_exact)).astype(jnp.int32)
    large = jnp.minimum(large, NUM_BUCKETS - 1)
    return jnp.where(n < max_exact, n, large)


def _attn_bias(rel_bias, window, dil):
    steps = window // dil
    qi = jnp.arange(BLK, dtype=jnp.int32)[:, None]
    ki = jnp.arange(2 * BLK, dtype=jnp.int32)[None, :]
    dist = BLK + qi - ki
    in_win = (dist >= 0) & (dist <= steps)
    bias = rel_bias[_t5_bucket(jnp.clip(dist, 0, steps) * dil)].astype(F32)
    bias = jnp.where(in_win[:, :, None], bias, NEG_INF).transpose(2, 0, 1)
    return bias.reshape(H_A // 2, 2 * BLK, 2 * BLK)


def _attn_body(*refs, qb, first, last):
    if first:
        q_ref, k_ref, kp_ref, v_ref, vp_ref, bias_ref, acc_out, st_out = refs
        acc_in = st_in = None
    elif last:
        q_ref, k_ref, kp_ref, v_ref, vp_ref, bias_ref, acc_in, st_in, acc_out = refs
        st_out = None
    else:
        q_ref, k_ref, kp_ref, v_ref, vp_ref, bias_ref, acc_in, st_in, acc_out, st_out = refs
    n = pl.program_id(1)
    lane = lax.broadcasted_iota(jnp.int32, (1, LANES), 1)
    lo = lane < HD_A
    mlo = lo.astype(BF16)
    mhi = (~lo).astype(BF16)
    col = lax.broadcasted_iota(jnp.int32, (2 * BLK, 2 * BLK), 1)
    st_lane = lax.broadcasted_iota(jnp.int32, (BLK, LANES), 1)
    for i in range(qb):
        rows = slice(i * BLK, (i + 1) * BLK)
        st = jnp.zeros((BLK, LANES), F32)
        for hp in range(H_A // 2):
            lanes = slice(hp * LANES, (hp + 1) * LANES)
            q2 = q_ref[rows, lanes]
            if i == 0:
                kk = jnp.concatenate([kp_ref[:, lanes], k_ref[0:BLK, lanes]], axis=0)
                vv = jnp.concatenate([vp_ref[:, lanes], v_ref[0:BLK, lanes]], axis=0)
            else:
                kk = k_ref[(i - 1) * BLK:(i + 1) * BLK, lanes]
                vv = v_ref[(i - 1) * BLK:(i + 1) * BLK, lanes]
            qs = jnp.concatenate([q2 * mlo, q2 * mhi], axis=0)
            s = _dot_nt(qs, kk) + bias_ref[hp]
            if i == 0:
                s = jnp.where((n > 0) | (col >= BLK), s, NEG_INF)
            m = jnp.max(s, axis=-1, keepdims=True)
            e = jnp.exp(s - m)
            l = jnp.sum(e, axis=-1, keepdims=True)
            o = _dot(e.astype(BF16), vv)
            if not first:
                mp = jnp.concatenate([st_in[rows, 2 * hp:2 * hp + 1],
                                      st_in[rows, 2 * hp + 1:2 * hp + 2]], axis=0)
                lp = jnp.concatenate([st_in[rows, H_A + 2 * hp:H_A + 2 * hp + 1],
                                      st_in[rows, H_A + 2 * hp + 1:H_A + 2 * hp + 2]], axis=0)
                mn = jnp.maximum(mp, m)
                a = jnp.exp(mp - mn)
                b = jnp.exp(m - mn)
                l = a * lp + b * l
                m = mn
                prev = acc_in[rows, lanes]
                o = o * b
                o2 = jnp.where(lo, o[:BLK] + a[:BLK] * prev, o[BLK:] + a[BLK:] * prev)
            else:
                o2 = jnp.where(lo, o[:BLK], o[BLK:])
            if last:
                o2 = o2 / jnp.where(lo, l[:BLK], l[BLK:])
            else:
                for half in range(2):
                    hs = slice(half * BLK, (half + 1) * BLK)
                    st = jnp.where(st_lane == 2 * hp + half, m[hs], st)
                    st = jnp.where(st_lane == H_A + 2 * hp + half, l[hs], st)
            acc_out[rows, lanes] = o2
        if not last:
            st_out[rows, :] = st


def _attn_branch(proj, bias2, carry, *, dil, first, last):
    s = proj.shape[0]
    sl = s // dil
    qb = min(ATTN_QB, sl // BLK)
    tq = qb * BLK
    pv = proj.reshape(sl, dil * D_IN)
    ncol = D_IN // D_A

    def cur(which):
        return pl.BlockSpec((tq, D_A), lambda r, n: (n, r * ncol + which))

    def prev(which):
        return pl.BlockSpec((BLK, D_A), lambda r, n: (jnp.maximum(n * qb - 1, 0), r * ncol + which))

    in_specs = [cur(0), cur(1), prev(1), cur(2), prev(2),
                pl.BlockSpec((H_A // 2, 2 * BLK, 2 * BLK), lambda r, n: (0, 0, 0))]
    args = [pv, pv, pv, pv, pv, bias2]
    acc_spec = pl.BlockSpec((tq, D_A), lambda r, n: (n, r))
    st_spec = pl.BlockSpec((tq, LANES), lambda r, n: (n, r))
    if not first:
        acc, st = carry
        in_specs += [acc_spec, st_spec]
        args += [acc.reshape(sl, dil * D_A), st.reshape(sl, dil * LANES)]
    out_shape = [jax.ShapeDtypeStruct((sl, dil * D_A), F32)]
    out_specs = [acc_spec]
    if not last:
        out_shape.append(jax.ShapeDtypeStruct((sl, dil * LANES), F32))
        out_specs.append(st_spec)
    outs = pl.pallas_call(
        functools.partial(_attn_body, qb=qb, first=first, last=last),
        out_shape=out_shape,
        grid=(dil, sl // tq),
        in_specs=in_specs,
        out_specs=out_specs,
        compiler_params=_params(("parallel", "arbitrary")),
        name=f"attn_d{dil}",
    )(*args)
    if last:
        return outs[0].reshape(s, D_A)
    return outs[0].reshape(s, D_A), outs[1].reshape(s, LANES)


def _dilated_attention(proj, rel_bias):
    carry = None
    for idx, (window, dil) in enumerate(WIN_DIL):
        carry = _attn_branch(proj, _attn_bias(rel_bias, window, dil), carry, dil=dil,
                             first=idx == 0, last=idx == len(WIN_DIL) - 1)
    return carry


def _s5_operators(a_re, a_im, b_re, b_im, c_re, c_im, log_dt, d_skip):
    L = S5_CHUNK
    lam = lax.complex(a_re.astype(F32), a_im.astype(F32))
    dt = jnp.exp(log_dt.astype(F32))[:, None]
    a_bar = jnp.exp(lam * dt)
    b_bar = ((a_bar - 1.0) / lam)[..., None] * lax.complex(b_re.astype(F32), b_im.astype(F32))
    c = lax.complex(c_re.astype(F32), c_im.astype(F32))
    j = jnp.arange(L + 1, dtype=F32)
    log_a = lam * dt
    apow = jnp.exp(log_a[None] * j[:, None, None])
    kj = jnp.einsum('gdp,jgp,gpc->jgdc', c, apow[:L], b_bar).real
    s_idx = jnp.arange(L)[:, None]
    t_idx = jnp.arange(L)[None, :]
    lag = t_idx - s_idx
    m = jnp.where((lag >= 0)[:, :, None, None, None], kj[jnp.clip(lag, 0, L - 1)], 0.0)
    m = m.transpose(2, 0, 4, 1, 3)
    eye = jnp.eye(L)[:, None, :, None] * jnp.eye(S5_CH)[None, :, None, :]
    m = m + eye[None] * d_skip.astype(F32).reshape(S5_G, 1, S5_CH, 1, 1)
    m = m.reshape(S5_G, L * S5_CH, L * S5_CH)
    p = jnp.einsum('sgp,gpc->gscp', apow[:L][::-1], b_bar).reshape(S5_G, L * S5_CH, S5_P)
    ca = jnp.einsum('gdp,tgp->gptd', c, apow[1:L + 1]).reshape(S5_G, S5_P, L * S5_CH)
    a_l = apow[L]

    def pair_cols(x):
        g, r, w = x.shape
        x = x.reshape(S5_PAIRS, 2, r, w)
        z = jnp.zeros_like(x[:, 0])
        top = jnp.concatenate([x[:, 0], z], axis=-1)
        bot = jnp.concatenate([z, x[:, 1]], axis=-1)
        return jnp.concatenate([top, bot], axis=1)

    p2 = jnp.concatenate([pair_cols(p.real), pair_cols(p.imag)], axis=-1)

    def pair_rows(x):
        g, r, w = x.shape
        x = x.reshape(S5_PAIRS, 2, r, w)
        z = jnp.zeros_like(x[:, 0])
        top = jnp.concatenate([x[:, 0], z], axis=-1)
        bot = jnp.concatenate([z, x[:, 1]], axis=-1)
        return jnp.concatenate([top, bot], axis=1)

    q2 = jnp.concatenate([pair_rows(ca.real), pair_rows(-ca.imag)], axis=1)
    a_lr = a_l.real.reshape(1, S5_G * S5_P)
    a_li = a_l.imag.reshape(1, S5_G * S5_P)
    return m.astype(BF16), p2.astype(BF16), q2.astype(BF16), a_lr, a_li


def _s5_body(w_ref, m_ref, p_ref, q_ref, ar_ref, ai_ref, o_ref, ere, eim, xre, xim, sre, sim):
    rows = w_ref.shape[0]
    gw = S5_CHUNK * S5_CH
    pw = 2 * gw

    @pl.when(pl.program_id(0) == 0)
    def _():
        sre[...] = jnp.zeros_like(sre)
        sim[...] = jnp.zeros_like(sim)

    for pr in range(S5_PAIRS):
        e = _dot(w_ref[:, pr * pw:(pr + 1) * pw], p_ref[pr])
        ere[:, pr * LANES:(pr + 1) * LANES] = e[:, :LANES]
        eim[:, pr * LANES:(pr + 1) * LANES] = e[:, LANES:]

    ar = ar_ref[...]
    ai = ai_ref[...]

    def step(n, carry):
        xr, xi = carry
        xre[pl.ds(n, 1), :] = xr
        xim[pl.ds(n, 1), :] = xi
        nr = ar * xr - ai * xi + ere[pl.ds(n, 1), :]
        ni = ar * xi + ai * xr + eim[pl.ds(n, 1), :]
        return nr, ni

    xr, xi = lax.fori_loop(0, rows, step, (sre[...], sim[...]))
    sre[...] = xr
    sim[...] = xi

    for pr in range(S5_PAIRS):
        xin = jnp.concatenate([xre[:, pr * LANES:(pr + 1) * LANES],
                               xim[:, pr * LANES:(pr + 1) * LANES]], axis=-1).astype(BF16)
        yc = _dot(xin, q_ref[pr])
        for half in range(2):
            g = 2 * pr + half
            cols = slice(g * gw, (g + 1) * gw)
            y = _dot(w_ref[:, cols], m_ref[g]) + yc[:, half * gw:(half + 1) * gw]
            o_ref[:, cols] = (0.5 * y * (1.0 + lax.erf(y * (2.0 ** -0.5)))).astype(BF16)


def _s5_core(u, ops):
    m, p2, q2, a_lr, a_li = ops
    s = u.shape[0]
    nc = s // S5_CHUNK
    wide = S5_G * S5_CHUNK * S5_CH
    w = u.reshape(nc, S5_CHUNK, S5_G, S5_CH).transpose(0, 2, 1, 3).reshape(nc, wide)
    rows = min(S5_ROWS, nc)
    nstate = S5_G * S5_P
    full = lambda shape: pl.BlockSpec(shape, lambda i: (0,) * len(shape))
    y = pl.pallas_call(
        _s5_body,
        out_shape=jax.ShapeDtypeStruct((nc, wide), BF16),
        grid=(nc // rows,),
        in_specs=[pl.BlockSpec((rows, wide), lambda i: (i, 0)),
                  full(m.shape), full(p2.shape), full(q2.shape), full(a_lr.shape), full(a_li.shape)],
        out_specs=pl.BlockSpec((rows, wide), lambda i: (i, 0)),
        scratch_shapes=[pltpu.VMEM((rows, nstate), F32)] * 4 + [pltpu.VMEM((1, nstate), F32)] * 2,
        compiler_params=_params(("arbitrary",)),
        name="s5",
    )(w, m, p2, q2, a_lr, a_li)
    return y.reshape(nc, S5_G, S5_CHUNK, S5_CH).transpose(0, 2, 1, 3).reshape(s, D_B)


def _split_bf16(x):
    hi = x.astype(BF16)
    lo = (x - hi.astype(F32)).astype(BF16)
    return hi, lo


def _mid_body(h_ref, ya_ref, yb_ref, wglu_ref, bglu_ref, ga_ref, gb_ref, wout_ref, gx_ref, wq_ref,
              k_ref, v_ref, wo_ref, gm_ref, wr_ref, br_ref, h_out, xn_out, idx_out, gate_out):
    yb = yb_ref[...]
    gate = jax.nn.sigmoid(_dot(yb, wglu_ref[...]) + bglu_ref[...])
    yb2 = yb.astype(F32) * gate
    na = _rms(ya_ref[...], ga_ref[...]).astype(BF16)
    nb = _rms(yb2, gb_ref[...]).astype(BF16)
    h1 = h_ref[...] + _dot(na, wout_ref[0:D_A, :]) + _dot(nb, wout_ref[D_A:D_MODEL, :])
    q = _dot(_rms(h1, gx_ref[...]).astype(BF16), wq_ref[...]).astype(BF16)
    heads = []
    for hd in range(H_X):
        lanes = slice(hd * HD_X, (hd + 1) * HD_X)
        s = _dot_nt(q[:, lanes], k_ref[:, lanes])
        e = jnp.exp(s - jnp.max(s, axis=-1, keepdims=True))
        p = e / jnp.sum(e, axis=-1, keepdims=True)
        heads.append(_dot(p.astype(BF16), v_ref[:, lanes]))
    o = jnp.concatenate(heads, axis=-1).astype(BF16)
    h2 = h1 + _dot(o, wo_ref[...])
    h_out[...] = h2
    xn = _rms(h2, gm_ref[...])
    xn_out[...] = xn.astype(BF16)
    x_hi, x_lo = _split_bf16(xn)
    logits = _dot(x_hi, wr_ref[0]) + _dot(x_lo, wr_ref[0]) + _dot(x_hi, wr_ref[1]) + br_ref[...]
    lane = lax.broadcasted_iota(jnp.int32, logits.shape, 1)
    logits = jnp.where(lane < N_EXPERTS, logits, -jnp.inf)
    vals, idxs = [], []
    for _ in range(TOP_K):
        mx = jnp.max(logits, axis=-1, keepdims=True)
        ix = jnp.min(jnp.where(logits == mx, lane, LANES), axis=-1, keepdims=True)
        vals.append(mx)
        idxs.append(ix)
        logits = jnp.where(lane == ix, -jnp.inf, logits)
    es = [jnp.exp(v - vals[0]) for v in vals]
    den = es[0] + es[1] + es[2] + es[3]
    idx_t = jnp.zeros(lane.shape, jnp.int32)
    gate_t = jnp.zeros(lane.shape, F32)
    for k in range(TOP_K):
        idx_t = jnp.where(lane == k, idxs[k], idx_t)
        gate_t = jnp.where(lane == k, es[k] / den, gate_t)
    idx_out[...] = idx_t
    gate_out[...] = gate_t


def _mid(h, ya, yb, wglu, bglu, ga, gb, wout, gx, wq, kmem, vmem, wo, gm, wr2, br):
    s = h.shape[0]
    tm = min(ROW_TILE // 2, s)
    row = lambda w: pl.BlockSpec((tm, w), lambda i: (i, 0))
    full = lambda a: pl.BlockSpec(a.shape, lambda i: (0,) * a.ndim)
    consts = [wglu, bglu, ga, gb, wout, gx, wq, kmem, vmem, wo, gm, wr2, br]
    return pl.pallas_call(
        _mid_body,
        out_shape=[jax.ShapeDtypeStruct((s, D_MODEL), F32), jax.ShapeDtypeStruct((s, D_MODEL), BF16),
                   jax.ShapeDtypeStruct((s, LANES), jnp.int32), jax.ShapeDtypeStruct((s, LANES), F32)],
        grid=(s // tm,),
        in_specs=[row(D_MODEL), row(D_A), row(D_B)] + [full(a) for a in consts],
        out_specs=[row(D_MODEL), row(D_MODEL), row(LANES), row(LANES)],
        compiler_params=_params(("parallel",)),
        name="mid",
    )(h, ya, yb, *consts)


def _memkv_body(mem_ref, g_ref, w_ref, o_ref):
    o_ref[...] = _dot(_rms(mem_ref[...], g_ref[...]).astype(BF16), w_ref[...]).astype(BF16)


def _memkv(mem, g, w):
    n, d = mem.shape
    return pl.pallas_call(
        _memkv_body,
        out_shape=jax.ShapeDtypeStruct((n, w.shape[1]), BF16),
        compiler_params=pltpu.CompilerParams(vmem_limit_bytes=VMEM_LIMIT),
        name="memkv",
    )(mem, g.reshape(1, d), w)


def _moe_body(te_ref, tv_ref, x_ref, g_ref, w1_ref, b1_ref, w2_ref, b2_ref, o_ref, w1b, w2b):
    i = pl.program_id(0)
    new_expert = (i == 0) | (te_ref[i] != te_ref[jnp.maximum(i - 1, 0)])

    @pl.when(new_expert)
    def _():
        w1b[...] = w1_ref[0].astype(BF16)
        w2b[...] = w2_ref[0].astype(BF16)

    @pl.when(tv_ref[i] > 0)
    def _():
        hb = _dot(x_ref[...], w1b[...]) + b1_ref[0]
        x_glu = jnp.minimum(hb[:, :D_FF], SWIGLU_LIMIT)
        x_lin = jnp.clip(hb[:, D_FF:], -SWIGLU_LIMIT, SWIGLU_LIMIT)
        act = x_glu * jax.nn.sigmoid(SWIGLU_ALPHA * x_glu) * (x_lin + 1.0)
        y = _dot(act.astype(BF16), w2b[...]) + b2_ref[0]
        o_ref[...] = (y * g_ref[...]).astype(BF16)

    @pl.when(tv_ref[i] == 0)
    def _():
        o_ref[...] = jnp.zeros_like(o_ref)


def _moe_experts(xs, gate_rows, tile_e, tile_valid, w1, b1, w2, b2):
    n_rows, d = xs.shape
    tm = MOE_TILE
    n_tiles = n_rows // tm
    ne, _, ff2 = w1.shape
    grid_spec = pltpu.PrefetchScalarGridSpec(
        num_scalar_prefetch=2,
        grid=(n_tiles,),
        in_specs=[pl.BlockSpec((tm, d), lambda i, te, tv: (i, 0)),
                  pl.BlockSpec((tm, 1), lambda i, te, tv: (i, 0)),
                  pl.BlockSpec((1, d, ff2), lambda i, te, tv: (te[i], 0, 0)),
                  pl.BlockSpec((1, 1, ff2), lambda i, te, tv: (te[i], 0, 0)),
                  pl.BlockSpec((1, ff2 // 2, d), lambda i, te, tv: (te[i], 0, 0)),
                  pl.BlockSpec((1, 1, d), lambda i, te, tv: (te[i], 0, 0))],
        out_specs=pl.BlockSpec((tm, d), lambda i, te, tv: (i, 0)),
        scratch_shapes=[pltpu.VMEM((d, ff2), BF16), pltpu.VMEM((ff2 // 2, d), BF16)],
    )
    return pl.pallas_call(
        _moe_body,
        out_shape=jax.ShapeDtypeStruct((n_rows, d), BF16),
        grid_spec=grid_spec,
        compiler_params=_params(("arbitrary",)),
        name="moe",
    )(tile_e, tile_valid, xs, gate_rows, w1, b1.reshape(ne, 1, ff2), w2, b2.reshape(ne, 1, d))


def _route(top_idx, gates, tm):
    t = top_idx.shape[0]
    tk = t * TOP_K
    flat_e = top_idx.reshape(tk)
    order = jnp.argsort(flat_e).astype(jnp.int32)
    counts = jnp.bincount(flat_e, length=N_EXPERTS).astype(jnp.int32)
    padded = (counts + tm - 1) // tm * tm
    start = jnp.cumsum(counts) - counts
    pend = jnp.cumsum(padded)
    pstart = pend - padded
    n_rows = tk + N_EXPERTS * tm
    n_tiles = n_rows // tm
    tile_first = jnp.arange(n_tiles, dtype=jnp.int32) * tm
    tile_e = jnp.minimum(jnp.searchsorted(pend, tile_first, side='right'), N_EXPERTS - 1).astype(jnp.int32)
    tile_valid = (tile_first < pend[-1]).astype(jnp.int32)
    r = jnp.arange(n_rows, dtype=jnp.int32)
    e_r = jnp.repeat(tile_e, tm)
    j = r - pstart[e_r]
    valid = (j < counts[e_r]) & (j >= 0)
    src = order[jnp.clip(start[e_r] + j, 0, tk - 1)]
    tok_r = jnp.where(valid, src // TOP_K, 0)
    gate_r = jnp.where(valid, gates.reshape(tk)[src], 0.0)
    rank = jnp.zeros((tk,), jnp.int32).at[order].set(jnp.arange(tk, dtype=jnp.int32))
    dest = pstart[flat_e] + rank - start[flat_e]
    return tok_r, gate_r.reshape(n_rows, 1), tile_e, tile_valid, dest.reshape(t, TOP_K)


def _combine_body(h_ref, y_ref, g_ref, o_ref, *, final):
    y = y_ref[...].astype(F32)
    h = h_ref[...] + (y[:, 0] + y[:, 1]) + (y[:, 2] + y[:, 3])
    o_ref[...] = _rms(h, g_ref[...]) if final else h


def _combine(h, y4, g, *, final):
    s, d = h.shape
    tm = min(ROW_TILE // 2, s)
    return pl.pallas_call(
        functools.partial(_combine_body, final=final),
        out_shape=jax.ShapeDtypeStruct((s, d), F32),
        grid=(s // tm,),
        in_specs=[pl.BlockSpec((tm, d), lambda i: (i, 0)),
                  pl.BlockSpec((tm, TOP_K, d), lambda i: (i, 0, 0)),
                  pl.BlockSpec((1, d), lambda i: (0, 0))],
        out_specs=pl.BlockSpec((tm, d), lambda i: (i, 0)),
        compiler_params=_params(("parallel",)),
        name="combine",
    )(h, y4, g.reshape(1, d))


def _layer(h, mem, rel_bias, p, l, g_final):
    row = lambda a: a.reshape(1, -1).astype(F32)
    q_scale = jnp.concatenate([jnp.full((D_A,), HD_A ** -0.5, F32), jnp.ones((D_IN - D_A,), F32)])
    w_in = (p["w_in"][l] * q_scale).astype(BF16)
    proj = _inproj(h, p["norm_mix"][l], w_in)
    ya = _dilated_attention(proj, rel_bias)
    ops = _s5_operators(p["s5_a_re"][l], p["s5_a_im"][l], p["s5_b_re"][l], p["s5_b_im"][l],
                        p["s5_c_re"][l], p["s5_c_im"][l], p["s5_log_dt"][l], p["s5_d"][l])
    yb = _s5_core(proj[:, 3 * D_A:], ops)
    kv = _memkv(mem, p["norm_mem"][l], p["w_xkv"][l].astype(BF16))
    wr = jnp.pad(p["w_router"][l].astype(F32), ((0, 0), (0, LANES - N_EXPERTS)))
    wr_hi = wr.astype(BF16)
    wr2 = jnp.stack([wr_hi, (wr - wr_hi.astype(F32)).astype(BF16)])
    br = jnp.pad(p["b_router"][l].astype(F32), (0, LANES - N_EXPERTS)).reshape(1, LANES)
    h2, xn, idx, gates = _mid(
        h, ya, yb, p["w_glu"][l].astype(BF16), row(p["b_glu"][l]), row(p["g_out_attn"][l]),
        row(p["g_out_ssm"][l]), p["w_out"][l].astype(BF16), row(p["norm_xattn"][l]),
        (p["w_xq"][l] * (HD_X ** -0.5)).astype(BF16), kv[:, :D_X], kv[:, D_X:],
        p["w_xo"][l].astype(BF16), row(p["norm_moe"][l]), wr2, br)
    tok_r, gate_r, tile_e, tile_valid, dest = _route(idx[:, :TOP_K], gates[:, :TOP_K], MOE_TILE)
    out = _moe_experts(xn[tok_r], gate_r, tile_e, tile_valid, p["w1"][l], p["b1"][l], p["w2"][l], p["b2"][l])
    final = l == DEPTH - 1
    return _combine(h2, out[dest], g_final, final=final)


def kernel(x, mem, rel_bias, norm_mix, w_in, s5_a_re, s5_a_im, s5_b_re, s5_b_im, s5_c_re, s5_c_im, s5_log_dt, s5_d, w_glu, b_glu, g_out_attn, g_out_ssm, w_out, norm_xattn, norm_mem, w_xq, w_xkv, w_xo, norm_moe, w_router, b_router, w1, b1, w2, b2, norm_final):
    p = dict(norm_mix=norm_mix, w_in=w_in, s5_a_re=s5_a_re, s5_a_im=s5_a_im, s5_b_re=s5_b_re,
             s5_b_im=s5_b_im, s5_c_re=s5_c_re, s5_c_im=s5_c_im, s5_log_dt=s5_log_dt, s5_d=s5_d,
             w_glu=w_glu, b_glu=b_glu, g_out_attn=g_out_attn, g_out_ssm=g_out_ssm, w_out=w_out,
             norm_xattn=norm_xattn, norm_mem=norm_mem, w_xq=w_xq, w_xkv=w_xkv, w_xo=w_xo,
             norm_moe=norm_moe, w_router=w_router, b_router=b_router, w1=w1, b1=b1, w2=w2, b2=b2)
    bsz, s, d = x.shape
    outs = []
    for b in range(bsz):
        h = x[b]
        for l in range(DEPTH):
            h = _layer(h, mem[b], rel_bias, p, l, norm_final)
        outs.append(h)
    return jnp.stack(outs)
```

```python
import functools
import math

import jax
import jax.numpy as jnp
from jax import lax
from jax.experimental import pallas as pl
from jax.experimental.pallas import tpu as pltpu

F32 = jnp.float32
BF16 = jnp.bfloat16

D_MODEL = 1024
DEPTH = 2
EPS = 1e-5
NEG_INF = -1e30
H_A = 8
HD_A = 64
D_A = H_A * HD_A
WIN_DIL = ((128, 1), (512, 4), (2048, 16))
BLK = 128
D_B = D_MODEL - D_A
S5_CH = 16
S5_G = D_B // S5_CH
S5_P = 64
D_IN = 3 * D_A + D_B
NUM_BUCKETS = 32
REL_MAX_DIST = 2048
H_X = 4
HD_X = 128
D_X = H_X * HD_X
N_EXPERTS = 32
TOP_K = 4
D_FF = D_MODEL
SWIGLU_ALPHA = 1.702
SWIGLU_LIMIT = 7.0

LANES = 128
NRES = WIN_DIL[-1][1]
SPAN = NRES * BLK
S5_CHUNK = NRES
S5_PAIRS = S5_G // 2
VMEM_LIMIT = 56 * 1024 * 1024

ROW_TILE = 512
S5_ROWS = 128
MOE_TILE = 256


def _params(sem):
    return pltpu.CompilerParams(dimension_semantics=sem, vmem_limit_bytes=VMEM_LIMIT)


def _rms(x, g):
    return x * lax.rsqrt(jnp.mean(x * x, axis=-1, keepdims=True) + EPS) * g


def _dot(a, b):
    return jnp.dot(a, b, preferred_element_type=F32)


def _dot_nt(a, b):
    return lax.dot_general(a, b, (((1,), (1,)), ((), ())), preferred_element_type=F32)


def _full(a):
    return pl.BlockSpec(a.shape, lambda *_: (0,) * a.ndim)


def _to_span_layout(x):
    s = x.shape[0]
    return x.reshape(s // SPAN, BLK, NRES, -1).transpose(0, 2, 1, 3).reshape(s, -1)


def _from_span_layout(x):
    s = x.shape[0]
    return x.reshape(s // SPAN, NRES, BLK, -1).transpose(0, 2, 1, 3).reshape(s, -1)


def _inproj_body(h_ref, g_ref, w_ref, o_ref):
    xn = _rms(h_ref[...], g_ref[...]).astype(BF16)
    o_ref[...] = _dot(xn, w_ref[...]).astype(BF16)


def _inproj(h, g, w):
    s, d = h.shape
    n = w.shape[1]
    tm = min(ROW_TILE, s)
    return pl.pallas_call(
        _inproj_body,
        out_shape=jax.ShapeDtypeStruct((s, n), BF16),
        grid=(s // tm,),
        in_specs=[pl.BlockSpec((tm, d), lambda i: (i, 0)),
                  pl.BlockSpec((1, d), lambda i: (0, 0)),
                  pl.BlockSpec((d, n), lambda i: (0, 0))],
        out_specs=pl.BlockSpec((tm, n), lambda i: (i, 0)),
        compiler_params=_params(("parallel",)),
        name="inproj",
    )(h, g.reshape(1, d), w)


def _t5_bucket(n):
    max_exact = NUM_BUCKETS // 2
    nf = jnp.maximum(n, 1).astype(F32)
    large = max_exact + (jnp.log(nf / max_exact) / math.log(REL_MAX_DIST / max_exact)
                         * (NUM_BUCKETS - max_exact)).astype(jnp.int32)
    large = jnp.minimum(large, NUM_BUCKETS - 1)
    return jnp.where(n < max_exact, n, large)


def _attn_bias(rel_bias, window, dil, perm):
    steps = window // dil
    qi = jnp.arange(BLK, dtype=jnp.int32)[:, None]
    ki = jnp.arange(2 * BLK, dtype=jnp.int32)[None, :]
    dist = BLK + qi - ki
    in_win = (dist >= 0) & (dist <= steps)
    bias = rel_bias[_t5_bucket(jnp.clip(dist, 0, steps) * dil)].astype(F32)
    bias = jnp.where(in_win[:, :, None], bias, NEG_INF).transpose(2, 0, 1)
    perm = jnp.asarray(perm, jnp.int32)
    bias = bias[:, perm][:, :, jnp.concatenate([perm, BLK + perm])]
    return bias.reshape(H_A // 2, 2 * BLK, 2 * BLK)


_PERM_D1 = [NRES * jl + r for r in range(NRES) for jl in range(BLK // NRES)]
_PERM_D4 = [4 * jl + i for i in range(4) for jl in range(BLK // 4)]
_PERM_D16 = list(range(BLK))


def _attn_body(q_ref, k_ref, v_ref, kp_ref, vp_ref, b1_ref, b4_ref, b16_ref, o_ref, acc, mst, lst):
    has_prev = pl.program_id(0) > 0
    lane = lax.broadcasted_iota(jnp.int32, (1, LANES), 1)
    lo = lane < HD_A
    mlo = lo.astype(BF16)
    mhi = (~lo).astype(BF16)
    col = lax.broadcasted_iota(jnp.int32, (2 * BLK, 2 * BLK), 1)

    def tile(q2, kk, vv, bias, mask_prev):
        qs = jnp.concatenate([q2 * mlo, q2 * mhi], axis=0)
        s = _dot_nt(qs, kk) + bias
        if mask_prev:
            s = jnp.where(jnp.logical_or(has_prev, col >= BLK), s, NEG_INF)
        m = jnp.max(s, axis=-1, keepdims=True)
        e = jnp.exp(s - m)
        l = jnp.sum(e, axis=-1, keepdims=True)
        o = _dot(e.astype(BF16), vv)
        return (jnp.where(lo, m[:BLK], m[BLK:]), jnp.where(lo, l[:BLK], l[BLK:]),
                jnp.where(lo, o[:BLK], o[BLK:]))

    def merge(prev, cur):
        mp, lp, ap = prev
        mc, lc, ac = cur
        mn = jnp.maximum(mp, mc)
        a = jnp.exp(mp - mn)
        b = jnp.exp(mc - mn)
        return mn, a * lp + b * lc, a * ap + b * ac

    def cat(xs):
        return jnp.concatenate(xs, axis=0)

    def d16_body(r, _):
        rows = pl.ds(pl.multiple_of(r * BLK, BLK), BLK)
        for hp in range(H_A // 2):
            lanes = slice(hp * LANES, (hp + 1) * LANES)
            kk = cat([kp_ref[rows, lanes], k_ref[rows, lanes]])
            vv = cat([vp_ref[rows, lanes], v_ref[rows, lanes]])
            m2, l2, o2 = tile(q_ref[rows, lanes], kk, vv, b16_ref[hp], True)
            mst[rows, lanes] = m2
            lst[rows, lanes] = l2
            acc[rows, lanes] = o2
        return 0

    lax.fori_loop(0, NRES, d16_body, 0)

    def d4_body(r4, _):
        for b in range(4):
            def chunk_rows(bb):
                return [pl.ds(pl.multiple_of(4 * BLK * i + BLK * r4 + 32 * bb, 32), 32) for i in range(4)]
            rows = chunk_rows(b)
            prows = chunk_rows(3 if b == 0 else b - 1)
            kprev, vprev = (kp_ref, vp_ref) if b == 0 else (k_ref, v_ref)
            for hp in range(H_A // 2):
                lanes = slice(hp * LANES, (hp + 1) * LANES)
                q2 = cat([q_ref[rr, lanes] for rr in rows])
                kk = cat([kprev[rr, lanes] for rr in prows] + [k_ref[rr, lanes] for rr in rows])
                vv = cat([vprev[rr, lanes] for rr in prows] + [v_ref[rr, lanes] for rr in rows])
                cur = tile(q2, kk, vv, b4_ref[hp], b == 0)
                prev = (cat([mst[rr, lanes] for rr in rows]), cat([lst[rr, lanes] for rr in rows]),
                        cat([acc[rr, lanes] for rr in rows]))
                mn, ln, an = merge(prev, cur)
                for i, rr in enumerate(rows):
                    part = slice(32 * i, 32 * (i + 1))
                    mst[rr, lanes] = mn[part]
                    lst[rr, lanes] = ln[part]
                    acc[rr, lanes] = an[part]
        return 0

    lax.fori_loop(0, 4, d4_body, 0)

    def d1_pair(ap, kprev, vprev, prev_ap, mask_prev):
        def tiles(a_):
            return [pl.ds(pl.multiple_of(BLK * r + 16 * a_, 16), 16) for r in range(NRES)]
        cur_t = tiles(ap)
        prev_t = tiles(prev_ap)

        def halves(ref, ts, lanes):
            xs = [ref[t, lanes].astype(F32) for t in ts]
            return cat([x[:8] for x in xs]).astype(BF16), cat([x[8:] for x in xs]).astype(BF16)

        for hp in range(H_A // 2):
            lanes = slice(hp * LANES, (hp + 1) * LANES)
            q_e, q_o = halves(q_ref, cur_t, lanes)
            k_e, k_o = halves(k_ref, cur_t, lanes)
            v_e, v_o = halves(v_ref, cur_t, lanes)
            _, k_p = halves(kprev, prev_t, lanes)
            _, v_p = halves(vprev, prev_t, lanes)
            cur_e = tile(q_e, cat([k_p, k_e]), cat([v_p, v_e]), b1_ref[hp], mask_prev)
            cur_o = tile(q_o, cat([k_e, k_o]), cat([v_e, v_o]), b1_ref[hp], False)
            ms = [mst[t, lanes] for t in cur_t]
            ls = [lst[t, lanes] for t in cur_t]
            ac = [acc[t, lanes] for t in cur_t]
            outs = []
            for half, cur in ((0, cur_e), (1, cur_o)):
                part = slice(8 * half, 8 * half + 8)
                prev = (cat([x[part] for x in ms]), cat([x[part] for x in ls]), cat([x[part] for x in ac]))
                _, ln, an = merge(prev, cur)
                outs.append(an / ln)
            for r, t in enumerate(cur_t):
                part = slice(8 * r, 8 * r + 8)
                o_ref[t, lanes] = cat([outs[0][part], outs[1][part]]).astype(o_ref.dtype)

    d1_pair(0, kp_ref, vp_ref, BLK // 16 - 1, True)

    def d1_body(ap, _):
        d1_pair(ap, k_ref, v_ref, ap - 1, False)
        return 0

    lax.fori_loop(1, BLK // 16, d1_body, 0)


def _dilated_attention(proj, rel_bias):
    s = proj.shape[0]
    biases = [_attn_bias(rel_bias, window, dil, perm)
              for (window, dil), perm in zip(WIN_DIL, (_PERM_D1, _PERM_D4, _PERM_D16))]
    cur = lambda which: pl.BlockSpec((SPAN, D_A), lambda c: (c, which))
    prev = lambda which: pl.BlockSpec((SPAN, D_A), lambda c: (jnp.maximum(c - 1, 0), which))
    return pl.pallas_call(
        _attn_body,
        out_shape=jax.ShapeDtypeStruct((s, D_A), BF16),
        grid=(s // SPAN,),
        in_specs=[cur(0), cur(1), cur(2), prev(1), prev(2)] + [_full(b) for b in biases],
        out_specs=pl.BlockSpec((SPAN, D_A), lambda c: (c, 0)),
        scratch_shapes=[pltpu.VMEM((SPAN, D_A), F32)] * 3,
        compiler_params=_params(("arbitrary",)),
        name="attn",
    )(proj, proj, proj, proj, proj, *biases)


def _s5_operators(a_re, a_im, b_re, b_im, c_re, c_im, log_dt, d_skip):
    L = S5_CHUNK
    lam = lax.complex(a_re.astype(F32), a_im.astype(F32))
    dt = jnp.exp(log_dt.astype(F32))[:, None]
    a_bar = jnp.exp(lam * dt)
    b_bar = ((a_bar - 1.0) / lam)[..., None] * lax.complex(b_re.astype(F32), b_im.astype(F32))
    c = lax.complex(c_re.astype(F32), c_im.astype(F32))
    j = jnp.arange(L + 1, dtype=F32)
    log_a = lam * dt
    apow = jnp.exp(log_a[None] * j[:, None, None])
    kj = jnp.einsum('gdp,jgp,gpc->jgdc', c, apow[:L], b_bar).real
    s_idx = jnp.arange(L)[:, None]
    t_idx = jnp.arange(L)[None, :]
    lag = t_idx - s_idx
    m = jnp.where((lag >= 0)[:, :, None, None, None], kj[jnp.clip(lag, 0, L - 1)], 0.0)
    m = m.transpose(2, 0, 4, 1, 3)
    eye = jnp.eye(L)[:, None, :, None] * jnp.eye(S5_CH)[None, :, None, :]
    m = m + eye[None] * d_skip.astype(F32).reshape(S5_G, 1, S5_CH, 1, 1)
    m = m.reshape(S5_G, L * S5_CH, L * S5_CH)
    p = jnp.einsum('sgp,gpc->gscp', apow[:L][::-1], b_bar).reshape(S5_G, L * S5_CH, S5_P)
    ca = jnp.einsum('gdp,tgp->gptd', c, apow[1:L + 1]).reshape(S5_G, S5_P, L * S5_CH)
    a_l = apow[L]

    def pair_blocks(x):
        g, r, w = x.shape
        x = x.reshape(S5_PAIRS, 2, r, w)
        z = jnp.zeros_like(x[:, 0])
        top = jnp.concatenate([x[:, 0], z], axis=-1)
        bot = jnp.concatenate([z, x[:, 1]], axis=-1)
        return jnp.concatenate([top, bot], axis=1)

    p2 = jnp.concatenate([pair_blocks(p.real), pair_blocks(p.imag)], axis=-1)
    q2 = jnp.concatenate([pair_blocks(ca.real), pair_blocks(-ca.imag)], axis=1)
    a_lr = a_l.real.reshape(1, S5_G * S5_P)
    a_li = a_l.imag.reshape(1, S5_G * S5_P)
    return m.astype(BF16), p2.astype(BF16), q2.astype(BF16), a_lr, a_li


def _s5_body(w_ref, m_ref, p_ref, q_ref, ar_ref, ai_ref, o_ref, ere, eim, xre, xim, sre, sim):
    rows = w_ref.shape[0]
    gw = S5_CHUNK * S5_CH
    pw = 2 * gw

    @pl.when(pl.program_id(0) == 0)
    def _():
        sre[...] = jnp.zeros_like(sre)
        sim[...] = jnp.zeros_like(sim)

    for pr in range(S5_PAIRS):
        e = _dot(w_ref[:, pr * pw:(pr + 1) * pw], p_ref[pr])
        ere[:, pr * LANES:(pr + 1) * LANES] = e[:, :LANES]
        eim[:, pr * LANES:(pr + 1) * LANES] = e[:, LANES:]

    ar = ar_ref[...]
    ai = ai_ref[...]

    def step(n, carry):
        xr, xi = carry
        xre[pl.ds(n, 1), :] = xr
        xim[pl.ds(n, 1), :] = xi
        nr = ar * xr - ai * xi + ere[pl.ds(n, 1), :]
        ni = ar * xi + ai * xr + eim[pl.ds(n, 1), :]
        return nr, ni

    xr, xi = lax.fori_loop(0, rows, step, (sre[...], sim[...]))
    sre[...] = xr
    sim[...] = xi

    for pr in range(S5_PAIRS):
        xin = jnp.concatenate([xre[:, pr * LANES:(pr + 1) * LANES],
                               xim[:, pr * LANES:(pr + 1) * LANES]], axis=-1).astype(BF16)
        yc = _dot(xin, q_ref[pr])
        for half in range(2):
            g = 2 * pr + half
            cols = slice(g * gw, (g + 1) * gw)
            y = _dot(w_ref[:, cols], m_ref[g]) + yc[:, half * gw:(half + 1) * gw]
            o_ref[:, cols] = (0.5 * y * (1.0 + lax.erf(y * (2.0 ** -0.5)))).astype(BF16)


def _s5_core(u, ops):
    m, p2, q2, a_lr, a_li = ops
    s = u.shape[0]
    nsp = s // SPAN
    nc = s // S5_CHUNK
    wide = S5_G * S5_CHUNK * S5_CH
    w = u.reshape(nsp, NRES, BLK, S5_G, S5_CH).transpose(0, 2, 3, 1, 4).reshape(nc, wide)
    rows = min(S5_ROWS, nc)
    nstate = S5_G * S5_P
    y = pl.pallas_call(
        _s5_body,
        out_shape=jax.ShapeDtypeStruct((nc, wide), BF16),
        grid=(nc // rows,),
        in_specs=[pl.BlockSpec((rows, wide), lambda i: (i, 0)),
                  _full(m), _full(p2), _full(q2), _full(a_lr), _full(a_li)],
        out_specs=pl.BlockSpec((rows, wide), lambda i: (i, 0)),
        scratch_shapes=[pltpu.VMEM((rows, nstate), F32)] * 4 + [pltpu.VMEM((1, nstate), F32)] * 2,
        compiler_params=_params(("arbitrary",)),
        name="s5",
    )(w, m, p2, q2, a_lr, a_li)
    return y.reshape(nsp, BLK, S5_G, NRES, S5_CH).transpose(0, 3, 1, 2, 4).reshape(s, D_B)


def _split_bf16(x):
    hi = x.astype(BF16)
    lo = (x - hi.astype(F32)).astype(BF16)
    return hi, lo


def _mid_body(h_ref, ya_ref, yb_ref, wglu_ref, bglu_ref, ga_ref, gb_ref, wout_ref, gx_ref, wq_ref,
              k_ref, v_ref, wo_ref, gm_ref, wr_ref, br_ref, h_out, xn_out, idx_out, gate_out):
    yb = yb_ref[...]
    gate = jax.nn.sigmoid(_dot(yb, wglu_ref[...]) + bglu_ref[...])
    yb2 = yb.astype(F32) * gate
    na = _rms(ya_ref[...].astype(F32), ga_ref[...]).astype(BF16)
    nb = _rms(yb2, gb_ref[...]).astype(BF16)
    h1 = h_ref[...] + _dot(na, wout_ref[0:D_A, :]) + _dot(nb, wout_ref[D_A:D_MODEL, :])
    q = _dot(_rms(h1, gx_ref[...]).astype(BF16), wq_ref[...]).astype(BF16)
    heads = []
    for hd in range(H_X):
        lanes = slice(hd * HD_X, (hd + 1) * HD_X)
        s = _dot_nt(q[:, lanes], k_ref[:, lanes])
        e = jnp.exp(s - jnp.max(s, axis=-1, keepdims=True))
        p = e / jnp.sum(e, axis=-1, keepdims=True)
        heads.append(_dot(p.astype(BF16), v_ref[:, lanes]))
    o = jnp.concatenate(heads, axis=-1).astype(BF16)
    h2 = h1 + _dot(o, wo_ref[...])
    h_out[...] = h2
    xn = _rms(h2, gm_ref[...])
    xn_out[...] = xn.astype(BF16)
    x_hi, x_lo = _split_bf16(xn)
    logits = _dot(x_hi, wr_ref[0]) + _dot(x_lo, wr_ref[0]) + _dot(x_hi, wr_ref[1]) + br_ref[...]
    lane = lax.broadcasted_iota(jnp.int32, logits.shape, 1)
    logits = jnp.where(lane < N_EXPERTS, logits, -jnp.inf)
    vals, idxs = [], []
    for _ in range(TOP_K):
        mx = jnp.max(logits, axis=-1, keepdims=True)
        ix = jnp.min(jnp.where(logits == mx, lane, LANES), axis=-1, keepdims=True)
        vals.append(mx)
        idxs.append(ix)
        logits = jnp.where(lane == ix, -jnp.inf, logits)
    es = [jnp.exp(v - vals[0]) for v in vals]
    den = es[0] + es[1] + es[2] + es[3]
    idx_t = jnp.full(lane.shape, N_EXPERTS, jnp.int32)
    gate_t = jnp.zeros(lane.shape, F32)
    for k in range(TOP_K):
        idx_t = jnp.where(lane == k, idxs[k], idx_t)
        gate_t = jnp.where(lane == k, es[k] / den, gate_t)
    idx_out[...] = idx_t
    gate_out[...] = gate_t


def _mid(h, ya, yb, wglu, bglu, ga, gb, wout, gx, wq, kmem, vmem, wo, gm, wr2, br):
    s = h.shape[0]
    tm = min(ROW_TILE // 2, s)
    row = lambda w: pl.BlockSpec((tm, w), lambda i: (i, 0))
    consts = [wglu, bglu, ga, gb, wout, gx, wq, kmem, vmem, wo, gm, wr2, br]
    return pl.pallas_call(
        _mid_body,
        out_shape=[jax.ShapeDtypeStruct((s, D_MODEL), F32), jax.ShapeDtypeStruct((s, D_MODEL), BF16),
                   jax.ShapeDtypeStruct((s, LANES), jnp.int32), jax.ShapeDtypeStruct((s, LANES), F32)],
        grid=(s // tm,),
        in_specs=[row(D_MODEL), row(D_A), row(D_B)] + [_full(a) for a in consts],
        out_specs=[row(D_MODEL), row(D_MODEL), row(LANES), row(LANES)],
        compiler_params=_params(("parallel",)),
        name="mid",
    )(h, ya, yb, *consts)


def _memkv_body(mem_ref, g_ref, w_ref, o_ref):
    o_ref[...] = _dot(_rms(mem_ref[...], g_ref[...]).astype(BF16), w_ref[...]).astype(BF16)


def _memkv(mem, g, w):
    n, d = mem.shape
    return pl.pallas_call(
        _memkv_body,
        out_shape=jax.ShapeDtypeStruct((n, w.shape[1]), BF16),
        compiler_params=pltpu.CompilerParams(vmem_limit_bytes=VMEM_LIMIT),
        name="memkv",
    )(mem, g.reshape(1, d), w)


def _rank_body(idx_ref, rank_ref, cnt_ref, carry):
    @pl.when(pl.program_id(0) == 0)
    def _():
        carry[...] = jnp.zeros_like(carry)

    tm = idx_ref.shape[0]
    lane = lax.broadcasted_iota(jnp.int32, (tm, LANES), 1)
    idx = idx_ref[...]
    hits = [lane == idx[:, k:k + 1] for k in range(TOP_K)]
    onehot = jnp.zeros((tm, LANES), F32)
    for hit in hits:
        onehot = onehot + jnp.where(hit, 1.0, 0.0)
    ri = lax.broadcasted_iota(jnp.int32, (tm, tm), 0)
    ci = lax.broadcasted_iota(jnp.int32, (tm, tm), 1)
    tri = jnp.where(ri >= ci, 1.0, 0.0).astype(BF16)
    inclusive = _dot(tri, onehot.astype(BF16))
    before = carry[...] + inclusive - onehot
    rank = jnp.zeros((tm, LANES), jnp.int32)
    for k, hit in enumerate(hits):
        rk = jnp.sum(jnp.where(hit, before, 0.0), axis=-1, keepdims=True)
        rank = jnp.where(lane == k, rk.astype(jnp.int32), rank)
    rank_ref[...] = rank
    total = carry[...] + inclusive[tm - 1:tm, :]
    carry[...] = total
    cnt_ref[...] = total.astype(jnp.int32)


def _rank(idx):
    t = idx.shape[0]
    tm = min(ROW_TILE, t)
    return pl.pallas_call(
        _rank_body,
        out_shape=[jax.ShapeDtypeStruct((t, LANES), jnp.int32), jax.ShapeDtypeStruct((1, LANES), jnp.int32)],
        grid=(t // tm,),
        in_specs=[pl.BlockSpec((tm, LANES), lambda i: (i, 0))],
        out_specs=[pl.BlockSpec((tm, LANES), lambda i: (i, 0)), pl.BlockSpec((1, LANES), lambda i: (0, 0))],
        scratch_shapes=[pltpu.VMEM((1, LANES), F32)],
        compiler_params=_params(("arbitrary",)),
        name="rank",
    )(idx)


def _route(idx, tm):
    t = idx.shape[0]
    tk = t * TOP_K
    rank, cnt = _rank(idx)
    counts = cnt[0, :N_EXPERTS]
    padded = (counts + tm - 1) // tm * tm
    pend = jnp.cumsum(padded)
    pstart = pend - padded
    n_rows = tk + N_EXPERTS * tm
    n_tiles = n_rows // tm
    tile_first = jnp.arange(n_tiles, dtype=jnp.int32) * tm
    tile_e = jnp.minimum(jnp.sum(tile_first[:, None] >= pend[None, :], axis=1), N_EXPERTS - 1).astype(jnp.int32)
    tile_valid = (tile_first < pend[-1]).astype(jnp.int32)
    top = idx[:, :TOP_K]
    base = jnp.sum(jnp.where(top[:, :, None] == jnp.arange(N_EXPERTS, dtype=jnp.int32), pstart, 0), axis=-1)
    dest = (rank[:, :TOP_K] + base).reshape(tk)
    return dest, tile_e, tile_valid, n_rows


def _moe_body(te_ref, tv_ref, x_ref, w1_ref, b1_ref, w2_ref, b2_ref, o_ref, w1b, w2b):
    i = pl.program_id(0)
    new_expert = (i == 0) | (te_ref[i] != te_ref[jnp.maximum(i - 1, 0)])

    @pl.when(new_expert)
    def _():
        w1b[...] = w1_ref[0].astype(BF16)
        w2b[...] = w2_ref[0].astype(BF16)

    @pl.when(tv_ref[i] > 0)
    def _():
        hb = _dot(x_ref[...], w1b[...]) + b1_ref[0]
        x_glu = jnp.minimum(hb[:, :D_FF], SWIGLU_LIMIT)
        x_lin = jnp.clip(hb[:, D_FF:], -SWIGLU_LIMIT, SWIGLU_LIMIT)
        act = x_glu * jax.nn.sigmoid(SWIGLU_ALPHA * x_glu) * (x_lin + 1.0)
        o_ref[...] = (_dot(act.astype(BF16), w2b[...]) + b2_ref[0]).astype(BF16)

    @pl.when(tv_ref[i] == 0)
    def _():
        o_ref[...] = jnp.zeros_like(o_ref)


def _moe_experts(xs, tile_e, tile_valid, w1, b1, w2, b2, layer):
    n_rows, d = xs.shape
    tm = MOE_TILE
    nl, ne, _, ff2 = w1.shape
    grid_spec = pltpu.PrefetchScalarGridSpec(
        num_scalar_prefetch=2,
        grid=(n_rows // tm,),
        in_specs=[pl.BlockSpec((tm, d), lambda i, te, tv: (i, 0)),
                  pl.BlockSpec((None, 1, d, ff2), lambda i, te, tv: (layer, te[i], 0, 0)),
                  pl.BlockSpec((None, 1, 1, ff2), lambda i, te, tv: (layer, te[i], 0, 0)),
                  pl.BlockSpec((None, 1, ff2 // 2, d), lambda i, te, tv: (layer, te[i], 0, 0)),
                  pl.BlockSpec((None, 1, 1, d), lambda i, te, tv: (layer, te[i], 0, 0))],
        out_specs=pl.BlockSpec((tm, d), lambda i, te, tv: (i, 0)),
        scratch_shapes=[pltpu.VMEM((d, ff2), BF16), pltpu.VMEM((ff2 // 2, d), BF16)],
    )
    return pl.pallas_call(
        _moe_body,
        out_shape=jax.ShapeDtypeStruct((n_rows, d), BF16),
        grid_spec=grid_spec,
        compiler_params=_params(("arbitrary",)),
        name="moe",
    )(tile_e, tile_valid, xs, w1, b1.reshape(nl, ne, 1, ff2), w2, b2.reshape(nl, ne, 1, d))


def _combine_body(h_ref, y_ref, gate_ref, g_ref, o_ref, *, final):
    h = h_ref[...]
    gates = gate_ref[...]
    for k in range(TOP_K):
        h = h + y_ref[:, k, :].astype(F32) * gates[:, k:k + 1]
    o_ref[...] = _rms(h, g_ref[...]) if final else h


def _combine(h, y4, gates, g, *, final):
    s, d = h.shape
    tm = min(ROW_TILE // 2, s)
    return pl.pallas_call(
        functools.partial(_combine_body, final=final),
        out_shape=jax.ShapeDtypeStruct((s, d), F32),
        grid=(s // tm,),
        in_specs=[pl.BlockSpec((tm, d), lambda i: (i, 0)),
                  pl.BlockSpec((tm, TOP_K, d), lambda i: (i, 0, 0)),
                  pl.BlockSpec((tm, LANES), lambda i: (i, 0)),
                  pl.BlockSpec((1, d), lambda i: (0, 0))],
        out_specs=pl.BlockSpec((tm, d), lambda i: (i, 0)),
        compiler_params=_params(("parallel",)),
        name="combine",
    )(h, y4, gates, g.reshape(1, d))


def _layer(h, mem, rel_bias, p, l, g_final):
    row = lambda a: a.reshape(1, -1).astype(F32)
    t = h.shape[0]
    q_scale = jnp.concatenate([jnp.full((D_A,), HD_A ** -0.5, F32), jnp.ones((D_IN - D_A,), F32)])
    w_in = (p["w_in"][l] * q_scale).astype(BF16)
    proj = _inproj(h, p["norm_mix"][l], w_in)
    ya = _dilated_attention(proj, rel_bias)
    ops = _s5_operators(p["s5_a_re"][l], p["s5_a_im"][l], p["s5_b_re"][l], p["s5_b_im"][l],
                        p["s5_c_re"][l], p["s5_c_im"][l], p["s5_log_dt"][l], p["s5_d"][l])
    yb = _s5_core(proj[:, 3 * D_A:], ops)
    kv = _memkv(mem, p["norm_mem"][l], p["w_xkv"][l].astype(BF16))
    wr = jnp.pad(p["w_router"][l].astype(F32), ((0, 0), (0, LANES - N_EXPERTS)))
    wr_hi = wr.astype(BF16)
    wr2 = jnp.stack([wr_hi, (wr - wr_hi.astype(F32)).astype(BF16)])
    br = jnp.pad(p["b_router"][l].astype(F32), (0, LANES - N_EXPERTS)).reshape(1, LANES)
    h2, xn, idx, gates = _mid(
        h, ya, yb, p["w_glu"][l].astype(BF16), row(p["b_glu"][l]), row(p["g_out_attn"][l]),
        row(p["g_out_ssm"][l]), p["w_out"][l].astype(BF16), row(p["norm_xattn"][l]),
        (p["w_xq"][l] * (HD_X ** -0.5)).astype(BF16), kv[:, :D_X], kv[:, D_X:],
        p["w_xo"][l].astype(BF16), row(p["norm_moe"][l]), wr2, br)
    dest, tile_e, tile_valid, n_rows = _route(idx, MOE_TILE)
    tok = jnp.zeros((n_rows,), jnp.int32).at[dest].set(jnp.arange(t * TOP_K, dtype=jnp.int32) // TOP_K)
    out = _moe_experts(xn[tok], tile_e, tile_valid, p["w1"], p["b1"], p["w2"], p["b2"], l)
    return _combine(h2, out[dest].reshape(t, TOP_K, D_MODEL), gates, g_final, final=l == DEPTH - 1)


def kernel(x, mem, rel_bias, norm_mix, w_in, s5_a_re, s5_a_im, s5_b_re, s5_b_im, s5_c_re, s5_c_im, s5_log_dt, s5_d, w_glu, b_glu, g_out_attn, g_out_ssm, w_out, norm_xattn, norm_mem, w_xq, w_xkv, w_xo, norm_moe, w_router, b_router, w1, b1, w2, b2, norm_final):
    p = dict(norm_mix=norm_mix, w_in=w_in, s5_a_re=s5_a_re, s5_a_im=s5_a_im, s5_b_re=s5_b_re,
             s5_b_im=s5_b_im, s5_c_re=s5_c_re, s5_c_im=s5_c_im, s5_log_dt=s5_log_dt, s5_d=s5_d,
             w_glu=w_glu, b_glu=b_glu, g_out_attn=g_out_attn, g_out_ssm=g_out_ssm, w_out=w_out,
             norm_xattn=norm_xattn, norm_mem=norm_mem, w_xq=w_xq, w_xkv=w_xkv, w_xo=w_xo,
             norm_moe=norm_moe, w_router=w_router, b_router=b_router, w1=w1, b1=b1, w2=w2, b2=b2)
    outs = []
    for b in range(x.shape[0]):
        h = _to_span_layout(x[b])
        for l in range(DEPTH):
            h = _layer(h, mem[b], rel_bias, p, l, norm_final)
        outs.append(_from_span_layout(h))
    return jnp.stack(outs)
```

```python
import functools
import math

import jax
import jax.numpy as jnp
from jax import lax
from jax.experimental import pallas as pl
from jax.experimental.pallas import tpu as pltpu
from jax.experimental.pallas import tpu_sc as plsc

F32 = jnp.float32
BF16 = jnp.bfloat16

D_MODEL = 1024
DEPTH = 2
EPS = 1e-5
NEG_INF = -1e30
H_A = 8
HD_A = 64
D_A = H_A * HD_A
WIN_DIL = ((128, 1), (512, 4), (2048, 16))
BLK = 128
D_B = D_MODEL - D_A
S5_CH = 16
S5_G = D_B // S5_CH
S5_P = 64
D_IN = 3 * D_A + D_B
NUM_BUCKETS = 32
REL_MAX_DIST = 2048
H_X = 4
HD_X = 128
D_X = H_X * HD_X
N_EXPERTS = 32
TOP_K = 4
D_FF = D_MODEL
SWIGLU_ALPHA = 1.702
SWIGLU_LIMIT = 7.0

LANES = 128
NRES = WIN_DIL[-1][1]
SPAN = NRES * BLK
S5_CHUNK = NRES
S5_PAIRS = S5_G // 2
VMEM_LIMIT = 56 * 1024 * 1024

SC_CORES = 2
SC_SUBCORES = 16
SC_GATHER_ROWS = 64

ROW_TILE = 512
S5_ROWS = 128
MOE_TILE = 256


def _params(sem):
    return pltpu.CompilerParams(dimension_semantics=sem, vmem_limit_bytes=VMEM_LIMIT)


def _rms(x, g):
    return x * lax.rsqrt(jnp.mean(x * x, axis=-1, keepdims=True) + EPS) * g


def _dot(a, b):
    return jnp.dot(a, b, preferred_element_type=F32)


def _dot_nt(a, b):
    return lax.dot_general(a, b, (((1,), (1,)), ((), ())), preferred_element_type=F32)


def _full(a):
    return pl.BlockSpec(a.shape, lambda *_: (0,) * a.ndim)


def _pack_rows(x):
    c = x.shape[1] // 2
    lo = lax.bitcast_convert_type(x[:, :c].astype(BF16).astype(F32), jnp.uint32)
    hi = lax.bitcast_convert_type(x[:, c:].astype(BF16).astype(F32), jnp.uint32)
    return lax.bitcast_convert_type(lax.shift_right_logical(lo, jnp.uint32(16)) | hi, jnp.int32)


def _unpack_rows(p):
    u = lax.bitcast_convert_type(p, jnp.uint32)
    lo = lax.bitcast_convert_type(lax.shift_left(u, jnp.uint32(16)), F32)
    hi = lax.bitcast_convert_type(u & jnp.uint32(0xFFFF0000), F32)
    return lo, hi


def _to_span_layout(x):
    s = x.shape[0]
    return x.reshape(s // SPAN, BLK, NRES, -1).transpose(0, 2, 1, 3).reshape(s, -1)


def _from_span_layout(x):
    s = x.shape[0]
    return x.reshape(s // SPAN, NRES, BLK, -1).transpose(0, 2, 1, 3).reshape(s, -1)


def _inproj_body(h_ref, g_ref, w_ref, o_ref):
    xn = _rms(h_ref[...], g_ref[...]).astype(BF16)
    o_ref[...] = _dot(xn, w_ref[...]).astype(BF16)


def _inproj(h, g, w):
    s, d = h.shape
    n = w.shape[1]
    tm = min(ROW_TILE, s)
    return pl.pallas_call(
        _inproj_body,
        out_shape=jax.ShapeDtypeStruct((s, n), BF16),
        grid=(s // tm,),
        in_specs=[pl.BlockSpec((tm, d), lambda i: (i, 0)),
                  pl.BlockSpec((1, d), lambda i: (0, 0)),
                  pl.BlockSpec((d, n), lambda i: (0, 0))],
        out_specs=pl.BlockSpec((tm, n), lambda i: (i, 0)),
        compiler_params=_params(("parallel",)),
        name="inproj",
    )(h, g.reshape(1, d), w)


def _t5_bucket(n):
    max_exact = NUM_BUCKETS // 2
    nf = jnp.maximum(n, 1).astype(F32)
    large = max_exact + (jnp.log(nf / max_exact) / math.log(REL_MAX_DIST / max_exact)
                         * (NUM_BUCKETS - max_exact)).astype(jnp.int32)
    large = jnp.minimum(large, NUM_BUCKETS - 1)
    return jnp.where(n < max_exact, n, large)


def _attn_bias(rel_bias, window, dil, perm):
    steps = window // dil
    perm = jnp.asarray(perm, jnp.int32)
    qi = perm[:, None]
    ki = jnp.concatenate([perm, BLK + perm])[None, :]
    dist = BLK + qi - ki
    in_win = (dist >= 0) & (dist <= steps)
    bucket = _t5_bucket(jnp.clip(dist, 0, steps) * dil)
    onehot = (bucket[:, :, None] == jnp.arange(NUM_BUCKETS, dtype=jnp.int32)).astype(F32)
    bias = jnp.einsum('qkb,bh->hqk', onehot, rel_bias.astype(F32), precision=lax.Precision.HIGHEST)
    bias = jnp.where(in_win[None], bias, NEG_INF)
    return bias.reshape(H_A // 2, 2 * BLK, 2 * BLK)


_PERM_D1 = [NRES * jl + r for r in range(NRES) for jl in range(BLK // NRES)]
_PERM_D4 = [4 * jl + i for i in range(4) for jl in range(BLK // 4)]
_PERM_D16 = list(range(BLK))


def _attn_body(q_ref, k_ref, v_ref, kp_ref, vp_ref, b1_ref, b4_ref, b16_ref, o_ref, acc, mst, lst):
    has_prev = pl.program_id(0) > 0
    lane = lax.broadcasted_iota(jnp.int32, (1, LANES), 1)
    lo = lane < HD_A
    mlo = lo.astype(BF16)
    mhi = (~lo).astype(BF16)
    col = lax.broadcasted_iota(jnp.int32, (2 * BLK, 2 * BLK), 1)

    def tile(q2, kk, vv, bias, mask_prev):
        qs = jnp.concatenate([q2 * mlo, q2 * mhi], axis=0)
        s = _dot_nt(qs, kk) + bias
        if mask_prev:
            s = jnp.where(jnp.logical_or(has_prev, col >= BLK), s, NEG_INF)
        m = jnp.max(s, axis=-1, keepdims=True)
        e = jnp.exp(s - m)
        l = jnp.sum(e, axis=-1, keepdims=True)
        o = _dot(e.astype(BF16), vv)
        return (jnp.where(lo, m[:BLK], m[BLK:]), jnp.where(lo, l[:BLK], l[BLK:]),
                jnp.where(lo, o[:BLK], o[BLK:]))

    def merge(prev, cur):
        mp, lp, ap = prev
        mc, lc, ac = cur
        mn = jnp.maximum(mp, mc)
        a = jnp.exp(mp - mn)
        b = jnp.exp(mc - mn)
        return mn, a * lp + b * lc, a * ap + b * ac

    def cat(xs):
        return jnp.concatenate(xs, axis=0)

    def d16_body(r, _):
        rows = pl.ds(pl.multiple_of(r * BLK, BLK), BLK)
        for hp in range(H_A // 2):
            lanes = slice(hp * LANES, (hp + 1) * LANES)
            kk = cat([kp_ref[rows, lanes], k_ref[rows, lanes]])
            vv = cat([vp_ref[rows, lanes], v_ref[rows, lanes]])
            m2, l2, o2 = tile(q_ref[rows, lanes], kk, vv, b16_ref[hp], True)
            mst[rows, lanes] = m2
            lst[rows, lanes] = l2
            acc[rows, lanes] = o2
        return 0

    lax.fori_loop(0, NRES, d16_body, 0)

    def d4_body(r4, _):
        for b in range(4):
            def chunk_rows(bb):
                return [pl.ds(pl.multiple_of(4 * BLK * i + BLK * r4 + 32 * bb, 32), 32) for i in range(4)]
            rows = chunk_rows(b)
            prows = chunk_rows(3 if b == 0 else b - 1)
            kprev, vprev = (kp_ref, vp_ref) if b == 0 else (k_ref, v_ref)
            for hp in range(H_A // 2):
                lanes = slice(hp * LANES, (hp + 1) * LANES)
                q2 = cat([q_ref[rr, lanes] for rr in rows])
                kk = cat([kprev[rr, lanes] for rr in prows] + [k_ref[rr, lanes] for rr in rows])
                vv = cat([vprev[rr, lanes] for rr in prows] + [v_ref[rr, lanes] for rr in rows])
                cur = tile(q2, kk, vv, b4_ref[hp], b == 0)
                prev = (cat([mst[rr, lanes] for rr in rows]), cat([lst[rr, lanes] for rr in rows]),
                        cat([acc[rr, lanes] for rr in rows]))
                mn, ln, an = merge(prev, cur)
                for i, rr in enumerate(rows):
                    part = slice(32 * i, 32 * (i + 1))
                    mst[rr, lanes] = mn[part]
                    lst[rr, lanes] = ln[part]
                    acc[rr, lanes] = an[part]
        return 0

    lax.fori_loop(0, 4, d4_body, 0)

    def d1_pair(ap, kprev, vprev, prev_ap, mask_prev):
        def tiles(a_):
            return [pl.ds(pl.multiple_of(BLK * r + 16 * a_, 16), 16) for r in range(NRES)]
        cur_t = tiles(ap)
        prev_t = tiles(prev_ap)

        def halves(ref, ts, lanes):
            xs = [ref[t, lanes].astype(F32) for t in ts]
            return cat([x[:8] for x in xs]).astype(BF16), cat([x[8:] for x in xs]).astype(BF16)

        for hp in range(H_A // 2):
            lanes = slice(hp * LANES, (hp + 1) * LANES)
            q_e, q_o = halves(q_ref, cur_t, lanes)
            k_e, k_o = halves(k_ref, cur_t, lanes)
            v_e, v_o = halves(v_ref, cur_t, lanes)
            _, k_p = halves(kprev, prev_t, lanes)
            _, v_p = halves(vprev, prev_t, lanes)
            cur_e = tile(q_e, cat([k_p, k_e]), cat([v_p, v_e]), b1_ref[hp], mask_prev)
            cur_o = tile(q_o, cat([k_e, k_o]), cat([v_e, v_o]), b1_ref[hp], False)
            ms = [mst[t, lanes] for t in cur_t]
            ls = [lst[t, lanes] for t in cur_t]
            ac = [acc[t, lanes] for t in cur_t]
            outs = []
            for half, cur in ((0, cur_e), (1, cur_o)):
                part = slice(8 * half, 8 * half + 8)
                prev = (cat([x[part] for x in ms]), cat([x[part] for x in ls]), cat([x[part] for x in ac]))
                _, ln, an = merge(prev, cur)
                outs.append(an / ln)
            for r, t in enumerate(cur_t):
                part = slice(8 * r, 8 * r + 8)
                o_ref[t, lanes] = cat([outs[0][part], outs[1][part]]).astype(o_ref.dtype)

    d1_pair(0, kp_ref, vp_ref, BLK // 16 - 1, True)

    def d1_body(ap, _):
        d1_pair(ap, k_ref, v_ref, ap - 1, False)
        return 0

    lax.fori_loop(1, BLK // 16, d1_body, 0)


def _dilated_attention(proj, biases):
    s = proj.shape[0]
    cur = lambda which: pl.BlockSpec((SPAN, D_A), lambda c: (c, which))
    prev = lambda which: pl.BlockSpec((SPAN, D_A), lambda c: (jnp.maximum(c - 1, 0), which))
    return pl.pallas_call(
        _attn_body,
        out_shape=jax.ShapeDtypeStruct((s, D_A), BF16),
        grid=(s // SPAN,),
        in_specs=[cur(0), cur(1), cur(2), prev(1), prev(2)] + [_full(b) for b in biases],
        out_specs=pl.BlockSpec((SPAN, D_A), lambda c: (c, 0)),
        scratch_shapes=[pltpu.VMEM((SPAN, D_A), F32)] * 3,
        compiler_params=_params(("arbitrary",)),
        name="attn",
    )(proj, proj, proj, proj, proj, *biases)


def _s5_operators(a_re, a_im, b_re, b_im, c_re, c_im, log_dt, d_skip):
    L = S5_CHUNK
    lam = lax.complex(a_re.astype(F32), a_im.astype(F32))
    dt = jnp.exp(log_dt.astype(F32))[:, None]
    a_bar = jnp.exp(lam * dt)
    b_bar = ((a_bar - 1.0) / lam)[..., None] * lax.complex(b_re.astype(F32), b_im.astype(F32))
    c = lax.complex(c_re.astype(F32), c_im.astype(F32))
    j = jnp.arange(L + 1, dtype=F32)
    log_a = lam * dt
    apow = jnp.exp(log_a[None] * j[:, None, None])
    kj = jnp.einsum('gdp,jgp,gpc->jgdc', c, apow[:L], b_bar).real
    s_idx = jnp.arange(L)[:, None]
    t_idx = jnp.arange(L)[None, :]
    lag = t_idx - s_idx
    m = jnp.where((lag >= 0)[:, :, None, None, None], kj[jnp.clip(lag, 0, L - 1)], 0.0)
    m = m.transpose(2, 0, 4, 1, 3)
    eye = jnp.eye(L)[:, None, :, None] * jnp.eye(S5_CH)[None, :, None, :]
    m = m + eye[None] * d_skip.astype(F32).reshape(S5_G, 1, S5_CH, 1, 1)
    m = m.reshape(S5_G, L * S5_CH, L * S5_CH)
    p = jnp.einsum('sgp,gpc->gscp', apow[:L][::-1], b_bar).reshape(S5_G, L * S5_CH, S5_P)
    ca = jnp.einsum('gdp,tgp->gptd', c, apow[1:L + 1]).reshape(S5_G, S5_P, L * S5_CH)
    a_l = apow[L]

    def pair_blocks(x):
        g, r, w = x.shape
        x = x.reshape(S5_PAIRS, 2, r, w)
        z = jnp.zeros_like(x[:, 0])
        top = jnp.concatenate([x[:, 0], z], axis=-1)
        bot = jnp.concatenate([z, x[:, 1]], axis=-1)
        return jnp.concatenate([top, bot], axis=1)

    p2 = jnp.concatenate([pair_blocks(p.real), pair_blocks(p.imag)], axis=-1)
    q2 = jnp.concatenate([pair_blocks(ca.real), pair_blocks(-ca.imag)], axis=1)
    a_lr = a_l.real.reshape(1, S5_G * S5_P)
    a_li = a_l.imag.reshape(1, S5_G * S5_P)
    return m.astype(BF16), p2.astype(BF16), q2.astype(BF16), a_lr, a_li


def _s5_body(w_ref, m_ref, p_ref, q_ref, ar_ref, ai_ref, o_ref, ere, eim, xre, xim, sre, sim):
    rows = w_ref.shape[0]
    gw = S5_CHUNK * S5_CH
    pw = 2 * gw

    @pl.when(pl.program_id(0) == 0)
    def _():
        sre[...] = jnp.zeros_like(sre)
        sim[...] = jnp.zeros_like(sim)

    for pr in range(S5_PAIRS):
        e = _dot(w_ref[:, pr * pw:(pr + 1) * pw], p_ref[pr])
        ere[:, pr * LANES:(pr + 1) * LANES] = e[:, :LANES]
        eim[:, pr * LANES:(pr + 1) * LANES] = e[:, LANES:]

    ar = ar_ref[...]
    ai = ai_ref[...]

    def step(n, carry):
        xr, xi = carry
        xre[pl.ds(n, 1), :] = xr
        xim[pl.ds(n, 1), :] = xi
        nr = ar * xr - ai * xi + ere[pl.ds(n, 1), :]
        ni = ar * xi + ai * xr + eim[pl.ds(n, 1), :]
        return nr, ni

    xr, xi = lax.fori_loop(0, rows, step, (sre[...], sim[...]))
    sre[...] = xr
    sim[...] = xi

    for pr in range(S5_PAIRS):
        xin = jnp.concatenate([xre[:, pr * LANES:(pr + 1) * LANES],
                               xim[:, pr * LANES:(pr + 1) * LANES]], axis=-1).astype(BF16)
        yc = _dot(xin, q_ref[pr])
        for half in range(2):
            g = 2 * pr + half
            cols = slice(g * gw, (g + 1) * gw)
            y = _dot(w_ref[:, cols], m_ref[g]) + yc[:, half * gw:(half + 1) * gw]
            o_ref[:, cols] = (0.5 * y * (1.0 + lax.erf(y * (2.0 ** -0.5)))).astype(BF16)


def _s5_core(u, ops):
    m, p2, q2, a_lr, a_li = ops
    s = u.shape[0]
    nsp = s // SPAN
    nc = s // S5_CHUNK
    wide = S5_G * S5_CHUNK * S5_CH
    w = u.reshape(nsp, NRES, BLK, S5_G, S5_CH).transpose(0, 2, 3, 1, 4).reshape(nc, wide)
    rows = min(S5_ROWS, nc)
    nstate = S5_G * S5_P
    y = pl.pallas_call(
        _s5_body,
        out_shape=jax.ShapeDtypeStruct((nc, wide), BF16),
        grid=(nc // rows,),
        in_specs=[pl.BlockSpec((rows, wide), lambda i: (i, 0)),
                  _full(m), _full(p2), _full(q2), _full(a_lr), _full(a_li)],
        out_specs=pl.BlockSpec((rows, wide), lambda i: (i, 0)),
        scratch_shapes=[pltpu.VMEM((rows, nstate), F32)] * 4 + [pltpu.VMEM((1, nstate), F32)] * 2,
        compiler_params=_params(("arbitrary",)),
        name="s5",
    )(w, m, p2, q2, a_lr, a_li)
    return y.reshape(nsp, BLK, S5_G, NRES, S5_CH).transpose(0, 3, 1, 2, 4).reshape(s, D_B)


def _split_bf16(x):
    hi = x.astype(BF16)
    lo = (x - hi.astype(F32)).astype(BF16)
    return hi, lo


def _mid_body(h_ref, ya_ref, yb_ref, wglu_ref, bglu_ref, ga_ref, gb_ref, wout_ref, gx_ref, wq_ref,
              k_ref, v_ref, wo_ref, gm_ref, wr_ref, br_ref, h_out, xn_out, idx_out, gate_out):
    yb = yb_ref[...]
    gate = jax.nn.sigmoid(_dot(yb, wglu_ref[...]) + bglu_ref[...])
    yb2 = yb.astype(F32) * gate
    na = _rms(ya_ref[...].astype(F32), ga_ref[...]).astype(BF16)
    nb = _rms(yb2, gb_ref[...]).astype(BF16)
    h1 = h_ref[...] + _dot(na, wout_ref[0:D_A, :]) + _dot(nb, wout_ref[D_A:D_MODEL, :])
    q = _dot(_rms(h1, gx_ref[...]).astype(BF16), wq_ref[...]).astype(BF16)
    heads = []
    for hd in range(H_X):
        lanes = slice(hd * HD_X, (hd + 1) * HD_X)
        s = _dot_nt(q[:, lanes], k_ref[:, lanes])
        e = jnp.exp(s - jnp.max(s, axis=-1, keepdims=True))
        p = e / jnp.sum(e, axis=-1, keepdims=True)
        heads.append(_dot(p.astype(BF16), v_ref[:, lanes]))
    o = jnp.concatenate(heads, axis=-1).astype(BF16)
    h2 = h1 + _dot(o, wo_ref[...])
    h_out[...] = h2
    xn = _rms(h2, gm_ref[...])
    xn_out[...] = _pack_rows(xn)
    x_hi, x_lo = _split_bf16(xn)
    logits = _dot(x_hi, wr_ref[0]) + _dot(x_lo, wr_ref[0]) + _dot(x_hi, wr_ref[1]) + br_ref[...]
    lane = lax.broadcasted_iota(jnp.int32, logits.shape, 1)
    logits = jnp.where(lane < N_EXPERTS, logits, -jnp.inf)
    vals, idxs = [], []
    for _ in range(TOP_K):
        mx = jnp.max(logits, axis=-1, keepdims=True)
        ix = jnp.min(jnp.where(logits == mx, lane, LANES), axis=-1, keepdims=True)
        vals.append(mx)
        idxs.append(ix)
        logits = jnp.where(lane == ix, -jnp.inf, logits)
    es = [jnp.exp(v - vals[0]) for v in vals]
    den = es[0] + es[1] + es[2] + es[3]
    idx_t = jnp.full(lane.shape, N_EXPERTS, jnp.int32)
    gate_t = jnp.zeros(lane.shape, F32)
    for k in range(TOP_K):
        idx_t = jnp.where(lane == k, idxs[k], idx_t)
        gate_t = jnp.where(lane == k, es[k] / den, gate_t)
    idx_out[...] = idx_t
    gate_out[...] = gate_t


def _mid(h, ya, yb, wglu, bglu, ga, gb, wout, gx, wq, kmem, vmem, wo, gm, wr2, br):
    s = h.shape[0]
    tm = min(ROW_TILE // 2, s)
    row = lambda w: pl.BlockSpec((tm, w), lambda i: (i, 0))
    consts = [wglu, bglu, ga, gb, wout, gx, wq, kmem, vmem, wo, gm, wr2, br]
    return pl.pallas_call(
        _mid_body,
        out_shape=[jax.ShapeDtypeStruct((s, D_MODEL), F32), jax.ShapeDtypeStruct((s, D_MODEL // 2), jnp.int32),
                   jax.ShapeDtypeStruct((s, LANES), jnp.int32), jax.ShapeDtypeStruct((s, LANES), F32)],
        grid=(s // tm,),
        in_specs=[row(D_MODEL), row(D_A), row(D_B)] + [_full(a) for a in consts],
        out_specs=[row(D_MODEL), row(D_MODEL // 2), row(LANES), row(LANES)],
        compiler_params=_params(("parallel",)),
        name="mid",
    )(h, ya, yb, *consts)


def _memkv_body(mem_ref, g_ref, w_ref, o_ref):
    o_ref[...] = _dot(_rms(mem_ref[...], g_ref[...]).astype(BF16), w_ref[...]).astype(BF16)


def _memkv(mem, g, w):
    n, d = mem.shape
    return pl.pallas_call(
        _memkv_body,
        out_shape=jax.ShapeDtypeStruct((n, w.shape[1]), BF16),
        compiler_params=pltpu.CompilerParams(vmem_limit_bytes=VMEM_LIMIT),
        name="memkv",
    )(mem, g.reshape(1, d), w)


def _rank_body(idx_ref, rank_ref, cnt_ref, carry):
    @pl.when(pl.program_id(0) == 0)
    def _():
        carry[...] = jnp.zeros_like(carry)

    tm = idx_ref.shape[0]
    lane = lax.broadcasted_iota(jnp.int32, (tm, LANES), 1)
    idx = idx_ref[...]
    hits = [lane == idx[:, k:k + 1] for k in range(TOP_K)]
    onehot = jnp.zeros((tm, LANES), F32)
    for hit in hits:
        onehot = onehot + jnp.where(hit, 1.0, 0.0)
    ri = lax.broadcasted_iota(jnp.int32, (tm, tm), 0)
    ci = lax.broadcasted_iota(jnp.int32, (tm, tm), 1)
    tri = jnp.where(ri >= ci, 1.0, 0.0).astype(BF16)
    inclusive = _dot(tri, onehot.astype(BF16))
    before = carry[...] + inclusive - onehot
    rank = jnp.zeros((tm, LANES), jnp.int32)
    for k, hit in enumerate(hits):
        rk = jnp.sum(jnp.where(hit, before, 0.0), axis=-1, keepdims=True)
        rank = jnp.where(lane == k, rk.astype(jnp.int32), rank)
    rank_ref[...] = rank
    total = carry[...] + inclusive[tm - 1:tm, :]
    carry[...] = total
    cnt_ref[...] = total.astype(jnp.int32)


def _rank(idx):
    t = idx.shape[0]
    tm = min(ROW_TILE, t)
    return pl.pallas_call(
        _rank_body,
        out_shape=[jax.ShapeDtypeStruct((t, LANES), jnp.int32), jax.ShapeDtypeStruct((1, LANES), jnp.int32)],
        grid=(t // tm,),
        in_specs=[pl.BlockSpec((tm, LANES), lambda i: (i, 0))],
        out_specs=[pl.BlockSpec((tm, LANES), lambda i: (i, 0)), pl.BlockSpec((1, LANES), lambda i: (0, 0))],
        scratch_shapes=[pltpu.VMEM((1, LANES), F32)],
        compiler_params=_params(("arbitrary",)),
        name="rank",
    )(idx)


def _route(idx, tm):
    t = idx.shape[0]
    tk = t * TOP_K
    rank, cnt = _rank(idx)
    counts = cnt[0, :N_EXPERTS]
    padded = (counts + tm - 1) // tm * tm
    pend = jnp.cumsum(padded)
    pstart = pend - padded
    n_rows = tk + N_EXPERTS * tm
    n_tiles = n_rows // tm
    tile_first = jnp.arange(n_tiles, dtype=jnp.int32) * tm
    tile_e = jnp.minimum(jnp.sum(tile_first[:, None] >= pend[None, :], axis=1), N_EXPERTS - 1).astype(jnp.int32)
    tile_valid = (tile_first < pend[-1]).astype(jnp.int32)
    top = idx[:, :TOP_K]
    base = jnp.sum(jnp.where(top[:, :, None] == jnp.arange(N_EXPERTS, dtype=jnp.int32), pstart, 0), axis=-1)
    dest = (rank[:, :TOP_K] + base).T.reshape(tk)
    fill_e = jnp.repeat(jnp.arange(N_EXPERTS, dtype=jnp.int32), tm)
    fill_j = jnp.tile(jnp.arange(tm, dtype=jnp.int32), N_EXPERTS)
    fill_key = jnp.where(fill_j < jnp.repeat(padded - counts, tm), 2 * fill_e + 1, 2 * N_EXPERTS)
    keys = jnp.concatenate([2 * top.reshape(tk), fill_key])
    vals = jnp.concatenate([jnp.arange(tk, dtype=jnp.int32) // TOP_K, jnp.zeros((N_EXPERTS * tm,), jnp.int32)])
    _, tok = lax.sort((keys, vals), num_keys=1, is_stable=True)
    return dest, tok, tile_e, tile_valid


def _gather_rows(table, idx):
    n, d = table.shape
    b = idx.shape[0]
    workers = SC_CORES * SC_SUBCORES
    per_worker = b // workers
    chunks = per_worker // SC_GATHER_ROWS
    mesh = plsc.VectorSubcoreMesh(core_axis_name="c", subcore_axis_name="s",
                                  num_cores=SC_CORES, num_subcores=SC_SUBCORES)

    @functools.partial(
        pl.kernel, mesh=mesh,
        out_type=jax.ShapeDtypeStruct((b, d), table.dtype),
        scratch_types=[pltpu.VMEM((SC_GATHER_ROWS,), jnp.int32),
                       pltpu.VMEM((SC_GATHER_ROWS, d), table.dtype),
                       pltpu.SemaphoreType.DMA],
    )
    def gather(table_hbm, idx_hbm, out_hbm, idx_v, rows_v, sem):
        base = (lax.axis_index("s") * SC_CORES + lax.axis_index("c")) * per_worker

        @pl.loop(0, chunks)
        def _(c):
            off = pl.multiple_of(base + c * SC_GATHER_ROWS, SC_GATHER_ROWS)
            pltpu.sync_copy(idx_hbm.at[pl.ds(off, SC_GATHER_ROWS)], idx_v)
            pltpu.async_copy(table_hbm.at[idx_v], rows_v, sem).wait()
            pltpu.sync_copy(rows_v, out_hbm.at[pl.ds(off, SC_GATHER_ROWS)])

    return gather(table, idx)


def _moe_body(te_ref, tv_ref, x_ref, w1_ref, b1_ref, w2_ref, b2_ref, o_ref, w1b, w2b):
    i = pl.program_id(0)
    new_expert = (i == 0) | (te_ref[i] != te_ref[jnp.maximum(i - 1, 0)])

    @pl.when(new_expert)
    def _():
        w1b[...] = w1_ref[0].astype(BF16)
        w2b[...] = w2_ref[0].astype(BF16)

    @pl.when(tv_ref[i] > 0)
    def _():
        lo, hi = _unpack_rows(x_ref[...])
        x = jnp.concatenate([lo, hi], axis=-1).astype(BF16)
        hb = _dot(x, w1b[...]) + b1_ref[0]
        x_glu = jnp.minimum(hb[:, :D_FF], SWIGLU_LIMIT)
        x_lin = jnp.clip(hb[:, D_FF:], -SWIGLU_LIMIT, SWIGLU_LIMIT)
        act = x_glu * jax.nn.sigmoid(SWIGLU_ALPHA * x_glu) * (x_lin + 1.0)
        o_ref[...] = _pack_rows(_dot(act.astype(BF16), w2b[...]) + b2_ref[0])

    @pl.when(tv_ref[i] == 0)
    def _():
        o_ref[...] = jnp.zeros_like(o_ref)


def _moe_experts(xs, tile_e, tile_valid, w1, b1, w2, b2, layer):
    n_rows = xs.shape[0]
    tm = MOE_TILE
    nl, ne, d, ff2 = w1.shape
    grid_spec = pltpu.PrefetchScalarGridSpec(
        num_scalar_prefetch=2,
        grid=(n_rows // tm,),
        in_specs=[pl.BlockSpec((tm, d // 2), lambda i, te, tv: (i, 0)),
                  pl.BlockSpec((None, 1, d, ff2), lambda i, te, tv: (layer, te[i], 0, 0)),
                  pl.BlockSpec((None, 1, 1, ff2), lambda i, te, tv: (layer, te[i], 0, 0)),
                  pl.BlockSpec((None, 1, ff2 // 2, d), lambda i, te, tv: (layer, te[i], 0, 0)),
                  pl.BlockSpec((None, 1, 1, d), lambda i, te, tv: (layer, te[i], 0, 0))],
        out_specs=pl.BlockSpec((tm, d // 2), lambda i, te, tv: (i, 0)),
        scratch_shapes=[pltpu.VMEM((d, ff2), BF16), pltpu.VMEM((ff2 // 2, d), BF16)],
    )
    return pl.pallas_call(
        _moe_body,
        out_shape=jax.ShapeDtypeStruct((n_rows, d // 2), jnp.int32),
        grid_spec=grid_spec,
        compiler_params=_params(("arbitrary",)),
        name="moe",
    )(tile_e, tile_valid, xs, w1, b1.reshape(nl, ne, 1, ff2), w2, b2.reshape(nl, ne, 1, d))


def _combine_body(h_ref, y0_ref, y1_ref, y2_ref, y3_ref, gate_ref, g_ref, o_ref, *, final):
    gates = gate_ref[...]
    lo = jnp.zeros(y0_ref.shape, F32)
    hi = jnp.zeros(y0_ref.shape, F32)
    for k, y_ref in enumerate((y0_ref, y1_ref, y2_ref, y3_ref)):
        yl, yh = _unpack_rows(y_ref[...])
        lo = lo + yl * gates[:, k:k + 1]
        hi = hi + yh * gates[:, k:k + 1]
    h = h_ref[...] + jnp.concatenate([lo, hi], axis=-1)
    o_ref[...] = _rms(h, g_ref[...]) if final else h


def _combine(h, y, gates, g, *, final):
    s, d = h.shape
    tm = min(ROW_TILE // 2, s)
    nt = s // tm
    return pl.pallas_call(
        functools.partial(_combine_body, final=final),
        out_shape=jax.ShapeDtypeStruct((s, d), F32),
        grid=(nt,),
        in_specs=[pl.BlockSpec((tm, d), lambda i: (i, 0))]
                 + [pl.BlockSpec((tm, d // 2), lambda i, k=k: (k * nt + i, 0)) for k in range(TOP_K)]
                 + [pl.BlockSpec((tm, LANES), lambda i: (i, 0)), pl.BlockSpec((1, d), lambda i: (0, 0))],
        out_specs=pl.BlockSpec((tm, d), lambda i: (i, 0)),
        compiler_params=_params(("parallel",)),
        name="combine",
    )(h, y, y, y, y, gates, g.reshape(1, d))


def _layer(h, mem, biases, p, l, g_final):
    row = lambda a: a.reshape(1, -1).astype(F32)
    q_scale = jnp.concatenate([jnp.full((D_A,), HD_A ** -0.5, F32), jnp.ones((D_IN - D_A,), F32)])
    w_in = (p["w_in"][l] * q_scale).astype(BF16)
    proj = _inproj(h, p["norm_mix"][l], w_in)
    ya = _dilated_attention(proj, biases)
    ops = _s5_operators(p["s5_a_re"][l], p["s5_a_im"][l], p["s5_b_re"][l], p["s5_b_im"][l],
                        p["s5_c_re"][l], p["s5_c_im"][l], p["s5_log_dt"][l], p["s5_d"][l])
    yb = _s5_core(proj[:, 3 * D_A:], ops)
    kv = _memkv(mem, p["norm_mem"][l], p["w_xkv"][l].astype(BF16))
    wr = jnp.pad(p["w_router"][l].astype(F32), ((0, 0), (0, LANES - N_EXPERTS)))
    wr_hi = wr.astype(BF16)
    wr2 = jnp.stack([wr_hi, (wr - wr_hi.astype(F32)).astype(BF16)])
    br = jnp.pad(p["b_router"][l].astype(F32), (0, LANES - N_EXPERTS)).reshape(1, LANES)
    h2, xn, idx, gates = _mid(
        h, ya, yb, p["w_glu"][l].astype(BF16), row(p["b_glu"][l]), row(p["g_out_attn"][l]),
        row(p["g_out_ssm"][l]), p["w_out"][l].astype(BF16), row(p["norm_xattn"][l]),
        (p["w_xq"][l] * (HD_X ** -0.5)).astype(BF16), kv[:, :D_X], kv[:, D_X:],
        p["w_xo"][l].astype(BF16), row(p["norm_moe"][l]), wr2, br)
    dest, tok, tile_e, tile_valid = _route(idx, MOE_TILE)
    out = _moe_experts(_gather_rows(xn, tok), tile_e, tile_valid, p["w1"], p["b1"], p["w2"], p["b2"], l)
    return _combine(h2, _gather_rows(out, dest), gates, g_final, final=l == DEPTH - 1)


def kernel(x, mem, rel_bias, norm_mix, w_in, s5_a_re, s5_a_im, s5_b_re, s5_b_im, s5_c_re, s5_c_im, s5_log_dt, s5_d, w_glu, b_glu, g_out_attn, g_out_ssm, w_out, norm_xattn, norm_mem, w_xq, w_xkv, w_xo, norm_moe, w_router, b_router, w1, b1, w2, b2, norm_final):
    p = dict(norm_mix=norm_mix, w_in=w_in, s5_a_re=s5_a_re, s5_a_im=s5_a_im, s5_b_re=s5_b_re,
             s5_b_im=s5_b_im, s5_c_re=s5_c_re, s5_c_im=s5_c_im, s5_log_dt=s5_log_dt, s5_d=s5_d,
             w_glu=w_glu, b_glu=b_glu, g_out_attn=g_out_attn, g_out_ssm=g_out_ssm, w_out=w_out,
             norm_xattn=norm_xattn, norm_mem=norm_mem, w_xq=w_xq, w_xkv=w_xkv, w_xo=w_xo,
             norm_moe=norm_moe, w_router=w_router, b_router=b_router, w1=w1, b1=b1, w2=w2, b2=b2)
    biases = [_attn_bias(rel_bias, window, dil, perm)
              for (window, dil), perm in zip(WIN_DIL, (_PERM_D1, _PERM_D4, _PERM_D16))]
    outs = []
    for b in range(x.shape[0]):
        h = _to_span_layout(x[b])
        for l in range(DEPTH):
            h = _layer(h, mem[b], biases, p, l, norm_final)
        outs.append(_from_span_layout(h))
    return jnp.stack(outs)
```

```python
import functools
import math

import jax
import jax.numpy as jnp
from jax import lax
from jax.experimental import pallas as pl
from jax.experimental.pallas import tpu as pltpu
from jax.experimental.pallas import tpu_sc as plsc

F32 = jnp.float32
BF16 = jnp.bfloat16

D_MODEL = 1024
DEPTH = 2
EPS = 1e-5
NEG_INF = -1e30
H_A = 8
HD_A = 64
D_A = H_A * HD_A
WIN_DIL = ((128, 1), (512, 4), (2048, 16))
BLK = 128
D_B = D_MODEL - D_A
S5_CH = 16
S5_G = D_B // S5_CH
S5_P = 64
D_IN = 3 * D_A + D_B
NUM_BUCKETS = 32
REL_MAX_DIST = 2048
H_X = 4
HD_X = 128
D_X = H_X * HD_X
N_EXPERTS = 32
TOP_K = 4
D_FF = D_MODEL
SWIGLU_ALPHA = 1.702
SWIGLU_LIMIT = 7.0

LANES = 128
NRES = WIN_DIL[-1][1]
SPAN = NRES * BLK
S5_CHUNK = NRES
S5_PAIRS = S5_G // 2
VMEM_LIMIT = 56 * 1024 * 1024

SC_CORES = 2
SC_SUBCORES = 16
SC_GATHER_ROWS = 64

ROW_TILE = 512
S5_ROWS = 128
MOE_TILE = 256
MOE_W1_PARTS = 4
MOE_W2_PARTS = 2


def _params(sem):
    return pltpu.CompilerParams(dimension_semantics=sem, vmem_limit_bytes=VMEM_LIMIT)


def _rms(x, g):
    return x * lax.rsqrt(jnp.mean(x * x, axis=-1, keepdims=True) + EPS) * g


def _dot(a, b):
    return jnp.dot(a, b, preferred_element_type=F32)


def _dot_nt(a, b):
    return lax.dot_general(a, b, (((1,), (1,)), ((), ())), preferred_element_type=F32)


def _full(a):
    return pl.BlockSpec(a.shape, lambda *_: (0,) * a.ndim)


def _pack_rows(x):
    c = x.shape[1] // 2
    lo = lax.bitcast_convert_type(x[:, :c].astype(BF16).astype(F32), jnp.uint32)
    hi = lax.bitcast_convert_type(x[:, c:].astype(BF16).astype(F32), jnp.uint32)
    return lax.bitcast_convert_type(lax.shift_right_logical(lo, jnp.uint32(16)) | hi, jnp.int32)


def _unpack_rows(p):
    u = lax.bitcast_convert_type(p, jnp.uint32)
    lo = lax.bitcast_convert_type(lax.shift_left(u, jnp.uint32(16)), F32)
    hi = lax.bitcast_convert_type(u & jnp.uint32(0xFFFF0000), F32)
    return lo, hi


def _to_span_layout(x):
    s = x.shape[0]
    return x.reshape(s // SPAN, BLK, NRES, -1).transpose(0, 2, 1, 3).reshape(s, -1)


def _from_span_layout(x):
    s = x.shape[0]
    return x.reshape(s // SPAN, NRES, BLK, -1).transpose(0, 2, 1, 3).reshape(s, -1)


def _inproj_body(h_ref, g_ref, w_ref, o_ref):
    xn = _rms(h_ref[...], g_ref[...]).astype(BF16)
    o_ref[...] = _dot(xn, w_ref[...]).astype(BF16)


def _inproj(h, g, w):
    s, d = h.shape
    n = w.shape[1]
    tm = min(ROW_TILE, s)
    return pl.pallas_call(
        _inproj_body,
        out_shape=jax.ShapeDtypeStruct((s, n), BF16),
        grid=(s // tm,),
        in_specs=[pl.BlockSpec((tm, d), lambda i: (i, 0)),
                  pl.BlockSpec((1, d), lambda i: (0, 0)),
                  pl.BlockSpec((d, n), lambda i: (0, 0))],
        out_specs=pl.BlockSpec((tm, n), lambda i: (i, 0)),
        compiler_params=_params(("parallel",)),
        name="inproj",
    )(h, g.reshape(1, d), w)


def _t5_bucket(n):
    max_exact = NUM_BUCKETS // 2
    nf = jnp.maximum(n, 1).astype(F32)
    large = max_exact + (jnp.log(nf / max_exact) / math.log(REL_MAX_DIST / max_exact)
                         * (NUM_BUCKETS - max_exact)).astype(jnp.int32)
    large = jnp.minimum(large, NUM_BUCKETS - 1)
    return jnp.where(n < max_exact, n, large)


def _attn_bias(rel_bias, window, dil, perm):
    steps = window // dil
    perm = jnp.asarray(perm, jnp.int32)
    qi = perm[:, None]
    ki = jnp.concatenate([perm, BLK + perm])[None, :]
    dist = BLK + qi - ki
    in_win = (dist >= 0) & (dist <= steps)
    bucket = _t5_bucket(jnp.clip(dist, 0, steps) * dil)
    onehot = (bucket[:, :, None] == jnp.arange(NUM_BUCKETS, dtype=jnp.int32)).astype(F32)
    bias = jnp.einsum('qkb,bh->hqk', onehot, rel_bias.astype(F32), precision=lax.Precision.HIGHEST)
    bias = jnp.where(in_win[None], bias, NEG_INF)
    return bias.reshape(H_A // 2, 2 * BLK, 2 * BLK)


_PERM_D1 = [NRES * jl + r for r in range(NRES) for jl in range(BLK // NRES)]
_PERM_D4 = [4 * jl + i for i in range(4) for jl in range(BLK // 4)]
_PERM_D16 = list(range(BLK))


def _attn_body(q_ref, k_ref, v_ref, kp_ref, vp_ref, b1_ref, b4_ref, b16_ref, o_ref, acc, mst, lst):
    has_prev = pl.program_id(0) > 0
    lane = lax.broadcasted_iota(jnp.int32, (1, LANES), 1)
    lo = lane < HD_A
    mlo = lo.astype(BF16)
    mhi = (~lo).astype(BF16)
    col = lax.broadcasted_iota(jnp.int32, (2 * BLK, 2 * BLK), 1)

    def tile(q2, kk, vv, bias, mask_prev):
        qs = jnp.concatenate([q2 * mlo, q2 * mhi], axis=0)
        s = _dot_nt(qs, kk) + bias
        if mask_prev:
            s = jnp.where(jnp.logical_or(has_prev, col >= BLK), s, NEG_INF)
        m = jnp.max(s, axis=-1, keepdims=True)
        e = jnp.exp(s - m)
        l = jnp.sum(e, axis=-1, keepdims=True)
        o = _dot(e.astype(BF16), vv)
        return (jnp.where(lo, m[:BLK], m[BLK:]), jnp.where(lo, l[:BLK], l[BLK:]),
                jnp.where(lo, o[:BLK], o[BLK:]))

    def merge(prev, cur):
        mp, lp, ap = prev
        mc, lc, ac = cur
        mn = jnp.maximum(mp, mc)
        a = jnp.exp(mp - mn)
        b = jnp.exp(mc - mn)
        return mn, a * lp + b * lc, a * ap + b * ac

    def cat(xs):
        return jnp.concatenate(xs, axis=0)

    def d16_body(r, _):
        rows = pl.ds(pl.multiple_of(r * BLK, BLK), BLK)
        for hp in range(H_A // 2):
            lanes = slice(hp * LANES, (hp + 1) * LANES)
            kk = cat([kp_ref[rows, lanes], k_ref[rows, lanes]])
            vv = cat([vp_ref[rows, lanes], v_ref[rows, lanes]])
            m2, l2, o2 = tile(q_ref[rows, lanes], kk, vv, b16_ref[hp], True)
            mst[rows, lanes] = m2
            lst[rows, lanes] = l2
            acc[rows, lanes] = o2
        return 0

    lax.fori_loop(0, NRES, d16_body, 0)

    def d4_body(r4, _):
        for b in range(4):
            def chunk_rows(bb):
                return [pl.ds(pl.multiple_of(4 * BLK * i + BLK * r4 + 32 * bb, 32), 32) for i in range(4)]
            rows = chunk_rows(b)
            prows = chunk_rows(3 if b == 0 else b - 1)
            kprev, vprev = (kp_ref, vp_ref) if b == 0 else (k_ref, v_ref)
            for hp in range(H_A // 2):
                lanes = slice(hp * LANES, (hp + 1) * LANES)
                q2 = cat([q_ref[rr, lanes] for rr in rows])
                kk = cat([kprev[rr, lanes] for rr in prows] + [k_ref[rr, lanes] for rr in rows])
                vv = cat([vprev[rr, lanes] for rr in prows] + [v_ref[rr, lanes] for rr in rows])
                cur = tile(q2, kk, vv, b4_ref[hp], b == 0)
                prev = (cat([mst[rr, lanes] for rr in rows]), cat([lst[rr, lanes] for rr in rows]),
                        cat([acc[rr, lanes] for rr in rows]))
                mn, ln, an = merge(prev, cur)
                for i, rr in enumerate(rows):
                    part = slice(32 * i, 32 * (i + 1))
                    mst[rr, lanes] = mn[part]
                    lst[rr, lanes] = ln[part]
                    acc[rr, lanes] = an[part]
        return 0

    lax.fori_loop(0, 4, d4_body, 0)

    def d1_pair(ap, kprev, vprev, prev_ap, mask_prev):
        def tiles(a_):
            return [pl.ds(pl.multiple_of(BLK * r + 16 * a_, 16), 16) for r in range(NRES)]
        cur_t = tiles(ap)
        prev_t = tiles(prev_ap)

        def halves(ref, ts, lanes):
            xs = [ref[t, lanes].astype(F32) for t in ts]
            return cat([x[:8] for x in xs]).astype(BF16), cat([x[8:] for x in xs]).astype(BF16)

        for hp in range(H_A // 2):
            lanes = slice(hp * LANES, (hp + 1) * LANES)
            q_e, q_o = halves(q_ref, cur_t, lanes)
            k_e, k_o = halves(k_ref, cur_t, lanes)
            v_e, v_o = halves(v_ref, cur_t, lanes)
            _, k_p = halves(kprev, prev_t, lanes)
            _, v_p = halves(vprev, prev_t, lanes)
            cur_e = tile(q_e, cat([k_p, k_e]), cat([v_p, v_e]), b1_ref[hp], mask_prev)
            cur_o = tile(q_o, cat([k_e, k_o]), cat([v_e, v_o]), b1_ref[hp], False)
            ms = [mst[t, lanes] for t in cur_t]
            ls = [lst[t, lanes] for t in cur_t]
            ac = [acc[t, lanes] for t in cur_t]
            outs = []
            for half, cur in ((0, cur_e), (1, cur_o)):
                part = slice(8 * half, 8 * half + 8)
                prev = (cat([x[part] for x in ms]), cat([x[part] for x in ls]), cat([x[part] for x in ac]))
                _, ln, an = merge(prev, cur)
                outs.append(an / ln)
            for r, t in enumerate(cur_t):
                part = slice(8 * r, 8 * r + 8)
                o_ref[t, lanes] = cat([outs[0][part], outs[1][part]]).astype(o_ref.dtype)

    d1_pair(0, kp_ref, vp_ref, BLK // 16 - 1, True)

    def d1_body(ap, _):
        d1_pair(ap, k_ref, v_ref, ap - 1, False)
        return 0

    lax.fori_loop(1, BLK // 16, d1_body, 0)


def _dilated_attention(proj, biases):
    s = proj.shape[0]
    cur = lambda which: pl.BlockSpec((SPAN, D_A), lambda c: (c, which))
    prev = lambda which: pl.BlockSpec((SPAN, D_A), lambda c: (jnp.maximum(c - 1, 0), which))
    return pl.pallas_call(
        _attn_body,
        out_shape=jax.ShapeDtypeStruct((s, D_A), BF16),
        grid=(s // SPAN,),
        in_specs=[cur(0), cur(1), cur(2), prev(1), prev(2)] + [_full(b) for b in biases],
        out_specs=pl.BlockSpec((SPAN, D_A), lambda c: (c, 0)),
        scratch_shapes=[pltpu.VMEM((SPAN, D_A), F32)] * 3,
        compiler_params=_params(("arbitrary",)),
        name="attn",
    )(proj, proj, proj, proj, proj, *biases)


def _s5_operators(a_re, a_im, b_re, b_im, c_re, c_im, log_dt, d_skip):
    L = S5_CHUNK
    lam = lax.complex(a_re.astype(F32), a_im.astype(F32))
    dt = jnp.exp(log_dt.astype(F32))[:, None]
    a_bar = jnp.exp(lam * dt)
    b_bar = ((a_bar - 1.0) / lam)[..., None] * lax.complex(b_re.astype(F32), b_im.astype(F32))
    c = lax.complex(c_re.astype(F32), c_im.astype(F32))
    j = jnp.arange(L + 1, dtype=F32)
    log_a = lam * dt
    apow = jnp.exp(log_a[None] * j[:, None, None])
    kj = jnp.einsum('gdp,jgp,gpc->jgdc', c, apow[:L], b_bar).real
    s_idx = jnp.arange(L)[:, None]
    t_idx = jnp.arange(L)[None, :]
    lag = t_idx - s_idx
    m = jnp.where((lag >= 0)[:, :, None, None, None], kj[jnp.clip(lag, 0, L - 1)], 0.0)
    m = m.transpose(2, 0, 4, 1, 3)
    eye = jnp.eye(L)[:, None, :, None] * jnp.eye(S5_CH)[None, :, None, :]
    m = m + eye[None] * d_skip.astype(F32).reshape(S5_G, 1, S5_CH, 1, 1)
    m = m.reshape(S5_G, L * S5_CH, L * S5_CH)
    p = jnp.einsum('sgp,gpc->gscp', apow[:L][::-1], b_bar).reshape(S5_G, L * S5_CH, S5_P)
    ca = jnp.einsum('gdp,tgp->gptd', c, apow[1:L + 1]).reshape(S5_G, S5_P, L * S5_CH)
    a_l = apow[L]

    def pair_blocks(x):
        g, r, w = x.shape
        x = x.reshape(S5_PAIRS, 2, r, w)
        z = jnp.zeros_like(x[:, 0])
        top = jnp.concatenate([x[:, 0], z], axis=-1)
        bot = jnp.concatenate([z, x[:, 1]], axis=-1)
        return jnp.concatenate([top, bot], axis=1)

    p2 = jnp.concatenate([pair_blocks(p.real), pair_blocks(p.imag)], axis=-1)
    q2 = jnp.concatenate([pair_blocks(ca.real), pair_blocks(-ca.imag)], axis=1)
    a_lr = a_l.real.reshape(1, S5_G * S5_P)
    a_li = a_l.imag.reshape(1, S5_G * S5_P)
    return m.astype(BF16), p2.astype(BF16), q2.astype(BF16), a_lr, a_li


def _s5_body(w_ref, m_ref, p_ref, q_ref, ar_ref, ai_ref, o_ref, ere, eim, xre, xim, sre, sim):
    rows = w_ref.shape[0]
    gw = S5_CHUNK * S5_CH
    pw = 2 * gw

    @pl.when(pl.program_id(0) == 0)
    def _():
        sre[...] = jnp.zeros_like(sre)
        sim[...] = jnp.zeros_like(sim)

    for pr in range(S5_PAIRS):
        e = _dot(w_ref[:, pr * pw:(pr + 1) * pw], p_ref[pr])
        ere[:, pr * LANES:(pr + 1) * LANES] = e[:, :LANES]
        eim[:, pr * LANES:(pr + 1) * LANES] = e[:, LANES:]

    ar = ar_ref[...]
    ai = ai_ref[...]

    def step(n, carry):
        xr, xi = carry
        xre[pl.ds(n, 1), :] = xr
        xim[pl.ds(n, 1), :] = xi
        nr = ar * xr - ai * xi + ere[pl.ds(n, 1), :]
        ni = ar * xi + ai * xr + eim[pl.ds(n, 1), :]
        return nr, ni

    xr, xi = lax.fori_loop(0, rows, step, (sre[...], sim[...]))
    sre[...] = xr
    sim[...] = xi

    for pr in range(S5_PAIRS):
        xin = jnp.concatenate([xre[:, pr * LANES:(pr + 1) * LANES],
                               xim[:, pr * LANES:(pr + 1) * LANES]], axis=-1).astype(BF16)
        yc = _dot(xin, q_ref[pr])
        for half in range(2):
            g = 2 * pr + half
            cols = slice(g * gw, (g + 1) * gw)
            y = _dot(w_ref[:, cols], m_ref[g]) + yc[:, half * gw:(half + 1) * gw]
            o_ref[:, cols] = (0.5 * y * (1.0 + lax.erf(y * (2.0 ** -0.5)))).astype(BF16)


def _s5_core(u, ops):
    m, p2, q2, a_lr, a_li = ops
    s = u.shape[0]
    nsp = s // SPAN
    nc = s // S5_CHUNK
    wide = S5_G * S5_CHUNK * S5_CH
    w = u.reshape(nsp, NRES, BLK, S5_G, S5_CH).transpose(0, 2, 3, 1, 4).reshape(nc, wide)
    rows = min(S5_ROWS, nc)
    nstate = S5_G * S5_P
    y = pl.pallas_call(
        _s5_body,
        out_shape=jax.ShapeDtypeStruct((nc, wide), BF16),
        grid=(nc // rows,),
        in_specs=[pl.BlockSpec((rows, wide), lambda i: (i, 0)),
                  _full(m), _full(p2), _full(q2), _full(a_lr), _full(a_li)],
        out_specs=pl.BlockSpec((rows, wide), lambda i: (i, 0)),
        scratch_shapes=[pltpu.VMEM((rows, nstate), F32)] * 4 + [pltpu.VMEM((1, nstate), F32)] * 2,
        compiler_params=_params(("arbitrary",)),
        name="s5",
    )(w, m, p2, q2, a_lr, a_li)
    return y.reshape(nsp, BLK, S5_G, NRES, S5_CH).transpose(0, 3, 1, 2, 4).reshape(s, D_B)


def _split_bf16(x):
    hi = x.astype(BF16)
    lo = (x - hi.astype(F32)).astype(BF16)
    return hi, lo


def _mid_body(h_ref, ya_ref, yb_ref, wglu_ref, bglu_ref, ga_ref, gb_ref, wout_ref, gx_ref, wq_ref,
              k_ref, v_ref, wo_ref, gm_ref, wr_ref, br_ref, h_out, xn_out, idx_out, gate_out):
    yb = yb_ref[...]
    gate = jax.nn.sigmoid(_dot(yb, wglu_ref[...]) + bglu_ref[...])
    yb2 = yb.astype(F32) * gate
    na = _rms(ya_ref[...].astype(F32), ga_ref[...]).astype(BF16)
    nb = _rms(yb2, gb_ref[...]).astype(BF16)
    h1 = h_ref[...] + _dot(na, wout_ref[0:D_A, :]) + _dot(nb, wout_ref[D_A:D_MODEL, :])
    q = _dot(_rms(h1, gx_ref[...]).astype(BF16), wq_ref[...]).astype(BF16)
    heads = []
    for hd in range(H_X):
        lanes = slice(hd * HD_X, (hd + 1) * HD_X)
        s = _dot_nt(q[:, lanes], k_ref[:, lanes])
        e = jnp.exp(s - jnp.max(s, axis=-1, keepdims=True))
        p = e / jnp.sum(e, axis=-1, keepdims=True)
        heads.append(_dot(p.astype(BF16), v_ref[:, lanes]))
    o = jnp.concatenate(heads, axis=-1).astype(BF16)
    h2 = h1 + _dot(o, wo_ref[...])
    h_out[...] = h2
    xn = _rms(h2, gm_ref[...])
    xn_out[...] = _pack_rows(xn)
    x_hi, x_lo = _split_bf16(xn)
    logits = _dot(x_hi, wr_ref[0]) + _dot(x_lo, wr_ref[0]) + _dot(x_hi, wr_ref[1]) + br_ref[...]
    lane = lax.broadcasted_iota(jnp.int32, logits.shape, 1)
    logits = jnp.where(lane < N_EXPERTS, logits, -jnp.inf)
    vals, idxs = [], []
    for _ in range(TOP_K):
        mx = jnp.max(logits, axis=-1, keepdims=True)
        ix = jnp.min(jnp.where(logits == mx, lane, LANES), axis=-1, keepdims=True)
        vals.append(mx)
        idxs.append(ix)
        logits = jnp.where(lane == ix, -jnp.inf, logits)
    es = [jnp.exp(v - vals[0]) for v in vals]
    den = es[0] + es[1] + es[2] + es[3]
    idx_t = jnp.full(lane.shape, N_EXPERTS, jnp.int32)
    gate_t = jnp.zeros(lane.shape, F32)
    for k in range(TOP_K):
        idx_t = jnp.where(lane == k, idxs[k], idx_t)
        gate_t = jnp.where(lane == k, es[k] / den, gate_t)
    idx_out[...] = idx_t
    gate_out[...] = gate_t


def _mid(h, ya, yb, wglu, bglu, ga, gb, wout, gx, wq, kmem, vmem, wo, gm, wr2, br):
    s = h.shape[0]
    tm = min(ROW_TILE // 2, s)
    row = lambda w: pl.BlockSpec((tm, w), lambda i: (i, 0))
    consts = [wglu, bglu, ga, gb, wout, gx, wq, kmem, vmem, wo, gm, wr2, br]
    return pl.pallas_call(
        _mid_body,
        out_shape=[jax.ShapeDtypeStruct((s, D_MODEL), F32), jax.ShapeDtypeStruct((s, D_MODEL // 2), jnp.int32),
                   jax.ShapeDtypeStruct((s, LANES), jnp.int32), jax.ShapeDtypeStruct((s, LANES), F32)],
        grid=(s // tm,),
        in_specs=[row(D_MODEL), row(D_A), row(D_B)] + [_full(a) for a in consts],
        out_specs=[row(D_MODEL), row(D_MODEL // 2), row(LANES), row(LANES)],
        compiler_params=_params(("parallel",)),
        name="mid",
    )(h, ya, yb, *consts)


def _memkv_body(mem_ref, g_ref, w_ref, o_ref):
    o_ref[...] = _dot(_rms(mem_ref[...], g_ref[...]).astype(BF16), w_ref[...]).astype(BF16)


def _memkv(mem, g, w):
    n, d = mem.shape
    return pl.pallas_call(
        _memkv_body,
        out_shape=jax.ShapeDtypeStruct((n, w.shape[1]), BF16),
        compiler_params=pltpu.CompilerParams(vmem_limit_bytes=VMEM_LIMIT),
        name="memkv",
    )(mem, g.reshape(1, d), w)


def _rank_body(idx_ref, rank_ref, cnt_ref, carry):
    @pl.when(pl.program_id(0) == 0)
    def _():
        carry[...] = jnp.zeros_like(carry)

    tm = idx_ref.shape[0]
    lane = lax.broadcasted_iota(jnp.int32, (tm, LANES), 1)
    idx = idx_ref[...]
    hits = [lane == idx[:, k:k + 1] for k in range(TOP_K)]
    onehot = jnp.zeros((tm, LANES), F32)
    for hit in hits:
        onehot = onehot + jnp.where(hit, 1.0, 0.0)
    ri = lax.broadcasted_iota(jnp.int32, (tm, tm), 0)
    ci = lax.broadcasted_iota(jnp.int32, (tm, tm), 1)
    tri = jnp.where(ri >= ci, 1.0, 0.0).astype(BF16)
    inclusive = _dot(tri, onehot.astype(BF16))
    before = carry[...] + inclusive - onehot
    rank = jnp.zeros((tm, LANES), jnp.int32)
    for k, hit in enumerate(hits):
        rk = jnp.sum(jnp.where(hit, before, 0.0), axis=-1, keepdims=True)
        rank = jnp.where(lane == k, rk.astype(jnp.int32), rank)
    rank_ref[...] = rank
    total = carry[...] + inclusive[tm - 1:tm, :]
    carry[...] = total
    cnt_ref[...] = total.astype(jnp.int32)


def _rank(idx):
    t = idx.shape[0]
    tm = min(ROW_TILE, t)
    return pl.pallas_call(
        _rank_body,
        out_shape=[jax.ShapeDtypeStruct((t, LANES), jnp.int32), jax.ShapeDtypeStruct((1, LANES), jnp.int32)],
        grid=(t // tm,),
        in_specs=[pl.BlockSpec((tm, LANES), lambda i: (i, 0))],
        out_specs=[pl.BlockSpec((tm, LANES), lambda i: (i, 0)), pl.BlockSpec((1, LANES), lambda i: (0, 0))],
        scratch_shapes=[pltpu.VMEM((1, LANES), F32)],
        compiler_params=_params(("arbitrary",)),
        name="rank",
    )(idx)


def _route(idx, tm):
    t = idx.shape[0]
    tk = t * TOP_K
    rank, cnt = _rank(idx)
    counts = cnt[0, :N_EXPERTS]
    padded = (counts + tm - 1) // tm * tm
    pend = jnp.cumsum(padded)
    pstart = pend - padded
    n_rows = tk + N_EXPERTS * tm
    n_tiles = n_rows // tm
    experts = jnp.arange(N_EXPERTS, dtype=jnp.int32)
    tile_first = jnp.arange(n_tiles, dtype=jnp.int32) * tm
    last_used = jnp.max(jnp.where(padded > 0, experts, 0))
    tile_e = jnp.minimum(jnp.sum(tile_first[:, None] >= pend[None, :], axis=1), last_used).astype(jnp.int32)
    tile_valid = (tile_first < pend[-1]).astype(jnp.int32)
    group = jnp.cumsum(jnp.concatenate([jnp.zeros((1,), jnp.int32),
                                        (tile_e[1:] != tile_e[:-1]).astype(jnp.int32)]))
    tile_slot = (group % 2).astype(jnp.int32)
    later = (experts[None, :] > experts[:, None]) & (padded > 0)[None, :]
    next_e = jnp.min(jnp.where(later, experts[None, :], N_EXPERTS), axis=1)
    next_e = jnp.where(next_e < N_EXPERTS, next_e, -1).astype(jnp.int32)
    tile_next = jnp.sum(jnp.where(tile_e[:, None] == experts[None, :], next_e[None, :], 0), axis=1).astype(jnp.int32)
    top = idx[:, :TOP_K]
    base = jnp.sum(jnp.where(top[:, :, None] == experts, pstart, 0), axis=-1)
    dest = (rank[:, :TOP_K] + base).T.reshape(tk)
    tbits = (t - 1).bit_length()
    fill_e = jnp.repeat(experts, tm)
    fill_j = jnp.tile(jnp.arange(tm, dtype=jnp.int32), N_EXPERTS)
    fill_key = jnp.where(fill_j < jnp.repeat(padded - counts, tm), 2 * fill_e + 1, 2 * N_EXPERTS)
    fill_tok = jnp.arange(N_EXPERTS * tm, dtype=jnp.int32) % t
    real_tok = jnp.arange(tk, dtype=jnp.int32) // TOP_K
    keys = jnp.concatenate([(2 * top.reshape(tk) << tbits) | real_tok, (fill_key << tbits) | fill_tok])
    tok = jnp.sort(keys) & ((1 << tbits) - 1)
    return dest, tok, tile_e, tile_valid, tile_slot, tile_next


def _gather_rows(table, idx):
    n, d = table.shape
    b = idx.shape[0]
    workers = SC_CORES * SC_SUBCORES
    per_worker = b // workers
    chunks = per_worker // SC_GATHER_ROWS
    mesh = plsc.VectorSubcoreMesh(core_axis_name="c", subcore_axis_name="s",
                                  num_cores=SC_CORES, num_subcores=SC_SUBCORES)

    @functools.partial(
        pl.kernel, mesh=mesh,
        out_type=jax.ShapeDtypeStruct((b, d), table.dtype),
        scratch_types=[pltpu.VMEM((SC_GATHER_ROWS,), jnp.int32),
                       pltpu.VMEM((SC_GATHER_ROWS, d), table.dtype),
                       pltpu.SemaphoreType.DMA],
    )
    def gather(table_hbm, idx_hbm, out_hbm, idx_v, rows_v, sem):
        base = (lax.axis_index("s") * SC_CORES + lax.axis_index("c")) * per_worker

        @pl.loop(0, chunks)
        def _(c):
            off = pl.multiple_of(base + c * SC_GATHER_ROWS, SC_GATHER_ROWS)
            pltpu.sync_copy(idx_hbm.at[pl.ds(off, SC_GATHER_ROWS)], idx_v)
            pltpu.async_copy(table_hbm.at[idx_v], rows_v, sem).wait()
            pltpu.sync_copy(rows_v, out_hbm.at[pl.ds(off, SC_GATHER_ROWS)])

    return gather(table, idx)


def _moe_body(te_ref, tv_ref, sl_ref, nx_ref, x_ref, w1_hbm, b1_ref, w2_hbm, b2_ref, o_ref,
              w1f, w2f, w1b, w2b, sem, *, layer):
    i = pl.program_id(0)
    e = te_ref[i]
    slot = sl_ref[i]
    new_expert = (i == 0) | (e != te_ref[jnp.maximum(i - 1, 0)])

    def weight_copies(expert, s):
        rows1 = w1f.shape[1] // MOE_W1_PARTS
        rows2 = w2f.shape[1] // MOE_W2_PARTS
        c1 = [pltpu.make_async_copy(w1_hbm.at[layer, expert, pl.ds(q * rows1, rows1)],
                                    w1f.at[s, pl.ds(q * rows1, rows1)], sem.at[s, q])
              for q in range(MOE_W1_PARTS)]
        c2 = [pltpu.make_async_copy(w2_hbm.at[layer, expert, pl.ds(q * rows2, rows2)],
                                    w2f.at[s, pl.ds(q * rows2, rows2)], sem.at[s, MOE_W1_PARTS + q])
              for q in range(MOE_W2_PARTS)]
        return c1 + c2

    @pl.when(i == 0)
    def _():
        for c in weight_copies(e, slot):
            c.start()

    @pl.when(new_expert)
    def _():
        for c in weight_copies(e, slot):
            c.wait()
        nxt = nx_ref[i]

        @pl.when(nxt >= 0)
        def _():
            for c in weight_copies(nxt, 1 - slot):
                c.start()

        w1b[...] = w1f[slot].astype(BF16)
        w2b[...] = w2f[slot].astype(BF16)

    @pl.when(tv_ref[i] > 0)
    def _():
        lo, hi = _unpack_rows(x_ref[...])
        x = jnp.concatenate([lo, hi], axis=-1).astype(BF16)
        hb = _dot(x, w1b[...]) + b1_ref[0]
        x_glu = jnp.minimum(hb[:, :D_FF], SWIGLU_LIMIT)
        x_lin = jnp.clip(hb[:, D_FF:], -SWIGLU_LIMIT, SWIGLU_LIMIT)
        act = x_glu * jax.nn.sigmoid(SWIGLU_ALPHA * x_glu) * (x_lin + 1.0)
        o_ref[...] = _pack_rows(_dot(act.astype(BF16), w2b[...]) + b2_ref[0])

    @pl.when(tv_ref[i] == 0)
    def _():
        o_ref[...] = jnp.zeros_like(o_ref)


def _moe_experts(xs, tile_e, tile_valid, tile_slot, tile_next, w1, b1, w2, b2, layer):
    n_rows = xs.shape[0]
    tm = MOE_TILE
    nl, ne, d, ff2 = w1.shape
    bias_map = lambda i, te, tv, sl, nx: (layer, te[i], 0, 0)
    grid_spec = pltpu.PrefetchScalarGridSpec(
        num_scalar_prefetch=4,
        grid=(n_rows // tm,),
        in_specs=[pl.BlockSpec((tm, d // 2), lambda i, *_: (i, 0)),
                  pl.BlockSpec(memory_space=pl.ANY),
                  pl.BlockSpec((None, 1, 1, ff2), bias_map),
                  pl.BlockSpec(memory_space=pl.ANY),
                  pl.BlockSpec((None, 1, 1, d), bias_map)],
        out_specs=pl.BlockSpec((tm, d // 2), lambda i, *_: (i, 0)),
        scratch_shapes=[pltpu.VMEM((2, d, ff2), F32), pltpu.VMEM((2, ff2 // 2, d), F32),
                        pltpu.VMEM((d, ff2), BF16), pltpu.VMEM((ff2 // 2, d), BF16),
                        pltpu.SemaphoreType.DMA((2, MOE_W1_PARTS + MOE_W2_PARTS))],
    )
    return pl.pallas_call(
        functools.partial(_moe_body, layer=layer),
        out_shape=jax.ShapeDtypeStruct((n_rows, d // 2), jnp.int32),
        grid_spec=grid_spec,
        compiler_params=_params(("arbitrary",)),
        name="moe",
    )(tile_e, tile_valid, tile_slot, tile_next, xs, w1, b1.reshape(nl, ne, 1, ff2), w2, b2.reshape(nl, ne, 1, d))


def _combine_body(h_ref, y0_ref, y1_ref, y2_ref, y3_ref, gate_ref, g_ref, o_ref, *, final):
    gates = gate_ref[...]
    lo = jnp.zeros(y0_ref.shape, F32)
    hi = jnp.zeros(y0_ref.shape, F32)
    for k, y_ref in enumerate((y0_ref, y1_ref, y2_ref, y3_ref)):
        yl, yh = _unpack_rows(y_ref[...])
        lo = lo + yl * gates[:, k:k + 1]
        hi = hi + yh * gates[:, k:k + 1]
    h = h_ref[...] + jnp.concatenate([lo, hi], axis=-1)
    o_ref[...] = _rms(h, g_ref[...]) if final else h


def _combine(h, y, gates, g, *, final):
    s, d = h.shape
    tm = min(ROW_TILE // 2, s)
    nt = s // tm
    return pl.pallas_call(
        functools.partial(_combine_body, final=final),
        out_shape=jax.ShapeDtypeStruct((s, d), F32),
        grid=(nt,),
        in_specs=[pl.BlockSpec((tm, d), lambda i: (i, 0))]
                 + [pl.BlockSpec((tm, d // 2), lambda i, k=k: (k * nt + i, 0)) for k in range(TOP_K)]
                 + [pl.BlockSpec((tm, LANES), lambda i: (i, 0)), pl.BlockSpec((1, d), lambda i: (0, 0))],
        out_specs=pl.BlockSpec((tm, d), lambda i: (i, 0)),
        compiler_params=_params(("parallel",)),
        name="combine",
    )(h, y, y, y, y, gates, g.reshape(1, d))


def _layer(h, mem, biases, p, l, g_final):
    row = lambda a: a.reshape(1, -1).astype(F32)
    q_scale = jnp.concatenate([jnp.full((D_A,), HD_A ** -0.5, F32), jnp.ones((D_IN - D_A,), F32)])
    w_in = (p["w_in"][l] * q_scale).astype(BF16)
    proj = _inproj(h, p["norm_mix"][l], w_in)
    ya = _dilated_attention(proj, biases)
    ops = _s5_operators(p["s5_a_re"][l], p["s5_a_im"][l], p["s5_b_re"][l], p["s5_b_im"][l],
                        p["s5_c_re"][l], p["s5_c_im"][l], p["s5_log_dt"][l], p["s5_d"][l])
    yb = _s5_core(proj[:, 3 * D_A:], ops)
    kv = _memkv(mem, p["norm_mem"][l], p["w_xkv"][l].astype(BF16))
    wr = jnp.pad(p["w_router"][l].astype(F32), ((0, 0), (0, LANES - N_EXPERTS)))
    wr_hi = wr.astype(BF16)
    wr2 = jnp.stack([wr_hi, (wr - wr_hi.astype(F32)).astype(BF16)])
    br = jnp.pad(p["b_router"][l].astype(F32), (0, LANES - N_EXPERTS)).reshape(1, LANES)
    h2, xn, idx, gates = _mid(
        h, ya, yb, p["w_glu"][l].astype(BF16), row(p["b_glu"][l]), row(p["g_out_attn"][l]),
        row(p["g_out_ssm"][l]), p["w_out"][l].astype(BF16), row(p["norm_xattn"][l]),
        (p["w_xq"][l] * (HD_X ** -0.5)).astype(BF16), kv[:, :D_X], kv[:, D_X:],
        p["w_xo"][l].astype(BF16), row(p["norm_moe"][l]), wr2, br)
    dest, tok, tile_e, tile_valid, tile_slot, tile_next = _route(idx, MOE_TILE)
    out = _moe_experts(_gather_rows(xn, tok), tile_e, tile_valid, tile_slot, tile_next,
                       p["w1"], p["b1"], p["w2"], p["b2"], l)
    return _combine(h2, _gather_rows(out, dest), gates, g_final, final=l == DEPTH - 1)


def kernel(x, mem, rel_bias, norm_mix, w_in, s5_a_re, s5_a_im, s5_b_re, s5_b_im, s5_c_re, s5_c_im, s5_log_dt, s5_d, w_glu, b_glu, g_out_attn, g_out_ssm, w_out, norm_xattn, norm_mem, w_xq, w_xkv, w_xo, norm_moe, w_router, b_router, w1, b1, w2, b2, norm_final):
    p = dict(norm_mix=norm_mix, w_in=w_in, s5_a_re=s5_a_re, s5_a_im=s5_a_im, s5_b_re=s5_b_re,
             s5_b_im=s5_b_im, s5_c_re=s5_c_re, s5_c_im=s5_c_im, s5_log_dt=s5_log_dt, s5_d=s5_d,
             w_glu=w_glu, b_glu=b_glu, g_out_attn=g_out_attn, g_out_ssm=g_out_ssm, w_out=w_out,
             norm_xattn=norm_xattn, norm_mem=norm_mem, w_xq=w_xq, w_xkv=w_xkv, w_xo=w_xo,
             norm_moe=norm_moe, w_router=w_router, b_router=b_router, w1=w1, b1=b1, w2=w2, b2=b2)
    biases = [_attn_bias(rel_bias, window, dil, perm)
              for (window, dil), perm in zip(WIN_DIL, (_PERM_D1, _PERM_D4, _PERM_D16))]
    outs = []
    for b in range(x.shape[0]):
        h = _to_span_layout(x[b])
        for l in range(DEPTH):
            h = _layer(h, mem[b], biases, p, l, norm_final)
        outs.append(_from_span_layout(h))
    return jnp.stack(outs)
```

```python
import functools
import math

import jax
import jax.numpy as jnp
from jax import lax
from jax.experimental import pallas as pl
from jax.experimental.pallas import tpu as pltpu
from jax.experimental.pallas import tpu_sc as plsc

F32 = jnp.float32
BF16 = jnp.bfloat16

D_MODEL = 1024
DEPTH = 2
EPS = 1e-5
NEG_INF = -1e30
LOG2E = math.log2(math.e)
H_A = 8
HD_A = 64
D_A = H_A * HD_A
WIN_DIL = ((128, 1), (512, 4), (2048, 16))
BLK = 128
D_B = D_MODEL - D_A
S5_CH = 16
S5_G = D_B // S5_CH
S5_P = 64
D_IN = 3 * D_A + D_B
NUM_BUCKETS = 32
REL_MAX_DIST = 2048
H_X = 4
HD_X = 128
D_X = H_X * HD_X
N_EXPERTS = 32
TOP_K = 4
D_FF = D_MODEL
SWIGLU_ALPHA = 1.702
SWIGLU_LIMIT = 7.0

LANES = 128
NRES = WIN_DIL[-1][1]
SPAN = NRES * BLK
S5_CHUNK = NRES
S5_PAIRS = S5_G // 2
VMEM_LIMIT = 56 * 1024 * 1024

SC_CORES = 2
SC_SUBCORES = 16
SC_GATHER_ROWS = 64

ROW_TILE = 512
S5_ROWS = 128
MID_CHAINS = 2
MOE_TILE = 512
MOE_W1_PARTS = 4
MOE_W2_PARTS = 2


def _params(sem):
    return pltpu.CompilerParams(dimension_semantics=sem, vmem_limit_bytes=VMEM_LIMIT)


def _rms(x, g):
    return x * lax.rsqrt(jnp.mean(x * x, axis=-1, keepdims=True) + EPS) * g


def _dot(a, b):
    return jnp.dot(a, b, preferred_element_type=F32)


def _dot_nt(a, b):
    return lax.dot_general(a, b, (((1,), (1,)), ((), ())), preferred_element_type=F32)


def _full(a):
    return pl.BlockSpec(a.shape, lambda *_: (0,) * a.ndim)


def _pack_rows(x):
    c = x.shape[1] // 2
    lo = lax.bitcast_convert_type(x[:, :c].astype(BF16).astype(F32), jnp.uint32)
    hi = lax.bitcast_convert_type(x[:, c:].astype(BF16).astype(F32), jnp.uint32)
    return lax.bitcast_convert_type(lax.shift_right_logical(lo, jnp.uint32(16)) | hi, jnp.int32)


def _unpack_rows(p):
    u = lax.bitcast_convert_type(p, jnp.uint32)
    lo = lax.bitcast_convert_type(lax.shift_left(u, jnp.uint32(16)), F32)
    hi = lax.bitcast_convert_type(u & jnp.uint32(0xFFFF0000), F32)
    return lo, hi


def _to_span_layout(x):
    s = x.shape[0]
    return x.reshape(s // SPAN, BLK, NRES, -1).transpose(0, 2, 1, 3).reshape(s, -1)


def _from_span_layout(x):
    s = x.shape[0]
    return x.reshape(s // SPAN, NRES, BLK, -1).transpose(0, 2, 1, 3).reshape(s, -1)


def _inproj_body(h_ref, g_ref, w_ref, o_ref):
    xn = _rms(h_ref[...], g_ref[...]).astype(BF16)
    o_ref[...] = _dot(xn, w_ref[...]).astype(BF16)


def _inproj(h, g, w):
    s, d = h.shape
    n = w.shape[1]
    tm = min(ROW_TILE, s)
    return pl.pallas_call(
        _inproj_body,
        out_shape=jax.ShapeDtypeStruct((s, n), BF16),
        grid=(s // tm,),
        in_specs=[pl.BlockSpec((tm, d), lambda i: (i, 0)),
                  pl.BlockSpec((1, d), lambda i: (0, 0)),
                  pl.BlockSpec((d, n), lambda i: (0, 0))],
        out_specs=pl.BlockSpec((tm, n), lambda i: (i, 0)),
        compiler_params=_params(("parallel",)),
        name="inproj",
    )(h, g.reshape(1, d), w)


def _t5_bucket(n):
    max_exact = NUM_BUCKETS // 2
    nf = jnp.maximum(n, 1).astype(F32)
    large = max_exact + (jnp.log(nf / max_exact) / math.log(REL_MAX_DIST / max_exact)
                         * (NUM_BUCKETS - max_exact)).astype(jnp.int32)
    large = jnp.minimum(large, NUM_BUCKETS - 1)
    return jnp.where(n < max_exact, n, large)


def _attn_bias(rel_bias, window, dil, perm):
    steps = window // dil
    perm = jnp.asarray(perm, jnp.int32)
    qi = perm[:, None]
    ki = jnp.concatenate([perm, BLK + perm])[None, :]
    dist = BLK + qi - ki
    in_win = (dist >= 0) & (dist <= steps)
    bucket = _t5_bucket(jnp.clip(dist, 0, steps) * dil)
    onehot = (bucket[:, :, None] == jnp.arange(NUM_BUCKETS, dtype=jnp.int32)).astype(F32)
    bias = jnp.einsum('qkb,bh->hqk', onehot, rel_bias.astype(F32), precision=lax.Precision.HIGHEST)
    bias = jnp.where(in_win[None], bias * LOG2E, NEG_INF)
    return bias.reshape(H_A // 2, 2 * BLK, 2 * BLK)


_PERM_D1 = [NRES * jl + r for r in range(NRES) for jl in range(BLK // NRES)]
_PERM_D4 = [4 * jl + i for i in range(4) for jl in range(BLK // 4)]
_PERM_D16 = list(range(BLK))


def _attn_body(q_ref, k_ref, v_ref, kp_ref, vp_ref, b1_ref, b4_ref, b16_ref, o_ref, acc, mst, lst):
    has_prev = pl.program_id(0) > 0
    lane = lax.broadcasted_iota(jnp.int32, (1, LANES), 1)
    lo = lane < HD_A
    mlo = lo.astype(BF16)
    mhi = (~lo).astype(BF16)
    col = lax.broadcasted_iota(jnp.int32, (2 * BLK, 2 * BLK), 1)
    ones = jnp.ones((2 * BLK, LANES), BF16)

    def tile(q2, kk, vv, bias, mask_prev):
        qs = jnp.concatenate([q2 * mlo, q2 * mhi], axis=0)
        s = _dot_nt(qs, kk) + bias
        if mask_prev:
            s = jnp.where(jnp.logical_or(has_prev, col >= BLK), s, NEG_INF)
        m = jnp.max(s, axis=-1, keepdims=True)
        e = jnp.exp2((s - m).astype(BF16))
        oa = _dot(e, jnp.concatenate([vv, ones], axis=1))
        o = oa[:, :LANES]
        l = oa[:, LANES:]
        return (jnp.where(lo, m[:BLK], m[BLK:]), jnp.where(lo, l[:BLK], l[BLK:]),
                jnp.where(lo, o[:BLK], o[BLK:]))

    def merge(prev, cur):
        mp, lp, ap = prev
        mc, lc, ac = cur
        mn = jnp.maximum(mp, mc)
        a = jnp.exp2(mp - mn)
        b = jnp.exp2(mc - mn)
        return mn, a * lp + b * lc, a * ap + b * ac

    def cat(xs):
        return jnp.concatenate(xs, axis=0)

    def d16_body(r, _):
        rows = pl.ds(pl.multiple_of(r * BLK, BLK), BLK)
        for hp in range(H_A // 2):
            lanes = slice(hp * LANES, (hp + 1) * LANES)
            kk = cat([kp_ref[rows, lanes], k_ref[rows, lanes]])
            vv = cat([vp_ref[rows, lanes], v_ref[rows, lanes]])
            m2, l2, o2 = tile(q_ref[rows, lanes], kk, vv, b16_ref[hp], True)
            mst[rows, lanes] = m2
            lst[rows, lanes] = l2
            acc[rows, lanes] = o2
        return 0

    lax.fori_loop(0, NRES, d16_body, 0)

    def d4_body(r4, _):
        for b in range(4):
            def chunk_rows(bb):
                return [pl.ds(pl.multiple_of(4 * BLK * i + BLK * r4 + 32 * bb, 32), 32) for i in range(4)]
            rows = chunk_rows(b)
            prows = chunk_rows(3 if b == 0 else b - 1)
            kprev, vprev = (kp_ref, vp_ref) if b == 0 else (k_ref, v_ref)
            for hp in range(H_A // 2):
                lanes = slice(hp * LANES, (hp + 1) * LANES)
                q2 = cat([q_ref[rr, lanes] for rr in rows])
                kk = cat([kprev[rr, lanes] for rr in prows] + [k_ref[rr, lanes] for rr in rows])
                vv = cat([vprev[rr, lanes] for rr in prows] + [v_ref[rr, lanes] for rr in rows])
                cur = tile(q2, kk, vv, b4_ref[hp], b == 0)
                prev = (cat([mst[rr, lanes] for rr in rows]), cat([lst[rr, lanes] for rr in rows]),
                        cat([acc[rr, lanes] for rr in rows]))
                mn, ln, an = merge(prev, cur)
                for i, rr in enumerate(rows):
                    part = slice(32 * i, 32 * (i + 1))
                    mst[rr, lanes] = mn[part]
                    lst[rr, lanes] = ln[part]
                    acc[rr, lanes] = an[part]
        return 0

    lax.fori_loop(0, 4, d4_body, 0)

    def d1_pair(ap, kprev, vprev, prev_ap, mask_prev):
        def tiles(a_):
            return [pl.ds(pl.multiple_of(BLK * r + 16 * a_, 16), 16) for r in range(NRES)]
        cur_t = tiles(ap)
        prev_t = tiles(prev_ap)

        def halves(ref, ts, lanes):
            xs = [ref[t, lanes].astype(F32) for t in ts]
            return cat([x[:8] for x in xs]).astype(BF16), cat([x[8:] for x in xs]).astype(BF16)

        for hp in range(H_A // 2):
            lanes = slice(hp * LANES, (hp + 1) * LANES)
            q_e, q_o = halves(q_ref, cur_t, lanes)
            k_e, k_o = halves(k_ref, cur_t, lanes)
            v_e, v_o = halves(v_ref, cur_t, lanes)
            _, k_p = halves(kprev, prev_t, lanes)
            _, v_p = halves(vprev, prev_t, lanes)
            cur_e = tile(q_e, cat([k_p, k_e]), cat([v_p, v_e]), b1_ref[hp], mask_prev)
            cur_o = tile(q_o, cat([k_e, k_o]), cat([v_e, v_o]), b1_ref[hp], False)
            ms = [mst[t, lanes] for t in cur_t]
            ls = [lst[t, lanes] for t in cur_t]
            ac = [acc[t, lanes] for t in cur_t]
            outs = []
            for half, cur in ((0, cur_e), (1, cur_o)):
                part = slice(8 * half, 8 * half + 8)
                prev = (cat([x[part] for x in ms]), cat([x[part] for x in ls]), cat([x[part] for x in ac]))
                _, ln, an = merge(prev, cur)
                outs.append(an / ln)
            for r, t in enumerate(cur_t):
                part = slice(8 * r, 8 * r + 8)
                o_ref[t, lanes] = cat([outs[0][part], outs[1][part]]).astype(o_ref.dtype)

    d1_pair(0, kp_ref, vp_ref, BLK // 16 - 1, True)

    def d1_body(ap, _):
        d1_pair(ap, k_ref, v_ref, ap - 1, False)
        return 0

    lax.fori_loop(1, BLK // 16, d1_body, 0)


def _dilated_attention(proj, biases):
    s = proj.shape[0]
    cur = lambda which: pl.BlockSpec((SPAN, D_A), lambda c: (c, which))
    prev = lambda which: pl.BlockSpec((SPAN, D_A), lambda c: (jnp.maximum(c - 1, 0), which))
    return pl.pallas_call(
        _attn_body,
        out_shape=jax.ShapeDtypeStruct((s, D_A), BF16),
        grid=(s // SPAN,),
        in_specs=[cur(0), cur(1), cur(2), prev(1), prev(2)] + [_full(b) for b in biases],
        out_specs=pl.BlockSpec((SPAN, D_A), lambda c: (c, 0)),
        scratch_shapes=[pltpu.VMEM((SPAN, D_A), F32)] * 3,
        compiler_params=_params(("arbitrary",)),
        name="attn",
    )(proj, proj, proj, proj, proj, *biases)


def _s5_operators(a_re, a_im, b_re, b_im, c_re, c_im, log_dt, d_skip):
    L = S5_CHUNK
    lam = lax.complex(a_re.astype(F32), a_im.astype(F32))
    dt = jnp.exp(log_dt.astype(F32))[:, None]
    a_bar = jnp.exp(lam * dt)
    b_bar = ((a_bar - 1.0) / lam)[..., None] * lax.complex(b_re.astype(F32), b_im.astype(F32))
    c = lax.complex(c_re.astype(F32), c_im.astype(F32))
    j = jnp.arange(L + 1, dtype=F32)
    log_a = lam * dt
    apow = jnp.exp(log_a[None] * j[:, None, None])
    kj = jnp.einsum('gdp,jgp,gpc->jgdc', c, apow[:L], b_bar).real
    s_idx = jnp.arange(L)[:, None]
    t_idx = jnp.arange(L)[None, :]
    lag = t_idx - s_idx
    m = jnp.where((lag >= 0)[:, :, None, None, None], kj[jnp.clip(lag, 0, L - 1)], 0.0)
    m = m.transpose(2, 0, 4, 1, 3)
    eye = jnp.eye(L)[:, None, :, None] * jnp.eye(S5_CH)[None, :, None, :]
    m = m + eye[None] * d_skip.astype(F32).reshape(S5_G, 1, S5_CH, 1, 1)
    m = m.reshape(S5_G, L * S5_CH, L * S5_CH)
    p = jnp.einsum('sgp,gpc->gscp', apow[:L][::-1], b_bar).reshape(S5_G, L * S5_CH, S5_P)
    ca = jnp.einsum('gdp,tgp->gptd', c, apow[1:L + 1]).reshape(S5_G, S5_P, L * S5_CH)
    a_l = apow[L]

    def pair_blocks(x):
        g, r, w = x.shape
        x = x.reshape(S5_PAIRS, 2, r, w)
        z = jnp.zeros_like(x[:, 0])
        top = jnp.concatenate([x[:, 0], z], axis=-1)
        bot = jnp.concatenate([z, x[:, 1]], axis=-1)
        return jnp.concatenate([top, bot], axis=1)

    p2 = jnp.concatenate([pair_blocks(p.real), pair_blocks(p.imag)], axis=-1)
    q2 = jnp.concatenate([pair_blocks(ca.real), pair_blocks(-ca.imag)], axis=1)
    a_lr = a_l.real.reshape(1, S5_G * S5_P)
    a_li = a_l.imag.reshape(1, S5_G * S5_P)
    return m.astype(BF16), p2.astype(BF16), q2.astype(BF16), a_lr, a_li


def _s5_body(w_ref, m_ref, p_ref, q_ref, ar_ref, ai_ref, o_ref, ere, eim, xre, xim, sre, sim):
    rows = w_ref.shape[0]
    gw = S5_CHUNK * S5_CH
    pw = 2 * gw

    @pl.when(pl.program_id(0) == 0)
    def _():
        sre[...] = jnp.zeros_like(sre)
        sim[...] = jnp.zeros_like(sim)

    for pr in range(S5_PAIRS):
        e = _dot(w_ref[:, pr * pw:(pr + 1) * pw], p_ref[pr])
        ere[:, pr * LANES:(pr + 1) * LANES] = e[:, :LANES]
        eim[:, pr * LANES:(pr + 1) * LANES] = e[:, LANES:]

    ar = ar_ref[...]
    ai = ai_ref[...]

    def step(n, carry):
        xr, xi = carry
        xre[pl.ds(n, 1), :] = xr
        xim[pl.ds(n, 1), :] = xi
        nr = ar * xr - ai * xi + ere[pl.ds(n, 1), :]
        ni = ar * xi + ai * xr + eim[pl.ds(n, 1), :]
        return nr, ni

    xr, xi = lax.fori_loop(0, rows, step, (sre[...], sim[...]))
    sre[...] = xr
    sim[...] = xi

    for pr in range(S5_PAIRS):
        xin = jnp.concatenate([xre[:, pr * LANES:(pr + 1) * LANES],
                               xim[:, pr * LANES:(pr + 1) * LANES]], axis=-1).astype(BF16)
        yc = _dot(xin, q_ref[pr])
        for half in range(2):
            g = 2 * pr + half
            cols = slice(g * gw, (g + 1) * gw)
            y = _dot(w_ref[:, cols], m_ref[g]) + yc[:, half * gw:(half + 1) * gw]
            o_ref[:, cols] = (0.5 * y * (1.0 + lax.erf(y * (2.0 ** -0.5)))).astype(BF16)


def _s5_core(u, ops):
    m, p2, q2, a_lr, a_li = ops
    s = u.shape[0]
    nsp = s // SPAN
    nc = s // S5_CHUNK
    wide = S5_G * S5_CHUNK * S5_CH
    w = u.reshape(nsp, NRES, BLK, S5_G, S5_CH).transpose(0, 2, 3, 1, 4).reshape(nc, wide)
    rows = min(S5_ROWS, nc)
    nstate = S5_G * S5_P
    y = pl.pallas_call(
        _s5_body,
        out_shape=jax.ShapeDtypeStruct((nc, wide), BF16),
        grid=(nc // rows,),
        in_specs=[pl.BlockSpec((rows, wide), lambda i: (i, 0)),
                  _full(m), _full(p2), _full(q2), _full(a_lr), _full(a_li)],
        out_specs=pl.BlockSpec((rows, wide), lambda i: (i, 0)),
        scratch_shapes=[pltpu.VMEM((rows, nstate), F32)] * 4 + [pltpu.VMEM((1, nstate), F32)] * 2,
        compiler_params=_params(("arbitrary",)),
        name="s5",
    )(w, m, p2, q2, a_lr, a_li)
    return y.reshape(nsp, BLK, S5_G, NRES, S5_CH).transpose(0, 3, 1, 2, 4).reshape(s, D_B)


def _split_bf16(x):
    hi = x.astype(BF16)
    lo = (x - hi.astype(F32)).astype(BF16)
    return hi, lo


def _mid_body(h_ref, ya_ref, yb_ref, wglu_ref, bglu_ref, ga_ref, gb_ref, wout_ref, gx_ref, wq_ref,
              k_ref, v_ref, wo_ref, gm_ref, wr_ref, br_ref, h_out, xn_out, idx_out, gate_out):
    part = h_ref.shape[0] // MID_CHAINS
    for c in range(MID_CHAINS):
        rows = slice(c * part, (c + 1) * part)
        outs = _mid_rows(h_ref[rows, :], ya_ref[rows, :], yb_ref[rows, :], wglu_ref, bglu_ref, ga_ref, gb_ref,
                         wout_ref, gx_ref, wq_ref, k_ref, v_ref, wo_ref, gm_ref, wr_ref, br_ref)
        for ref, val in zip((h_out, xn_out, idx_out, gate_out), outs):
            ref[rows, :] = val


def _mid_rows(h, ya, yb, wglu_ref, bglu_ref, ga_ref, gb_ref, wout_ref, gx_ref, wq_ref,
              k_ref, v_ref, wo_ref, gm_ref, wr_ref, br_ref):
    gate = jax.nn.sigmoid(_dot(yb, wglu_ref[...]) + bglu_ref[...])
    yb2 = yb.astype(F32) * gate
    na = _rms(ya.astype(F32), ga_ref[...]).astype(BF16)
    nb = _rms(yb2, gb_ref[...]).astype(BF16)
    h1 = h + _dot(na, wout_ref[0:D_A, :]) + _dot(nb, wout_ref[D_A:D_MODEL, :])
    q = _dot(_rms(h1, gx_ref[...]).astype(BF16), wq_ref[...]).astype(BF16)
    heads = []
    for hd in range(H_X):
        lanes = slice(hd * HD_X, (hd + 1) * HD_X)
        s = _dot_nt(q[:, lanes], k_ref[:, lanes])
        e = jnp.exp(s - jnp.max(s, axis=-1, keepdims=True))
        heads.append(_dot(e.astype(BF16), v_ref[:, lanes]) / jnp.sum(e, axis=-1, keepdims=True))
    o = jnp.concatenate(heads, axis=-1).astype(BF16)
    h2 = h1 + _dot(o, wo_ref[...])
    xn = _rms(h2, gm_ref[...])
    x_hi, x_lo = _split_bf16(xn)
    logits = _dot(x_hi, wr_ref[0]) + _dot(x_lo, wr_ref[0]) + _dot(x_hi, wr_ref[1]) + br_ref[...]
    lane = lax.broadcasted_iota(jnp.int32, logits.shape, 1)
    logits = jnp.where(lane < N_EXPERTS, logits, -jnp.inf)
    vals, idxs = [], []
    for _ in range(TOP_K):
        mx = jnp.max(logits, axis=-1, keepdims=True)
        ix = jnp.min(jnp.where(logits == mx, lane, LANES), axis=-1, keepdims=True)
        vals.append(mx)
        idxs.append(ix)
        logits = jnp.where(lane == ix, -jnp.inf, logits)
    es = [jnp.exp(v - vals[0]) for v in vals]
    den = es[0] + es[1] + es[2] + es[3]
    idx_t = jnp.full(lane.shape, N_EXPERTS, jnp.int32)
    gate_t = jnp.zeros(lane.shape, F32)
    for k in range(TOP_K):
        idx_t = jnp.where(lane == k, idxs[k], idx_t)
        gate_t = jnp.where(lane == k, es[k] / den, gate_t)
    return h2, _pack_rows(xn), idx_t, gate_t


def _mid(h, ya, yb, wglu, bglu, ga, gb, wout, gx, wq, kmem, vmem, wo, gm, wr2, br):
    s = h.shape[0]
    tm = min(ROW_TILE, s)
    row = lambda w: pl.BlockSpec((tm, w), lambda i: (i, 0))
    consts = [wglu, bglu, ga, gb, wout, gx, wq, kmem, vmem, wo, gm, wr2, br]
    return pl.pallas_call(
        _mid_body,
        out_shape=[jax.ShapeDtypeStruct((s, D_MODEL), F32), jax.ShapeDtypeStruct((s, D_MODEL // 2), jnp.int32),
                   jax.ShapeDtypeStruct((s, LANES), jnp.int32), jax.ShapeDtypeStruct((s, LANES), F32)],
        grid=(s // tm,),
        in_specs=[row(D_MODEL), row(D_A), row(D_B)] + [_full(a) for a in consts],
        out_specs=[row(D_MODEL), row(D_MODEL // 2), row(LANES), row(LANES)],
        compiler_params=_params(("parallel",)),
        name="mid",
    )(h, ya, yb, *consts)


def _memkv_body(mem_ref, g_ref, w_ref, o_ref):
    o_ref[...] = _dot(_rms(mem_ref[...], g_ref[...]).astype(BF16), w_ref[...]).astype(BF16)


def _memkv(mem, g, w):
    n, d = mem.shape
    return pl.pallas_call(
        _memkv_body,
        out_shape=jax.ShapeDtypeStruct((n, w.shape[1]), BF16),
        compiler_params=pltpu.CompilerParams(vmem_limit_bytes=VMEM_LIMIT),
        name="memkv",
    )(mem, g.reshape(1, d), w)


def _rank_body(idx_ref, rank_ref, cnt_ref, carry):
    @pl.when(pl.program_id(0) == 0)
    def _():
        carry[...] = jnp.zeros_like(carry)

    tm = idx_ref.shape[0]
    lane = lax.broadcasted_iota(jnp.int32, (tm, LANES), 1)
    idx = idx_ref[...]
    hits = [lane == idx[:, k:k + 1] for k in range(TOP_K)]
    onehot = jnp.zeros((tm, LANES), F32)
    for hit in hits:
        onehot = onehot + jnp.where(hit, 1.0, 0.0)
    ri = lax.broadcasted_iota(jnp.int32, (tm, tm), 0)
    ci = lax.broadcasted_iota(jnp.int32, (tm, tm), 1)
    tri = jnp.where(ri >= ci, 1.0, 0.0).astype(BF16)
    inclusive = _dot(tri, onehot.astype(BF16))
    before = carry[...] + inclusive - onehot
    rank = jnp.zeros((tm, LANES), jnp.int32)
    for k, hit in enumerate(hits):
        rk = jnp.sum(jnp.where(hit, before, 0.0), axis=-1, keepdims=True)
        rank = jnp.where(lane == k, rk.astype(jnp.int32), rank)
    rank_ref[...] = rank
    total = carry[...] + inclusive[tm - 1:tm, :]
    carry[...] = total
    cnt_ref[...] = total.astype(jnp.int32)


def _rank(idx):
    t = idx.shape[0]
    tm = min(ROW_TILE, t)
    return pl.pallas_call(
        _rank_body,
        out_shape=[jax.ShapeDtypeStruct((t, LANES), jnp.int32), jax.ShapeDtypeStruct((1, LANES), jnp.int32)],
        grid=(t // tm,),
        in_specs=[pl.BlockSpec((tm, LANES), lambda i: (i, 0))],
        out_specs=[pl.BlockSpec((tm, LANES), lambda i: (i, 0)), pl.BlockSpec((1, LANES), lambda i: (0, 0))],
        scratch_shapes=[pltpu.VMEM((1, LANES), F32)],
        compiler_params=_params(("arbitrary",)),
        name="rank",
    )(idx)


def _route(idx, tm):
    t = idx.shape[0]
    tk = t * TOP_K
    rank, cnt = _rank(idx)
    counts = cnt[0, :N_EXPERTS]
    padded = (counts + tm - 1) // tm * tm
    pend = jnp.cumsum(padded)
    pstart = pend - padded
    n_rows = tk + N_EXPERTS * tm
    n_tiles = n_rows // tm
    experts = jnp.arange(N_EXPERTS, dtype=jnp.int32)
    tile_first = jnp.arange(n_tiles, dtype=jnp.int32) * tm
    last_used = jnp.max(jnp.where(padded > 0, experts, 0))
    tile_e = jnp.minimum(jnp.sum(tile_first[:, None] >= pend[None, :], axis=1), last_used).astype(jnp.int32)
    tile_valid = (tile_first < pend[-1]).astype(jnp.int32)
    group = jnp.cumsum(jnp.concatenate([jnp.zeros((1,), jnp.int32),
                                        (tile_e[1:] != tile_e[:-1]).astype(jnp.int32)]))
    tile_slot = (group % 2).astype(jnp.int32)
    later = (experts[None, :] > experts[:, None]) & (padded > 0)[None, :]
    next_e = jnp.min(jnp.where(later, experts[None, :], N_EXPERTS), axis=1)
    next_e = jnp.where(next_e < N_EXPERTS, next_e, -1).astype(jnp.int32)
    tile_next = jnp.sum(jnp.where(tile_e[:, None] == experts[None, :], next_e[None, :], 0), axis=1).astype(jnp.int32)
    top = idx[:, :TOP_K]
    base = jnp.sum(jnp.where(top[:, :, None] == experts, pstart, 0), axis=-1)
    dest = (rank[:, :TOP_K] + base).T.reshape(tk)
    tbits = (t - 1).bit_length()
    fill_e = jnp.repeat(experts, tm)
    fill_j = jnp.tile(jnp.arange(tm, dtype=jnp.int32), N_EXPERTS)
    fill_key = jnp.where(fill_j < jnp.repeat(padded - counts, tm), 2 * fill_e + 1, 2 * N_EXPERTS)
    fill_tok = jnp.arange(N_EXPERTS * tm, dtype=jnp.int32) % t
    real_tok = jnp.arange(tk, dtype=jnp.int32) // TOP_K
    keys = jnp.concatenate([(2 * top.reshape(tk) << tbits) | real_tok, (fill_key << tbits) | fill_tok])
    tok = jnp.sort(keys) & ((1 << tbits) - 1)
    return dest, tok, tile_e, tile_valid, tile_slot, tile_next


def _gather_rows(table, idx):
    n, d = table.shape
    b = idx.shape[0]
    workers = SC_CORES * SC_SUBCORES
    per_worker = b // workers
    chunks = per_worker // SC_GATHER_ROWS
    mesh = plsc.VectorSubcoreMesh(core_axis_name="c", subcore_axis_name="s",
                                  num_cores=SC_CORES, num_subcores=SC_SUBCORES)

    @functools.partial(
        pl.kernel, mesh=mesh,
        out_type=jax.ShapeDtypeStruct((b, d), table.dtype),
        scratch_types=[pltpu.VMEM((SC_GATHER_ROWS,), jnp.int32),
                       pltpu.VMEM((SC_GATHER_ROWS, d), table.dtype),
                       pltpu.SemaphoreType.DMA],
    )
    def gather(table_hbm, idx_hbm, out_hbm, idx_v, rows_v, sem):
        base = (lax.axis_index("s") * SC_CORES + lax.axis_index("c")) * per_worker

        @pl.loop(0, chunks)
        def _(c):
            off = pl.multiple_of(base + c * SC_GATHER_ROWS, SC_GATHER_ROWS)
            pltpu.sync_copy(idx_hbm.at[pl.ds(off, SC_GATHER_ROWS)], idx_v)
            pltpu.async_copy(table_hbm.at[idx_v], rows_v, sem).wait()
            pltpu.sync_copy(rows_v, out_hbm.at[pl.ds(off, SC_GATHER_ROWS)])

    return gather(table, idx)


def _moe_body(te_ref, tv_ref, sl_ref, nx_ref, x_ref, w1_hbm, b1_ref, w2_hbm, b2_ref, o_ref,
              w1f, w2f, w1b, w2b, sem, *, layer):
    i = pl.program_id(0)
    e = te_ref[i]
    slot = sl_ref[i]
    new_expert = (i == 0) | (e != te_ref[jnp.maximum(i - 1, 0)])

    def weight_copies(expert, s):
        rows1 = w1f.shape[1] // MOE_W1_PARTS
        rows2 = w2f.shape[1] // MOE_W2_PARTS
        c1 = [pltpu.make_async_copy(w1_hbm.at[layer, expert, pl.ds(q * rows1, rows1)],
                                    w1f.at[s, pl.ds(q * rows1, rows1)], sem.at[s, q])
              for q in range(MOE_W1_PARTS)]
        c2 = [pltpu.make_async_copy(w2_hbm.at[layer, expert, pl.ds(q * rows2, rows2)],
                                    w2f.at[s, pl.ds(q * rows2, rows2)], sem.at[s, MOE_W1_PARTS + q])
              for q in range(MOE_W2_PARTS)]
        return c1 + c2

    @pl.when(i == 0)
    def _():
        for c in weight_copies(e, slot):
            c.start()

    @pl.when(new_expert)
    def _():
        for c in weight_copies(e, slot):
            c.wait()
        nxt = nx_ref[i]

        @pl.when(nxt >= 0)
        def _():
            for c in weight_copies(nxt, 1 - slot):
                c.start()

        w1b[...] = w1f[slot].astype(BF16)
        w2b[...] = w2f[slot].astype(BF16)

    @pl.when(tv_ref[i] > 0)
    def _():
        lo, hi = _unpack_rows(x_ref[...])
        x = jnp.concatenate([lo, hi], axis=-1).astype(BF16)
        hb = _dot(x, w1b[...]) + b1_ref[0]
        x_glu = jnp.minimum(hb[:, :D_FF], SWIGLU_LIMIT)
        x_lin = jnp.clip(hb[:, D_FF:], -SWIGLU_LIMIT, SWIGLU_LIMIT)
        act = x_glu * jax.nn.sigmoid(SWIGLU_ALPHA * x_glu) * (x_lin + 1.0)
        o_ref[...] = _pack_rows(_dot(act.astype(BF16), w2b[...]) + b2_ref[0])

    @pl.when(tv_ref[i] == 0)
    def _():
        o_ref[...] = jnp.zeros_like(o_ref)


def _moe_experts(xs, tile_e, tile_valid, tile_slot, tile_next, w1, b1, w2, b2, layer):
    n_rows = xs.shape[0]
    tm = MOE_TILE
    nl, ne, d, ff2 = w1.shape
    bias_map = lambda i, te, tv, sl, nx: (layer, te[i], 0, 0)
    grid_spec = pltpu.PrefetchScalarGridSpec(
        num_scalar_prefetch=4,
        grid=(n_rows // tm,),
        in_specs=[pl.BlockSpec((tm, d // 2), lambda i, *_: (i, 0)),
                  pl.BlockSpec(memory_space=pl.ANY),
                  pl.BlockSpec((None, 1, 1, ff2), bias_map),
                  pl.BlockSpec(memory_space=pl.ANY),
                  pl.BlockSpec((None, 1, 1, d), bias_map)],
        out_specs=pl.BlockSpec((tm, d // 2), lambda i, *_: (i, 0)),
        scratch_shapes=[pltpu.VMEM((2, d, ff2), F32), pltpu.VMEM((2, ff2 // 2, d), F32),
                        pltpu.VMEM((d, ff2), BF16), pltpu.VMEM((ff2 // 2, d), BF16),
                        pltpu.SemaphoreType.DMA((2, MOE_W1_PARTS + MOE_W2_PARTS))],
    )
    return pl.pallas_call(
        functools.partial(_moe_body, layer=layer),
        out_shape=jax.ShapeDtypeStruct((n_rows, d // 2), jnp.int32),
        grid_spec=grid_spec,
        compiler_params=_params(("arbitrary",)),
        name="moe",
    )(tile_e, tile_valid, tile_slot, tile_next, xs, w1, b1.reshape(nl, ne, 1, ff2), w2, b2.reshape(nl, ne, 1, d))


def _combine_body(h_ref, y0_ref, y1_ref, y2_ref, y3_ref, gate_ref, g_ref, o_ref, *, final):
    gates = gate_ref[...]
    lo = jnp.zeros(y0_ref.shape, F32)
    hi = jnp.zeros(y0_ref.shape, F32)
    for k, y_ref in enumerate((y0_ref, y1_ref, y2_ref, y3_ref)):
        yl, yh = _unpack_rows(y_ref[...])
        lo = lo + yl * gates[:, k:k + 1]
        hi = hi + yh * gates[:, k:k + 1]
    h = h_ref[...] + jnp.concatenate([lo, hi], axis=-1)
    o_ref[...] = _rms(h, g_ref[...]) if final else h


def _combine(h, y, gates, g, *, final):
    s, d = h.shape
    tm = min(ROW_TILE // 2, s)
    nt = s // tm
    return pl.pallas_call(
        functools.partial(_combine_body, final=final),
        out_shape=jax.ShapeDtypeStruct((s, d), F32),
        grid=(nt,),
        in_specs=[pl.BlockSpec((tm, d), lambda i: (i, 0))]
                 + [pl.BlockSpec((tm, d // 2), lambda i, k=k: (k * nt + i, 0)) for k in range(TOP_K)]
                 + [pl.BlockSpec((tm, LANES), lambda i: (i, 0)), pl.BlockSpec((1, d), lambda i: (0, 0))],
        out_specs=pl.BlockSpec((tm, d), lambda i: (i, 0)),
        compiler_params=_params(("parallel",)),
        name="combine",
    )(h, y, y, y, y, gates, g.reshape(1, d))


def _layer(h, mem, biases, p, l, g_final):
    row = lambda a: a.reshape(1, -1).astype(F32)
    q_scale = jnp.concatenate([jnp.full((D_A,), HD_A ** -0.5 * LOG2E, F32), jnp.ones((D_IN - D_A,), F32)])
    w_in = (p["w_in"][l] * q_scale).astype(BF16)
    proj = _inproj(h, p["norm_mix"][l], w_in)
    ya = _dilated_attention(proj, biases)
    ops = _s5_operators(p["s5_a_re"][l], p["s5_a_im"][l], p["s5_b_re"][l], p["s5_b_im"][l],
                        p["s5_c_re"][l], p["s5_c_im"][l], p["s5_log_dt"][l], p["s5_d"][l])
    yb = _s5_core(proj[:, 3 * D_A:], ops)
    kv = _memkv(mem, p["norm_mem"][l], p["w_xkv"][l].astype(BF16))
    wr = jnp.pad(p["w_router"][l].astype(F32), ((0, 0), (0, LANES - N_EXPERTS)))
    wr_hi = wr.astype(BF16)
    wr2 = jnp.stack([wr_hi, (wr - wr_hi.astype(F32)).astype(BF16)])
    br = jnp.pad(p["b_router"][l].astype(F32), (0, LANES - N_EXPERTS)).reshape(1, LANES)
    h2, xn, idx, gates = _mid(
        h, ya, yb, p["w_glu"][l].astype(BF16), row(p["b_glu"][l]), row(p["g_out_attn"][l]),
        row(p["g_out_ssm"][l]), p["w_out"][l].astype(BF16), row(p["norm_xattn"][l]),
        (p["w_xq"][l] * (HD_X ** -0.5)).astype(BF16), kv[:, :D_X], kv[:, D_X:],
        p["w_xo"][l].astype(BF16), row(p["norm_moe"][l]), wr2, br)
    dest, tok, tile_e, tile_valid, tile_slot, tile_next = _route(idx, MOE_TILE)
    out = _moe_experts(_gather_rows(xn, tok), tile_e, tile_valid, tile_slot, tile_next,
                       p["w1"], p["b1"], p["w2"], p["b2"], l)
    return _combine(h2, _gather_rows(out, dest), gates, g_final, final=l == DEPTH - 1)


def kernel(x, mem, rel_bias, norm_mix, w_in, s5_a_re, s5_a_im, s5_b_re, s5_b_im, s5_c_re, s5_c_im, s5_log_dt, s5_d, w_glu, b_glu, g_out_attn, g_out_ssm, w_out, norm_xattn, norm_mem, w_xq, w_xkv, w_xo, norm_moe, w_router, b_router, w1, b1, w2, b2, norm_final):
    p = dict(norm_mix=norm_mix, w_in=w_in, s5_a_re=s5_a_re, s5_a_im=s5_a_im, s5_b_re=s5_b_re,
             s5_b_im=s5_b_im, s5_c_re=s5_c_re, s5_c_im=s5_c_im, s5_log_dt=s5_log_dt, s5_d=s5_d,
             w_glu=w_glu, b_glu=b_glu, g_out_attn=g_out_attn, g_out_ssm=g_out_ssm, w_out=w_out,
             norm_xattn=norm_xattn, norm_mem=norm_mem, w_xq=w_xq, w_xkv=w_xkv, w_xo=w_xo,
             norm_moe=norm_moe, w_router=w_router, b_router=b_router, w1=w1, b1=b1, w2=w2, b2=b2)
    biases = [_attn_bias(rel_bias, window, dil, perm)
              for (window, dil), perm in zip(WIN_DIL, (_PERM_D1, _PERM_D4, _PERM_D16))]
    outs = []
    for b in range(x.shape[0]):
        h = _to_span_layout(x[b])
        for l in range(DEPTH):
            h = _layer(h, mem[b], biases, p, l, norm_final)
        outs.append(_from_span_layout(h))
    return jnp.stack(outs)
```

```python
import functools
import math

import jax
import jax.numpy as jnp
from jax import lax
from jax.experimental import pallas as pl
from jax.experimental.pallas import tpu as pltpu
from jax.experimental.pallas import tpu_sc as plsc

F32 = jnp.float32
BF16 = jnp.bfloat16

D_MODEL = 1024
DEPTH = 2
EPS = 1e-5
NEG_INF = -1e30
LOG2E = math.log2(math.e)
H_A = 8
HD_A = 64
D_A = H_A * HD_A
WIN_DIL = ((128, 1), (512, 4), (2048, 16))
BLK = 128
D_B = D_MODEL - D_A
S5_CH = 16
S5_G = D_B // S5_CH
S5_P = 64
D_IN = 3 * D_A + D_B
NUM_BUCKETS = 32
REL_MAX_DIST = 2048
H_X = 4
HD_X = 128
D_X = H_X * HD_X
N_EXPERTS = 32
TOP_K = 4
D_FF = D_MODEL
SWIGLU_ALPHA = 1.702
SWIGLU_LIMIT = 7.0

LANES = 128
NRES = WIN_DIL[-1][1]
SPAN = NRES * BLK
S5_CHUNK = NRES
S5_PAIRS = S5_G // 2
VMEM_LIMIT = 56 * 1024 * 1024

SC_CORES = 2
SC_SUBCORES = 16
SC_GATHER_ROWS = 64

ROW_TILE = 512
MID_CHAINS = 2
MOE_TILE = 512
MOE_W1_PARTS = 4
MOE_W2_PARTS = 2


def _params(sem):
    return pltpu.CompilerParams(dimension_semantics=sem, vmem_limit_bytes=VMEM_LIMIT)


def _rms(x, g):
    return x * lax.rsqrt(jnp.mean(x * x, axis=-1, keepdims=True) + EPS) * g


def _dot(a, b):
    return jnp.dot(a, b, preferred_element_type=F32)


def _dot_nt(a, b):
    return lax.dot_general(a, b, (((1,), (1,)), ((), ())), preferred_element_type=F32)


def _full(a):
    return pl.BlockSpec(a.shape, lambda *_: (0,) * a.ndim)


def _pack_rows(x):
    c = x.shape[1] // 2
    lo = lax.bitcast_convert_type(x[:, :c].astype(BF16).astype(F32), jnp.uint32)
    hi = lax.bitcast_convert_type(x[:, c:].astype(BF16).astype(F32), jnp.uint32)
    return lax.bitcast_convert_type(lax.shift_right_logical(lo, jnp.uint32(16)) | hi, jnp.int32)


def _unpack_rows(p):
    u = lax.bitcast_convert_type(p, jnp.uint32)
    lo = lax.bitcast_convert_type(lax.shift_left(u, jnp.uint32(16)), F32)
    hi = lax.bitcast_convert_type(u & jnp.uint32(0xFFFF0000), F32)
    return lo, hi


def _to_span_layout(x):
    s = x.shape[0]
    return x.reshape(s // SPAN, BLK, NRES, -1).transpose(0, 2, 1, 3).reshape(s, -1)


def _from_span_layout(x):
    s = x.shape[0]
    return x.reshape(s // SPAN, NRES, BLK, -1).transpose(0, 2, 1, 3).reshape(s, -1)


def _inproj_body(h_ref, g_ref, w_ref, o_ref):
    xn = _rms(h_ref[...], g_ref[...]).astype(BF16)
    o_ref[...] = _dot(xn, w_ref[...]).astype(BF16)


def _inproj(h, g, w):
    s, d = h.shape
    n = w.shape[1]
    tm = min(ROW_TILE, s)
    return pl.pallas_call(
        _inproj_body,
        out_shape=jax.ShapeDtypeStruct((s, n), BF16),
        grid=(s // tm,),
        in_specs=[pl.BlockSpec((tm, d), lambda i: (i, 0)),
                  pl.BlockSpec((1, d), lambda i: (0, 0)),
                  pl.BlockSpec((d, n), lambda i: (0, 0))],
        out_specs=pl.BlockSpec((tm, n), lambda i: (i, 0)),
        compiler_params=_params(("parallel",)),
        name="inproj",
    )(h, g.reshape(1, d), w)


def _t5_bucket(n):
    max_exact = NUM_BUCKETS // 2
    nf = jnp.maximum(n, 1).astype(F32)
    large = max_exact + (jnp.log(nf / max_exact) / math.log(REL_MAX_DIST / max_exact)
                         * (NUM_BUCKETS - max_exact)).astype(jnp.int32)
    large = jnp.minimum(large, NUM_BUCKETS - 1)
    return jnp.where(n < max_exact, n, large)


def _attn_bias(rel_bias, window, dil, perm):
    steps = window // dil
    perm = jnp.asarray(perm, jnp.int32)
    qi = perm[:, None]
    ki = jnp.concatenate([perm, BLK + perm])[None, :]
    dist = BLK + qi - ki
    in_win = (dist >= 0) & (dist <= steps)
    bucket = _t5_bucket(jnp.clip(dist, 0, steps) * dil)
    onehot = (bucket[:, :, None] == jnp.arange(NUM_BUCKETS, dtype=jnp.int32)).astype(F32)
    bias = jnp.einsum('qkb,bh->hqk', onehot, rel_bias.astype(F32), precision=lax.Precision.HIGHEST)
    bias = jnp.where(in_win[None], bias * LOG2E, NEG_INF)
    return bias.reshape(H_A // 2, 2 * BLK, 2 * BLK)


_PERM_D1 = [NRES * jl + r for r in range(NRES) for jl in range(BLK // NRES)]
_PERM_D4 = [4 * jl + i for i in range(4) for jl in range(BLK // 4)]
_PERM_D16 = list(range(BLK))


def _attn_body(q_ref, k_ref, v_ref, kp_ref, vp_ref, b1_ref, b4_ref, b16_ref, o_ref, acc, mst, lst):
    has_prev = pl.program_id(0) > 0
    lane = lax.broadcasted_iota(jnp.int32, (1, LANES), 1)
    lo = lane < HD_A
    mlo = lo.astype(BF16)
    mhi = (~lo).astype(BF16)
    col = lax.broadcasted_iota(jnp.int32, (2 * BLK, 2 * BLK), 1)
    ones = jnp.ones((2 * BLK, LANES), BF16)

    def tile(q2, kk, vv, bias, mask_prev):
        qs = jnp.concatenate([q2 * mlo, q2 * mhi], axis=0)
        s = _dot_nt(qs, kk) + bias
        if mask_prev:
            s = jnp.where(jnp.logical_or(has_prev, col >= BLK), s, NEG_INF)
        m = jnp.max(s, axis=-1, keepdims=True)
        e = jnp.exp2((s - m).astype(BF16))
        oa = _dot(e, jnp.concatenate([vv, ones], axis=1))
        o = oa[:, :LANES]
        l = oa[:, LANES:]
        return (jnp.where(lo, m[:BLK], m[BLK:]), jnp.where(lo, l[:BLK], l[BLK:]),
                jnp.where(lo, o[:BLK], o[BLK:]))

    def merge(prev, cur):
        mp, lp, ap = prev
        mc, lc, ac = cur
        mn = jnp.maximum(mp, mc)
        a = jnp.exp2(mp - mn)
        b = jnp.exp2(mc - mn)
        return mn, a * lp + b * lc, a * ap + b * ac

    def cat(xs):
        return jnp.concatenate(xs, axis=0)

    def d16_body(r, _):
        rows = pl.ds(pl.multiple_of(r * BLK, BLK), BLK)
        for hp in range(H_A // 2):
            lanes = slice(hp * LANES, (hp + 1) * LANES)
            kk = cat([kp_ref[rows, lanes], k_ref[rows, lanes]])
            vv = cat([vp_ref[rows, lanes], v_ref[rows, lanes]])
            m2, l2, o2 = tile(q_ref[rows, lanes], kk, vv, b16_ref[hp], True)
            mst[rows, lanes] = m2
            lst[rows, lanes] = l2
            acc[rows, lanes] = o2
        return 0

    lax.fori_loop(0, NRES, d16_body, 0)

    def d4_body(r4, _):
        for b in range(4):
            def chunk_rows(bb):
                return [pl.ds(pl.multiple_of(4 * BLK * i + BLK * r4 + 32 * bb, 32), 32) for i in range(4)]
            rows = chunk_rows(b)
            prows = chunk_rows(3 if b == 0 else b - 1)
            kprev, vprev = (kp_ref, vp_ref) if b == 0 else (k_ref, v_ref)
            for hp in range(H_A // 2):
                lanes = slice(hp * LANES, (hp + 1) * LANES)
                q2 = cat([q_ref[rr, lanes] for rr in rows])
                kk = cat([kprev[rr, lanes] for rr in prows] + [k_ref[rr, lanes] for rr in rows])
                vv = cat([vprev[rr, lanes] for rr in prows] + [v_ref[rr, lanes] for rr in rows])
                cur = tile(q2, kk, vv, b4_ref[hp], b == 0)
                prev = (cat([mst[rr, lanes] for rr in rows]), cat([lst[rr, lanes] for rr in rows]),
                        cat([acc[rr, lanes] for rr in rows]))
                mn, ln, an = merge(prev, cur)
                for i, rr in enumerate(rows):
                    part = slice(32 * i, 32 * (i + 1))
                    mst[rr, lanes] = mn[part]
                    lst[rr, lanes] = ln[part]
                    acc[rr, lanes] = an[part]
        return 0

    lax.fori_loop(0, 4, d4_body, 0)

    def d1_pair(ap, kprev, vprev, prev_ap, mask_prev):
        def tiles(a_):
            return [pl.ds(pl.multiple_of(BLK * r + 16 * a_, 16), 16) for r in range(NRES)]
        cur_t = tiles(ap)
        prev_t = tiles(prev_ap)

        def halves(ref, ts, lanes):
            xs = [ref[t, lanes].astype(F32) for t in ts]
            return cat([x[:8] for x in xs]).astype(BF16), cat([x[8:] for x in xs]).astype(BF16)

        for hp in range(H_A // 2):
            lanes = slice(hp * LANES, (hp + 1) * LANES)
            q_e, q_o = halves(q_ref, cur_t, lanes)
            k_e, k_o = halves(k_ref, cur_t, lanes)
            v_e, v_o = halves(v_ref, cur_t, lanes)
            _, k_p = halves(kprev, prev_t, lanes)
            _, v_p = halves(vprev, prev_t, lanes)
            cur_e = tile(q_e, cat([k_p, k_e]), cat([v_p, v_e]), b1_ref[hp], mask_prev)
            cur_o = tile(q_o, cat([k_e, k_o]), cat([v_e, v_o]), b1_ref[hp], False)
            ms = [mst[t, lanes] for t in cur_t]
            ls = [lst[t, lanes] for t in cur_t]
            ac = [acc[t, lanes] for t in cur_t]
            outs = []
            for half, cur in ((0, cur_e), (1, cur_o)):
                part = slice(8 * half, 8 * half + 8)
                prev = (cat([x[part] for x in ms]), cat([x[part] for x in ls]), cat([x[part] for x in ac]))
                _, ln, an = merge(prev, cur)
                outs.append(an / ln)
            for r, t in enumerate(cur_t):
                part = slice(8 * r, 8 * r + 8)
                o_ref[t, lanes] = cat([outs[0][part], outs[1][part]]).astype(o_ref.dtype)

    d1_pair(0, kp_ref, vp_ref, BLK // 16 - 1, True)

    def d1_body(ap, _):
        d1_pair(ap, k_ref, v_ref, ap - 1, False)
        return 0

    lax.fori_loop(1, BLK // 16, d1_body, 0)


def _dilated_attention(proj, biases):
    s = proj.shape[0]
    cur = lambda which: pl.BlockSpec((SPAN, D_A), lambda c: (c, which))
    prev = lambda which: pl.BlockSpec((SPAN, D_A), lambda c: (jnp.maximum(c - 1, 0), which))
    return pl.pallas_call(
        _attn_body,
        out_shape=jax.ShapeDtypeStruct((s, D_A), BF16),
        grid=(s // SPAN,),
        in_specs=[cur(0), cur(1), cur(2), prev(1), prev(2)] + [_full(b) for b in biases],
        out_specs=pl.BlockSpec((SPAN, D_A), lambda c: (c, 0)),
        scratch_shapes=[pltpu.VMEM((SPAN, D_A), F32)] * 3,
        compiler_params=_params(("arbitrary",)),
        name="attn",
    )(proj, proj, proj, proj, proj, *biases)


def _s5_operators(a_re, a_im, b_re, b_im, c_re, c_im, log_dt, d_skip):
    L = S5_CHUNK
    lam = lax.complex(a_re.astype(F32), a_im.astype(F32))
    dt = jnp.exp(log_dt.astype(F32))[:, None]
    a_bar = jnp.exp(lam * dt)
    b_bar = ((a_bar - 1.0) / lam)[..., None] * lax.complex(b_re.astype(F32), b_im.astype(F32))
    c = lax.complex(c_re.astype(F32), c_im.astype(F32))
    j = jnp.arange(L + 1, dtype=F32)
    log_a = lam * dt
    apow = jnp.exp(log_a[None] * j[:, None, None])
    kj = jnp.einsum('gdp,jgp,gpc->jgdc', c, apow[:L], b_bar).real
    s_idx = jnp.arange(L)[:, None]
    t_idx = jnp.arange(L)[None, :]
    lag = t_idx - s_idx
    lag_onehot = (lag[None] == jnp.arange(L)[:, None, None]).astype(F32)
    m = jnp.einsum('jst,jgdc->stgdc', lag_onehot, kj, precision=lax.Precision.HIGHEST)
    m = m.transpose(2, 0, 4, 1, 3)
    eye = jnp.eye(L)[:, None, :, None] * jnp.eye(S5_CH)[None, :, None, :]
    m = m + eye[None] * d_skip.astype(F32).reshape(S5_G, 1, S5_CH, 1, 1)
    m = m.reshape(S5_G, L * S5_CH, L * S5_CH)
    p = jnp.einsum('sgp,gpc->gscp', apow[:L][::-1], b_bar).reshape(S5_G, L * S5_CH, S5_P)
    ca = jnp.einsum('gdp,tgp->gptd', c, apow[1:L + 1]).reshape(S5_G, S5_P, L * S5_CH)
    a_l = apow[L]

    def pair_blocks(x):
        g, r, w = x.shape
        x = x.reshape(S5_PAIRS, 2, r, w)
        z = jnp.zeros_like(x[:, 0])
        top = jnp.concatenate([x[:, 0], z], axis=-1)
        bot = jnp.concatenate([z, x[:, 1]], axis=-1)
        return jnp.concatenate([top, bot], axis=1)

    p2 = jnp.concatenate([pair_blocks(p.real), pair_blocks(p.imag)], axis=-1)
    q2 = jnp.concatenate([pair_blocks(ca.real), pair_blocks(-ca.imag)], axis=1)
    a_lr = a_l.real.reshape(1, S5_G * S5_P)
    a_li = a_l.imag.reshape(1, S5_G * S5_P)
    return m.astype(BF16), p2.astype(BF16), q2.astype(BF16), a_lr, a_li


def _s5_body(u_ref, m_ref, p_ref, q_ref, ar_ref, ai_ref, o_ref, w_ref, y_ref, ere, eim, xre, xim, sre, sim):
    rows = BLK
    gw = S5_CHUNK * S5_CH
    pw = 2 * gw
    per_vreg = LANES // S5_CH
    chunk_of_lane = lax.broadcasted_iota(jnp.int32, (rows, LANES), 1) // S5_CH

    def regroup(srcs, k):
        acc = None
        for i, src in enumerate(srcs):
            shift = ((i - k) * S5_CH) % LANES
            moved = pltpu.roll(src, shift, axis=1) if shift else src
            acc = moved if acc is None else jnp.where(chunk_of_lane == i, moved, acc)
        return acc

    @pl.when(pl.program_id(0) == 0)
    def _():
        sre[...] = jnp.zeros_like(sre)
        sim[...] = jnp.zeros_like(sim)

    for b in range(D_B // LANES):
        for a in range(S5_CHUNK // per_vreg):
            srcs = [u_ref[(per_vreg * a + i) * rows:(per_vreg * a + i + 1) * rows,
                          b * LANES:(b + 1) * LANES].astype(F32) for i in range(per_vreg)]
            for gi in range(per_vreg):
                g = per_vreg * b + gi
                w_ref[:, g * gw + a * LANES:g * gw + (a + 1) * LANES] = regroup(srcs, gi).astype(BF16)

    for pr in range(S5_PAIRS):
        e = _dot(w_ref[:, pr * pw:(pr + 1) * pw], p_ref[pr])
        ere[:, pr * LANES:(pr + 1) * LANES] = e[:, :LANES]
        eim[:, pr * LANES:(pr + 1) * LANES] = e[:, LANES:]

    ar = ar_ref[...]
    ai = ai_ref[...]

    def step(n, carry):
        xr, xi = carry
        xre[pl.ds(n, 1), :] = xr
        xim[pl.ds(n, 1), :] = xi
        nr = ar * xr - ai * xi + ere[pl.ds(n, 1), :]
        ni = ar * xi + ai * xr + eim[pl.ds(n, 1), :]
        return nr, ni

    xr, xi = lax.fori_loop(0, rows, step, (sre[...], sim[...]))
    sre[...] = xr
    sim[...] = xi

    for pr in range(S5_PAIRS):
        xin = jnp.concatenate([xre[:, pr * LANES:(pr + 1) * LANES],
                               xim[:, pr * LANES:(pr + 1) * LANES]], axis=-1).astype(BF16)
        yc = _dot(xin, q_ref[pr])
        for half in range(2):
            g = 2 * pr + half
            cols = slice(g * gw, (g + 1) * gw)
            y = _dot(w_ref[:, cols], m_ref[g]) + yc[:, half * gw:(half + 1) * gw]
            y_ref[:, cols] = 0.5 * y * (1.0 + lax.erf(y * (2.0 ** -0.5)))

    for b in range(D_B // LANES):
        for a in range(S5_CHUNK // per_vreg):
            srcs = [y_ref[:, (per_vreg * b + i) * gw + a * LANES:(per_vreg * b + i) * gw + (a + 1) * LANES]
                    for i in range(per_vreg)]
            for ri in range(per_vreg):
                r = per_vreg * a + ri
                o_ref[r * rows:(r + 1) * rows, b * LANES:(b + 1) * LANES] = regroup(srcs, ri).astype(BF16)


def _s5_core(proj, ops):
    m, p2, q2, a_lr, a_li = ops
    s = proj.shape[0]
    wide = S5_G * S5_CHUNK * S5_CH
    nstate = S5_G * S5_P
    return pl.pallas_call(
        _s5_body,
        out_shape=jax.ShapeDtypeStruct((s, D_B), BF16),
        grid=(s // SPAN,),
        in_specs=[pl.BlockSpec((SPAN, D_B), lambda i: (i, 3 * D_A // D_B)),
                  _full(m), _full(p2), _full(q2), _full(a_lr), _full(a_li)],
        out_specs=pl.BlockSpec((SPAN, D_B), lambda i: (i, 0)),
        scratch_shapes=[pltpu.VMEM((BLK, wide), BF16), pltpu.VMEM((BLK, wide), F32)]
                       + [pltpu.VMEM((BLK, nstate), F32)] * 4 + [pltpu.VMEM((1, nstate), F32)] * 2,
        compiler_params=_params(("arbitrary",)),
        name="s5",
    )(proj, m, p2, q2, a_lr, a_li)


def _split_bf16(x):
    hi = x.astype(BF16)
    lo = (x - hi.astype(F32)).astype(BF16)
    return hi, lo


def _mid_body(h_ref, ya_ref, yb_ref, wglu_ref, bglu_ref, ga_ref, gb_ref, wout_ref, gx_ref, wq_ref,
              k_ref, v_ref, wo_ref, gm_ref, wr_ref, br_ref, h_out, xn_out, idx_out, gate_out):
    part = h_ref.shape[0] // MID_CHAINS
    for c in range(MID_CHAINS):
        rows = slice(c * part, (c + 1) * part)
        outs = _mid_rows(h_ref[rows, :], ya_ref[rows, :], yb_ref[rows, :], wglu_ref, bglu_ref, ga_ref, gb_ref,
                         wout_ref, gx_ref, wq_ref, k_ref, v_ref, wo_ref, gm_ref, wr_ref, br_ref)
        for ref, val in zip((h_out, xn_out, idx_out, gate_out), outs):
            ref[rows, :] = val


def _mid_rows(h, ya, yb, wglu_ref, bglu_ref, ga_ref, gb_ref, wout_ref, gx_ref, wq_ref,
              k_ref, v_ref, wo_ref, gm_ref, wr_ref, br_ref):
    gate = jax.nn.sigmoid(_dot(yb, wglu_ref[...]) + bglu_ref[...])
    yb2 = yb.astype(F32) * gate
    na = _rms(ya.astype(F32), ga_ref[...]).astype(BF16)
    nb = _rms(yb2, gb_ref[...]).astype(BF16)
    h1 = h + _dot(na, wout_ref[0:D_A, :]) + _dot(nb, wout_ref[D_A:D_MODEL, :])
    q = _dot(_rms(h1, gx_ref[...]).astype(BF16), wq_ref[...]).astype(BF16)
    heads = []
    for hd in range(H_X):
        lanes = slice(hd * HD_X, (hd + 1) * HD_X)
        s = _dot_nt(q[:, lanes], k_ref[:, lanes])
        e = jnp.exp(s - jnp.max(s, axis=-1, keepdims=True))
        heads.append(_dot(e.astype(BF16), v_ref[:, lanes]) / jnp.sum(e, axis=-1, keepdims=True))
    o = jnp.concatenate(heads, axis=-1).astype(BF16)
    h2 = h1 + _dot(o, wo_ref[...])
    xn = _rms(h2, gm_ref[...])
    x_hi, x_lo = _split_bf16(xn)
    logits = _dot(x_hi, wr_ref[0]) + _dot(x_lo, wr_ref[0]) + _dot(x_hi, wr_ref[1]) + br_ref[...]
    lane = lax.broadcasted_iota(jnp.int32, logits.shape, 1)
    logits = jnp.where(lane < N_EXPERTS, logits, -jnp.inf)
    vals, idxs = [], []
    for _ in range(TOP_K):
        mx = jnp.max(logits, axis=-1, keepdims=True)
        ix = jnp.min(jnp.where(logits == mx, lane, LANES), axis=-1, keepdims=True)
        vals.append(mx)
        idxs.append(ix)
        logits = jnp.where(lane == ix, -jnp.inf, logits)
    es = [jnp.exp(v - vals[0]) for v in vals]
    den = es[0] + es[1] + es[2] + es[3]
    idx_t = jnp.full(lane.shape, N_EXPERTS, jnp.int32)
    gate_t = jnp.zeros(lane.shape, F32)
    for k in range(TOP_K):
        idx_t = jnp.where(lane == k, idxs[k], idx_t)
        gate_t = jnp.where(lane == k, es[k] / den, gate_t)
    return h2, _pack_rows(xn), idx_t, gate_t


def _mid(h, ya, yb, wglu, bglu, ga, gb, wout, gx, wq, kmem, vmem, wo, gm, wr2, br):
    s = h.shape[0]
    tm = min(ROW_TILE, s)
    row = lambda w: pl.BlockSpec((tm, w), lambda i: (i, 0))
    consts = [wglu, bglu, ga, gb, wout, gx, wq, kmem, vmem, wo, gm, wr2, br]
    return pl.pallas_call(
        _mid_body,
        out_shape=[jax.ShapeDtypeStruct((s, D_MODEL), F32), jax.ShapeDtypeStruct((s, D_MODEL // 2), jnp.int32),
                   jax.ShapeDtypeStruct((s, LANES), jnp.int32), jax.ShapeDtypeStruct((s, LANES), F32)],
        grid=(s // tm,),
        in_specs=[row(D_MODEL), row(D_A), row(D_B)] + [_full(a) for a in consts],
        out_specs=[row(D_MODEL), row(D_MODEL // 2), row(LANES), row(LANES)],
        compiler_params=_params(("parallel",)),
        name="mid",
    )(h, ya, yb, *consts)


def _memkv_body(mem_ref, g_ref, w_ref, o_ref):
    o_ref[...] = _dot(_rms(mem_ref[...], g_ref[...]).astype(BF16), w_ref[...]).astype(BF16)


def _memkv(mem, g, w):
    n, d = mem.shape
    return pl.pallas_call(
        _memkv_body,
        out_shape=jax.ShapeDtypeStruct((n, w.shape[1]), BF16),
        compiler_params=pltpu.CompilerParams(vmem_limit_bytes=VMEM_LIMIT),
        name="memkv",
    )(mem, g.reshape(1, d), w)


def _rank_body(idx_ref, rank_ref, cnt_ref, carry):
    @pl.when(pl.program_id(0) == 0)
    def _():
        carry[...] = jnp.zeros_like(carry)

    tm = idx_ref.shape[0]
    lane = lax.broadcasted_iota(jnp.int32, (tm, LANES), 1)
    idx = idx_ref[...]
    hits = [lane == idx[:, k:k + 1] for k in range(TOP_K)]
    onehot = jnp.zeros((tm, LANES), F32)
    for hit in hits:
        onehot = onehot + jnp.where(hit, 1.0, 0.0)
    ri = lax.broadcasted_iota(jnp.int32, (tm, tm), 0)
    ci = lax.broadcasted_iota(jnp.int32, (tm, tm), 1)
    tri = jnp.where(ri >= ci, 1.0, 0.0).astype(BF16)
    inclusive = _dot(tri, onehot.astype(BF16))
    before = carry[...] + inclusive - onehot
    rank = jnp.zeros((tm, LANES), jnp.int32)
    for k, hit in enumerate(hits):
        rk = jnp.sum(jnp.where(hit, before, 0.0), axis=-1, keepdims=True)
        rank = jnp.where(lane == k, rk.astype(jnp.int32), rank)
    rank_ref[...] = rank
    total = carry[...] + inclusive[tm - 1:tm, :]
    carry[...] = total
    cnt_ref[...] = total.astype(jnp.int32)


def _rank(idx):
    t = idx.shape[0]
    tm = min(ROW_TILE, t)
    return pl.pallas_call(
        _rank_body,
        out_shape=[jax.ShapeDtypeStruct((t, LANES), jnp.int32), jax.ShapeDtypeStruct((1, LANES), jnp.int32)],
        grid=(t // tm,),
        in_specs=[pl.BlockSpec((tm, LANES), lambda i: (i, 0))],
        out_specs=[pl.BlockSpec((tm, LANES), lambda i: (i, 0)), pl.BlockSpec((1, LANES), lambda i: (0, 0))],
        scratch_shapes=[pltpu.VMEM((1, LANES), F32)],
        compiler_params=_params(("arbitrary",)),
        name="rank",
    )(idx)


def _route(idx, tm):
    t = idx.shape[0]
    tk = t * TOP_K
    rank, cnt = _rank(idx)
    counts = cnt[0, :N_EXPERTS]
    padded = (counts + tm - 1) // tm * tm
    pend = jnp.cumsum(padded)
    pstart = pend - padded
    n_rows = tk + N_EXPERTS * tm
    n_tiles = n_rows // tm
    experts = jnp.arange(N_EXPERTS, dtype=jnp.int32)
    tile_first = jnp.arange(n_tiles, dtype=jnp.int32) * tm
    last_used = jnp.max(jnp.where(padded > 0, experts, 0))
    tile_e = jnp.minimum(jnp.sum(tile_first[:, None] >= pend[None, :], axis=1), last_used).astype(jnp.int32)
    tile_valid = (tile_first < pend[-1]).astype(jnp.int32)
    group = jnp.cumsum(jnp.concatenate([jnp.zeros((1,), jnp.int32),
                                        (tile_e[1:] != tile_e[:-1]).astype(jnp.int32)]))
    tile_slot = (group % 2).astype(jnp.int32)
    later = (experts[None, :] > experts[:, None]) & (padded > 0)[None, :]
    next_e = jnp.min(jnp.where(later, experts[None, :], N_EXPERTS), axis=1)
    next_e = jnp.where(next_e < N_EXPERTS, next_e, -1).astype(jnp.int32)
    tile_next = jnp.sum(jnp.where(tile_e[:, None] == experts[None, :], next_e[None, :], 0), axis=1).astype(jnp.int32)
    top = idx[:, :TOP_K]
    base = jnp.sum(jnp.where(top[:, :, None] == experts, pstart, 0), axis=-1)
    dest = (rank[:, :TOP_K] + base).T.reshape(tk)
    tbits = (t - 1).bit_length()
    fill_e = jnp.repeat(experts, tm)
    fill_j = jnp.tile(jnp.arange(tm, dtype=jnp.int32), N_EXPERTS)
    fill_key = jnp.where(fill_j < jnp.repeat(padded - counts, tm), 2 * fill_e + 1, 2 * N_EXPERTS)
    fill_tok = jnp.arange(N_EXPERTS * tm, dtype=jnp.int32) % t
    real_tok = jnp.arange(tk, dtype=jnp.int32) // TOP_K
    keys = jnp.concatenate([(2 * top.reshape(tk) << tbits) | real_tok, (fill_key << tbits) | fill_tok])
    tok = jnp.sort(keys) & ((1 << tbits) - 1)
    return dest, tok, tile_e, tile_valid, tile_slot, tile_next


def _gather_rows(table, idx):
    n, d = table.shape
    b = idx.shape[0]
    workers = SC_CORES * SC_SUBCORES
    per_worker = b // workers
    chunks = per_worker // SC_GATHER_ROWS
    mesh = plsc.VectorSubcoreMesh(core_axis_name="c", subcore_axis_name="s",
                                  num_cores=SC_CORES, num_subcores=SC_SUBCORES)

    @functools.partial(
        pl.kernel, mesh=mesh,
        out_type=jax.ShapeDtypeStruct((b, d), table.dtype),
        scratch_types=[pltpu.VMEM((SC_GATHER_ROWS,), jnp.int32),
                       pltpu.VMEM((SC_GATHER_ROWS, d), table.dtype),
                       pltpu.SemaphoreType.DMA],
    )
    def gather(table_hbm, idx_hbm, out_hbm, idx_v, rows_v, sem):
        base = (lax.axis_index("s") * SC_CORES + lax.axis_index("c")) * per_worker

        @pl.loop(0, chunks)
        def _(c):
            off = pl.multiple_of(base + c * SC_GATHER_ROWS, SC_GATHER_ROWS)
            pltpu.sync_copy(idx_hbm.at[pl.ds(off, SC_GATHER_ROWS)], idx_v)
            pltpu.async_copy(table_hbm.at[idx_v], rows_v, sem).wait()
            pltpu.sync_copy(rows_v, out_hbm.at[pl.ds(off, SC_GATHER_ROWS)])

    return gather(table, idx)


def _moe_body(te_ref, tv_ref, sl_ref, nx_ref, x_ref, w1_hbm, b1_ref, w2_hbm, b2_ref, o_ref,
              w1f, w2f, w1b, w2b, sem, *, layer):
    i = pl.program_id(0)
    e = te_ref[i]
    slot = sl_ref[i]
    new_expert = (i == 0) | (e != te_ref[jnp.maximum(i - 1, 0)])

    def weight_copies(expert, s):
        rows1 = w1f.shape[1] // MOE_W1_PARTS
        rows2 = w2f.shape[1] // MOE_W2_PARTS
        c1 = [pltpu.make_async_copy(w1_hbm.at[layer, expert, pl.ds(q * rows1, rows1)],
                                    w1f.at[s, pl.ds(q * rows1, rows1)], sem.at[s, q])
              for q in range(MOE_W1_PARTS)]
        c2 = [pltpu.make_async_copy(w2_hbm.at[layer, expert, pl.ds(q * rows2, rows2)],
                                    w2f.at[s, pl.ds(q * rows2, rows2)], sem.at[s, MOE_W1_PARTS + q])
              for q in range(MOE_W2_PARTS)]
        return c1 + c2

    @pl.when(i == 0)
    def _():
        for c in weight_copies(e, slot):
            c.start()

    @pl.when(new_expert)
    def _():
        for c in weight_copies(e, slot):
            c.wait()
        nxt = nx_ref[i]

        @pl.when(nxt >= 0)
        def _():
            for c in weight_copies(nxt, 1 - slot):
                c.start()

        w1b[...] = w1f[slot].astype(BF16)
        w2b[...] = w2f[slot].astype(BF16)

    @pl.when(tv_ref[i] > 0)
    def _():
        lo, hi = _unpack_rows(x_ref[...])
        x = jnp.concatenate([lo, hi], axis=-1).astype(BF16)
        hb = _dot(x, w1b[...]) + b1_ref[0]
        x_glu = jnp.minimum(hb[:, :D_FF], SWIGLU_LIMIT)
        x_lin = jnp.clip(hb[:, D_FF:], -SWIGLU_LIMIT, SWIGLU_LIMIT)
        act = x_glu * jax.nn.sigmoid(SWIGLU_ALPHA * x_glu) * (x_lin + 1.0)
        o_ref[...] = _pack_rows(_dot(act.astype(BF16), w2b[...]) + b2_ref[0])

    @pl.when(tv_ref[i] == 0)
    def _():
        o_ref[...] = jnp.zeros_like(o_ref)


def _moe_experts(xs, tile_e, tile_valid, tile_slot, tile_next, w1, b1, w2, b2, layer):
    n_rows = xs.shape[0]
    tm = MOE_TILE
    nl, ne, d, ff2 = w1.shape
    bias_map = lambda i, te, tv, sl, nx: (layer, te[i], 0, 0)
    grid_spec = pltpu.PrefetchScalarGridSpec(
        num_scalar_prefetch=4,
        grid=(n_rows // tm,),
        in_specs=[pl.BlockSpec((tm, d // 2), lambda i, *_: (i, 0)),
                  pl.BlockSpec(memory_space=pl.ANY),
                  pl.BlockSpec((None, 1, 1, ff2), bias_map),
                  pl.BlockSpec(memory_space=pl.ANY),
                  pl.BlockSpec((None, 1, 1, d), bias_map)],
        out_specs=pl.BlockSpec((tm, d // 2), lambda i, *_: (i, 0)),
        scratch_shapes=[pltpu.VMEM((2, d, ff2), F32), pltpu.VMEM((2, ff2 // 2, d), F32),
                        pltpu.VMEM((d, ff2), BF16), pltpu.VMEM((ff2 // 2, d), BF16),
                        pltpu.SemaphoreType.DMA((2, MOE_W1_PARTS + MOE_W2_PARTS))],
    )
    return pl.pallas_call(
        functools.partial(_moe_body, layer=layer),
        out_shape=jax.ShapeDtypeStruct((n_rows, d // 2), jnp.int32),
        grid_spec=grid_spec,
        compiler_params=_params(("arbitrary",)),
        name="moe",
    )(tile_e, tile_valid, tile_slot, tile_next, xs, w1, b1.reshape(nl, ne, 1, ff2), w2, b2.reshape(nl, ne, 1, d))


def _combine_body(h_ref, y0_ref, y1_ref, y2_ref, y3_ref, gate_ref, g_ref, o_ref, *, final):
    gates = gate_ref[...]
    lo = jnp.zeros(y0_ref.shape, F32)
    hi = jnp.zeros(y0_ref.shape, F32)
    for k, y_ref in enumerate((y0_ref, y1_ref, y2_ref, y3_ref)):
        yl, yh = _unpack_rows(y_ref[...])
        lo = lo + yl * gates[:, k:k + 1]
        hi = hi + yh * gates[:, k:k + 1]
    h = h_ref[...] + jnp.concatenate([lo, hi], axis=-1)
    o_ref[...] = _rms(h, g_ref[...]) if final else h


def _combine(h, y, gates, g, *, final):
    s, d = h.shape
    tm = min(ROW_TILE // 2, s)
    nt = s // tm
    return pl.pallas_call(
        functools.partial(_combine_body, final=final),
        out_shape=jax.ShapeDtypeStruct((s, d), F32),
        grid=(nt,),
        in_specs=[pl.BlockSpec((tm, d), lambda i: (i, 0))]
                 + [pl.BlockSpec((tm, d // 2), lambda i, k=k: (k * nt + i, 0)) for k in range(TOP_K)]
                 + [pl.BlockSpec((tm, LANES), lambda i: (i, 0)), pl.BlockSpec((1, d), lambda i: (0, 0))],
        out_specs=pl.BlockSpec((tm, d), lambda i: (i, 0)),
        compiler_params=_params(("parallel",)),
        name="combine",
    )(h, y, y, y, y, gates, g.reshape(1, d))


def _layer(h, mem, biases, p, l, g_final):
    row = lambda a: a.reshape(1, -1).astype(F32)
    q_scale = jnp.concatenate([jnp.full((D_A,), HD_A ** -0.5 * LOG2E, F32), jnp.ones((D_IN - D_A,), F32)])
    w_in = (p["w_in"][l] * q_scale).astype(BF16)
    proj = _inproj(h, p["norm_mix"][l], w_in)
    ya = _dilated_attention(proj, biases)
    ops = _s5_operators(p["s5_a_re"][l], p["s5_a_im"][l], p["s5_b_re"][l], p["s5_b_im"][l],
                        p["s5_c_re"][l], p["s5_c_im"][l], p["s5_log_dt"][l], p["s5_d"][l])
    yb = _s5_core(proj, ops)
    kv = _memkv(mem, p["norm_mem"][l], p["w_xkv"][l].astype(BF16))
    wr = jnp.pad(p["w_router"][l].astype(F32), ((0, 0), (0, LANES - N_EXPERTS)))
    wr_hi = wr.astype(BF16)
    wr2 = jnp.stack([wr_hi, (wr - wr_hi.astype(F32)).astype(BF16)])
    br = jnp.pad(p["b_router"][l].astype(F32), (0, LANES - N_EXPERTS)).reshape(1, LANES)
    h2, xn, idx, gates = _mid(
        h, ya, yb, p["w_glu"][l].astype(BF16), row(p["b_glu"][l]), row(p["g_out_attn"][l]),
        row(p["g_out_ssm"][l]), p["w_out"][l].astype(BF16), row(p["norm_xattn"][l]),
        (p["w_xq"][l] * (HD_X ** -0.5)).astype(BF16), kv[:, :D_X], kv[:, D_X:],
        p["w_xo"][l].astype(BF16), row(p["norm_moe"][l]), wr2, br)
    dest, tok, tile_e, tile_valid, tile_slot, tile_next = _route(idx, MOE_TILE)
    out = _moe_experts(_gather_rows(xn, tok), tile_e, tile_valid, tile_slot, tile_next,
                       p["w1"], p["b1"], p["w2"], p["b2"], l)
    return _combine(h2, _gather_rows(out, dest), gates, g_final, final=l == DEPTH - 1)


def kernel(x, mem, rel_bias, norm_mix, w_in, s5_a_re, s5_a_im, s5_b_re, s5_b_im, s5_c_re, s5_c_im, s5_log_dt, s5_d, w_glu, b_glu, g_out_attn, g_out_ssm, w_out, norm_xattn, norm_mem, w_xq, w_xkv, w_xo, norm_moe, w_router, b_router, w1, b1, w2, b2, norm_final):
    p = dict(norm_mix=norm_mix, w_in=w_in, s5_a_re=s5_a_re, s5_a_im=s5_a_im, s5_b_re=s5_b_re,
             s5_b_im=s5_b_im, s5_c_re=s5_c_re, s5_c_im=s5_c_im, s5_log_dt=s5_log_dt, s5_d=s5_d,
             w_glu=w_glu, b_glu=b_glu, g_out_attn=g_out_attn, g_out_ssm=g_out_ssm, w_out=w_out,
             norm_xattn=norm_xattn, norm_mem=norm_mem, w_xq=w_xq, w_xkv=w_xkv, w_xo=w_xo,
             norm_moe=norm_moe, w_router=w_router, b_router=b_router, w1=w1, b1=b1, w2=w2, b2=b2)
    biases = [_attn_bias(rel_bias, window, dil, perm)
              for (window, dil), perm in zip(WIN_DIL, (_PERM_D1, _PERM_D4, _PERM_D16))]
    outs = []
    for b in range(x.shape[0]):
        h = _to_span_layout(x[b])
        for l in range(DEPTH):
            h = _layer(h, mem[b], biases, p, l, norm_final)
        outs.append(_from_span_layout(h))
    return jnp.stack(outs)
```

```python
import functools
import math

import jax
import jax.numpy as jnp
from jax import lax
from jax.experimental import pallas as pl
from jax.experimental.pallas import tpu as pltpu
from jax.experimental.pallas import tpu_sc as plsc

F32 = jnp.float32
BF16 = jnp.bfloat16

D_MODEL = 1024
DEPTH = 2
EPS = 1e-5
NEG_INF = -1e30
LOG2E = math.log2(math.e)
H_A = 8
HD_A = 64
D_A = H_A * HD_A
WIN_DIL = ((128, 1), (512, 4), (2048, 16))
BLK = 128
D_B = D_MODEL - D_A
S5_CH = 16
S5_G = D_B // S5_CH
S5_P = 64
D_IN = 3 * D_A + D_B
NUM_BUCKETS = 32
REL_MAX_DIST = 2048
H_X = 4
HD_X = 128
D_X = H_X * HD_X
N_EXPERTS = 32
TOP_K = 4
D_FF = D_MODEL
SWIGLU_ALPHA = 1.702
SWIGLU_LIMIT = 7.0

LANES = 128
NRES = WIN_DIL[-1][1]
SPAN = NRES * BLK
S5_CHUNK = NRES
S5_PAIRS = S5_G // 2
VMEM_LIMIT = 56 * 1024 * 1024

SC_CORES = 2
SC_SUBCORES = 16
SC_ROWS = 64

ROW_TILE = 512
MID_CHAINS = 2
MOE_TILE = 512
MOE_W1_PARTS = 4
MOE_W2_PARTS = 2


def _params(sem):
    return pltpu.CompilerParams(dimension_semantics=sem, vmem_limit_bytes=VMEM_LIMIT)


def _rms(x, g):
    return x * lax.rsqrt(jnp.mean(x * x, axis=-1, keepdims=True) + EPS) * g


def _dot(a, b):
    return jnp.dot(a, b, preferred_element_type=F32)


def _dot_nt(a, b):
    return lax.dot_general(a, b, (((1,), (1,)), ((), ())), preferred_element_type=F32)


def _full(a):
    return pl.BlockSpec(a.shape, lambda *_: (0,) * a.ndim)


def _pack_rows(x):
    c = x.shape[1] // 2
    lo = lax.bitcast_convert_type(x[:, :c].astype(BF16).astype(F32), jnp.uint32)
    hi = lax.bitcast_convert_type(x[:, c:].astype(BF16).astype(F32), jnp.uint32)
    return lax.bitcast_convert_type(lax.shift_right_logical(lo, jnp.uint32(16)) | hi, jnp.int32)


def _unpack_rows(p):
    u = lax.bitcast_convert_type(p, jnp.uint32)
    lo = lax.bitcast_convert_type(lax.shift_left(u, jnp.uint32(16)), F32)
    hi = lax.bitcast_convert_type(u & jnp.uint32(0xFFFF0000), F32)
    return lo, hi


def _to_span_layout(x):
    s = x.shape[0]
    return x.reshape(s // SPAN, BLK, NRES, -1).transpose(0, 2, 1, 3).reshape(s, -1)


def _from_span_layout(x):
    s = x.shape[0]
    return x.reshape(s // SPAN, NRES, BLK, -1).transpose(0, 2, 1, 3).reshape(s, -1)


def _inproj_body(h_ref, g_ref, w_ref, o_ref):
    xn = _rms(h_ref[...], g_ref[...]).astype(BF16)
    o_ref[...] = _dot(xn, w_ref[...]).astype(BF16)


def _inproj(h, g, w):
    s, d = h.shape
    n = w.shape[1]
    tm = min(ROW_TILE, s)
    return pl.pallas_call(
        _inproj_body,
        out_shape=jax.ShapeDtypeStruct((s, n), BF16),
        grid=(s // tm,),
        in_specs=[pl.BlockSpec((tm, d), lambda i: (i, 0)),
                  pl.BlockSpec((1, d), lambda i: (0, 0)),
                  pl.BlockSpec((d, n), lambda i: (0, 0))],
        out_specs=pl.BlockSpec((tm, n), lambda i: (i, 0)),
        compiler_params=_params(("parallel",)),
        name="inproj",
    )(h, g.reshape(1, d), w)


def _t5_bucket(n):
    max_exact = NUM_BUCKETS // 2
    nf = jnp.maximum(n, 1).astype(F32)
    large = max_exact + (jnp.log(nf / max_exact) / math.log(REL_MAX_DIST / max_exact)
                         * (NUM_BUCKETS - max_exact)).astype(jnp.int32)
    large = jnp.minimum(large, NUM_BUCKETS - 1)
    return jnp.where(n < max_exact, n, large)


def _attn_bias(rel_bias, window, dil, perm):
    steps = window // dil
    perm = jnp.asarray(perm, jnp.int32)
    qi = perm[:, None]
    ki = jnp.concatenate([perm, BLK + perm])[None, :]
    dist = BLK + qi - ki
    in_win = (dist >= 0) & (dist <= steps)
    bucket = _t5_bucket(jnp.clip(dist, 0, steps) * dil)
    onehot = (bucket[:, :, None] == jnp.arange(NUM_BUCKETS, dtype=jnp.int32)).astype(F32)
    bias = jnp.einsum('qkb,bh->hqk', onehot, rel_bias.astype(F32), precision=lax.Precision.HIGHEST)
    bias = jnp.where(in_win[None], bias * LOG2E, NEG_INF)
    return bias.reshape(H_A // 2, 2 * BLK, 2 * BLK)


_PERM_D1 = [NRES * jl + r for r in range(NRES) for jl in range(BLK // NRES)]
_PERM_D4 = [4 * jl + i for i in range(4) for jl in range(BLK // 4)]
_PERM_D16 = list(range(BLK))


def _attn_body(q_ref, k_ref, v_ref, kp_ref, vp_ref, b1_ref, b4_ref, b16_ref, o_ref, acc, mst, lst):
    has_prev = pl.program_id(0) > 0
    lane = lax.broadcasted_iota(jnp.int32, (1, LANES), 1)
    lo = lane < HD_A
    mlo = lo.astype(BF16)
    mhi = (~lo).astype(BF16)
    col = lax.broadcasted_iota(jnp.int32, (2 * BLK, 2 * BLK), 1)
    ones = jnp.ones((2 * BLK, LANES), BF16)

    def tile(q2, kk, vv, bias, mask_prev):
        qs = jnp.concatenate([q2 * mlo, q2 * mhi], axis=0)
        s = _dot_nt(qs, kk) + bias
        if mask_prev:
            s = jnp.where(jnp.logical_or(has_prev, col >= BLK), s, NEG_INF)
        m = jnp.max(s, axis=-1, keepdims=True)
        e = jnp.exp2((s - m).astype(BF16))
        oa = _dot(e, jnp.concatenate([vv, ones], axis=1))
        o = oa[:, :LANES]
        l = oa[:, LANES:]
        return (jnp.where(lo, m[:BLK], m[BLK:]), jnp.where(lo, l[:BLK], l[BLK:]),
                jnp.where(lo, o[:BLK], o[BLK:]))

    def merge(prev, cur):
        mp, lp, ap = prev
        mc, lc, ac = cur
        mn = jnp.maximum(mp, mc)
        a = jnp.exp2(mp - mn)
        b = jnp.exp2(mc - mn)
        return mn, a * lp + b * lc, a * ap + b * ac

    def cat(xs):
        return jnp.concatenate(xs, axis=0)

    def d16_body(r, _):
        rows = pl.ds(pl.multiple_of(r * BLK, BLK), BLK)
        for hp in range(H_A // 2):
            lanes = slice(hp * LANES, (hp + 1) * LANES)
            kk = cat([kp_ref[rows, lanes], k_ref[rows, lanes]])
            vv = cat([vp_ref[rows, lanes], v_ref[rows, lanes]])
            m2, l2, o2 = tile(q_ref[rows, lanes], kk, vv, b16_ref[hp], True)
            mst[rows, lanes] = m2
            lst[rows, lanes] = l2
            acc[rows, lanes] = o2
        return 0

    lax.fori_loop(0, NRES, d16_body, 0)

    def d4_body(r4, _):
        for b in range(4):
            def chunk_rows(bb):
                return [pl.ds(pl.multiple_of(4 * BLK * i + BLK * r4 + 32 * bb, 32), 32) for i in range(4)]
            rows = chunk_rows(b)
            prows = chunk_rows(3 if b == 0 else b - 1)
            kprev, vprev = (kp_ref, vp_ref) if b == 0 else (k_ref, v_ref)
            for hp in range(H_A // 2):
                lanes = slice(hp * LANES, (hp + 1) * LANES)
                q2 = cat([q_ref[rr, lanes] for rr in rows])
                kk = cat([kprev[rr, lanes] for rr in prows] + [k_ref[rr, lanes] for rr in rows])
                vv = cat([vprev[rr, lanes] for rr in prows] + [v_ref[rr, lanes] for rr in rows])
                cur = tile(q2, kk, vv, b4_ref[hp], b == 0)
                prev = (cat([mst[rr, lanes] for rr in rows]), cat([lst[rr, lanes] for rr in rows]),
                        cat([acc[rr, lanes] for rr in rows]))
                mn, ln, an = merge(prev, cur)
                for i, rr in enumerate(rows):
                    part = slice(32 * i, 32 * (i + 1))
                    mst[rr, lanes] = mn[part]
                    lst[rr, lanes] = ln[part]
                    acc[rr, lanes] = an[part]
        return 0

    lax.fori_loop(0, 4, d4_body, 0)

    def d1_pair(ap, kprev, vprev, prev_ap, mask_prev):
        def tiles(a_):
            return [pl.ds(pl.multiple_of(BLK * r + 16 * a_, 16), 16) for r in range(NRES)]
        cur_t = tiles(ap)
        prev_t = tiles(prev_ap)

        def halves(ref, ts, lanes):
            xs = [ref[t, lanes].astype(F32) for t in ts]
            return cat([x[:8] for x in xs]).astype(BF16), cat([x[8:] for x in xs]).astype(BF16)

        for hp in range(H_A // 2):
            lanes = slice(hp * LANES, (hp + 1) * LANES)
            q_e, q_o = halves(q_ref, cur_t, lanes)
            k_e, k_o = halves(k_ref, cur_t, lanes)
            v_e, v_o = halves(v_ref, cur_t, lanes)
            _, k_p = halves(kprev, prev_t, lanes)
            _, v_p = halves(vprev, prev_t, lanes)
            cur_e = tile(q_e, cat([k_p, k_e]), cat([v_p, v_e]), b1_ref[hp], mask_prev)
            cur_o = tile(q_o, cat([k_e, k_o]), cat([v_e, v_o]), b1_ref[hp], False)
            ms = [mst[t, lanes] for t in cur_t]
            ls = [lst[t, lanes] for t in cur_t]
            ac = [acc[t, lanes] for t in cur_t]
            outs = []
            for half, cur in ((0, cur_e), (1, cur_o)):
                part = slice(8 * half, 8 * half + 8)
                prev = (cat([x[part] for x in ms]), cat([x[part] for x in ls]), cat([x[part] for x in ac]))
                _, ln, an = merge(prev, cur)
                outs.append(an / ln)
            for r, t in enumerate(cur_t):
                part = slice(8 * r, 8 * r + 8)
                o_ref[t, lanes] = cat([outs[0][part], outs[1][part]]).astype(o_ref.dtype)

    d1_pair(0, kp_ref, vp_ref, BLK // 16 - 1, True)

    def d1_body(ap, _):
        d1_pair(ap, k_ref, v_ref, ap - 1, False)
        return 0

    lax.fori_loop(1, BLK // 16, d1_body, 0)


def _dilated_attention(proj, biases):
    s = proj.shape[0]
    cur = lambda which: pl.BlockSpec((SPAN, D_A), lambda c: (c, which))
    prev = lambda which: pl.BlockSpec((SPAN, D_A), lambda c: (jnp.maximum(c - 1, 0), which))
    return pl.pallas_call(
        _attn_body,
        out_shape=jax.ShapeDtypeStruct((s, D_A), BF16),
        grid=(s // SPAN,),
        in_specs=[cur(0), cur(1), cur(2), prev(1), prev(2)] + [_full(b) for b in biases],
        out_specs=pl.BlockSpec((SPAN, D_A), lambda c: (c, 0)),
        scratch_shapes=[pltpu.VMEM((SPAN, D_A), F32)] * 3,
        compiler_params=_params(("arbitrary",)),
        name="attn",
    )(proj, proj, proj, proj, proj, *biases)


def _s5_operators(a_re, a_im, b_re, b_im, c_re, c_im, log_dt, d_skip):
    L = S5_CHUNK
    lam = lax.complex(a_re.astype(F32), a_im.astype(F32))
    dt = jnp.exp(log_dt.astype(F32))[:, None]
    a_bar = jnp.exp(lam * dt)
    b_bar = ((a_bar - 1.0) / lam)[..., None] * lax.complex(b_re.astype(F32), b_im.astype(F32))
    c = lax.complex(c_re.astype(F32), c_im.astype(F32))
    j = jnp.arange(L + 1, dtype=F32)
    log_a = lam * dt
    apow = jnp.exp(log_a[None] * j[:, None, None])
    kj = jnp.einsum('gdp,jgp,gpc->jgdc', c, apow[:L], b_bar).real
    s_idx = jnp.arange(L)[:, None]
    t_idx = jnp.arange(L)[None, :]
    lag = t_idx - s_idx
    lag_onehot = (lag[None] == jnp.arange(L)[:, None, None]).astype(F32)
    m = jnp.einsum('jst,jgdc->stgdc', lag_onehot, kj, precision=lax.Precision.HIGHEST)
    m = m.transpose(2, 0, 4, 1, 3)
    eye = jnp.eye(L)[:, None, :, None] * jnp.eye(S5_CH)[None, :, None, :]
    m = m + eye[None] * d_skip.astype(F32).reshape(S5_G, 1, S5_CH, 1, 1)
    m = m.reshape(S5_G, L * S5_CH, L * S5_CH)
    p = jnp.einsum('sgp,gpc->gscp', apow[:L][::-1], b_bar).reshape(S5_G, L * S5_CH, S5_P)
    ca = jnp.einsum('gdp,tgp->gptd', c, apow[1:L + 1]).reshape(S5_G, S5_P, L * S5_CH)
    a_l = apow[L]

    def pair_blocks(x):
        g, r, w = x.shape
        x = x.reshape(S5_PAIRS, 2, r, w)
        z = jnp.zeros_like(x[:, 0])
        top = jnp.concatenate([x[:, 0], z], axis=-1)
        bot = jnp.concatenate([z, x[:, 1]], axis=-1)
        return jnp.concatenate([top, bot], axis=1)

    p2 = jnp.concatenate([pair_blocks(p.real), pair_blocks(p.imag)], axis=-1)
    q2 = jnp.concatenate([pair_blocks(ca.real), pair_blocks(-ca.imag)], axis=1)
    a_lr = a_l.real.reshape(1, S5_G * S5_P)
    a_li = a_l.imag.reshape(1, S5_G * S5_P)
    return m.astype(BF16), p2.astype(BF16), q2.astype(BF16), a_lr, a_li


def _s5_body(u_ref, m_ref, p_ref, q_ref, ar_ref, ai_ref, o_ref, w_ref, y_ref, ere, eim, xre, xim, sre, sim):
    rows = BLK
    gw = S5_CHUNK * S5_CH
    pw = 2 * gw
    per_vreg = LANES // S5_CH
    chunk_of_lane = lax.broadcasted_iota(jnp.int32, (rows, LANES), 1) // S5_CH

    def regroup(srcs, k):
        acc = None
        for i, src in enumerate(srcs):
            shift = ((i - k) * S5_CH) % LANES
            moved = pltpu.roll(src, shift, axis=1) if shift else src
            acc = moved if acc is None else jnp.where(chunk_of_lane == i, moved, acc)
        return acc

    @pl.when(pl.program_id(0) == 0)
    def _():
        sre[...] = jnp.zeros_like(sre)
        sim[...] = jnp.zeros_like(sim)

    for b in range(D_B // LANES):
        for a in range(S5_CHUNK // per_vreg):
            srcs = [u_ref[(per_vreg * a + i) * rows:(per_vreg * a + i + 1) * rows,
                          b * LANES:(b + 1) * LANES].astype(F32) for i in range(per_vreg)]
            for gi in range(per_vreg):
                g = per_vreg * b + gi
                w_ref[:, g * gw + a * LANES:g * gw + (a + 1) * LANES] = regroup(srcs, gi).astype(BF16)

    for pr in range(S5_PAIRS):
        e = _dot(w_ref[:, pr * pw:(pr + 1) * pw], p_ref[pr])
        ere[:, pr * LANES:(pr + 1) * LANES] = e[:, :LANES]
        eim[:, pr * LANES:(pr + 1) * LANES] = e[:, LANES:]

    ar = ar_ref[...]
    ai = ai_ref[...]

    def step(n, carry):
        xr, xi = carry
        xre[pl.ds(n, 1), :] = xr
        xim[pl.ds(n, 1), :] = xi
        nr = ar * xr - ai * xi + ere[pl.ds(n, 1), :]
        ni = ar * xi + ai * xr + eim[pl.ds(n, 1), :]
        return nr, ni

    xr, xi = lax.fori_loop(0, rows, step, (sre[...], sim[...]))
    sre[...] = xr
    sim[...] = xi

    for pr in range(S5_PAIRS):
        xin = jnp.concatenate([xre[:, pr * LANES:(pr + 1) * LANES],
                               xim[:, pr * LANES:(pr + 1) * LANES]], axis=-1).astype(BF16)
        yc = _dot(xin, q_ref[pr])
        for half in range(2):
            g = 2 * pr + half
            cols = slice(g * gw, (g + 1) * gw)
            y = _dot(w_ref[:, cols], m_ref[g]) + yc[:, half * gw:(half + 1) * gw]
            y_ref[:, cols] = 0.5 * y * (1.0 + lax.erf(y * (2.0 ** -0.5)))

    for b in range(D_B // LANES):
        for a in range(S5_CHUNK // per_vreg):
            srcs = [y_ref[:, (per_vreg * b + i) * gw + a * LANES:(per_vreg * b + i) * gw + (a + 1) * LANES]
                    for i in range(per_vreg)]
            for ri in range(per_vreg):
                r = per_vreg * a + ri
                o_ref[r * rows:(r + 1) * rows, b * LANES:(b + 1) * LANES] = regroup(srcs, ri).astype(BF16)


def _s5_core(proj, ops):
    m, p2, q2, a_lr, a_li = ops
    s = proj.shape[0]
    wide = S5_G * S5_CHUNK * S5_CH
    nstate = S5_G * S5_P
    return pl.pallas_call(
        _s5_body,
        out_shape=jax.ShapeDtypeStruct((s, D_B), BF16),
        grid=(s // SPAN,),
        in_specs=[pl.BlockSpec((SPAN, D_B), lambda i: (i, 3 * D_A // D_B)),
                  _full(m), _full(p2), _full(q2), _full(a_lr), _full(a_li)],
        out_specs=pl.BlockSpec((SPAN, D_B), lambda i: (i, 0)),
        scratch_shapes=[pltpu.VMEM((BLK, wide), BF16), pltpu.VMEM((BLK, wide), F32)]
                       + [pltpu.VMEM((BLK, nstate), F32)] * 4 + [pltpu.VMEM((1, nstate), F32)] * 2,
        compiler_params=_params(("arbitrary",)),
        name="s5",
    )(proj, m, p2, q2, a_lr, a_li)


def _split_bf16(x):
    hi = x.astype(BF16)
    lo = (x - hi.astype(F32)).astype(BF16)
    return hi, lo


def _mid_body(h_ref, ya_ref, yb_ref, wglu_ref, bglu_ref, ga_ref, gb_ref, wout_ref, gx_ref, wq_ref,
              k_ref, v_ref, wo_ref, gm_ref, wr_ref, br_ref, h_out, xn_out, idx_out, gate_out):
    part = h_ref.shape[0] // MID_CHAINS
    for c in range(MID_CHAINS):
        rows = slice(c * part, (c + 1) * part)
        outs = _mid_rows(h_ref[rows, :], ya_ref[rows, :], yb_ref[rows, :], wglu_ref, bglu_ref, ga_ref, gb_ref,
                         wout_ref, gx_ref, wq_ref, k_ref, v_ref, wo_ref, gm_ref, wr_ref, br_ref)
        for ref, val in zip((h_out, xn_out, idx_out, gate_out), outs):
            ref[rows, :] = val


def _mid_rows(h, ya, yb, wglu_ref, bglu_ref, ga_ref, gb_ref, wout_ref, gx_ref, wq_ref,
              k_ref, v_ref, wo_ref, gm_ref, wr_ref, br_ref):
    gate = jax.nn.sigmoid(_dot(yb, wglu_ref[...]) + bglu_ref[...])
    yb2 = yb.astype(F32) * gate
    na = _rms(ya.astype(F32), ga_ref[...]).astype(BF16)
    nb = _rms(yb2, gb_ref[...]).astype(BF16)
    h1 = h + _dot(na, wout_ref[0:D_A, :]) + _dot(nb, wout_ref[D_A:D_MODEL, :])
    q = _dot(_rms(h1, gx_ref[...]).astype(BF16), wq_ref[...]).astype(BF16)
    heads = []
    for hd in range(H_X):
        lanes = slice(hd * HD_X, (hd + 1) * HD_X)
        s = _dot_nt(q[:, lanes], k_ref[:, lanes])
        e = jnp.exp(s - jnp.max(s, axis=-1, keepdims=True))
        heads.append(_dot(e.astype(BF16), v_ref[:, lanes]) / jnp.sum(e, axis=-1, keepdims=True))
    o = jnp.concatenate(heads, axis=-1).astype(BF16)
    h2 = h1 + _dot(o, wo_ref[...])
    xn = _rms(h2, gm_ref[...])
    x_hi, x_lo = _split_bf16(xn)
    logits = _dot(x_hi, wr_ref[0]) + _dot(x_lo, wr_ref[0]) + _dot(x_hi, wr_ref[1]) + br_ref[...]
    lane = lax.broadcasted_iota(jnp.int32, logits.shape, 1)
    logits = jnp.where(lane < N_EXPERTS, logits, -jnp.inf)
    vals, idxs = [], []
    for _ in range(TOP_K):
        mx = jnp.max(logits, axis=-1, keepdims=True)
        ix = jnp.min(jnp.where(logits == mx, lane, LANES), axis=-1, keepdims=True)
        vals.append(mx)
        idxs.append(ix)
        logits = jnp.where(lane == ix, -jnp.inf, logits)
    es = [jnp.exp(v - vals[0]) for v in vals]
    den = es[0] + es[1] + es[2] + es[3]
    idx_t = jnp.full(lane.shape, N_EXPERTS, jnp.int32)
    gate_t = jnp.zeros(lane.shape, F32)
    for k in range(TOP_K):
        idx_t = jnp.where(lane == k, idxs[k], idx_t)
        gate_t = jnp.where(lane == k, es[k] / den, gate_t)
    return h2, _pack_rows(xn), idx_t, gate_t


def _mid(h, ya, yb, wglu, bglu, ga, gb, wout, gx, wq, kmem, vmem, wo, gm, wr2, br):
    s = h.shape[0]
    tm = min(ROW_TILE, s)
    row = lambda w: pl.BlockSpec((tm, w), lambda i: (i, 0))
    consts = [wglu, bglu, ga, gb, wout, gx, wq, kmem, vmem, wo, gm, wr2, br]
    return pl.pallas_call(
        _mid_body,
        out_shape=[jax.ShapeDtypeStruct((s, D_MODEL), F32), jax.ShapeDtypeStruct((s, D_MODEL // 2), jnp.int32),
                   jax.ShapeDtypeStruct((s, LANES), jnp.int32), jax.ShapeDtypeStruct((s, LANES), F32)],
        grid=(s // tm,),
        in_specs=[row(D_MODEL), row(D_A), row(D_B)] + [_full(a) for a in consts],
        out_specs=[row(D_MODEL), row(D_MODEL // 2), row(LANES), row(LANES)],
        compiler_params=_params(("parallel",)),
        name="mid",
    )(h, ya, yb, *consts)


def _memkv_body(mem_ref, g_ref, w_ref, o_ref):
    o_ref[...] = _dot(_rms(mem_ref[...], g_ref[...]).astype(BF16), w_ref[...]).astype(BF16)


def _memkv(mem, g, w):
    n, d = mem.shape
    return pl.pallas_call(
        _memkv_body,
        out_shape=jax.ShapeDtypeStruct((n, w.shape[1]), BF16),
        compiler_params=pltpu.CompilerParams(vmem_limit_bytes=VMEM_LIMIT),
        name="memkv",
    )(mem, g.reshape(1, d), w)


def _rank_body(idx_ref, rank_ref, cnt_ref, carry):
    @pl.when(pl.program_id(0) == 0)
    def _():
        carry[...] = jnp.zeros_like(carry)

    tm = idx_ref.shape[0]
    lane = lax.broadcasted_iota(jnp.int32, (tm, LANES), 1)
    idx = idx_ref[...]
    hits = [lane == idx[:, k:k + 1] for k in range(TOP_K)]
    onehot = jnp.zeros((tm, LANES), F32)
    for hit in hits:
        onehot = onehot + jnp.where(hit, 1.0, 0.0)
    ri = lax.broadcasted_iota(jnp.int32, (tm, tm), 0)
    ci = lax.broadcasted_iota(jnp.int32, (tm, tm), 1)
    tri = jnp.where(ri >= ci, 1.0, 0.0).astype(BF16)
    inclusive = _dot(tri, onehot.astype(BF16))
    before = carry[...] + inclusive - onehot
    rank = jnp.zeros((tm, LANES), jnp.int32)
    for k, hit in enumerate(hits):
        rk = jnp.sum(jnp.where(hit, before, 0.0), axis=-1, keepdims=True)
        rank = jnp.where(lane == k, rk.astype(jnp.int32), rank)
    rank_ref[...] = rank
    total = carry[...] + inclusive[tm - 1:tm, :]
    carry[...] = total
    cnt_ref[...] = total.astype(jnp.int32)


def _rank(idx):
    t = idx.shape[0]
    tm = min(ROW_TILE, t)
    return pl.pallas_call(
        _rank_body,
        out_shape=[jax.ShapeDtypeStruct((t, LANES), jnp.int32), jax.ShapeDtypeStruct((1, LANES), jnp.int32)],
        grid=(t // tm,),
        in_specs=[pl.BlockSpec((tm, LANES), lambda i: (i, 0))],
        out_specs=[pl.BlockSpec((tm, LANES), lambda i: (i, 0)), pl.BlockSpec((1, LANES), lambda i: (0, 0))],
        scratch_shapes=[pltpu.VMEM((1, LANES), F32)],
        compiler_params=_params(("arbitrary",)),
        name="rank",
    )(idx)


def _route(idx, tm):
    t = idx.shape[0]
    tk = t * TOP_K
    rank, cnt = _rank(idx)
    counts = cnt[0, :N_EXPERTS]
    padded = (counts + tm - 1) // tm * tm
    pend = jnp.cumsum(padded)
    pstart = pend - padded
    n_rows = tk + N_EXPERTS * tm
    n_tiles = n_rows // tm
    experts = jnp.arange(N_EXPERTS, dtype=jnp.int32)
    tile_first = jnp.arange(n_tiles, dtype=jnp.int32) * tm
    last_used = jnp.max(jnp.where(padded > 0, experts, 0))
    tile_e = jnp.minimum(jnp.sum(tile_first[:, None] >= pend[None, :], axis=1), last_used).astype(jnp.int32)
    tile_rows = jnp.clip(jnp.sum(jnp.where(tile_e[:, None] == experts[None, :],
                                           (pstart + counts)[None, :], 0), axis=1) - tile_first, 0, tm)
    tile_rows = jnp.where(tile_first < pend[-1], tile_rows, 0).astype(jnp.int32)
    group = jnp.cumsum(jnp.concatenate([jnp.zeros((1,), jnp.int32),
                                        (tile_e[1:] != tile_e[:-1]).astype(jnp.int32)]))
    tile_slot = (group % 2).astype(jnp.int32)
    later = (experts[None, :] > experts[:, None]) & (padded > 0)[None, :]
    next_e = jnp.min(jnp.where(later, experts[None, :], N_EXPERTS), axis=1)
    next_e = jnp.where(next_e < N_EXPERTS, next_e, -1).astype(jnp.int32)
    tile_next = jnp.sum(jnp.where(tile_e[:, None] == experts[None, :], next_e[None, :], 0), axis=1).astype(jnp.int32)
    top = idx[:, :TOP_K]
    base = jnp.sum(jnp.where(top[:, :, None] == experts, pstart, 0), axis=-1)
    dest = (rank[:, :TOP_K] + base).T
    return dest, n_rows, tile_e, tile_rows, tile_slot, tile_next


def _sc_mesh():
    return plsc.VectorSubcoreMesh(core_axis_name="c", subcore_axis_name="s",
                                  num_cores=SC_CORES, num_subcores=SC_SUBCORES)


def _sc_worker():
    return lax.axis_index("s") * SC_CORES + lax.axis_index("c")


def _scatter_rows(x, dest, n_rows):
    t, d = x.shape
    per_worker = t // (SC_CORES * SC_SUBCORES)
    chunks = per_worker // SC_ROWS

    @functools.partial(
        pl.kernel, mesh=_sc_mesh(),
        out_type=jax.ShapeDtypeStruct((n_rows, d), x.dtype),
        scratch_types=[pltpu.VMEM((SC_ROWS, d), x.dtype)] + [pltpu.VMEM((SC_ROWS,), jnp.int32)] * TOP_K
                      + [pltpu.SemaphoreType.DMA((TOP_K,))],
    )
    def scatter(x_hbm, *rest):
        dest_hbm, out_hbm, rows_v = rest[:TOP_K], rest[TOP_K], rest[TOP_K + 1]
        idx_v, sem = rest[TOP_K + 2:2 * TOP_K + 2], rest[2 * TOP_K + 2]
        base = _sc_worker() * per_worker

        @pl.loop(0, chunks)
        def _(c):
            off = pl.multiple_of(base + c * SC_ROWS, SC_ROWS)
            pltpu.sync_copy(x_hbm.at[pl.ds(off, SC_ROWS)], rows_v)
            for k in range(TOP_K):
                pltpu.sync_copy(dest_hbm[k].at[pl.ds(off, SC_ROWS)], idx_v[k])
            copies = [pltpu.async_copy(rows_v, out_hbm.at[idx_v[k]], sem.at[k]) for k in range(TOP_K)]
            for cp in copies:
                cp.wait()

    return scatter(x, *[dest[k] for k in range(TOP_K)])


def _gather_rows(table, idx):
    n, d = table.shape
    b = idx.shape[0]
    per_worker = b // (SC_CORES * SC_SUBCORES)
    chunks = per_worker // SC_ROWS
    assert chunks % 2 == 0

    @functools.partial(
        pl.kernel, mesh=_sc_mesh(),
        out_type=jax.ShapeDtypeStruct((b, d), table.dtype),
        scratch_types=[pltpu.VMEM((SC_ROWS,), jnp.int32)] * 2 + [pltpu.VMEM((SC_ROWS, d), table.dtype)] * 2
                      + [pltpu.SemaphoreType.DMA((2,)), pltpu.SemaphoreType.DMA((2,))],
    )
    def gather(table_hbm, idx_hbm, out_hbm, idx0, idx1, rows0, rows1, gsem, wsem):
        idx_v, rows_v = (idx0, idx1), (rows0, rows1)
        base = _sc_worker() * per_worker

        def rows_at(c):
            return pl.ds(pl.multiple_of(base + c * SC_ROWS, SC_ROWS), SC_ROWS)

        def fetch(c, s):
            pltpu.sync_copy(idx_hbm.at[rows_at(c)], idx_v[s])
            pltpu.async_copy(table_hbm.at[idx_v[s]], rows_v[s], gsem.at[s])

        def fetched(s):
            return pltpu.make_async_copy(table_hbm.at[idx_v[s]], rows_v[s], gsem.at[s])

        def written(c, s):
            return pltpu.make_async_copy(rows_v[s], out_hbm.at[rows_at(c)], wsem.at[s])

        fetch(0, 0)

        @pl.loop(0, chunks, step=2)
        def _(c0):
            for s in range(2):
                c = c0 + s
                fetched(s).wait()
                written(c, s).start()

                @pl.when(c >= 1)
                def _():
                    written(c - 1, 1 - s).wait()

                @pl.when(c + 1 < chunks)
                def _():
                    fetch(c + 1, 1 - s)

        written(chunks - 1, 1).wait()

    return gather(table, idx)


def _moe_body(te_ref, tv_ref, sl_ref, nx_ref, x_ref, w1_hbm, b1_ref, w2_hbm, b2_ref, o_ref,
              w1f, w2f, w1b, w2b, sem, *, layer):
    i = pl.program_id(0)
    e = te_ref[i]
    slot = sl_ref[i]
    new_expert = (i == 0) | (e != te_ref[jnp.maximum(i - 1, 0)])

    def weight_copies(expert, s):
        rows1 = w1f.shape[1] // MOE_W1_PARTS
        rows2 = w2f.shape[1] // MOE_W2_PARTS
        c1 = [pltpu.make_async_copy(w1_hbm.at[layer, expert, pl.ds(q * rows1, rows1)],
                                    w1f.at[s, pl.ds(q * rows1, rows1)], sem.at[s, q])
              for q in range(MOE_W1_PARTS)]
        c2 = [pltpu.make_async_copy(w2_hbm.at[layer, expert, pl.ds(q * rows2, rows2)],
                                    w2f.at[s, pl.ds(q * rows2, rows2)], sem.at[s, MOE_W1_PARTS + q])
              for q in range(MOE_W2_PARTS)]
        return c1 + c2

    @pl.when(i == 0)
    def _():
        for c in weight_copies(e, slot):
            c.start()

    @pl.when(new_expert)
    def _():
        for c in weight_copies(e, slot):
            c.wait()
        nxt = nx_ref[i]

        @pl.when(nxt >= 0)
        def _():
            for c in weight_copies(nxt, 1 - slot):
                c.start()

        w1b[...] = w1f[slot].astype(BF16)
        w2b[...] = w2f[slot].astype(BF16)

    @pl.when(tv_ref[i] > 0)
    def _():
        row = lax.broadcasted_iota(jnp.int32, x_ref.shape, 0)
        lo, hi = _unpack_rows(jnp.where(row < tv_ref[i], x_ref[...], 0))
        x = jnp.concatenate([lo, hi], axis=-1).astype(BF16)
        hb = _dot(x, w1b[...]) + b1_ref[0]
        x_glu = jnp.minimum(hb[:, :D_FF], SWIGLU_LIMIT)
        x_lin = jnp.clip(hb[:, D_FF:], -SWIGLU_LIMIT, SWIGLU_LIMIT)
        act = x_glu * jax.nn.sigmoid(SWIGLU_ALPHA * x_glu) * (x_lin + 1.0)
        o_ref[...] = _pack_rows(_dot(act.astype(BF16), w2b[...]) + b2_ref[0])

    @pl.when(tv_ref[i] == 0)
    def _():
        o_ref[...] = jnp.zeros_like(o_ref)


def _moe_experts(xs, tile_e, tile_rows, tile_slot, tile_next, w1, b1, w2, b2, layer):
    n_rows = xs.shape[0]
    tm = MOE_TILE
    nl, ne, d, ff2 = w1.shape
    bias_map = lambda i, te, tv, sl, nx: (layer, te[i], 0, 0)
    grid_spec = pltpu.PrefetchScalarGridSpec(
        num_scalar_prefetch=4,
        grid=(n_rows // tm,),
        in_specs=[pl.BlockSpec((tm, d // 2), lambda i, *_: (i, 0)),
                  pl.BlockSpec(memory_space=pl.ANY),
                  pl.BlockSpec((None, 1, 1, ff2), bias_map),
                  pl.BlockSpec(memory_space=pl.ANY),
                  pl.BlockSpec((None, 1, 1, d), bias_map)],
        out_specs=pl.BlockSpec((tm, d // 2), lambda i, *_: (i, 0)),
        scratch_shapes=[pltpu.VMEM((2, d, ff2), F32), pltpu.VMEM((2, ff2 // 2, d), F32),
                        pltpu.VMEM((d, ff2), BF16), pltpu.VMEM((ff2 // 2, d), BF16),
                        pltpu.SemaphoreType.DMA((2, MOE_W1_PARTS + MOE_W2_PARTS))],
    )
    return pl.pallas_call(
        functools.partial(_moe_body, layer=layer),
        out_shape=jax.ShapeDtypeStruct((n_rows, d // 2), jnp.int32),
        grid_spec=grid_spec,
        compiler_params=_params(("arbitrary",)),
        name="moe",
    )(tile_e, tile_rows, tile_slot, tile_next, xs, w1, b1.reshape(nl, ne, 1, ff2), w2, b2.reshape(nl, ne, 1, d))


def _combine_body(h_ref, y0_ref, y1_ref, y2_ref, y3_ref, gate_ref, g_ref, o_ref, *, final):
    gates = gate_ref[...]
    lo = jnp.zeros(y0_ref.shape, F32)
    hi = jnp.zeros(y0_ref.shape, F32)
    for k, y_ref in enumerate((y0_ref, y1_ref, y2_ref, y3_ref)):
        yl, yh = _unpack_rows(y_ref[...])
        lo = lo + yl * gates[:, k:k + 1]
        hi = hi + yh * gates[:, k:k + 1]
    h = h_ref[...] + jnp.concatenate([lo, hi], axis=-1)
    o_ref[...] = _rms(h, g_ref[...]) if final else h


def _combine(h, y, gates, g, *, final):
    s, d = h.shape
    tm = min(ROW_TILE // 2, s)
    nt = s // tm
    return pl.pallas_call(
        functools.partial(_combine_body, final=final),
        out_shape=jax.ShapeDtypeStruct((s, d), F32),
        grid=(nt,),
        in_specs=[pl.BlockSpec((tm, d), lambda i: (i, 0))]
                 + [pl.BlockSpec((tm, d // 2), lambda i, k=k: (k * nt + i, 0)) for k in range(TOP_K)]
                 + [pl.BlockSpec((tm, LANES), lambda i: (i, 0)), pl.BlockSpec((1, d), lambda i: (0, 0))],
        out_specs=pl.BlockSpec((tm, d), lambda i: (i, 0)),
        compiler_params=_params(("parallel",)),
        name="combine",
    )(h, y, y, y, y, gates, g.reshape(1, d))


def _layer(h, mem, biases, p, l, g_final):
    row = lambda a: a.reshape(1, -1).astype(F32)
    q_scale = jnp.concatenate([jnp.full((D_A,), HD_A ** -0.5 * LOG2E, F32), jnp.ones((D_IN - D_A,), F32)])
    w_in = (p["w_in"][l] * q_scale).astype(BF16)
    proj = _inproj(h, p["norm_mix"][l], w_in)
    ya = _dilated_attention(proj, biases)
    ops = _s5_operators(p["s5_a_re"][l], p["s5_a_im"][l], p["s5_b_re"][l], p["s5_b_im"][l],
                        p["s5_c_re"][l], p["s5_c_im"][l], p["s5_log_dt"][l], p["s5_d"][l])
    yb = _s5_core(proj, ops)
    kv = _memkv(mem, p["norm_mem"][l], p["w_xkv"][l].astype(BF16))
    wr = jnp.pad(p["w_router"][l].astype(F32), ((0, 0), (0, LANES - N_EXPERTS)))
    wr_hi = wr.astype(BF16)
    wr2 = jnp.stack([wr_hi, (wr - wr_hi.astype(F32)).astype(BF16)])
    br = jnp.pad(p["b_router"][l].astype(F32), (0, LANES - N_EXPERTS)).reshape(1, LANES)
    h2, xn, idx, gates = _mid(
        h, ya, yb, p["w_glu"][l].astype(BF16), row(p["b_glu"][l]), row(p["g_out_attn"][l]),
        row(p["g_out_ssm"][l]), p["w_out"][l].astype(BF16), row(p["norm_xattn"][l]),
        (p["w_xq"][l] * (HD_X ** -0.5)).astype(BF16), kv[:, :D_X], kv[:, D_X:],
        p["w_xo"][l].astype(BF16), row(p["norm_moe"][l]), wr2, br)
    dest, n_rows, tile_e, tile_rows, tile_slot, tile_next = _route(idx, MOE_TILE)
    out = _moe_experts(_scatter_rows(xn, dest, n_rows), tile_e, tile_rows, tile_slot, tile_next,
                       p["w1"], p["b1"], p["w2"], p["b2"], l)
    return _combine(h2, _gather_rows(out, dest.reshape(-1)), gates, g_final, final=l == DEPTH - 1)


def kernel(x, mem, rel_bias, norm_mix, w_in, s5_a_re, s5_a_im, s5_b_re, s5_b_im, s5_c_re, s5_c_im, s5_log_dt, s5_d, w_glu, b_glu, g_out_attn, g_out_ssm, w_out, norm_xattn, norm_mem, w_xq, w_xkv, w_xo, norm_moe, w_router, b_router, w1, b1, w2, b2, norm_final):
    p = dict(norm_mix=norm_mix, w_in=w_in, s5_a_re=s5_a_re, s5_a_im=s5_a_im, s5_b_re=s5_b_re,
             s5_b_im=s5_b_im, s5_c_re=s5_c_re, s5_c_im=s5_c_im, s5_log_dt=s5_log_dt, s5_d=s5_d,
             w_glu=w_glu, b_glu=b_glu, g_out_attn=g_out_attn, g_out_ssm=g_out_ssm, w_out=w_out,
             norm_xattn=norm_xattn, norm_mem=norm_mem, w_xq=w_xq, w_xkv=w_xkv, w_xo=w_xo,
             norm_moe=norm_moe, w_router=w_router, b_router=b_router, w1=w1, b1=b1, w2=w2, b2=b2)
    biases = [_attn_bias(rel_bias, window, dil, perm)
              for (window, dil), perm in zip(WIN_DIL, (_PERM_D1, _PERM_D4, _PERM_D16))]
    outs = []
    for b in range(x.shape[0]):
        h = _to_span_layout(x[b])
        for l in range(DEPTH):
            h = _layer(h, mem[b], biases, p, l, norm_final)
        outs.append(_from_span_layout(h))
    return jnp.stack(outs)
```

```python
import functools
import math

import jax
import jax.numpy as jnp
from jax import lax
from jax.experimental import pallas as pl
from jax.experimental.pallas import tpu as pltpu
from jax.experimental.pallas import tpu_sc as plsc

F32 = jnp.float32
BF16 = jnp.bfloat16

D_MODEL = 1024
DEPTH = 2
EPS = 1e-5
NEG_INF = -1e30
LOG2E = math.log2(math.e)
H_A = 8
HD_A = 64
D_A = H_A * HD_A
WIN_DIL = ((128, 1), (512, 4), (2048, 16))
BLK = 128
D_B = D_MODEL - D_A
S5_CH = 16
S5_G = D_B // S5_CH
S5_P = 64
D_IN = 3 * D_A + D_B
NUM_BUCKETS = 32
REL_MAX_DIST = 2048
H_X = 4
HD_X = 128
D_X = H_X * HD_X
N_EXPERTS = 32
TOP_K = 4
D_FF = D_MODEL
SWIGLU_ALPHA = 1.702
SWIGLU_LIMIT = 7.0

LANES = 128
NRES = WIN_DIL[-1][1]
SPAN = NRES * BLK
S5_CHUNK = NRES
S5_PAIRS = S5_G // 2
VMEM_LIMIT = 56 * 1024 * 1024

SC_CORES = 2
SC_SUBCORES = 16
SC_ROWS = 64

ROW_TILE = 512
MID_CHAINS = 2
MOE_TILE = 512
MOE_W1_PARTS = 4
MOE_W2_PARTS = 2


def _params(sem):
    return pltpu.CompilerParams(dimension_semantics=sem, vmem_limit_bytes=VMEM_LIMIT)


def _rms(x, g):
    return x * lax.rsqrt(jnp.mean(x * x, axis=-1, keepdims=True) + EPS) * g


def _dot(a, b):
    return jnp.dot(a, b, preferred_element_type=F32)


def _dot_nt(a, b):
    return lax.dot_general(a, b, (((1,), (1,)), ((), ())), preferred_element_type=F32)


def _full(a):
    return pl.BlockSpec(a.shape, lambda *_: (0,) * a.ndim)


def _pack_rows(x):
    c = x.shape[1] // 2
    lo = lax.bitcast_convert_type(x[:, :c].astype(BF16).astype(F32), jnp.uint32)
    hi = lax.bitcast_convert_type(x[:, c:].astype(BF16).astype(F32), jnp.uint32)
    return lax.bitcast_convert_type(lax.shift_right_logical(lo, jnp.uint32(16)) | hi, jnp.int32)


def _unpack_rows(p):
    u = lax.bitcast_convert_type(p, jnp.uint32)
    lo = lax.bitcast_convert_type(lax.shift_left(u, jnp.uint32(16)), F32)
    hi = lax.bitcast_convert_type(u & jnp.uint32(0xFFFF0000), F32)
    return lo, hi


def _to_span_layout(x):
    s = x.shape[0]
    return x.reshape(s // SPAN, BLK, NRES, -1).transpose(0, 2, 1, 3).reshape(s, -1)


def _from_span_layout(x):
    s = x.shape[0]
    return x.reshape(s // SPAN, NRES, BLK, -1).transpose(0, 2, 1, 3).reshape(s, -1)


def _inproj_body(h_ref, g_ref, w_ref, o_ref):
    xn = _rms(h_ref[...], g_ref[...]).astype(BF16)
    o_ref[...] = _dot(xn, w_ref[...]).astype(BF16)


def _inproj(h, g, w):
    s, d = h.shape
    n = w.shape[1]
    tm = min(ROW_TILE, s)
    return pl.pallas_call(
        _inproj_body,
        out_shape=jax.ShapeDtypeStruct((s, n), BF16),
        grid=(s // tm,),
        in_specs=[pl.BlockSpec((tm, d), lambda i: (i, 0)),
                  pl.BlockSpec((1, d), lambda i: (0, 0)),
                  pl.BlockSpec((d, n), lambda i: (0, 0))],
        out_specs=pl.BlockSpec((tm, n), lambda i: (i, 0)),
        compiler_params=_params(("parallel",)),
        name="inproj",
    )(h, g.reshape(1, d), w)


def _t5_bucket(n):
    max_exact = NUM_BUCKETS // 2
    nf = jnp.maximum(n, 1).astype(F32)
    large = max_exact + (jnp.log(nf / max_exact) / math.log(REL_MAX_DIST / max_exact)
                         * (NUM_BUCKETS - max_exact)).astype(jnp.int32)
    large = jnp.minimum(large, NUM_BUCKETS - 1)
    return jnp.where(n < max_exact, n, large)


def _attn_bias(rel_bias, window, dil, perm):
    steps = window // dil
    perm = jnp.asarray(perm, jnp.int32)
    qi = perm[:, None]
    ki = jnp.concatenate([perm, BLK + perm])[None, :]
    dist = BLK + qi - ki
    in_win = (dist >= 0) & (dist <= steps)
    bucket = _t5_bucket(jnp.clip(dist, 0, steps) * dil)
    onehot = (bucket[:, :, None] == jnp.arange(NUM_BUCKETS, dtype=jnp.int32)).astype(F32)
    bias = jnp.einsum('qkb,bh->hqk', onehot, rel_bias.astype(F32), precision=lax.Precision.HIGHEST)
    bias = jnp.where(in_win[None], bias * LOG2E, NEG_INF)
    return bias.reshape(H_A // 2, 2 * BLK, 2 * BLK)


_PERM_D1 = [NRES * jl + r for r in range(NRES) for jl in range(BLK // NRES)]
_PERM_D4 = [4 * jl + i for i in range(4) for jl in range(BLK // 4)]
_PERM_D16 = list(range(BLK))


def _attn_body(q_ref, k_ref, v_ref, kp_ref, vp_ref, b1_ref, b4_ref, b16_ref, o_ref, acc, mst, lst):
    has_prev = pl.program_id(0) > 0
    lane = lax.broadcasted_iota(jnp.int32, (1, LANES), 1)
    lo = lane < HD_A
    mlo = lo.astype(BF16)
    mhi = (~lo).astype(BF16)
    col = lax.broadcasted_iota(jnp.int32, (2 * BLK, 2 * BLK), 1)
    ones = jnp.ones((2 * BLK, LANES), BF16)

    def tile(q2, kk, vv, bias, mask_prev):
        qs = jnp.concatenate([q2 * mlo, q2 * mhi], axis=0)
        s = _dot_nt(qs, kk) + bias
        if mask_prev:
            s = jnp.where(jnp.logical_or(has_prev, col >= BLK), s, NEG_INF)
        m = jnp.max(s, axis=-1, keepdims=True)
        e = jnp.exp2((s - m).astype(BF16))
        oa = _dot(e, jnp.concatenate([vv, ones], axis=1))
        o = oa[:, :LANES]
        l = oa[:, LANES:]
        return (jnp.where(lo, m[:BLK], m[BLK:]), jnp.where(lo, l[:BLK], l[BLK:]),
                jnp.where(lo, o[:BLK], o[BLK:]))

    def merge(prev, cur):
        mp, lp, ap = prev
        mc, lc, ac = cur
        mn = jnp.maximum(mp, mc)
        a = jnp.exp2(mp - mn)
        b = jnp.exp2(mc - mn)
        return mn, a * lp + b * lc, a * ap + b * ac

    def cat(xs):
        return jnp.concatenate(xs, axis=0)

    def d16_body(r, _):
        rows = pl.ds(pl.multiple_of(r * BLK, BLK), BLK)
        for hp in range(H_A // 2):
            lanes = slice(hp * LANES, (hp + 1) * LANES)
            kk = cat([kp_ref[rows, lanes], k_ref[rows, lanes]])
            vv = cat([vp_ref[rows, lanes], v_ref[rows, lanes]])
            m2, l2, o2 = tile(q_ref[rows, lanes], kk, vv, b16_ref[hp], True)
            mst[rows, lanes] = m2
            lst[rows, lanes] = l2
            acc[rows, lanes] = o2
        return 0

    lax.fori_loop(0, NRES, d16_body, 0)

    def d4_body(r4, _):
        for b in range(4):
            def chunk_rows(bb):
                return [pl.ds(pl.multiple_of(4 * BLK * i + BLK * r4 + 32 * bb, 32), 32) for i in range(4)]
            rows = chunk_rows(b)
            prows = chunk_rows(3 if b == 0 else b - 1)
            kprev, vprev = (kp_ref, vp_ref) if b == 0 else (k_ref, v_ref)
            for hp in range(H_A // 2):
                lanes = slice(hp * LANES, (hp + 1) * LANES)
                q2 = cat([q_ref[rr, lanes] for rr in rows])
                kk = cat([kprev[rr, lanes] for rr in prows] + [k_ref[rr, lanes] for rr in rows])
                vv = cat([vprev[rr, lanes] for rr in prows] + [v_ref[rr, lanes] for rr in rows])
                cur = tile(q2, kk, vv, b4_ref[hp], b == 0)
                prev = (cat([mst[rr, lanes] for rr in rows]), cat([lst[rr, lanes] for rr in rows]),
                        cat([acc[rr, lanes] for rr in rows]))
                mn, ln, an = merge(prev, cur)
                for i, rr in enumerate(rows):
                    part = slice(32 * i, 32 * (i + 1))
                    mst[rr, lanes] = mn[part]
                    lst[rr, lanes] = ln[part]
                    acc[rr, lanes] = an[part]
        return 0

    lax.fori_loop(0, 4, d4_body, 0)

    def d1_pair(ap, kprev, vprev, prev_ap, mask_prev):
        def tiles(a_):
            return [pl.ds(pl.multiple_of(BLK * r + 16 * a_, 16), 16) for r in range(NRES)]
        cur_t = tiles(ap)
        prev_t = tiles(prev_ap)

        def halves(ref, ts, lanes):
            xs = [ref[t, lanes].astype(F32) for t in ts]
            return cat([x[:8] for x in xs]).astype(BF16), cat([x[8:] for x in xs]).astype(BF16)

        for hp in range(H_A // 2):
            lanes = slice(hp * LANES, (hp + 1) * LANES)
            q_e, q_o = halves(q_ref, cur_t, lanes)
            k_e, k_o = halves(k_ref, cur_t, lanes)
            v_e, v_o = halves(v_ref, cur_t, lanes)
            _, k_p = halves(kprev, prev_t, lanes)
            _, v_p = halves(vprev, prev_t, lanes)
            cur_e = tile(q_e, cat([k_p, k_e]), cat([v_p, v_e]), b1_ref[hp], mask_prev)
            cur_o = tile(q_o, cat([k_e, k_o]), cat([v_e, v_o]), b1_ref[hp], False)
            ms = [mst[t, lanes] for t in cur_t]
            ls = [lst[t, lanes] for t in cur_t]
            ac = [acc[t, lanes] for t in cur_t]
            outs = []
            for half, cur in ((0, cur_e), (1, cur_o)):
                part = slice(8 * half, 8 * half + 8)
                prev = (cat([x[part] for x in ms]), cat([x[part] for x in ls]), cat([x[part] for x in ac]))
                _, ln, an = merge(prev, cur)
                outs.append(an / ln)
            for r, t in enumerate(cur_t):
                part = slice(8 * r, 8 * r + 8)
                o_ref[t, lanes] = cat([outs[0][part], outs[1][part]]).astype(o_ref.dtype)

    d1_pair(0, kp_ref, vp_ref, BLK // 16 - 1, True)

    def d1_body(ap, _):
        d1_pair(ap, k_ref, v_ref, ap - 1, False)
        return 0

    lax.fori_loop(1, BLK // 16, d1_body, 0)


def _dilated_attention(proj, biases):
    s = proj.shape[0]
    cur = lambda which: pl.BlockSpec((SPAN, D_A), lambda c: (c, which))
    prev = lambda which: pl.BlockSpec((SPAN, D_A), lambda c: (jnp.maximum(c - 1, 0), which))
    return pl.pallas_call(
        _attn_body,
        out_shape=jax.ShapeDtypeStruct((s, D_A), BF16),
        grid=(s // SPAN,),
        in_specs=[cur(0), cur(1), cur(2), prev(1), prev(2)] + [_full(b) for b in biases],
        out_specs=pl.BlockSpec((SPAN, D_A), lambda c: (c, 0)),
        scratch_shapes=[pltpu.VMEM((SPAN, D_A), F32)] * 3,
        compiler_params=_params(("arbitrary",)),
        name="attn",
    )(proj, proj, proj, proj, proj, *biases)


def _s5_operators(a_re, a_im, b_re, b_im, c_re, c_im, log_dt, d_skip):
    L = S5_CHUNK
    lam = lax.complex(a_re.astype(F32), a_im.astype(F32))
    dt = jnp.exp(log_dt.astype(F32))[:, None]
    a_bar = jnp.exp(lam * dt)
    b_bar = ((a_bar - 1.0) / lam)[..., None] * lax.complex(b_re.astype(F32), b_im.astype(F32))
    c = lax.complex(c_re.astype(F32), c_im.astype(F32))
    j = jnp.arange(L + 1, dtype=F32)
    log_a = lam * dt
    apow = jnp.exp(log_a[None] * j[:, None, None])
    kt = jnp.einsum('gdp,jgp,gpc->gcjd', c, apow[:L], b_bar).real
    skip = d_skip.astype(F32).reshape(S5_G, S5_CH, 1, 1) * jnp.eye(S5_CH)[None, :, None, :]
    kt = (kt + skip * (jnp.arange(L) == 0)[None, None, :, None]).reshape(S5_G, S5_CH, L * S5_CH)
    p = jnp.einsum('sgp,gpc->gscp', apow[:L][::-1], b_bar).reshape(S5_G, L * S5_CH, S5_P)
    ca = jnp.einsum('gdp,tgp->gptd', c, apow[1:L + 1]).reshape(S5_G, S5_P, L * S5_CH)
    a_l = apow[L]

    def pair_blocks(x):
        g, r, w = x.shape
        x = x.reshape(S5_PAIRS, 2, r, w)
        z = jnp.zeros_like(x[:, 0])
        top = jnp.concatenate([x[:, 0], z], axis=-1)
        bot = jnp.concatenate([z, x[:, 1]], axis=-1)
        return jnp.concatenate([top, bot], axis=1)

    p2 = jnp.concatenate([pair_blocks(p.real), pair_blocks(p.imag)], axis=-1)
    q2 = jnp.concatenate([pair_blocks(ca.real), pair_blocks(-ca.imag)], axis=1)
    a_lr = a_l.real.reshape(1, S5_G * S5_P)
    a_li = a_l.imag.reshape(1, S5_G * S5_P)
    return kt, p2.astype(BF16), q2.astype(BF16), a_lr, a_li


def _s5_body(u_ref, kt_ref, p_ref, q_ref, ar_ref, ai_ref, o_ref, m_ref, w_ref, y_ref,
             ere, eim, xre, xim, sre, sim):
    rows = BLK
    gw = S5_CHUNK * S5_CH
    pw = 2 * gw
    per_vreg = LANES // S5_CH
    chunk_of_lane = lax.broadcasted_iota(jnp.int32, (rows, LANES), 1) // S5_CH

    def regroup(srcs, k):
        acc = None
        for i, src in enumerate(srcs):
            shift = ((i - k) * S5_CH) % LANES
            moved = pltpu.roll(src, shift, axis=1) if shift else src
            acc = moved if acc is None else jnp.where(chunk_of_lane == i, moved, acc)
        return acc

    @pl.when(pl.program_id(0) == 0)
    def _():
        sre[...] = jnp.zeros_like(sre)
        sim[...] = jnp.zeros_like(sim)
        lane = lax.broadcasted_iota(jnp.int32, (S5_CH, gw), 1)

        def build(g, _):
            kt = kt_ref[g]
            for s in range(S5_CHUNK):
                blk = kt if s == 0 else jnp.where(lane >= s * S5_CH, pltpu.roll(kt, s * S5_CH, axis=1), 0.0)
                m_ref[g, s * S5_CH:(s + 1) * S5_CH, :] = blk.astype(BF16)
            return 0

        lax.fori_loop(0, S5_G, build, 0)

    for b in range(D_B // LANES):
        for a in range(S5_CHUNK // per_vreg):
            srcs = [u_ref[(per_vreg * a + i) * rows:(per_vreg * a + i + 1) * rows,
                          b * LANES:(b + 1) * LANES].astype(F32) for i in range(per_vreg)]
            for gi in range(per_vreg):
                g = per_vreg * b + gi
                w_ref[:, g * gw + a * LANES:g * gw + (a + 1) * LANES] = regroup(srcs, gi).astype(BF16)

    for pr in range(S5_PAIRS):
        e = _dot(w_ref[:, pr * pw:(pr + 1) * pw], p_ref[pr])
        ere[:, pr * LANES:(pr + 1) * LANES] = e[:, :LANES]
        eim[:, pr * LANES:(pr + 1) * LANES] = e[:, LANES:]

    ar = ar_ref[...]
    ai = ai_ref[...]

    def step(n, carry):
        xr, xi = carry
        xre[pl.ds(n, 1), :] = xr
        xim[pl.ds(n, 1), :] = xi
        nr = ar * xr - ai * xi + ere[pl.ds(n, 1), :]
        ni = ar * xi + ai * xr + eim[pl.ds(n, 1), :]
        return nr, ni

    xr, xi = lax.fori_loop(0, rows, step, (sre[...], sim[...]))
    sre[...] = xr
    sim[...] = xi

    for pr in range(S5_PAIRS):
        xin = jnp.concatenate([xre[:, pr * LANES:(pr + 1) * LANES],
                               xim[:, pr * LANES:(pr + 1) * LANES]], axis=-1).astype(BF16)
        yc = _dot(xin, q_ref[pr])
        for half in range(2):
            g = 2 * pr + half
            cols = slice(g * gw, (g + 1) * gw)
            y = _dot(w_ref[:, cols], m_ref[g]) + yc[:, half * gw:(half + 1) * gw]
            y_ref[:, cols] = 0.5 * y * (1.0 + lax.erf(y * (2.0 ** -0.5)))

    for b in range(D_B // LANES):
        for a in range(S5_CHUNK // per_vreg):
            srcs = [y_ref[:, (per_vreg * b + i) * gw + a * LANES:(per_vreg * b + i) * gw + (a + 1) * LANES]
                    for i in range(per_vreg)]
            for ri in range(per_vreg):
                r = per_vreg * a + ri
                o_ref[r * rows:(r + 1) * rows, b * LANES:(b + 1) * LANES] = regroup(srcs, ri).astype(BF16)


def _s5_core(proj, ops):
    kt, p2, q2, a_lr, a_li = ops
    s = proj.shape[0]
    gw = S5_CHUNK * S5_CH
    wide = S5_G * gw
    nstate = S5_G * S5_P
    return pl.pallas_call(
        _s5_body,
        out_shape=jax.ShapeDtypeStruct((s, D_B), BF16),
        grid=(s // SPAN,),
        in_specs=[pl.BlockSpec((SPAN, D_B), lambda i: (i, 3 * D_A // D_B)),
                  _full(kt), _full(p2), _full(q2), _full(a_lr), _full(a_li)],
        out_specs=pl.BlockSpec((SPAN, D_B), lambda i: (i, 0)),
        scratch_shapes=[pltpu.VMEM((S5_G, gw, gw), BF16), pltpu.VMEM((BLK, wide), BF16),
                        pltpu.VMEM((BLK, wide), F32)]
                       + [pltpu.VMEM((BLK, nstate), F32)] * 4 + [pltpu.VMEM((1, nstate), F32)] * 2,
        compiler_params=_params(("arbitrary",)),
        name="s5",
    )(proj, kt, p2, q2, a_lr, a_li)


def _split_bf16(x):
    hi = x.astype(BF16)
    lo = (x - hi.astype(F32)).astype(BF16)
    return hi, lo


def _mid_body(h_ref, ya_ref, yb_ref, wglu_ref, bglu_ref, ga_ref, gb_ref, wout_ref, gx_ref, wq_ref,
              k_ref, v_ref, wo_ref, gm_ref, wr_ref, br_ref, h_out, xn_out, logit_out):
    part = h_ref.shape[0] // MID_CHAINS
    for c in range(MID_CHAINS):
        rows = slice(c * part, (c + 1) * part)
        outs = _mid_rows(h_ref[rows, :], ya_ref[rows, :], yb_ref[rows, :], wglu_ref, bglu_ref, ga_ref, gb_ref,
                         wout_ref, gx_ref, wq_ref, k_ref, v_ref, wo_ref, gm_ref, wr_ref, br_ref)
        for ref, val in zip((h_out, xn_out, logit_out), outs):
            ref[rows, :] = val


def _mid_rows(h, ya, yb, wglu_ref, bglu_ref, ga_ref, gb_ref, wout_ref, gx_ref, wq_ref,
              k_ref, v_ref, wo_ref, gm_ref, wr_ref, br_ref):
    gate = jax.nn.sigmoid(_dot(yb, wglu_ref[...]) + bglu_ref[...])
    yb2 = yb.astype(F32) * gate
    na = _rms(ya.astype(F32), ga_ref[...]).astype(BF16)
    nb = _rms(yb2, gb_ref[...]).astype(BF16)
    h1 = h + _dot(na, wout_ref[0:D_A, :]) + _dot(nb, wout_ref[D_A:D_MODEL, :])
    q = _dot(_rms(h1, gx_ref[...]).astype(BF16), wq_ref[...]).astype(BF16)
    heads = []
    for hd in range(H_X):
        lanes = slice(hd * HD_X, (hd + 1) * HD_X)
        s = _dot_nt(q[:, lanes], k_ref[:, lanes])
        e = jnp.exp(s - jnp.max(s, axis=-1, keepdims=True))
        heads.append(_dot(e.astype(BF16), v_ref[:, lanes]) / jnp.sum(e, axis=-1, keepdims=True))
    o = jnp.concatenate(heads, axis=-1).astype(BF16)
    h2 = h1 + _dot(o, wo_ref[...])
    xn = _rms(h2, gm_ref[...])
    x_hi, x_lo = _split_bf16(xn)
    part = _dot(x_hi, wr_ref[...]) + _dot(x_lo, wr_ref[...])
    logits = part + pltpu.roll(part, LANES - N_EXPERTS, axis=1) + br_ref[...]
    return h2, _pack_rows(xn), logits


def _top_k_gates(logits):
    lane = lax.broadcasted_iota(jnp.int32, logits.shape, 1)
    logits = jnp.where(lane < N_EXPERTS, logits, -jnp.inf)
    vals, idxs = [], []
    for _ in range(TOP_K):
        mx = jnp.max(logits, axis=-1, keepdims=True)
        ix = jnp.min(jnp.where(logits == mx, lane, LANES), axis=-1, keepdims=True)
        vals.append(mx)
        idxs.append(ix)
        logits = jnp.where(lane == ix, -jnp.inf, logits)
    es = [jnp.exp(v - vals[0]) for v in vals]
    den = es[0] + es[1] + es[2] + es[3]
    idx_t = jnp.full(lane.shape, N_EXPERTS, jnp.int32)
    gate_t = jnp.zeros(lane.shape, F32)
    for k in range(TOP_K):
        idx_t = jnp.where(lane == k, idxs[k], idx_t)
        gate_t = jnp.where(lane == k, es[k] / den, gate_t)
    return idx_t, gate_t


def _mid(h, ya, yb, wglu, bglu, ga, gb, wout, gx, wq, kmem, vmem, wo, gm, wr2, br):
    s = h.shape[0]
    tm = min(ROW_TILE, s)
    row = lambda w: pl.BlockSpec((tm, w), lambda i: (i, 0))
    consts = [wglu, bglu, ga, gb, wout, gx, wq, kmem, vmem, wo, gm, wr2, br]
    return pl.pallas_call(
        _mid_body,
        out_shape=[jax.ShapeDtypeStruct((s, D_MODEL), F32), jax.ShapeDtypeStruct((s, D_MODEL // 2), jnp.int32),
                   jax.ShapeDtypeStruct((s, LANES), F32)],
        grid=(s // tm,),
        in_specs=[row(D_MODEL), row(D_A), row(D_B)] + [_full(a) for a in consts],
        out_specs=[row(D_MODEL), row(D_MODEL // 2), row(LANES)],
        compiler_params=_params(("parallel",)),
        name="mid",
    )(h, ya, yb, *consts)


def _memkv_body(mem_ref, g_ref, w_ref, o_ref):
    o_ref[...] = _dot(_rms(mem_ref[...], g_ref[...]).astype(BF16), w_ref[...]).astype(BF16)


def _memkv(mem, g, w):
    n, d = mem.shape
    return pl.pallas_call(
        _memkv_body,
        out_shape=jax.ShapeDtypeStruct((n, w.shape[1]), BF16),
        compiler_params=pltpu.CompilerParams(vmem_limit_bytes=VMEM_LIMIT),
        name="memkv",
    )(mem, g.reshape(1, d), w)


def _rank_body(logit_ref, idx_ref, gate_ref, rank_ref, cnt_ref, carry):
    @pl.when(pl.program_id(0) == 0)
    def _():
        carry[...] = jnp.zeros_like(carry)

    tm = logit_ref.shape[0]
    lane = lax.broadcasted_iota(jnp.int32, (tm, LANES), 1)
    idx, gates = _top_k_gates(logit_ref[...])
    idx_ref[...] = idx
    gate_ref[...] = gates
    hits = [lane == idx[:, k:k + 1] for k in range(TOP_K)]
    onehot = jnp.zeros((tm, LANES), F32)
    for hit in hits:
        onehot = onehot + jnp.where(hit, 1.0, 0.0)
    ri = lax.broadcasted_iota(jnp.int32, (tm, tm), 0)
    ci = lax.broadcasted_iota(jnp.int32, (tm, tm), 1)
    tri = jnp.where(ri >= ci, 1.0, 0.0).astype(BF16)
    inclusive = _dot(tri, onehot.astype(BF16))
    before = carry[...] + inclusive - onehot
    rank = jnp.zeros((tm, LANES), jnp.int32)
    for k, hit in enumerate(hits):
        rk = jnp.sum(jnp.where(hit, before, 0.0), axis=-1, keepdims=True)
        rank = jnp.where(lane == k, rk.astype(jnp.int32), rank)
    rank_ref[...] = rank
    total = carry[...] + inclusive[tm - 1:tm, :]
    carry[...] = total
    cnt_ref[...] = total.astype(jnp.int32)


def _rank(logits):
    t = logits.shape[0]
    tm = min(ROW_TILE, t)
    rows = pl.BlockSpec((tm, LANES), lambda i: (i, 0))
    return pl.pallas_call(
        _rank_body,
        out_shape=[jax.ShapeDtypeStruct((t, LANES), jnp.int32), jax.ShapeDtypeStruct((t, LANES), F32),
                   jax.ShapeDtypeStruct((t, LANES), jnp.int32), jax.ShapeDtypeStruct((1, LANES), jnp.int32)],
        grid=(t // tm,),
        in_specs=[rows],
        out_specs=[rows, rows, rows, pl.BlockSpec((1, LANES), lambda i: (0, 0))],
        scratch_shapes=[pltpu.VMEM((1, LANES), F32)],
        compiler_params=_params(("arbitrary",)),
        name="rank",
    )(logits)


def _route(logits, tm):
    t = logits.shape[0]
    tk = t * TOP_K
    idx, gates, rank, cnt = _rank(logits)
    counts = cnt[0, :N_EXPERTS]
    padded = (counts + tm - 1) // tm * tm
    pend = jnp.cumsum(padded)
    pstart = pend - padded
    n_rows = tk + N_EXPERTS * tm
    n_tiles = n_rows // tm
    experts = jnp.arange(N_EXPERTS, dtype=jnp.int32)
    tile_first = jnp.arange(n_tiles, dtype=jnp.int32) * tm
    last_used = jnp.max(jnp.where(padded > 0, experts, 0))
    tile_e = jnp.minimum(jnp.sum(tile_first[:, None] >= pend[None, :], axis=1), last_used).astype(jnp.int32)
    tile_rows = jnp.clip(jnp.sum(jnp.where(tile_e[:, None] == experts[None, :],
                                           (pstart + counts)[None, :], 0), axis=1) - tile_first, 0, tm)
    tile_rows = jnp.where(tile_first < pend[-1], tile_rows, 0).astype(jnp.int32)
    group = jnp.cumsum(jnp.concatenate([jnp.zeros((1,), jnp.int32),
                                        (tile_e[1:] != tile_e[:-1]).astype(jnp.int32)]))
    tile_slot = (group % 2).astype(jnp.int32)
    later = (experts[None, :] > experts[:, None]) & (padded > 0)[None, :]
    next_e = jnp.min(jnp.where(later, experts[None, :], N_EXPERTS), axis=1)
    next_e = jnp.where(next_e < N_EXPERTS, next_e, -1).astype(jnp.int32)
    tile_next = jnp.sum(jnp.where(tile_e[:, None] == experts[None, :], next_e[None, :], 0), axis=1).astype(jnp.int32)
    top = idx[:, :TOP_K]
    base = jnp.sum(jnp.where(top[:, :, None] == experts, pstart, 0), axis=-1)
    dest = (rank[:, :TOP_K] + base).T
    return gates, dest, n_rows, tile_e, tile_rows, tile_slot, tile_next


def _sc_mesh():
    return plsc.VectorSubcoreMesh(core_axis_name="c", subcore_axis_name="s",
                                  num_cores=SC_CORES, num_subcores=SC_SUBCORES)


def _sc_worker():
    return lax.axis_index("s") * SC_CORES + lax.axis_index("c")


def _scatter_rows(x, dest, n_rows):
    t, d = x.shape
    per_worker = t // (SC_CORES * SC_SUBCORES)
    chunks = per_worker // SC_ROWS

    @functools.partial(
        pl.kernel, mesh=_sc_mesh(),
        out_type=jax.ShapeDtypeStruct((n_rows, d), x.dtype),
        scratch_types=[pltpu.VMEM((SC_ROWS, d), x.dtype)] + [pltpu.VMEM((SC_ROWS,), jnp.int32)] * TOP_K
                      + [pltpu.SemaphoreType.DMA((TOP_K,))],
    )
    def scatter(x_hbm, *rest):
        dest_hbm, out_hbm, rows_v = rest[:TOP_K], rest[TOP_K], rest[TOP_K + 1]
        idx_v, sem = rest[TOP_K + 2:2 * TOP_K + 2], rest[2 * TOP_K + 2]
        base = _sc_worker() * per_worker

        @pl.loop(0, chunks)
        def _(c):
            off = pl.multiple_of(base + c * SC_ROWS, SC_ROWS)
            pltpu.sync_copy(x_hbm.at[pl.ds(off, SC_ROWS)], rows_v)
            for k in range(TOP_K):
                pltpu.sync_copy(dest_hbm[k].at[pl.ds(off, SC_ROWS)], idx_v[k])
            copies = [pltpu.async_copy(rows_v, out_hbm.at[idx_v[k]], sem.at[k]) for k in range(TOP_K)]
            for cp in copies:
                cp.wait()

    return scatter(x, *[dest[k] for k in range(TOP_K)])


def _gather_rows(table, idx):
    n, d = table.shape
    b = idx.shape[0]
    per_worker = b // (SC_CORES * SC_SUBCORES)
    chunks = per_worker // SC_ROWS
    assert chunks % 2 == 0

    @functools.partial(
        pl.kernel, mesh=_sc_mesh(),
        out_type=jax.ShapeDtypeStruct((b, d), table.dtype),
        scratch_types=[pltpu.VMEM((SC_ROWS,), jnp.int32)] * 2 + [pltpu.VMEM((SC_ROWS, d), table.dtype)] * 2
                      + [pltpu.SemaphoreType.DMA((2,)), pltpu.SemaphoreType.DMA((2,))],
    )
    def gather(table_hbm, idx_hbm, out_hbm, idx0, idx1, rows0, rows1, gsem, wsem):
        idx_v, rows_v = (idx0, idx1), (rows0, rows1)
        base = _sc_worker() * per_worker

        def rows_at(c):
            return pl.ds(pl.multiple_of(base + c * SC_ROWS, SC_ROWS), SC_ROWS)

        def fetch(c, s):
            pltpu.sync_copy(idx_hbm.at[rows_at(c)], idx_v[s])
            pltpu.async_copy(table_hbm.at[idx_v[s]], rows_v[s], gsem.at[s])

        def fetched(s):
            return pltpu.make_async_copy(table_hbm.at[idx_v[s]], rows_v[s], gsem.at[s])

        def written(c, s):
            return pltpu.make_async_copy(rows_v[s], out_hbm.at[rows_at(c)], wsem.at[s])

        fetch(0, 0)

        @pl.loop(0, chunks, step=2)
        def _(c0):
            for s in range(2):
                c = c0 + s
                fetched(s).wait()
                written(c, s).start()

                @pl.when(c >= 1)
                def _():
                    written(c - 1, 1 - s).wait()

                @pl.when(c + 1 < chunks)
                def _():
                    fetch(c + 1, 1 - s)

        written(chunks - 1, 1).wait()

    return gather(table, idx)


def _moe_body(te_ref, tv_ref, sl_ref, nx_ref, x_ref, w1_hbm, b1_ref, w2_hbm, b2_ref, o_ref,
              w1f, w2f, w1b, w2b, sem, *, layer):
    i = pl.program_id(0)
    e = te_ref[i]
    slot = sl_ref[i]
    new_expert = (i == 0) | (e != te_ref[jnp.maximum(i - 1, 0)])

    def weight_copies(expert, s):
        rows1 = w1f.shape[1] // MOE_W1_PARTS
        rows2 = w2f.shape[1] // MOE_W2_PARTS
        c1 = [pltpu.make_async_copy(w1_hbm.at[layer, expert, pl.ds(q * rows1, rows1)],
                                    w1f.at[s, pl.ds(q * rows1, rows1)], sem.at[s, q])
              for q in range(MOE_W1_PARTS)]
        c2 = [pltpu.make_async_copy(w2_hbm.at[layer, expert, pl.ds(q * rows2, rows2)],
                                    w2f.at[s, pl.ds(q * rows2, rows2)], sem.at[s, MOE_W1_PARTS + q])
              for q in range(MOE_W2_PARTS)]
        return c1 + c2

    @pl.when(i == 0)
    def _():
        for c in weight_copies(e, slot):
            c.start()

    @pl.when(new_expert)
    def _():
        for c in weight_copies(e, slot):
            c.wait()
        nxt = nx_ref[i]

        @pl.when(nxt >= 0)
        def _():
            for c in weight_copies(nxt, 1 - slot):
                c.start()

        w1b[...] = w1f[slot].astype(BF16)
        w2b[...] = w2f[slot].astype(BF16)

    @pl.when(tv_ref[i] > 0)
    def _():
        row = lax.broadcasted_iota(jnp.int32, x_ref.shape, 0)
        lo, hi = _unpack_rows(jnp.where(row < tv_ref[i], x_ref[...], 0))
        x = jnp.concatenate([lo, hi], axis=-1).astype(BF16)
        hb = _dot(x, w1b[...]) + b1_ref[0]
        x_glu = jnp.minimum(hb[:, :D_FF], SWIGLU_LIMIT)
        x_lin = jnp.clip(hb[:, D_FF:], -SWIGLU_LIMIT, SWIGLU_LIMIT)
        act = x_glu * jax.nn.sigmoid(SWIGLU_ALPHA * x_glu) * (x_lin + 1.0)
        o_ref[...] = _pack_rows(_dot(act.astype(BF16), w2b[...]) + b2_ref[0])

    @pl.when(tv_ref[i] == 0)
    def _():
        o_ref[...] = jnp.zeros_like(o_ref)


def _moe_experts(xs, tile_e, tile_rows, tile_slot, tile_next, w1, b1, w2, b2, layer):
    n_rows = xs.shape[0]
    tm = MOE_TILE
    nl, ne, d, ff2 = w1.shape
    bias_map = lambda i, te, tv, sl, nx: (layer, te[i], 0, 0)
    grid_spec = pltpu.PrefetchScalarGridSpec(
        num_scalar_prefetch=4,
        grid=(n_rows // tm,),
        in_specs=[pl.BlockSpec((tm, d // 2), lambda i, *_: (i, 0)),
                  pl.BlockSpec(memory_space=pl.ANY),
                  pl.BlockSpec((None, 1, 1, ff2), bias_map),
                  pl.BlockSpec(memory_space=pl.ANY),
                  pl.BlockSpec((None, 1, 1, d), bias_map)],
        out_specs=pl.BlockSpec((tm, d // 2), lambda i, *_: (i, 0)),
        scratch_shapes=[pltpu.VMEM((2, d, ff2), F32), pltpu.VMEM((2, ff2 // 2, d), F32),
                        pltpu.VMEM((d, ff2), BF16), pltpu.VMEM((ff2 // 2, d), BF16),
                        pltpu.SemaphoreType.DMA((2, MOE_W1_PARTS + MOE_W2_PARTS))],
    )
    return pl.pallas_call(
        functools.partial(_moe_body, layer=layer),
        out_shape=jax.ShapeDtypeStruct((n_rows, d // 2), jnp.int32),
        grid_spec=grid_spec,
        compiler_params=_params(("arbitrary",)),
        name="moe",
    )(tile_e, tile_rows, tile_slot, tile_next, xs, w1, b1.reshape(nl, ne, 1, ff2), w2, b2.reshape(nl, ne, 1, d))


def _combine_body(h_ref, y0_ref, y1_ref, y2_ref, y3_ref, gate_ref, g_ref, o_ref, *, final):
    gates = gate_ref[...]
    lo = jnp.zeros(y0_ref.shape, F32)
    hi = jnp.zeros(y0_ref.shape, F32)
    for k, y_ref in enumerate((y0_ref, y1_ref, y2_ref, y3_ref)):
        yl, yh = _unpack_rows(y_ref[...])
        lo = lo + yl * gates[:, k:k + 1]
        hi = hi + yh * gates[:, k:k + 1]
    h = h_ref[...] + jnp.concatenate([lo, hi], axis=-1)
    o_ref[...] = _rms(h, g_ref[...]) if final else h


def _combine(h, y, gates, g, *, final):
    s, d = h.shape
    tm = min(ROW_TILE // 2, s)
    nt = s // tm
    return pl.pallas_call(
        functools.partial(_combine_body, final=final),
        out_shape=jax.ShapeDtypeStruct((s, d), F32),
        grid=(nt,),
        in_specs=[pl.BlockSpec((tm, d), lambda i: (i, 0))]
                 + [pl.BlockSpec((tm, d // 2), lambda i, k=k: (k * nt + i, 0)) for k in range(TOP_K)]
                 + [pl.BlockSpec((tm, LANES), lambda i: (i, 0)), pl.BlockSpec((1, d), lambda i: (0, 0))],
        out_specs=pl.BlockSpec((tm, d), lambda i: (i, 0)),
        compiler_params=_params(("parallel",)),
        name="combine",
    )(h, y, y, y, y, gates, g.reshape(1, d))


def _layer(h, mem, biases, p, l, g_final):
    row = lambda a: a.reshape(1, -1).astype(F32)
    q_scale = jnp.concatenate([jnp.full((D_A,), HD_A ** -0.5 * LOG2E, F32), jnp.ones((D_IN - D_A,), F32)])
    w_in = (p["w_in"][l] * q_scale).astype(BF16)
    proj = _inproj(h, p["norm_mix"][l], w_in)
    ya = _dilated_attention(proj, biases)
    ops = _s5_operators(p["s5_a_re"][l], p["s5_a_im"][l], p["s5_b_re"][l], p["s5_b_im"][l],
                        p["s5_c_re"][l], p["s5_c_im"][l], p["s5_log_dt"][l], p["s5_d"][l])
    yb = _s5_core(proj, ops)
    kv = _memkv(mem, p["norm_mem"][l], p["w_xkv"][l].astype(BF16))
    wr = p["w_router"][l].astype(F32)
    wr_hi = wr.astype(BF16)
    wr2 = jnp.pad(jnp.concatenate([wr_hi, (wr - wr_hi.astype(F32)).astype(BF16)], axis=1),
                  ((0, 0), (0, LANES - 2 * N_EXPERTS)))
    br = jnp.pad(p["b_router"][l].astype(F32), (0, LANES - N_EXPERTS)).reshape(1, LANES)
    h2, xn, logits = _mid(
        h, ya, yb, p["w_glu"][l].astype(BF16), row(p["b_glu"][l]), row(p["g_out_attn"][l]),
        row(p["g_out_ssm"][l]), p["w_out"][l].astype(BF16), row(p["norm_xattn"][l]),
        (p["w_xq"][l] * (HD_X ** -0.5)).astype(BF16), kv[:, :D_X], kv[:, D_X:],
        p["w_xo"][l].astype(BF16), row(p["norm_moe"][l]), wr2, br)
    gates, dest, n_rows, tile_e, tile_rows, tile_slot, tile_next = _route(logits, MOE_TILE)
    out = _moe_experts(_scatter_rows(xn, dest, n_rows), tile_e, tile_rows, tile_slot, tile_next,
                       p["w1"], p["b1"], p["w2"], p["b2"], l)
    return _combine(h2, _gather_rows(out, dest.reshape(-1)), gates, g_final, final=l == DEPTH - 1)


def kernel(x, mem, rel_bias, norm_mix, w_in, s5_a_re, s5_a_im, s5_b_re, s5_b_im, s5_c_re, s5_c_im, s5_log_dt, s5_d, w_glu, b_glu, g_out_attn, g_out_ssm, w_out, norm_xattn, norm_mem, w_xq, w_xkv, w_xo, norm_moe, w_router, b_router, w1, b1, w2, b2, norm_final):
    p = dict(norm_mix=norm_mix, w_in=w_in, s5_a_re=s5_a_re, s5_a_im=s5_a_im, s5_b_re=s5_b_re,
             s5_b_im=s5_b_im, s5_c_re=s5_c_re, s5_c_im=s5_c_im, s5_log_dt=s5_log_dt, s5_d=s5_d,
             w_glu=w_glu, b_glu=b_glu, g_out_attn=g_out_attn, g_out_ssm=g_out_ssm, w_out=w_out,
             norm_xattn=norm_xattn, norm_mem=norm_mem, w_xq=w_xq, w_xkv=w_xkv, w_xo=w_xo,
             norm_moe=norm_moe, w_router=w_router, b_router=b_router, w1=w1, b1=b1, w2=w2, b2=b2)
    biases = [_attn_bias(rel_bias, window, dil, perm)
              for (window, dil), perm in zip(WIN_DIL, (_PERM_D1, _PERM_D4, _PERM_D16))]
    outs = []
    for b in range(x.shape[0]):
        h = _to_span_layout(x[b])
        for l in range(DEPTH):
            h = _layer(h, mem[b], biases, p, l, norm_final)
        outs.append(_from_span_layout(h))
    return jnp.stack(outs)
```

```python
import functools
import math

import jax
import jax.numpy as jnp
from jax import lax
from jax.experimental import pallas as pl
from jax.experimental.pallas import tpu as pltpu
from jax.experimental.pallas import tpu_sc as plsc

F32 = jnp.float32
BF16 = jnp.bfloat16

D_MODEL = 1024
DEPTH = 2
EPS = 1e-5
NEG_INF = -1e30
LOG2E = math.log2(math.e)
H_A = 8
HD_A = 64
D_A = H_A * HD_A
WIN_DIL = ((128, 1), (512, 4), (2048, 16))
BLK = 128
D_B = D_MODEL - D_A
S5_CH = 16
S5_G = D_B // S5_CH
S5_P = 64
D_IN = 3 * D_A + D_B
NUM_BUCKETS = 32
REL_MAX_DIST = 2048
H_X = 4
HD_X = 128
D_X = H_X * HD_X
N_EXPERTS = 32
TOP_K = 4
D_FF = D_MODEL
SWIGLU_ALPHA = 1.702
SWIGLU_LIMIT = 7.0

LANES = 128
NRES = WIN_DIL[-1][1]
SPAN = NRES * BLK
S5_CHUNK = NRES
S5_PAIRS = S5_G // 2
VMEM_LIMIT = 56 * 1024 * 1024

SC_CORES = 2
SC_SUBCORES = 16
SC_ROWS = 64

ROW_TILE = 512
ATTN_D16_UNROLL = 4
MID_CHAINS = 2
MOE_TILE = 512
MOE_W1_PARTS = 4
MOE_W2_PARTS = 2


def _params(sem):
    return pltpu.CompilerParams(dimension_semantics=sem, vmem_limit_bytes=VMEM_LIMIT)


def _rms(x, g):
    return x * lax.rsqrt(jnp.mean(x * x, axis=-1, keepdims=True) + EPS) * g


def _dot(a, b):
    return jnp.dot(a, b, preferred_element_type=F32)


def _dot_nt(a, b):
    return lax.dot_general(a, b, (((1,), (1,)), ((), ())), preferred_element_type=F32)


def _full(a):
    return pl.BlockSpec(a.shape, lambda *_: (0,) * a.ndim)


def _pack_rows(x):
    c = x.shape[1] // 2
    lo = lax.bitcast_convert_type(x[:, :c].astype(BF16).astype(F32), jnp.uint32)
    hi = lax.bitcast_convert_type(x[:, c:].astype(BF16).astype(F32), jnp.uint32)
    return lax.bitcast_convert_type(lax.shift_right_logical(lo, jnp.uint32(16)) | hi, jnp.int32)


def _unpack_rows(p):
    u = lax.bitcast_convert_type(p, jnp.uint32)
    lo = lax.bitcast_convert_type(lax.shift_left(u, jnp.uint32(16)), F32)
    hi = lax.bitcast_convert_type(u & jnp.uint32(0xFFFF0000), F32)
    return lo, hi


def _to_span_layout(x):
    s = x.shape[0]
    return x.reshape(s // SPAN, BLK, NRES, -1).transpose(0, 2, 1, 3).reshape(s, -1)


def _from_span_layout(x):
    s = x.shape[0]
    return x.reshape(s // SPAN, NRES, BLK, -1).transpose(0, 2, 1, 3).reshape(s, -1)


def _inproj_body(h_ref, g_ref, w_ref, o_ref):
    xn = _rms(h_ref[...], g_ref[...]).astype(BF16)
    o_ref[...] = _dot(xn, w_ref[...]).astype(BF16)


def _inproj(h, g, w):
    s, d = h.shape
    n = w.shape[1]
    tm = min(ROW_TILE, s)
    return pl.pallas_call(
        _inproj_body,
        out_shape=jax.ShapeDtypeStruct((s, n), BF16),
        grid=(s // tm,),
        in_specs=[pl.BlockSpec((tm, d), lambda i: (i, 0)),
                  pl.BlockSpec((1, d), lambda i: (0, 0)),
                  pl.BlockSpec((d, n), lambda i: (0, 0))],
        out_specs=pl.BlockSpec((tm, n), lambda i: (i, 0)),
        compiler_params=_params(("parallel",)),
        name="inproj",
    )(h, g.reshape(1, d), w)


def _t5_bucket(n):
    max_exact = NUM_BUCKETS // 2
    nf = jnp.maximum(n, 1).astype(F32)
    large = max_exact + (jnp.log(nf / max_exact) / math.log(REL_MAX_DIST / max_exact)
                         * (NUM_BUCKETS - max_exact)).astype(jnp.int32)
    large = jnp.minimum(large, NUM_BUCKETS - 1)
    return jnp.where(n < max_exact, n, large)


def _attn_bias(rel_bias, window, dil, perm):
    steps = window // dil
    perm = jnp.asarray(perm, jnp.int32)
    qi = perm[:, None]
    ki = jnp.concatenate([perm, BLK + perm])[None, :]
    dist = BLK + qi - ki
    in_win = (dist >= 0) & (dist <= steps)
    bucket = _t5_bucket(jnp.clip(dist, 0, steps) * dil)
    onehot = (bucket[:, :, None] == jnp.arange(NUM_BUCKETS, dtype=jnp.int32)).astype(F32)
    bias = jnp.einsum('qkb,bh->hqk', onehot, rel_bias.astype(F32), precision=lax.Precision.HIGHEST)
    bias = jnp.where(in_win[None], bias * LOG2E, NEG_INF)
    return bias.reshape(H_A // 2, 2 * BLK, 2 * BLK)


_PERM_D1 = [NRES * jl + r for r in range(NRES) for jl in range(BLK // NRES)]
_PERM_D4 = [4 * jl + i for i in range(4) for jl in range(BLK // 4)]
_PERM_D16 = list(range(BLK))


def _attn_body(q_ref, k_ref, v_ref, kp_ref, vp_ref, b1_ref, b4_ref, b16_ref, o_ref, acc, mst, lst):
    has_prev = pl.program_id(0) > 0
    lane = lax.broadcasted_iota(jnp.int32, (1, LANES), 1)
    lo = lane < HD_A
    mlo = lo.astype(BF16)
    mhi = (~lo).astype(BF16)
    col = lax.broadcasted_iota(jnp.int32, (2 * BLK, 2 * BLK), 1)
    ones = jnp.ones((2 * BLK, LANES), BF16)

    def tile(q2, kk, vv, bias, mask_prev):
        qs = jnp.concatenate([q2 * mlo, q2 * mhi], axis=0)
        s = _dot_nt(qs, kk) + bias
        if mask_prev:
            s = jnp.where(jnp.logical_or(has_prev, col >= BLK), s, NEG_INF)
        m = jnp.max(s, axis=-1, keepdims=True)
        e = jnp.exp2((s - m).astype(BF16))
        oa = _dot(e, jnp.concatenate([vv, ones], axis=1))
        o = oa[:, :LANES]
        l = oa[:, LANES:]
        return (jnp.where(lo, m[:BLK], m[BLK:]), jnp.where(lo, l[:BLK], l[BLK:]),
                jnp.where(lo, o[:BLK], o[BLK:]))

    def merge(prev, cur):
        mp, lp, ap = prev
        mc, lc, ac = cur
        mn = jnp.maximum(mp, mc)
        a = jnp.exp2(mp - mn)
        b = jnp.exp2(mc - mn)
        return mn, a * lp + b * lc, a * ap + b * ac

    def cat(xs):
        return jnp.concatenate(xs, axis=0)

    def d16_body(i, _):
        for sub in range(ATTN_D16_UNROLL):
            rows = pl.ds(pl.multiple_of((i * ATTN_D16_UNROLL + sub) * BLK, BLK), BLK)
            for hp in range(H_A // 2):
                lanes = slice(hp * LANES, (hp + 1) * LANES)
                kk = cat([kp_ref[rows, lanes], k_ref[rows, lanes]])
                vv = cat([vp_ref[rows, lanes], v_ref[rows, lanes]])
                m2, l2, o2 = tile(q_ref[rows, lanes], kk, vv, b16_ref[hp], True)
                mst[rows, lanes] = m2
                lst[rows, lanes] = l2
                acc[rows, lanes] = o2
        return 0

    lax.fori_loop(0, NRES // ATTN_D16_UNROLL, d16_body, 0)

    def d4_body(r4, _):
        for b in range(4):
            def chunk_rows(bb):
                return [pl.ds(pl.multiple_of(4 * BLK * i + BLK * r4 + 32 * bb, 32), 32) for i in range(4)]
            rows = chunk_rows(b)
            prows = chunk_rows(3 if b == 0 else b - 1)
            kprev, vprev = (kp_ref, vp_ref) if b == 0 else (k_ref, v_ref)
            for hp in range(H_A // 2):
                lanes = slice(hp * LANES, (hp + 1) * LANES)
                q2 = cat([q_ref[rr, lanes] for rr in rows])
                kk = cat([kprev[rr, lanes] for rr in prows] + [k_ref[rr, lanes] for rr in rows])
                vv = cat([vprev[rr, lanes] for rr in prows] + [v_ref[rr, lanes] for rr in rows])
                cur = tile(q2, kk, vv, b4_ref[hp], b == 0)
                prev = (cat([mst[rr, lanes] for rr in rows]), cat([lst[rr, lanes] for rr in rows]),
                        cat([acc[rr, lanes] for rr in rows]))
                mn, ln, an = merge(prev, cur)
                for i, rr in enumerate(rows):
                    part = slice(32 * i, 32 * (i + 1))
                    mst[rr, lanes] = mn[part]
                    lst[rr, lanes] = ln[part]
                    acc[rr, lanes] = an[part]
        return 0

    lax.fori_loop(0, 4, d4_body, 0)

    def d1_pair(ap, kprev, vprev, prev_ap, mask_prev):
        def tiles(a_):
            return [pl.ds(pl.multiple_of(BLK * r + 16 * a_, 16), 16) for r in range(NRES)]
        cur_t = tiles(ap)
        prev_t = tiles(prev_ap)

        def halves(ref, ts, lanes):
            xs = [ref[t, lanes].astype(F32) for t in ts]
            return cat([x[:8] for x in xs]).astype(BF16), cat([x[8:] for x in xs]).astype(BF16)

        for hp in range(H_A // 2):
            lanes = slice(hp * LANES, (hp + 1) * LANES)
            q_e, q_o = halves(q_ref, cur_t, lanes)
            k_e, k_o = halves(k_ref, cur_t, lanes)
            v_e, v_o = halves(v_ref, cur_t, lanes)
            _, k_p = halves(kprev, prev_t, lanes)
            _, v_p = halves(vprev, prev_t, lanes)
            cur_e = tile(q_e, cat([k_p, k_e]), cat([v_p, v_e]), b1_ref[hp], mask_prev)
            cur_o = tile(q_o, cat([k_e, k_o]), cat([v_e, v_o]), b1_ref[hp], False)
            ms = [mst[t, lanes] for t in cur_t]
            ls = [lst[t, lanes] for t in cur_t]
            ac = [acc[t, lanes] for t in cur_t]
            outs = []
            for half, cur in ((0, cur_e), (1, cur_o)):
                part = slice(8 * half, 8 * half + 8)
                prev = (cat([x[part] for x in ms]), cat([x[part] for x in ls]), cat([x[part] for x in ac]))
                _, ln, an = merge(prev, cur)
                outs.append(an / ln)
            for r, t in enumerate(cur_t):
                part = slice(8 * r, 8 * r + 8)
                o_ref[t, lanes] = cat([outs[0][part], outs[1][part]]).astype(o_ref.dtype)

    d1_pair(0, kp_ref, vp_ref, BLK // 16 - 1, True)

    def d1_body(ap, _):
        d1_pair(ap, k_ref, v_ref, ap - 1, False)
        return 0

    lax.fori_loop(1, BLK // 16, d1_body, 0)


def _dilated_attention(proj, biases):
    s = proj.shape[0]
    cur = lambda which: pl.BlockSpec((SPAN, D_A), lambda c: (c, which))
    prev = lambda which: pl.BlockSpec((SPAN, D_A), lambda c: (jnp.maximum(c - 1, 0), which))
    return pl.pallas_call(
        _attn_body,
        out_shape=jax.ShapeDtypeStruct((s, D_A), BF16),
        grid=(s // SPAN,),
        in_specs=[cur(0), cur(1), cur(2), prev(1), prev(2)] + [_full(b) for b in biases],
        out_specs=pl.BlockSpec((SPAN, D_A), lambda c: (c, 0)),
        scratch_shapes=[pltpu.VMEM((SPAN, D_A), F32)] * 3,
        compiler_params=_params(("arbitrary",)),
        name="attn",
    )(proj, proj, proj, proj, proj, *biases)


def _s5_operators(a_re, a_im, b_re, b_im, c_re, c_im, log_dt, d_skip):
    L = S5_CHUNK
    lam = lax.complex(a_re.astype(F32), a_im.astype(F32))
    dt = jnp.exp(log_dt.astype(F32))[:, None]
    a_bar = jnp.exp(lam * dt)
    b_bar = ((a_bar - 1.0) / lam)[..., None] * lax.complex(b_re.astype(F32), b_im.astype(F32))
    c = lax.complex(c_re.astype(F32), c_im.astype(F32))
    j = jnp.arange(L + 1, dtype=F32)
    log_a = lam * dt
    apow = jnp.exp(log_a[None] * j[:, None, None])
    kt = jnp.einsum('gdp,jgp,gpc->gcjd', c, apow[:L], b_bar).real
    skip = d_skip.astype(F32).reshape(S5_G, S5_CH, 1, 1) * jnp.eye(S5_CH)[None, :, None, :]
    kt = (kt + skip * (jnp.arange(L) == 0)[None, None, :, None]).reshape(S5_G, S5_CH, L * S5_CH)
    p = jnp.einsum('sgp,gpc->gscp', apow[:L][::-1], b_bar).reshape(S5_G, L * S5_CH, S5_P)
    ca = jnp.einsum('gdp,tgp->gptd', c, apow[1:L + 1]).reshape(S5_G, S5_P, L * S5_CH)
    a_l = apow[L]

    def pair_blocks(x):
        g, r, w = x.shape
        x = x.reshape(S5_PAIRS, 2, r, w)
        z = jnp.zeros_like(x[:, 0])
        top = jnp.concatenate([x[:, 0], z], axis=-1)
        bot = jnp.concatenate([z, x[:, 1]], axis=-1)
        return jnp.concatenate([top, bot], axis=1)

    p2 = jnp.concatenate([pair_blocks(p.real), pair_blocks(p.imag)], axis=-1)
    q2 = jnp.concatenate([pair_blocks(ca.real), pair_blocks(-ca.imag)], axis=1)
    a_lr = a_l.real.reshape(1, S5_G * S5_P)
    a_li = a_l.imag.reshape(1, S5_G * S5_P)
    return kt, p2.astype(BF16), q2.astype(BF16), a_lr, a_li


def _s5_body(u_ref, kt_ref, p_ref, q_ref, ar_ref, ai_ref, o_ref, m_ref, w_ref, y_ref,
             ere, eim, xre, xim, sre, sim):
    rows = BLK
    gw = S5_CHUNK * S5_CH
    pw = 2 * gw
    per_vreg = LANES // S5_CH
    chunk_of_lane = lax.broadcasted_iota(jnp.int32, (rows, LANES), 1) // S5_CH

    def regroup(srcs, k):
        acc = None
        for i, src in enumerate(srcs):
            shift = ((i - k) * S5_CH) % LANES
            moved = pltpu.roll(src, shift, axis=1) if shift else src
            acc = moved if acc is None else jnp.where(chunk_of_lane == i, moved, acc)
        return acc

    @pl.when(pl.program_id(0) == 0)
    def _():
        sre[...] = jnp.zeros_like(sre)
        sim[...] = jnp.zeros_like(sim)
        lane = lax.broadcasted_iota(jnp.int32, (S5_CH, gw), 1)

        def build(g, _):
            kt = kt_ref[g]
            for s in range(S5_CHUNK):
                blk = kt if s == 0 else jnp.where(lane >= s * S5_CH, pltpu.roll(kt, s * S5_CH, axis=1), 0.0)
                m_ref[g, s * S5_CH:(s + 1) * S5_CH, :] = blk.astype(BF16)
            return 0

        lax.fori_loop(0, S5_G, build, 0)

    for b in range(D_B // LANES):
        for a in range(S5_CHUNK // per_vreg):
            srcs = [u_ref[(per_vreg * a + i) * rows:(per_vreg * a + i + 1) * rows,
                          b * LANES:(b + 1) * LANES].astype(F32) for i in range(per_vreg)]
            for gi in range(per_vreg):
                g = per_vreg * b + gi
                w_ref[:, g * gw + a * LANES:g * gw + (a + 1) * LANES] = regroup(srcs, gi).astype(BF16)

    for pr in range(S5_PAIRS):
        e = _dot(w_ref[:, pr * pw:(pr + 1) * pw], p_ref[pr])
        ere[:, pr * LANES:(pr + 1) * LANES] = e[:, :LANES]
        eim[:, pr * LANES:(pr + 1) * LANES] = e[:, LANES:]

    ar = ar_ref[...]
    ai = ai_ref[...]

    def step(n, carry):
        xr, xi = carry
        xre[pl.ds(n, 1), :] = xr
        xim[pl.ds(n, 1), :] = xi
        nr = ar * xr - ai * xi + ere[pl.ds(n, 1), :]
        ni = ar * xi + ai * xr + eim[pl.ds(n, 1), :]
        return nr, ni

    xr, xi = lax.fori_loop(0, rows, step, (sre[...], sim[...]))
    sre[...] = xr
    sim[...] = xi

    for pr in range(S5_PAIRS):
        xin = jnp.concatenate([xre[:, pr * LANES:(pr + 1) * LANES],
                               xim[:, pr * LANES:(pr + 1) * LANES]], axis=-1).astype(BF16)
        yc = _dot(xin, q_ref[pr])
        for half in range(2):
            g = 2 * pr + half
            cols = slice(g * gw, (g + 1) * gw)
            y = _dot(w_ref[:, cols], m_ref[g]) + yc[:, half * gw:(half + 1) * gw]
            y_ref[:, cols] = 0.5 * y * (1.0 + lax.erf(y * (2.0 ** -0.5)))

    for b in range(D_B // LANES):
        for a in range(S5_CHUNK // per_vreg):
            srcs = [y_ref[:, (per_vreg * b + i) * gw + a * LANES:(per_vreg * b + i) * gw + (a + 1) * LANES]
                    for i in range(per_vreg)]
            for ri in range(per_vreg):
                r = per_vreg * a + ri
                o_ref[r * rows:(r + 1) * rows, b * LANES:(b + 1) * LANES] = regroup(srcs, ri).astype(BF16)


def _s5_core(proj, ops):
    kt, p2, q2, a_lr, a_li = ops
    s = proj.shape[0]
    gw = S5_CHUNK * S5_CH
    wide = S5_G * gw
    nstate = S5_G * S5_P
    return pl.pallas_call(
        _s5_body,
        out_shape=jax.ShapeDtypeStruct((s, D_B), BF16),
        grid=(s // SPAN,),
        in_specs=[pl.BlockSpec((SPAN, D_B), lambda i: (i, 3 * D_A // D_B)),
                  _full(kt), _full(p2), _full(q2), _full(a_lr), _full(a_li)],
        out_specs=pl.BlockSpec((SPAN, D_B), lambda i: (i, 0)),
        scratch_shapes=[pltpu.VMEM((S5_G, gw, gw), BF16), pltpu.VMEM((BLK, wide), BF16),
                        pltpu.VMEM((BLK, wide), F32)]
                       + [pltpu.VMEM((BLK, nstate), F32)] * 4 + [pltpu.VMEM((1, nstate), F32)] * 2,
        compiler_params=_params(("arbitrary",)),
        name="s5",
    )(proj, kt, p2, q2, a_lr, a_li)


def _split_bf16(x):
    hi = x.astype(BF16)
    lo = (x - hi.astype(F32)).astype(BF16)
    return hi, lo


def _mid_body(h_ref, ya_ref, yb_ref, wglu_ref, bglu_ref, ga_ref, gb_ref, wout_ref, gx_ref, wq_ref,
              k_ref, v_ref, wo_ref, gm_ref, wr_ref, br_ref, h_out, xn_out, logit_out):
    part = h_ref.shape[0] // MID_CHAINS
    for c in range(MID_CHAINS):
        rows = slice(c * part, (c + 1) * part)
        outs = _mid_rows(h_ref[rows, :], ya_ref[rows, :], yb_ref[rows, :], wglu_ref, bglu_ref, ga_ref, gb_ref,
                         wout_ref, gx_ref, wq_ref, k_ref, v_ref, wo_ref, gm_ref, wr_ref, br_ref)
        for ref, val in zip((h_out, xn_out, logit_out), outs):
            ref[rows, :] = val


def _mid_rows(h, ya, yb, wglu_ref, bglu_ref, ga_ref, gb_ref, wout_ref, gx_ref, wq_ref,
              k_ref, v_ref, wo_ref, gm_ref, wr_ref, br_ref):
    gate = jax.nn.sigmoid(_dot(yb, wglu_ref[...]) + bglu_ref[...])
    yb2 = yb.astype(F32) * gate
    na = _rms(ya.astype(F32), ga_ref[...]).astype(BF16)
    nb = _rms(yb2, gb_ref[...]).astype(BF16)
    h1 = h + _dot(na, wout_ref[0:D_A, :]) + _dot(nb, wout_ref[D_A:D_MODEL, :])
    q = _dot(_rms(h1, gx_ref[...]).astype(BF16), wq_ref[...]).astype(BF16)
    heads = []
    for hd in range(H_X):
        lanes = slice(hd * HD_X, (hd + 1) * HD_X)
        s = _dot_nt(q[:, lanes], k_ref[:, lanes])
        e = jnp.exp(s - jnp.max(s, axis=-1, keepdims=True))
        heads.append(_dot(e.astype(BF16), v_ref[:, lanes]) / jnp.sum(e, axis=-1, keepdims=True))
    o = jnp.concatenate(heads, axis=-1).astype(BF16)
    h2 = h1 + _dot(o, wo_ref[...])
    xn = _rms(h2, gm_ref[...])
    x_hi, x_lo = _split_bf16(xn)
    part = _dot(x_hi, wr_ref[...]) + _dot(x_lo, wr_ref[...])
    logits = part + pltpu.roll(part, LANES - N_EXPERTS, axis=1) + br_ref[...]
    return h2, _pack_rows(xn), logits


def _top_k_gates(logits):
    lane = lax.broadcasted_iota(jnp.int32, logits.shape, 1)
    logits = jnp.where(lane < N_EXPERTS, logits, -jnp.inf)
    vals, idxs = [], []
    for _ in range(TOP_K):
        mx = jnp.max(logits, axis=-1, keepdims=True)
        ix = jnp.min(jnp.where(logits == mx, lane, LANES), axis=-1, keepdims=True)
        vals.append(mx)
        idxs.append(ix)
        logits = jnp.where(lane == ix, -jnp.inf, logits)
    es = [jnp.exp(v - vals[0]) for v in vals]
    den = es[0] + es[1] + es[2] + es[3]
    idx_t = jnp.full(lane.shape, N_EXPERTS, jnp.int32)
    gate_t = jnp.zeros(lane.shape, F32)
    for k in range(TOP_K):
        idx_t = jnp.where(lane == k, idxs[k], idx_t)
        gate_t = jnp.where(lane == k, es[k] / den, gate_t)
    return idx_t, gate_t


def _mid(h, ya, yb, wglu, bglu, ga, gb, wout, gx, wq, kmem, vmem, wo, gm, wr2, br):
    s = h.shape[0]
    tm = min(ROW_TILE, s)
    row = lambda w: pl.BlockSpec((tm, w), lambda i: (i, 0))
    consts = [wglu, bglu, ga, gb, wout, gx, wq, kmem, vmem, wo, gm, wr2, br]
    return pl.pallas_call(
        _mid_body,
        out_shape=[jax.ShapeDtypeStruct((s, D_MODEL), F32), jax.ShapeDtypeStruct((s, D_MODEL // 2), jnp.int32),
                   jax.ShapeDtypeStruct((s, LANES), F32)],
        grid=(s // tm,),
        in_specs=[row(D_MODEL), row(D_A), row(D_B)] + [_full(a) for a in consts],
        out_specs=[row(D_MODEL), row(D_MODEL // 2), row(LANES)],
        compiler_params=_params(("parallel",)),
        name="mid",
    )(h, ya, yb, *consts)


def _memkv_body(mem_ref, g_ref, w_ref, o_ref):
    o_ref[...] = _dot(_rms(mem_ref[...], g_ref[...]).astype(BF16), w_ref[...]).astype(BF16)


def _memkv(mem, g, w):
    n, d = mem.shape
    return pl.pallas_call(
        _memkv_body,
        out_shape=jax.ShapeDtypeStruct((n, w.shape[1]), BF16),
        compiler_params=pltpu.CompilerParams(vmem_limit_bytes=VMEM_LIMIT),
        name="memkv",
    )(mem, g.reshape(1, d), w)


def _rank_body(logit_ref, idx_ref, gate_ref, rank_ref, cnt_ref, carry):
    @pl.when(pl.program_id(0) == 0)
    def _():
        carry[...] = jnp.zeros_like(carry)

    tm = logit_ref.shape[0]
    lane = lax.broadcasted_iota(jnp.int32, (tm, LANES), 1)
    idx, gates = _top_k_gates(logit_ref[...])
    idx_ref[...] = idx
    gate_ref[...] = gates
    hits = [lane == idx[:, k:k + 1] for k in range(TOP_K)]
    onehot = jnp.zeros((tm, LANES), F32)
    for hit in hits:
        onehot = onehot + jnp.where(hit, 1.0, 0.0)
    ri = lax.broadcasted_iota(jnp.int32, (tm, tm), 0)
    ci = lax.broadcasted_iota(jnp.int32, (tm, tm), 1)
    tri = jnp.where(ri >= ci, 1.0, 0.0).astype(BF16)
    inclusive = _dot(tri, onehot.astype(BF16))
    before = carry[...] + inclusive - onehot
    rank = jnp.zeros((tm, LANES), jnp.int32)
    for k, hit in enumerate(hits):
        rk = jnp.sum(jnp.where(hit, before, 0.0), axis=-1, keepdims=True)
        rank = jnp.where(lane == k, rk.astype(jnp.int32), rank)
    rank_ref[...] = rank
    total = carry[...] + inclusive[tm - 1:tm, :]
    carry[...] = total
    cnt_ref[...] = total.astype(jnp.int32)


def _rank(logits):
    t = logits.shape[0]
    tm = min(ROW_TILE, t)
    rows = pl.BlockSpec((tm, LANES), lambda i: (i, 0))
    return pl.pallas_call(
        _rank_body,
        out_shape=[jax.ShapeDtypeStruct((t, LANES), jnp.int32), jax.ShapeDtypeStruct((t, LANES), F32),
                   jax.ShapeDtypeStruct((t, LANES), jnp.int32), jax.ShapeDtypeStruct((1, LANES), jnp.int32)],
        grid=(t // tm,),
        in_specs=[rows],
        out_specs=[rows, rows, rows, pl.BlockSpec((1, LANES), lambda i: (0, 0))],
        scratch_shapes=[pltpu.VMEM((1, LANES), F32)],
        compiler_params=_params(("arbitrary",)),
        name="rank",
    )(logits)


def _route(logits, tm):
    t = logits.shape[0]
    tk = t * TOP_K
    idx, gates, rank, cnt = _rank(logits)
    counts = cnt[0, :N_EXPERTS]
    padded = (counts + tm - 1) // tm * tm
    pend = jnp.cumsum(padded)
    pstart = pend - padded
    n_rows = tk + N_EXPERTS * tm
    n_tiles = n_rows // tm
    experts = jnp.arange(N_EXPERTS, dtype=jnp.int32)
    tile_first = jnp.arange(n_tiles, dtype=jnp.int32) * tm
    last_used = jnp.max(jnp.where(padded > 0, experts, 0))
    tile_e = jnp.minimum(jnp.sum(tile_first[:, None] >= pend[None, :], axis=1), last_used).astype(jnp.int32)
    tile_rows = jnp.clip(jnp.sum(jnp.where(tile_e[:, None] == experts[None, :],
                                           (pstart + counts)[None, :], 0), axis=1) - tile_first, 0, tm)
    tile_rows = jnp.where(tile_first < pend[-1], tile_rows, 0).astype(jnp.int32)
    group = jnp.cumsum(jnp.concatenate([jnp.zeros((1,), jnp.int32),
                                        (tile_e[1:] != tile_e[:-1]).astype(jnp.int32)]))
    tile_slot = (group % 2).astype(jnp.int32)
    later = (experts[None, :] > experts[:, None]) & (padded > 0)[None, :]
    next_e = jnp.min(jnp.where(later, experts[None, :], N_EXPERTS), axis=1)
    next_e = jnp.where(next_e < N_EXPERTS, next_e, -1).astype(jnp.int32)
    tile_next = jnp.sum(jnp.where(tile_e[:, None] == experts[None, :], next_e[None, :], 0), axis=1).astype(jnp.int32)
    top = idx[:, :TOP_K]
    base = jnp.sum(jnp.where(top[:, :, None] == experts, pstart, 0), axis=-1)
    dest = (rank[:, :TOP_K] + base).T
    return gates, dest, n_rows, tile_e, tile_rows, tile_slot, tile_next


def _sc_mesh():
    return plsc.VectorSubcoreMesh(core_axis_name="c", subcore_axis_name="s",
                                  num_cores=SC_CORES, num_subcores=SC_SUBCORES)


def _sc_worker():
    return lax.axis_index("s") * SC_CORES + lax.axis_index("c")


def _scatter_rows(x, dest, n_rows):
    t, d = x.shape
    per_worker = t // (SC_CORES * SC_SUBCORES)
    chunks = per_worker // SC_ROWS

    @functools.partial(
        pl.kernel, mesh=_sc_mesh(),
        out_type=jax.ShapeDtypeStruct((n_rows, d), x.dtype),
        scratch_types=[pltpu.VMEM((SC_ROWS, d), x.dtype)] + [pltpu.VMEM((SC_ROWS,), jnp.int32)] * TOP_K
                      + [pltpu.SemaphoreType.DMA((TOP_K,))],
    )
    def scatter(x_hbm, *rest):
        dest_hbm, out_hbm, rows_v = rest[:TOP_K], rest[TOP_K], rest[TOP_K + 1]
        idx_v, sem = rest[TOP_K + 2:2 * TOP_K + 2], rest[2 * TOP_K + 2]
        base = _sc_worker() * per_worker

        @pl.loop(0, chunks)
        def _(c):
            off = pl.multiple_of(base + c * SC_ROWS, SC_ROWS)
            pltpu.sync_copy(x_hbm.at[pl.ds(off, SC_ROWS)], rows_v)
            for k in range(TOP_K):
                pltpu.sync_copy(dest_hbm[k].at[pl.ds(off, SC_ROWS)], idx_v[k])
            copies = [pltpu.async_copy(rows_v, out_hbm.at[idx_v[k]], sem.at[k]) for k in range(TOP_K)]
            for cp in copies:
                cp.wait()

    return scatter(x, *[dest[k] for k in range(TOP_K)])


def _gather_rows(table, idx):
    n, d = table.shape
    b = idx.shape[0]
    per_worker = b // (SC_CORES * SC_SUBCORES)
    chunks = per_worker // SC_ROWS
    assert chunks % 2 == 0

    @functools.partial(
        pl.kernel, mesh=_sc_mesh(),
        out_type=jax.ShapeDtypeStruct((b, d), table.dtype),
        scratch_types=[pltpu.VMEM((SC_ROWS,), jnp.int32)] * 2 + [pltpu.VMEM((SC_ROWS, d), table.dtype)] * 2
                      + [pltpu.SemaphoreType.DMA((2,)), pltpu.SemaphoreType.DMA((2,))],
    )
    def gather(table_hbm, idx_hbm, out_hbm, idx0, idx1, rows0, rows1, gsem, wsem):
        idx_v, rows_v = (idx0, idx1), (rows0, rows1)
        base = _sc_worker() * per_worker

        def rows_at(c):
            return pl.ds(pl.multiple_of(base + c * SC_ROWS, SC_ROWS), SC_ROWS)

        def fetch(c, s):
            pltpu.sync_copy(idx_hbm.at[rows_at(c)], idx_v[s])
            pltpu.async_copy(table_hbm.at[idx_v[s]], rows_v[s], gsem.at[s])

        def fetched(s):
            return pltpu.make_async_copy(table_hbm.at[idx_v[s]], rows_v[s], gsem.at[s])

        def written(c, s):
            return pltpu.make_async_copy(rows_v[s], out_hbm.at[rows_at(c)], wsem.at[s])

        fetch(0, 0)

        @pl.loop(0, chunks, step=2)
        def _(c0):
            for s in range(2):
                c = c0 + s
                fetched(s).wait()
                written(c, s).start()

                @pl.when(c >= 1)
                def _():
                    written(c - 1, 1 - s).wait()

                @pl.when(c + 1 < chunks)
                def _():
                    fetch(c + 1, 1 - s)

        written(chunks - 1, 1).wait()

    return gather(table, idx)


def _moe_body(te_ref, tv_ref, sl_ref, nx_ref, x_ref, w1_hbm, b1_ref, w2_hbm, b2_ref, o_ref,
              w1f, w2f, w1b, w2b, sem, *, layer):
    i = pl.program_id(0)
    e = te_ref[i]
    slot = sl_ref[i]
    new_expert = (i == 0) | (e != te_ref[jnp.maximum(i - 1, 0)])

    def weight_copies(expert, s):
        rows1 = w1f.shape[1] // MOE_W1_PARTS
        rows2 = w2f.shape[1] // MOE_W2_PARTS
        c1 = [pltpu.make_async_copy(w1_hbm.at[layer, expert, pl.ds(q * rows1, rows1)],
                                    w1f.at[s, pl.ds(q * rows1, rows1)], sem.at[s, q])
              for q in range(MOE_W1_PARTS)]
        c2 = [pltpu.make_async_copy(w2_hbm.at[layer, expert, pl.ds(q * rows2, rows2)],
                                    w2f.at[s, pl.ds(q * rows2, rows2)], sem.at[s, MOE_W1_PARTS + q])
              for q in range(MOE_W2_PARTS)]
        return c1 + c2

    @pl.when(i == 0)
    def _():
        for c in weight_copies(e, slot):
            c.start()

    @pl.when(new_expert)
    def _():
        for c in weight_copies(e, slot):
            c.wait()
        nxt = nx_ref[i]

        @pl.when(nxt >= 0)
        def _():
            for c in weight_copies(nxt, 1 - slot):
                c.start()

        w1b[...] = w1f[slot].astype(BF16)
        w2b[...] = w2f[slot].astype(BF16)

    half = x_ref.shape[0] // 2

    def expert(rows):
        row = lax.broadcasted_iota(jnp.int32, (rows, x_ref.shape[1]), 0)
        lo, hi = _unpack_rows(jnp.where(row < tv_ref[i], x_ref[0:rows, :], 0))
        x = jnp.concatenate([lo, hi], axis=-1).astype(BF16)
        hb = _dot(x, w1b[...]) + b1_ref[0]
        x_glu = jnp.minimum(hb[:, :D_FF], SWIGLU_LIMIT)
        x_lin = jnp.clip(hb[:, D_FF:], -SWIGLU_LIMIT, SWIGLU_LIMIT)
        act = x_glu * jax.nn.sigmoid(SWIGLU_ALPHA * x_glu) * (x_lin + 1.0)
        o_ref[0:rows, :] = _pack_rows(_dot(act.astype(BF16), w2b[...]) + b2_ref[0])

    @pl.when(tv_ref[i] > half)
    def _():
        expert(2 * half)

    @pl.when((tv_ref[i] > 0) & (tv_ref[i] <= half))
    def _():
        expert(half)
        o_ref[half:, :] = jnp.zeros((half, o_ref.shape[1]), o_ref.dtype)

    @pl.when(tv_ref[i] == 0)
    def _():
        o_ref[...] = jnp.zeros_like(o_ref)


def _moe_experts(xs, tile_e, tile_rows, tile_slot, tile_next, w1, b1, w2, b2, layer):
    n_rows = xs.shape[0]
    tm = MOE_TILE
    nl, ne, d, ff2 = w1.shape
    bias_map = lambda i, te, tv, sl, nx: (layer, te[i], 0, 0)
    grid_spec = pltpu.PrefetchScalarGridSpec(
        num_scalar_prefetch=4,
        grid=(n_rows // tm,),
        in_specs=[pl.BlockSpec((tm, d // 2), lambda i, *_: (i, 0)),
                  pl.BlockSpec(memory_space=pl.ANY),
                  pl.BlockSpec((None, 1, 1, ff2), bias_map),
                  pl.BlockSpec(memory_space=pl.ANY),
                  pl.BlockSpec((None, 1, 1, d), bias_map)],
        out_specs=pl.BlockSpec((tm, d // 2), lambda i, *_: (i, 0)),
        scratch_shapes=[pltpu.VMEM((2, d, ff2), F32), pltpu.VMEM((2, ff2 // 2, d), F32),
                        pltpu.VMEM((d, ff2), BF16), pltpu.VMEM((ff2 // 2, d), BF16),
                        pltpu.SemaphoreType.DMA((2, MOE_W1_PARTS + MOE_W2_PARTS))],
    )
    return pl.pallas_call(
        functools.partial(_moe_body, layer=layer),
        out_shape=jax.ShapeDtypeStruct((n_rows, d // 2), jnp.int32),
        grid_spec=grid_spec,
        compiler_params=_params(("arbitrary",)),
        name="moe",
    )(tile_e, tile_rows, tile_slot, tile_next, xs, w1, b1.reshape(nl, ne, 1, ff2), w2, b2.reshape(nl, ne, 1, d))


def _combine_body(h_ref, y0_ref, y1_ref, y2_ref, y3_ref, gate_ref, g_ref, o_ref, *, final):
    gates = gate_ref[...]
    lo = jnp.zeros(y0_ref.shape, F32)
    hi = jnp.zeros(y0_ref.shape, F32)
    for k, y_ref in enumerate((y0_ref, y1_ref, y2_ref, y3_ref)):
        yl, yh = _unpack_rows(y_ref[...])
        lo = lo + yl * gates[:, k:k + 1]
        hi = hi + yh * gates[:, k:k + 1]
    h = h_ref[...] + jnp.concatenate([lo, hi], axis=-1)
    o_ref[...] = _rms(h, g_ref[...]) if final else h


def _combine(h, y, gates, g, *, final):
    s, d = h.shape
    tm = min(ROW_TILE // 2, s)
    nt = s // tm
    return pl.pallas_call(
        functools.partial(_combine_body, final=final),
        out_shape=jax.ShapeDtypeStruct((s, d), F32),
        grid=(nt,),
        in_specs=[pl.BlockSpec((tm, d), lambda i: (i, 0))]
                 + [pl.BlockSpec((tm, d // 2), lambda i, k=k: (k * nt + i, 0)) for k in range(TOP_K)]
                 + [pl.BlockSpec((tm, LANES), lambda i: (i, 0)), pl.BlockSpec((1, d), lambda i: (0, 0))],
        out_specs=pl.BlockSpec((tm, d), lambda i: (i, 0)),
        compiler_params=_params(("parallel",)),
        name="combine",
    )(h, y, y, y, y, gates, g.reshape(1, d))


def _layer(h, mem, biases, p, l, g_final):
    row = lambda a: a.reshape(1, -1).astype(F32)
    q_scale = jnp.concatenate([jnp.full((D_A,), HD_A ** -0.5 * LOG2E, F32), jnp.ones((D_IN - D_A,), F32)])
    w_in = (p["w_in"][l] * q_scale).astype(BF16)
    proj = _inproj(h, p["norm_mix"][l], w_in)
    ya = _dilated_attention(proj, biases)
    ops = _s5_operators(p["s5_a_re"][l], p["s5_a_im"][l], p["s5_b_re"][l], p["s5_b_im"][l],
                        p["s5_c_re"][l], p["s5_c_im"][l], p["s5_log_dt"][l], p["s5_d"][l])
    yb = _s5_core(proj, ops)
    kv = _memkv(mem, p["norm_mem"][l], p["w_xkv"][l].astype(BF16))
    wr = p["w_router"][l].astype(F32)
    wr_hi = wr.astype(BF16)
    wr2 = jnp.pad(jnp.concatenate([wr_hi, (wr - wr_hi.astype(F32)).astype(BF16)], axis=1),
                  ((0, 0), (0, LANES - 2 * N_EXPERTS)))
    br = jnp.pad(p["b_router"][l].astype(F32), (0, LANES - N_EXPERTS)).reshape(1, LANES)
    h2, xn, logits = _mid(
        h, ya, yb, p["w_glu"][l].astype(BF16), row(p["b_glu"][l]), row(p["g_out_attn"][l]),
        row(p["g_out_ssm"][l]), p["w_out"][l].astype(BF16), row(p["norm_xattn"][l]),
        (p["w_xq"][l] * (HD_X ** -0.5)).astype(BF16), kv[:, :D_X], kv[:, D_X:],
        p["w_xo"][l].astype(BF16), row(p["norm_moe"][l]), wr2, br)
    gates, dest, n_rows, tile_e, tile_rows, tile_slot, tile_next = _route(logits, MOE_TILE)
    out = _moe_experts(_scatter_rows(xn, dest, n_rows), tile_e, tile_rows, tile_slot, tile_next,
                       p["w1"], p["b1"], p["w2"], p["b2"], l)
    return _combine(h2, _gather_rows(out, dest.reshape(-1)), gates, g_final, final=l == DEPTH - 1)


def kernel(x, mem, rel_bias, norm_mix, w_in, s5_a_re, s5_a_im, s5_b_re, s5_b_im, s5_c_re, s5_c_im, s5_log_dt, s5_d, w_glu, b_glu, g_out_attn, g_out_ssm, w_out, norm_xattn, norm_mem, w_xq, w_xkv, w_xo, norm_moe, w_router, b_router, w1, b1, w2, b2, norm_final):
    p = dict(norm_mix=norm_mix, w_in=w_in, s5_a_re=s5_a_re, s5_a_im=s5_a_im, s5_b_re=s5_b_re,
             s5_b_im=s5_b_im, s5_c_re=s5_c_re, s5_c_im=s5_c_im, s5_log_dt=s5_log_dt, s5_d=s5_d,
             w_glu=w_glu, b_glu=b_glu, g_out_attn=g_out_attn, g_out_ssm=g_out_ssm, w_out=w_out,
             norm_xattn=norm_xattn, norm_mem=norm_mem, w_xq=w_xq, w_xkv=w_xkv, w_xo=w_xo,
             norm_moe=norm_moe, w_router=w_router, b_router=b_router, w1=w1, b1=b1, w2=w2, b2=b2)
    biases = [_attn_bias(rel_bias, window, dil, perm)
              for (window, dil), perm in zip(WIN_DIL, (_PERM_D1, _PERM_D4, _PERM_D16))]
    outs = []
    for b in range(x.shape[0]):
        h = _to_span_layout(x[b])
        for l in range(DEPTH):
            h = _layer(h, mem[b], biases, p, l, norm_final)
        outs.append(_from_span_layout(h))
    return jnp.stack(outs)
```

```python
import functools
import math

import jax
import jax.numpy as jnp
import numpy as np
from jax import lax
from jax.experimental import pallas as pl
from jax.experimental.pallas import tpu as pltpu
from jax.experimental.pallas import tpu_sc as plsc

F32 = jnp.float32
BF16 = jnp.bfloat16

D_MODEL = 1024
DEPTH = 2
EPS = 1e-5
NEG_INF = -1e30
LOG2E = math.log2(math.e)
H_A = 8
HD_A = 64
D_A = H_A * HD_A
WIN_DIL = ((128, 1), (512, 4), (2048, 16))
BLK = 128
D_B = D_MODEL - D_A
S5_CH = 16
S5_G = D_B // S5_CH
S5_P = 64
D_IN = 3 * D_A + D_B
NUM_BUCKETS = 32
REL_MAX_DIST = 2048
H_X = 4
HD_X = 128
D_X = H_X * HD_X
N_EXPERTS = 32
TOP_K = 4
D_FF = D_MODEL
SWIGLU_ALPHA = 1.702
SWIGLU_LIMIT = 7.0

LANES = 128
NRES = WIN_DIL[-1][1]
SPAN = NRES * BLK
S5_CHUNK = NRES
S5_PAIRS = S5_G // 2
VMEM_LIMIT = 56 * 1024 * 1024

SC_CORES = 2
SC_SUBCORES = 16
SC_ROWS = 64

ROW_TILE = 512
ATTN_D16_UNROLL = 4
MID_CHAINS = 2
MOE_TILE = 512
MOE_W1_PARTS = 4
MOE_W2_PARTS = 2


def _params(sem):
    return pltpu.CompilerParams(dimension_semantics=sem, vmem_limit_bytes=VMEM_LIMIT)


def _rms(x, g):
    return x * lax.rsqrt(jnp.mean(x * x, axis=-1, keepdims=True) + EPS) * g


def _dot(a, b):
    return jnp.dot(a, b, preferred_element_type=F32)


def _dot_nt(a, b):
    return lax.dot_general(a, b, (((1,), (1,)), ((), ())), preferred_element_type=F32)


def _full(a):
    return pl.BlockSpec(a.shape, lambda *_: (0,) * a.ndim)


def _pack_rows(x):
    c = x.shape[1] // 2
    lo = lax.bitcast_convert_type(x[:, :c].astype(BF16).astype(F32), jnp.uint32)
    hi = lax.bitcast_convert_type(x[:, c:].astype(BF16).astype(F32), jnp.uint32)
    return lax.bitcast_convert_type(lax.shift_right_logical(lo, jnp.uint32(16)) | hi, jnp.int32)


def _unpack_rows(p):
    u = lax.bitcast_convert_type(p, jnp.uint32)
    lo = lax.bitcast_convert_type(lax.shift_left(u, jnp.uint32(16)), F32)
    hi = lax.bitcast_convert_type(u & jnp.uint32(0xFFFF0000), F32)
    return lo, hi


def _to_span_layout(x):
    s = x.shape[0]
    return x.reshape(s // SPAN, BLK, NRES, -1).transpose(0, 2, 1, 3).reshape(s, -1)


def _from_span_layout(x):
    s = x.shape[0]
    return x.reshape(s // SPAN, NRES, BLK, -1).transpose(0, 2, 1, 3).reshape(s, -1)


def _inproj_body(h_ref, g_ref, w_ref, o_ref):
    xn = _rms(h_ref[...], g_ref[...]).astype(BF16)
    o_ref[...] = _dot(xn, w_ref[...]).astype(BF16)


def _inproj(h, g, w):
    s, d = h.shape
    n = w.shape[1]
    tm = min(ROW_TILE, s)
    return pl.pallas_call(
        _inproj_body,
        out_shape=jax.ShapeDtypeStruct((s, n), BF16),
        grid=(s // tm,),
        in_specs=[pl.BlockSpec((tm, d), lambda i: (i, 0)),
                  pl.BlockSpec((1, d), lambda i: (0, 0)),
                  pl.BlockSpec((d, n), lambda i: (0, 0))],
        out_specs=pl.BlockSpec((tm, n), lambda i: (i, 0)),
        compiler_params=_params(("parallel",)),
        name="inproj",
    )(h, g.reshape(1, d), w)


def _t5_bucket(n):
    max_exact = NUM_BUCKETS // 2
    nf = jnp.maximum(n, 1).astype(F32)
    large = max_exact + (jnp.log(nf / max_exact) / math.log(REL_MAX_DIST / max_exact)
                         * (NUM_BUCKETS - max_exact)).astype(jnp.int32)
    large = jnp.minimum(large, NUM_BUCKETS - 1)
    return jnp.where(n < max_exact, n, large)


def _attn_bias(rel_bias, window, dil, perm):
    steps = window // dil
    perm = jnp.asarray(perm, jnp.int32)
    qi = perm[:, None]
    ki = jnp.concatenate([perm, BLK + perm])[None, :]
    dist = BLK + qi - ki
    in_win = (dist >= 0) & (dist <= steps)
    bucket = _t5_bucket(jnp.clip(dist, 0, steps) * dil)
    onehot = (bucket[:, :, None] == jnp.arange(NUM_BUCKETS, dtype=jnp.int32)).astype(F32)
    bias = jnp.einsum('qkb,bh->hqk', onehot, rel_bias.astype(F32), precision=lax.Precision.HIGHEST)
    bias = jnp.where(in_win[None], bias * LOG2E, NEG_INF)
    return bias.reshape(H_A // 2, 2 * BLK, 2 * BLK)


_PERM_D1 = [NRES * jl + r for r in range(NRES) for jl in range(BLK // NRES)]
_PERM_D4 = [4 * jl + i for i in range(4) for jl in range(BLK // 4)]
_PERM_D16 = list(range(BLK))


def _attn_body(q_ref, k_ref, v_ref, kp_ref, vp_ref, b1_ref, b4_ref, b16_ref, o_ref, acc, mst, lst):
    has_prev = pl.program_id(0) > 0
    lane = lax.broadcasted_iota(jnp.int32, (1, LANES), 1)
    lo = lane < HD_A
    mlo = lo.astype(BF16)
    mhi = (~lo).astype(BF16)
    col = lax.broadcasted_iota(jnp.int32, (2 * BLK, 2 * BLK), 1)
    ones = jnp.ones((2 * BLK, LANES), BF16)

    def tile(q2, kk, vv, bias, mask_prev):
        qs = jnp.concatenate([q2 * mlo, q2 * mhi], axis=0)
        s = _dot_nt(qs, kk) + bias
        if mask_prev:
            s = jnp.where(jnp.logical_or(has_prev, col >= BLK), s, NEG_INF)
        m = jnp.max(s, axis=-1, keepdims=True)
        e = jnp.exp2((s - m).astype(BF16))
        oa = _dot(e, jnp.concatenate([vv, ones], axis=1))
        o = oa[:, :LANES]
        l = oa[:, LANES:]
        return (jnp.where(lo, m[:BLK], m[BLK:]), jnp.where(lo, l[:BLK], l[BLK:]),
                jnp.where(lo, o[:BLK], o[BLK:]))

    def merge(prev, cur):
        mp, lp, ap = prev
        mc, lc, ac = cur
        mn = jnp.maximum(mp, mc)
        a = jnp.exp2(mp - mn)
        b = jnp.exp2(mc - mn)
        return mn, a * lp + b * lc, a * ap + b * ac

    def cat(xs):
        return jnp.concatenate(xs, axis=0)

    def d16_body(i, _):
        for sub in range(ATTN_D16_UNROLL):
            rows = pl.ds(pl.multiple_of((i * ATTN_D16_UNROLL + sub) * BLK, BLK), BLK)
            for hp in range(H_A // 2):
                lanes = slice(hp * LANES, (hp + 1) * LANES)
                kk = cat([kp_ref[rows, lanes], k_ref[rows, lanes]])
                vv = cat([vp_ref[rows, lanes], v_ref[rows, lanes]])
                m2, l2, o2 = tile(q_ref[rows, lanes], kk, vv, b16_ref[hp], True)
                mst[rows, lanes] = m2
                lst[rows, lanes] = l2
                acc[rows, lanes] = o2
        return 0

    lax.fori_loop(0, NRES // ATTN_D16_UNROLL, d16_body, 0)

    def d4_body(r4, _):
        for b in range(4):
            def chunk_rows(bb):
                return [pl.ds(pl.multiple_of(4 * BLK * i + BLK * r4 + 32 * bb, 32), 32) for i in range(4)]
            rows = chunk_rows(b)
            prows = chunk_rows(3 if b == 0 else b - 1)
            kprev, vprev = (kp_ref, vp_ref) if b == 0 else (k_ref, v_ref)
            for hp in range(H_A // 2):
                lanes = slice(hp * LANES, (hp + 1) * LANES)
                q2 = cat([q_ref[rr, lanes] for rr in rows])
                kk = cat([kprev[rr, lanes] for rr in prows] + [k_ref[rr, lanes] for rr in rows])
                vv = cat([vprev[rr, lanes] for rr in prows] + [v_ref[rr, lanes] for rr in rows])
                cur = tile(q2, kk, vv, b4_ref[hp], b == 0)
                prev = (cat([mst[rr, lanes] for rr in rows]), cat([lst[rr, lanes] for rr in rows]),
                        cat([acc[rr, lanes] for rr in rows]))
                mn, ln, an = merge(prev, cur)
                for i, rr in enumerate(rows):
                    part = slice(32 * i, 32 * (i + 1))
                    mst[rr, lanes] = mn[part]
                    lst[rr, lanes] = ln[part]
                    acc[rr, lanes] = an[part]
        return 0

    lax.fori_loop(0, 4, d4_body, 0)

    def d1_pair(ap, kprev, vprev, prev_ap, mask_prev):
        def tiles(a_):
            return [pl.ds(pl.multiple_of(BLK * r + 16 * a_, 16), 16) for r in range(NRES)]
        cur_t = tiles(ap)
        prev_t = tiles(prev_ap)

        def halves(ref, ts, lanes):
            xs = [ref[t, lanes].astype(F32) for t in ts]
            return cat([x[:8] for x in xs]).astype(BF16), cat([x[8:] for x in xs]).astype(BF16)

        for hp in range(H_A // 2):
            lanes = slice(hp * LANES, (hp + 1) * LANES)
            q_e, q_o = halves(q_ref, cur_t, lanes)
            k_e, k_o = halves(k_ref, cur_t, lanes)
            v_e, v_o = halves(v_ref, cur_t, lanes)
            _, k_p = halves(kprev, prev_t, lanes)
            _, v_p = halves(vprev, prev_t, lanes)
            cur_e = tile(q_e, cat([k_p, k_e]), cat([v_p, v_e]), b1_ref[hp], mask_prev)
            cur_o = tile(q_o, cat([k_e, k_o]), cat([v_e, v_o]), b1_ref[hp], False)
            ms = [mst[t, lanes] for t in cur_t]
            ls = [lst[t, lanes] for t in cur_t]
            ac = [acc[t, lanes] for t in cur_t]
            outs = []
            for half, cur in ((0, cur_e), (1, cur_o)):
                part = slice(8 * half, 8 * half + 8)
                prev = (cat([x[part] for x in ms]), cat([x[part] for x in ls]), cat([x[part] for x in ac]))
                _, ln, an = merge(prev, cur)
                outs.append(an / ln)
            for r, t in enumerate(cur_t):
                part = slice(8 * r, 8 * r + 8)
                o_ref[t, lanes] = cat([outs[0][part], outs[1][part]]).astype(o_ref.dtype)

    d1_pair(0, kp_ref, vp_ref, BLK // 16 - 1, True)

    def d1_body(ap, _):
        d1_pair(ap, k_ref, v_ref, ap - 1, False)
        return 0

    lax.fori_loop(1, BLK // 16, d1_body, 0)


def _dilated_attention(proj, biases):
    s = proj.shape[0]
    cur = lambda which: pl.BlockSpec((SPAN, D_A), lambda c: (c, which))
    prev = lambda which: pl.BlockSpec((SPAN, D_A), lambda c: (jnp.maximum(c - 1, 0), which))
    return pl.pallas_call(
        _attn_body,
        out_shape=jax.ShapeDtypeStruct((s, D_A), BF16),
        grid=(s // SPAN,),
        in_specs=[cur(0), cur(1), cur(2), prev(1), prev(2)] + [_full(b) for b in biases],
        out_specs=pl.BlockSpec((SPAN, D_A), lambda c: (c, 0)),
        scratch_shapes=[pltpu.VMEM((SPAN, D_A), F32)] * 3,
        compiler_params=_params(("arbitrary",)),
        name="attn",
    )(proj, proj, proj, proj, proj, *biases)


def _s5_operators(a_re, a_im, b_re, b_im, c_re, c_im, log_dt, d_skip):
    L = S5_CHUNK
    lam = lax.complex(a_re.astype(F32), a_im.astype(F32))
    dt = jnp.exp(log_dt.astype(F32))[:, None]
    a_bar = jnp.exp(lam * dt)
    b_bar = ((a_bar - 1.0) / lam)[..., None] * lax.complex(b_re.astype(F32), b_im.astype(F32))
    c = lax.complex(c_re.astype(F32), c_im.astype(F32))
    j = jnp.arange(L + 1, dtype=F32)
    log_a = lam * dt
    apow = jnp.exp(log_a[None] * j[:, None, None])
    kt = jnp.einsum('gdp,jgp,gpc->gcjd', c, apow[:L], b_bar).real
    skip = d_skip.astype(F32).reshape(S5_G, S5_CH, 1, 1) * jnp.eye(S5_CH)[None, :, None, :]
    kt = (kt + skip * (jnp.arange(L) == 0)[None, None, :, None]).reshape(S5_G, S5_CH, L * S5_CH)
    p = jnp.einsum('sgp,gpc->gscp', apow[:L][::-1], b_bar).reshape(S5_G, L * S5_CH, S5_P)
    ca = jnp.einsum('gdp,tgp->gptd', c, apow[1:L + 1]).reshape(S5_G, S5_P, L * S5_CH)
    a_l = apow[L]

    def pair_blocks(x):
        g, r, w = x.shape
        x = x.reshape(S5_PAIRS, 2, r, w)
        z = jnp.zeros_like(x[:, 0])
        top = jnp.concatenate([x[:, 0], z], axis=-1)
        bot = jnp.concatenate([z, x[:, 1]], axis=-1)
        return jnp.concatenate([top, bot], axis=1)

    p2 = jnp.concatenate([pair_blocks(p.real), pair_blocks(p.imag)], axis=-1)
    q2 = jnp.concatenate([pair_blocks(ca.real), pair_blocks(-ca.imag)], axis=1)
    a_lr = a_l.real.reshape(1, S5_G * S5_P)
    a_li = a_l.imag.reshape(1, S5_G * S5_P)
    return kt, p2.astype(BF16), q2.astype(BF16), a_lr, a_li


def _s5_body(u_ref, kt_ref, p_ref, q_ref, ar_ref, ai_ref, o_ref, m_ref, w_ref, y_ref,
             ere, eim, xre, xim, sre, sim):
    rows = BLK
    gw = S5_CHUNK * S5_CH
    pw = 2 * gw
    per_vreg = LANES // S5_CH
    chunk_of_lane = lax.broadcasted_iota(jnp.int32, (rows, LANES), 1) // S5_CH

    def chunk_transpose(arrs):
        arrs = list(arrs)
        for s in (4, 2, 1):
            upper = (chunk_of_lane & s) != 0
            nxt = list(arrs)
            for i in range(per_vreg):
                if i & s:
                    continue
                lo_a, hi_a = arrs[i], arrs[i + s]
                nxt[i] = jnp.where(upper, pltpu.roll(hi_a, s * S5_CH, axis=1), lo_a)
                nxt[i + s] = jnp.where(upper, hi_a, pltpu.roll(lo_a, LANES - s * S5_CH, axis=1))
            arrs = nxt
        return arrs

    @pl.when(pl.program_id(0) == 0)
    def _():
        sre[...] = jnp.zeros_like(sre)
        sim[...] = jnp.zeros_like(sim)
        lane = lax.broadcasted_iota(jnp.int32, (S5_CH, gw), 1)

        def build(g, _):
            kt = kt_ref[g]
            for s in range(S5_CHUNK):
                blk = kt if s == 0 else jnp.where(lane >= s * S5_CH, pltpu.roll(kt, s * S5_CH, axis=1), 0.0)
                m_ref[g, s * S5_CH:(s + 1) * S5_CH, :] = blk.astype(BF16)
            return 0

        lax.fori_loop(0, S5_G, build, 0)

    for b in range(D_B // LANES):
        for a in range(S5_CHUNK // per_vreg):
            srcs = [u_ref[(per_vreg * a + i) * rows:(per_vreg * a + i + 1) * rows,
                          b * LANES:(b + 1) * LANES].astype(F32) for i in range(per_vreg)]
            for gi, arr in enumerate(chunk_transpose(srcs)):
                g = per_vreg * b + gi
                w_ref[:, g * gw + a * LANES:g * gw + (a + 1) * LANES] = arr.astype(BF16)

    for pr in range(S5_PAIRS):
        e = _dot(w_ref[:, pr * pw:(pr + 1) * pw], p_ref[pr])
        ere[:, pr * LANES:(pr + 1) * LANES] = e[:, :LANES]
        eim[:, pr * LANES:(pr + 1) * LANES] = e[:, LANES:]

    ar = ar_ref[...]
    ai = ai_ref[...]

    def step(n, carry):
        xr, xi = carry
        xre[pl.ds(n, 1), :] = xr
        xim[pl.ds(n, 1), :] = xi
        nr = ar * xr - ai * xi + ere[pl.ds(n, 1), :]
        ni = ar * xi + ai * xr + eim[pl.ds(n, 1), :]
        return nr, ni

    xr, xi = lax.fori_loop(0, rows, step, (sre[...], sim[...]))
    sre[...] = xr
    sim[...] = xi

    for pr in range(S5_PAIRS):
        xin = jnp.concatenate([xre[:, pr * LANES:(pr + 1) * LANES],
                               xim[:, pr * LANES:(pr + 1) * LANES]], axis=-1).astype(BF16)
        yc = _dot(xin, q_ref[pr])
        for half in range(2):
            g = 2 * pr + half
            cols = slice(g * gw, (g + 1) * gw)
            y = _dot(w_ref[:, cols], m_ref[g]) + yc[:, half * gw:(half + 1) * gw]
            y_ref[:, cols] = 0.5 * y * (1.0 + lax.erf(y * (2.0 ** -0.5)))

    for b in range(D_B // LANES):
        for a in range(S5_CHUNK // per_vreg):
            srcs = [y_ref[:, (per_vreg * b + i) * gw + a * LANES:(per_vreg * b + i) * gw + (a + 1) * LANES]
                    for i in range(per_vreg)]
            for ri, arr in enumerate(chunk_transpose(srcs)):
                r = per_vreg * a + ri
                o_ref[r * rows:(r + 1) * rows, b * LANES:(b + 1) * LANES] = arr.astype(BF16)


def _s5_core(proj, ops):
    kt, p2, q2, a_lr, a_li = ops
    s = proj.shape[0]
    gw = S5_CHUNK * S5_CH
    wide = S5_G * gw
    nstate = S5_G * S5_P
    return pl.pallas_call(
        _s5_body,
        out_shape=jax.ShapeDtypeStruct((s, D_B), BF16),
        grid=(s // SPAN,),
        in_specs=[pl.BlockSpec((SPAN, D_B), lambda i: (i, 3 * D_A // D_B)),
                  _full(kt), _full(p2), _full(q2), _full(a_lr), _full(a_li)],
        out_specs=pl.BlockSpec((SPAN, D_B), lambda i: (i, 0)),
        scratch_shapes=[pltpu.VMEM((S5_G, gw, gw), BF16), pltpu.VMEM((BLK, wide), BF16),
                        pltpu.VMEM((BLK, wide), F32)]
                       + [pltpu.VMEM((BLK, nstate), F32)] * 4 + [pltpu.VMEM((1, nstate), F32)] * 2,
        compiler_params=_params(("arbitrary",)),
        name="s5",
    )(proj, kt, p2, q2, a_lr, a_li)


def _split_bf16(x):
    hi = x.astype(BF16)
    lo = (x - hi.astype(F32)).astype(BF16)
    return hi, lo


def _mid_body(h_ref, ya_ref, yb_ref, wglu_ref, bglu_ref, ga_ref, gb_ref, wout_ref, gx_ref, wq_ref,
              k_ref, v_ref, wo_ref, gm_ref, wr_ref, br_ref, h_out, xn_out, logit_out):
    part = h_ref.shape[0] // MID_CHAINS
    for c in range(MID_CHAINS):
        rows = slice(c * part, (c + 1) * part)
        outs = _mid_rows(h_ref[rows, :], ya_ref[rows, :], yb_ref[rows, :], wglu_ref, bglu_ref, ga_ref, gb_ref,
                         wout_ref, gx_ref, wq_ref, k_ref, v_ref, wo_ref, gm_ref, wr_ref, br_ref)
        for ref, val in zip((h_out, xn_out, logit_out), outs):
            ref[rows, :] = val


def _mid_rows(h, ya, yb, wglu_ref, bglu_ref, ga_ref, gb_ref, wout_ref, gx_ref, wq_ref,
              k_ref, v_ref, wo_ref, gm_ref, wr_ref, br_ref):
    gate = jax.nn.sigmoid(_dot(yb, wglu_ref[...]) + bglu_ref[...])
    yb2 = yb.astype(F32) * gate
    na = _rms(ya.astype(F32), ga_ref[...]).astype(BF16)
    nb = _rms(yb2, gb_ref[...]).astype(BF16)
    h1 = h + _dot(na, wout_ref[0:D_A, :]) + _dot(nb, wout_ref[D_A:D_MODEL, :])
    q = _dot(_rms(h1, gx_ref[...]).astype(BF16), wq_ref[...]).astype(BF16)
    heads = []
    for hd in range(H_X):
        lanes = slice(hd * HD_X, (hd + 1) * HD_X)
        s = _dot_nt(q[:, lanes], k_ref[:, lanes])
        e = jnp.exp(s - jnp.max(s, axis=-1, keepdims=True))
        heads.append(_dot(e.astype(BF16), v_ref[:, lanes]) / jnp.sum(e, axis=-1, keepdims=True))
    o = jnp.concatenate(heads, axis=-1).astype(BF16)
    h2 = h1 + _dot(o, wo_ref[...])
    xn = _rms(h2, gm_ref[...])
    x_hi, x_lo = _split_bf16(xn)
    part = _dot(x_hi, wr_ref[...]) + _dot(x_lo, wr_ref[...])
    logits = part + pltpu.roll(part, LANES - N_EXPERTS, axis=1) + br_ref[...]
    return h2, _pack_rows(xn), logits


def _top_k_gates(logits):
    lane = lax.broadcasted_iota(jnp.int32, logits.shape, 1)
    logits = jnp.where(lane < N_EXPERTS, logits, -jnp.inf)
    vals, idxs = [], []
    for _ in range(TOP_K):
        mx = jnp.max(logits, axis=-1, keepdims=True)
        ix = jnp.min(jnp.where(logits == mx, lane, LANES), axis=-1, keepdims=True)
        vals.append(mx)
        idxs.append(ix)
        logits = jnp.where(lane == ix, -jnp.inf, logits)
    es = [jnp.exp(v - vals[0]) for v in vals]
    den = es[0] + es[1] + es[2] + es[3]
    idx_t = jnp.full(lane.shape, N_EXPERTS, jnp.int32)
    gate_t = jnp.zeros(lane.shape, F32)
    for k in range(TOP_K):
        idx_t = jnp.where(lane == k, idxs[k], idx_t)
        gate_t = jnp.where(lane == k, es[k] / den, gate_t)
    return idx_t, gate_t


def _mid(h, ya, yb, wglu, bglu, ga, gb, wout, gx, wq, kmem, vmem, wo, gm, wr2, br):
    s = h.shape[0]
    tm = min(ROW_TILE, s)
    row = lambda w: pl.BlockSpec((tm, w), lambda i: (i, 0))
    consts = [wglu, bglu, ga, gb, wout, gx, wq, kmem, vmem, wo, gm, wr2, br]
    return pl.pallas_call(
        _mid_body,
        out_shape=[jax.ShapeDtypeStruct((s, D_MODEL), F32), jax.ShapeDtypeStruct((s, D_MODEL // 2), jnp.int32),
                   jax.ShapeDtypeStruct((s, LANES), F32)],
        grid=(s // tm,),
        in_specs=[row(D_MODEL), row(D_A), row(D_B)] + [_full(a) for a in consts],
        out_specs=[row(D_MODEL), row(D_MODEL // 2), row(LANES)],
        compiler_params=_params(("parallel",)),
        name="mid",
    )(h, ya, yb, *consts)


def _memkv_body(mem_ref, g_ref, w_ref, o_ref):
    o_ref[...] = _dot(_rms(mem_ref[...], g_ref[...]).astype(BF16), w_ref[...]).astype(BF16)


def _memkv(mem, g, w):
    n, d = mem.shape
    return pl.pallas_call(
        _memkv_body,
        out_shape=jax.ShapeDtypeStruct((n, w.shape[1]), BF16),
        compiler_params=pltpu.CompilerParams(vmem_limit_bytes=VMEM_LIMIT),
        name="memkv",
    )(mem, g.reshape(1, d), w)


def _rank_body(logit_ref, tri_ref, idx_ref, gate_ref, rank_ref, cnt_ref, carry):
    @pl.when(pl.program_id(0) == 0)
    def _():
        carry[...] = jnp.zeros_like(carry)

    tm = logit_ref.shape[0]
    lane = lax.broadcasted_iota(jnp.int32, (tm, LANES), 1)
    idx, gates = _top_k_gates(logit_ref[...])
    idx_ref[...] = idx
    gate_ref[...] = gates
    hits = [lane == idx[:, k:k + 1] for k in range(TOP_K)]
    onehot = jnp.zeros((tm, LANES), F32)
    for hit in hits:
        onehot = onehot + jnp.where(hit, 1.0, 0.0)
    inclusive = _dot(tri_ref[...], onehot.astype(BF16))
    before = carry[...] + inclusive - onehot
    rank = jnp.zeros((tm, LANES), jnp.int32)
    for k, hit in enumerate(hits):
        rk = jnp.sum(jnp.where(hit, before, 0.0), axis=-1, keepdims=True)
        rank = jnp.where(lane == k, rk.astype(jnp.int32), rank)
    rank_ref[...] = rank
    total = carry[...] + inclusive[tm - 1:tm, :]
    carry[...] = total
    cnt_ref[...] = total.astype(jnp.int32)


def _rank(logits):
    t = logits.shape[0]
    tm = min(ROW_TILE, t)
    rows = pl.BlockSpec((tm, LANES), lambda i: (i, 0))
    tri = jnp.asarray(np.tril(np.ones((tm, tm), np.float32)), BF16)
    return pl.pallas_call(
        _rank_body,
        out_shape=[jax.ShapeDtypeStruct((t, LANES), jnp.int32), jax.ShapeDtypeStruct((t, LANES), F32),
                   jax.ShapeDtypeStruct((t, LANES), jnp.int32), jax.ShapeDtypeStruct((1, LANES), jnp.int32)],
        grid=(t // tm,),
        in_specs=[rows, _full(tri)],
        out_specs=[rows, rows, rows, pl.BlockSpec((1, LANES), lambda i: (0, 0))],
        scratch_shapes=[pltpu.VMEM((1, LANES), F32)],
        compiler_params=_params(("arbitrary",)),
        name="rank",
    )(logits, tri)


def _route(logits, tm):
    t = logits.shape[0]
    tk = t * TOP_K
    idx, gates, rank, cnt = _rank(logits)
    counts = cnt[0, :N_EXPERTS]
    padded = (counts + tm - 1) // tm * tm
    pend = jnp.cumsum(padded)
    pstart = pend - padded
    n_rows = tk + N_EXPERTS * tm
    n_tiles = n_rows // tm
    experts = jnp.arange(N_EXPERTS, dtype=jnp.int32)
    tile_first = jnp.arange(n_tiles, dtype=jnp.int32) * tm
    last_used = jnp.max(jnp.where(padded > 0, experts, 0))
    tile_e = jnp.minimum(jnp.sum(tile_first[:, None] >= pend[None, :], axis=1), last_used).astype(jnp.int32)
    tile_rows = jnp.clip(jnp.sum(jnp.where(tile_e[:, None] == experts[None, :],
                                           (pstart + counts)[None, :], 0), axis=1) - tile_first, 0, tm)
    tile_rows = jnp.where(tile_first < pend[-1], tile_rows, 0).astype(jnp.int32)
    group = jnp.cumsum(jnp.concatenate([jnp.zeros((1,), jnp.int32),
                                        (tile_e[1:] != tile_e[:-1]).astype(jnp.int32)]))
    tile_slot = (group % 2).astype(jnp.int32)
    later = (experts[None, :] > experts[:, None]) & (padded > 0)[None, :]
    next_e = jnp.min(jnp.where(later, experts[None, :], N_EXPERTS), axis=1)
    next_e = jnp.where(next_e < N_EXPERTS, next_e, -1).astype(jnp.int32)
    tile_next = jnp.sum(jnp.where(tile_e[:, None] == experts[None, :], next_e[None, :], 0), axis=1).astype(jnp.int32)
    top = idx[:, :TOP_K]
    base = jnp.sum(jnp.where(top[:, :, None] == experts, pstart, 0), axis=-1)
    dest = (rank[:, :TOP_K] + base).T
    return gates, dest, n_rows, tile_e, tile_rows, tile_slot, tile_next


def _sc_mesh():
    return plsc.VectorSubcoreMesh(core_axis_name="c", subcore_axis_name="s",
                                  num_cores=SC_CORES, num_subcores=SC_SUBCORES)


def _sc_worker():
    return lax.axis_index("s") * SC_CORES + lax.axis_index("c")


def _scatter_rows(x, dest, n_rows):
    t, d = x.shape
    per_worker = t // (SC_CORES * SC_SUBCORES)
    chunks = per_worker // SC_ROWS

    @functools.partial(
        pl.kernel, mesh=_sc_mesh(),
        out_type=jax.ShapeDtypeStruct((n_rows, d), x.dtype),
        scratch_types=[pltpu.VMEM((SC_ROWS, d), x.dtype)] + [pltpu.VMEM((SC_ROWS,), jnp.int32)] * TOP_K
                      + [pltpu.SemaphoreType.DMA((TOP_K,))],
    )
    def scatter(x_hbm, *rest):
        dest_hbm, out_hbm, rows_v = rest[:TOP_K], rest[TOP_K], rest[TOP_K + 1]
        idx_v, sem = rest[TOP_K + 2:2 * TOP_K + 2], rest[2 * TOP_K + 2]
        base = _sc_worker() * per_worker

        @pl.loop(0, chunks)
        def _(c):
            off = pl.multiple_of(base + c * SC_ROWS, SC_ROWS)
            pltpu.sync_copy(x_hbm.at[pl.ds(off, SC_ROWS)], rows_v)
            for k in range(TOP_K):
                pltpu.sync_copy(dest_hbm[k].at[pl.ds(off, SC_ROWS)], idx_v[k])
            copies = [pltpu.async_copy(rows_v, out_hbm.at[idx_v[k]], sem.at[k]) for k in range(TOP_K)]
            for cp in copies:
                cp.wait()

    return scatter(x, *[dest[k] for k in range(TOP_K)])


def _gather_rows(table, idx):
    n, d = table.shape
    b = idx.shape[0]
    per_worker = b // (SC_CORES * SC_SUBCORES)
    chunks = per_worker // SC_ROWS
    assert chunks % 2 == 0

    @functools.partial(
        pl.kernel, mesh=_sc_mesh(),
        out_type=jax.ShapeDtypeStruct((b, d), table.dtype),
        scratch_types=[pltpu.VMEM((SC_ROWS,), jnp.int32)] * 2 + [pltpu.VMEM((SC_ROWS, d), table.dtype)] * 2
                      + [pltpu.SemaphoreType.DMA((2,)), pltpu.SemaphoreType.DMA((2,))],
    )
    def gather(table_hbm, idx_hbm, out_hbm, idx0, idx1, rows0, rows1, gsem, wsem):
        idx_v, rows_v = (idx0, idx1), (rows0, rows1)
        base = _sc_worker() * per_worker

        def rows_at(c):
            return pl.ds(pl.multiple_of(base + c * SC_ROWS, SC_ROWS), SC_ROWS)

        def fetch(c, s):
            pltpu.sync_copy(idx_hbm.at[rows_at(c)], idx_v[s])
            pltpu.async_copy(table_hbm.at[idx_v[s]], rows_v[s], gsem.at[s])

        def fetched(s):
            return pltpu.make_async_copy(table_hbm.at[idx_v[s]], rows_v[s], gsem.at[s])

        def written(c, s):
            return pltpu.make_async_copy(rows_v[s], out_hbm.at[rows_at(c)], wsem.at[s])

        fetch(0, 0)

        @pl.loop(0, chunks, step=2)
        def _(c0):
            for s in range(2):
                c = c0 + s
                fetched(s).wait()
                written(c, s).start()

                @pl.when(c >= 1)
                def _():
                    written(c - 1, 1 - s).wait()

                @pl.when(c + 1 < chunks)
                def _():
                    fetch(c + 1, 1 - s)

        written(chunks - 1, 1).wait()

    return gather(table, idx)


def _moe_body(te_ref, tv_ref, sl_ref, nx_ref, x_ref, w1_hbm, b1_ref, w2_hbm, b2_ref, o_ref,
              w1f, w2f, w1b, w2b, sem, *, layer):
    i = pl.program_id(0)
    e = te_ref[i]
    slot = sl_ref[i]
    new_expert = (i == 0) | (e != te_ref[jnp.maximum(i - 1, 0)])

    def weight_copies(expert, s):
        rows1 = w1f.shape[1] // MOE_W1_PARTS
        rows2 = w2f.shape[1] // MOE_W2_PARTS
        c1 = [pltpu.make_async_copy(w1_hbm.at[layer, expert, pl.ds(q * rows1, rows1)],
                                    w1f.at[s, pl.ds(q * rows1, rows1)], sem.at[s, q])
              for q in range(MOE_W1_PARTS)]
        c2 = [pltpu.make_async_copy(w2_hbm.at[layer, expert, pl.ds(q * rows2, rows2)],
                                    w2f.at[s, pl.ds(q * rows2, rows2)], sem.at[s, MOE_W1_PARTS + q])
              for q in range(MOE_W2_PARTS)]
        return c1 + c2

    @pl.when(i == 0)
    def _():
        for c in weight_copies(e, slot):
            c.start()

    @pl.when(new_expert)
    def _():
        for c in weight_copies(e, slot):
            c.wait()
        nxt = nx_ref[i]

        @pl.when(nxt >= 0)
        def _():
            for c in weight_copies(nxt, 1 - slot):
                c.start()

        w1b[...] = w1f[slot].astype(BF16)
        w2b[...] = w2f[slot].astype(BF16)

    half = x_ref.shape[0] // 2

    def expert(rows):
        row = lax.broadcasted_iota(jnp.int32, (rows, x_ref.shape[1]), 0)
        lo, hi = _unpack_rows(jnp.where(row < tv_ref[i], x_ref[0:rows, :], 0))
        x = jnp.concatenate([lo, hi], axis=-1).astype(BF16)
        hb = _dot(x, w1b[...]) + b1_ref[0]
        x_glu = jnp.minimum(hb[:, :D_FF], SWIGLU_LIMIT)
        x_lin = jnp.clip(hb[:, D_FF:], -SWIGLU_LIMIT, SWIGLU_LIMIT)
        act = x_glu * jax.nn.sigmoid(SWIGLU_ALPHA * x_glu) * (x_lin + 1.0)
        o_ref[0:rows, :] = _pack_rows(_dot(act.astype(BF16), w2b[...]) + b2_ref[0])

    @pl.when(tv_ref[i] > half)
    def _():
        expert(2 * half)

    @pl.when((tv_ref[i] > 0) & (tv_ref[i] <= half))
    def _():
        expert(half)
        o_ref[half:, :] = jnp.zeros((half, o_ref.shape[1]), o_ref.dtype)

    @pl.when(tv_ref[i] == 0)
    def _():
        o_ref[...] = jnp.zeros_like(o_ref)


def _moe_experts(xs, tile_e, tile_rows, tile_slot, tile_next, w1, b1, w2, b2, layer):
    n_rows = xs.shape[0]
    tm = MOE_TILE
    nl, ne, d, ff2 = w1.shape
    bias_map = lambda i, te, tv, sl, nx: (layer, te[i], 0, 0)
    grid_spec = pltpu.PrefetchScalarGridSpec(
        num_scalar_prefetch=4,
        grid=(n_rows // tm,),
        in_specs=[pl.BlockSpec((tm, d // 2), lambda i, *_: (i, 0)),
                  pl.BlockSpec(memory_space=pl.ANY),
                  pl.BlockSpec((None, 1, 1, ff2), bias_map),
                  pl.BlockSpec(memory_space=pl.ANY),
                  pl.BlockSpec((None, 1, 1, d), bias_map)],
        out_specs=pl.BlockSpec((tm, d // 2), lambda i, *_: (i, 0)),
        scratch_shapes=[pltpu.VMEM((2, d, ff2), F32), pltpu.VMEM((2, ff2 // 2, d), F32),
                        pltpu.VMEM((d, ff2), BF16), pltpu.VMEM((ff2 // 2, d), BF16),
                        pltpu.SemaphoreType.DMA((2, MOE_W1_PARTS + MOE_W2_PARTS))],
    )
    return pl.pallas_call(
        functools.partial(_moe_body, layer=layer),
        out_shape=jax.ShapeDtypeStruct((n_rows, d // 2), jnp.int32),
        grid_spec=grid_spec,
        compiler_params=_params(("arbitrary",)),
        name="moe",
    )(tile_e, tile_rows, tile_slot, tile_next, xs, w1, b1.reshape(nl, ne, 1, ff2), w2, b2.reshape(nl, ne, 1, d))


def _combine_body(h_ref, y0_ref, y1_ref, y2_ref, y3_ref, gate_ref, g_ref, o_ref, *, final):
    gates = gate_ref[...]
    lo = jnp.zeros(y0_ref.shape, F32)
    hi = jnp.zeros(y0_ref.shape, F32)
    for k, y_ref in enumerate((y0_ref, y1_ref, y2_ref, y3_ref)):
        yl, yh = _unpack_rows(y_ref[...])
        lo = lo + yl * gates[:, k:k + 1]
        hi = hi + yh * gates[:, k:k + 1]
    h = h_ref[...] + jnp.concatenate([lo, hi], axis=-1)
    o_ref[...] = _rms(h, g_ref[...]) if final else h


def _combine(h, y, gates, g, *, final):
    s, d = h.shape
    tm = min(ROW_TILE // 2, s)
    nt = s // tm
    return pl.pallas_call(
        functools.partial(_combine_body, final=final),
        out_shape=jax.ShapeDtypeStruct((s, d), F32),
        grid=(nt,),
        in_specs=[pl.BlockSpec((tm, d), lambda i: (i, 0))]
                 + [pl.BlockSpec((tm, d // 2), lambda i, k=k: (k * nt + i, 0)) for k in range(TOP_K)]
                 + [pl.BlockSpec((tm, LANES), lambda i: (i, 0)), pl.BlockSpec((1, d), lambda i: (0, 0))],
        out_specs=pl.BlockSpec((tm, d), lambda i: (i, 0)),
        compiler_params=_params(("parallel",)),
        name="combine",
    )(h, y, y, y, y, gates, g.reshape(1, d))


def _layer(h, mem, biases, p, l, g_final):
    row = lambda a: a.reshape(1, -1).astype(F32)
    q_scale = jnp.concatenate([jnp.full((D_A,), HD_A ** -0.5 * LOG2E, F32), jnp.ones((D_IN - D_A,), F32)])
    w_in = (p["w_in"][l] * q_scale).astype(BF16)
    proj = _inproj(h, p["norm_mix"][l], w_in)
    ya = _dilated_attention(proj, biases)
    ops = _s5_operators(p["s5_a_re"][l], p["s5_a_im"][l], p["s5_b_re"][l], p["s5_b_im"][l],
                        p["s5_c_re"][l], p["s5_c_im"][l], p["s5_log_dt"][l], p["s5_d"][l])
    yb = _s5_core(proj, ops)
    kv = _memkv(mem, p["norm_mem"][l], p["w_xkv"][l].astype(BF16))
    wr = p["w_router"][l].astype(F32)
    wr_hi = wr.astype(BF16)
    wr2 = jnp.pad(jnp.concatenate([wr_hi, (wr - wr_hi.astype(F32)).astype(BF16)], axis=1),
                  ((0, 0), (0, LANES - 2 * N_EXPERTS)))
    br = jnp.pad(p["b_router"][l].astype(F32), (0, LANES - N_EXPERTS)).reshape(1, LANES)
    h2, xn, logits = _mid(
        h, ya, yb, p["w_glu"][l].astype(BF16), row(p["b_glu"][l]), row(p["g_out_attn"][l]),
        row(p["g_out_ssm"][l]), p["w_out"][l].astype(BF16), row(p["norm_xattn"][l]),
        (p["w_xq"][l] * (HD_X ** -0.5)).astype(BF16), kv[:, :D_X], kv[:, D_X:],
        p["w_xo"][l].astype(BF16), row(p["norm_moe"][l]), wr2, br)
    gates, dest, n_rows, tile_e, tile_rows, tile_slot, tile_next = _route(logits, MOE_TILE)
    out = _moe_experts(_scatter_rows(xn, dest, n_rows), tile_e, tile_rows, tile_slot, tile_next,
                       p["w1"], p["b1"], p["w2"], p["b2"], l)
    return _combine(h2, _gather_rows(out, dest.reshape(-1)), gates, g_final, final=l == DEPTH - 1)


def kernel(x, mem, rel_bias, norm_mix, w_in, s5_a_re, s5_a_im, s5_b_re, s5_b_im, s5_c_re, s5_c_im, s5_log_dt, s5_d, w_glu, b_glu, g_out_attn, g_out_ssm, w_out, norm_xattn, norm_mem, w_xq, w_xkv, w_xo, norm_moe, w_router, b_router, w1, b1, w2, b2, norm_final):
    p = dict(norm_mix=norm_mix, w_in=w_in, s5_a_re=s5_a_re, s5_a_im=s5_a_im, s5_b_re=s5_b_re,
             s5_b_im=s5_b_im, s5_c_re=s5_c_re, s5_c_im=s5_c_im, s5_log_dt=s5_log_dt, s5_d=s5_d,
             w_glu=w_glu, b_glu=b_glu, g_out_attn=g_out_attn, g_out_ssm=g_out_ssm, w_out=w_out,
             norm_xattn=norm_xattn, norm_mem=norm_mem, w_xq=w_xq, w_xkv=w_xkv, w_xo=w_xo,
             norm_moe=norm_moe, w_router=w_router, b_router=b_router, w1=w1, b1=b1, w2=w2, b2=b2)
    biases = [_attn_bias(rel_bias, window, dil, perm)
              for (window, dil), perm in zip(WIN_DIL, (_PERM_D1, _PERM_D4, _PERM_D16))]
    outs = []
    for b in range(x.shape[0]):
        h = _to_span_layout(x[b])
        for l in range(DEPTH):
            h = _layer(h, mem[b], biases, p, l, norm_final)
        outs.append(_from_span_layout(h))
    return jnp.stack(outs)
```

```python
import functools
import math

import jax
import jax.numpy as jnp
import numpy as np
from jax import lax
from jax.experimental import pallas as pl
from jax.experimental.pallas import tpu as pltpu
from jax.experimental.pallas import tpu_sc as plsc

F32 = jnp.float32
BF16 = jnp.bfloat16

D_MODEL = 1024
DEPTH = 2
EPS = 1e-5
NEG_INF = -1e30
LOG2E = math.log2(math.e)
H_A = 8
HD_A = 64
D_A = H_A * HD_A
WIN_DIL = ((128, 1), (512, 4), (2048, 16))
BLK = 128
D_B = D_MODEL - D_A
S5_CH = 16
S5_G = D_B // S5_CH
S5_P = 64
D_IN = 3 * D_A + D_B
NUM_BUCKETS = 32
REL_MAX_DIST = 2048
H_X = 4
HD_X = 128
D_X = H_X * HD_X
N_EXPERTS = 32
TOP_K = 4
D_FF = D_MODEL
SWIGLU_ALPHA = 1.702
SWIGLU_LIMIT = 7.0

LANES = 128
SUBLANES = 8
NRES = WIN_DIL[-1][1]
SPAN = NRES * BLK
S5_CHUNK = NRES
S5_PAIRS = S5_G // 2
VMEM_LIMIT = 56 * 1024 * 1024

SC_CORES = 2
SC_SUBCORES = 16
SC_ROWS = 64

ROW_TILE = 512
ATTN_D16_UNROLL = 4
MID_CHAINS = 2
RANK_TILE = 1024
MOE_TILE = 512
MOE_W1_PARTS = 4
MOE_W2_PARTS = 2


def _params(sem):
    return pltpu.CompilerParams(dimension_semantics=sem, vmem_limit_bytes=VMEM_LIMIT)


def _rms(x, g):
    return x * lax.rsqrt(jnp.mean(x * x, axis=-1, keepdims=True) + EPS) * g


def _dot(a, b):
    return jnp.dot(a, b, preferred_element_type=F32)


def _dot_nt(a, b):
    return lax.dot_general(a, b, (((1,), (1,)), ((), ())), preferred_element_type=F32)


def _full(a):
    return pl.BlockSpec(a.shape, lambda *_: (0,) * a.ndim)


def _pack_rows(x):
    c = x.shape[1] // 2
    lo = lax.bitcast_convert_type(x[:, :c].astype(BF16).astype(F32), jnp.uint32)
    hi = lax.bitcast_convert_type(x[:, c:].astype(BF16).astype(F32), jnp.uint32)
    return lax.bitcast_convert_type(lax.shift_right_logical(lo, jnp.uint32(16)) | hi, jnp.int32)


def _unpack_rows(p):
    u = lax.bitcast_convert_type(p, jnp.uint32)
    lo = lax.bitcast_convert_type(lax.shift_left(u, jnp.uint32(16)), F32)
    hi = lax.bitcast_convert_type(u & jnp.uint32(0xFFFF0000), F32)
    return lo, hi


def _to_span_layout(x):
    s = x.shape[0]
    return x.reshape(s // SPAN, BLK, NRES, -1).transpose(0, 2, 1, 3).reshape(s, -1)


def _from_span_layout(x):
    s = x.shape[0]
    return x.reshape(s // SPAN, NRES, BLK, -1).transpose(0, 2, 1, 3).reshape(s, -1)


def _inproj_body(h_ref, g_ref, w_ref, o_ref):
    xn = _rms(h_ref[...], g_ref[...]).astype(BF16)
    o_ref[...] = _dot(xn, w_ref[...]).astype(BF16)


def _inproj(h, g, w):
    s, d = h.shape
    n = w.shape[1]
    tm = min(ROW_TILE, s)
    return pl.pallas_call(
        _inproj_body,
        out_shape=jax.ShapeDtypeStruct((s, n), BF16),
        grid=(s // tm,),
        in_specs=[pl.BlockSpec((tm, d), lambda i: (i, 0)),
                  pl.BlockSpec((1, d), lambda i: (0, 0)),
                  pl.BlockSpec((d, n), lambda i: (0, 0))],
        out_specs=pl.BlockSpec((tm, n), lambda i: (i, 0)),
        compiler_params=_params(("parallel",)),
        name="inproj",
    )(h, g.reshape(1, d), w)


def _t5_bucket(n):
    max_exact = NUM_BUCKETS // 2
    nf = jnp.maximum(n, 1).astype(F32)
    large = max_exact + (jnp.log(nf / max_exact) / math.log(REL_MAX_DIST / max_exact)
                         * (NUM_BUCKETS - max_exact)).astype(jnp.int32)
    large = jnp.minimum(large, NUM_BUCKETS - 1)
    return jnp.where(n < max_exact, n, large)


def _attn_bias(rel_bias, window, dil, perm):
    steps = window // dil
    perm = jnp.asarray(perm, jnp.int32)
    qi = perm[:, None]
    ki = jnp.concatenate([perm, BLK + perm])[None, :]
    dist = BLK + qi - ki
    in_win = (dist >= 0) & (dist <= steps)
    bucket = _t5_bucket(jnp.clip(dist, 0, steps) * dil)
    onehot = (bucket[:, :, None] == jnp.arange(NUM_BUCKETS, dtype=jnp.int32)).astype(F32)
    bias = jnp.einsum('qkb,bh->hqk', onehot, rel_bias.astype(F32), precision=lax.Precision.HIGHEST)
    bias = jnp.where(in_win[None], bias * LOG2E, NEG_INF)
    return bias.reshape(H_A // 2, 2 * BLK, 2 * BLK)


_PERM_D1 = [NRES * jl + r for r in range(NRES) for jl in range(BLK // NRES)]
_PERM_D4 = [4 * jl + i for i in range(4) for jl in range(BLK // 4)]
_PERM_D16 = list(range(BLK))


def _attn_body(q_ref, k_ref, v_ref, kp_ref, vp_ref, b1_ref, b4_ref, b16_ref, o_ref, acc, mst, lst):
    has_prev = pl.program_id(0) > 0
    lane = lax.broadcasted_iota(jnp.int32, (1, LANES), 1)
    lo = lane < HD_A
    mlo = lo.astype(BF16)
    mhi = (~lo).astype(BF16)
    col = lax.broadcasted_iota(jnp.int32, (2 * BLK, 2 * BLK), 1)
    ones = jnp.ones((2 * BLK, LANES), BF16)

    def tile(q2, kk, vv, bias, mask_prev):
        qs = jnp.concatenate([q2 * mlo, q2 * mhi], axis=0)
        s = _dot_nt(qs, kk) + bias
        if mask_prev:
            s = jnp.where(jnp.logical_or(has_prev, col >= BLK), s, NEG_INF)
        m = jnp.max(s, axis=-1, keepdims=True)
        e = jnp.exp2((s - m).astype(BF16))
        oa = _dot(e, jnp.concatenate([vv, ones], axis=1))
        o = oa[:, :LANES]
        l = oa[:, LANES:]
        return (jnp.where(lo, m[:BLK], m[BLK:]), jnp.where(lo, l[:BLK], l[BLK:]),
                jnp.where(lo, o[:BLK], o[BLK:]))

    def merge(prev, cur):
        mp, lp, ap = prev
        mc, lc, ac = cur
        mn = jnp.maximum(mp, mc)
        a = jnp.exp2(mp - mn)
        b = jnp.exp2(mc - mn)
        return mn, a * lp + b * lc, a * ap + b * ac

    def cat(xs):
        return jnp.concatenate(xs, axis=0)

    def d16_body(i, _):
        for sub in range(ATTN_D16_UNROLL):
            rows = pl.ds(pl.multiple_of((i * ATTN_D16_UNROLL + sub) * BLK, BLK), BLK)
            for hp in range(H_A // 2):
                lanes = slice(hp * LANES, (hp + 1) * LANES)
                kk = cat([kp_ref[rows, lanes], k_ref[rows, lanes]])
                vv = cat([vp_ref[rows, lanes], v_ref[rows, lanes]])
                m2, l2, o2 = tile(q_ref[rows, lanes], kk, vv, b16_ref[hp], True)
                mst[rows, lanes] = m2
                lst[rows, lanes] = l2
                acc[rows, lanes] = o2
        return 0

    lax.fori_loop(0, NRES // ATTN_D16_UNROLL, d16_body, 0)

    def d4_body(r4, _):
        for b in range(4):
            def chunk_rows(bb):
                return [pl.ds(pl.multiple_of(4 * BLK * i + BLK * r4 + 32 * bb, 32), 32) for i in range(4)]
            rows = chunk_rows(b)
            prows = chunk_rows(3 if b == 0 else b - 1)
            kprev, vprev = (kp_ref, vp_ref) if b == 0 else (k_ref, v_ref)
            for hp in range(H_A // 2):
                lanes = slice(hp * LANES, (hp + 1) * LANES)
                q2 = cat([q_ref[rr, lanes] for rr in rows])
                kk = cat([kprev[rr, lanes] for rr in prows] + [k_ref[rr, lanes] for rr in rows])
                vv = cat([vprev[rr, lanes] for rr in prows] + [v_ref[rr, lanes] for rr in rows])
                cur = tile(q2, kk, vv, b4_ref[hp], b == 0)
                prev = (cat([mst[rr, lanes] for rr in rows]), cat([lst[rr, lanes] for rr in rows]),
                        cat([acc[rr, lanes] for rr in rows]))
                mn, ln, an = merge(prev, cur)
                for i, rr in enumerate(rows):
                    part = slice(32 * i, 32 * (i + 1))
                    mst[rr, lanes] = mn[part]
                    lst[rr, lanes] = ln[part]
                    acc[rr, lanes] = an[part]
        return 0

    lax.fori_loop(0, 4, d4_body, 0)

    def d1_pair(ap, kprev, vprev, prev_ap, mask_prev):
        def tiles(a_):
            return [pl.ds(pl.multiple_of(BLK * r + 16 * a_, 16), 16) for r in range(NRES)]
        cur_t = tiles(ap)
        prev_t = tiles(prev_ap)

        def halves(ref, ts, lanes):
            xs = [ref[t, lanes].astype(F32) for t in ts]
            return cat([x[:8] for x in xs]).astype(BF16), cat([x[8:] for x in xs]).astype(BF16)

        for hp in range(H_A // 2):
            lanes = slice(hp * LANES, (hp + 1) * LANES)
            q_e, q_o = halves(q_ref, cur_t, lanes)
            k_e, k_o = halves(k_ref, cur_t, lanes)
            v_e, v_o = halves(v_ref, cur_t, lanes)
            _, k_p = halves(kprev, prev_t, lanes)
            _, v_p = halves(vprev, prev_t, lanes)
            cur_e = tile(q_e, cat([k_p, k_e]), cat([v_p, v_e]), b1_ref[hp], mask_prev)
            cur_o = tile(q_o, cat([k_e, k_o]), cat([v_e, v_o]), b1_ref[hp], False)
            ms = [mst[t, lanes] for t in cur_t]
            ls = [lst[t, lanes] for t in cur_t]
            ac = [acc[t, lanes] for t in cur_t]
            outs = []
            for half, cur in ((0, cur_e), (1, cur_o)):
                part = slice(8 * half, 8 * half + 8)
                prev = (cat([x[part] for x in ms]), cat([x[part] for x in ls]), cat([x[part] for x in ac]))
                _, ln, an = merge(prev, cur)
                outs.append(an / ln)
            for r, t in enumerate(cur_t):
                part = slice(8 * r, 8 * r + 8)
                o_ref[t, lanes] = cat([outs[0][part], outs[1][part]]).astype(o_ref.dtype)

    d1_pair(0, kp_ref, vp_ref, BLK // 16 - 1, True)

    def d1_body(ap, _):
        d1_pair(ap, k_ref, v_ref, ap - 1, False)
        return 0

    lax.fori_loop(1, BLK // 16, d1_body, 0)


def _dilated_attention(proj, biases):
    s = proj.shape[0]
    cur = lambda which: pl.BlockSpec((SPAN, D_A), lambda c: (c, which))
    prev = lambda which: pl.BlockSpec((SPAN, D_A), lambda c: (jnp.maximum(c - 1, 0), which))
    return pl.pallas_call(
        _attn_body,
        out_shape=jax.ShapeDtypeStruct((s, D_A), BF16),
        grid=(s // SPAN,),
        in_specs=[cur(0), cur(1), cur(2), prev(1), prev(2)] + [_full(b) for b in biases],
        out_specs=pl.BlockSpec((SPAN, D_A), lambda c: (c, 0)),
        scratch_shapes=[pltpu.VMEM((SPAN, D_A), F32)] * 3,
        compiler_params=_params(("arbitrary",)),
        name="attn",
    )(proj, proj, proj, proj, proj, *biases)


def _s5_operators(a_re, a_im, b_re, b_im, c_re, c_im, log_dt, d_skip):
    L = S5_CHUNK
    lam = lax.complex(a_re.astype(F32), a_im.astype(F32))
    dt = jnp.exp(log_dt.astype(F32))[:, None]
    a_bar = jnp.exp(lam * dt)
    b_bar = ((a_bar - 1.0) / lam)[..., None] * lax.complex(b_re.astype(F32), b_im.astype(F32))
    c = lax.complex(c_re.astype(F32), c_im.astype(F32))
    j = jnp.arange(L + 1, dtype=F32)
    log_a = lam * dt
    apow = jnp.exp(log_a[None] * j[:, None, None])
    kt = jnp.einsum('gdp,jgp,gpc->gcjd', c, apow[:L], b_bar).real
    skip = d_skip.astype(F32).reshape(S5_G, S5_CH, 1, 1) * jnp.eye(S5_CH)[None, :, None, :]
    kt = (kt + skip * (jnp.arange(L) == 0)[None, None, :, None]).reshape(S5_G, S5_CH, L * S5_CH)
    p = jnp.einsum('sgp,gpc->gscp', apow[:L][::-1], b_bar).reshape(S5_G, L * S5_CH, S5_P)
    ca = jnp.einsum('gdp,tgp->gptd', c, apow[1:L + 1]).reshape(S5_G, S5_P, L * S5_CH)
    a_l = apow[L]

    def pair_blocks(x):
        g, r, w = x.shape
        x = x.reshape(S5_PAIRS, 2, r, w)
        z = jnp.zeros_like(x[:, 0])
        top = jnp.concatenate([x[:, 0], z], axis=-1)
        bot = jnp.concatenate([z, x[:, 1]], axis=-1)
        return jnp.concatenate([top, bot], axis=1)

    p2 = jnp.concatenate([pair_blocks(p.real), pair_blocks(p.imag)], axis=-1)
    q2 = jnp.concatenate([pair_blocks(ca.real), pair_blocks(-ca.imag)], axis=1)
    a_lr = a_l.real.reshape(1, S5_G * S5_P)
    a_li = a_l.imag.reshape(1, S5_G * S5_P)
    return kt, p2.astype(BF16), q2.astype(BF16), a_lr, a_li


def _s5_body(u_ref, kt_ref, p_ref, q_ref, ar_ref, ai_ref, o_ref, m_ref, w_ref, y_ref,
             ere, eim, xre, xim, sre, sim):
    rows = BLK
    gw = S5_CHUNK * S5_CH
    pw = 2 * gw
    per_vreg = LANES // S5_CH
    chunk_of_lane = lax.broadcasted_iota(jnp.int32, (rows, LANES), 1) // S5_CH

    def chunk_transpose(arrs):
        arrs = list(arrs)
        for s in (4, 2, 1):
            upper = (chunk_of_lane & s) != 0
            nxt = list(arrs)
            for i in range(per_vreg):
                if i & s:
                    continue
                lo_a, hi_a = arrs[i], arrs[i + s]
                nxt[i] = jnp.where(upper, pltpu.roll(hi_a, s * S5_CH, axis=1), lo_a)
                nxt[i + s] = jnp.where(upper, hi_a, pltpu.roll(lo_a, LANES - s * S5_CH, axis=1))
            arrs = nxt
        return arrs

    @pl.when(pl.program_id(0) == 0)
    def _():
        sre[...] = jnp.zeros_like(sre)
        sim[...] = jnp.zeros_like(sim)
        lane = lax.broadcasted_iota(jnp.int32, (S5_CH, gw), 1)

        def build(g, _):
            kt = kt_ref[g]
            for s in range(S5_CHUNK):
                blk = kt if s == 0 else jnp.where(lane >= s * S5_CH, pltpu.roll(kt, s * S5_CH, axis=1), 0.0)
                m_ref[g, s * S5_CH:(s + 1) * S5_CH, :] = blk.astype(BF16)
            return 0

        lax.fori_loop(0, S5_G, build, 0)

    for b in range(D_B // LANES):
        for a in range(S5_CHUNK // per_vreg):
            srcs = [u_ref[(per_vreg * a + i) * rows:(per_vreg * a + i + 1) * rows,
                          b * LANES:(b + 1) * LANES].astype(F32) for i in range(per_vreg)]
            for gi, arr in enumerate(chunk_transpose(srcs)):
                g = per_vreg * b + gi
                w_ref[:, g * gw + a * LANES:g * gw + (a + 1) * LANES] = arr.astype(BF16)

    for pr in range(S5_PAIRS):
        e = _dot(w_ref[:, pr * pw:(pr + 1) * pw], p_ref[pr])
        ere[:, pr * LANES:(pr + 1) * LANES] = e[:, :LANES]
        eim[:, pr * LANES:(pr + 1) * LANES] = e[:, LANES:]

    ar = ar_ref[...]
    ai = ai_ref[...]

    def step(n, carry):
        xr, xi = carry
        xre[pl.ds(n, 1), :] = xr
        xim[pl.ds(n, 1), :] = xi
        nr = ar * xr - ai * xi + ere[pl.ds(n, 1), :]
        ni = ar * xi + ai * xr + eim[pl.ds(n, 1), :]
        return nr, ni

    xr, xi = lax.fori_loop(0, rows, step, (sre[...], sim[...]))
    sre[...] = xr
    sim[...] = xi

    for pr in range(S5_PAIRS):
        xin = jnp.concatenate([xre[:, pr * LANES:(pr + 1) * LANES],
                               xim[:, pr * LANES:(pr + 1) * LANES]], axis=-1).astype(BF16)
        yc = _dot(xin, q_ref[pr])
        for half in range(2):
            g = 2 * pr + half
            cols = slice(g * gw, (g + 1) * gw)
            y = _dot(w_ref[:, cols], m_ref[g]) + yc[:, half * gw:(half + 1) * gw]
            y_ref[:, cols] = 0.5 * y * (1.0 + lax.erf(y * (2.0 ** -0.5)))

    for b in range(D_B // LANES):
        for a in range(S5_CHUNK // per_vreg):
            srcs = [y_ref[:, (per_vreg * b + i) * gw + a * LANES:(per_vreg * b + i) * gw + (a + 1) * LANES]
                    for i in range(per_vreg)]
            for ri, arr in enumerate(chunk_transpose(srcs)):
                r = per_vreg * a + ri
                o_ref[r * rows:(r + 1) * rows, b * LANES:(b + 1) * LANES] = arr.astype(BF16)


def _s5_core(proj, ops):
    kt, p2, q2, a_lr, a_li = ops
    s = proj.shape[0]
    gw = S5_CHUNK * S5_CH
    wide = S5_G * gw
    nstate = S5_G * S5_P
    return pl.pallas_call(
        _s5_body,
        out_shape=jax.ShapeDtypeStruct((s, D_B), BF16),
        grid=(s // SPAN,),
        in_specs=[pl.BlockSpec((SPAN, D_B), lambda i: (i, 3 * D_A // D_B)),
                  _full(kt), _full(p2), _full(q2), _full(a_lr), _full(a_li)],
        out_specs=pl.BlockSpec((SPAN, D_B), lambda i: (i, 0)),
        scratch_shapes=[pltpu.VMEM((S5_G, gw, gw), BF16), pltpu.VMEM((BLK, wide), BF16),
                        pltpu.VMEM((BLK, wide), F32)]
                       + [pltpu.VMEM((BLK, nstate), F32)] * 4 + [pltpu.VMEM((1, nstate), F32)] * 2,
        compiler_params=_params(("arbitrary",)),
        name="s5",
    )(proj, kt, p2, q2, a_lr, a_li)


def _split_bf16(x):
    hi = x.astype(BF16)
    lo = (x - hi.astype(F32)).astype(BF16)
    return hi, lo


def _mid_body(h_ref, ya_ref, yb_ref, wglu_ref, bglu_ref, ga_ref, gb_ref, wout_ref, gx_ref, wq_ref,
              k_ref, v_ref, wo_ref, gm_ref, wr_ref, br_ref, h_out, xn_out, logit_out):
    part = h_ref.shape[0] // MID_CHAINS
    for c in range(MID_CHAINS):
        rows = slice(c * part, (c + 1) * part)
        outs = _mid_rows(h_ref[rows, :], ya_ref[rows, :], yb_ref[rows, :], wglu_ref, bglu_ref, ga_ref, gb_ref,
                         wout_ref, gx_ref, wq_ref, k_ref, v_ref, wo_ref, gm_ref, wr_ref, br_ref)
        for ref, val in zip((h_out, xn_out, logit_out), outs):
            ref[rows, :] = val


def _mid_rows(h, ya, yb, wglu_ref, bglu_ref, ga_ref, gb_ref, wout_ref, gx_ref, wq_ref,
              k_ref, v_ref, wo_ref, gm_ref, wr_ref, br_ref):
    gate = jax.nn.sigmoid(_dot(yb, wglu_ref[...]) + bglu_ref[...])
    yb2 = yb.astype(F32) * gate
    na = _rms(ya.astype(F32), ga_ref[...]).astype(BF16)
    nb = _rms(yb2, gb_ref[...]).astype(BF16)
    h1 = h + _dot(na, wout_ref[0:D_A, :]) + _dot(nb, wout_ref[D_A:D_MODEL, :])
    q = _dot(_rms(h1, gx_ref[...]).astype(BF16), wq_ref[...]).astype(BF16)
    heads = []
    for hd in range(H_X):
        lanes = slice(hd * HD_X, (hd + 1) * HD_X)
        s = _dot_nt(q[:, lanes], k_ref[:, lanes])
        e = jnp.exp(s - jnp.max(s, axis=-1, keepdims=True))
        heads.append(_dot(e.astype(BF16), v_ref[:, lanes]) / jnp.sum(e, axis=-1, keepdims=True))
    o = jnp.concatenate(heads, axis=-1).astype(BF16)
    h2 = h1 + _dot(o, wo_ref[...])
    xn = _rms(h2, gm_ref[...])
    x_hi, x_lo = _split_bf16(xn)
    part = _dot(x_hi, wr_ref[...]) + _dot(x_lo, wr_ref[...])
    logits = part + pltpu.roll(part, LANES - N_EXPERTS, axis=1) + br_ref[...]
    return h2, _pack_rows(xn), logits


def _mid(h, ya, yb, wglu, bglu, ga, gb, wout, gx, wq, kmem, vmem, wo, gm, wr2, br):
    s = h.shape[0]
    tm = min(ROW_TILE, s)
    row = lambda w: pl.BlockSpec((tm, w), lambda i: (i, 0))
    consts = [wglu, bglu, ga, gb, wout, gx, wq, kmem, vmem, wo, gm, wr2, br]
    return pl.pallas_call(
        _mid_body,
        out_shape=[jax.ShapeDtypeStruct((s, D_MODEL), F32), jax.ShapeDtypeStruct((s, D_MODEL // 2), jnp.int32),
                   jax.ShapeDtypeStruct((s, LANES), F32)],
        grid=(s // tm,),
        in_specs=[row(D_MODEL), row(D_A), row(D_B)] + [_full(a) for a in consts],
        out_specs=[row(D_MODEL), row(D_MODEL // 2), row(LANES)],
        compiler_params=_params(("parallel",)),
        name="mid",
    )(h, ya, yb, *consts)


def _memkv_body(mem_ref, g_ref, w_ref, o_ref):
    o_ref[...] = _dot(_rms(mem_ref[...], g_ref[...]).astype(BF16), w_ref[...]).astype(BF16)


def _memkv(mem, g, w):
    n, d = mem.shape
    return pl.pallas_call(
        _memkv_body,
        out_shape=jax.ShapeDtypeStruct((n, w.shape[1]), BF16),
        compiler_params=pltpu.CompilerParams(vmem_limit_bytes=VMEM_LIMIT),
        name="memkv",
    )(mem, g.reshape(1, d), w)


def _rank_body(logit_ref, tri_ref, idx_ref, gate_ref, rank_ref, cnt_ref, carry):
    @pl.when(pl.program_id(0) == 0)
    def _():
        carry[...] = jnp.zeros_like(carry)

    tm = logit_ref.shape[0]
    logits = jnp.transpose(logit_ref[...])[:N_EXPERTS, :]
    expert = lax.broadcasted_iota(jnp.int32, (N_EXPERTS, tm), 0)
    vals, idxs = [], []
    for _ in range(TOP_K):
        mx = jnp.max(logits, axis=0, keepdims=True)
        ix = jnp.min(jnp.where(logits == mx, expert, N_EXPERTS), axis=0, keepdims=True)
        vals.append(mx)
        idxs.append(ix)
        logits = jnp.where(expert == ix, -jnp.inf, logits)
    es = [jnp.exp(v - vals[0]) for v in vals]
    den = es[0] + es[1] + es[2] + es[3]
    hits = [expert == ix for ix in idxs]
    onehot = jnp.zeros((N_EXPERTS, tm), F32)
    for hit in hits:
        onehot = onehot + jnp.where(hit, 1.0, 0.0)
    nb = tm // LANES
    blocks = jnp.concatenate([onehot[:, b * LANES:(b + 1) * LANES] for b in range(nb)], axis=0)
    inc = _dot(blocks.astype(BF16), tri_ref[...])
    run = carry[...][:, 0:1]
    before = []
    for b in range(nb):
        inc_b = inc[b * N_EXPERTS:(b + 1) * N_EXPERTS, :]
        before.append(run + inc_b - onehot[:, b * LANES:(b + 1) * LANES])
        run = run + inc_b[:, LANES - 1:LANES]
    before = jnp.concatenate(before, axis=1)
    choice = lax.broadcasted_iota(jnp.int32, (SUBLANES, tm), 0)
    idx_t = jnp.full((SUBLANES, tm), N_EXPERTS, jnp.int32)
    gate_t = jnp.zeros((SUBLANES, tm), F32)
    rank_t = jnp.zeros((SUBLANES, tm), jnp.int32)
    for k in range(TOP_K):
        rk = jnp.sum(jnp.where(hits[k], before, 0.0), axis=0, keepdims=True)
        idx_t = jnp.where(choice == k, idxs[k], idx_t)
        gate_t = jnp.where(choice == k, es[k] / den, gate_t)
        rank_t = jnp.where(choice == k, rk.astype(jnp.int32), rank_t)
    idx_ref[...] = idx_t
    gate_ref[...] = gate_t
    rank_ref[...] = rank_t
    total = jnp.broadcast_to(run, carry.shape)
    carry[...] = total
    cnt_ref[...] = total.astype(jnp.int32)


def _rank(logits):
    t = logits.shape[0]
    tm = min(RANK_TILE, t)
    cols = pl.BlockSpec((SUBLANES, tm), lambda i: (0, i))
    tri = jnp.asarray(np.triu(np.ones((LANES, LANES), np.float32)), BF16)
    return pl.pallas_call(
        _rank_body,
        out_shape=[jax.ShapeDtypeStruct((SUBLANES, t), jnp.int32), jax.ShapeDtypeStruct((SUBLANES, t), F32),
                   jax.ShapeDtypeStruct((SUBLANES, t), jnp.int32),
                   jax.ShapeDtypeStruct((N_EXPERTS, LANES), jnp.int32)],
        grid=(t // tm,),
        in_specs=[pl.BlockSpec((tm, LANES), lambda i: (i, 0)), _full(tri)],
        out_specs=[cols, cols, cols, pl.BlockSpec((N_EXPERTS, LANES), lambda i: (0, 0))],
        scratch_shapes=[pltpu.VMEM((N_EXPERTS, LANES), F32)],
        compiler_params=_params(("arbitrary",)),
        name="rank",
    )(logits, tri)


def _route(logits, tm):
    t = logits.shape[0]
    tk = t * TOP_K
    idx, gates, rank, cnt = _rank(logits)
    counts = cnt[:, 0]
    padded = (counts + tm - 1) // tm * tm
    pend = jnp.cumsum(padded)
    pstart = pend - padded
    n_rows = tk + N_EXPERTS * tm
    n_tiles = n_rows // tm
    experts = jnp.arange(N_EXPERTS, dtype=jnp.int32)
    tile_first = jnp.arange(n_tiles, dtype=jnp.int32) * tm
    last_used = jnp.max(jnp.where(padded > 0, experts, 0))
    tile_e = jnp.minimum(jnp.sum(tile_first[:, None] >= pend[None, :], axis=1), last_used).astype(jnp.int32)
    tile_rows = jnp.clip(jnp.sum(jnp.where(tile_e[:, None] == experts[None, :],
                                           (pstart + counts)[None, :], 0), axis=1) - tile_first, 0, tm)
    tile_rows = jnp.where(tile_first < pend[-1], tile_rows, 0).astype(jnp.int32)
    group = jnp.cumsum(jnp.concatenate([jnp.zeros((1,), jnp.int32),
                                        (tile_e[1:] != tile_e[:-1]).astype(jnp.int32)]))
    tile_slot = (group % 2).astype(jnp.int32)
    later = (experts[None, :] > experts[:, None]) & (padded > 0)[None, :]
    next_e = jnp.min(jnp.where(later, experts[None, :], N_EXPERTS), axis=1)
    next_e = jnp.where(next_e < N_EXPERTS, next_e, -1).astype(jnp.int32)
    tile_next = jnp.sum(jnp.where(tile_e[:, None] == experts[None, :], next_e[None, :], 0), axis=1).astype(jnp.int32)
    base = jnp.sum(jnp.where(idx[:TOP_K, :, None] == experts, pstart, 0), axis=-1)
    dest = rank[:TOP_K] + base
    return gates[:TOP_K].T, dest, n_rows, tile_e, tile_rows, tile_slot, tile_next


def _sc_mesh():
    return plsc.VectorSubcoreMesh(core_axis_name="c", subcore_axis_name="s",
                                  num_cores=SC_CORES, num_subcores=SC_SUBCORES)


def _sc_worker():
    return lax.axis_index("s") * SC_CORES + lax.axis_index("c")


def _scatter_rows(x, dest, n_rows):
    t, d = x.shape
    per_worker = t // (SC_CORES * SC_SUBCORES)
    chunks = per_worker // SC_ROWS

    @functools.partial(
        pl.kernel, mesh=_sc_mesh(),
        out_type=jax.ShapeDtypeStruct((n_rows, d), x.dtype),
        scratch_types=[pltpu.VMEM((SC_ROWS, d), x.dtype)] + [pltpu.VMEM((SC_ROWS,), jnp.int32)] * TOP_K
                      + [pltpu.SemaphoreType.DMA((TOP_K,))],
    )
    def scatter(x_hbm, *rest):
        dest_hbm, out_hbm, rows_v = rest[:TOP_K], rest[TOP_K], rest[TOP_K + 1]
        idx_v, sem = rest[TOP_K + 2:2 * TOP_K + 2], rest[2 * TOP_K + 2]
        base = _sc_worker() * per_worker

        @pl.loop(0, chunks)
        def _(c):
            off = pl.multiple_of(base + c * SC_ROWS, SC_ROWS)
            pltpu.sync_copy(x_hbm.at[pl.ds(off, SC_ROWS)], rows_v)
            for k in range(TOP_K):
                pltpu.sync_copy(dest_hbm[k].at[pl.ds(off, SC_ROWS)], idx_v[k])
            copies = [pltpu.async_copy(rows_v, out_hbm.at[idx_v[k]], sem.at[k]) for k in range(TOP_K)]
            for cp in copies:
                cp.wait()

    return scatter(x, *[dest[k] for k in range(TOP_K)])


def _gather_rows(table, idx):
    n, d = table.shape
    b = idx.shape[0]
    per_worker = b // (SC_CORES * SC_SUBCORES)
    chunks = per_worker // SC_ROWS
    assert chunks % 2 == 0

    @functools.partial(
        pl.kernel, mesh=_sc_mesh(),
        out_type=jax.ShapeDtypeStruct((b, d), table.dtype),
        scratch_types=[pltpu.VMEM((SC_ROWS,), jnp.int32)] * 2 + [pltpu.VMEM((SC_ROWS, d), table.dtype)] * 2
                      + [pltpu.SemaphoreType.DMA((2,)), pltpu.SemaphoreType.DMA((2,))],
    )
    def gather(table_hbm, idx_hbm, out_hbm, idx0, idx1, rows0, rows1, gsem, wsem):
        idx_v, rows_v = (idx0, idx1), (rows0, rows1)
        base = _sc_worker() * per_worker

        def rows_at(c):
            return pl.ds(pl.multiple_of(base + c * SC_ROWS, SC_ROWS), SC_ROWS)

        def fetch(c, s):
            pltpu.sync_copy(idx_hbm.at[rows_at(c)], idx_v[s])
            pltpu.async_copy(table_hbm.at[idx_v[s]], rows_v[s], gsem.at[s])

        def fetched(s):
            return pltpu.make_async_copy(table_hbm.at[idx_v[s]], rows_v[s], gsem.at[s])

        def written(c, s):
            return pltpu.make_async_copy(rows_v[s], out_hbm.at[rows_at(c)], wsem.at[s])

        fetch(0, 0)

        @pl.loop(0, chunks, step=2)
        def _(c0):
            for s in range(2):
                c = c0 + s
                fetched(s).wait()
                written(c, s).start()

                @pl.when(c >= 1)
                def _():
                    written(c - 1, 1 - s).wait()

                @pl.when(c + 1 < chunks)
                def _():
                    fetch(c + 1, 1 - s)

        written(chunks - 1, 1).wait()

    return gather(table, idx)


def _moe_body(te_ref, tv_ref, sl_ref, nx_ref, x_ref, w1_hbm, b1_ref, w2_hbm, b2_ref, o_ref,
              w1f, w2f, w1b, w2b, sem, *, layer):
    i = pl.program_id(0)
    e = te_ref[i]
    slot = sl_ref[i]
    new_expert = (i == 0) | (e != te_ref[jnp.maximum(i - 1, 0)])

    def weight_copies(expert, s):
        rows1 = w1f.shape[1] // MOE_W1_PARTS
        rows2 = w2f.shape[1] // MOE_W2_PARTS
        c1 = [pltpu.make_async_copy(w1_hbm.at[layer, expert, pl.ds(q * rows1, rows1)],
                                    w1f.at[s, pl.ds(q * rows1, rows1)], sem.at[s, q])
              for q in range(MOE_W1_PARTS)]
        c2 = [pltpu.make_async_copy(w2_hbm.at[layer, expert, pl.ds(q * rows2, rows2)],
                                    w2f.at[s, pl.ds(q * rows2, rows2)], sem.at[s, MOE_W1_PARTS + q])
              for q in range(MOE_W2_PARTS)]
        return c1 + c2

    @pl.when(i == 0)
    def _():
        for c in weight_copies(e, slot):
            c.start()

    @pl.when(new_expert)
    def _():
        for c in weight_copies(e, slot):
            c.wait()
        nxt = nx_ref[i]

        @pl.when(nxt >= 0)
        def _():
            for c in weight_copies(nxt, 1 - slot):
                c.start()

        w1b[...] = w1f[slot].astype(BF16)
        w2b[...] = w2f[slot].astype(BF16)

    half = x_ref.shape[0] // 2

    def expert(rows):
        row = lax.broadcasted_iota(jnp.int32, (rows, x_ref.shape[1]), 0)
        lo, hi = _unpack_rows(jnp.where(row < tv_ref[i], x_ref[0:rows, :], 0))
        x = jnp.concatenate([lo, hi], axis=-1).astype(BF16)
        hb = _dot(x, w1b[...]) + b1_ref[0]
        x_glu = jnp.minimum(hb[:, :D_FF], SWIGLU_LIMIT)
        x_lin = jnp.clip(hb[:, D_FF:], -SWIGLU_LIMIT, SWIGLU_LIMIT)
        act = x_glu * jax.nn.sigmoid(SWIGLU_ALPHA * x_glu) * (x_lin + 1.0)
        o_ref[0:rows, :] = _pack_rows(_dot(act.astype(BF16), w2b[...]) + b2_ref[0])

    @pl.when(tv_ref[i] > half)
    def _():
        expert(2 * half)

    @pl.when((tv_ref[i] > 0) & (tv_ref[i] <= half))
    def _():
        expert(half)
        o_ref[half:, :] = jnp.zeros((half, o_ref.shape[1]), o_ref.dtype)

    @pl.when(tv_ref[i] == 0)
    def _():
        o_ref[...] = jnp.zeros_like(o_ref)


def _moe_experts(xs, tile_e, tile_rows, tile_slot, tile_next, w1, b1, w2, b2, layer):
    n_rows = xs.shape[0]
    tm = MOE_TILE
    nl, ne, d, ff2 = w1.shape
    bias_map = lambda i, te, tv, sl, nx: (layer, te[i], 0, 0)
    grid_spec = pltpu.PrefetchScalarGridSpec(
        num_scalar_prefetch=4,
        grid=(n_rows // tm,),
        in_specs=[pl.BlockSpec((tm, d // 2), lambda i, *_: (i, 0)),
                  pl.BlockSpec(memory_space=pl.ANY),
                  pl.BlockSpec((None, 1, 1, ff2), bias_map),
                  pl.BlockSpec(memory_space=pl.ANY),
                  pl.BlockSpec((None, 1, 1, d), bias_map)],
        out_specs=pl.BlockSpec((tm, d // 2), lambda i, *_: (i, 0)),
        scratch_shapes=[pltpu.VMEM((2, d, ff2), F32), pltpu.VMEM((2, ff2 // 2, d), F32),
                        pltpu.VMEM((d, ff2), BF16), pltpu.VMEM((ff2 // 2, d), BF16),
                        pltpu.SemaphoreType.DMA((2, MOE_W1_PARTS + MOE_W2_PARTS))],
    )
    return pl.pallas_call(
        functools.partial(_moe_body, layer=layer),
        out_shape=jax.ShapeDtypeStruct((n_rows, d // 2), jnp.int32),
        grid_spec=grid_spec,
        compiler_params=_params(("arbitrary",)),
        name="moe",
    )(tile_e, tile_rows, tile_slot, tile_next, xs, w1, b1.reshape(nl, ne, 1, ff2), w2, b2.reshape(nl, ne, 1, d))


def _combine_body(h_ref, y0_ref, y1_ref, y2_ref, y3_ref, gate_ref, g_ref, o_ref, *, final):
    gates = gate_ref[...]
    lo = jnp.zeros(y0_ref.shape, F32)
    hi = jnp.zeros(y0_ref.shape, F32)
    for k, y_ref in enumerate((y0_ref, y1_ref, y2_ref, y3_ref)):
        yl, yh = _unpack_rows(y_ref[...])
        lo = lo + yl * gates[:, k:k + 1]
        hi = hi + yh * gates[:, k:k + 1]
    h = h_ref[...] + jnp.concatenate([lo, hi], axis=-1)
    o_ref[...] = _rms(h, g_ref[...]) if final else h


def _combine(h, y, gates, g, *, final):
    s, d = h.shape
    tm = min(ROW_TILE // 2, s)
    nt = s // tm
    return pl.pallas_call(
        functools.partial(_combine_body, final=final),
        out_shape=jax.ShapeDtypeStruct((s, d), F32),
        grid=(nt,),
        in_specs=[pl.BlockSpec((tm, d), lambda i: (i, 0))]
                 + [pl.BlockSpec((tm, d // 2), lambda i, k=k: (k * nt + i, 0)) for k in range(TOP_K)]
                 + [pl.BlockSpec((tm, TOP_K), lambda i: (i, 0)), pl.BlockSpec((1, d), lambda i: (0, 0))],
        out_specs=pl.BlockSpec((tm, d), lambda i: (i, 0)),
        compiler_params=_params(("parallel",)),
        name="combine",
    )(h, y, y, y, y, gates, g.reshape(1, d))


def _layer(h, mem, biases, p, l, g_final):
    row = lambda a: a.reshape(1, -1).astype(F32)
    q_scale = jnp.concatenate([jnp.full((D_A,), HD_A ** -0.5 * LOG2E, F32), jnp.ones((D_IN - D_A,), F32)])
    w_in = (p["w_in"][l] * q_scale).astype(BF16)
    proj = _inproj(h, p["norm_mix"][l], w_in)
    ya = _dilated_attention(proj, biases)
    ops = _s5_operators(p["s5_a_re"][l], p["s5_a_im"][l], p["s5_b_re"][l], p["s5_b_im"][l],
                        p["s5_c_re"][l], p["s5_c_im"][l], p["s5_log_dt"][l], p["s5_d"][l])
    yb = _s5_core(proj, ops)
    kv = _memkv(mem, p["norm_mem"][l], p["w_xkv"][l].astype(BF16))
    wr = p["w_router"][l].astype(F32)
    wr_hi = wr.astype(BF16)
    wr2 = jnp.pad(jnp.concatenate([wr_hi, (wr - wr_hi.astype(F32)).astype(BF16)], axis=1),
                  ((0, 0), (0, LANES - 2 * N_EXPERTS)))
    br = jnp.pad(p["b_router"][l].astype(F32), (0, LANES - N_EXPERTS)).reshape(1, LANES)
    h2, xn, logits = _mid(
        h, ya, yb, p["w_glu"][l].astype(BF16), row(p["b_glu"][l]), row(p["g_out_attn"][l]),
        row(p["g_out_ssm"][l]), p["w_out"][l].astype(BF16), row(p["norm_xattn"][l]),
        (p["w_xq"][l] * (HD_X ** -0.5)).astype(BF16), kv[:, :D_X], kv[:, D_X:],
        p["w_xo"][l].astype(BF16), row(p["norm_moe"][l]), wr2, br)
    gates, dest, n_rows, tile_e, tile_rows, tile_slot, tile_next = _route(logits, MOE_TILE)
    out = _moe_experts(_scatter_rows(xn, dest, n_rows), tile_e, tile_rows, tile_slot, tile_next,
                       p["w1"], p["b1"], p["w2"], p["b2"], l)
    return _combine(h2, _gather_rows(out, dest.reshape(-1)), gates, g_final, final=l == DEPTH - 1)


def kernel(x, mem, rel_bias, norm_mix, w_in, s5_a_re, s5_a_im, s5_b_re, s5_b_im, s5_c_re, s5_c_im, s5_log_dt, s5_d, w_glu, b_glu, g_out_attn, g_out_ssm, w_out, norm_xattn, norm_mem, w_xq, w_xkv, w_xo, norm_moe, w_router, b_router, w1, b1, w2, b2, norm_final):
    p = dict(norm_mix=norm_mix, w_in=w_in, s5_a_re=s5_a_re, s5_a_im=s5_a_im, s5_b_re=s5_b_re,
             s5_b_im=s5_b_im, s5_c_re=s5_c_re, s5_c_im=s5_c_im, s5_log_dt=s5_log_dt, s5_d=s5_d,
             w_glu=w_glu, b_glu=b_glu, g_out_attn=g_out_attn, g_out_ssm=g_out_ssm, w_out=w_out,
             norm_xattn=norm_xattn, norm_mem=norm_mem, w_xq=w_xq, w_xkv=w_xkv, w_xo=w_xo,
             norm_moe=norm_moe, w_router=w_router, b_router=b_router, w1=w1, b1=b1, w2=w2, b2=b2)
    biases = [_attn_bias(rel_bias, window, dil, perm)
              for (window, dil), perm in zip(WIN_DIL, (_PERM_D1, _PERM_D4, _PERM_D16))]
    outs = []
    for b in range(x.shape[0]):
        h = _to_span_layout(x[b])
        for l in range(DEPTH):
            h = _layer(h, mem[b], biases, p, l, norm_final)
        outs.append(_from_span_layout(h))
    return jnp.stack(outs)
```

```python
import functools
import math

import jax
import jax.numpy as jnp
import numpy as np
from jax import lax
from jax.experimental import pallas as pl
from jax.experimental.pallas import tpu as pltpu
from jax.experimental.pallas import tpu_sc as plsc

F32 = jnp.float32
BF16 = jnp.bfloat16

D_MODEL = 1024
DEPTH = 2
EPS = 1e-5
NEG_INF = -1e30
LOG2E = math.log2(math.e)
H_A = 8
HD_A = 64
D_A = H_A * HD_A
WIN_DIL = ((128, 1), (512, 4), (2048, 16))
BLK = 128
D_B = D_MODEL - D_A
S5_CH = 16
S5_G = D_B // S5_CH
S5_P = 64
D_IN = 3 * D_A + D_B
NUM_BUCKETS = 32
REL_MAX_DIST = 2048
H_X = 4
HD_X = 128
D_X = H_X * HD_X
N_EXPERTS = 32
TOP_K = 4
D_FF = D_MODEL
SWIGLU_ALPHA = 1.702
SWIGLU_LIMIT = 7.0

LANES = 128
SUBLANES = 8
NRES = WIN_DIL[-1][1]
SPAN = NRES * BLK
S5_CHUNK = NRES
S5_PAIRS = S5_G // 2
VMEM_LIMIT = 56 * 1024 * 1024

SC_CORES = 2
SC_SUBCORES = 16
SC_ROWS = 64

ROW_TILE = 512
ATTN_D16_UNROLL = 4
MID_CHAINS = 2
RANK_TILE = 1024
MOE_TILE = 512
MOE_ROW_PATHS = 4
MOE_W1_PARTS = 4
MOE_W2_PARTS = 2


def _params(sem):
    return pltpu.CompilerParams(dimension_semantics=sem, vmem_limit_bytes=VMEM_LIMIT)


def _rms(x, g):
    return x * lax.rsqrt(jnp.mean(x * x, axis=-1, keepdims=True) + EPS) * g


def _dot(a, b):
    return jnp.dot(a, b, preferred_element_type=F32)


def _dot_nt(a, b):
    return lax.dot_general(a, b, (((1,), (1,)), ((), ())), preferred_element_type=F32)


def _full(a):
    return pl.BlockSpec(a.shape, lambda *_: (0,) * a.ndim)


def _pack_rows(x):
    c = x.shape[1] // 2
    lo = lax.bitcast_convert_type(x[:, :c].astype(BF16).astype(F32), jnp.uint32)
    hi = lax.bitcast_convert_type(x[:, c:].astype(BF16).astype(F32), jnp.uint32)
    return lax.bitcast_convert_type(lax.shift_right_logical(lo, jnp.uint32(16)) | hi, jnp.int32)


def _unpack_rows(p):
    u = lax.bitcast_convert_type(p, jnp.uint32)
    lo = lax.bitcast_convert_type(lax.shift_left(u, jnp.uint32(16)), F32)
    hi = lax.bitcast_convert_type(u & jnp.uint32(0xFFFF0000), F32)
    return lo, hi


def _to_span_layout(x):
    s = x.shape[0]
    return x.reshape(s // SPAN, BLK, NRES, -1).transpose(0, 2, 1, 3).reshape(s, -1)


def _from_span_layout(x):
    s = x.shape[0]
    return x.reshape(s // SPAN, NRES, BLK, -1).transpose(0, 2, 1, 3).reshape(s, -1)


def _inproj_body(h_ref, g_ref, w_ref, o_ref):
    xn = _rms(h_ref[...], g_ref[...]).astype(BF16)
    o_ref[...] = _dot(xn, w_ref[...]).astype(BF16)


def _inproj(h, g, w):
    s, d = h.shape
    n = w.shape[1]
    tm = min(ROW_TILE, s)
    return pl.pallas_call(
        _inproj_body,
        out_shape=jax.ShapeDtypeStruct((s, n), BF16),
        grid=(s // tm,),
        in_specs=[pl.BlockSpec((tm, d), lambda i: (i, 0)),
                  pl.BlockSpec((1, d), lambda i: (0, 0)),
                  pl.BlockSpec((d, n), lambda i: (0, 0))],
        out_specs=pl.BlockSpec((tm, n), lambda i: (i, 0)),
        compiler_params=_params(("parallel",)),
        name="inproj",
    )(h, g.reshape(1, d), w)


def _t5_bucket(n):
    max_exact = NUM_BUCKETS // 2
    nf = jnp.maximum(n, 1).astype(F32)
    large = max_exact + (jnp.log(nf / max_exact) / math.log(REL_MAX_DIST / max_exact)
                         * (NUM_BUCKETS - max_exact)).astype(jnp.int32)
    large = jnp.minimum(large, NUM_BUCKETS - 1)
    return jnp.where(n < max_exact, n, large)


def _attn_bias(rel_bias, window, dil, perm):
    steps = window // dil
    perm = jnp.asarray(perm, jnp.int32)
    qi = perm[:, None]
    ki = jnp.concatenate([perm, BLK + perm])[None, :]
    dist = BLK + qi - ki
    in_win = (dist >= 0) & (dist <= steps)
    bucket = _t5_bucket(jnp.clip(dist, 0, steps) * dil)
    onehot = (bucket[:, :, None] == jnp.arange(NUM_BUCKETS, dtype=jnp.int32)).astype(F32)
    bias = jnp.einsum('qkb,bh->hqk', onehot, rel_bias.astype(F32), precision=lax.Precision.HIGHEST)
    bias = jnp.where(in_win[None], bias * LOG2E, NEG_INF)
    return bias.reshape(H_A // 2, 2 * BLK, 2 * BLK)


_PERM_D1 = [NRES * jl + r for r in range(NRES) for jl in range(BLK // NRES)]
_PERM_D4 = [4 * jl + i for i in range(4) for jl in range(BLK // 4)]
_PERM_D16 = list(range(BLK))


def _attn_body(q_ref, k_ref, v_ref, kp_ref, vp_ref, b1_ref, b4_ref, b16_ref, o_ref, acc, mst, lst):
    has_prev = pl.program_id(0) > 0
    lane = lax.broadcasted_iota(jnp.int32, (1, LANES), 1)
    lo = lane < HD_A
    mlo = lo.astype(BF16)
    mhi = (~lo).astype(BF16)
    col = lax.broadcasted_iota(jnp.int32, (2 * BLK, 2 * BLK), 1)
    ones = jnp.ones((2 * BLK, LANES), BF16)

    def tile(q2, kk, vv, bias, mask_prev):
        qs = jnp.concatenate([q2 * mlo, q2 * mhi], axis=0)
        s = _dot_nt(qs, kk) + bias
        if mask_prev:
            s = jnp.where(jnp.logical_or(has_prev, col >= BLK), s, NEG_INF)
        m = jnp.max(s, axis=-1, keepdims=True)
        e = jnp.exp2((s - m).astype(BF16))
        oa = _dot(e, jnp.concatenate([vv, ones], axis=1))
        o = oa[:, :LANES]
        l = oa[:, LANES:]
        return (jnp.where(lo, m[:BLK], m[BLK:]), jnp.where(lo, l[:BLK], l[BLK:]),
                jnp.where(lo, o[:BLK], o[BLK:]))

    def merge(prev, cur):
        mp, lp, ap = prev
        mc, lc, ac = cur
        mn = jnp.maximum(mp, mc)
        a = jnp.exp2(mp - mn)
        b = jnp.exp2(mc - mn)
        return mn, a * lp + b * lc, a * ap + b * ac

    def cat(xs):
        return jnp.concatenate(xs, axis=0)

    def d16_body(i, _):
        for sub in range(ATTN_D16_UNROLL):
            rows = pl.ds(pl.multiple_of((i * ATTN_D16_UNROLL + sub) * BLK, BLK), BLK)
            for hp in range(H_A // 2):
                lanes = slice(hp * LANES, (hp + 1) * LANES)
                kk = cat([kp_ref[rows, lanes], k_ref[rows, lanes]])
                vv = cat([vp_ref[rows, lanes], v_ref[rows, lanes]])
                m2, l2, o2 = tile(q_ref[rows, lanes], kk, vv, b16_ref[hp], True)
                mst[rows, lanes] = m2
                lst[rows, lanes] = l2
                acc[rows, lanes] = o2
        return 0

    lax.fori_loop(0, NRES // ATTN_D16_UNROLL, d16_body, 0)

    def d4_body(r4, _):
        for b in range(4):
            def chunk_rows(bb):
                return [pl.ds(pl.multiple_of(4 * BLK * i + BLK * r4 + 32 * bb, 32), 32) for i in range(4)]
            rows = chunk_rows(b)
            prows = chunk_rows(3 if b == 0 else b - 1)
            kprev, vprev = (kp_ref, vp_ref) if b == 0 else (k_ref, v_ref)
            for hp in range(H_A // 2):
                lanes = slice(hp * LANES, (hp + 1) * LANES)
                q2 = cat([q_ref[rr, lanes] for rr in rows])
                kk = cat([kprev[rr, lanes] for rr in prows] + [k_ref[rr, lanes] for rr in rows])
                vv = cat([vprev[rr, lanes] for rr in prows] + [v_ref[rr, lanes] for rr in rows])
                cur = tile(q2, kk, vv, b4_ref[hp], b == 0)
                prev = (cat([mst[rr, lanes] for rr in rows]), cat([lst[rr, lanes] for rr in rows]),
                        cat([acc[rr, lanes] for rr in rows]))
                mn, ln, an = merge(prev, cur)
                for i, rr in enumerate(rows):
                    part = slice(32 * i, 32 * (i + 1))
                    mst[rr, lanes] = mn[part]
                    lst[rr, lanes] = ln[part]
                    acc[rr, lanes] = an[part]
        return 0

    lax.fori_loop(0, 4, d4_body, 0)

    def d1_pair(ap, kprev, vprev, prev_ap, mask_prev):
        def tiles(a_):
            return [pl.ds(pl.multiple_of(BLK * r + 16 * a_, 16), 16) for r in range(NRES)]
        cur_t = tiles(ap)
        prev_t = tiles(prev_ap)

        def halves(ref, ts, lanes):
            xs = [ref[t, lanes].astype(F32) for t in ts]
            return cat([x[:8] for x in xs]).astype(BF16), cat([x[8:] for x in xs]).astype(BF16)

        for hp in range(H_A // 2):
            lanes = slice(hp * LANES, (hp + 1) * LANES)
            q_e, q_o = halves(q_ref, cur_t, lanes)
            k_e, k_o = halves(k_ref, cur_t, lanes)
            v_e, v_o = halves(v_ref, cur_t, lanes)
            _, k_p = halves(kprev, prev_t, lanes)
            _, v_p = halves(vprev, prev_t, lanes)
            cur_e = tile(q_e, cat([k_p, k_e]), cat([v_p, v_e]), b1_ref[hp], mask_prev)
            cur_o = tile(q_o, cat([k_e, k_o]), cat([v_e, v_o]), b1_ref[hp], False)
            ms = [mst[t, lanes] for t in cur_t]
            ls = [lst[t, lanes] for t in cur_t]
            ac = [acc[t, lanes] for t in cur_t]
            outs = []
            for half, cur in ((0, cur_e), (1, cur_o)):
                part = slice(8 * half, 8 * half + 8)
                prev = (cat([x[part] for x in ms]), cat([x[part] for x in ls]), cat([x[part] for x in ac]))
                _, ln, an = merge(prev, cur)
                outs.append(an / ln)
            for r, t in enumerate(cur_t):
                part = slice(8 * r, 8 * r + 8)
                o_ref[t, lanes] = cat([outs[0][part], outs[1][part]]).astype(o_ref.dtype)

    d1_pair(0, kp_ref, vp_ref, BLK // 16 - 1, True)

    def d1_body(ap, _):
        d1_pair(ap, k_ref, v_ref, ap - 1, False)
        return 0

    lax.fori_loop(1, BLK // 16, d1_body, 0)


def _dilated_attention(proj, biases):
    s = proj.shape[0]
    cur = lambda which: pl.BlockSpec((SPAN, D_A), lambda c: (c, which))
    prev = lambda which: pl.BlockSpec((SPAN, D_A), lambda c: (jnp.maximum(c - 1, 0), which))
    return pl.pallas_call(
        _attn_body,
        out_shape=jax.ShapeDtypeStruct((s, D_A), BF16),
        grid=(s // SPAN,),
        in_specs=[cur(0), cur(1), cur(2), prev(1), prev(2)] + [_full(b) for b in biases],
        out_specs=pl.BlockSpec((SPAN, D_A), lambda c: (c, 0)),
        scratch_shapes=[pltpu.VMEM((SPAN, D_A), F32)] * 3,
        compiler_params=_params(("arbitrary",)),
        name="attn",
    )(proj, proj, proj, proj, proj, *biases)


def _s5_operators(a_re, a_im, b_re, b_im, c_re, c_im, log_dt, d_skip):
    L = S5_CHUNK
    lam = lax.complex(a_re.astype(F32), a_im.astype(F32))
    dt = jnp.exp(log_dt.astype(F32))[:, None]
    a_bar = jnp.exp(lam * dt)
    b_bar = ((a_bar - 1.0) / lam)[..., None] * lax.complex(b_re.astype(F32), b_im.astype(F32))
    c = lax.complex(c_re.astype(F32), c_im.astype(F32))
    j = jnp.arange(L + 1, dtype=F32)
    log_a = lam * dt
    apow = jnp.exp(log_a[None] * j[:, None, None])
    kt = jnp.einsum('gdp,jgp,gpc->gcjd', c, apow[:L], b_bar).real
    skip = d_skip.astype(F32).reshape(S5_G, S5_CH, 1, 1) * jnp.eye(S5_CH)[None, :, None, :]
    kt = (kt + skip * (jnp.arange(L) == 0)[None, None, :, None]).reshape(S5_G, S5_CH, L * S5_CH)
    p = jnp.einsum('sgp,gpc->gscp', apow[:L][::-1], b_bar).reshape(S5_G, L * S5_CH, S5_P)
    ca = jnp.einsum('gdp,tgp->gptd', c, apow[1:L + 1]).reshape(S5_G, S5_P, L * S5_CH)
    a_l = apow[L]

    def pair_blocks(x):
        g, r, w = x.shape
        x = x.reshape(S5_PAIRS, 2, r, w)
        z = jnp.zeros_like(x[:, 0])
        top = jnp.concatenate([x[:, 0], z], axis=-1)
        bot = jnp.concatenate([z, x[:, 1]], axis=-1)
        return jnp.concatenate([top, bot], axis=1)

    p2 = jnp.concatenate([pair_blocks(p.real), pair_blocks(p.imag)], axis=-1)
    q2 = jnp.concatenate([pair_blocks(ca.real), pair_blocks(-ca.imag)], axis=1)
    a_lr = a_l.real.reshape(1, S5_G * S5_P)
    a_li = a_l.imag.reshape(1, S5_G * S5_P)
    return kt, p2.astype(BF16), q2.astype(BF16), a_lr, a_li


def _s5_body(u_ref, kt_ref, p_ref, q_ref, ar_ref, ai_ref, o_ref, m_ref, w_ref, y_ref,
             ere, eim, xre, xim, sre, sim):
    rows = BLK
    gw = S5_CHUNK * S5_CH
    pw = 2 * gw
    per_vreg = LANES // S5_CH
    chunk_of_lane = lax.broadcasted_iota(jnp.int32, (rows, LANES), 1) // S5_CH

    def chunk_transpose(arrs):
        arrs = list(arrs)
        for s in (4, 2, 1):
            upper = (chunk_of_lane & s) != 0
            nxt = list(arrs)
            for i in range(per_vreg):
                if i & s:
                    continue
                lo_a, hi_a = arrs[i], arrs[i + s]
                nxt[i] = jnp.where(upper, pltpu.roll(hi_a, s * S5_CH, axis=1), lo_a)
                nxt[i + s] = jnp.where(upper, hi_a, pltpu.roll(lo_a, LANES - s * S5_CH, axis=1))
            arrs = nxt
        return arrs

    @pl.when(pl.program_id(0) == 0)
    def _():
        sre[...] = jnp.zeros_like(sre)
        sim[...] = jnp.zeros_like(sim)
        lane = lax.broadcasted_iota(jnp.int32, (S5_CH, gw), 1)

        def build(g, _):
            kt = kt_ref[g]
            for s in range(S5_CHUNK):
                blk = kt if s == 0 else jnp.where(lane >= s * S5_CH, pltpu.roll(kt, s * S5_CH, axis=1), 0.0)
                m_ref[g, s * S5_CH:(s + 1) * S5_CH, :] = blk.astype(BF16)
            return 0

        lax.fori_loop(0, S5_G, build, 0)

    for b in range(D_B // LANES):
        for a in range(S5_CHUNK // per_vreg):
            srcs = [u_ref[(per_vreg * a + i) * rows:(per_vreg * a + i + 1) * rows,
                          b * LANES:(b + 1) * LANES].astype(F32) for i in range(per_vreg)]
            for gi, arr in enumerate(chunk_transpose(srcs)):
                g = per_vreg * b + gi
                w_ref[:, g * gw + a * LANES:g * gw + (a + 1) * LANES] = arr.astype(BF16)

    for pr in range(S5_PAIRS):
        e = _dot(w_ref[:, pr * pw:(pr + 1) * pw], p_ref[pr])
        ere[:, pr * LANES:(pr + 1) * LANES] = e[:, :LANES]
        eim[:, pr * LANES:(pr + 1) * LANES] = e[:, LANES:]

    ar = ar_ref[...]
    ai = ai_ref[...]

    def step(n, carry):
        xr, xi = carry
        xre[pl.ds(n, 1), :] = xr
        xim[pl.ds(n, 1), :] = xi
        nr = ar * xr - ai * xi + ere[pl.ds(n, 1), :]
        ni = ar * xi + ai * xr + eim[pl.ds(n, 1), :]
        return nr, ni

    xr, xi = lax.fori_loop(0, rows, step, (sre[...], sim[...]))
    sre[...] = xr
    sim[...] = xi

    for pr in range(S5_PAIRS):
        xin = jnp.concatenate([xre[:, pr * LANES:(pr + 1) * LANES],
                               xim[:, pr * LANES:(pr + 1) * LANES]], axis=-1).astype(BF16)
        yc = _dot(xin, q_ref[pr])
        for half in range(2):
            g = 2 * pr + half
            cols = slice(g * gw, (g + 1) * gw)
            y = _dot(w_ref[:, cols], m_ref[g]) + yc[:, half * gw:(half + 1) * gw]
            y_ref[:, cols] = 0.5 * y * (1.0 + lax.erf(y * (2.0 ** -0.5)))

    for b in range(D_B // LANES):
        for a in range(S5_CHUNK // per_vreg):
            srcs = [y_ref[:, (per_vreg * b + i) * gw + a * LANES:(per_vreg * b + i) * gw + (a + 1) * LANES]
                    for i in range(per_vreg)]
            for ri, arr in enumerate(chunk_transpose(srcs)):
                r = per_vreg * a + ri
                o_ref[r * rows:(r + 1) * rows, b * LANES:(b + 1) * LANES] = arr.astype(BF16)


def _s5_core(proj, ops):
    kt, p2, q2, a_lr, a_li = ops
    s = proj.shape[0]
    gw = S5_CHUNK * S5_CH
    wide = S5_G * gw
    nstate = S5_G * S5_P
    return pl.pallas_call(
        _s5_body,
        out_shape=jax.ShapeDtypeStruct((s, D_B), BF16),
        grid=(s // SPAN,),
        in_specs=[pl.BlockSpec((SPAN, D_B), lambda i: (i, 3 * D_A // D_B)),
                  _full(kt), _full(p2), _full(q2), _full(a_lr), _full(a_li)],
        out_specs=pl.BlockSpec((SPAN, D_B), lambda i: (i, 0)),
        scratch_shapes=[pltpu.VMEM((S5_G, gw, gw), BF16), pltpu.VMEM((BLK, wide), BF16),
                        pltpu.VMEM((BLK, wide), F32)]
                       + [pltpu.VMEM((BLK, nstate), F32)] * 4 + [pltpu.VMEM((1, nstate), F32)] * 2,
        compiler_params=_params(("arbitrary",)),
        name="s5",
    )(proj, kt, p2, q2, a_lr, a_li)


def _split_bf16(x):
    hi = x.astype(BF16)
    lo = (x - hi.astype(F32)).astype(BF16)
    return hi, lo


def _mid_body(h_ref, ya_ref, yb_ref, wglu_ref, bglu_ref, ga_ref, gb_ref, wout_ref, gx_ref, wq_ref,
              k_ref, v_ref, wo_ref, gm_ref, wr_ref, br_ref, h_out, xn_out, logit_out):
    part = h_ref.shape[0] // MID_CHAINS
    for c in range(MID_CHAINS):
        rows = slice(c * part, (c + 1) * part)
        outs = _mid_rows(h_ref[rows, :], ya_ref[rows, :], yb_ref[rows, :], wglu_ref, bglu_ref, ga_ref, gb_ref,
                         wout_ref, gx_ref, wq_ref, k_ref, v_ref, wo_ref, gm_ref, wr_ref, br_ref)
        for ref, val in zip((h_out, xn_out, logit_out), outs):
            ref[rows, :] = val


def _mid_rows(h, ya, yb, wglu_ref, bglu_ref, ga_ref, gb_ref, wout_ref, gx_ref, wq_ref,
              k_ref, v_ref, wo_ref, gm_ref, wr_ref, br_ref):
    gate = jax.nn.sigmoid(_dot(yb, wglu_ref[...]) + bglu_ref[...])
    yb2 = yb.astype(F32) * gate
    na = _rms(ya.astype(F32), ga_ref[...]).astype(BF16)
    nb = _rms(yb2, gb_ref[...]).astype(BF16)
    h1 = h + _dot(na, wout_ref[0:D_A, :]) + _dot(nb, wout_ref[D_A:D_MODEL, :])
    q = _dot(_rms(h1, gx_ref[...]).astype(BF16), wq_ref[...]).astype(BF16)
    heads = []
    for hd in range(H_X):
        lanes = slice(hd * HD_X, (hd + 1) * HD_X)
        s = _dot_nt(q[:, lanes], k_ref[:, lanes])
        e = jnp.exp(s - jnp.max(s, axis=-1, keepdims=True))
        heads.append(_dot(e.astype(BF16), v_ref[:, lanes]) / jnp.sum(e, axis=-1, keepdims=True))
    o = jnp.concatenate(heads, axis=-1).astype(BF16)
    h2 = h1 + _dot(o, wo_ref[...])
    xn = _rms(h2, gm_ref[...])
    x_hi, x_lo = _split_bf16(xn)
    part = _dot(x_hi, wr_ref[...]) + _dot(x_lo, wr_ref[...])
    logits = part + pltpu.roll(part, LANES - N_EXPERTS, axis=1) + br_ref[...]
    return h2, _pack_rows(xn), logits


def _mid(h, ya, yb, wglu, bglu, ga, gb, wout, gx, wq, kmem, vmem, wo, gm, wr2, br):
    s = h.shape[0]
    tm = min(ROW_TILE, s)
    row = lambda w: pl.BlockSpec((tm, w), lambda i: (i, 0))
    consts = [wglu, bglu, ga, gb, wout, gx, wq, kmem, vmem, wo, gm, wr2, br]
    return pl.pallas_call(
        _mid_body,
        out_shape=[jax.ShapeDtypeStruct((s, D_MODEL), F32), jax.ShapeDtypeStruct((s, D_MODEL // 2), jnp.int32),
                   jax.ShapeDtypeStruct((s, LANES), F32)],
        grid=(s // tm,),
        in_specs=[row(D_MODEL), row(D_A), row(D_B)] + [_full(a) for a in consts],
        out_specs=[row(D_MODEL), row(D_MODEL // 2), row(LANES)],
        compiler_params=_params(("parallel",)),
        name="mid",
    )(h, ya, yb, *consts)


def _memkv_body(mem_ref, g_ref, w_ref, o_ref):
    o_ref[...] = _dot(_rms(mem_ref[...], g_ref[...]).astype(BF16), w_ref[...]).astype(BF16)


def _memkv(mem, g, w):
    n, d = mem.shape
    return pl.pallas_call(
        _memkv_body,
        out_shape=jax.ShapeDtypeStruct((n, w.shape[1]), BF16),
        compiler_params=pltpu.CompilerParams(vmem_limit_bytes=VMEM_LIMIT),
        name="memkv",
    )(mem, g.reshape(1, d), w)


def _rank_body(logit_ref, tri_ref, idx_ref, gate_ref, rank_ref, cnt_ref, carry):
    @pl.when(pl.program_id(0) == 0)
    def _():
        carry[...] = jnp.zeros_like(carry)

    tm = logit_ref.shape[0]
    logits = jnp.transpose(logit_ref[...])[:N_EXPERTS, :]
    expert = lax.broadcasted_iota(jnp.int32, (N_EXPERTS, tm), 0)
    vals, idxs = [], []
    for _ in range(TOP_K):
        mx = jnp.max(logits, axis=0, keepdims=True)
        ix = jnp.min(jnp.where(logits == mx, expert, N_EXPERTS), axis=0, keepdims=True)
        vals.append(mx)
        idxs.append(ix)
        logits = jnp.where(expert == ix, -jnp.inf, logits)
    es = [jnp.exp(v - vals[0]) for v in vals]
    den = es[0] + es[1] + es[2] + es[3]
    hits = [expert == ix for ix in idxs]
    onehot = jnp.zeros((N_EXPERTS, tm), F32)
    for hit in hits:
        onehot = onehot + jnp.where(hit, 1.0, 0.0)
    nb = tm // LANES
    blocks = jnp.concatenate([onehot[:, b * LANES:(b + 1) * LANES] for b in range(nb)], axis=0)
    inc = _dot(blocks.astype(BF16), tri_ref[...])
    run = carry[...][:, 0:1]
    before = []
    for b in range(nb):
        inc_b = inc[b * N_EXPERTS:(b + 1) * N_EXPERTS, :]
        before.append(run + inc_b - onehot[:, b * LANES:(b + 1) * LANES])
        run = run + inc_b[:, LANES - 1:LANES]
    before = jnp.concatenate(before, axis=1)
    choice = lax.broadcasted_iota(jnp.int32, (SUBLANES, tm), 0)
    idx_t = jnp.full((SUBLANES, tm), N_EXPERTS, jnp.int32)
    gate_t = jnp.zeros((SUBLANES, tm), F32)
    rank_t = jnp.zeros((SUBLANES, tm), jnp.int32)
    for k in range(TOP_K):
        rk = jnp.sum(jnp.where(hits[k], before, 0.0), axis=0, keepdims=True)
        idx_t = jnp.where(choice == k, idxs[k], idx_t)
        gate_t = jnp.where(choice == k, es[k] / den, gate_t)
        rank_t = jnp.where(choice == k, rk.astype(jnp.int32), rank_t)
    idx_ref[...] = idx_t
    gate_ref[...] = gate_t
    rank_ref[...] = rank_t
    total = jnp.broadcast_to(run, carry.shape)
    carry[...] = total
    cnt_ref[...] = total.astype(jnp.int32)


def _rank(logits):
    t = logits.shape[0]
    tm = min(RANK_TILE, t)
    cols = pl.BlockSpec((SUBLANES, tm), lambda i: (0, i))
    tri = jnp.asarray(np.triu(np.ones((LANES, LANES), np.float32)), BF16)
    return pl.pallas_call(
        _rank_body,
        out_shape=[jax.ShapeDtypeStruct((SUBLANES, t), jnp.int32), jax.ShapeDtypeStruct((SUBLANES, t), F32),
                   jax.ShapeDtypeStruct((SUBLANES, t), jnp.int32),
                   jax.ShapeDtypeStruct((N_EXPERTS, LANES), jnp.int32)],
        grid=(t // tm,),
        in_specs=[pl.BlockSpec((tm, LANES), lambda i: (i, 0)), _full(tri)],
        out_specs=[cols, cols, cols, pl.BlockSpec((N_EXPERTS, LANES), lambda i: (0, 0))],
        scratch_shapes=[pltpu.VMEM((N_EXPERTS, LANES), F32)],
        compiler_params=_params(("arbitrary",)),
        name="rank",
    )(logits, tri)


def _route(logits, tm):
    t = logits.shape[0]
    tk = t * TOP_K
    idx, gates, rank, cnt = _rank(logits)
    counts = cnt[:, 0]
    padded = (counts + tm - 1) // tm * tm
    pend = jnp.cumsum(padded)
    pstart = pend - padded
    n_rows = tk + N_EXPERTS * tm
    n_tiles = n_rows // tm
    experts = jnp.arange(N_EXPERTS, dtype=jnp.int32)
    tile_first = jnp.arange(n_tiles, dtype=jnp.int32) * tm
    last_used = jnp.max(jnp.where(padded > 0, experts, 0))
    tile_e = jnp.minimum(jnp.sum(tile_first[:, None] >= pend[None, :], axis=1), last_used).astype(jnp.int32)
    tile_rows = jnp.clip(jnp.sum(jnp.where(tile_e[:, None] == experts[None, :],
                                           (pstart + counts)[None, :], 0), axis=1) - tile_first, 0, tm)
    tile_rows = jnp.where(tile_first < pend[-1], tile_rows, 0).astype(jnp.int32)
    group = jnp.cumsum(jnp.concatenate([jnp.zeros((1,), jnp.int32),
                                        (tile_e[1:] != tile_e[:-1]).astype(jnp.int32)]))
    tile_slot = (group % 2).astype(jnp.int32)
    later = (experts[None, :] > experts[:, None]) & (padded > 0)[None, :]
    next_e = jnp.min(jnp.where(later, experts[None, :], N_EXPERTS), axis=1)
    next_e = jnp.where(next_e < N_EXPERTS, next_e, -1).astype(jnp.int32)
    tile_next = jnp.sum(jnp.where(tile_e[:, None] == experts[None, :], next_e[None, :], 0), axis=1).astype(jnp.int32)
    base = jnp.sum(jnp.where(idx[:TOP_K, :, None] == experts, pstart, 0), axis=-1)
    dest = rank[:TOP_K] + base
    return gates[:TOP_K].T, dest, n_rows, tile_e, tile_rows, tile_slot, tile_next


def _sc_mesh():
    return plsc.VectorSubcoreMesh(core_axis_name="c", subcore_axis_name="s",
                                  num_cores=SC_CORES, num_subcores=SC_SUBCORES)


def _sc_worker():
    return lax.axis_index("s") * SC_CORES + lax.axis_index("c")


def _scatter_rows(x, dest, n_rows):
    t, d = x.shape
    per_worker = t // (SC_CORES * SC_SUBCORES)
    chunks = per_worker // SC_ROWS

    @functools.partial(
        pl.kernel, mesh=_sc_mesh(),
        out_type=jax.ShapeDtypeStruct((n_rows, d), x.dtype),
        scratch_types=[pltpu.VMEM((SC_ROWS, d), x.dtype)] + [pltpu.VMEM((SC_ROWS,), jnp.int32)] * TOP_K
                      + [pltpu.SemaphoreType.DMA((TOP_K,))],
    )
    def scatter(x_hbm, *rest):
        dest_hbm, out_hbm, rows_v = rest[:TOP_K], rest[TOP_K], rest[TOP_K + 1]
        idx_v, sem = rest[TOP_K + 2:2 * TOP_K + 2], rest[2 * TOP_K + 2]
        base = _sc_worker() * per_worker

        @pl.loop(0, chunks)
        def _(c):
            off = pl.multiple_of(base + c * SC_ROWS, SC_ROWS)
            pltpu.sync_copy(x_hbm.at[pl.ds(off, SC_ROWS)], rows_v)
            for k in range(TOP_K):
                pltpu.sync_copy(dest_hbm[k].at[pl.ds(off, SC_ROWS)], idx_v[k])
            copies = [pltpu.async_copy(rows_v, out_hbm.at[idx_v[k]], sem.at[k]) for k in range(TOP_K)]
            for cp in copies:
                cp.wait()

    return scatter(x, *[dest[k] for k in range(TOP_K)])


def _gather_rows(table, idx):
    n, d = table.shape
    b = idx.shape[0]
    per_worker = b // (SC_CORES * SC_SUBCORES)
    chunks = per_worker // SC_ROWS
    assert chunks % 2 == 0

    @functools.partial(
        pl.kernel, mesh=_sc_mesh(),
        out_type=jax.ShapeDtypeStruct((b, d), table.dtype),
        scratch_types=[pltpu.VMEM((chunks, SC_ROWS), jnp.int32)] + [pltpu.VMEM((SC_ROWS, d), table.dtype)] * 2
                      + [pltpu.SemaphoreType.DMA((2,)), pltpu.SemaphoreType.DMA((2,))],
    )
    def gather(table_hbm, idx_hbm, out_hbm, idx_v, rows0, rows1, gsem, wsem):
        rows_v = (rows0, rows1)
        worker = _sc_worker()
        base = worker * per_worker
        pltpu.sync_copy(idx_hbm.at[pl.ds(pl.multiple_of(worker * chunks, chunks), chunks)], idx_v)

        def rows_at(c):
            return pl.ds(pl.multiple_of(base + c * SC_ROWS, SC_ROWS), SC_ROWS)

        def fetched(c, s):
            return pltpu.make_async_copy(table_hbm.at[idx_v.at[c]], rows_v[s], gsem.at[s])

        def written(c, s):
            return pltpu.make_async_copy(rows_v[s], out_hbm.at[rows_at(c)], wsem.at[s])

        fetched(0, 0).start()

        @pl.loop(0, chunks, step=2)
        def _(c0):
            for s in range(2):
                c = c0 + s
                fetched(c, s).wait()
                written(c, s).start()

                @pl.when(c >= 1)
                def _():
                    written(c - 1, 1 - s).wait()

                @pl.when(c + 1 < chunks)
                def _():
                    fetched(c + 1, 1 - s).start()

        written(chunks - 1, 1).wait()

    return gather(table, idx.reshape(b // SC_ROWS, SC_ROWS))


def _moe_body(te_ref, tv_ref, sl_ref, nx_ref, x_ref, w1_hbm, b1_ref, w2_hbm, b2_ref, o_ref,
              w1f, w2f, w1b, w2b, sem, *, layer):
    i = pl.program_id(0)
    e = te_ref[i]
    slot = sl_ref[i]
    new_expert = (i == 0) | (e != te_ref[jnp.maximum(i - 1, 0)])

    def weight_copies(expert, s):
        rows1 = w1f.shape[1] // MOE_W1_PARTS
        rows2 = w2f.shape[1] // MOE_W2_PARTS
        c1 = [pltpu.make_async_copy(w1_hbm.at[layer, expert, pl.ds(q * rows1, rows1)],
                                    w1f.at[s, pl.ds(q * rows1, rows1)], sem.at[s, q])
              for q in range(MOE_W1_PARTS)]
        c2 = [pltpu.make_async_copy(w2_hbm.at[layer, expert, pl.ds(q * rows2, rows2)],
                                    w2f.at[s, pl.ds(q * rows2, rows2)], sem.at[s, MOE_W1_PARTS + q])
              for q in range(MOE_W2_PARTS)]
        return c1 + c2

    @pl.when(i == 0)
    def _():
        for c in weight_copies(e, slot):
            c.start()

    @pl.when(new_expert)
    def _():
        for c in weight_copies(e, slot):
            c.wait()
        nxt = nx_ref[i]

        @pl.when(nxt >= 0)
        def _():
            for c in weight_copies(nxt, 1 - slot):
                c.start()

        w1b[...] = w1f[slot].astype(BF16)
        w2b[...] = w2f[slot].astype(BF16)

    def expert(rows):
        row = lax.broadcasted_iota(jnp.int32, (rows, x_ref.shape[1]), 0)
        lo, hi = _unpack_rows(jnp.where(row < tv_ref[i], x_ref[0:rows, :], 0))
        x = jnp.concatenate([lo, hi], axis=-1).astype(BF16)
        hb = _dot(x, w1b[...]) + b1_ref[0]
        x_glu = jnp.minimum(hb[:, :D_FF], SWIGLU_LIMIT)
        x_lin = jnp.clip(hb[:, D_FF:], -SWIGLU_LIMIT, SWIGLU_LIMIT)
        act = x_glu * jax.nn.sigmoid(SWIGLU_ALPHA * x_glu) * (x_lin + 1.0)
        o_ref[0:rows, :] = _pack_rows(_dot(act.astype(BF16), w2b[...]) + b2_ref[0])

    step = x_ref.shape[0] // MOE_ROW_PATHS
    for path in range(1, MOE_ROW_PATHS + 1):
        rows = path * step

        @pl.when((tv_ref[i] > rows - step) & (tv_ref[i] <= rows))
        def _(rows=rows):
            expert(rows)
            if rows < x_ref.shape[0]:
                o_ref[rows:, :] = jnp.zeros((x_ref.shape[0] - rows, o_ref.shape[1]), o_ref.dtype)

    @pl.when(tv_ref[i] == 0)
    def _():
        o_ref[...] = jnp.zeros_like(o_ref)


def _moe_experts(xs, tile_e, tile_rows, tile_slot, tile_next, w1, b1, w2, b2, layer):
    n_rows = xs.shape[0]
    tm = MOE_TILE
    nl, ne, d, ff2 = w1.shape
    bias_map = lambda i, te, tv, sl, nx: (layer, te[i], 0, 0)
    grid_spec = pltpu.PrefetchScalarGridSpec(
        num_scalar_prefetch=4,
        grid=(n_rows // tm,),
        in_specs=[pl.BlockSpec((tm, d // 2), lambda i, *_: (i, 0)),
                  pl.BlockSpec(memory_space=pl.ANY),
                  pl.BlockSpec((None, 1, 1, ff2), bias_map),
                  pl.BlockSpec(memory_space=pl.ANY),
                  pl.BlockSpec((None, 1, 1, d), bias_map)],
        out_specs=pl.BlockSpec((tm, d // 2), lambda i, *_: (i, 0)),
        scratch_shapes=[pltpu.VMEM((2, d, ff2), F32), pltpu.VMEM((2, ff2 // 2, d), F32),
                        pltpu.VMEM((d, ff2), BF16), pltpu.VMEM((ff2 // 2, d), BF16),
                        pltpu.SemaphoreType.DMA((2, MOE_W1_PARTS + MOE_W2_PARTS))],
    )
    return pl.pallas_call(
        functools.partial(_moe_body, layer=layer),
        out_shape=jax.ShapeDtypeStruct((n_rows, d // 2), jnp.int32),
        grid_spec=grid_spec,
        compiler_params=_params(("arbitrary",)),
        name="moe",
    )(tile_e, tile_rows, tile_slot, tile_next, xs, w1, b1.reshape(nl, ne, 1, ff2), w2, b2.reshape(nl, ne, 1, d))


def _combined(h_ref, y_refs, gate_ref):
    gates = gate_ref[...]
    lo = jnp.zeros(y_refs[0].shape, F32)
    hi = jnp.zeros(y_refs[0].shape, F32)
    for k, y_ref in enumerate(y_refs):
        yl, yh = _unpack_rows(y_ref[...])
        lo = lo + yl * gates[:, k:k + 1]
        hi = hi + yh * gates[:, k:k + 1]
    return h_ref[...] + jnp.concatenate([lo, hi], axis=-1)


def _combine_body(h_ref, y0_ref, y1_ref, y2_ref, y3_ref, gate_ref, o_ref):
    o_ref[...] = _combined(h_ref, (y0_ref, y1_ref, y2_ref, y3_ref), gate_ref)


def _combine_final_body(h_ref, y0_ref, y1_ref, y2_ref, y3_ref, gate_ref, g_ref, out_hbm, buf, sem):
    i = pl.program_id(0)
    n = pl.num_programs(0)
    slot = i % 2
    per_step = h_ref.shape[0] // BLK
    steps_per_span = NRES // per_step

    def writes(step, s):
        span = step // steps_per_span
        r0 = (step % steps_per_span) * per_step
        return [pltpu.make_async_copy(buf.at[s, pl.ds(rr * BLK, BLK), :], out_hbm.at[span, :, r0 + rr, :],
                                      sem.at[s, rr]) for rr in range(per_step)]

    @pl.when(i >= 2)
    def _():
        for cp in writes(i - 2, slot):
            cp.wait()

    buf[slot] = _rms(_combined(h_ref, (y0_ref, y1_ref, y2_ref, y3_ref), gate_ref), g_ref[...])
    for cp in writes(i, slot):
        cp.start()

    @pl.when(i == n - 1)
    def _():
        for cp in writes(i, slot):
            cp.wait()

        @pl.when(i >= 1)
        def _():
            for cp in writes(i - 1, 1 - slot):
                cp.wait()


def _combine(h, y, gates, g, *, final):
    s, d = h.shape
    tm = min(ROW_TILE // 2, s)
    nt = s // tm
    in_specs = ([pl.BlockSpec((tm, d), lambda i: (i, 0))]
                + [pl.BlockSpec((tm, d // 2), lambda i, k=k: (k * nt + i, 0)) for k in range(TOP_K)]
                + [pl.BlockSpec((tm, TOP_K), lambda i: (i, 0))])
    if not final:
        return pl.pallas_call(
            _combine_body,
            out_shape=jax.ShapeDtypeStruct((s, d), F32),
            grid=(nt,),
            in_specs=in_specs,
            out_specs=pl.BlockSpec((tm, d), lambda i: (i, 0)),
            compiler_params=_params(("parallel",)),
            name="combine",
        )(h, y, y, y, y, gates)
    out = pl.pallas_call(
        _combine_final_body,
        out_shape=jax.ShapeDtypeStruct((s // SPAN, BLK, NRES, d), F32),
        grid=(nt,),
        in_specs=in_specs + [pl.BlockSpec((1, d), lambda i: (0, 0))],
        out_specs=pl.BlockSpec(memory_space=pl.ANY),
        scratch_shapes=[pltpu.VMEM((2, tm, d), F32), pltpu.SemaphoreType.DMA((2, tm // BLK))],
        compiler_params=_params(("arbitrary",)),
        name="combine_final",
    )(h, y, y, y, y, gates, g.reshape(1, d))
    return out.reshape(s, d)


def _layer(h, mem, biases, p, l, g_final):
    row = lambda a: a.reshape(1, -1).astype(F32)
    q_scale = jnp.concatenate([jnp.full((D_A,), HD_A ** -0.5 * LOG2E, F32), jnp.ones((D_IN - D_A,), F32)])
    w_in = (p["w_in"][l] * q_scale).astype(BF16)
    proj = _inproj(h, p["norm_mix"][l], w_in)
    ya = _dilated_attention(proj, biases)
    ops = _s5_operators(p["s5_a_re"][l], p["s5_a_im"][l], p["s5_b_re"][l], p["s5_b_im"][l],
                        p["s5_c_re"][l], p["s5_c_im"][l], p["s5_log_dt"][l], p["s5_d"][l])
    yb = _s5_core(proj, ops)
    kv = _memkv(mem, p["norm_mem"][l], p["w_xkv"][l].astype(BF16))
    wr = p["w_router"][l].astype(F32)
    wr_hi = wr.astype(BF16)
    wr2 = jnp.pad(jnp.concatenate([wr_hi, (wr - wr_hi.astype(F32)).astype(BF16)], axis=1),
                  ((0, 0), (0, LANES - 2 * N_EXPERTS)))
    br = jnp.pad(p["b_router"][l].astype(F32), (0, LANES - N_EXPERTS)).reshape(1, LANES)
    h2, xn, logits = _mid(
        h, ya, yb, p["w_glu"][l].astype(BF16), row(p["b_glu"][l]), row(p["g_out_attn"][l]),
        row(p["g_out_ssm"][l]), p["w_out"][l].astype(BF16), row(p["norm_xattn"][l]),
        (p["w_xq"][l] * (HD_X ** -0.5)).astype(BF16), kv[:, :D_X], kv[:, D_X:],
        p["w_xo"][l].astype(BF16), row(p["norm_moe"][l]), wr2, br)
    gates, dest, n_rows, tile_e, tile_rows, tile_slot, tile_next = _route(logits, MOE_TILE)
    out = _moe_experts(_scatter_rows(xn, dest, n_rows), tile_e, tile_rows, tile_slot, tile_next,
                       p["w1"], p["b1"], p["w2"], p["b2"], l)
    return _combine(h2, _gather_rows(out, dest.reshape(-1)), gates, g_final, final=l == DEPTH - 1)


def kernel(x, mem, rel_bias, norm_mix, w_in, s5_a_re, s5_a_im, s5_b_re, s5_b_im, s5_c_re, s5_c_im, s5_log_dt, s5_d, w_glu, b_glu, g_out_attn, g_out_ssm, w_out, norm_xattn, norm_mem, w_xq, w_xkv, w_xo, norm_moe, w_router, b_router, w1, b1, w2, b2, norm_final):
    p = dict(norm_mix=norm_mix, w_in=w_in, s5_a_re=s5_a_re, s5_a_im=s5_a_im, s5_b_re=s5_b_re,
             s5_b_im=s5_b_im, s5_c_re=s5_c_re, s5_c_im=s5_c_im, s5_log_dt=s5_log_dt, s5_d=s5_d,
             w_glu=w_glu, b_glu=b_glu, g_out_attn=g_out_attn, g_out_ssm=g_out_ssm, w_out=w_out,
             norm_xattn=norm_xattn, norm_mem=norm_mem, w_xq=w_xq, w_xkv=w_xkv, w_xo=w_xo,
             norm_moe=norm_moe, w_router=w_router, b_router=b_router, w1=w1, b1=b1, w2=w2, b2=b2)
    biases = [_attn_bias(rel_bias, window, dil, perm)
              for (window, dil), perm in zip(WIN_DIL, (_PERM_D1, _PERM_D4, _PERM_D16))]
    outs = []
    for b in range(x.shape[0]):
        h = _to_span_layout(x[b])
        for l in range(DEPTH):
            h = _layer(h, mem[b], biases, p, l, norm_final)
        outs.append(h)
    return jnp.stack(outs)
```

```python
import functools
import math

import jax
import jax.numpy as jnp
import numpy as np
from jax import lax
from jax.experimental import pallas as pl
from jax.experimental.pallas import tpu as pltpu
from jax.experimental.pallas import tpu_sc as plsc

F32 = jnp.float32
BF16 = jnp.bfloat16

D_MODEL = 1024
DEPTH = 2
EPS = 1e-5
NEG_INF = -1e30
LOG2E = math.log2(math.e)
H_A = 8
HD_A = 64
D_A = H_A * HD_A
WIN_DIL = ((128, 1), (512, 4), (2048, 16))
BLK = 128
D_B = D_MODEL - D_A
S5_CH = 16
S5_G = D_B // S5_CH
S5_P = 64
D_IN = 3 * D_A + D_B
NUM_BUCKETS = 32
REL_MAX_DIST = 2048
H_X = 4
HD_X = 128
D_X = H_X * HD_X
N_EXPERTS = 32
TOP_K = 4
D_FF = D_MODEL
SWIGLU_ALPHA = 1.702
SWIGLU_LIMIT = 7.0

LANES = 128
SUBLANES = 8
NRES = WIN_DIL[-1][1]
SPAN = NRES * BLK
S5_CHUNK = NRES
S5_PAIRS = S5_G // 2
VMEM_LIMIT = 56 * 1024 * 1024

SC_CORES = 2
SC_SUBCORES = 16
SC_ROWS = 64

ROW_TILE = 512
ATTN_D16_UNROLL = 4
MID_CHAINS = 2
RANK_TILE = 1024
MOE_TILE = 512
MOE_ROW_PATHS = 4
MOE_W1_PARTS = 4
MOE_W2_PARTS = 2


def _params(sem):
    return pltpu.CompilerParams(dimension_semantics=sem, vmem_limit_bytes=VMEM_LIMIT)


def _rms(x, g):
    return x * lax.rsqrt(jnp.mean(x * x, axis=-1, keepdims=True) + EPS) * g


def _dot(a, b):
    return jnp.dot(a, b, preferred_element_type=F32)


def _dot_nt(a, b):
    return lax.dot_general(a, b, (((1,), (1,)), ((), ())), preferred_element_type=F32)


def _full(a):
    return pl.BlockSpec(a.shape, lambda *_: (0,) * a.ndim)


def _pack_rows(x):
    c = x.shape[1] // 2
    lo = lax.bitcast_convert_type(x[:, :c].astype(BF16).astype(F32), jnp.uint32)
    hi = lax.bitcast_convert_type(x[:, c:].astype(BF16).astype(F32), jnp.uint32)
    return lax.bitcast_convert_type(lax.shift_right_logical(lo, jnp.uint32(16)) | hi, jnp.int32)


def _unpack_rows(p):
    u = lax.bitcast_convert_type(p, jnp.uint32)
    lo = lax.bitcast_convert_type(lax.shift_left(u, jnp.uint32(16)), F32)
    hi = lax.bitcast_convert_type(u & jnp.uint32(0xFFFF0000), F32)
    return lo, hi


def _to_span_layout(x):
    s = x.shape[0]
    return x.reshape(s // SPAN, BLK, NRES, -1).transpose(0, 2, 1, 3).reshape(s, -1)


def _from_span_layout(x):
    s = x.shape[0]
    return x.reshape(s // SPAN, NRES, BLK, -1).transpose(0, 2, 1, 3).reshape(s, -1)


def _inproj_body(h_ref, g_ref, w_ref, o_ref):
    xn = _rms(h_ref[...], g_ref[...]).astype(BF16)
    o_ref[...] = _dot(xn, w_ref[...]).astype(BF16)


def _inproj(h, g, w):
    s, d = h.shape
    n = w.shape[1]
    tm = min(ROW_TILE, s)
    return pl.pallas_call(
        _inproj_body,
        out_shape=jax.ShapeDtypeStruct((s, n), BF16),
        grid=(s // tm,),
        in_specs=[pl.BlockSpec((tm, d), lambda i: (i, 0)),
                  pl.BlockSpec((1, d), lambda i: (0, 0)),
                  pl.BlockSpec((d, n), lambda i: (0, 0))],
        out_specs=pl.BlockSpec((tm, n), lambda i: (i, 0)),
        compiler_params=_params(("parallel",)),
        name="inproj",
    )(h, g.reshape(1, d), w)


def _combine_inproj_body(h_ref, y0_ref, y1_ref, y2_ref, y3_ref, gate_ref, g_ref, w_ref, h_out, o_ref):
    h = _combined(h_ref, (y0_ref, y1_ref, y2_ref, y3_ref), gate_ref)
    h_out[...] = h
    o_ref[...] = _dot(_rms(h, g_ref[...]).astype(BF16), w_ref[...]).astype(BF16)


def _combine_inproj(h, y, gates, g, w):
    s, d = h.shape
    n = w.shape[1]
    tm = min(ROW_TILE, s)
    nt = s // tm
    return pl.pallas_call(
        _combine_inproj_body,
        out_shape=[jax.ShapeDtypeStruct((s, d), F32), jax.ShapeDtypeStruct((s, n), BF16)],
        grid=(nt,),
        in_specs=[pl.BlockSpec((tm, d), lambda i: (i, 0))]
                 + [pl.BlockSpec((tm, d // 2), lambda i, k=k: (k * nt + i, 0)) for k in range(TOP_K)]
                 + [pl.BlockSpec((tm, TOP_K), lambda i: (i, 0)), pl.BlockSpec((1, d), lambda i: (0, 0)),
                    pl.BlockSpec((d, n), lambda i: (0, 0))],
        out_specs=[pl.BlockSpec((tm, d), lambda i: (i, 0)), pl.BlockSpec((tm, n), lambda i: (i, 0))],
        compiler_params=_params(("parallel",)),
        name="combine_inproj",
    )(h, y, y, y, y, gates, g.reshape(1, d), w)


def _t5_bucket(n):
    max_exact = NUM_BUCKETS // 2
    nf = jnp.maximum(n, 1).astype(F32)
    large = max_exact + (jnp.log(nf / max_exact) / math.log(REL_MAX_DIST / max_exact)
                         * (NUM_BUCKETS - max_exact)).astype(jnp.int32)
    large = jnp.minimum(large, NUM_BUCKETS - 1)
    return jnp.where(n < max_exact, n, large)


def _attn_bias(rel_bias, window, dil, perm):
    steps = window // dil
    perm = jnp.asarray(perm, jnp.int32)
    qi = perm[:, None]
    ki = jnp.concatenate([perm, BLK + perm])[None, :]
    dist = BLK + qi - ki
    in_win = (dist >= 0) & (dist <= steps)
    bucket = _t5_bucket(jnp.clip(dist, 0, steps) * dil)
    onehot = (bucket[:, :, None] == jnp.arange(NUM_BUCKETS, dtype=jnp.int32)).astype(F32)
    bias = jnp.einsum('qkb,bh->hqk', onehot, rel_bias.astype(F32), precision=lax.Precision.HIGHEST)
    bias = jnp.where(in_win[None], bias * LOG2E, NEG_INF)
    return bias.reshape(H_A // 2, 2 * BLK, 2 * BLK)


_PERM_D1 = [NRES * jl + r for r in range(NRES) for jl in range(BLK // NRES)]
_PERM_D4 = [4 * jl + i for i in range(4) for jl in range(BLK // 4)]
_PERM_D16 = list(range(BLK))


def _attn_body(q_ref, k_ref, v_ref, kp_ref, vp_ref, b1_ref, b4_ref, b16_ref, o_ref, acc, mst, lst):
    has_prev = pl.program_id(0) > 0
    lane = lax.broadcasted_iota(jnp.int32, (1, LANES), 1)
    lo = lane < HD_A
    mlo = lo.astype(BF16)
    mhi = (~lo).astype(BF16)
    col = lax.broadcasted_iota(jnp.int32, (2 * BLK, 2 * BLK), 1)
    ones = jnp.ones((2 * BLK, LANES), BF16)

    def tile(q2, kk, vv, bias, mask_prev):
        qs = jnp.concatenate([q2 * mlo, q2 * mhi], axis=0)
        s = _dot_nt(qs, kk) + bias
        if mask_prev:
            s = jnp.where(jnp.logical_or(has_prev, col >= BLK), s, NEG_INF)
        m = jnp.max(s, axis=-1, keepdims=True)
        e = jnp.exp2((s - m).astype(BF16))
        oa = _dot(e, jnp.concatenate([vv, ones], axis=1))
        o = oa[:, :LANES]
        l = oa[:, LANES:]
        return (jnp.where(lo, m[:BLK], m[BLK:]), jnp.where(lo, l[:BLK], l[BLK:]),
                jnp.where(lo, o[:BLK], o[BLK:]))

    def merge(prev, cur):
        mp, lp, ap = prev
        mc, lc, ac = cur
        mn = jnp.maximum(mp, mc)
        a = jnp.exp2(mp - mn)
        b = jnp.exp2(mc - mn)
        return mn, a * lp + b * lc, a * ap + b * ac

    def cat(xs):
        return jnp.concatenate(xs, axis=0)

    def d16_body(i, _):
        for sub in range(ATTN_D16_UNROLL):
            rows = pl.ds(pl.multiple_of((i * ATTN_D16_UNROLL + sub) * BLK, BLK), BLK)
            for hp in range(H_A // 2):
                lanes = slice(hp * LANES, (hp + 1) * LANES)
                kk = cat([kp_ref[rows, lanes], k_ref[rows, lanes]])
                vv = cat([vp_ref[rows, lanes], v_ref[rows, lanes]])
                m2, l2, o2 = tile(q_ref[rows, lanes], kk, vv, b16_ref[hp], True)
                mst[rows, lanes] = m2
                lst[rows, lanes] = l2
                acc[rows, lanes] = o2
        return 0

    lax.fori_loop(0, NRES // ATTN_D16_UNROLL, d16_body, 0)

    def d4_body(r4, _):
        for b in range(4):
            def chunk_rows(bb):
                return [pl.ds(pl.multiple_of(4 * BLK * i + BLK * r4 + 32 * bb, 32), 32) for i in range(4)]
            rows = chunk_rows(b)
            prows = chunk_rows(3 if b == 0 else b - 1)
            kprev, vprev = (kp_ref, vp_ref) if b == 0 else (k_ref, v_ref)
            for hp in range(H_A // 2):
                lanes = slice(hp * LANES, (hp + 1) * LANES)
                q2 = cat([q_ref[rr, lanes] for rr in rows])
                kk = cat([kprev[rr, lanes] for rr in prows] + [k_ref[rr, lanes] for rr in rows])
                vv = cat([vprev[rr, lanes] for rr in prows] + [v_ref[rr, lanes] for rr in rows])
                cur = tile(q2, kk, vv, b4_ref[hp], b == 0)
                prev = (cat([mst[rr, lanes] for rr in rows]), cat([lst[rr, lanes] for rr in rows]),
                        cat([acc[rr, lanes] for rr in rows]))
                mn, ln, an = merge(prev, cur)
                for i, rr in enumerate(rows):
                    part = slice(32 * i, 32 * (i + 1))
                    mst[rr, lanes] = mn[part]
                    lst[rr, lanes] = ln[part]
                    acc[rr, lanes] = an[part]
        return 0

    lax.fori_loop(0, 4, d4_body, 0)

    def d1_pair(ap, kprev, vprev, prev_ap, mask_prev):
        def tiles(a_):
            return [pl.ds(pl.multiple_of(BLK * r + 16 * a_, 16), 16) for r in range(NRES)]
        cur_t = tiles(ap)
        prev_t = tiles(prev_ap)

        def halves(ref, ts, lanes):
            xs = [ref[t, lanes].astype(F32) for t in ts]
            return cat([x[:8] for x in xs]).astype(BF16), cat([x[8:] for x in xs]).astype(BF16)

        for hp in range(H_A // 2):
            lanes = slice(hp * LANES, (hp + 1) * LANES)
            q_e, q_o = halves(q_ref, cur_t, lanes)
            k_e, k_o = halves(k_ref, cur_t, lanes)
            v_e, v_o = halves(v_ref, cur_t, lanes)
            _, k_p = halves(kprev, prev_t, lanes)
            _, v_p = halves(vprev, prev_t, lanes)
            cur_e = tile(q_e, cat([k_p, k_e]), cat([v_p, v_e]), b1_ref[hp], mask_prev)
            cur_o = tile(q_o, cat([k_e, k_o]), cat([v_e, v_o]), b1_ref[hp], False)
            ms = [mst[t, lanes] for t in cur_t]
            ls = [lst[t, lanes] for t in cur_t]
            ac = [acc[t, lanes] for t in cur_t]
            outs = []
            for half, cur in ((0, cur_e), (1, cur_o)):
                part = slice(8 * half, 8 * half + 8)
                prev = (cat([x[part] for x in ms]), cat([x[part] for x in ls]), cat([x[part] for x in ac]))
                _, ln, an = merge(prev, cur)
                outs.append(an / ln)
            for r, t in enumerate(cur_t):
                part = slice(8 * r, 8 * r + 8)
                o_ref[t, lanes] = cat([outs[0][part], outs[1][part]]).astype(o_ref.dtype)

    d1_pair(0, kp_ref, vp_ref, BLK // 16 - 1, True)

    def d1_body(ap, _):
        d1_pair(ap, k_ref, v_ref, ap - 1, False)
        return 0

    lax.fori_loop(1, BLK // 16, d1_body, 0)


def _dilated_attention(proj, biases):
    s = proj.shape[0]
    cur = lambda which: pl.BlockSpec((SPAN, D_A), lambda c: (c, which))
    prev = lambda which: pl.BlockSpec((SPAN, D_A), lambda c: (jnp.maximum(c - 1, 0), which))
    return pl.pallas_call(
        _attn_body,
        out_shape=jax.ShapeDtypeStruct((s, D_A), BF16),
        grid=(s // SPAN,),
        in_specs=[cur(0), cur(1), cur(2), prev(1), prev(2)] + [_full(b) for b in biases],
        out_specs=pl.BlockSpec((SPAN, D_A), lambda c: (c, 0)),
        scratch_shapes=[pltpu.VMEM((SPAN, D_A), F32)] * 3,
        compiler_params=_params(("arbitrary",)),
        name="attn",
    )(proj, proj, proj, proj, proj, *biases)


def _s5_operators(a_re, a_im, b_re, b_im, c_re, c_im, log_dt, d_skip):
    L = S5_CHUNK
    lam = lax.complex(a_re.astype(F32), a_im.astype(F32))
    dt = jnp.exp(log_dt.astype(F32))[:, None]
    a_bar = jnp.exp(lam * dt)
    b_bar = ((a_bar - 1.0) / lam)[..., None] * lax.complex(b_re.astype(F32), b_im.astype(F32))
    c = lax.complex(c_re.astype(F32), c_im.astype(F32))
    j = jnp.arange(L + 1, dtype=F32)
    log_a = lam * dt
    apow = jnp.exp(log_a[None] * j[:, None, None])
    kt = jnp.einsum('gdp,jgp,gpc->gcjd', c, apow[:L], b_bar).real
    skip = d_skip.astype(F32).reshape(S5_G, S5_CH, 1, 1) * jnp.eye(S5_CH)[None, :, None, :]
    kt = (kt + skip * (jnp.arange(L) == 0)[None, None, :, None]).reshape(S5_G, S5_CH, L * S5_CH)
    p = jnp.einsum('sgp,gpc->gscp', apow[:L][::-1], b_bar).reshape(S5_G, L * S5_CH, S5_P)
    ca = jnp.einsum('gdp,tgp->gptd', c, apow[1:L + 1]).reshape(S5_G, S5_P, L * S5_CH)
    a_l = apow[L]

    def pair_blocks(x):
        g, r, w = x.shape
        x = x.reshape(S5_PAIRS, 2, r, w)
        z = jnp.zeros_like(x[:, 0])
        top = jnp.concatenate([x[:, 0], z], axis=-1)
        bot = jnp.concatenate([z, x[:, 1]], axis=-1)
        return jnp.concatenate([top, bot], axis=1)

    p2 = jnp.concatenate([pair_blocks(p.real), pair_blocks(p.imag)], axis=-1)
    q2 = jnp.concatenate([pair_blocks(ca.real), pair_blocks(-ca.imag)], axis=1)
    a_lr = a_l.real.reshape(1, S5_G * S5_P)
    a_li = a_l.imag.reshape(1, S5_G * S5_P)
    return kt, p2.astype(BF16), q2.astype(BF16), a_lr, a_li


def _s5_body(u_ref, kt_ref, p_ref, q_ref, ar_ref, ai_ref, o_ref, m_ref, w_ref, y_ref,
             ere, eim, xre, xim, sre, sim):
    rows = BLK
    gw = S5_CHUNK * S5_CH
    pw = 2 * gw
    per_vreg = LANES // S5_CH
    chunk_of_lane = lax.broadcasted_iota(jnp.int32, (rows, LANES), 1) // S5_CH

    def chunk_transpose(arrs):
        arrs = list(arrs)
        for s in (4, 2, 1):
            upper = (chunk_of_lane & s) != 0
            nxt = list(arrs)
            for i in range(per_vreg):
                if i & s:
                    continue
                lo_a, hi_a = arrs[i], arrs[i + s]
                nxt[i] = jnp.where(upper, pltpu.roll(hi_a, s * S5_CH, axis=1), lo_a)
                nxt[i + s] = jnp.where(upper, hi_a, pltpu.roll(lo_a, LANES - s * S5_CH, axis=1))
            arrs = nxt
        return arrs

    @pl.when(pl.program_id(0) == 0)
    def _():
        sre[...] = jnp.zeros_like(sre)
        sim[...] = jnp.zeros_like(sim)
        lane = lax.broadcasted_iota(jnp.int32, (S5_CH, gw), 1)

        def build(g, _):
            kt = kt_ref[g]
            for s in range(S5_CHUNK):
                blk = kt if s == 0 else jnp.where(lane >= s * S5_CH, pltpu.roll(kt, s * S5_CH, axis=1), 0.0)
                m_ref[g, s * S5_CH:(s + 1) * S5_CH, :] = blk.astype(BF16)
            return 0

        lax.fori_loop(0, S5_G, build, 0)

    for b in range(D_B // LANES):
        for a in range(S5_CHUNK // per_vreg):
            srcs = [u_ref[(per_vreg * a + i) * rows:(per_vreg * a + i + 1) * rows,
                          b * LANES:(b + 1) * LANES].astype(F32) for i in range(per_vreg)]
            for gi, arr in enumerate(chunk_transpose(srcs)):
                g = per_vreg * b + gi
                w_ref[:, g * gw + a * LANES:g * gw + (a + 1) * LANES] = arr.astype(BF16)

    for pr in range(S5_PAIRS):
        e = _dot(w_ref[:, pr * pw:(pr + 1) * pw], p_ref[pr])
        ere[:, pr * LANES:(pr + 1) * LANES] = e[:, :LANES]
        eim[:, pr * LANES:(pr + 1) * LANES] = e[:, LANES:]

    ar = ar_ref[...]
    ai = ai_ref[...]

    def step(n, carry):
        xr, xi = carry
        xre[pl.ds(n, 1), :] = xr
        xim[pl.ds(n, 1), :] = xi
        nr = ar * xr - ai * xi + ere[pl.ds(n, 1), :]
        ni = ar * xi + ai * xr + eim[pl.ds(n, 1), :]
        return nr, ni

    xr, xi = lax.fori_loop(0, rows, step, (sre[...], sim[...]))
    sre[...] = xr
    sim[...] = xi

    for pr in range(S5_PAIRS):
        xin = jnp.concatenate([xre[:, pr * LANES:(pr + 1) * LANES],
                               xim[:, pr * LANES:(pr + 1) * LANES]], axis=-1).astype(BF16)
        yc = _dot(xin, q_ref[pr])
        for half in range(2):
            g = 2 * pr + half
            cols = slice(g * gw, (g + 1) * gw)
            y = _dot(w_ref[:, cols], m_ref[g]) + yc[:, half * gw:(half + 1) * gw]
            y_ref[:, cols] = 0.5 * y * (1.0 + lax.erf(y * (2.0 ** -0.5)))

    for b in range(D_B // LANES):
        for a in range(S5_CHUNK // per_vreg):
            srcs = [y_ref[:, (per_vreg * b + i) * gw + a * LANES:(per_vreg * b + i) * gw + (a + 1) * LANES]
                    for i in range(per_vreg)]
            for ri, arr in enumerate(chunk_transpose(srcs)):
                r = per_vreg * a + ri
                o_ref[r * rows:(r + 1) * rows, b * LANES:(b + 1) * LANES] = arr.astype(BF16)


def _s5_core(proj, ops):
    kt, p2, q2, a_lr, a_li = ops
    s = proj.shape[0]
    gw = S5_CHUNK * S5_CH
    wide = S5_G * gw
    nstate = S5_G * S5_P
    return pl.pallas_call(
        _s5_body,
        out_shape=jax.ShapeDtypeStruct((s, D_B), BF16),
        grid=(s // SPAN,),
        in_specs=[pl.BlockSpec((SPAN, D_B), lambda i: (i, 3 * D_A // D_B)),
                  _full(kt), _full(p2), _full(q2), _full(a_lr), _full(a_li)],
        out_specs=pl.BlockSpec((SPAN, D_B), lambda i: (i, 0)),
        scratch_shapes=[pltpu.VMEM((S5_G, gw, gw), BF16), pltpu.VMEM((BLK, wide), BF16),
                        pltpu.VMEM((BLK, wide), F32)]
                       + [pltpu.VMEM((BLK, nstate), F32)] * 4 + [pltpu.VMEM((1, nstate), F32)] * 2,
        compiler_params=_params(("arbitrary",)),
        name="s5",
    )(proj, kt, p2, q2, a_lr, a_li)


def _split_bf16(x):
    hi = x.astype(BF16)
    lo = (x - hi.astype(F32)).astype(BF16)
    return hi, lo


def _mid_body(h_ref, ya_ref, yb_ref, wglu_ref, bglu_ref, ga_ref, gb_ref, wout_ref, gx_ref, wq_ref,
              k_ref, v_ref, wo_ref, gm_ref, wr_ref, br_ref, h_out, xn_out, logit_out):
    part = h_ref.shape[0] // MID_CHAINS
    for c in range(MID_CHAINS):
        rows = slice(c * part, (c + 1) * part)
        outs = _mid_rows(h_ref[rows, :], ya_ref[rows, :], yb_ref[rows, :], wglu_ref, bglu_ref, ga_ref, gb_ref,
                         wout_ref, gx_ref, wq_ref, k_ref, v_ref, wo_ref, gm_ref, wr_ref, br_ref)
        for ref, val in zip((h_out, xn_out, logit_out), outs):
            ref[rows, :] = val


def _mid_rows(h, ya, yb, wglu_ref, bglu_ref, ga_ref, gb_ref, wout_ref, gx_ref, wq_ref,
              k_ref, v_ref, wo_ref, gm_ref, wr_ref, br_ref):
    gate = jax.nn.sigmoid(_dot(yb, wglu_ref[...]) + bglu_ref[...])
    yb2 = yb.astype(F32) * gate
    na = _rms(ya.astype(F32), ga_ref[...]).astype(BF16)
    nb = _rms(yb2, gb_ref[...]).astype(BF16)
    h1 = h + _dot(na, wout_ref[0:D_A, :]) + _dot(nb, wout_ref[D_A:D_MODEL, :])
    q = _dot(_rms(h1, gx_ref[...]).astype(BF16), wq_ref[...]).astype(BF16)
    heads = []
    for hd in range(H_X):
        lanes = slice(hd * HD_X, (hd + 1) * HD_X)
        s = _dot_nt(q[:, lanes], k_ref[:, lanes])
        e = jnp.exp(s - jnp.max(s, axis=-1, keepdims=True))
        heads.append(_dot(e.astype(BF16), v_ref[:, lanes]) / jnp.sum(e, axis=-1, keepdims=True))
    o = jnp.concatenate(heads, axis=-1).astype(BF16)
    h2 = h1 + _dot(o, wo_ref[...])
    xn = _rms(h2, gm_ref[...])
    x_hi, x_lo = _split_bf16(xn)
    part = _dot(x_hi, wr_ref[...]) + _dot(x_lo, wr_ref[...])
    logits = part + pltpu.roll(part, LANES - N_EXPERTS, axis=1) + br_ref[...]
    return h2, _pack_rows(xn), logits


def _mid(h, ya, yb, wglu, bglu, ga, gb, wout, gx, wq, kmem, vmem, wo, gm, wr2, br):
    s = h.shape[0]
    tm = min(ROW_TILE, s)
    row = lambda w: pl.BlockSpec((tm, w), lambda i: (i, 0))
    consts = [wglu, bglu, ga, gb, wout, gx, wq, kmem, vmem, wo, gm, wr2, br]
    return pl.pallas_call(
        _mid_body,
        out_shape=[jax.ShapeDtypeStruct((s, D_MODEL), F32), jax.ShapeDtypeStruct((s, D_MODEL // 2), jnp.int32),
                   jax.ShapeDtypeStruct((s, LANES), F32)],
        grid=(s // tm,),
        in_specs=[row(D_MODEL), row(D_A), row(D_B)] + [_full(a) for a in consts],
        out_specs=[row(D_MODEL), row(D_MODEL // 2), row(LANES)],
        compiler_params=_params(("parallel",)),
        name="mid",
    )(h, ya, yb, *consts)


def _memkv_body(mem_ref, g_ref, w_ref, o_ref):
    o_ref[...] = _dot(_rms(mem_ref[...], g_ref[...]).astype(BF16), w_ref[...]).astype(BF16)


def _memkv(mem, g, w):
    n, d = mem.shape
    return pl.pallas_call(
        _memkv_body,
        out_shape=jax.ShapeDtypeStruct((n, w.shape[1]), BF16),
        compiler_params=pltpu.CompilerParams(vmem_limit_bytes=VMEM_LIMIT),
        name="memkv",
    )(mem, g.reshape(1, d), w)


def _rank_body(logit_ref, tri_ref, idx_ref, gate_ref, rank_ref, cnt_ref, carry):
    @pl.when(pl.program_id(0) == 0)
    def _():
        carry[...] = jnp.zeros_like(carry)

    tm = logit_ref.shape[0]
    logits = jnp.transpose(logit_ref[...])[:N_EXPERTS, :]
    expert = lax.broadcasted_iota(jnp.int32, (N_EXPERTS, tm), 0)
    vals, idxs = [], []
    for _ in range(TOP_K):
        mx = jnp.max(logits, axis=0, keepdims=True)
        ix = jnp.min(jnp.where(logits == mx, expert, N_EXPERTS), axis=0, keepdims=True)
        vals.append(mx)
        idxs.append(ix)
        logits = jnp.where(expert == ix, -jnp.inf, logits)
    es = [jnp.exp(v - vals[0]) for v in vals]
    den = es[0] + es[1] + es[2] + es[3]
    hits = [expert == ix for ix in idxs]
    onehot = jnp.zeros((N_EXPERTS, tm), F32)
    for hit in hits:
        onehot = onehot + jnp.where(hit, 1.0, 0.0)
    nb = tm // LANES
    blocks = jnp.concatenate([onehot[:, b * LANES:(b + 1) * LANES] for b in range(nb)], axis=0)
    inc = _dot(blocks.astype(BF16), tri_ref[...])
    run = carry[...][:, 0:1]
    before = []
    for b in range(nb):
        inc_b = inc[b * N_EXPERTS:(b + 1) * N_EXPERTS, :]
        before.append(run + inc_b - onehot[:, b * LANES:(b + 1) * LANES])
        run = run + inc_b[:, LANES - 1:LANES]
    before = jnp.concatenate(before, axis=1)
    choice = lax.broadcasted_iota(jnp.int32, (SUBLANES, tm), 0)
    idx_t = jnp.full((SUBLANES, tm), N_EXPERTS, jnp.int32)
    gate_t = jnp.zeros((SUBLANES, tm), F32)
    rank_t = jnp.zeros((SUBLANES, tm), jnp.int32)
    for k in range(TOP_K):
        rk = jnp.sum(jnp.where(hits[k], before, 0.0), axis=0, keepdims=True)
        idx_t = jnp.where(choice == k, idxs[k], idx_t)
        gate_t = jnp.where(choice == k, es[k] / den, gate_t)
        rank_t = jnp.where(choice == k, rk.astype(jnp.int32), rank_t)
    idx_ref[...] = idx_t
    gate_ref[...] = gate_t
    rank_ref[...] = rank_t
    total = jnp.broadcast_to(run, carry.shape)
    carry[...] = total
    cnt_ref[...] = total.astype(jnp.int32)


def _rank(logits):
    t = logits.shape[0]
    tm = min(RANK_TILE, t)
    cols = pl.BlockSpec((SUBLANES, tm), lambda i: (0, i))
    tri = jnp.asarray(np.triu(np.ones((LANES, LANES), np.float32)), BF16)
    return pl.pallas_call(
        _rank_body,
        out_shape=[jax.ShapeDtypeStruct((SUBLANES, t), jnp.int32), jax.ShapeDtypeStruct((SUBLANES, t), F32),
                   jax.ShapeDtypeStruct((SUBLANES, t), jnp.int32),
                   jax.ShapeDtypeStruct((N_EXPERTS, LANES), jnp.int32)],
        grid=(t // tm,),
        in_specs=[pl.BlockSpec((tm, LANES), lambda i: (i, 0)), _full(tri)],
        out_specs=[cols, cols, cols, pl.BlockSpec((N_EXPERTS, LANES), lambda i: (0, 0))],
        scratch_shapes=[pltpu.VMEM((N_EXPERTS, LANES), F32)],
        compiler_params=_params(("arbitrary",)),
        name="rank",
    )(logits, tri)


def _route(logits, tm):
    t = logits.shape[0]
    tk = t * TOP_K
    idx, gates, rank, cnt = _rank(logits)
    counts = cnt[:, 0]
    padded = (counts + tm - 1) // tm * tm
    pend = jnp.cumsum(padded)
    pstart = pend - padded
    n_rows = tk + N_EXPERTS * tm
    n_tiles = n_rows // tm
    experts = jnp.arange(N_EXPERTS, dtype=jnp.int32)
    tile_first = jnp.arange(n_tiles, dtype=jnp.int32) * tm
    last_used = jnp.max(jnp.where(padded > 0, experts, 0))
    tile_e = jnp.minimum(jnp.sum(tile_first[:, None] >= pend[None, :], axis=1), last_used).astype(jnp.int32)
    tile_rows = jnp.clip(jnp.sum(jnp.where(tile_e[:, None] == experts[None, :],
                                           (pstart + counts)[None, :], 0), axis=1) - tile_first, 0, tm)
    tile_rows = jnp.where(tile_first < pend[-1], tile_rows, 0).astype(jnp.int32)
    group = jnp.cumsum(jnp.concatenate([jnp.zeros((1,), jnp.int32),
                                        (tile_e[1:] != tile_e[:-1]).astype(jnp.int32)]))
    tile_slot = (group % 2).astype(jnp.int32)
    later = (experts[None, :] > experts[:, None]) & (padded > 0)[None, :]
    next_e = jnp.min(jnp.where(later, experts[None, :], N_EXPERTS), axis=1)
    next_e = jnp.where(next_e < N_EXPERTS, next_e, -1).astype(jnp.int32)
    tile_next = jnp.sum(jnp.where(tile_e[:, None] == experts[None, :], next_e[None, :], 0), axis=1).astype(jnp.int32)
    base = jnp.sum(jnp.where(idx[:TOP_K, :, None] == experts, pstart, 0), axis=-1)
    dest = rank[:TOP_K] + base
    return gates[:TOP_K].T, dest, n_rows, tile_e, tile_rows, tile_slot, tile_next


def _sc_mesh():
    return plsc.VectorSubcoreMesh(core_axis_name="c", subcore_axis_name="s",
                                  num_cores=SC_CORES, num_subcores=SC_SUBCORES)


def _sc_worker():
    return lax.axis_index("s") * SC_CORES + lax.axis_index("c")


def _scatter_rows(x, dest, n_rows):
    t, d = x.shape
    per_worker = t // (SC_CORES * SC_SUBCORES)
    chunks = per_worker // SC_ROWS

    assert chunks % 2 == 0

    @functools.partial(
        pl.kernel, mesh=_sc_mesh(),
        out_type=jax.ShapeDtypeStruct((n_rows, d), x.dtype),
        scratch_types=[pltpu.VMEM((SC_ROWS, d), x.dtype)] * 2 + [pltpu.VMEM((SC_ROWS,), jnp.int32)] * (2 * TOP_K)
                      + [pltpu.SemaphoreType.DMA((2,)), pltpu.SemaphoreType.DMA((2, TOP_K))],
    )
    def scatter(x_hbm, *rest):
        dest_hbm, out_hbm = rest[:TOP_K], rest[TOP_K]
        rows_v = rest[TOP_K + 1:TOP_K + 3]
        idx_v = (rest[TOP_K + 3:2 * TOP_K + 3], rest[2 * TOP_K + 3:3 * TOP_K + 3])
        lsem, ssem = rest[3 * TOP_K + 3], rest[3 * TOP_K + 4]
        base = _sc_worker() * per_worker

        def rows_at(c):
            return pl.ds(pl.multiple_of(base + c * SC_ROWS, SC_ROWS), SC_ROWS)

        def loaded(c, s):
            return pltpu.make_async_copy(x_hbm.at[rows_at(c)], rows_v[s], lsem.at[s])

        def load(c, s):
            loaded(c, s).start()
            for k in range(TOP_K):
                pltpu.sync_copy(dest_hbm[k].at[rows_at(c)], idx_v[s][k])

        def scattered(s):
            return [pltpu.make_async_copy(rows_v[s], out_hbm.at[idx_v[s][k]], ssem.at[s, k]) for k in range(TOP_K)]

        load(0, 0)

        @pl.loop(0, chunks, step=2)
        def _(c0):
            for s in range(2):
                c = c0 + s
                loaded(c, s).wait()
                for cp in scattered(s):
                    cp.start()

                @pl.when(c >= 1)
                def _():
                    for cp in scattered(1 - s):
                        cp.wait()

                @pl.when(c + 1 < chunks)
                def _():
                    load(c + 1, 1 - s)

        for cp in scattered(1):
            cp.wait()

    return scatter(x, *[dest[k] for k in range(TOP_K)])


def _gather_rows(table, idx):
    n, d = table.shape
    b = idx.shape[0]
    per_worker = b // (SC_CORES * SC_SUBCORES)
    chunks = per_worker // SC_ROWS
    assert chunks % 2 == 0

    @functools.partial(
        pl.kernel, mesh=_sc_mesh(),
        out_type=jax.ShapeDtypeStruct((b, d), table.dtype),
        scratch_types=[pltpu.VMEM((chunks, SC_ROWS), jnp.int32)] + [pltpu.VMEM((SC_ROWS, d), table.dtype)] * 2
                      + [pltpu.SemaphoreType.DMA((2,)), pltpu.SemaphoreType.DMA((2,))],
    )
    def gather(table_hbm, idx_hbm, out_hbm, idx_v, rows0, rows1, gsem, wsem):
        rows_v = (rows0, rows1)
        worker = _sc_worker()
        base = worker * per_worker
        pltpu.sync_copy(idx_hbm.at[pl.ds(pl.multiple_of(worker * chunks, chunks), chunks)], idx_v)

        def rows_at(c):
            return pl.ds(pl.multiple_of(base + c * SC_ROWS, SC_ROWS), SC_ROWS)

        def fetched(c, s):
            return pltpu.make_async_copy(table_hbm.at[idx_v.at[c]], rows_v[s], gsem.at[s])

        def written(c, s):
            return pltpu.make_async_copy(rows_v[s], out_hbm.at[rows_at(c)], wsem.at[s])

        fetched(0, 0).start()

        @pl.loop(0, chunks, step=2)
        def _(c0):
            for s in range(2):
                c = c0 + s
                fetched(c, s).wait()
                written(c, s).start()

                @pl.when(c >= 1)
                def _():
                    written(c - 1, 1 - s).wait()

                @pl.when(c + 1 < chunks)
                def _():
                    fetched(c + 1, 1 - s).start()

        written(chunks - 1, 1).wait()

    return gather(table, idx.reshape(b // SC_ROWS, SC_ROWS))


def _moe_body(te_ref, tv_ref, sl_ref, nx_ref, x_ref, w1_hbm, b1_ref, w2_hbm, b2_ref, o_ref,
              w1f, w2f, w1b, w2b, sem, *, layer):
    i = pl.program_id(0)
    e = te_ref[i]
    slot = sl_ref[i]
    new_expert = (i == 0) | (e != te_ref[jnp.maximum(i - 1, 0)])

    def weight_copies(expert, s):
        rows1 = w1f.shape[1] // MOE_W1_PARTS
        rows2 = w2f.shape[1] // MOE_W2_PARTS
        c1 = [pltpu.make_async_copy(w1_hbm.at[layer, expert, pl.ds(q * rows1, rows1)],
                                    w1f.at[s, pl.ds(q * rows1, rows1)], sem.at[s, q])
              for q in range(MOE_W1_PARTS)]
        c2 = [pltpu.make_async_copy(w2_hbm.at[layer, expert, pl.ds(q * rows2, rows2)],
                                    w2f.at[s, pl.ds(q * rows2, rows2)], sem.at[s, MOE_W1_PARTS + q])
              for q in range(MOE_W2_PARTS)]
        return c1 + c2

    @pl.when(i == 0)
    def _():
        for c in weight_copies(e, slot):
            c.start()

    @pl.when(new_expert)
    def _():
        for c in weight_copies(e, slot):
            c.wait()
        nxt = nx_ref[i]

        @pl.when(nxt >= 0)
        def _():
            for c in weight_copies(nxt, 1 - slot):
                c.start()

        w1b[...] = w1f[slot].astype(BF16)
        w2b[...] = w2f[slot].astype(BF16)

    def expert(rows):
        row = lax.broadcasted_iota(jnp.int32, (rows, x_ref.shape[1]), 0)
        lo, hi = _unpack_rows(jnp.where(row < tv_ref[i], x_ref[0:rows, :], 0))
        x = jnp.concatenate([lo, hi], axis=-1).astype(BF16)
        hb = _dot(x, w1b[...]) + b1_ref[0]
        x_glu = jnp.minimum(hb[:, :D_FF], SWIGLU_LIMIT)
        x_lin = jnp.clip(hb[:, D_FF:], -SWIGLU_LIMIT, SWIGLU_LIMIT)
        act = x_glu * jax.nn.sigmoid(SWIGLU_ALPHA * x_glu) * (x_lin + 1.0)
        o_ref[0:rows, :] = _pack_rows(_dot(act.astype(BF16), w2b[...]) + b2_ref[0])

    step = x_ref.shape[0] // MOE_ROW_PATHS
    for path in range(1, MOE_ROW_PATHS + 1):
        rows = path * step

        @pl.when((tv_ref[i] > rows - step) & (tv_ref[i] <= rows))
        def _(rows=rows):
            expert(rows)
            if rows < x_ref.shape[0]:
                o_ref[rows:, :] = jnp.zeros((x_ref.shape[0] - rows, o_ref.shape[1]), o_ref.dtype)

    @pl.when(tv_ref[i] == 0)
    def _():
        o_ref[...] = jnp.zeros_like(o_ref)


def _moe_experts(xs, tile_e, tile_rows, tile_slot, tile_next, w1, b1, w2, b2, layer):
    n_rows = xs.shape[0]
    tm = MOE_TILE
    nl, ne, d, ff2 = w1.shape
    bias_map = lambda i, te, tv, sl, nx: (layer, te[i], 0, 0)
    grid_spec = pltpu.PrefetchScalarGridSpec(
        num_scalar_prefetch=4,
        grid=(n_rows // tm,),
        in_specs=[pl.BlockSpec((tm, d // 2), lambda i, *_: (i, 0)),
                  pl.BlockSpec(memory_space=pl.ANY),
                  pl.BlockSpec((None, 1, 1, ff2), bias_map),
                  pl.BlockSpec(memory_space=pl.ANY),
                  pl.BlockSpec((None, 1, 1, d), bias_map)],
        out_specs=pl.BlockSpec((tm, d // 2), lambda i, *_: (i, 0)),
        scratch_shapes=[pltpu.VMEM((2, d, ff2), F32), pltpu.VMEM((2, ff2 // 2, d), F32),
                        pltpu.VMEM((d, ff2), BF16), pltpu.VMEM((ff2 // 2, d), BF16),
                        pltpu.SemaphoreType.DMA((2, MOE_W1_PARTS + MOE_W2_PARTS))],
    )
    return pl.pallas_call(
        functools.partial(_moe_body, layer=layer),
        out_shape=jax.ShapeDtypeStruct((n_rows, d // 2), jnp.int32),
        grid_spec=grid_spec,
        compiler_params=_params(("arbitrary",)),
        name="moe",
    )(tile_e, tile_rows, tile_slot, tile_next, xs, w1, b1.reshape(nl, ne, 1, ff2), w2, b2.reshape(nl, ne, 1, d))


def _combined(h_ref, y_refs, gate_ref):
    gates = gate_ref[...]
    lo = jnp.zeros(y_refs[0].shape, F32)
    hi = jnp.zeros(y_refs[0].shape, F32)
    for k, y_ref in enumerate(y_refs):
        yl, yh = _unpack_rows(y_ref[...])
        lo = lo + yl * gates[:, k:k + 1]
        hi = hi + yh * gates[:, k:k + 1]
    return h_ref[...] + jnp.concatenate([lo, hi], axis=-1)


def _combine_final_body(h_ref, y0_ref, y1_ref, y2_ref, y3_ref, gate_ref, g_ref, out_hbm, buf, sem):
    i = pl.program_id(0)
    n = pl.num_programs(0)
    slot = i % 2
    per_step = h_ref.shape[0] // BLK
    steps_per_span = NRES // per_step

    def writes(step, s):
        span = step // steps_per_span
        r0 = (step % steps_per_span) * per_step
        return [pltpu.make_async_copy(buf.at[s, pl.ds(rr * BLK, BLK), :], out_hbm.at[span, :, r0 + rr, :],
                                      sem.at[s, rr]) for rr in range(per_step)]

    @pl.when(i >= 2)
    def _():
        for cp in writes(i - 2, slot):
            cp.wait()

    buf[slot] = _rms(_combined(h_ref, (y0_ref, y1_ref, y2_ref, y3_ref), gate_ref), g_ref[...])
    for cp in writes(i, slot):
        cp.start()

    @pl.when(i == n - 1)
    def _():
        for cp in writes(i, slot):
            cp.wait()

        @pl.when(i >= 1)
        def _():
            for cp in writes(i - 1, 1 - slot):
                cp.wait()


def _combine_final(h, y, gates, g):
    s, d = h.shape
    tm = min(ROW_TILE // 2, s)
    nt = s // tm
    in_specs = ([pl.BlockSpec((tm, d), lambda i: (i, 0))]
                + [pl.BlockSpec((tm, d // 2), lambda i, k=k: (k * nt + i, 0)) for k in range(TOP_K)]
                + [pl.BlockSpec((tm, TOP_K), lambda i: (i, 0))])
    out = pl.pallas_call(
        _combine_final_body,
        out_shape=jax.ShapeDtypeStruct((s // SPAN, BLK, NRES, d), F32),
        grid=(nt,),
        in_specs=in_specs + [pl.BlockSpec((1, d), lambda i: (0, 0))],
        out_specs=pl.BlockSpec(memory_space=pl.ANY),
        scratch_shapes=[pltpu.VMEM((2, tm, d), F32), pltpu.SemaphoreType.DMA((2, tm // BLK))],
        compiler_params=_params(("arbitrary",)),
        name="combine_final",
    )(h, y, y, y, y, gates, g.reshape(1, d))
    return out.reshape(s, d)


def _layer(state, mem, biases, p, l, g_final):
    row = lambda a: a.reshape(1, -1).astype(F32)
    q_scale = jnp.concatenate([jnp.full((D_A,), HD_A ** -0.5 * LOG2E, F32), jnp.ones((D_IN - D_A,), F32)])
    w_in = (p["w_in"][l] * q_scale).astype(BF16)
    if l == 0:
        h, proj = state, _inproj(state, p["norm_mix"][l], w_in)
    else:
        h, proj = _combine_inproj(*state, p["norm_mix"][l], w_in)
    ya = _dilated_attention(proj, biases)
    ops = _s5_operators(p["s5_a_re"][l], p["s5_a_im"][l], p["s5_b_re"][l], p["s5_b_im"][l],
                        p["s5_c_re"][l], p["s5_c_im"][l], p["s5_log_dt"][l], p["s5_d"][l])
    yb = _s5_core(proj, ops)
    kv = _memkv(mem, p["norm_mem"][l], p["w_xkv"][l].astype(BF16))
    wr = p["w_router"][l].astype(F32)
    wr_hi = wr.astype(BF16)
    wr2 = jnp.pad(jnp.concatenate([wr_hi, (wr - wr_hi.astype(F32)).astype(BF16)], axis=1),
                  ((0, 0), (0, LANES - 2 * N_EXPERTS)))
    br = jnp.pad(p["b_router"][l].astype(F32), (0, LANES - N_EXPERTS)).reshape(1, LANES)
    h2, xn, logits = _mid(
        h, ya, yb, p["w_glu"][l].astype(BF16), row(p["b_glu"][l]), row(p["g_out_attn"][l]),
        row(p["g_out_ssm"][l]), p["w_out"][l].astype(BF16), row(p["norm_xattn"][l]),
        (p["w_xq"][l] * (HD_X ** -0.5)).astype(BF16), kv[:, :D_X], kv[:, D_X:],
        p["w_xo"][l].astype(BF16), row(p["norm_moe"][l]), wr2, br)
    gates, dest, n_rows, tile_e, tile_rows, tile_slot, tile_next = _route(logits, MOE_TILE)
    out = _moe_experts(_scatter_rows(xn, dest, n_rows), tile_e, tile_rows, tile_slot, tile_next,
                       p["w1"], p["b1"], p["w2"], p["b2"], l)
    y = _gather_rows(out, dest.reshape(-1))
    return _combine_final(h2, y, gates, g_final) if l == DEPTH - 1 else (h2, y, gates)


def kernel(x, mem, rel_bias, norm_mix, w_in, s5_a_re, s5_a_im, s5_b_re, s5_b_im, s5_c_re, s5_c_im, s5_log_dt, s5_d, w_glu, b_glu, g_out_attn, g_out_ssm, w_out, norm_xattn, norm_mem, w_xq, w_xkv, w_xo, norm_moe, w_router, b_router, w1, b1, w2, b2, norm_final):
    p = dict(norm_mix=norm_mix, w_in=w_in, s5_a_re=s5_a_re, s5_a_im=s5_a_im, s5_b_re=s5_b_re,
             s5_b_im=s5_b_im, s5_c_re=s5_c_re, s5_c_im=s5_c_im, s5_log_dt=s5_log_dt, s5_d=s5_d,
             w_glu=w_glu, b_glu=b_glu, g_out_attn=g_out_attn, g_out_ssm=g_out_ssm, w_out=w_out,
             norm_xattn=norm_xattn, norm_mem=norm_mem, w_xq=w_xq, w_xkv=w_xkv, w_xo=w_xo,
             norm_moe=norm_moe, w_router=w_router, b_router=b_router, w1=w1, b1=b1, w2=w2, b2=b2)
    biases = [_attn_bias(rel_bias, window, dil, perm)
              for (window, dil), perm in zip(WIN_DIL, (_PERM_D1, _PERM_D4, _PERM_D16))]
    outs = []
    for b in range(x.shape[0]):
        h = _to_span_layout(x[b])
        for l in range(DEPTH):
            h = _layer(h, mem[b], biases, p, l, norm_final)
        outs.append(h)
    return jnp.stack(outs)
```

```python
import functools
import math

import jax
import jax.numpy as jnp
import numpy as np
from jax import lax
from jax.experimental import pallas as pl
from jax.experimental.pallas import tpu as pltpu
from jax.experimental.pallas import tpu_sc as plsc

F32 = jnp.float32
BF16 = jnp.bfloat16

D_MODEL = 1024
DEPTH = 2
EPS = 1e-5
NEG_INF = -1e30
LOG2E = math.log2(math.e)
H_A = 8
HD_A = 64
D_A = H_A * HD_A
WIN_DIL = ((128, 1), (512, 4), (2048, 16))
BLK = 128
D_B = D_MODEL - D_A
S5_CH = 16
S5_G = D_B // S5_CH
S5_P = 64
D_IN = 3 * D_A + D_B
NUM_BUCKETS = 32
REL_MAX_DIST = 2048
H_X = 4
HD_X = 128
D_X = H_X * HD_X
N_EXPERTS = 32
TOP_K = 4
D_FF = D_MODEL
SWIGLU_ALPHA = 1.702
SWIGLU_LIMIT = 7.0

LANES = 128
SUBLANES = 8
NRES = WIN_DIL[-1][1]
SPAN = NRES * BLK
S5_CHUNK = NRES
S5_PAIRS = S5_G // 2
VMEM_LIMIT = 56 * 1024 * 1024

SC_CORES = 2
SC_SUBCORES = 16
SC_ROWS = 64

ROW_TILE = 512
ATTN_D16_UNROLL = 4
MID_CHAINS = 2
RANK_TILE = 1024
MOE_TILE = 512
MOE_ROW_PATHS = 4
MOE_W1_PARTS = 4
MOE_W2_PARTS = 2


def _params(sem):
    return pltpu.CompilerParams(dimension_semantics=sem, vmem_limit_bytes=VMEM_LIMIT)


def _rms(x, g):
    return x * lax.rsqrt(jnp.mean(x * x, axis=-1, keepdims=True) + EPS) * g


def _dot(a, b):
    return jnp.dot(a, b, preferred_element_type=F32)


def _dot_nt(a, b):
    return lax.dot_general(a, b, (((1,), (1,)), ((), ())), preferred_element_type=F32)


def _full(a):
    return pl.BlockSpec(a.shape, lambda *_: (0,) * a.ndim)


def _pack_rows(x):
    c = x.shape[1] // 2
    lo = lax.bitcast_convert_type(x[:, :c].astype(BF16).astype(F32), jnp.uint32)
    hi = lax.bitcast_convert_type(x[:, c:].astype(BF16).astype(F32), jnp.uint32)
    return lax.bitcast_convert_type(lax.shift_right_logical(lo, jnp.uint32(16)) | hi, jnp.int32)


def _unpack_rows(p):
    u = lax.bitcast_convert_type(p, jnp.uint32)
    lo = lax.bitcast_convert_type(lax.shift_left(u, jnp.uint32(16)), F32)
    hi = lax.bitcast_convert_type(u & jnp.uint32(0xFFFF0000), F32)
    return lo, hi


def _to_span_layout(x):
    s = x.shape[0]
    return x.reshape(s // SPAN, BLK, NRES, -1).transpose(0, 2, 1, 3).reshape(s, -1)


def _inproj_body(h_ref, g_ref, w_ref, o_ref):
    xn = _rms(h_ref[...], g_ref[...]).astype(BF16)
    o_ref[...] = _dot(xn, w_ref[...]).astype(BF16)


def _inproj(h, g, w):
    s, d = h.shape
    n = w.shape[1]
    tm = min(ROW_TILE, s)
    return pl.pallas_call(
        _inproj_body,
        out_shape=jax.ShapeDtypeStruct((s, n), BF16),
        grid=(s // tm,),
        in_specs=[pl.BlockSpec((tm, d), lambda i: (i, 0)),
                  pl.BlockSpec((1, d), lambda i: (0, 0)),
                  pl.BlockSpec((d, n), lambda i: (0, 0))],
        out_specs=pl.BlockSpec((tm, n), lambda i: (i, 0)),
        compiler_params=_params(("parallel",)),
        name="inproj",
    )(h, g.reshape(1, d), w)


def _combine_inproj_body(h_ref, y0_ref, y1_ref, y2_ref, y3_ref, gate_ref, g_ref, w_ref, h_out, o_ref):
    h = _combined(h_ref, (y0_ref, y1_ref, y2_ref, y3_ref), gate_ref)
    h_out[...] = h
    o_ref[...] = _dot(_rms(h, g_ref[...]).astype(BF16), w_ref[...]).astype(BF16)


def _combine_inproj(h, y, gates, g, w):
    s, d = h.shape
    n = w.shape[1]
    tm = min(ROW_TILE, s)
    nt = s // tm
    return pl.pallas_call(
        _combine_inproj_body,
        out_shape=[jax.ShapeDtypeStruct((s, d), F32), jax.ShapeDtypeStruct((s, n), BF16)],
        grid=(nt,),
        in_specs=[pl.BlockSpec((tm, d), lambda i: (i, 0))]
                 + [pl.BlockSpec((tm, d // 2), lambda i, k=k: (k * nt + i, 0)) for k in range(TOP_K)]
                 + [pl.BlockSpec((tm, LANES), lambda i: (i, 0)), pl.BlockSpec((1, d), lambda i: (0, 0)),
                    pl.BlockSpec((d, n), lambda i: (0, 0))],
        out_specs=[pl.BlockSpec((tm, d), lambda i: (i, 0)), pl.BlockSpec((tm, n), lambda i: (i, 0))],
        compiler_params=_params(("parallel",)),
        name="combine_inproj",
    )(h, y, y, y, y, gates, g.reshape(1, d), w)


def _t5_bucket(n):
    max_exact = NUM_BUCKETS // 2
    nf = jnp.maximum(n, 1).astype(F32)
    large = max_exact + (jnp.log(nf / max_exact) / math.log(REL_MAX_DIST / max_exact)
                         * (NUM_BUCKETS - max_exact)).astype(jnp.int32)
    large = jnp.minimum(large, NUM_BUCKETS - 1)
    return jnp.where(n < max_exact, n, large)


def _attn_bias(rel_bias, window, dil, perm):
    steps = window // dil
    perm = jnp.asarray(perm, jnp.int32)
    qi = perm[:, None]
    ki = jnp.concatenate([perm, BLK + perm])[None, :]
    dist = BLK + qi - ki
    in_win = (dist >= 0) & (dist <= steps)
    bucket = _t5_bucket(jnp.clip(dist, 0, steps) * dil)
    onehot = (bucket[:, :, None] == jnp.arange(NUM_BUCKETS, dtype=jnp.int32)).astype(F32)
    bias = jnp.einsum('qkb,bh->hqk', onehot, rel_bias.astype(F32), precision=lax.Precision.HIGHEST)
    bias = jnp.where(in_win[None], bias * LOG2E, NEG_INF)
    return bias.reshape(H_A // 2, 2 * BLK, 2 * BLK)


_PERM_D1 = [NRES * jl + r for r in range(NRES) for jl in range(BLK // NRES)]
_PERM_D4 = [4 * jl + i for i in range(4) for jl in range(BLK // 4)]
_PERM_D16 = list(range(BLK))


def _attn_body(q_ref, k_ref, v_ref, kp_ref, vp_ref, b1_ref, b4_ref, b16_ref, o_ref, acc, mst, lst):
    has_prev = pl.program_id(0) > 0
    lane = lax.broadcasted_iota(jnp.int32, (1, LANES), 1)
    lo = lane < HD_A
    mlo = lo.astype(BF16)
    mhi = (~lo).astype(BF16)
    col = lax.broadcasted_iota(jnp.int32, (2 * BLK, 2 * BLK), 1)
    ones = jnp.ones((2 * BLK, LANES), BF16)

    def tile(q2, kk, vv, bias, mask_prev):
        qs = jnp.concatenate([q2 * mlo, q2 * mhi], axis=0)
        s = _dot_nt(qs, kk) + bias
        if mask_prev:
            s = jnp.where(jnp.logical_or(has_prev, col >= BLK), s, NEG_INF)
        m = jnp.max(s, axis=-1, keepdims=True)
        e = jnp.exp2((s - m).astype(BF16))
        oa = _dot(e, jnp.concatenate([vv, ones], axis=1))
        o = oa[:, :LANES]
        l = oa[:, LANES:]
        return (jnp.where(lo, m[:BLK], m[BLK:]), jnp.where(lo, l[:BLK], l[BLK:]),
                jnp.where(lo, o[:BLK], o[BLK:]))

    def merge(prev, cur):
        mp, lp, ap = prev
        mc, lc, ac = cur
        mn = jnp.maximum(mp, mc)
        a = jnp.exp2(mp - mn)
        b = jnp.exp2(mc - mn)
        return mn, a * lp + b * lc, a * ap + b * ac

    def cat(xs):
        return jnp.concatenate(xs, axis=0)

    def d16_body(i, _):
        for sub in range(ATTN_D16_UNROLL):
            rows = pl.ds(pl.multiple_of((i * ATTN_D16_UNROLL + sub) * BLK, BLK), BLK)
            for hp in range(H_A // 2):
                lanes = slice(hp * LANES, (hp + 1) * LANES)
                kk = cat([kp_ref[rows, lanes], k_ref[rows, lanes]])
                vv = cat([vp_ref[rows, lanes], v_ref[rows, lanes]])
                m2, l2, o2 = tile(q_ref[rows, lanes], kk, vv, b16_ref[hp], True)
                mst[rows, lanes] = m2
                lst[rows, lanes] = l2
                acc[rows, lanes] = o2
        return 0

    lax.fori_loop(0, NRES // ATTN_D16_UNROLL, d16_body, 0)

    def d4_body(r4, _):
        for b in range(4):
            def chunk_rows(bb):
                return [pl.ds(pl.multiple_of(4 * BLK * i + BLK * r4 + 32 * bb, 32), 32) for i in range(4)]
            rows = chunk_rows(b)
            prows = chunk_rows(3 if b == 0 else b - 1)
            kprev, vprev = (kp_ref, vp_ref) if b == 0 else (k_ref, v_ref)
            for hp in range(H_A // 2):
                lanes = slice(hp * LANES, (hp + 1) * LANES)
                q2 = cat([q_ref[rr, lanes] for rr in rows])
                kk = cat([kprev[rr, lanes] for rr in prows] + [k_ref[rr, lanes] for rr in rows])
                vv = cat([vprev[rr, lanes] for rr in prows] + [v_ref[rr, lanes] for rr in rows])
                cur = tile(q2, kk, vv, b4_ref[hp], b == 0)
                prev = (cat([mst[rr, lanes] for rr in rows]), cat([lst[rr, lanes] for rr in rows]),
                        cat([acc[rr, lanes] for rr in rows]))
                mn, ln, an = merge(prev, cur)
                for i, rr in enumerate(rows):
                    part = slice(32 * i, 32 * (i + 1))
                    mst[rr, lanes] = mn[part]
                    lst[rr, lanes] = ln[part]
                    acc[rr, lanes] = an[part]
        return 0

    lax.fori_loop(0, 4, d4_body, 0)

    def d1_pair(ap, kprev, vprev, prev_ap, mask_prev):
        def tiles(a_):
            return [pl.ds(pl.multiple_of(BLK * r + 16 * a_, 16), 16) for r in range(NRES)]
        cur_t = tiles(ap)
        prev_t = tiles(prev_ap)

        def halves(ref, ts, lanes):
            xs = [ref[t, lanes].astype(F32) for t in ts]
            return cat([x[:8] for x in xs]).astype(BF16), cat([x[8:] for x in xs]).astype(BF16)

        for hp in range(H_A // 2):
            lanes = slice(hp * LANES, (hp + 1) * LANES)
            q_e, q_o = halves(q_ref, cur_t, lanes)
            k_e, k_o = halves(k_ref, cur_t, lanes)
            v_e, v_o = halves(v_ref, cur_t, lanes)
            _, k_p = halves(kprev, prev_t, lanes)
            _, v_p = halves(vprev, prev_t, lanes)
            cur_e = tile(q_e, cat([k_p, k_e]), cat([v_p, v_e]), b1_ref[hp], mask_prev)
            cur_o = tile(q_o, cat([k_e, k_o]), cat([v_e, v_o]), b1_ref[hp], False)
            ms = [mst[t, lanes] for t in cur_t]
            ls = [lst[t, lanes] for t in cur_t]
            ac = [acc[t, lanes] for t in cur_t]
            outs = []
            for half, cur in ((0, cur_e), (1, cur_o)):
                part = slice(8 * half, 8 * half + 8)
                prev = (cat([x[part] for x in ms]), cat([x[part] for x in ls]), cat([x[part] for x in ac]))
                _, ln, an = merge(prev, cur)
                outs.append(an / ln)
            for r, t in enumerate(cur_t):
                part = slice(8 * r, 8 * r + 8)
                o_ref[t, lanes] = cat([outs[0][part], outs[1][part]]).astype(o_ref.dtype)

    d1_pair(0, kp_ref, vp_ref, BLK // 16 - 1, True)

    def d1_body(ap, _):
        d1_pair(ap, k_ref, v_ref, ap - 1, False)
        return 0

    lax.fori_loop(1, BLK // 16, d1_body, 0)


def _dilated_attention(proj, biases):
    s = proj.shape[0]
    cur = lambda which: pl.BlockSpec((SPAN, D_A), lambda c: (c, which))
    prev = lambda which: pl.BlockSpec((SPAN, D_A), lambda c: (jnp.maximum(c - 1, 0), which))
    return pl.pallas_call(
        _attn_body,
        out_shape=jax.ShapeDtypeStruct((s, D_A), BF16),
        grid=(s // SPAN,),
        in_specs=[cur(0), cur(1), cur(2), prev(1), prev(2)] + [_full(b) for b in biases],
        out_specs=pl.BlockSpec((SPAN, D_A), lambda c: (c, 0)),
        scratch_shapes=[pltpu.VMEM((SPAN, D_A), F32)] * 3,
        compiler_params=_params(("arbitrary",)),
        name="attn",
    )(proj, proj, proj, proj, proj, *biases)


def _s5_operators(a_re, a_im, b_re, b_im, c_re, c_im, log_dt, d_skip):
    L = S5_CHUNK
    lam = lax.complex(a_re.astype(F32), a_im.astype(F32))
    dt = jnp.exp(log_dt.astype(F32))[:, None]
    a_bar = jnp.exp(lam * dt)
    b_bar = ((a_bar - 1.0) / lam)[..., None] * lax.complex(b_re.astype(F32), b_im.astype(F32))
    c = lax.complex(c_re.astype(F32), c_im.astype(F32))
    j = jnp.arange(L + 1, dtype=F32)
    log_a = lam * dt
    apow = jnp.exp(log_a[None] * j[:, None, None])
    kt = jnp.einsum('gdp,jgp,gpc->gcjd', c, apow[:L], b_bar).real
    skip = d_skip.astype(F32).reshape(S5_G, S5_CH, 1, 1) * jnp.eye(S5_CH)[None, :, None, :]
    kt = (kt + skip * (jnp.arange(L) == 0)[None, None, :, None]).reshape(S5_G, S5_CH, L * S5_CH)
    p = jnp.einsum('sgp,gpc->gscp', apow[:L][::-1], b_bar).reshape(S5_G, L * S5_CH, S5_P)
    ca = jnp.einsum('gdp,tgp->gptd', c, apow[1:L + 1]).reshape(S5_G, S5_P, L * S5_CH)
    a_l = apow[L]

    def pair_blocks(x):
        g, r, w = x.shape
        x = x.reshape(S5_PAIRS, 2, r, w)
        z = jnp.zeros_like(x[:, 0])
        top = jnp.concatenate([x[:, 0], z], axis=-1)
        bot = jnp.concatenate([z, x[:, 1]], axis=-1)
        return jnp.concatenate([top, bot], axis=1)

    p2 = jnp.concatenate([pair_blocks(p.real), pair_blocks(p.imag)], axis=-1)
    q2 = jnp.concatenate([pair_blocks(ca.real), pair_blocks(-ca.imag)], axis=1)
    a_lr = a_l.real.reshape(1, S5_G * S5_P)
    a_li = a_l.imag.reshape(1, S5_G * S5_P)
    return kt, p2.astype(BF16), q2.astype(BF16), a_lr, a_li


def _s5_body(u_ref, kt_ref, p_ref, q_ref, ar_ref, ai_ref, o_ref, m_ref, w_ref, y_ref,
             ere, eim, xre, xim, sre, sim):
    rows = BLK
    gw = S5_CHUNK * S5_CH
    pw = 2 * gw
    per_vreg = LANES // S5_CH
    chunk_of_lane = lax.broadcasted_iota(jnp.int32, (rows, LANES), 1) // S5_CH

    def chunk_transpose(arrs):
        arrs = list(arrs)
        for s in (4, 2, 1):
            upper = (chunk_of_lane & s) != 0
            nxt = list(arrs)
            for i in range(per_vreg):
                if i & s:
                    continue
                lo_a, hi_a = arrs[i], arrs[i + s]
                nxt[i] = jnp.where(upper, pltpu.roll(hi_a, s * S5_CH, axis=1), lo_a)
                nxt[i + s] = jnp.where(upper, hi_a, pltpu.roll(lo_a, LANES - s * S5_CH, axis=1))
            arrs = nxt
        return arrs

    @pl.when(pl.program_id(0) == 0)
    def _():
        sre[...] = jnp.zeros_like(sre)
        sim[...] = jnp.zeros_like(sim)
        lane = lax.broadcasted_iota(jnp.int32, (S5_CH, gw), 1)

        def build(g, _):
            kt = kt_ref[g]
            for s in range(S5_CHUNK):
                blk = kt if s == 0 else jnp.where(lane >= s * S5_CH, pltpu.roll(kt, s * S5_CH, axis=1), 0.0)
                m_ref[g, s * S5_CH:(s + 1) * S5_CH, :] = blk.astype(BF16)
            return 0

        lax.fori_loop(0, S5_G, build, 0)

    for b in range(D_B // LANES):
        for a in range(S5_CHUNK // per_vreg):
            srcs = [u_ref[(per_vreg * a + i) * rows:(per_vreg * a + i + 1) * rows,
                          b * LANES:(b + 1) * LANES].astype(F32) for i in range(per_vreg)]
            for gi, arr in enumerate(chunk_transpose(srcs)):
                g = per_vreg * b + gi
                w_ref[:, g * gw + a * LANES:g * gw + (a + 1) * LANES] = arr.astype(BF16)

    for pr in range(S5_PAIRS):
        e = _dot(w_ref[:, pr * pw:(pr + 1) * pw], p_ref[pr])
        ere[:, pr * LANES:(pr + 1) * LANES] = e[:, :LANES]
        eim[:, pr * LANES:(pr + 1) * LANES] = e[:, LANES:]

    ar = ar_ref[...]
    ai = ai_ref[...]

    def step(n, carry):
        xr, xi = carry
        xre[pl.ds(n, 1), :] = xr
        xim[pl.ds(n, 1), :] = xi
        nr = ar * xr - ai * xi + ere[pl.ds(n, 1), :]
        ni = ar * xi + ai * xr + eim[pl.ds(n, 1), :]
        return nr, ni

    xr, xi = lax.fori_loop(0, rows, step, (sre[...], sim[...]))
    sre[...] = xr
    sim[...] = xi

    for pr in range(S5_PAIRS):
        xin = jnp.concatenate([xre[:, pr * LANES:(pr + 1) * LANES],
                               xim[:, pr * LANES:(pr + 1) * LANES]], axis=-1).astype(BF16)
        yc = _dot(xin, q_ref[pr])
        for half in range(2):
            g = 2 * pr + half
            cols = slice(g * gw, (g + 1) * gw)
            y = _dot(w_ref[:, cols], m_ref[g]) + yc[:, half * gw:(half + 1) * gw]
            y_ref[:, cols] = 0.5 * y * (1.0 + lax.erf(y * (2.0 ** -0.5)))

    for b in range(D_B // LANES):
        for a in range(S5_CHUNK // per_vreg):
            srcs = [y_ref[:, (per_vreg * b + i) * gw + a * LANES:(per_vreg * b + i) * gw + (a + 1) * LANES]
                    for i in range(per_vreg)]
            for ri, arr in enumerate(chunk_transpose(srcs)):
                r = per_vreg * a + ri
                o_ref[r * rows:(r + 1) * rows, b * LANES:(b + 1) * LANES] = arr.astype(BF16)


def _s5_core(proj, ops):
    kt, p2, q2, a_lr, a_li = ops
    s = proj.shape[0]
    gw = S5_CHUNK * S5_CH
    wide = S5_G * gw
    nstate = S5_G * S5_P
    return pl.pallas_call(
        _s5_body,
        out_shape=jax.ShapeDtypeStruct((s, D_B), BF16),
        grid=(s // SPAN,),
        in_specs=[pl.BlockSpec((SPAN, D_B), lambda i: (i, 3 * D_A // D_B)),
                  _full(kt), _full(p2), _full(q2), _full(a_lr), _full(a_li)],
        out_specs=pl.BlockSpec((SPAN, D_B), lambda i: (i, 0)),
        scratch_shapes=[pltpu.VMEM((S5_G, gw, gw), BF16), pltpu.VMEM((BLK, wide), BF16),
                        pltpu.VMEM((BLK, wide), F32)]
                       + [pltpu.VMEM((BLK, nstate), F32)] * 4 + [pltpu.VMEM((1, nstate), F32)] * 2,
        compiler_params=_params(("arbitrary",)),
        name="s5",
    )(proj, kt, p2, q2, a_lr, a_li)


def _split_bf16(x):
    hi = x.astype(BF16)
    lo = (x - hi.astype(F32)).astype(BF16)
    return hi, lo


def _mid_body(h_ref, ya_ref, yb_ref, wglu_ref, bglu_ref, ga_ref, gb_ref, wout_ref, gx_ref, wq_ref,
              k_ref, v_ref, wo_ref, gm_ref, wr_ref, br_ref, h_out, xn_out, logit_out):
    part = h_ref.shape[0] // MID_CHAINS
    for c in range(MID_CHAINS):
        rows = slice(c * part, (c + 1) * part)
        outs = _mid_rows(h_ref[rows, :], ya_ref[rows, :], yb_ref[rows, :], wglu_ref, bglu_ref, ga_ref, gb_ref,
                         wout_ref, gx_ref, wq_ref, k_ref, v_ref, wo_ref, gm_ref, wr_ref, br_ref)
        for ref, val in zip((h_out, xn_out, logit_out), outs):
            ref[rows, :] = val


def _mid_rows(h, ya, yb, wglu_ref, bglu_ref, ga_ref, gb_ref, wout_ref, gx_ref, wq_ref,
              k_ref, v_ref, wo_ref, gm_ref, wr_ref, br_ref):
    gate = jax.nn.sigmoid(_dot(yb, wglu_ref[...]) + bglu_ref[...])
    yb2 = yb.astype(F32) * gate
    na = _rms(ya.astype(F32), ga_ref[...]).astype(BF16)
    nb = _rms(yb2, gb_ref[...]).astype(BF16)
    h1 = h + _dot(na, wout_ref[0:D_A, :]) + _dot(nb, wout_ref[D_A:D_MODEL, :])
    q = _dot(_rms(h1, gx_ref[...]).astype(BF16), wq_ref[...]).astype(BF16)
    heads = []
    for hd in range(H_X):
        lanes = slice(hd * HD_X, (hd + 1) * HD_X)
        s = _dot_nt(q[:, lanes], k_ref[:, lanes])
        e = jnp.exp(s - jnp.max(s, axis=-1, keepdims=True))
        heads.append(_dot(e.astype(BF16), v_ref[:, lanes]) / jnp.sum(e, axis=-1, keepdims=True))
    o = jnp.concatenate(heads, axis=-1).astype(BF16)
    h2 = h1 + _dot(o, wo_ref[...])
    xn = _rms(h2, gm_ref[...])
    x_hi, x_lo = _split_bf16(xn)
    part = _dot(x_hi, wr_ref[...]) + _dot(x_lo, wr_ref[...])
    logits = part + pltpu.roll(part, LANES - N_EXPERTS, axis=1) + br_ref[...]
    return h2, _pack_rows(xn), logits


def _mid(h, ya, yb, wglu, bglu, ga, gb, wout, gx, wq, kmem, vmem, wo, gm, wr2, br):
    s = h.shape[0]
    tm = min(ROW_TILE, s)
    row = lambda w: pl.BlockSpec((tm, w), lambda i: (i, 0))
    consts = [wglu, bglu, ga, gb, wout, gx, wq, kmem, vmem, wo, gm, wr2, br]
    return pl.pallas_call(
        _mid_body,
        out_shape=[jax.ShapeDtypeStruct((s, D_MODEL), F32), jax.ShapeDtypeStruct((s, D_MODEL // 2), jnp.int32),
                   jax.ShapeDtypeStruct((s, LANES), F32)],
        grid=(s // tm,),
        in_specs=[row(D_MODEL), row(D_A), row(D_B)] + [_full(a) for a in consts],
        out_specs=[row(D_MODEL), row(D_MODEL // 2), row(LANES)],
        compiler_params=_params(("parallel",)),
        name="mid",
    )(h, ya, yb, *consts)


def _memkv_body(mem_ref, g_ref, w_ref, o_ref):
    o_ref[...] = _dot(_rms(mem_ref[...], g_ref[...]).astype(BF16), w_ref[...]).astype(BF16)


def _memkv(mem, g, w):
    n, d = mem.shape
    return pl.pallas_call(
        _memkv_body,
        out_shape=jax.ShapeDtypeStruct((n, w.shape[1]), BF16),
        compiler_params=pltpu.CompilerParams(vmem_limit_bytes=VMEM_LIMIT),
        name="memkv",
    )(mem, g.reshape(1, d), w)


def _rank_body(logit_ref, tri_ref, idx_ref, gate_ref, rank_ref, cnt_ref, carry):
    @pl.when(pl.program_id(0) == 0)
    def _():
        carry[...] = jnp.zeros_like(carry)

    tm = logit_ref.shape[0]
    logits = jnp.transpose(logit_ref[...])[:N_EXPERTS, :]
    expert = lax.broadcasted_iota(jnp.int32, (N_EXPERTS, tm), 0)
    vals, idxs = [], []
    for _ in range(TOP_K):
        mx = jnp.max(logits, axis=0, keepdims=True)
        ix = jnp.min(jnp.where(logits == mx, expert, N_EXPERTS), axis=0, keepdims=True)
        vals.append(mx)
        idxs.append(ix)
        logits = jnp.where(expert == ix, -jnp.inf, logits)
    es = [jnp.exp(v - vals[0]) for v in vals]
    den = es[0] + es[1] + es[2] + es[3]
    hits = [expert == ix for ix in idxs]
    onehot = jnp.zeros((N_EXPERTS, tm), F32)
    for hit in hits:
        onehot = onehot + jnp.where(hit, 1.0, 0.0)
    nb = tm // LANES
    blocks = jnp.concatenate([onehot[:, b * LANES:(b + 1) * LANES] for b in range(nb)], axis=0)
    inc = _dot(blocks.astype(BF16), tri_ref[...])
    run = carry[...][:, 0:1]
    before = []
    for b in range(nb):
        inc_b = inc[b * N_EXPERTS:(b + 1) * N_EXPERTS, :]
        before.append(run + inc_b - onehot[:, b * LANES:(b + 1) * LANES])
        run = run + inc_b[:, LANES - 1:LANES]
    before = jnp.concatenate(before, axis=1)
    choice = lax.broadcasted_iota(jnp.int32, (SUBLANES, tm), 0)
    idx_t = jnp.full((SUBLANES, tm), N_EXPERTS, jnp.int32)
    gate_t = jnp.zeros((SUBLANES, tm), F32)
    rank_t = jnp.zeros((SUBLANES, tm), jnp.int32)
    for k in range(TOP_K):
        rk = jnp.sum(jnp.where(hits[k], before, 0.0), axis=0, keepdims=True)
        idx_t = jnp.where(choice == k, idxs[k], idx_t)
        gate_t = jnp.where(choice == k, es[k] / den, gate_t)
        rank_t = jnp.where(choice == k, rk.astype(jnp.int32), rank_t)
    idx_ref[...] = idx_t
    gate_ref[...] = jnp.transpose(jnp.concatenate([gate_t, jnp.zeros((LANES - SUBLANES, tm), F32)], axis=0))
    rank_ref[...] = rank_t
    total = jnp.broadcast_to(run, carry.shape)
    carry[...] = total
    cnt_ref[...] = total.astype(jnp.int32)


def _rank(logits):
    t = logits.shape[0]
    tm = min(RANK_TILE, t)
    cols = pl.BlockSpec((SUBLANES, tm), lambda i: (0, i))
    rows = pl.BlockSpec((tm, LANES), lambda i: (i, 0))
    tri = jnp.asarray(np.triu(np.ones((LANES, LANES), np.float32)), BF16)
    return pl.pallas_call(
        _rank_body,
        out_shape=[jax.ShapeDtypeStruct((SUBLANES, t), jnp.int32), jax.ShapeDtypeStruct((t, LANES), F32),
                   jax.ShapeDtypeStruct((SUBLANES, t), jnp.int32),
                   jax.ShapeDtypeStruct((N_EXPERTS, LANES), jnp.int32)],
        grid=(t // tm,),
        in_specs=[rows, _full(tri)],
        out_specs=[cols, rows, cols, pl.BlockSpec((N_EXPERTS, LANES), lambda i: (0, 0))],
        scratch_shapes=[pltpu.VMEM((N_EXPERTS, LANES), F32)],
        compiler_params=_params(("arbitrary",)),
        name="rank",
    )(logits, tri)


def _route(logits, tm):
    t = logits.shape[0]
    tk = t * TOP_K
    idx, gates, rank, cnt = _rank(logits)
    counts = cnt[:, 0]
    padded = (counts + tm - 1) // tm * tm
    pend = jnp.cumsum(padded)
    pstart = pend - padded
    n_rows = tk + N_EXPERTS * tm
    n_tiles = n_rows // tm
    experts = jnp.arange(N_EXPERTS, dtype=jnp.int32)
    tile_first = jnp.arange(n_tiles, dtype=jnp.int32) * tm
    last_used = jnp.max(jnp.where(padded > 0, experts, 0))
    tile_e = jnp.minimum(jnp.sum(tile_first[:, None] >= pend[None, :], axis=1), last_used).astype(jnp.int32)
    tile_rows = jnp.clip(jnp.sum(jnp.where(tile_e[:, None] == experts[None, :],
                                           (pstart + counts)[None, :], 0), axis=1) - tile_first, 0, tm)
    tile_rows = jnp.where(tile_first < pend[-1], tile_rows, 0).astype(jnp.int32)
    group = jnp.cumsum(jnp.concatenate([jnp.zeros((1,), jnp.int32),
                                        (tile_e[1:] != tile_e[:-1]).astype(jnp.int32)]))
    tile_slot = (group % 2).astype(jnp.int32)
    later = (experts[None, :] > experts[:, None]) & (padded > 0)[None, :]
    next_e = jnp.min(jnp.where(later, experts[None, :], N_EXPERTS), axis=1)
    next_e = jnp.where(next_e < N_EXPERTS, next_e, -1).astype(jnp.int32)
    tile_next = jnp.sum(jnp.where(tile_e[:, None] == experts[None, :], next_e[None, :], 0), axis=1).astype(jnp.int32)
    base = jnp.sum(jnp.where(idx[:TOP_K, :, None] == experts, pstart, 0), axis=-1)
    dest = rank[:TOP_K] + base
    return gates, dest, n_rows, tile_e, tile_rows, tile_slot, tile_next


def _sc_mesh():
    return plsc.VectorSubcoreMesh(core_axis_name="c", subcore_axis_name="s",
                                  num_cores=SC_CORES, num_subcores=SC_SUBCORES)


def _sc_worker():
    return lax.axis_index("s") * SC_CORES + lax.axis_index("c")


def _scatter_rows(x, dest, n_rows):
    t, d = x.shape
    per_worker = t // (SC_CORES * SC_SUBCORES)
    chunks = per_worker // SC_ROWS

    assert chunks % 2 == 0

    @functools.partial(
        pl.kernel, mesh=_sc_mesh(),
        out_type=jax.ShapeDtypeStruct((n_rows, d), x.dtype),
        scratch_types=[pltpu.VMEM((SC_ROWS, d), x.dtype)] * 2 + [pltpu.VMEM((SC_ROWS,), jnp.int32)] * (2 * TOP_K)
                      + [pltpu.SemaphoreType.DMA((2,)), pltpu.SemaphoreType.DMA((2, TOP_K))],
    )
    def scatter(x_hbm, *rest):
        dest_hbm, out_hbm = rest[:TOP_K], rest[TOP_K]
        rows_v = rest[TOP_K + 1:TOP_K + 3]
        idx_v = (rest[TOP_K + 3:2 * TOP_K + 3], rest[2 * TOP_K + 3:3 * TOP_K + 3])
        lsem, ssem = rest[3 * TOP_K + 3], rest[3 * TOP_K + 4]
        base = _sc_worker() * per_worker

        def rows_at(c):
            return pl.ds(pl.multiple_of(base + c * SC_ROWS, SC_ROWS), SC_ROWS)

        def loaded(c, s):
            return pltpu.make_async_copy(x_hbm.at[rows_at(c)], rows_v[s], lsem.at[s])

        def load(c, s):
            loaded(c, s).start()
            for k in range(TOP_K):
                pltpu.sync_copy(dest_hbm[k].at[rows_at(c)], idx_v[s][k])

        def scattered(s):
            return [pltpu.make_async_copy(rows_v[s], out_hbm.at[idx_v[s][k]], ssem.at[s, k]) for k in range(TOP_K)]

        load(0, 0)

        @pl.loop(0, chunks, step=2)
        def _(c0):
            for s in range(2):
                c = c0 + s
                loaded(c, s).wait()
                for cp in scattered(s):
                    cp.start()

                @pl.when(c >= 1)
                def _():
                    for cp in scattered(1 - s):
                        cp.wait()

                @pl.when(c + 1 < chunks)
                def _():
                    load(c + 1, 1 - s)

        for cp in scattered(1):
            cp.wait()

    return scatter(x, *[dest[k] for k in range(TOP_K)])


def _gather_rows(table, idx):
    n, d = table.shape
    b = idx.shape[0]
    per_worker = b // (SC_CORES * SC_SUBCORES)
    chunks = per_worker // SC_ROWS
    assert chunks % 2 == 0

    @functools.partial(
        pl.kernel, mesh=_sc_mesh(),
        out_type=jax.ShapeDtypeStruct((b, d), table.dtype),
        scratch_types=[pltpu.VMEM((chunks, SC_ROWS), jnp.int32)] + [pltpu.VMEM((SC_ROWS, d), table.dtype)] * 2
                      + [pltpu.SemaphoreType.DMA((2,)), pltpu.SemaphoreType.DMA((2,))],
    )
    def gather(table_hbm, idx_hbm, out_hbm, idx_v, rows0, rows1, gsem, wsem):
        rows_v = (rows0, rows1)
        worker = _sc_worker()
        base = worker * per_worker
        pltpu.sync_copy(idx_hbm.at[pl.ds(pl.multiple_of(worker * chunks, chunks), chunks)], idx_v)

        def rows_at(c):
            return pl.ds(pl.multiple_of(base + c * SC_ROWS, SC_ROWS), SC_ROWS)

        def fetched(c, s):
            return pltpu.make_async_copy(table_hbm.at[idx_v.at[c]], rows_v[s], gsem.at[s])

        def written(c, s):
            return pltpu.make_async_copy(rows_v[s], out_hbm.at[rows_at(c)], wsem.at[s])

        fetched(0, 0).start()

        @pl.loop(0, chunks, step=2)
        def _(c0):
            for s in range(2):
                c = c0 + s
                fetched(c, s).wait()
                written(c, s).start()

                @pl.when(c >= 1)
                def _():
                    written(c - 1, 1 - s).wait()

                @pl.when(c + 1 < chunks)
                def _():
                    fetched(c + 1, 1 - s).start()

        written(chunks - 1, 1).wait()

    return gather(table, idx.reshape(b // SC_ROWS, SC_ROWS))


def _moe_body(te_ref, tv_ref, sl_ref, nx_ref, x_ref, w1_hbm, b1_ref, w2_hbm, b2_ref, o_ref,
              w1f, w2f, w1b, w2b, sem, *, layer):
    i = pl.program_id(0)
    e = te_ref[i]
    slot = sl_ref[i]
    new_expert = (i == 0) | (e != te_ref[jnp.maximum(i - 1, 0)])

    def weight_copies(expert, s):
        rows1 = w1f.shape[1] // MOE_W1_PARTS
        rows2 = w2f.shape[1] // MOE_W2_PARTS
        c1 = [pltpu.make_async_copy(w1_hbm.at[layer, expert, pl.ds(q * rows1, rows1)],
                                    w1f.at[s, pl.ds(q * rows1, rows1)], sem.at[s, q])
              for q in range(MOE_W1_PARTS)]
        c2 = [pltpu.make_async_copy(w2_hbm.at[layer, expert, pl.ds(q * rows2, rows2)],
                                    w2f.at[s, pl.ds(q * rows2, rows2)], sem.at[s, MOE_W1_PARTS + q])
              for q in range(MOE_W2_PARTS)]
        return c1 + c2

    @pl.when(i == 0)
    def _():
        for c in weight_copies(e, slot):
            c.start()

    @pl.when(new_expert)
    def _():
        for c in weight_copies(e, slot):
            c.wait()
        nxt = nx_ref[i]

        @pl.when(nxt >= 0)
        def _():
            for c in weight_copies(nxt, 1 - slot):
                c.start()

        w1b[...] = w1f[slot].astype(BF16)
        w2b[...] = w2f[slot].astype(BF16)

    def expert(rows):
        row = lax.broadcasted_iota(jnp.int32, (rows, x_ref.shape[1]), 0)
        lo, hi = _unpack_rows(jnp.where(row < tv_ref[i], x_ref[0:rows, :], 0))
        x = jnp.concatenate([lo, hi], axis=-1).astype(BF16)
        hb = _dot(x, w1b[...]) + b1_ref[0]
        x_glu = jnp.minimum(hb[:, :D_FF], SWIGLU_LIMIT)
        x_lin = jnp.clip(hb[:, D_FF:], -SWIGLU_LIMIT, SWIGLU_LIMIT)
        act = x_glu * jax.nn.sigmoid(SWIGLU_ALPHA * x_glu) * (x_lin + 1.0)
        o_ref[0:rows, :] = _pack_rows(_dot(act.astype(BF16), w2b[...]) + b2_ref[0])

    step = x_ref.shape[0] // MOE_ROW_PATHS
    for path in range(1, MOE_ROW_PATHS + 1):
        rows = path * step

        @pl.when((tv_ref[i] > rows - step) & (tv_ref[i] <= rows))
        def _(rows=rows):
            expert(rows)
            if rows < x_ref.shape[0]:
                o_ref[rows:, :] = jnp.zeros((x_ref.shape[0] - rows, o_ref.shape[1]), o_ref.dtype)

    @pl.when(tv_ref[i] == 0)
    def _():
        o_ref[...] = jnp.zeros_like(o_ref)


def _moe_experts(xs, tile_e, tile_rows, tile_slot, tile_next, w1, b1, w2, b2, layer):
    n_rows = xs.shape[0]
    tm = MOE_TILE
    nl, ne, d, ff2 = w1.shape
    bias_map = lambda i, te, tv, sl, nx: (layer, te[i], 0, 0)
    grid_spec = pltpu.PrefetchScalarGridSpec(
        num_scalar_prefetch=4,
        grid=(n_rows // tm,),
        in_specs=[pl.BlockSpec((tm, d // 2), lambda i, *_: (i, 0)),
                  pl.BlockSpec(memory_space=pl.ANY),
                  pl.BlockSpec((None, 1, 1, ff2), bias_map),
                  pl.BlockSpec(memory_space=pl.ANY),
                  pl.BlockSpec((None, 1, 1, d), bias_map)],
        out_specs=pl.BlockSpec((tm, d // 2), lambda i, *_: (i, 0)),
        scratch_shapes=[pltpu.VMEM((2, d, ff2), F32), pltpu.VMEM((2, ff2 // 2, d), F32),
                        pltpu.VMEM((d, ff2), BF16), pltpu.VMEM((ff2 // 2, d), BF16),
                        pltpu.SemaphoreType.DMA((2, MOE_W1_PARTS + MOE_W2_PARTS))],
    )
    return pl.pallas_call(
        functools.partial(_moe_body, layer=layer),
        out_shape=jax.ShapeDtypeStruct((n_rows, d // 2), jnp.int32),
        grid_spec=grid_spec,
        compiler_params=_params(("arbitrary",)),
        name="moe",
    )(tile_e, tile_rows, tile_slot, tile_next, xs, w1, b1.reshape(nl, ne, 1, ff2), w2, b2.reshape(nl, ne, 1, d))


def _combined(h_ref, y_refs, gate_ref):
    gates = gate_ref[...]
    lo = jnp.zeros(y_refs[0].shape, F32)
    hi = jnp.zeros(y_refs[0].shape, F32)
    for k, y_ref in enumerate(y_refs):
        yl, yh = _unpack_rows(y_ref[...])
        lo = lo + yl * gates[:, k:k + 1]
        hi = hi + yh * gates[:, k:k + 1]
    return h_ref[...] + jnp.concatenate([lo, hi], axis=-1)


def _combine_final_body(h_ref, y0_ref, y1_ref, y2_ref, y3_ref, gate_ref, g_ref, out_hbm, buf, sem):
    i = pl.program_id(0)
    n = pl.num_programs(0)
    slot = i % 2
    per_step = h_ref.shape[0] // BLK
    steps_per_span = NRES // per_step

    def writes(step, s):
        span = step // steps_per_span
        r0 = (step % steps_per_span) * per_step
        return [pltpu.make_async_copy(buf.at[s, pl.ds(rr * BLK, BLK), :], out_hbm.at[span, :, r0 + rr, :],
                                      sem.at[s, rr]) for rr in range(per_step)]

    @pl.when(i >= 2)
    def _():
        for cp in writes(i - 2, slot):
            cp.wait()

    buf[slot] = _rms(_combined(h_ref, (y0_ref, y1_ref, y2_ref, y3_ref), gate_ref), g_ref[...])
    for cp in writes(i, slot):
        cp.start()

    @pl.when(i == n - 1)
    def _():
        for cp in writes(i, slot):
            cp.wait()

        @pl.when(i >= 1)
        def _():
            for cp in writes(i - 1, 1 - slot):
                cp.wait()


def _combine_final(h, y, gates, g):
    s, d = h.shape
    tm = min(ROW_TILE // 2, s)
    nt = s // tm
    in_specs = ([pl.BlockSpec((tm, d), lambda i: (i, 0))]
                + [pl.BlockSpec((tm, d // 2), lambda i, k=k: (k * nt + i, 0)) for k in range(TOP_K)]
                + [pl.BlockSpec((tm, LANES), lambda i: (i, 0))])
    out = pl.pallas_call(
        _combine_final_body,
        out_shape=jax.ShapeDtypeStruct((s // SPAN, BLK, NRES, d), F32),
        grid=(nt,),
        in_specs=in_specs + [pl.BlockSpec((1, d), lambda i: (0, 0))],
        out_specs=pl.BlockSpec(memory_space=pl.ANY),
        scratch_shapes=[pltpu.VMEM((2, tm, d), F32), pltpu.SemaphoreType.DMA((2, tm // BLK))],
        compiler_params=_params(("arbitrary",)),
        name="combine_final",
    )(h, y, y, y, y, gates, g.reshape(1, d))
    return out.reshape(s, d)


def _layer(state, mem, biases, p, l, g_final):
    row = lambda a: a.reshape(1, -1).astype(F32)
    q_scale = jnp.concatenate([jnp.full((D_A,), HD_A ** -0.5 * LOG2E, F32), jnp.ones((D_IN - D_A,), F32)])
    w_in = (p["w_in"][l] * q_scale).astype(BF16)
    if l == 0:
        h, proj = state, _inproj(state, p["norm_mix"][l], w_in)
    else:
        h, proj = _combine_inproj(*state, p["norm_mix"][l], w_in)
    ya = _dilated_attention(proj, biases)
    ops = _s5_operators(p["s5_a_re"][l], p["s5_a_im"][l], p["s5_b_re"][l], p["s5_b_im"][l],
                        p["s5_c_re"][l], p["s5_c_im"][l], p["s5_log_dt"][l], p["s5_d"][l])
    yb = _s5_core(proj, ops)
    kv = _memkv(mem, p["norm_mem"][l], p["w_xkv"][l].astype(BF16))
    wr = p["w_router"][l].astype(F32)
    wr_hi = wr.astype(BF16)
    wr2 = jnp.pad(jnp.concatenate([wr_hi, (wr - wr_hi.astype(F32)).astype(BF16)], axis=1),
                  ((0, 0), (0, LANES - 2 * N_EXPERTS)))
    br = jnp.pad(p["b_router"][l].astype(F32), (0, LANES - N_EXPERTS)).reshape(1, LANES)
    h2, xn, logits = _mid(
        h, ya, yb, p["w_glu"][l].astype(BF16), row(p["b_glu"][l]), row(p["g_out_attn"][l]),
        row(p["g_out_ssm"][l]), p["w_out"][l].astype(BF16), row(p["norm_xattn"][l]),
        (p["w_xq"][l] * (HD_X ** -0.5)).astype(BF16), kv[:, :D_X], kv[:, D_X:],
        p["w_xo"][l].astype(BF16), row(p["norm_moe"][l]), wr2, br)
    gates, dest, n_rows, tile_e, tile_rows, tile_slot, tile_next = _route(logits, MOE_TILE)
    out = _moe_experts(_scatter_rows(xn, dest, n_rows), tile_e, tile_rows, tile_slot, tile_next,
                       p["w1"], p["b1"], p["w2"], p["b2"], l)
    y = _gather_rows(out, dest.reshape(-1))
    return _combine_final(h2, y, gates, g_final) if l == DEPTH - 1 else (h2, y, gates)


def kernel(x, mem, rel_bias, norm_mix, w_in, s5_a_re, s5_a_im, s5_b_re, s5_b_im, s5_c_re, s5_c_im, s5_log_dt, s5_d, w_glu, b_glu, g_out_attn, g_out_ssm, w_out, norm_xattn, norm_mem, w_xq, w_xkv, w_xo, norm_moe, w_router, b_router, w1, b1, w2, b2, norm_final):
    p = dict(norm_mix=norm_mix, w_in=w_in, s5_a_re=s5_a_re, s5_a_im=s5_a_im, s5_b_re=s5_b_re,
             s5_b_im=s5_b_im, s5_c_re=s5_c_re, s5_c_im=s5_c_im, s5_log_dt=s5_log_dt, s5_d=s5_d,
             w_glu=w_glu, b_glu=b_glu, g_out_attn=g_out_attn, g_out_ssm=g_out_ssm, w_out=w_out,
             norm_xattn=norm_xattn, norm_mem=norm_mem, w_xq=w_xq, w_xkv=w_xkv, w_xo=w_xo,
             norm_moe=norm_moe, w_router=w_router, b_router=b_router, w1=w1, b1=b1, w2=w2, b2=b2)
    biases = [_attn_bias(rel_bias, window, dil, perm)
              for (window, dil), perm in zip(WIN_DIL, (_PERM_D1, _PERM_D4, _PERM_D16))]
    outs = []
    for b in range(x.shape[0]):
        h = _to_span_layout(x[b])
        for l in range(DEPTH):
            h = _layer(h, mem[b], biases, p, l, norm_final)
        outs.append(h)
    return jnp.stack(outs)
```

```python
import functools
import math

import jax
import jax.numpy as jnp
import numpy as np
from jax import lax
from jax.experimental import pallas as pl
from jax.experimental.pallas import tpu as pltpu
from jax.experimental.pallas import tpu_sc as plsc

F32 = jnp.float32
BF16 = jnp.bfloat16

D_MODEL = 1024
DEPTH = 2
EPS = 1e-5
NEG_INF = -1e30
LOG2E = math.log2(math.e)
H_A = 8
HD_A = 64
D_A = H_A * HD_A
WIN_DIL = ((128, 1), (512, 4), (2048, 16))
BLK = 128
D_B = D_MODEL - D_A
S5_CH = 16
S5_G = D_B // S5_CH
S5_P = 64
D_IN = 3 * D_A + D_B
NUM_BUCKETS = 32
REL_MAX_DIST = 2048
H_X = 4
HD_X = 128
D_X = H_X * HD_X
N_EXPERTS = 32
TOP_K = 4
D_FF = D_MODEL
SWIGLU_ALPHA = 1.702
SWIGLU_LIMIT = 7.0

LANES = 128
SUBLANES = 8
NRES = WIN_DIL[-1][1]
SPAN = NRES * BLK
S5_CHUNK = NRES
S5_PAIRS = S5_G // 2
VMEM_LIMIT = 56 * 1024 * 1024

SC_CORES = 2
SC_SUBCORES = 16
SC_ROWS = 64

ROW_TILE = 512
MID_CHAINS = 2
RANK_TILE = 1024
MOE_TILE = 512
MOE_ROW_PATHS = 4
MOE_W1_PARTS = 4
MOE_W2_PARTS = 2


def _params(sem):
    return pltpu.CompilerParams(dimension_semantics=sem, vmem_limit_bytes=VMEM_LIMIT)


def _rms(x, g):
    return x * lax.rsqrt(jnp.mean(x * x, axis=-1, keepdims=True) + EPS) * g


def _dot(a, b):
    return jnp.dot(a, b, preferred_element_type=F32)


def _dot_nt(a, b):
    return lax.dot_general(a, b, (((1,), (1,)), ((), ())), preferred_element_type=F32)


def _full(a):
    return pl.BlockSpec(a.shape, lambda *_: (0,) * a.ndim)


def _pack_rows(x):
    c = x.shape[1] // 2
    lo = lax.bitcast_convert_type(x[:, :c].astype(BF16).astype(F32), jnp.uint32)
    hi = lax.bitcast_convert_type(x[:, c:].astype(BF16).astype(F32), jnp.uint32)
    return lax.bitcast_convert_type(lax.shift_right_logical(lo, jnp.uint32(16)) | hi, jnp.int32)


def _unpack_rows(p):
    u = lax.bitcast_convert_type(p, jnp.uint32)
    lo = lax.bitcast_convert_type(lax.shift_left(u, jnp.uint32(16)), F32)
    hi = lax.bitcast_convert_type(u & jnp.uint32(0xFFFF0000), F32)
    return lo, hi


def _to_span_layout(x):
    s = x.shape[0]
    return x.reshape(s // SPAN, BLK, NRES, -1).transpose(0, 2, 1, 3).reshape(s, -1)


def _from_span_layout(x):
    s = x.shape[0]
    return x.reshape(s // SPAN, NRES, BLK, -1).transpose(0, 2, 1, 3).reshape(s, -1)


def _inproj_body(h_ref, g_ref, w_ref, o_ref):
    xn = _rms(h_ref[...], g_ref[...]).astype(BF16)
    o_ref[...] = _dot(xn, w_ref[...]).astype(BF16)


def _inproj(h, g, w):
    s, d = h.shape
    n = w.shape[1]
    tm = min(ROW_TILE, s)
    return pl.pallas_call(
        _inproj_body,
        out_shape=jax.ShapeDtypeStruct((s, n), BF16),
        grid=(s // tm,),
        in_specs=[pl.BlockSpec((tm, d), lambda i: (i, 0)),
                  pl.BlockSpec((1, d), lambda i: (0, 0)),
                  pl.BlockSpec((d, n), lambda i: (0, 0))],
        out_specs=pl.BlockSpec((tm, n), lambda i: (i, 0)),
        compiler_params=_params(("parallel",)),
        name="inproj",
    )(h, g.reshape(1, d), w)


def _combine_inproj_body(h_ref, y0_ref, y1_ref, y2_ref, y3_ref, gate_ref, g_ref, w_ref, h_out, o_ref):
    h = _combined(h_ref, (y0_ref, y1_ref, y2_ref, y3_ref), gate_ref)
    h_out[...] = h
    o_ref[...] = _dot(_rms(h, g_ref[...]).astype(BF16), w_ref[...]).astype(BF16)


def _combine_inproj(h, y, gates, g, w):
    s, d = h.shape
    n = w.shape[1]
    tm = min(ROW_TILE, s)
    nt = s // tm
    return pl.pallas_call(
        _combine_inproj_body,
        out_shape=[jax.ShapeDtypeStruct((s, d), F32), jax.ShapeDtypeStruct((s, n), BF16)],
        grid=(nt,),
        in_specs=[pl.BlockSpec((tm, d), lambda i: (i, 0))]
                 + [pl.BlockSpec((tm, d // 2), lambda i, k=k: (k * nt + i, 0)) for k in range(TOP_K)]
                 + [pl.BlockSpec((tm, TOP_K), lambda i: (i, 0)), pl.BlockSpec((1, d), lambda i: (0, 0)),
                    pl.BlockSpec((d, n), lambda i: (0, 0))],
        out_specs=[pl.BlockSpec((tm, d), lambda i: (i, 0)), pl.BlockSpec((tm, n), lambda i: (i, 0))],
        compiler_params=_params(("parallel",)),
        name="combine_inproj",
    )(h, y, y, y, y, gates, g.reshape(1, d), w)


def _t5_bucket(n):
    max_exact = NUM_BUCKETS // 2
    nf = jnp.maximum(n, 1).astype(F32)
    large = max_exact + (jnp.log(nf / max_exact) / math.log(REL_MAX_DIST / max_exact)
                         * (NUM_BUCKETS - max_exact)).astype(jnp.int32)
    large = jnp.minimum(large, NUM_BUCKETS - 1)
    return jnp.where(n < max_exact, n, large)


def _attn_bias(rel_bias, window, dil, perm):
    steps = window // dil
    perm = jnp.asarray(perm, jnp.int32)
    qi = perm[:, None]
    ki = jnp.concatenate([perm, BLK + perm])[None, :]
    dist = BLK + qi - ki
    in_win = (dist >= 0) & (dist <= steps)
    bucket = _t5_bucket(jnp.clip(dist, 0, steps) * dil)
    onehot = (bucket[:, :, None] == jnp.arange(NUM_BUCKETS, dtype=jnp.int32)).astype(F32)
    bias = jnp.einsum('qkb,bh->hqk', onehot, rel_bias.astype(F32), precision=lax.Precision.HIGHEST)
    bias = jnp.where(in_win[None], bias * LOG2E, NEG_INF)
    return bias.reshape(H_A // 2, 2 * BLK, 2 * BLK)


_PERM_D1 = [NRES * jl + r for r in range(NRES) for jl in range(BLK // NRES)]
_PERM_D4 = [4 * jl + i for i in range(4) for jl in range(BLK // 4)]
_PERM_D16 = list(range(BLK))


def _attn_body(q_ref, k_ref, v_ref, kp_ref, vp_ref, b1_ref, b4_ref, b16_ref, o_ref, acc, mst, lst):
    has_prev = pl.program_id(0) > 0
    lane = lax.broadcasted_iota(jnp.int32, (1, LANES), 1)
    lo = lane < HD_A
    mlo = lo.astype(BF16)
    mhi = (~lo).astype(BF16)
    col = lax.broadcasted_iota(jnp.int32, (2 * BLK, 2 * BLK), 1)
    ones = jnp.ones((2 * BLK, LANES), BF16)

    def tile(q2, kk, vv, bias, mask_prev):
        qs = jnp.concatenate([q2 * mlo, q2 * mhi], axis=0)
        s = _dot_nt(qs, kk) + bias
        if mask_prev:
            s = jnp.where(jnp.logical_or(has_prev, col >= BLK), s, NEG_INF)
        m = jnp.max(s, axis=-1, keepdims=True)
        e = jnp.exp2((s - m).astype(BF16))
        oa = _dot(e, jnp.concatenate([vv, ones], axis=1))
        o = oa[:, :LANES]
        l = oa[:, LANES:]
        return (jnp.where(lo, m[:BLK], m[BLK:]), jnp.where(lo, l[:BLK], l[BLK:]),
                jnp.where(lo, o[:BLK], o[BLK:]))

    def merge(prev, cur):
        mp, lp, ap = prev
        mc, lc, ac = cur
        mn = jnp.maximum(mp, mc)
        a = jnp.exp2(mp - mn)
        b = jnp.exp2(mc - mn)
        return mn, a * lp + b * lc, a * ap + b * ac

    def cat(xs):
        return jnp.concatenate(xs, axis=0)


    for r in range(NRES):
        rows = pl.ds(r * BLK, BLK)
        for hp in range(H_A // 2):
            lanes = slice(hp * LANES, (hp + 1) * LANES)
            kk = cat([kp_ref[rows, lanes], k_ref[rows, lanes]])
            vv = cat([vp_ref[rows, lanes], v_ref[rows, lanes]])
            m2, l2, o2 = tile(q_ref[rows, lanes], kk, vv, b16_ref[hp], True)
            mst[rows, lanes] = m2
            lst[rows, lanes] = l2
            acc[rows, lanes] = o2

    for r4 in range(4):
        for b in range(4):
            def chunk_rows(bb):
                return [pl.ds(4 * BLK * i + BLK * r4 + 32 * bb, 32) for i in range(4)]
            rows = chunk_rows(b)
            prows = chunk_rows(3 if b == 0 else b - 1)
            kprev, vprev = (kp_ref, vp_ref) if b == 0 else (k_ref, v_ref)
            for hp in range(H_A // 2):
                lanes = slice(hp * LANES, (hp + 1) * LANES)
                q2 = cat([q_ref[rr, lanes] for rr in rows])
                kk = cat([kprev[rr, lanes] for rr in prows] + [k_ref[rr, lanes] for rr in rows])
                vv = cat([vprev[rr, lanes] for rr in prows] + [v_ref[rr, lanes] for rr in rows])
                cur = tile(q2, kk, vv, b4_ref[hp], b == 0)
                prev = (cat([mst[rr, lanes] for rr in rows]), cat([lst[rr, lanes] for rr in rows]),
                        cat([acc[rr, lanes] for rr in rows]))
                mn, ln, an = merge(prev, cur)
                for i, rr in enumerate(rows):
                    part = slice(32 * i, 32 * (i + 1))
                    mst[rr, lanes] = mn[part]
                    lst[rr, lanes] = ln[part]
                    acc[rr, lanes] = an[part]

    def d1_pair(ap, kprev, vprev, prev_ap, mask_prev):
        def tiles(a_):
            return [pl.ds(BLK * r + 16 * a_, 16) for r in range(NRES)]
        cur_t = tiles(ap)
        prev_t = tiles(prev_ap)

        def halves(ref, ts, lanes):
            xs = [ref[t, lanes].astype(F32) for t in ts]
            return cat([x[:8] for x in xs]).astype(BF16), cat([x[8:] for x in xs]).astype(BF16)

        for hp in range(H_A // 2):
            lanes = slice(hp * LANES, (hp + 1) * LANES)
            q_e, q_o = halves(q_ref, cur_t, lanes)
            k_e, k_o = halves(k_ref, cur_t, lanes)
            v_e, v_o = halves(v_ref, cur_t, lanes)
            _, k_p = halves(kprev, prev_t, lanes)
            _, v_p = halves(vprev, prev_t, lanes)
            cur_e = tile(q_e, cat([k_p, k_e]), cat([v_p, v_e]), b1_ref[hp], mask_prev)
            cur_o = tile(q_o, cat([k_e, k_o]), cat([v_e, v_o]), b1_ref[hp], False)
            ms = [mst[t, lanes] for t in cur_t]
            ls = [lst[t, lanes] for t in cur_t]
            ac = [acc[t, lanes] for t in cur_t]
            outs = []
            for half, cur in ((0, cur_e), (1, cur_o)):
                part = slice(8 * half, 8 * half + 8)
                prev = (cat([x[part] for x in ms]), cat([x[part] for x in ls]), cat([x[part] for x in ac]))
                _, ln, an = merge(prev, cur)
                outs.append(an / ln)
            for r, t in enumerate(cur_t):
                part = slice(8 * r, 8 * r + 8)
                o_ref[t, lanes] = cat([outs[0][part], outs[1][part]]).astype(o_ref.dtype)

    d1_pair(0, kp_ref, vp_ref, BLK // 16 - 1, True)
    for ap in range(1, BLK // 16):
        d1_pair(ap, k_ref, v_ref, ap - 1, False)


def _dilated_attention(proj, biases):
    s = proj.shape[0]
    cur = lambda which: pl.BlockSpec((SPAN, D_A), lambda c: (c, which))
    prev = lambda which: pl.BlockSpec((SPAN, D_A), lambda c: (jnp.maximum(c - 1, 0), which))
    return pl.pallas_call(
        _attn_body,
        out_shape=jax.ShapeDtypeStruct((s, D_A), BF16),
        grid=(s // SPAN,),
        in_specs=[cur(0), cur(1), cur(2), prev(1), prev(2)] + [_full(b) for b in biases],
        out_specs=pl.BlockSpec((SPAN, D_A), lambda c: (c, 0)),
        scratch_shapes=[pltpu.VMEM((SPAN, D_A), F32)] * 3,
        compiler_params=_params(("arbitrary",)),
        name="attn",
    )(proj, proj, proj, proj, proj, *biases)


def _s5_operators(a_re, a_im, b_re, b_im, c_re, c_im, log_dt, d_skip):
    L = S5_CHUNK
    lam = lax.complex(a_re.astype(F32), a_im.astype(F32))
    dt = jnp.exp(log_dt.astype(F32))[:, None]
    a_bar = jnp.exp(lam * dt)
    b_bar = ((a_bar - 1.0) / lam)[..., None] * lax.complex(b_re.astype(F32), b_im.astype(F32))
    c = lax.complex(c_re.astype(F32), c_im.astype(F32))
    j = jnp.arange(L + 1, dtype=F32)
    log_a = lam * dt
    apow = jnp.exp(log_a[None] * j[:, None, None])
    kt = jnp.einsum('gdp,jgp,gpc->gcjd', c, apow[:L], b_bar).real
    skip = d_skip.astype(F32).reshape(S5_G, S5_CH, 1, 1) * jnp.eye(S5_CH)[None, :, None, :]
    kt = (kt + skip * (jnp.arange(L) == 0)[None, None, :, None]).reshape(S5_G, S5_CH, L * S5_CH)
    p = jnp.einsum('sgp,gpc->gscp', apow[:L][::-1], b_bar).reshape(S5_G, L * S5_CH, S5_P)
    ca = jnp.einsum('gdp,tgp->gptd', c, apow[1:L + 1]).reshape(S5_G, S5_P, L * S5_CH)
    a_l = apow[L]

    def pair_blocks(x):
        g, r, w = x.shape
        x = x.reshape(S5_PAIRS, 2, r, w)
        z = jnp.zeros_like(x[:, 0])
        top = jnp.concatenate([x[:, 0], z], axis=-1)
        bot = jnp.concatenate([z, x[:, 1]], axis=-1)
        return jnp.concatenate([top, bot], axis=1)

    p2 = jnp.concatenate([pair_blocks(p.real), pair_blocks(p.imag)], axis=-1)
    q2 = jnp.concatenate([pair_blocks(ca.real), pair_blocks(-ca.imag)], axis=1)
    a_lr = a_l.real.reshape(1, S5_G * S5_P)
    a_li = a_l.imag.reshape(1, S5_G * S5_P)
    return kt, p2.astype(BF16), q2.astype(BF16), a_lr, a_li


def _s5_body(u_ref, kt_ref, p_ref, q_ref, ar_ref, ai_ref, o_ref, m_ref, w_ref, y_ref,
             ere, eim, xre, xim, sre, sim):
    rows = BLK
    gw = S5_CHUNK * S5_CH
    pw = 2 * gw
    per_vreg = LANES // S5_CH
    chunk_of_lane = lax.broadcasted_iota(jnp.int32, (rows, LANES), 1) // S5_CH

    def chunk_transpose(arrs):
        arrs = list(arrs)
        for s in (4, 2, 1):
            upper = (chunk_of_lane & s) != 0
            nxt = list(arrs)
            for i in range(per_vreg):
                if i & s:
                    continue
                lo_a, hi_a = arrs[i], arrs[i + s]
                nxt[i] = jnp.where(upper, pltpu.roll(hi_a, s * S5_CH, axis=1), lo_a)
                nxt[i + s] = jnp.where(upper, hi_a, pltpu.roll(lo_a, LANES - s * S5_CH, axis=1))
            arrs = nxt
        return arrs

    @pl.when(pl.program_id(0) == 0)
    def _():
        sre[...] = jnp.zeros_like(sre)
        sim[...] = jnp.zeros_like(sim)
        lane = lax.broadcasted_iota(jnp.int32, (S5_CH, gw), 1)

        def build(g, _):
            kt = kt_ref[g]
            for s in range(S5_CHUNK):
                blk = kt if s == 0 else jnp.where(lane >= s * S5_CH, pltpu.roll(kt, s * S5_CH, axis=1), 0.0)
                m_ref[g, s * S5_CH:(s + 1) * S5_CH, :] = blk.astype(BF16)
            return 0

        lax.fori_loop(0, S5_G, build, 0)

    for b in range(D_B // LANES):
        for a in range(S5_CHUNK // per_vreg):
            srcs = [u_ref[(per_vreg * a + i) * rows:(per_vreg * a + i + 1) * rows,
                          b * LANES:(b + 1) * LANES].astype(F32) for i in range(per_vreg)]
            for gi, arr in enumerate(chunk_transpose(srcs)):
                g = per_vreg * b + gi
                w_ref[:, g * gw + a * LANES:g * gw + (a + 1) * LANES] = arr.astype(BF16)

    for pr in range(S5_PAIRS):
        e = _dot(w_ref[:, pr * pw:(pr + 1) * pw], p_ref[pr])
        ere[:, pr * LANES:(pr + 1) * LANES] = e[:, :LANES]
        eim[:, pr * LANES:(pr + 1) * LANES] = e[:, LANES:]

    ar = ar_ref[...]
    ai = ai_ref[...]

    def step(n, carry):
        xr, xi = carry
        xre[pl.ds(n, 1), :] = xr
        xim[pl.ds(n, 1), :] = xi
        nr = ar * xr - ai * xi + ere[pl.ds(n, 1), :]
        ni = ar * xi + ai * xr + eim[pl.ds(n, 1), :]
        return nr, ni

    xr, xi = lax.fori_loop(0, rows, step, (sre[...], sim[...]))
    sre[...] = xr
    sim[...] = xi

    for pr in range(S5_PAIRS):
        xin = jnp.concatenate([xre[:, pr * LANES:(pr + 1) * LANES],
                               xim[:, pr * LANES:(pr + 1) * LANES]], axis=-1).astype(BF16)
        yc = _dot(xin, q_ref[pr])
        for half in range(2):
            g = 2 * pr + half
            cols = slice(g * gw, (g + 1) * gw)
            y = _dot(w_ref[:, cols], m_ref[g]) + yc[:, half * gw:(half + 1) * gw]
            y_ref[:, cols] = 0.5 * y * (1.0 + lax.erf(y * (2.0 ** -0.5)))

    for b in range(D_B // LANES):
        for a in range(S5_CHUNK // per_vreg):
            srcs = [y_ref[:, (per_vreg * b + i) * gw + a * LANES:(per_vreg * b + i) * gw + (a + 1) * LANES]
                    for i in range(per_vreg)]
            for ri, arr in enumerate(chunk_transpose(srcs)):
                r = per_vreg * a + ri
                o_ref[r * rows:(r + 1) * rows, b * LANES:(b + 1) * LANES] = arr.astype(BF16)


def _s5_core(proj, ops):
    kt, p2, q2, a_lr, a_li = ops
    s = proj.shape[0]
    gw = S5_CHUNK * S5_CH
    wide = S5_G * gw
    nstate = S5_G * S5_P
    return pl.pallas_call(
        _s5_body,
        out_shape=jax.ShapeDtypeStruct((s, D_B), BF16),
        grid=(s // SPAN,),
        in_specs=[pl.BlockSpec((SPAN, D_B), lambda i: (i, 3 * D_A // D_B)),
                  _full(kt), _full(p2), _full(q2), _full(a_lr), _full(a_li)],
        out_specs=pl.BlockSpec((SPAN, D_B), lambda i: (i, 0)),
        scratch_shapes=[pltpu.VMEM((S5_G, gw, gw), BF16), pltpu.VMEM((BLK, wide), BF16),
                        pltpu.VMEM((BLK, wide), F32)]
                       + [pltpu.VMEM((BLK, nstate), F32)] * 4 + [pltpu.VMEM((1, nstate), F32)] * 2,
        compiler_params=_params(("arbitrary",)),
        name="s5",
    )(proj, kt, p2, q2, a_lr, a_li)


def _split_bf16(x):
    hi = x.astype(BF16)
    lo = (x - hi.astype(F32)).astype(BF16)
    return hi, lo


def _mid_body(h_ref, ya_ref, yb_ref, wglu_ref, bglu_ref, ga_ref, gb_ref, wout_ref, gx_ref, wq_ref,
              k_ref, v_ref, wo_ref, gm_ref, wr_ref, br_ref, h_out, xn_out, logit_out):
    part = h_ref.shape[0] // MID_CHAINS
    for c in range(MID_CHAINS):
        rows = slice(c * part, (c + 1) * part)
        outs = _mid_rows(h_ref[rows, :], ya_ref[rows, :], yb_ref[rows, :], wglu_ref, bglu_ref, ga_ref, gb_ref,
                         wout_ref, gx_ref, wq_ref, k_ref, v_ref, wo_ref, gm_ref, wr_ref, br_ref)
        for ref, val in zip((h_out, xn_out, logit_out), outs):
            ref[rows, :] = val


def _mid_rows(h, ya, yb, wglu_ref, bglu_ref, ga_ref, gb_ref, wout_ref, gx_ref, wq_ref,
              k_ref, v_ref, wo_ref, gm_ref, wr_ref, br_ref):
    gate = jax.nn.sigmoid(_dot(yb, wglu_ref[...]) + bglu_ref[...])
    yb2 = yb.astype(F32) * gate
    na = _rms(ya.astype(F32), ga_ref[...]).astype(BF16)
    nb = _rms(yb2, gb_ref[...]).astype(BF16)
    h1 = h + _dot(na, wout_ref[0:D_A, :]) + _dot(nb, wout_ref[D_A:D_MODEL, :])
    q = _dot(_rms(h1, gx_ref[...]).astype(BF16), wq_ref[...]).astype(BF16)
    heads = []
    for hd in range(H_X):
        lanes = slice(hd * HD_X, (hd + 1) * HD_X)
        s = _dot_nt(q[:, lanes], k_ref[:, lanes])
        e = jnp.exp(s - jnp.max(s, axis=-1, keepdims=True))
        heads.append(_dot(e.astype(BF16), v_ref[:, lanes]) / jnp.sum(e, axis=-1, keepdims=True))
    o = jnp.concatenate(heads, axis=-1).astype(BF16)
    h2 = h1 + _dot(o, wo_ref[...])
    xn = _rms(h2, gm_ref[...])
    x_hi, x_lo = _split_bf16(xn)
    part = _dot(x_hi, wr_ref[...]) + _dot(x_lo, wr_ref[...])
    logits = part + pltpu.roll(part, LANES - N_EXPERTS, axis=1) + br_ref[...]
    return h2, _pack_rows(xn), logits


def _mid(h, ya, yb, wglu, bglu, ga, gb, wout, gx, wq, kmem, vmem, wo, gm, wr2, br):
    s = h.shape[0]
    tm = min(ROW_TILE, s)
    row = lambda w: pl.BlockSpec((tm, w), lambda i: (i, 0))
    consts = [wglu, bglu, ga, gb, wout, gx, wq, kmem, vmem, wo, gm, wr2, br]
    return pl.pallas_call(
        _mid_body,
        out_shape=[jax.ShapeDtypeStruct((s, D_MODEL), F32), jax.ShapeDtypeStruct((s, D_MODEL // 2), jnp.int32),
                   jax.ShapeDtypeStruct((s, LANES), F32)],
        grid=(s // tm,),
        in_specs=[row(D_MODEL), row(D_A), row(D_B)] + [_full(a) for a in consts],
        out_specs=[row(D_MODEL), row(D_MODEL // 2), row(LANES)],
        compiler_params=_params(("parallel",)),
        name="mid",
    )(h, ya, yb, *consts)


def _memkv_body(mem_ref, g_ref, w_ref, o_ref):
    o_ref[...] = _dot(_rms(mem_ref[...], g_ref[...]).astype(BF16), w_ref[...]).astype(BF16)


def _memkv(mem, g, w):
    n, d = mem.shape
    return pl.pallas_call(
        _memkv_body,
        out_shape=jax.ShapeDtypeStruct((n, w.shape[1]), BF16),
        compiler_params=pltpu.CompilerParams(vmem_limit_bytes=VMEM_LIMIT),
        name="memkv",
    )(mem, g.reshape(1, d), w)


def _rank_body(logit_ref, tri_ref, idx_ref, gate_ref, rank_ref, cnt_ref, carry):
    @pl.when(pl.program_id(0) == 0)
    def _():
        carry[...] = jnp.zeros_like(carry)

    tm = logit_ref.shape[0]
    logits = jnp.transpose(logit_ref[...])[:N_EXPERTS, :]
    expert = lax.broadcasted_iota(jnp.int32, (N_EXPERTS, tm), 0)
    vals, idxs = [], []
    for _ in range(TOP_K):
        mx = jnp.max(logits, axis=0, keepdims=True)
        ix = jnp.min(jnp.where(logits == mx, expert, N_EXPERTS), axis=0, keepdims=True)
        vals.append(mx)
        idxs.append(ix)
        logits = jnp.where(expert == ix, -jnp.inf, logits)
    es = [jnp.exp(v - vals[0]) for v in vals]
    den = es[0] + es[1] + es[2] + es[3]
    hits = [expert == ix for ix in idxs]
    onehot = jnp.zeros((N_EXPERTS, tm), F32)
    for hit in hits:
        onehot = onehot + jnp.where(hit, 1.0, 0.0)
    nb = tm // LANES
    blocks = jnp.concatenate([onehot[:, b * LANES:(b + 1) * LANES] for b in range(nb)], axis=0)
    inc = _dot(blocks.astype(BF16), tri_ref[...])
    run = carry[...][:, 0:1]
    before = []
    for b in range(nb):
        inc_b = inc[b * N_EXPERTS:(b + 1) * N_EXPERTS, :]
        before.append(run + inc_b - onehot[:, b * LANES:(b + 1) * LANES])
        run = run + inc_b[:, LANES - 1:LANES]
    before = jnp.concatenate(before, axis=1)
    choice = lax.broadcasted_iota(jnp.int32, (SUBLANES, tm), 0)
    idx_t = jnp.full((SUBLANES, tm), N_EXPERTS, jnp.int32)
    gate_t = jnp.zeros((SUBLANES, tm), F32)
    rank_t = jnp.zeros((SUBLANES, tm), jnp.int32)
    for k in range(TOP_K):
        rk = jnp.sum(jnp.where(hits[k], before, 0.0), axis=0, keepdims=True)
        idx_t = jnp.where(choice == k, idxs[k], idx_t)
        gate_t = jnp.where(choice == k, es[k] / den, gate_t)
        rank_t = jnp.where(choice == k, rk.astype(jnp.int32), rank_t)
    idx_ref[...] = idx_t
    gate_ref[...] = gate_t
    rank_ref[...] = rank_t
    total = jnp.broadcast_to(run, carry.shape)
    carry[...] = total
    cnt_ref[...] = total.astype(jnp.int32)


def _rank(logits):
    t = logits.shape[0]
    tm = min(RANK_TILE, t)
    cols = pl.BlockSpec((SUBLANES, tm), lambda i: (0, i))
    tri = jnp.asarray(np.triu(np.ones((LANES, LANES), np.float32)), BF16)
    return pl.pallas_call(
        _rank_body,
        out_shape=[jax.ShapeDtypeStruct((SUBLANES, t), jnp.int32), jax.ShapeDtypeStruct((SUBLANES, t), F32),
                   jax.ShapeDtypeStruct((SUBLANES, t), jnp.int32),
                   jax.ShapeDtypeStruct((N_EXPERTS, LANES), jnp.int32)],
        grid=(t // tm,),
        in_specs=[pl.BlockSpec((tm, LANES), lambda i: (i, 0)), _full(tri)],
        out_specs=[cols, cols, cols, pl.BlockSpec((N_EXPERTS, LANES), lambda i: (0, 0))],
        scratch_shapes=[pltpu.VMEM((N_EXPERTS, LANES), F32)],
        compiler_params=_params(("arbitrary",)),
        name="rank",
    )(logits, tri)


def _route(logits, tm):
    t = logits.shape[0]
    tk = t * TOP_K
    idx, gates, rank, cnt = _rank(logits)
    counts = cnt[:, 0]
    padded = (counts + tm - 1) // tm * tm
    pend = jnp.cumsum(padded)
    pstart = pend - padded
    n_rows = tk + N_EXPERTS * tm
    n_tiles = n_rows // tm
    experts = jnp.arange(N_EXPERTS, dtype=jnp.int32)
    tile_first = jnp.arange(n_tiles, dtype=jnp.int32) * tm
    last_used = jnp.max(jnp.where(padded > 0, experts, 0))
    tile_e = jnp.minimum(jnp.sum(tile_first[:, None] >= pend[None, :], axis=1), last_used).astype(jnp.int32)
    tile_rows = jnp.clip(jnp.sum(jnp.where(tile_e[:, None] == experts[None, :],
                                           (pstart + counts)[None, :], 0), axis=1) - tile_first, 0, tm)
    tile_rows = jnp.where(tile_first < pend[-1], tile_rows, 0).astype(jnp.int32)
    group = jnp.cumsum(jnp.concatenate([jnp.zeros((1,), jnp.int32),
                                        (tile_e[1:] != tile_e[:-1]).astype(jnp.int32)]))
    tile_slot = (group % 2).astype(jnp.int32)
    later = (experts[None, :] > experts[:, None]) & (padded > 0)[None, :]
    next_e = jnp.min(jnp.where(later, experts[None, :], N_EXPERTS), axis=1)
    next_e = jnp.where(next_e < N_EXPERTS, next_e, -1).astype(jnp.int32)
    tile_next = jnp.sum(jnp.where(tile_e[:, None] == experts[None, :], next_e[None, :], 0), axis=1).astype(jnp.int32)
    base = jnp.sum(jnp.where(idx[:TOP_K, :, None] == experts, pstart, 0), axis=-1)
    dest = rank[:TOP_K] + base
    return gates[:TOP_K].T, dest, n_rows, tile_e, tile_rows, tile_slot, tile_next


def _sc_mesh():
    return plsc.VectorSubcoreMesh(core_axis_name="c", subcore_axis_name="s",
                                  num_cores=SC_CORES, num_subcores=SC_SUBCORES)


def _sc_worker():
    return lax.axis_index("s") * SC_CORES + lax.axis_index("c")


def _scatter_rows(x, dest, n_rows):
    t, d = x.shape
    per_worker = t // (SC_CORES * SC_SUBCORES)
    chunks = per_worker // SC_ROWS

    assert chunks % 2 == 0

    @functools.partial(
        pl.kernel, mesh=_sc_mesh(),
        out_type=jax.ShapeDtypeStruct((n_rows, d), x.dtype),
        scratch_types=[pltpu.VMEM((SC_ROWS, d), x.dtype)] * 2 + [pltpu.VMEM((SC_ROWS,), jnp.int32)] * (2 * TOP_K)
                      + [pltpu.SemaphoreType.DMA((2,)), pltpu.SemaphoreType.DMA((2, TOP_K))],
    )
    def scatter(x_hbm, *rest):
        dest_hbm, out_hbm = rest[:TOP_K], rest[TOP_K]
        rows_v = rest[TOP_K + 1:TOP_K + 3]
        idx_v = (rest[TOP_K + 3:2 * TOP_K + 3], rest[2 * TOP_K + 3:3 * TOP_K + 3])
        lsem, ssem = rest[3 * TOP_K + 3], rest[3 * TOP_K + 4]
        base = _sc_worker() * per_worker

        def rows_at(c):
            return pl.ds(pl.multiple_of(base + c * SC_ROWS, SC_ROWS), SC_ROWS)

        def loaded(c, s):
            return pltpu.make_async_copy(x_hbm.at[rows_at(c)], rows_v[s], lsem.at[s])

        def load(c, s):
            loaded(c, s).start()
            for k in range(TOP_K):
                pltpu.sync_copy(dest_hbm[k].at[rows_at(c)], idx_v[s][k])

        def scattered(s):
            return [pltpu.make_async_copy(rows_v[s], out_hbm.at[idx_v[s][k]], ssem.at[s, k]) for k in range(TOP_K)]

        load(0, 0)

        @pl.loop(0, chunks, step=2)
        def _(c0):
            for s in range(2):
                c = c0 + s
                loaded(c, s).wait()
                for cp in scattered(s):
                    cp.start()

                @pl.when(c >= 1)
                def _():
                    for cp in scattered(1 - s):
                        cp.wait()

                @pl.when(c + 1 < chunks)
                def _():
                    load(c + 1, 1 - s)

        for cp in scattered(1):
            cp.wait()

    return scatter(x, *[dest[k] for k in range(TOP_K)])


def _gather_rows(table, idx):
    n, d = table.shape
    b = idx.shape[0]
    per_worker = b // (SC_CORES * SC_SUBCORES)
    chunks = per_worker // SC_ROWS
    assert chunks % 2 == 0

    @functools.partial(
        pl.kernel, mesh=_sc_mesh(),
        out_type=jax.ShapeDtypeStruct((b, d), table.dtype),
        scratch_types=[pltpu.VMEM((chunks, SC_ROWS), jnp.int32)] + [pltpu.VMEM((SC_ROWS, d), table.dtype)] * 2
                      + [pltpu.SemaphoreType.DMA((2,)), pltpu.SemaphoreType.DMA((2,))],
    )
    def gather(table_hbm, idx_hbm, out_hbm, idx_v, rows0, rows1, gsem, wsem):
        rows_v = (rows0, rows1)
        worker = _sc_worker()
        base = worker * per_worker
        pltpu.sync_copy(idx_hbm.at[pl.ds(pl.multiple_of(worker * chunks, chunks), chunks)], idx_v)

        def rows_at(c):
            return pl.ds(pl.multiple_of(base + c * SC_ROWS, SC_ROWS), SC_ROWS)

        def fetched(c, s):
            return pltpu.make_async_copy(table_hbm.at[idx_v.at[c]], rows_v[s], gsem.at[s])

        def written(c, s):
            return pltpu.make_async_copy(rows_v[s], out_hbm.at[rows_at(c)], wsem.at[s])

        fetched(0, 0).start()

        @pl.loop(0, chunks, step=2)
        def _(c0):
            for s in range(2):
                c = c0 + s
                fetched(c, s).wait()
                written(c, s).start()

                @pl.when(c >= 1)
                def _():
                    written(c - 1, 1 - s).wait()

                @pl.when(c + 1 < chunks)
                def _():
                    fetched(c + 1, 1 - s).start()

        written(chunks - 1, 1).wait()

    return gather(table, idx.reshape(b // SC_ROWS, SC_ROWS))


def _moe_body(te_ref, tv_ref, sl_ref, nx_ref, x_ref, w1_hbm, b1_ref, w2_hbm, b2_ref, o_ref,
              w1f, w2f, w1b, w2b, sem, *, layer):
    i = pl.program_id(0)
    e = te_ref[i]
    slot = sl_ref[i]
    new_expert = (i == 0) | (e != te_ref[jnp.maximum(i - 1, 0)])

    def weight_copies(expert, s):
        rows1 = w1f.shape[1] // MOE_W1_PARTS
        rows2 = w2f.shape[1] // MOE_W2_PARTS
        c1 = [pltpu.make_async_copy(w1_hbm.at[layer, expert, pl.ds(q * rows1, rows1)],
                                    w1f.at[s, pl.ds(q * rows1, rows1)], sem.at[s, q])
              for q in range(MOE_W1_PARTS)]
        c2 = [pltpu.make_async_copy(w2_hbm.at[layer, expert, pl.ds(q * rows2, rows2)],
                                    w2f.at[s, pl.ds(q * rows2, rows2)], sem.at[s, MOE_W1_PARTS + q])
              for q in range(MOE_W2_PARTS)]
        return c1 + c2

    @pl.when(i == 0)
    def _():
        for c in weight_copies(e, slot):
            c.start()

    @pl.when(new_expert)
    def _():
        for c in weight_copies(e, slot):
            c.wait()
        nxt = nx_ref[i]

        @pl.when(nxt >= 0)
        def _():
            for c in weight_copies(nxt, 1 - slot):
                c.start()

        w1b[...] = w1f[slot].astype(BF16)
        w2b[...] = w2f[slot].astype(BF16)

    def expert(rows):
        row = lax.broadcasted_iota(jnp.int32, (rows, x_ref.shape[1]), 0)
        lo, hi = _unpack_rows(jnp.where(row < tv_ref[i], x_ref[0:rows, :], 0))
        x = jnp.concatenate([lo, hi], axis=-1).astype(BF16)
        hb = _dot(x, w1b[...]) + b1_ref[0]
        x_glu = jnp.minimum(hb[:, :D_FF], SWIGLU_LIMIT)
        x_lin = jnp.clip(hb[:, D_FF:], -SWIGLU_LIMIT, SWIGLU_LIMIT)
        act = x_glu * jax.nn.sigmoid(SWIGLU_ALPHA * x_glu) * (x_lin + 1.0)
        o_ref[0:rows, :] = _pack_rows(_dot(act.astype(BF16), w2b[...]) + b2_ref[0])

    step = x_ref.shape[0] // MOE_ROW_PATHS
    for path in range(1, MOE_ROW_PATHS + 1):
        rows = path * step

        @pl.when((tv_ref[i] > rows - step) & (tv_ref[i] <= rows))
        def _(rows=rows):
            expert(rows)
            if rows < x_ref.shape[0]:
                o_ref[rows:, :] = jnp.zeros((x_ref.shape[0] - rows, o_ref.shape[1]), o_ref.dtype)

    @pl.when(tv_ref[i] == 0)
    def _():
        o_ref[...] = jnp.zeros_like(o_ref)


def _moe_experts(xs, tile_e, tile_rows, tile_slot, tile_next, w1, b1, w2, b2, layer):
    n_rows = xs.shape[0]
    tm = MOE_TILE
    nl, ne, d, ff2 = w1.shape
    bias_map = lambda i, te, tv, sl, nx: (layer, te[i], 0, 0)
    grid_spec = pltpu.PrefetchScalarGridSpec(
        num_scalar_prefetch=4,
        grid=(n_rows // tm,),
        in_specs=[pl.BlockSpec((tm, d // 2), lambda i, *_: (i, 0)),
                  pl.BlockSpec(memory_space=pl.ANY),
                  pl.BlockSpec((None, 1, 1, ff2), bias_map),
                  pl.BlockSpec(memory_space=pl.ANY),
                  pl.BlockSpec((None, 1, 1, d), bias_map)],
        out_specs=pl.BlockSpec((tm, d // 2), lambda i, *_: (i, 0)),
        scratch_shapes=[pltpu.VMEM((2, d, ff2), F32), pltpu.VMEM((2, ff2 // 2, d), F32),
                        pltpu.VMEM((d, ff2), BF16), pltpu.VMEM((ff2 // 2, d), BF16),
                        pltpu.SemaphoreType.DMA((2, MOE_W1_PARTS + MOE_W2_PARTS))],
    )
    return pl.pallas_call(
        functools.partial(_moe_body, layer=layer),
        out_shape=jax.ShapeDtypeStruct((n_rows, d // 2), jnp.int32),
        grid_spec=grid_spec,
        compiler_params=_params(("arbitrary",)),
        name="moe",
    )(tile_e, tile_rows, tile_slot, tile_next, xs, w1, b1.reshape(nl, ne, 1, ff2), w2, b2.reshape(nl, ne, 1, d))


def _combined(h_ref, y_refs, gate_ref):
    gates = gate_ref[...]
    lo = jnp.zeros(y_refs[0].shape, F32)
    hi = jnp.zeros(y_refs[0].shape, F32)
    for k, y_ref in enumerate(y_refs):
        yl, yh = _unpack_rows(y_ref[...])
        lo = lo + yl * gates[:, k:k + 1]
        hi = hi + yh * gates[:, k:k + 1]
    return h_ref[...] + jnp.concatenate([lo, hi], axis=-1)


def _combine_final_body(h_ref, y0_ref, y1_ref, y2_ref, y3_ref, gate_ref, g_ref, out_hbm, buf, sem):
    i = pl.program_id(0)
    n = pl.num_programs(0)
    slot = i % 2
    per_step = h_ref.shape[0] // BLK
    steps_per_span = NRES // per_step

    def writes(step, s):
        span = step // steps_per_span
        r0 = (step % steps_per_span) * per_step
        return [pltpu.make_async_copy(buf.at[s, pl.ds(rr * BLK, BLK), :], out_hbm.at[span, :, r0 + rr, :],
                                      sem.at[s, rr]) for rr in range(per_step)]

    @pl.when(i >= 2)
    def _():
        for cp in writes(i - 2, slot):
            cp.wait()

    buf[slot] = _rms(_combined(h_ref, (y0_ref, y1_ref, y2_ref, y3_ref), gate_ref), g_ref[...])
    for cp in writes(i, slot):
        cp.start()

    @pl.when(i == n - 1)
    def _():
        for cp in writes(i, slot):
            cp.wait()

        @pl.when(i >= 1)
        def _():
            for cp in writes(i - 1, 1 - slot):
                cp.wait()


def _combine_final(h, y, gates, g):
    s, d = h.shape
    tm = min(ROW_TILE // 2, s)
    nt = s // tm
    in_specs = ([pl.BlockSpec((tm, d), lambda i: (i, 0))]
                + [pl.BlockSpec((tm, d // 2), lambda i, k=k: (k * nt + i, 0)) for k in range(TOP_K)]
                + [pl.BlockSpec((tm, TOP_K), lambda i: (i, 0))])
    out = pl.pallas_call(
        _combine_final_body,
        out_shape=jax.ShapeDtypeStruct((s // SPAN, BLK, NRES, d), F32),
        grid=(nt,),
        in_specs=in_specs + [pl.BlockSpec((1, d), lambda i: (0, 0))],
        out_specs=pl.BlockSpec(memory_space=pl.ANY),
        scratch_shapes=[pltpu.VMEM((2, tm, d), F32), pltpu.SemaphoreType.DMA((2, tm // BLK))],
        compiler_params=_params(("arbitrary",)),
        name="combine_final",
    )(h, y, y, y, y, gates, g.reshape(1, d))
    return out.reshape(s, d)


def _layer(state, mem, biases, p, l, g_final):
    row = lambda a: a.reshape(1, -1).astype(F32)
    q_scale = jnp.concatenate([jnp.full((D_A,), HD_A ** -0.5 * LOG2E, F32), jnp.ones((D_IN - D_A,), F32)])
    w_in = (p["w_in"][l] * q_scale).astype(BF16)
    if l == 0:
        h, proj = state, _inproj(state, p["norm_mix"][l], w_in)
    else:
        h, proj = _combine_inproj(*state, p["norm_mix"][l], w_in)
    ya = _dilated_attention(proj, biases)
    ops = _s5_operators(p["s5_a_re"][l], p["s5_a_im"][l], p["s5_b_re"][l], p["s5_b_im"][l],
                        p["s5_c_re"][l], p["s5_c_im"][l], p["s5_log_dt"][l], p["s5_d"][l])
    yb = _s5_core(proj, ops)
    kv = _memkv(mem, p["norm_mem"][l], p["w_xkv"][l].astype(BF16))
    wr = p["w_router"][l].astype(F32)
    wr_hi = wr.astype(BF16)
    wr2 = jnp.pad(jnp.concatenate([wr_hi, (wr - wr_hi.astype(F32)).astype(BF16)], axis=1),
                  ((0, 0), (0, LANES - 2 * N_EXPERTS)))
    br = jnp.pad(p["b_router"][l].astype(F32), (0, LANES - N_EXPERTS)).reshape(1, LANES)
    h2, xn, logits = _mid(
        h, ya, yb, p["w_glu"][l].astype(BF16), row(p["b_glu"][l]), row(p["g_out_attn"][l]),
        row(p["g_out_ssm"][l]), p["w_out"][l].astype(BF16), row(p["norm_xattn"][l]),
        (p["w_xq"][l] * (HD_X ** -0.5)).astype(BF16), kv[:, :D_X], kv[:, D_X:],
        p["w_xo"][l].astype(BF16), row(p["norm_moe"][l]), wr2, br)
    gates, dest, n_rows, tile_e, tile_rows, tile_slot, tile_next = _route(logits, MOE_TILE)
    out = _moe_experts(_scatter_rows(xn, dest, n_rows), tile_e, tile_rows, tile_slot, tile_next,
                       p["w1"], p["b1"], p["w2"], p["b2"], l)
    y = _gather_rows(out, dest.reshape(-1))
    return _combine_final(h2, y, gates, g_final) if l == DEPTH - 1 else (h2, y, gates)


def kernel(x, mem, rel_bias, norm_mix, w_in, s5_a_re, s5_a_im, s5_b_re, s5_b_im, s5_c_re, s5_c_im, s5_log_dt, s5_d, w_glu, b_glu, g_out_attn, g_out_ssm, w_out, norm_xattn, norm_mem, w_xq, w_xkv, w_xo, norm_moe, w_router, b_router, w1, b1, w2, b2, norm_final):
    p = dict(norm_mix=norm_mix, w_in=w_in, s5_a_re=s5_a_re, s5_a_im=s5_a_im, s5_b_re=s5_b_re,
             s5_b_im=s5_b_im, s5_c_re=s5_c_re, s5_c_im=s5_c_im, s5_log_dt=s5_log_dt, s5_d=s5_d,
             w_glu=w_glu, b_glu=b_glu, g_out_attn=g_out_attn, g_out_ssm=g_out_ssm, w_out=w_out,
             norm_xattn=norm_xattn, norm_mem=norm_mem, w_xq=w_xq, w_xkv=w_xkv, w_xo=w_xo,
             norm_moe=norm_moe, w_router=w_router, b_router=b_router, w1=w1, b1=b1, w2=w2, b2=b2)
    biases = [_attn_bias(rel_bias, window, dil, perm)
              for (window, dil), perm in zip(WIN_DIL, (_PERM_D1, _PERM_D4, _PERM_D16))]
    outs = []
    for b in range(x.shape[0]):
        h = _to_span_layout(x[b])
        for l in range(DEPTH):
            h = _layer(h, mem[b], biases, p, l, norm_final)
        outs.append(h)
    return jnp.stack(outs)
```

```python
import functools
import math

import jax
import jax.numpy as jnp
import numpy as np
from jax import lax
from jax.experimental import pallas as pl
from jax.experimental.pallas import tpu as pltpu
from jax.experimental.pallas import tpu_sc as plsc

F32 = jnp.float32
BF16 = jnp.bfloat16

D_MODEL = 1024
DEPTH = 2
EPS = 1e-5
NEG_INF = -1e30
LOG2E = math.log2(math.e)
H_A = 8
HD_A = 64
D_A = H_A * HD_A
WIN_DIL = ((128, 1), (512, 4), (2048, 16))
BLK = 128
D_B = D_MODEL - D_A
S5_CH = 16
S5_G = D_B // S5_CH
S5_P = 64
D_IN = 3 * D_A + D_B
NUM_BUCKETS = 32
REL_MAX_DIST = 2048
H_X = 4
HD_X = 128
D_X = H_X * HD_X
N_EXPERTS = 32
TOP_K = 4
D_FF = D_MODEL
SWIGLU_ALPHA = 1.702
SWIGLU_LIMIT = 7.0

LANES = 128
SUBLANES = 8
NRES = WIN_DIL[-1][1]
SPAN = NRES * BLK
S5_CHUNK = NRES
S5_PAIRS = S5_G // 2
VMEM_LIMIT = 56 * 1024 * 1024

SC_CORES = 2
SC_SUBCORES = 16
SC_ROWS = 64

ROW_TILE = 512
ROW_CHAINS = 2
MID_CHAINS = 2
RANK_TILE = 1024
MOE_TILE = 512
MOE_ROW_PATHS = 4
MOE_W1_PARTS = 4
MOE_W2_PARTS = 2


def _params(sem):
    return pltpu.CompilerParams(dimension_semantics=sem, vmem_limit_bytes=VMEM_LIMIT)


def _rms(x, g):
    return x * lax.rsqrt(jnp.mean(x * x, axis=-1, keepdims=True) + EPS) * g


def _dot(a, b):
    return jnp.dot(a, b, preferred_element_type=F32)


def _dot_nt(a, b):
    return lax.dot_general(a, b, (((1,), (1,)), ((), ())), preferred_element_type=F32)


def _full(a):
    return pl.BlockSpec(a.shape, lambda *_: (0,) * a.ndim)


def _pack_rows(x):
    c = x.shape[1] // 2
    lo = lax.bitcast_convert_type(x[:, :c].astype(BF16).astype(F32), jnp.uint32)
    hi = lax.bitcast_convert_type(x[:, c:].astype(BF16).astype(F32), jnp.uint32)
    return lax.bitcast_convert_type(lax.shift_right_logical(lo, jnp.uint32(16)) | hi, jnp.int32)


def _unpack_rows(p):
    u = lax.bitcast_convert_type(p, jnp.uint32)
    lo = lax.bitcast_convert_type(lax.shift_left(u, jnp.uint32(16)), F32)
    hi = lax.bitcast_convert_type(u & jnp.uint32(0xFFFF0000), F32)
    return lo, hi


def _to_span_layout(x):
    s = x.shape[0]
    return x.reshape(s // SPAN, BLK, NRES, -1).transpose(0, 2, 1, 3).reshape(s, -1)


def _from_span_layout(x):
    s = x.shape[0]
    return x.reshape(s // SPAN, NRES, BLK, -1).transpose(0, 2, 1, 3).reshape(s, -1)


def _row_chains(n_rows):
    part = n_rows // ROW_CHAINS
    return [slice(c * part, (c + 1) * part) for c in range(ROW_CHAINS)]


def _inproj_body(h_ref, g_ref, w_ref, o_ref):
    for rows in _row_chains(h_ref.shape[0]):
        xn = _rms(h_ref[rows, :], g_ref[...]).astype(BF16)
        o_ref[rows, :] = _dot(xn, w_ref[...]).astype(BF16)


def _inproj(h, g, w):
    s, d = h.shape
    n = w.shape[1]
    tm = min(ROW_TILE, s)
    return pl.pallas_call(
        _inproj_body,
        out_shape=jax.ShapeDtypeStruct((s, n), BF16),
        grid=(s // tm,),
        in_specs=[pl.BlockSpec((tm, d), lambda i: (i, 0)),
                  pl.BlockSpec((1, d), lambda i: (0, 0)),
                  pl.BlockSpec((d, n), lambda i: (0, 0))],
        out_specs=pl.BlockSpec((tm, n), lambda i: (i, 0)),
        compiler_params=_params(("parallel",)),
        name="inproj",
    )(h, g.reshape(1, d), w)


def _combine_inproj_body(h_ref, y0_ref, y1_ref, y2_ref, y3_ref, gate_ref, g_ref, w_ref, h_out, o_ref):
    for rows in _row_chains(h_ref.shape[0]):
        h = _combined(h_ref.at[rows, :], [y.at[rows, :] for y in (y0_ref, y1_ref, y2_ref, y3_ref)],
                      gate_ref.at[rows, :])
        h_out[rows, :] = h
        o_ref[rows, :] = _dot(_rms(h, g_ref[...]).astype(BF16), w_ref[...]).astype(BF16)


def _combine_inproj(h, y, gates, g, w):
    s, d = h.shape
    n = w.shape[1]
    tm = min(ROW_TILE, s)
    nt = s // tm
    return pl.pallas_call(
        _combine_inproj_body,
        out_shape=[jax.ShapeDtypeStruct((s, d), F32), jax.ShapeDtypeStruct((s, n), BF16)],
        grid=(nt,),
        in_specs=[pl.BlockSpec((tm, d), lambda i: (i, 0))]
                 + [pl.BlockSpec((tm, d // 2), lambda i, k=k: (k * nt + i, 0)) for k in range(TOP_K)]
                 + [pl.BlockSpec((tm, TOP_K), lambda i: (i, 0)), pl.BlockSpec((1, d), lambda i: (0, 0)),
                    pl.BlockSpec((d, n), lambda i: (0, 0))],
        out_specs=[pl.BlockSpec((tm, d), lambda i: (i, 0)), pl.BlockSpec((tm, n), lambda i: (i, 0))],
        compiler_params=_params(("parallel",)),
        name="combine_inproj",
    )(h, y, y, y, y, gates, g.reshape(1, d), w)


def _t5_bucket(n):
    max_exact = NUM_BUCKETS // 2
    nf = jnp.maximum(n, 1).astype(F32)
    large = max_exact + (jnp.log(nf / max_exact) / math.log(REL_MAX_DIST / max_exact)
                         * (NUM_BUCKETS - max_exact)).astype(jnp.int32)
    large = jnp.minimum(large, NUM_BUCKETS - 1)
    return jnp.where(n < max_exact, n, large)


def _attn_bias(rel_bias, window, dil, perm):
    steps = window // dil
    perm = jnp.asarray(perm, jnp.int32)
    qi = perm[:, None]
    ki = jnp.concatenate([perm, BLK + perm])[None, :]
    dist = BLK + qi - ki
    in_win = (dist >= 0) & (dist <= steps)
    bucket = _t5_bucket(jnp.clip(dist, 0, steps) * dil)
    onehot = (bucket[:, :, None] == jnp.arange(NUM_BUCKETS, dtype=jnp.int32)).astype(F32)
    bias = jnp.einsum('qkb,bh->hqk', onehot, rel_bias.astype(F32), precision=lax.Precision.HIGHEST)
    bias = jnp.where(in_win[None], bias * LOG2E, NEG_INF)
    return bias.reshape(H_A // 2, 2 * BLK, 2 * BLK)


_PERM_D1 = [NRES * jl + r for r in range(NRES) for jl in range(BLK // NRES)]
_PERM_D4 = [4 * jl + i for i in range(4) for jl in range(BLK // 4)]
_PERM_D16 = list(range(BLK))


def _attn_body(q_ref, k_ref, v_ref, kp_ref, vp_ref, b1_ref, b4_ref, b16_ref, o_ref, acc, mst, lst):
    has_prev = pl.program_id(0) > 0
    lane = lax.broadcasted_iota(jnp.int32, (1, LANES), 1)
    lo = lane < HD_A
    mlo = lo.astype(BF16)
    mhi = (~lo).astype(BF16)
    col = lax.broadcasted_iota(jnp.int32, (2 * BLK, 2 * BLK), 1)
    ones = jnp.ones((2 * BLK, LANES), BF16)

    def tile(q2, kk, vv, bias, mask_prev):
        qs = jnp.concatenate([q2 * mlo, q2 * mhi], axis=0)
        s = _dot_nt(qs, kk) + bias
        if mask_prev:
            s = jnp.where(jnp.logical_or(has_prev, col >= BLK), s, NEG_INF)
        m = jnp.max(s, axis=-1, keepdims=True)
        e = jnp.exp2((s - m).astype(BF16))
        oa = _dot(e, jnp.concatenate([vv, ones], axis=1))
        o = oa[:, :LANES]
        l = oa[:, LANES:]
        return (jnp.where(lo, m[:BLK], m[BLK:]), jnp.where(lo, l[:BLK], l[BLK:]),
                jnp.where(lo, o[:BLK], o[BLK:]))

    def merge(prev, cur):
        mp, lp, ap = prev
        mc, lc, ac = cur
        mn = jnp.maximum(mp, mc)
        a = jnp.exp2(mp - mn)
        b = jnp.exp2(mc - mn)
        return mn, a * lp + b * lc, a * ap + b * ac

    def cat(xs):
        return jnp.concatenate(xs, axis=0)


    for r in range(NRES):
        rows = pl.ds(r * BLK, BLK)
        for hp in range(H_A // 2):
            lanes = slice(hp * LANES, (hp + 1) * LANES)
            kk = cat([kp_ref[rows, lanes], k_ref[rows, lanes]])
            vv = cat([vp_ref[rows, lanes], v_ref[rows, lanes]])
            m2, l2, o2 = tile(q_ref[rows, lanes], kk, vv, b16_ref[hp], True)
            mst[rows, lanes] = m2
            lst[rows, lanes] = l2
            acc[rows, lanes] = o2

    for r4 in range(4):
        for b in range(4):
            def chunk_rows(bb):
                return [pl.ds(4 * BLK * i + BLK * r4 + 32 * bb, 32) for i in range(4)]
            rows = chunk_rows(b)
            prows = chunk_rows(3 if b == 0 else b - 1)
            kprev, vprev = (kp_ref, vp_ref) if b == 0 else (k_ref, v_ref)
            for hp in range(H_A // 2):
                lanes = slice(hp * LANES, (hp + 1) * LANES)
                q2 = cat([q_ref[rr, lanes] for rr in rows])
                kk = cat([kprev[rr, lanes] for rr in prows] + [k_ref[rr, lanes] for rr in rows])
                vv = cat([vprev[rr, lanes] for rr in prows] + [v_ref[rr, lanes] for rr in rows])
                cur = tile(q2, kk, vv, b4_ref[hp], b == 0)
                prev = (cat([mst[rr, lanes] for rr in rows]), cat([lst[rr, lanes] for rr in rows]),
                        cat([acc[rr, lanes] for rr in rows]))
                mn, ln, an = merge(prev, cur)
                for i, rr in enumerate(rows):
                    part = slice(32 * i, 32 * (i + 1))
                    mst[rr, lanes] = mn[part]
                    lst[rr, lanes] = ln[part]
                    acc[rr, lanes] = an[part]

    def d1_pair(ap, kprev, vprev, prev_ap, mask_prev):
        def tiles(a_):
            return [pl.ds(BLK * r + 16 * a_, 16) for r in range(NRES)]
        cur_t = tiles(ap)
        prev_t = tiles(prev_ap)

        def halves(ref, ts, lanes):
            xs = [ref[t, lanes].astype(F32) for t in ts]
            return cat([x[:8] for x in xs]).astype(BF16), cat([x[8:] for x in xs]).astype(BF16)

        for hp in range(H_A // 2):
            lanes = slice(hp * LANES, (hp + 1) * LANES)
            q_e, q_o = halves(q_ref, cur_t, lanes)
            k_e, k_o = halves(k_ref, cur_t, lanes)
            v_e, v_o = halves(v_ref, cur_t, lanes)
            _, k_p = halves(kprev, prev_t, lanes)
            _, v_p = halves(vprev, prev_t, lanes)
            cur_e = tile(q_e, cat([k_p, k_e]), cat([v_p, v_e]), b1_ref[hp], mask_prev)
            cur_o = tile(q_o, cat([k_e, k_o]), cat([v_e, v_o]), b1_ref[hp], False)
            ms = [mst[t, lanes] for t in cur_t]
            ls = [lst[t, lanes] for t in cur_t]
            ac = [acc[t, lanes] for t in cur_t]
            outs = []
            for half, cur in ((0, cur_e), (1, cur_o)):
                part = slice(8 * half, 8 * half + 8)
                prev = (cat([x[part] for x in ms]), cat([x[part] for x in ls]), cat([x[part] for x in ac]))
                _, ln, an = merge(prev, cur)
                outs.append(an / ln)
            for r, t in enumerate(cur_t):
                part = slice(8 * r, 8 * r + 8)
                o_ref[t, lanes] = cat([outs[0][part], outs[1][part]]).astype(o_ref.dtype)

    d1_pair(0, kp_ref, vp_ref, BLK // 16 - 1, True)
    for ap in range(1, BLK // 16):
        d1_pair(ap, k_ref, v_ref, ap - 1, False)


def _dilated_attention(proj, biases):
    s = proj.shape[0]
    cur = lambda which: pl.BlockSpec((SPAN, D_A), lambda c: (c, which))
    prev = lambda which: pl.BlockSpec((SPAN, D_A), lambda c: (jnp.maximum(c - 1, 0), which))
    return pl.pallas_call(
        _attn_body,
        out_shape=jax.ShapeDtypeStruct((s, D_A), BF16),
        grid=(s // SPAN,),
        in_specs=[cur(0), cur(1), cur(2), prev(1), prev(2)] + [_full(b) for b in biases],
        out_specs=pl.BlockSpec((SPAN, D_A), lambda c: (c, 0)),
        scratch_shapes=[pltpu.VMEM((SPAN, D_A), F32)] * 3,
        compiler_params=_params(("arbitrary",)),
        name="attn",
    )(proj, proj, proj, proj, proj, *biases)


def _s5_operators(a_re, a_im, b_re, b_im, c_re, c_im, log_dt, d_skip):
    L = S5_CHUNK
    lam = lax.complex(a_re.astype(F32), a_im.astype(F32))
    dt = jnp.exp(log_dt.astype(F32))[:, None]
    a_bar = jnp.exp(lam * dt)
    b_bar = ((a_bar - 1.0) / lam)[..., None] * lax.complex(b_re.astype(F32), b_im.astype(F32))
    c = lax.complex(c_re.astype(F32), c_im.astype(F32))
    j = jnp.arange(L + 1, dtype=F32)
    log_a = lam * dt
    apow = jnp.exp(log_a[None] * j[:, None, None])
    kt = jnp.einsum('gdp,jgp,gpc->gcjd', c, apow[:L], b_bar).real
    skip = d_skip.astype(F32).reshape(S5_G, S5_CH, 1, 1) * jnp.eye(S5_CH)[None, :, None, :]
    kt = (kt + skip * (jnp.arange(L) == 0)[None, None, :, None]).reshape(S5_G, S5_CH, L * S5_CH)
    p = jnp.einsum('sgp,gpc->gscp', apow[:L][::-1], b_bar).reshape(S5_G, L * S5_CH, S5_P)
    ca = jnp.einsum('gdp,tgp->gptd', c, apow[1:L + 1]).reshape(S5_G, S5_P, L * S5_CH)
    a_l = apow[L]

    def pair_blocks(x):
        g, r, w = x.shape
        x = x.reshape(S5_PAIRS, 2, r, w)
        z = jnp.zeros_like(x[:, 0])
        top = jnp.concatenate([x[:, 0], z], axis=-1)
        bot = jnp.concatenate([z, x[:, 1]], axis=-1)
        return jnp.concatenate([top, bot], axis=1)

    p2 = jnp.concatenate([pair_blocks(p.real), pair_blocks(p.imag)], axis=-1)
    q2 = jnp.concatenate([pair_blocks(ca.real), pair_blocks(-ca.imag)], axis=1)
    a_lr = a_l.real.reshape(1, S5_G * S5_P)
    a_li = a_l.imag.reshape(1, S5_G * S5_P)
    return kt, p2.astype(BF16), q2.astype(BF16), a_lr, a_li


def _s5_body(u_ref, kt_ref, p_ref, q_ref, ar_ref, ai_ref, o_ref, m_ref, w_ref, y_ref,
             ere, eim, xre, xim, sre, sim):
    rows = BLK
    gw = S5_CHUNK * S5_CH
    pw = 2 * gw
    per_vreg = LANES // S5_CH
    chunk_of_lane = lax.broadcasted_iota(jnp.int32, (rows, LANES), 1) // S5_CH

    def chunk_transpose(arrs):
        arrs = list(arrs)
        for s in (4, 2, 1):
            upper = (chunk_of_lane & s) != 0
            nxt = list(arrs)
            for i in range(per_vreg):
                if i & s:
                    continue
                lo_a, hi_a = arrs[i], arrs[i + s]
                nxt[i] = jnp.where(upper, pltpu.roll(hi_a, s * S5_CH, axis=1), lo_a)
                nxt[i + s] = jnp.where(upper, hi_a, pltpu.roll(lo_a, LANES - s * S5_CH, axis=1))
            arrs = nxt
        return arrs

    @pl.when(pl.program_id(0) == 0)
    def _():
        sre[...] = jnp.zeros_like(sre)
        sim[...] = jnp.zeros_like(sim)
        lane = lax.broadcasted_iota(jnp.int32, (S5_CH, gw), 1)

        def build(g, _):
            kt = kt_ref[g]
            for s in range(S5_CHUNK):
                blk = kt if s == 0 else jnp.where(lane >= s * S5_CH, pltpu.roll(kt, s * S5_CH, axis=1), 0.0)
                m_ref[g, s * S5_CH:(s + 1) * S5_CH, :] = blk.astype(BF16)
            return 0

        lax.fori_loop(0, S5_G, build, 0)

    for b in range(D_B // LANES):
        for a in range(S5_CHUNK // per_vreg):
            srcs = [u_ref[(per_vreg * a + i) * rows:(per_vreg * a + i + 1) * rows,
                          b * LANES:(b + 1) * LANES].astype(F32) for i in range(per_vreg)]
            for gi, arr in enumerate(chunk_transpose(srcs)):
                g = per_vreg * b + gi
                w_ref[:, g * gw + a * LANES:g * gw + (a + 1) * LANES] = arr.astype(BF16)

    for pr in range(S5_PAIRS):
        e = _dot(w_ref[:, pr * pw:(pr + 1) * pw], p_ref[pr])
        ere[:, pr * LANES:(pr + 1) * LANES] = e[:, :LANES]
        eim[:, pr * LANES:(pr + 1) * LANES] = e[:, LANES:]

    ar = ar_ref[...]
    ai = ai_ref[...]

    def step(n, carry):
        xr, xi = carry
        xre[pl.ds(n, 1), :] = xr
        xim[pl.ds(n, 1), :] = xi
        nr = ar * xr - ai * xi + ere[pl.ds(n, 1), :]
        ni = ar * xi + ai * xr + eim[pl.ds(n, 1), :]
        return nr, ni

    xr, xi = lax.fori_loop(0, rows, step, (sre[...], sim[...]))
    sre[...] = xr
    sim[...] = xi

    for pr in range(S5_PAIRS):
        xin = jnp.concatenate([xre[:, pr * LANES:(pr + 1) * LANES],
                               xim[:, pr * LANES:(pr + 1) * LANES]], axis=-1).astype(BF16)
        yc = _dot(xin, q_ref[pr])
        for half in range(2):
            g = 2 * pr + half
            cols = slice(g * gw, (g + 1) * gw)
            y = _dot(w_ref[:, cols], m_ref[g]) + yc[:, half * gw:(half + 1) * gw]
            y_ref[:, cols] = 0.5 * y * (1.0 + lax.erf(y * (2.0 ** -0.5)))

    for b in range(D_B // LANES):
        for a in range(S5_CHUNK // per_vreg):
            srcs = [y_ref[:, (per_vreg * b + i) * gw + a * LANES:(per_vreg * b + i) * gw + (a + 1) * LANES]
                    for i in range(per_vreg)]
            for ri, arr in enumerate(chunk_transpose(srcs)):
                r = per_vreg * a + ri
                o_ref[r * rows:(r + 1) * rows, b * LANES:(b + 1) * LANES] = arr.astype(BF16)


def _s5_core(proj, ops):
    kt, p2, q2, a_lr, a_li = ops
    s = proj.shape[0]
    gw = S5_CHUNK * S5_CH
    wide = S5_G * gw
    nstate = S5_G * S5_P
    return pl.pallas_call(
        _s5_body,
        out_shape=jax.ShapeDtypeStruct((s, D_B), BF16),
        grid=(s // SPAN,),
        in_specs=[pl.BlockSpec((SPAN, D_B), lambda i: (i, 3 * D_A // D_B)),
                  _full(kt), _full(p2), _full(q2), _full(a_lr), _full(a_li)],
        out_specs=pl.BlockSpec((SPAN, D_B), lambda i: (i, 0)),
        scratch_shapes=[pltpu.VMEM((S5_G, gw, gw), BF16), pltpu.VMEM((BLK, wide), BF16),
                        pltpu.VMEM((BLK, wide), F32)]
                       + [pltpu.VMEM((BLK, nstate), F32)] * 4 + [pltpu.VMEM((1, nstate), F32)] * 2,
        compiler_params=_params(("arbitrary",)),
        name="s5",
    )(proj, kt, p2, q2, a_lr, a_li)


def _split_bf16(x):
    hi = x.astype(BF16)
    lo = (x - hi.astype(F32)).astype(BF16)
    return hi, lo


def _mid_body(h_ref, ya_ref, yb_ref, wglu_ref, bglu_ref, ga_ref, gb_ref, wout_ref, gx_ref, wq_ref,
              k_ref, v_ref, wo_ref, gm_ref, wr_ref, br_ref, h_out, xn_out, logit_out):
    part = h_ref.shape[0] // MID_CHAINS
    for c in range(MID_CHAINS):
        rows = slice(c * part, (c + 1) * part)
        outs = _mid_rows(h_ref[rows, :], ya_ref[rows, :], yb_ref[rows, :], wglu_ref, bglu_ref, ga_ref, gb_ref,
                         wout_ref, gx_ref, wq_ref, k_ref, v_ref, wo_ref, gm_ref, wr_ref, br_ref)
        for ref, val in zip((h_out, xn_out, logit_out), outs):
            ref[rows, :] = val


def _mid_rows(h, ya, yb, wglu_ref, bglu_ref, ga_ref, gb_ref, wout_ref, gx_ref, wq_ref,
              k_ref, v_ref, wo_ref, gm_ref, wr_ref, br_ref):
    gate = jax.nn.sigmoid(_dot(yb, wglu_ref[...]) + bglu_ref[...])
    yb2 = yb.astype(F32) * gate
    na = _rms(ya.astype(F32), ga_ref[...]).astype(BF16)
    nb = _rms(yb2, gb_ref[...]).astype(BF16)
    h1 = h + _dot(na, wout_ref[0:D_A, :]) + _dot(nb, wout_ref[D_A:D_MODEL, :])
    q = _dot(_rms(h1, gx_ref[...]).astype(BF16), wq_ref[...]).astype(BF16)
    heads = []
    for hd in range(H_X):
        lanes = slice(hd * HD_X, (hd + 1) * HD_X)
        s = _dot_nt(q[:, lanes], k_ref[:, lanes])
        e = jnp.exp(s - jnp.max(s, axis=-1, keepdims=True))
        heads.append(_dot(e.astype(BF16), v_ref[:, lanes]) / jnp.sum(e, axis=-1, keepdims=True))
    o = jnp.concatenate(heads, axis=-1).astype(BF16)
    h2 = h1 + _dot(o, wo_ref[...])
    xn = _rms(h2, gm_ref[...])
    x_hi, x_lo = _split_bf16(xn)
    part = _dot(x_hi, wr_ref[...]) + _dot(x_lo, wr_ref[...])
    logits = part + pltpu.roll(part, LANES - N_EXPERTS, axis=1) + br_ref[...]
    return h2, _pack_rows(xn), logits


def _mid(h, ya, yb, wglu, bglu, ga, gb, wout, gx, wq, kmem, vmem, wo, gm, wr2, br):
    s = h.shape[0]
    tm = min(ROW_TILE, s)
    row = lambda w: pl.BlockSpec((tm, w), lambda i: (i, 0))
    consts = [wglu, bglu, ga, gb, wout, gx, wq, kmem, vmem, wo, gm, wr2, br]
    return pl.pallas_call(
        _mid_body,
        out_shape=[jax.ShapeDtypeStruct((s, D_MODEL), F32), jax.ShapeDtypeStruct((s, D_MODEL // 2), jnp.int32),
                   jax.ShapeDtypeStruct((s, LANES), F32)],
        grid=(s // tm,),
        in_specs=[row(D_MODEL), row(D_A), row(D_B)] + [_full(a) for a in consts],
        out_specs=[row(D_MODEL), row(D_MODEL // 2), row(LANES)],
        compiler_params=_params(("parallel",)),
        name="mid",
    )(h, ya, yb, *consts)


def _memkv_body(mem_ref, g_ref, w_ref, o_ref):
    o_ref[...] = _dot(_rms(mem_ref[...], g_ref[...]).astype(BF16), w_ref[...]).astype(BF16)


def _memkv(mem, g, w):
    n, d = mem.shape
    return pl.pallas_call(
        _memkv_body,
        out_shape=jax.ShapeDtypeStruct((n, w.shape[1]), BF16),
        compiler_params=pltpu.CompilerParams(vmem_limit_bytes=VMEM_LIMIT),
        name="memkv",
    )(mem, g.reshape(1, d), w)


def _rank_body(logit_ref, tri_ref, idx_ref, gate_ref, rank_ref, cnt_ref, carry):
    @pl.when(pl.program_id(0) == 0)
    def _():
        carry[...] = jnp.zeros_like(carry)

    tm = logit_ref.shape[0]
    logits = jnp.transpose(logit_ref[...])[:N_EXPERTS, :]
    expert = lax.broadcasted_iota(jnp.int32, (N_EXPERTS, tm), 0)
    vals, idxs = [], []
    for _ in range(TOP_K):
        mx = jnp.max(logits, axis=0, keepdims=True)
        ix = jnp.min(jnp.where(logits == mx, expert, N_EXPERTS), axis=0, keepdims=True)
        vals.append(mx)
        idxs.append(ix)
        logits = jnp.where(expert == ix, -jnp.inf, logits)
    es = [jnp.exp(v - vals[0]) for v in vals]
    den = es[0] + es[1] + es[2] + es[3]
    hits = [expert == ix for ix in idxs]
    onehot = jnp.zeros((N_EXPERTS, tm), F32)
    for hit in hits:
        onehot = onehot + jnp.where(hit, 1.0, 0.0)
    nb = tm // LANES
    blocks = jnp.concatenate([onehot[:, b * LANES:(b + 1) * LANES] for b in range(nb)], axis=0)
    inc = _dot(blocks.astype(BF16), tri_ref[...])
    run = carry[...][:, 0:1]
    before = []
    for b in range(nb):
        inc_b = inc[b * N_EXPERTS:(b + 1) * N_EXPERTS, :]
        before.append(run + inc_b - onehot[:, b * LANES:(b + 1) * LANES])
        run = run + inc_b[:, LANES - 1:LANES]
    before = jnp.concatenate(before, axis=1)
    choice = lax.broadcasted_iota(jnp.int32, (SUBLANES, tm), 0)
    idx_t = jnp.full((SUBLANES, tm), N_EXPERTS, jnp.int32)
    gate_t = jnp.zeros((SUBLANES, tm), F32)
    rank_t = jnp.zeros((SUBLANES, tm), jnp.int32)
    for k in range(TOP_K):
        rk = jnp.sum(jnp.where(hits[k], before, 0.0), axis=0, keepdims=True)
        idx_t = jnp.where(choice == k, idxs[k], idx_t)
        gate_t = jnp.where(choice == k, es[k] / den, gate_t)
        rank_t = jnp.where(choice == k, rk.astype(jnp.int32), rank_t)
    idx_ref[...] = idx_t
    gate_ref[...] = gate_t
    rank_ref[...] = rank_t
    total = jnp.broadcast_to(run, carry.shape)
    carry[...] = total
    cnt_ref[...] = total.astype(jnp.int32)


def _rank(logits):
    t = logits.shape[0]
    tm = min(RANK_TILE, t)
    cols = pl.BlockSpec((SUBLANES, tm), lambda i: (0, i))
    tri = jnp.asarray(np.triu(np.ones((LANES, LANES), np.float32)), BF16)
    return pl.pallas_call(
        _rank_body,
        out_shape=[jax.ShapeDtypeStruct((SUBLANES, t), jnp.int32), jax.ShapeDtypeStruct((SUBLANES, t), F32),
                   jax.ShapeDtypeStruct((SUBLANES, t), jnp.int32),
                   jax.ShapeDtypeStruct((N_EXPERTS, LANES), jnp.int32)],
        grid=(t // tm,),
        in_specs=[pl.BlockSpec((tm, LANES), lambda i: (i, 0)), _full(tri)],
        out_specs=[cols, cols, cols, pl.BlockSpec((N_EXPERTS, LANES), lambda i: (0, 0))],
        scratch_shapes=[pltpu.VMEM((N_EXPERTS, LANES), F32)],
        compiler_params=_params(("arbitrary",)),
        name="rank",
    )(logits, tri)


def _route(logits, tm):
    t = logits.shape[0]
    tk = t * TOP_K
    idx, gates, rank, cnt = _rank(logits)
    counts = cnt[:, 0]
    padded = (counts + tm - 1) // tm * tm
    pend = jnp.cumsum(padded)
    pstart = pend - padded
    n_rows = tk + N_EXPERTS * tm
    n_tiles = n_rows // tm
    experts = jnp.arange(N_EXPERTS, dtype=jnp.int32)
    tile_first = jnp.arange(n_tiles, dtype=jnp.int32) * tm
    last_used = jnp.max(jnp.where(padded > 0, experts, 0))
    tile_e = jnp.minimum(jnp.sum(tile_first[:, None] >= pend[None, :], axis=1), last_used).astype(jnp.int32)
    tile_rows = jnp.clip(jnp.sum(jnp.where(tile_e[:, None] == experts[None, :],
                                           (pstart + counts)[None, :], 0), axis=1) - tile_first, 0, tm)
    tile_rows = jnp.where(tile_first < pend[-1], tile_rows, 0).astype(jnp.int32)
    group = jnp.cumsum(jnp.concatenate([jnp.zeros((1,), jnp.int32),
                                        (tile_e[1:] != tile_e[:-1]).astype(jnp.int32)]))
    tile_slot = (group % 2).astype(jnp.int32)
    later = (experts[None, :] > experts[:, None]) & (padded > 0)[None, :]
    next_e = jnp.min(jnp.where(later, experts[None, :], N_EXPERTS), axis=1)
    next_e = jnp.where(next_e < N_EXPERTS, next_e, -1).astype(jnp.int32)
    tile_next = jnp.sum(jnp.where(tile_e[:, None] == experts[None, :], next_e[None, :], 0), axis=1).astype(jnp.int32)
    base = jnp.sum(jnp.where(idx[:TOP_K, :, None] == experts, pstart, 0), axis=-1)
    dest = rank[:TOP_K] + base
    return gates[:TOP_K].T, dest, n_rows, tile_e, tile_rows, tile_slot, tile_next


def _sc_mesh():
    return plsc.VectorSubcoreMesh(core_axis_name="c", subcore_axis_name="s",
                                  num_cores=SC_CORES, num_subcores=SC_SUBCORES)


def _sc_worker():
    return lax.axis_index("s") * SC_CORES + lax.axis_index("c")


def _scatter_rows(x, dest, n_rows):
    t, d = x.shape
    per_worker = t // (SC_CORES * SC_SUBCORES)
    chunks = per_worker // SC_ROWS

    assert chunks % 2 == 0

    @functools.partial(
        pl.kernel, mesh=_sc_mesh(),
        out_type=jax.ShapeDtypeStruct((n_rows, d), x.dtype),
        scratch_types=[pltpu.VMEM((SC_ROWS, d), x.dtype)] * 2 + [pltpu.VMEM((SC_ROWS,), jnp.int32)] * (2 * TOP_K)
                      + [pltpu.SemaphoreType.DMA((2,)), pltpu.SemaphoreType.DMA((2, TOP_K))],
    )
    def scatter(x_hbm, *rest):
        dest_hbm, out_hbm = rest[:TOP_K], rest[TOP_K]
        rows_v = rest[TOP_K + 1:TOP_K + 3]
        idx_v = (rest[TOP_K + 3:2 * TOP_K + 3], rest[2 * TOP_K + 3:3 * TOP_K + 3])
        lsem, ssem = rest[3 * TOP_K + 3], rest[3 * TOP_K + 4]
        base = _sc_worker() * per_worker

        def rows_at(c):
            return pl.ds(pl.multiple_of(base + c * SC_ROWS, SC_ROWS), SC_ROWS)

        def loaded(c, s):
            return pltpu.make_async_copy(x_hbm.at[rows_at(c)], rows_v[s], lsem.at[s])

        def load(c, s):
            loaded(c, s).start()
            for k in range(TOP_K):
                pltpu.sync_copy(dest_hbm[k].at[rows_at(c)], idx_v[s][k])

        def scattered(s):
            return [pltpu.make_async_copy(rows_v[s], out_hbm.at[idx_v[s][k]], ssem.at[s, k]) for k in range(TOP_K)]

        load(0, 0)

        @pl.loop(0, chunks, step=2)
        def _(c0):
            for s in range(2):
                c = c0 + s
                loaded(c, s).wait()
                for cp in scattered(s):
                    cp.start()

                @pl.when(c >= 1)
                def _():
                    for cp in scattered(1 - s):
                        cp.wait()

                @pl.when(c + 1 < chunks)
                def _():
                    load(c + 1, 1 - s)

        for cp in scattered(1):
            cp.wait()

    return scatter(x, *[dest[k] for k in range(TOP_K)])


def _gather_rows(table, idx):
    n, d = table.shape
    b = idx.shape[0]
    per_worker = b // (SC_CORES * SC_SUBCORES)
    chunks = per_worker // SC_ROWS
    assert chunks % 2 == 0

    @functools.partial(
        pl.kernel, mesh=_sc_mesh(),
        out_type=jax.ShapeDtypeStruct((b, d), table.dtype),
        scratch_types=[pltpu.VMEM((chunks, SC_ROWS), jnp.int32)] + [pltpu.VMEM((SC_ROWS, d), table.dtype)] * 2
                      + [pltpu.SemaphoreType.DMA((2,)), pltpu.SemaphoreType.DMA((2,))],
    )
    def gather(table_hbm, idx_hbm, out_hbm, idx_v, rows0, rows1, gsem, wsem):
        rows_v = (rows0, rows1)
        worker = _sc_worker()
        base = worker * per_worker
        pltpu.sync_copy(idx_hbm.at[pl.ds(pl.multiple_of(worker * chunks, chunks), chunks)], idx_v)

        def rows_at(c):
            return pl.ds(pl.multiple_of(base + c * SC_ROWS, SC_ROWS), SC_ROWS)

        def fetched(c, s):
            return pltpu.make_async_copy(table_hbm.at[idx_v.at[c]], rows_v[s], gsem.at[s])

        def written(c, s):
            return pltpu.make_async_copy(rows_v[s], out_hbm.at[rows_at(c)], wsem.at[s])

        fetched(0, 0).start()

        @pl.loop(0, chunks, step=2)
        def _(c0):
            for s in range(2):
                c = c0 + s
                fetched(c, s).wait()
                written(c, s).start()

                @pl.when(c >= 1)
                def _():
                    written(c - 1, 1 - s).wait()

                @pl.when(c + 1 < chunks)
                def _():
                    fetched(c + 1, 1 - s).start()

        written(chunks - 1, 1).wait()

    return gather(table, idx.reshape(b // SC_ROWS, SC_ROWS))


def _moe_body(te_ref, tv_ref, sl_ref, nx_ref, x_ref, w1_hbm, b1_ref, w2_hbm, b2_ref, o_ref,
              w1f, w2f, w1b, w2b, sem, *, layer):
    i = pl.program_id(0)
    e = te_ref[i]
    slot = sl_ref[i]
    new_expert = (i == 0) | (e != te_ref[jnp.maximum(i - 1, 0)])

    def weight_copies(expert, s):
        rows1 = w1f.shape[1] // MOE_W1_PARTS
        rows2 = w2f.shape[1] // MOE_W2_PARTS
        c1 = [pltpu.make_async_copy(w1_hbm.at[layer, expert, pl.ds(q * rows1, rows1)],
                                    w1f.at[s, pl.ds(q * rows1, rows1)], sem.at[s, q])
              for q in range(MOE_W1_PARTS)]
        c2 = [pltpu.make_async_copy(w2_hbm.at[layer, expert, pl.ds(q * rows2, rows2)],
                                    w2f.at[s, pl.ds(q * rows2, rows2)], sem.at[s, MOE_W1_PARTS + q])
              for q in range(MOE_W2_PARTS)]
        return c1 + c2

    @pl.when(i == 0)
    def _():
        for c in weight_copies(e, slot):
            c.start()

    @pl.when(new_expert)
    def _():
        for c in weight_copies(e, slot):
            c.wait()
        nxt = nx_ref[i]

        @pl.when(nxt >= 0)
        def _():
            for c in weight_copies(nxt, 1 - slot):
                c.start()

        w1b[...] = w1f[slot].astype(BF16)
        w2b[...] = w2f[slot].astype(BF16)

    def expert(rows):
        row = lax.broadcasted_iota(jnp.int32, (rows, x_ref.shape[1]), 0)
        lo, hi = _unpack_rows(jnp.where(row < tv_ref[i], x_ref[0:rows, :], 0))
        x = jnp.concatenate([lo, hi], axis=-1).astype(BF16)
        hb = _dot(x, w1b[...]) + b1_ref[0]
        x_glu = jnp.minimum(hb[:, :D_FF], SWIGLU_LIMIT)
        x_lin = jnp.clip(hb[:, D_FF:], -SWIGLU_LIMIT, SWIGLU_LIMIT)
        act = x_glu * jax.nn.sigmoid(SWIGLU_ALPHA * x_glu) * (x_lin + 1.0)
        o_ref[0:rows, :] = _pack_rows(_dot(act.astype(BF16), w2b[...]) + b2_ref[0])

    step = x_ref.shape[0] // MOE_ROW_PATHS
    for path in range(1, MOE_ROW_PATHS + 1):
        rows = path * step

        @pl.when((tv_ref[i] > rows - step) & (tv_ref[i] <= rows))
        def _(rows=rows):
            expert(rows)
            if rows < x_ref.shape[0]:
                o_ref[rows:, :] = jnp.zeros((x_ref.shape[0] - rows, o_ref.shape[1]), o_ref.dtype)

    @pl.when(tv_ref[i] == 0)
    def _():
        o_ref[...] = jnp.zeros_like(o_ref)


def _moe_experts(xs, tile_e, tile_rows, tile_slot, tile_next, w1, b1, w2, b2, layer):
    n_rows = xs.shape[0]
    tm = MOE_TILE
    nl, ne, d, ff2 = w1.shape
    bias_map = lambda i, te, tv, sl, nx: (layer, te[i], 0, 0)
    grid_spec = pltpu.PrefetchScalarGridSpec(
        num_scalar_prefetch=4,
        grid=(n_rows // tm,),
        in_specs=[pl.BlockSpec((tm, d // 2), lambda i, *_: (i, 0)),
                  pl.BlockSpec(memory_space=pl.ANY),
                  pl.BlockSpec((None, 1, 1, ff2), bias_map),
                  pl.BlockSpec(memory_space=pl.ANY),
                  pl.BlockSpec((None, 1, 1, d), bias_map)],
        out_specs=pl.BlockSpec((tm, d // 2), lambda i, *_: (i, 0)),
        scratch_shapes=[pltpu.VMEM((2, d, ff2), F32), pltpu.VMEM((2, ff2 // 2, d), F32),
                        pltpu.VMEM((d, ff2), BF16), pltpu.VMEM((ff2 // 2, d), BF16),
                        pltpu.SemaphoreType.DMA((2, MOE_W1_PARTS + MOE_W2_PARTS))],
    )
    return pl.pallas_call(
        functools.partial(_moe_body, layer=layer),
        out_shape=jax.ShapeDtypeStruct((n_rows, d // 2), jnp.int32),
        grid_spec=grid_spec,
        compiler_params=_params(("arbitrary",)),
        name="moe",
    )(tile_e, tile_rows, tile_slot, tile_next, xs, w1, b1.reshape(nl, ne, 1, ff2), w2, b2.reshape(nl, ne, 1, d))


def _combined(h_ref, y_refs, gate_ref):
    gates = gate_ref[...]
    lo = jnp.zeros(y_refs[0].shape, F32)
    hi = jnp.zeros(y_refs[0].shape, F32)
    for k, y_ref in enumerate(y_refs):
        yl, yh = _unpack_rows(y_ref[...])
        lo = lo + yl * gates[:, k:k + 1]
        hi = hi + yh * gates[:, k:k + 1]
    return h_ref[...] + jnp.concatenate([lo, hi], axis=-1)


def _combine_final_body(h_ref, y0_ref, y1_ref, y2_ref, y3_ref, gate_ref, g_ref, out_hbm, buf, sem):
    i = pl.program_id(0)
    n = pl.num_programs(0)
    slot = i % 2
    per_step = h_ref.shape[0] // BLK
    steps_per_span = NRES // per_step

    def writes(step, s):
        span = step // steps_per_span
        r0 = (step % steps_per_span) * per_step
        return [pltpu.make_async_copy(buf.at[s, pl.ds(rr * BLK, BLK), :], out_hbm.at[span, :, r0 + rr, :],
                                      sem.at[s, rr]) for rr in range(per_step)]

    @pl.when(i >= 2)
    def _():
        for cp in writes(i - 2, slot):
            cp.wait()

    buf[slot] = _rms(_combined(h_ref, (y0_ref, y1_ref, y2_ref, y3_ref), gate_ref), g_ref[...])
    for cp in writes(i, slot):
        cp.start()

    @pl.when(i == n - 1)
    def _():
        for cp in writes(i, slot):
            cp.wait()

        @pl.when(i >= 1)
        def _():
            for cp in writes(i - 1, 1 - slot):
                cp.wait()


def _combine_final(h, y, gates, g):
    s, d = h.shape
    tm = min(ROW_TILE // 2, s)
    nt = s // tm
    in_specs = ([pl.BlockSpec((tm, d), lambda i: (i, 0))]
                + [pl.BlockSpec((tm, d // 2), lambda i, k=k: (k * nt + i, 0)) for k in range(TOP_K)]
                + [pl.BlockSpec((tm, TOP_K), lambda i: (i, 0))])
    out = pl.pallas_call(
        _combine_final_body,
        out_shape=jax.ShapeDtypeStruct((s // SPAN, BLK, NRES, d), F32),
        grid=(nt,),
        in_specs=in_specs + [pl.BlockSpec((1, d), lambda i: (0, 0))],
        out_specs=pl.BlockSpec(memory_space=pl.ANY),
        scratch_shapes=[pltpu.VMEM((2, tm, d), F32), pltpu.SemaphoreType.DMA((2, tm // BLK))],
        compiler_params=_params(("arbitrary",)),
        name="combine_final",
    )(h, y, y, y, y, gates, g.reshape(1, d))
    return out.reshape(s, d)


def _layer(state, mem, biases, p, l, g_final):
    row = lambda a: a.reshape(1, -1).astype(F32)
    q_scale = jnp.concatenate([jnp.full((D_A,), HD_A ** -0.5 * LOG2E, F32), jnp.ones((D_IN - D_A,), F32)])
    w_in = (p["w_in"][l] * q_scale).astype(BF16)
    if l == 0:
        h, proj = state, _inproj(state, p["norm_mix"][l], w_in)
    else:
        h, proj = _combine_inproj(*state, p["norm_mix"][l], w_in)
    ya = _dilated_attention(proj, biases)
    ops = _s5_operators(p["s5_a_re"][l], p["s5_a_im"][l], p["s5_b_re"][l], p["s5_b_im"][l],
                        p["s5_c_re"][l], p["s5_c_im"][l], p["s5_log_dt"][l], p["s5_d"][l])
    yb = _s5_core(proj, ops)
    kv = _memkv(mem, p["norm_mem"][l], p["w_xkv"][l].astype(BF16))
    wr = p["w_router"][l].astype(F32)
    wr_hi = wr.astype(BF16)
    wr2 = jnp.pad(jnp.concatenate([wr_hi, (wr - wr_hi.astype(F32)).astype(BF16)], axis=1),
                  ((0, 0), (0, LANES - 2 * N_EXPERTS)))
    br = jnp.pad(p["b_router"][l].astype(F32), (0, LANES - N_EXPERTS)).reshape(1, LANES)
    h2, xn, logits = _mid(
        h, ya, yb, p["w_glu"][l].astype(BF16), row(p["b_glu"][l]), row(p["g_out_attn"][l]),
        row(p["g_out_ssm"][l]), p["w_out"][l].astype(BF16), row(p["norm_xattn"][l]),
        (p["w_xq"][l] * (HD_X ** -0.5)).astype(BF16), kv[:, :D_X], kv[:, D_X:],
        p["w_xo"][l].astype(BF16), row(p["norm_moe"][l]), wr2, br)
    gates, dest, n_rows, tile_e, tile_rows, tile_slot, tile_next = _route(logits, MOE_TILE)
    out = _moe_experts(_scatter_rows(xn, dest, n_rows), tile_e, tile_rows, tile_slot, tile_next,
                       p["w1"], p["b1"], p["w2"], p["b2"], l)
    y = _gather_rows(out, dest.reshape(-1))
    return _combine_final(h2, y, gates, g_final) if l == DEPTH - 1 else (h2, y, gates)


def kernel(x, mem, rel_bias, norm_mix, w_in, s5_a_re, s5_a_im, s5_b_re, s5_b_im, s5_c_re, s5_c_im, s5_log_dt, s5_d, w_glu, b_glu, g_out_attn, g_out_ssm, w_out, norm_xattn, norm_mem, w_xq, w_xkv, w_xo, norm_moe, w_router, b_router, w1, b1, w2, b2, norm_final):
    p = dict(norm_mix=norm_mix, w_in=w_in, s5_a_re=s5_a_re, s5_a_im=s5_a_im, s5_b_re=s5_b_re,
             s5_b_im=s5_b_im, s5_c_re=s5_c_re, s5_c_im=s5_c_im, s5_log_dt=s5_log_dt, s5_d=s5_d,
             w_glu=w_glu, b_glu=b_glu, g_out_attn=g_out_attn, g_out_ssm=g_out_ssm, w_out=w_out,
             norm_xattn=norm_xattn, norm_mem=norm_mem, w_xq=w_xq, w_xkv=w_xkv, w_xo=w_xo,
             norm_moe=norm_moe, w_router=w_router, b_router=b_router, w1=w1, b1=b1, w2=w2, b2=b2)
    biases = [_attn_bias(rel_bias, window, dil, perm)
              for (window, dil), perm in zip(WIN_DIL, (_PERM_D1, _PERM_D4, _PERM_D16))]
    outs = []
    for b in range(x.shape[0]):
        h = _to_span_layout(x[b])
        for l in range(DEPTH):
            h = _layer(h, mem[b], biases, p, l, norm_final)
        outs.append(h)
    return jnp.stack(outs)
```

```python
import functools
import math

import jax
import jax.numpy as jnp
import numpy as np
from jax import lax
from jax.experimental import pallas as pl
from jax.experimental.pallas import tpu as pltpu
from jax.experimental.pallas import tpu_sc as plsc

F32 = jnp.float32
BF16 = jnp.bfloat16

D_MODEL = 1024
DEPTH = 2
EPS = 1e-5
NEG_INF = -1e30
LOG2E = math.log2(math.e)
H_A = 8
HD_A = 64
D_A = H_A * HD_A
WIN_DIL = ((128, 1), (512, 4), (2048, 16))
BLK = 128
D_B = D_MODEL - D_A
S5_CH = 16
S5_G = D_B // S5_CH
S5_P = 64
D_IN = 3 * D_A + D_B
NUM_BUCKETS = 32
REL_MAX_DIST = 2048
H_X = 4
HD_X = 128
D_X = H_X * HD_X
N_EXPERTS = 32
TOP_K = 4
D_FF = D_MODEL
SWIGLU_ALPHA = 1.702
SWIGLU_LIMIT = 7.0

LANES = 128
SUBLANES = 8
NRES = WIN_DIL[-1][1]
SPAN = NRES * BLK
S5_CHUNK = NRES
S5_PAIRS = S5_G // 2
VMEM_LIMIT = 56 * 1024 * 1024

SC_CORES = 2
SC_SUBCORES = 16
SC_ROWS = 64

ROW_TILE = 512
ROW_CHAINS = 2
MID_CHAINS = 4
RANK_TILE = 1024
MOE_TILE = 512
MOE_ROW_PATHS = 4
MOE_W1_PARTS = 4
MOE_W2_PARTS = 2


def _params(sem):
    return pltpu.CompilerParams(dimension_semantics=sem, vmem_limit_bytes=VMEM_LIMIT)


def _rms(x, g):
    return x * lax.rsqrt(jnp.mean(x * x, axis=-1, keepdims=True) + EPS) * g


def _dot(a, b):
    return jnp.dot(a, b, preferred_element_type=F32)


def _dot_nt(a, b):
    return lax.dot_general(a, b, (((1,), (1,)), ((), ())), preferred_element_type=F32)


def _full(a):
    return pl.BlockSpec(a.shape, lambda *_: (0,) * a.ndim)


def _pack_rows(x):
    c = x.shape[1] // 2
    lo = lax.bitcast_convert_type(x[:, :c].astype(BF16).astype(F32), jnp.uint32)
    hi = lax.bitcast_convert_type(x[:, c:].astype(BF16).astype(F32), jnp.uint32)
    return lax.bitcast_convert_type(lax.shift_right_logical(lo, jnp.uint32(16)) | hi, jnp.int32)


def _unpack_rows(p):
    u = lax.bitcast_convert_type(p, jnp.uint32)
    lo = lax.bitcast_convert_type(lax.shift_left(u, jnp.uint32(16)), F32)
    hi = lax.bitcast_convert_type(u & jnp.uint32(0xFFFF0000), F32)
    return lo, hi


def _to_span_layout(x):
    s = x.shape[0]
    return x.reshape(s // SPAN, BLK, NRES, -1).transpose(0, 2, 1, 3).reshape(s, -1)


def _from_span_layout(x):
    s = x.shape[0]
    return x.reshape(s // SPAN, NRES, BLK, -1).transpose(0, 2, 1, 3).reshape(s, -1)


def _row_chains(n_rows):
    part = n_rows // ROW_CHAINS
    return [slice(c * part, (c + 1) * part) for c in range(ROW_CHAINS)]


def _inproj_body(h_ref, g_ref, w_ref, o_ref):
    for rows in _row_chains(h_ref.shape[0]):
        xn = _rms(h_ref[rows, :], g_ref[...]).astype(BF16)
        o_ref[rows, :] = _dot(xn, w_ref[...]).astype(BF16)


def _inproj(h, g, w):
    s, d = h.shape
    n = w.shape[1]
    tm = min(ROW_TILE, s)
    return pl.pallas_call(
        _inproj_body,
        out_shape=jax.ShapeDtypeStruct((s, n), BF16),
        grid=(s // tm,),
        in_specs=[pl.BlockSpec((tm, d), lambda i: (i, 0)),
                  pl.BlockSpec((1, d), lambda i: (0, 0)),
                  pl.BlockSpec((d, n), lambda i: (0, 0))],
        out_specs=pl.BlockSpec((tm, n), lambda i: (i, 0)),
        compiler_params=_params(("parallel",)),
        name="inproj",
    )(h, g.reshape(1, d), w)


def _combine_inproj_body(h_ref, y0_ref, y1_ref, y2_ref, y3_ref, gate_ref, g_ref, w_ref, h_out, o_ref):
    for rows in _row_chains(h_ref.shape[0]):
        h = _combined(h_ref.at[rows, :], [y.at[rows, :] for y in (y0_ref, y1_ref, y2_ref, y3_ref)],
                      gate_ref.at[rows, :])
        h_out[rows, :] = h
        o_ref[rows, :] = _dot(_rms(h, g_ref[...]).astype(BF16), w_ref[...]).astype(BF16)


def _combine_inproj(h, y, gates, g, w):
    s, d = h.shape
    n = w.shape[1]
    tm = min(ROW_TILE, s)
    nt = s // tm
    return pl.pallas_call(
        _combine_inproj_body,
        out_shape=[jax.ShapeDtypeStruct((s, d), F32), jax.ShapeDtypeStruct((s, n), BF16)],
        grid=(nt,),
        in_specs=[pl.BlockSpec((tm, d), lambda i: (i, 0))]
                 + [pl.BlockSpec((tm, d // 2), lambda i, k=k: (k * nt + i, 0)) for k in range(TOP_K)]
                 + [pl.BlockSpec((tm, TOP_K), lambda i: (i, 0)), pl.BlockSpec((1, d), lambda i: (0, 0)),
                    pl.BlockSpec((d, n), lambda i: (0, 0))],
        out_specs=[pl.BlockSpec((tm, d), lambda i: (i, 0)), pl.BlockSpec((tm, n), lambda i: (i, 0))],
        compiler_params=_params(("parallel",)),
        name="combine_inproj",
    )(h, y, y, y, y, gates, g.reshape(1, d), w)


def _t5_bucket(n):
    max_exact = NUM_BUCKETS // 2
    nf = jnp.maximum(n, 1).astype(F32)
    large = max_exact + (jnp.log(nf / max_exact) / math.log(REL_MAX_DIST / max_exact)
                         * (NUM_BUCKETS - max_exact)).astype(jnp.int32)
    large = jnp.minimum(large, NUM_BUCKETS - 1)
    return jnp.where(n < max_exact, n, large)


def _attn_bias(rel_bias, window, dil, perm):
    steps = window // dil
    perm = jnp.asarray(perm, jnp.int32)
    qi = perm[:, None]
    ki = jnp.concatenate([perm, BLK + perm])[None, :]
    dist = BLK + qi - ki
    in_win = (dist >= 0) & (dist <= steps)
    bucket = _t5_bucket(jnp.clip(dist, 0, steps) * dil)
    onehot = (bucket[:, :, None] == jnp.arange(NUM_BUCKETS, dtype=jnp.int32)).astype(F32)
    bias = jnp.einsum('qkb,bh->hqk', onehot, rel_bias.astype(F32), precision=lax.Precision.HIGHEST)
    bias = jnp.where(in_win[None], bias * LOG2E, NEG_INF)
    return bias.reshape(H_A // 2, 2 * BLK, 2 * BLK)


_PERM_D1 = [NRES * jl + r for r in range(NRES) for jl in range(BLK // NRES)]
_PERM_D4 = [4 * jl + i for i in range(4) for jl in range(BLK // 4)]
_PERM_D16 = list(range(BLK))


def _attn_body(q_ref, k_ref, v_ref, kp_ref, vp_ref, b1_ref, b4_ref, b16_ref, o_ref, acc, mst, lst):
    has_prev = pl.program_id(0) > 0
    lane = lax.broadcasted_iota(jnp.int32, (1, LANES), 1)
    lo = lane < HD_A
    mlo = lo.astype(BF16)
    mhi = (~lo).astype(BF16)
    col = lax.broadcasted_iota(jnp.int32, (2 * BLK, 2 * BLK), 1)
    ones = jnp.ones((2 * BLK, LANES), BF16)

    def tile(q2, kk, vv, bias, mask_prev):
        qs = jnp.concatenate([q2 * mlo, q2 * mhi], axis=0)
        s = _dot_nt(qs, kk) + bias
        if mask_prev:
            s = jnp.where(jnp.logical_or(has_prev, col >= BLK), s, NEG_INF)
        m = jnp.max(s, axis=-1, keepdims=True)
        e = jnp.exp2((s - m).astype(BF16))
        oa = _dot(e, jnp.concatenate([vv, ones], axis=1))
        o = oa[:, :LANES]
        l = oa[:, LANES:]
        return (jnp.where(lo, m[:BLK], m[BLK:]), jnp.where(lo, l[:BLK], l[BLK:]),
                jnp.where(lo, o[:BLK], o[BLK:]))

    def merge(prev, cur):
        mp, lp, ap = prev
        mc, lc, ac = cur
        mn = jnp.maximum(mp, mc)
        a = jnp.exp2(mp - mn)
        b = jnp.exp2(mc - mn)
        return mn, a * lp + b * lc, a * ap + b * ac

    def cat(xs):
        return jnp.concatenate(xs, axis=0)


    for r in range(NRES):
        rows = pl.ds(r * BLK, BLK)
        for hp in range(H_A // 2):
            lanes = slice(hp * LANES, (hp + 1) * LANES)
            kk = cat([kp_ref[rows, lanes], k_ref[rows, lanes]])
            vv = cat([vp_ref[rows, lanes], v_ref[rows, lanes]])
            m2, l2, o2 = tile(q_ref[rows, lanes], kk, vv, b16_ref[hp], True)
            mst[rows, lanes] = m2
            lst[rows, lanes] = l2
            acc[rows, lanes] = o2

    for r4 in range(4):
        for b in range(4):
            def chunk_rows(bb):
                return [pl.ds(4 * BLK * i + BLK * r4 + 32 * bb, 32) for i in range(4)]
            rows = chunk_rows(b)
            prows = chunk_rows(3 if b == 0 else b - 1)
            kprev, vprev = (kp_ref, vp_ref) if b == 0 else (k_ref, v_ref)
            for hp in range(H_A // 2):
                lanes = slice(hp * LANES, (hp + 1) * LANES)
                q2 = cat([q_ref[rr, lanes] for rr in rows])
                kk = cat([kprev[rr, lanes] for rr in prows] + [k_ref[rr, lanes] for rr in rows])
                vv = cat([vprev[rr, lanes] for rr in prows] + [v_ref[rr, lanes] for rr in rows])
                cur = tile(q2, kk, vv, b4_ref[hp], b == 0)
                prev = (cat([mst[rr, lanes] for rr in rows]), cat([lst[rr, lanes] for rr in rows]),
                        cat([acc[rr, lanes] for rr in rows]))
                mn, ln, an = merge(prev, cur)
                for i, rr in enumerate(rows):
                    part = slice(32 * i, 32 * (i + 1))
                    mst[rr, lanes] = mn[part]
                    lst[rr, lanes] = ln[part]
                    acc[rr, lanes] = an[part]

    def d1_pair(ap, kprev, vprev, prev_ap, mask_prev):
        def tiles(a_):
            return [pl.ds(BLK * r + 16 * a_, 16) for r in range(NRES)]
        cur_t = tiles(ap)
        prev_t = tiles(prev_ap)

        def halves(ref, ts, lanes):
            xs = [ref[t, lanes].astype(F32) for t in ts]
            return cat([x[:8] for x in xs]).astype(BF16), cat([x[8:] for x in xs]).astype(BF16)

        for hp in range(H_A // 2):
            lanes = slice(hp * LANES, (hp + 1) * LANES)
            q_e, q_o = halves(q_ref, cur_t, lanes)
            k_e, k_o = halves(k_ref, cur_t, lanes)
            v_e, v_o = halves(v_ref, cur_t, lanes)
            _, k_p = halves(kprev, prev_t, lanes)
            _, v_p = halves(vprev, prev_t, lanes)
            cur_e = tile(q_e, cat([k_p, k_e]), cat([v_p, v_e]), b1_ref[hp], mask_prev)
            cur_o = tile(q_o, cat([k_e, k_o]), cat([v_e, v_o]), b1_ref[hp], False)
            ms = [mst[t, lanes] for t in cur_t]
            ls = [lst[t, lanes] for t in cur_t]
            ac = [acc[t, lanes] for t in cur_t]
            outs = []
            for half, cur in ((0, cur_e), (1, cur_o)):
                part = slice(8 * half, 8 * half + 8)
                prev = (cat([x[part] for x in ms]), cat([x[part] for x in ls]), cat([x[part] for x in ac]))
                _, ln, an = merge(prev, cur)
                outs.append(an / ln)
            for r, t in enumerate(cur_t):
                part = slice(8 * r, 8 * r + 8)
                o_ref[t, lanes] = cat([outs[0][part], outs[1][part]]).astype(o_ref.dtype)

    d1_pair(0, kp_ref, vp_ref, BLK // 16 - 1, True)
    for ap in range(1, BLK // 16):
        d1_pair(ap, k_ref, v_ref, ap - 1, False)


def _dilated_attention(proj, biases):
    s = proj.shape[0]
    cur = lambda which: pl.BlockSpec((SPAN, D_A), lambda c: (c, which))
    prev = lambda which: pl.BlockSpec((SPAN, D_A), lambda c: (jnp.maximum(c - 1, 0), which))
    return pl.pallas_call(
        _attn_body,
        out_shape=jax.ShapeDtypeStruct((s, D_A), BF16),
        grid=(s // SPAN,),
        in_specs=[cur(0), cur(1), cur(2), prev(1), prev(2)] + [_full(b) for b in biases],
        out_specs=pl.BlockSpec((SPAN, D_A), lambda c: (c, 0)),
        scratch_shapes=[pltpu.VMEM((SPAN, D_A), F32)] * 3,
        compiler_params=_params(("arbitrary",)),
        name="attn",
    )(proj, proj, proj, proj, proj, *biases)


def _s5_operators(a_re, a_im, b_re, b_im, c_re, c_im, log_dt, d_skip):
    L = S5_CHUNK
    lam = lax.complex(a_re.astype(F32), a_im.astype(F32))
    dt = jnp.exp(log_dt.astype(F32))[:, None]
    a_bar = jnp.exp(lam * dt)
    b_bar = ((a_bar - 1.0) / lam)[..., None] * lax.complex(b_re.astype(F32), b_im.astype(F32))
    c = lax.complex(c_re.astype(F32), c_im.astype(F32))
    j = jnp.arange(L + 1, dtype=F32)
    log_a = lam * dt
    apow = jnp.exp(log_a[None] * j[:, None, None])
    kt = jnp.einsum('gdp,jgp,gpc->gcjd', c, apow[:L], b_bar).real
    skip = d_skip.astype(F32).reshape(S5_G, S5_CH, 1, 1) * jnp.eye(S5_CH)[None, :, None, :]
    kt = (kt + skip * (jnp.arange(L) == 0)[None, None, :, None]).reshape(S5_G, S5_CH, L * S5_CH)
    p = jnp.einsum('sgp,gpc->gscp', apow[:L][::-1], b_bar).reshape(S5_G, L * S5_CH, S5_P)
    ca = jnp.einsum('gdp,tgp->gptd', c, apow[1:L + 1]).reshape(S5_G, S5_P, L * S5_CH)
    a_l = apow[L]

    def pair_blocks(x):
        g, r, w = x.shape
        x = x.reshape(S5_PAIRS, 2, r, w)
        z = jnp.zeros_like(x[:, 0])
        top = jnp.concatenate([x[:, 0], z], axis=-1)
        bot = jnp.concatenate([z, x[:, 1]], axis=-1)
        return jnp.concatenate([top, bot], axis=1)

    p2 = jnp.concatenate([pair_blocks(p.real), pair_blocks(p.imag)], axis=-1)
    q2 = jnp.concatenate([pair_blocks(ca.real), pair_blocks(-ca.imag)], axis=1)
    a_lr = a_l.real.reshape(1, S5_G * S5_P)
    a_li = a_l.imag.reshape(1, S5_G * S5_P)
    return kt, p2.astype(BF16), q2.astype(BF16), a_lr, a_li


def _s5_body(u_ref, kt_ref, p_ref, q_ref, ar_ref, ai_ref, o_ref, m_ref, w_ref, y_ref,
             ere, eim, xre, xim, sre, sim):
    rows = BLK
    gw = S5_CHUNK * S5_CH
    pw = 2 * gw
    per_vreg = LANES // S5_CH
    chunk_of_lane = lax.broadcasted_iota(jnp.int32, (rows, LANES), 1) // S5_CH

    def chunk_transpose(arrs):
        arrs = list(arrs)
        for s in (4, 2, 1):
            upper = (chunk_of_lane & s) != 0
            nxt = list(arrs)
            for i in range(per_vreg):
                if i & s:
                    continue
                lo_a, hi_a = arrs[i], arrs[i + s]
                nxt[i] = jnp.where(upper, pltpu.roll(hi_a, s * S5_CH, axis=1), lo_a)
                nxt[i + s] = jnp.where(upper, hi_a, pltpu.roll(lo_a, LANES - s * S5_CH, axis=1))
            arrs = nxt
        return arrs

    @pl.when(pl.program_id(0) == 0)
    def _():
        sre[...] = jnp.zeros_like(sre)
        sim[...] = jnp.zeros_like(sim)
        lane = lax.broadcasted_iota(jnp.int32, (S5_CH, gw), 1)

        def build(g, _):
            kt = kt_ref[g]
            for s in range(S5_CHUNK):
                blk = kt if s == 0 else jnp.where(lane >= s * S5_CH, pltpu.roll(kt, s * S5_CH, axis=1), 0.0)
                m_ref[g, s * S5_CH:(s + 1) * S5_CH, :] = blk.astype(BF16)
            return 0

        lax.fori_loop(0, S5_G, build, 0)

    for b in range(D_B // LANES):
        for a in range(S5_CHUNK // per_vreg):
            srcs = [u_ref[(per_vreg * a + i) * rows:(per_vreg * a + i + 1) * rows,
                          b * LANES:(b + 1) * LANES].astype(F32) for i in range(per_vreg)]
            for gi, arr in enumerate(chunk_transpose(srcs)):
                g = per_vreg * b + gi
                w_ref[:, g * gw + a * LANES:g * gw + (a + 1) * LANES] = arr.astype(BF16)

    for pr in range(S5_PAIRS):
        e = _dot(w_ref[:, pr * pw:(pr + 1) * pw], p_ref[pr])
        ere[:, pr * LANES:(pr + 1) * LANES] = e[:, :LANES]
        eim[:, pr * LANES:(pr + 1) * LANES] = e[:, LANES:]

    ar = ar_ref[...]
    ai = ai_ref[...]

    def step(n, carry):
        xr, xi = carry
        xre[pl.ds(n, 1), :] = xr
        xim[pl.ds(n, 1), :] = xi
        nr = ar * xr - ai * xi + ere[pl.ds(n, 1), :]
        ni = ar * xi + ai * xr + eim[pl.ds(n, 1), :]
        return nr, ni

    xr, xi = lax.fori_loop(0, rows, step, (sre[...], sim[...]))
    sre[...] = xr
    sim[...] = xi

    for pr in range(S5_PAIRS):
        xin = jnp.concatenate([xre[:, pr * LANES:(pr + 1) * LANES],
                               xim[:, pr * LANES:(pr + 1) * LANES]], axis=-1).astype(BF16)
        yc = _dot(xin, q_ref[pr])
        for half in range(2):
            g = 2 * pr + half
            cols = slice(g * gw, (g + 1) * gw)
            y = _dot(w_ref[:, cols], m_ref[g]) + yc[:, half * gw:(half + 1) * gw]
            y_ref[:, cols] = 0.5 * y * (1.0 + lax.erf(y * (2.0 ** -0.5)))

    for b in range(D_B // LANES):
        for a in range(S5_CHUNK // per_vreg):
            srcs = [y_ref[:, (per_vreg * b + i) * gw + a * LANES:(per_vreg * b + i) * gw + (a + 1) * LANES]
                    for i in range(per_vreg)]
            for ri, arr in enumerate(chunk_transpose(srcs)):
                r = per_vreg * a + ri
                o_ref[r * rows:(r + 1) * rows, b * LANES:(b + 1) * LANES] = arr.astype(BF16)


def _s5_core(proj, ops):
    kt, p2, q2, a_lr, a_li = ops
    s = proj.shape[0]
    gw = S5_CHUNK * S5_CH
    wide = S5_G * gw
    nstate = S5_G * S5_P
    return pl.pallas_call(
        _s5_body,
        out_shape=jax.ShapeDtypeStruct((s, D_B), BF16),
        grid=(s // SPAN,),
        in_specs=[pl.BlockSpec((SPAN, D_B), lambda i: (i, 3 * D_A // D_B)),
                  _full(kt), _full(p2), _full(q2), _full(a_lr), _full(a_li)],
        out_specs=pl.BlockSpec((SPAN, D_B), lambda i: (i, 0)),
        scratch_shapes=[pltpu.VMEM((S5_G, gw, gw), BF16), pltpu.VMEM((BLK, wide), BF16),
                        pltpu.VMEM((BLK, wide), F32)]
                       + [pltpu.VMEM((BLK, nstate), F32)] * 4 + [pltpu.VMEM((1, nstate), F32)] * 2,
        compiler_params=_params(("arbitrary",)),
        name="s5",
    )(proj, kt, p2, q2, a_lr, a_li)


def _split_bf16(x):
    hi = x.astype(BF16)
    lo = (x - hi.astype(F32)).astype(BF16)
    return hi, lo


def _mid_body(h_ref, ya_ref, yb_ref, wglu_ref, bglu_ref, ga_ref, gb_ref, wout_ref, gx_ref, wq_ref,
              k_ref, v_ref, wo_ref, gm_ref, wr_ref, br_ref, h_out, xn_out, logit_out):
    part = h_ref.shape[0] // MID_CHAINS
    for c in range(MID_CHAINS):
        rows = slice(c * part, (c + 1) * part)
        outs = _mid_rows(h_ref[rows, :], ya_ref[rows, :], yb_ref[rows, :], wglu_ref, bglu_ref, ga_ref, gb_ref,
                         wout_ref, gx_ref, wq_ref, k_ref, v_ref, wo_ref, gm_ref, wr_ref, br_ref)
        for ref, val in zip((h_out, xn_out, logit_out), outs):
            ref[rows, :] = val


def _mid_rows(h, ya, yb, wglu_ref, bglu_ref, ga_ref, gb_ref, wout_ref, gx_ref, wq_ref,
              k_ref, v_ref, wo_ref, gm_ref, wr_ref, br_ref):
    gate = jax.nn.sigmoid(_dot(yb, wglu_ref[...]) + bglu_ref[...])
    yb2 = yb.astype(F32) * gate
    na = _rms(ya.astype(F32), ga_ref[...]).astype(BF16)
    nb = _rms(yb2, gb_ref[...]).astype(BF16)
    h1 = h + _dot(na, wout_ref[0:D_A, :]) + _dot(nb, wout_ref[D_A:D_MODEL, :])
    q = _dot(_rms(h1, gx_ref[...]).astype(BF16), wq_ref[...]).astype(BF16)
    heads = []
    for hd in range(H_X):
        lanes = slice(hd * HD_X, (hd + 1) * HD_X)
        s = _dot_nt(q[:, lanes], k_ref[:, lanes])
        e = jnp.exp(s - jnp.max(s, axis=-1, keepdims=True))
        heads.append(_dot(e.astype(BF16), v_ref[:, lanes]) / jnp.sum(e, axis=-1, keepdims=True))
    o = jnp.concatenate(heads, axis=-1).astype(BF16)
    h2 = h1 + _dot(o, wo_ref[...])
    xn = _rms(h2, gm_ref[...])
    x_hi, x_lo = _split_bf16(xn)
    part = _dot(x_hi, wr_ref[...]) + _dot(x_lo, wr_ref[...])
    logits = part + pltpu.roll(part, LANES - N_EXPERTS, axis=1) + br_ref[...]
    return h2, _pack_rows(xn), logits


def _mid(h, ya, yb, wglu, bglu, ga, gb, wout, gx, wq, kmem, vmem, wo, gm, wr2, br):
    s = h.shape[0]
    tm = min(MID_CHAINS * ROW_TILE // 2, s)
    row = lambda w: pl.BlockSpec((tm, w), lambda i: (i, 0))
    consts = [wglu, bglu, ga, gb, wout, gx, wq, kmem, vmem, wo, gm, wr2, br]
    return pl.pallas_call(
        _mid_body,
        out_shape=[jax.ShapeDtypeStruct((s, D_MODEL), F32), jax.ShapeDtypeStruct((s, D_MODEL // 2), jnp.int32),
                   jax.ShapeDtypeStruct((s, LANES), F32)],
        grid=(s // tm,),
        in_specs=[row(D_MODEL), row(D_A), row(D_B)] + [_full(a) for a in consts],
        out_specs=[row(D_MODEL), row(D_MODEL // 2), row(LANES)],
        compiler_params=_params(("parallel",)),
        name="mid",
    )(h, ya, yb, *consts)


def _memkv_body(mem_ref, g_ref, w_ref, o_ref):
    o_ref[...] = _dot(_rms(mem_ref[...], g_ref[...]).astype(BF16), w_ref[...]).astype(BF16)


def _memkv(mem, g, w):
    n, d = mem.shape
    return pl.pallas_call(
        _memkv_body,
        out_shape=jax.ShapeDtypeStruct((n, w.shape[1]), BF16),
        compiler_params=pltpu.CompilerParams(vmem_limit_bytes=VMEM_LIMIT),
        name="memkv",
    )(mem, g.reshape(1, d), w)


def _rank_body(logit_ref, tri_ref, idx_ref, gate_ref, rank_ref, cnt_ref, carry):
    @pl.when(pl.program_id(0) == 0)
    def _():
        carry[...] = jnp.zeros_like(carry)

    tm = logit_ref.shape[0]
    logits = jnp.transpose(logit_ref[...])[:N_EXPERTS, :]
    expert = lax.broadcasted_iota(jnp.int32, (N_EXPERTS, tm), 0)
    vals, idxs = [], []
    for _ in range(TOP_K):
        mx = jnp.max(logits, axis=0, keepdims=True)
        ix = jnp.min(jnp.where(logits == mx, expert, N_EXPERTS), axis=0, keepdims=True)
        vals.append(mx)
        idxs.append(ix)
        logits = jnp.where(expert == ix, -jnp.inf, logits)
    es = [jnp.exp(v - vals[0]) for v in vals]
    den = es[0] + es[1] + es[2] + es[3]
    hits = [expert == ix for ix in idxs]
    onehot = jnp.zeros((N_EXPERTS, tm), F32)
    for hit in hits:
        onehot = onehot + jnp.where(hit, 1.0, 0.0)
    nb = tm // LANES
    blocks = jnp.concatenate([onehot[:, b * LANES:(b + 1) * LANES] for b in range(nb)], axis=0)
    inc = _dot(blocks.astype(BF16), tri_ref[...])
    run = carry[...][:, 0:1]
    before = []
    for b in range(nb):
        inc_b = inc[b * N_EXPERTS:(b + 1) * N_EXPERTS, :]
        before.append(run + inc_b - onehot[:, b * LANES:(b + 1) * LANES])
        run = run + inc_b[:, LANES - 1:LANES]
    before = jnp.concatenate(before, axis=1)
    choice = lax.broadcasted_iota(jnp.int32, (SUBLANES, tm), 0)
    idx_t = jnp.full((SUBLANES, tm), N_EXPERTS, jnp.int32)
    gate_t = jnp.zeros((SUBLANES, tm), F32)
    rank_t = jnp.zeros((SUBLANES, tm), jnp.int32)
    for k in range(TOP_K):
        rk = jnp.sum(jnp.where(hits[k], before, 0.0), axis=0, keepdims=True)
        idx_t = jnp.where(choice == k, idxs[k], idx_t)
        gate_t = jnp.where(choice == k, es[k] / den, gate_t)
        rank_t = jnp.where(choice == k, rk.astype(jnp.int32), rank_t)
    idx_ref[...] = idx_t
    gate_ref[...] = gate_t
    rank_ref[...] = rank_t
    total = jnp.broadcast_to(run, carry.shape)
    carry[...] = total
    cnt_ref[...] = total.astype(jnp.int32)


def _rank(logits):
    t = logits.shape[0]
    tm = min(RANK_TILE, t)
    cols = pl.BlockSpec((SUBLANES, tm), lambda i: (0, i))
    tri = jnp.asarray(np.triu(np.ones((LANES, LANES), np.float32)), BF16)
    return pl.pallas_call(
        _rank_body,
        out_shape=[jax.ShapeDtypeStruct((SUBLANES, t), jnp.int32), jax.ShapeDtypeStruct((SUBLANES, t), F32),
                   jax.ShapeDtypeStruct((SUBLANES, t), jnp.int32),
                   jax.ShapeDtypeStruct((N_EXPERTS, LANES), jnp.int32)],
        grid=(t // tm,),
        in_specs=[pl.BlockSpec((tm, LANES), lambda i: (i, 0)), _full(tri)],
        out_specs=[cols, cols, cols, pl.BlockSpec((N_EXPERTS, LANES), lambda i: (0, 0))],
        scratch_shapes=[pltpu.VMEM((N_EXPERTS, LANES), F32)],
        compiler_params=_params(("arbitrary",)),
        name="rank",
    )(logits, tri)


def _route(logits, tm):
    t = logits.shape[0]
    tk = t * TOP_K
    idx, gates, rank, cnt = _rank(logits)
    counts = cnt[:, 0]
    padded = (counts + tm - 1) // tm * tm
    pend = jnp.cumsum(padded)
    pstart = pend - padded
    n_rows = tk + N_EXPERTS * tm
    n_tiles = n_rows // tm
    experts = jnp.arange(N_EXPERTS, dtype=jnp.int32)
    tile_first = jnp.arange(n_tiles, dtype=jnp.int32) * tm
    last_used = jnp.max(jnp.where(padded > 0, experts, 0))
    tile_e = jnp.minimum(jnp.sum(tile_first[:, None] >= pend[None, :], axis=1), last_used).astype(jnp.int32)
    tile_rows = jnp.clip(jnp.sum(jnp.where(tile_e[:, None] == experts[None, :],
                                           (pstart + counts)[None, :], 0), axis=1) - tile_first, 0, tm)
    tile_rows = jnp.where(tile_first < pend[-1], tile_rows, 0).astype(jnp.int32)
    group = jnp.cumsum(jnp.concatenate([jnp.zeros((1,), jnp.int32),
                                        (tile_e[1:] != tile_e[:-1]).astype(jnp.int32)]))
    tile_slot = (group % 2).astype(jnp.int32)
    later = (experts[None, :] > experts[:, None]) & (padded > 0)[None, :]
    next_e = jnp.min(jnp.where(later, experts[None, :], N_EXPERTS), axis=1)
    next_e = jnp.where(next_e < N_EXPERTS, next_e, -1).astype(jnp.int32)
    tile_next = jnp.sum(jnp.where(tile_e[:, None] == experts[None, :], next_e[None, :], 0), axis=1).astype(jnp.int32)
    base = jnp.sum(jnp.where(idx[:TOP_K, :, None] == experts, pstart, 0), axis=-1)
    dest = rank[:TOP_K] + base
    return gates[:TOP_K].T, dest, n_rows, tile_e, tile_rows, tile_slot, tile_next


def _sc_mesh():
    return plsc.VectorSubcoreMesh(core_axis_name="c", subcore_axis_name="s",
                                  num_cores=SC_CORES, num_subcores=SC_SUBCORES)


def _sc_worker():
    return lax.axis_index("s") * SC_CORES + lax.axis_index("c")


def _scatter_rows(x, dest, n_rows):
    t, d = x.shape
    per_worker = t // (SC_CORES * SC_SUBCORES)
    chunks = per_worker // SC_ROWS

    assert chunks % 2 == 0

    @functools.partial(
        pl.kernel, mesh=_sc_mesh(),
        out_type=jax.ShapeDtypeStruct((n_rows, d), x.dtype),
        scratch_types=[pltpu.VMEM((SC_ROWS, d), x.dtype)] * 2 + [pltpu.VMEM((SC_ROWS,), jnp.int32)] * (2 * TOP_K)
                      + [pltpu.SemaphoreType.DMA((2,)), pltpu.SemaphoreType.DMA((2, TOP_K))],
    )
    def scatter(x_hbm, *rest):
        dest_hbm, out_hbm = rest[:TOP_K], rest[TOP_K]
        rows_v = rest[TOP_K + 1:TOP_K + 3]
        idx_v = (rest[TOP_K + 3:2 * TOP_K + 3], rest[2 * TOP_K + 3:3 * TOP_K + 3])
        lsem, ssem = rest[3 * TOP_K + 3], rest[3 * TOP_K + 4]
        base = _sc_worker() * per_worker

        def rows_at(c):
            return pl.ds(pl.multiple_of(base + c * SC_ROWS, SC_ROWS), SC_ROWS)

        def loaded(c, s):
            return pltpu.make_async_copy(x_hbm.at[rows_at(c)], rows_v[s], lsem.at[s])

        def load(c, s):
            loaded(c, s).start()
            for k in range(TOP_K):
                pltpu.sync_copy(dest_hbm[k].at[rows_at(c)], idx_v[s][k])

        def scattered(s):
            return [pltpu.make_async_copy(rows_v[s], out_hbm.at[idx_v[s][k]], ssem.at[s, k]) for k in range(TOP_K)]

        load(0, 0)

        @pl.loop(0, chunks, step=2)
        def _(c0):
            for s in range(2):
                c = c0 + s
                loaded(c, s).wait()
                for cp in scattered(s):
                    cp.start()

                @pl.when(c >= 1)
                def _():
                    for cp in scattered(1 - s):
                        cp.wait()

                @pl.when(c + 1 < chunks)
                def _():
                    load(c + 1, 1 - s)

        for cp in scattered(1):
            cp.wait()

    return scatter(x, *[dest[k] for k in range(TOP_K)])


def _gather_rows(table, idx):
    n, d = table.shape
    b = idx.shape[0]
    per_worker = b // (SC_CORES * SC_SUBCORES)
    chunks = per_worker // SC_ROWS
    assert chunks % 2 == 0

    @functools.partial(
        pl.kernel, mesh=_sc_mesh(),
        out_type=jax.ShapeDtypeStruct((b, d), table.dtype),
        scratch_types=[pltpu.VMEM((chunks, SC_ROWS), jnp.int32)] + [pltpu.VMEM((SC_ROWS, d), table.dtype)] * 2
                      + [pltpu.SemaphoreType.DMA((2,)), pltpu.SemaphoreType.DMA((2,))],
    )
    def gather(table_hbm, idx_hbm, out_hbm, idx_v, rows0, rows1, gsem, wsem):
        rows_v = (rows0, rows1)
        worker = _sc_worker()
        base = worker * per_worker
        pltpu.sync_copy(idx_hbm.at[pl.ds(pl.multiple_of(worker * chunks, chunks), chunks)], idx_v)

        def rows_at(c):
            return pl.ds(pl.multiple_of(base + c * SC_ROWS, SC_ROWS), SC_ROWS)

        def fetched(c, s):
            return pltpu.make_async_copy(table_hbm.at[idx_v.at[c]], rows_v[s], gsem.at[s])

        def written(c, s):
            return pltpu.make_async_copy(rows_v[s], out_hbm.at[rows_at(c)], wsem.at[s])

        fetched(0, 0).start()

        @pl.loop(0, chunks, step=2)
        def _(c0):
            for s in range(2):
                c = c0 + s
                fetched(c, s).wait()
                written(c, s).start()

                @pl.when(c >= 1)
                def _():
                    written(c - 1, 1 - s).wait()

                @pl.when(c + 1 < chunks)
                def _():
                    fetched(c + 1, 1 - s).start()

        written(chunks - 1, 1).wait()

    return gather(table, idx.reshape(b // SC_ROWS, SC_ROWS))


def _moe_body(te_ref, tv_ref, sl_ref, nx_ref, x_ref, w1_hbm, b1_ref, w2_hbm, b2_ref, o_ref,
              w1f, w2f, w1b, w2b, sem, *, layer):
    i = pl.program_id(0)
    e = te_ref[i]
    slot = sl_ref[i]
    new_expert = (i == 0) | (e != te_ref[jnp.maximum(i - 1, 0)])

    def weight_copies(expert, s):
        rows1 = w1f.shape[1] // MOE_W1_PARTS
        rows2 = w2f.shape[1] // MOE_W2_PARTS
        c1 = [pltpu.make_async_copy(w1_hbm.at[layer, expert, pl.ds(q * rows1, rows1)],
                                    w1f.at[s, pl.ds(q * rows1, rows1)], sem.at[s, q])
              for q in range(MOE_W1_PARTS)]
        c2 = [pltpu.make_async_copy(w2_hbm.at[layer, expert, pl.ds(q * rows2, rows2)],
                                    w2f.at[s, pl.ds(q * rows2, rows2)], sem.at[s, MOE_W1_PARTS + q])
              for q in range(MOE_W2_PARTS)]
        return c1 + c2

    @pl.when(i == 0)
    def _():
        for c in weight_copies(e, slot):
            c.start()

    @pl.when(new_expert)
    def _():
        for c in weight_copies(e, slot):
            c.wait()
        nxt = nx_ref[i]

        @pl.when(nxt >= 0)
        def _():
            for c in weight_copies(nxt, 1 - slot):
                c.start()

        w1b[...] = w1f[slot].astype(BF16)
        w2b[...] = w2f[slot].astype(BF16)

    def expert(rows):
        row = lax.broadcasted_iota(jnp.int32, (rows, x_ref.shape[1]), 0)
        lo, hi = _unpack_rows(jnp.where(row < tv_ref[i], x_ref[0:rows, :], 0))
        x = jnp.concatenate([lo, hi], axis=-1).astype(BF16)
        hb = _dot(x, w1b[...]) + b1_ref[0]
        x_glu = jnp.minimum(hb[:, :D_FF], SWIGLU_LIMIT)
        x_lin = jnp.clip(hb[:, D_FF:], -SWIGLU_LIMIT, SWIGLU_LIMIT)
        act = x_glu * jax.nn.sigmoid(SWIGLU_ALPHA * x_glu) * (x_lin + 1.0)
        o_ref[0:rows, :] = _pack_rows(_dot(act.astype(BF16), w2b[...]) + b2_ref[0])

    step = x_ref.shape[0] // MOE_ROW_PATHS
    for path in range(1, MOE_ROW_PATHS + 1):
        rows = path * step

        @pl.when((tv_ref[i] > rows - step) & (tv_ref[i] <= rows))
        def _(rows=rows):
            expert(rows)
            if rows < x_ref.shape[0]:
                o_ref[rows:, :] = jnp.zeros((x_ref.shape[0] - rows, o_ref.shape[1]), o_ref.dtype)

    @pl.when(tv_ref[i] == 0)
    def _():
        o_ref[...] = jnp.zeros_like(o_ref)


def _moe_experts(xs, tile_e, tile_rows, tile_slot, tile_next, w1, b1, w2, b2, layer):
    n_rows = xs.shape[0]
    tm = MOE_TILE
    nl, ne, d, ff2 = w1.shape
    bias_map = lambda i, te, tv, sl, nx: (layer, te[i], 0, 0)
    grid_spec = pltpu.PrefetchScalarGridSpec(
        num_scalar_prefetch=4,
        grid=(n_rows // tm,),
        in_specs=[pl.BlockSpec((tm, d // 2), lambda i, *_: (i, 0)),
                  pl.BlockSpec(memory_space=pl.ANY),
                  pl.BlockSpec((None, 1, 1, ff2), bias_map),
                  pl.BlockSpec(memory_space=pl.ANY),
                  pl.BlockSpec((None, 1, 1, d), bias_map)],
        out_specs=pl.BlockSpec((tm, d // 2), lambda i, *_: (i, 0)),
        scratch_shapes=[pltpu.VMEM((2, d, ff2), F32), pltpu.VMEM((2, ff2 // 2, d), F32),
                        pltpu.VMEM((d, ff2), BF16), pltpu.VMEM((ff2 // 2, d), BF16),
                        pltpu.SemaphoreType.DMA((2, MOE_W1_PARTS + MOE_W2_PARTS))],
    )
    return pl.pallas_call(
        functools.partial(_moe_body, layer=layer),
        out_shape=jax.ShapeDtypeStruct((n_rows, d // 2), jnp.int32),
        grid_spec=grid_spec,
        compiler_params=_params(("arbitrary",)),
        name="moe",
    )(tile_e, tile_rows, tile_slot, tile_next, xs, w1, b1.reshape(nl, ne, 1, ff2), w2, b2.reshape(nl, ne, 1, d))


def _combined(h_ref, y_refs, gate_ref):
    gates = gate_ref[...]
    lo = jnp.zeros(y_refs[0].shape, F32)
    hi = jnp.zeros(y_refs[0].shape, F32)
    for k, y_ref in enumerate(y_refs):
        yl, yh = _unpack_rows(y_ref[...])
        lo = lo + yl * gates[:, k:k + 1]
        hi = hi + yh * gates[:, k:k + 1]
    return h_ref[...] + jnp.concatenate([lo, hi], axis=-1)


def _combine_final_body(h_ref, y0_ref, y1_ref, y2_ref, y3_ref, gate_ref, g_ref, out_hbm, buf, sem):
    i = pl.program_id(0)
    n = pl.num_programs(0)
    slot = i % 2
    per_step = h_ref.shape[0] // BLK
    steps_per_span = NRES // per_step

    def writes(step, s):
        span = step // steps_per_span
        r0 = (step % steps_per_span) * per_step
        return [pltpu.make_async_copy(buf.at[s, pl.ds(rr * BLK, BLK), :], out_hbm.at[span, :, r0 + rr, :],
                                      sem.at[s, rr]) for rr in range(per_step)]

    @pl.when(i >= 2)
    def _():
        for cp in writes(i - 2, slot):
            cp.wait()

    buf[slot] = _rms(_combined(h_ref, (y0_ref, y1_ref, y2_ref, y3_ref), gate_ref), g_ref[...])
    for cp in writes(i, slot):
        cp.start()

    @pl.when(i == n - 1)
    def _():
        for cp in writes(i, slot):
            cp.wait()

        @pl.when(i >= 1)
        def _():
            for cp in writes(i - 1, 1 - slot):
                cp.wait()


def _combine_final(h, y, gates, g):
    s, d = h.shape
    tm = min(ROW_TILE, s)
    nt = s // tm
    in_specs = ([pl.BlockSpec((tm, d), lambda i: (i, 0))]
                + [pl.BlockSpec((tm, d // 2), lambda i, k=k: (k * nt + i, 0)) for k in range(TOP_K)]
                + [pl.BlockSpec((tm, TOP_K), lambda i: (i, 0))])
    out = pl.pallas_call(
        _combine_final_body,
        out_shape=jax.ShapeDtypeStruct((s // SPAN, BLK, NRES, d), F32),
        grid=(nt,),
        in_specs=in_specs + [pl.BlockSpec((1, d), lambda i: (0, 0))],
        out_specs=pl.BlockSpec(memory_space=pl.ANY),
        scratch_shapes=[pltpu.VMEM((2, tm, d), F32), pltpu.SemaphoreType.DMA((2, tm // BLK))],
        compiler_params=_params(("arbitrary",)),
        name="combine_final",
    )(h, y, y, y, y, gates, g.reshape(1, d))
    return out.reshape(s, d)


def _layer(state, mem, biases, p, l, g_final):
    row = lambda a: a.reshape(1, -1).astype(F32)
    q_scale = jnp.concatenate([jnp.full((D_A,), HD_A ** -0.5 * LOG2E, F32), jnp.ones((D_IN - D_A,), F32)])
    w_in = (p["w_in"][l] * q_scale).astype(BF16)
    if l == 0:
        h, proj = state, _inproj(state, p["norm_mix"][l], w_in)
    else:
        h, proj = _combine_inproj(*state, p["norm_mix"][l], w_in)
    ya = _dilated_attention(proj, biases)
    ops = _s5_operators(p["s5_a_re"][l], p["s5_a_im"][l], p["s5_b_re"][l], p["s5_b_im"][l],
                        p["s5_c_re"][l], p["s5_c_im"][l], p["s5_log_dt"][l], p["s5_d"][l])
    yb = _s5_core(proj, ops)
    kv = _memkv(mem, p["norm_mem"][l], p["w_xkv"][l].astype(BF16))
    wr = p["w_router"][l].astype(F32)
    wr_hi = wr.astype(BF16)
    wr2 = jnp.pad(jnp.concatenate([wr_hi, (wr - wr_hi.astype(F32)).astype(BF16)], axis=1),
                  ((0, 0), (0, LANES - 2 * N_EXPERTS)))
    br = jnp.pad(p["b_router"][l].astype(F32), (0, LANES - N_EXPERTS)).reshape(1, LANES)
    h2, xn, logits = _mid(
        h, ya, yb, p["w_glu"][l].astype(BF16), row(p["b_glu"][l]), row(p["g_out_attn"][l]),
        row(p["g_out_ssm"][l]), p["w_out"][l].astype(BF16), row(p["norm_xattn"][l]),
        (p["w_xq"][l] * (HD_X ** -0.5)).astype(BF16), kv[:, :D_X], kv[:, D_X:],
        p["w_xo"][l].astype(BF16), row(p["norm_moe"][l]), wr2, br)
    gates, dest, n_rows, tile_e, tile_rows, tile_slot, tile_next = _route(logits, MOE_TILE)
    out = _moe_experts(_scatter_rows(xn, dest, n_rows), tile_e, tile_rows, tile_slot, tile_next,
                       p["w1"], p["b1"], p["w2"], p["b2"], l)
    y = _gather_rows(out, dest.reshape(-1))
    return _combine_final(h2, y, gates, g_final) if l == DEPTH - 1 else (h2, y, gates)


def kernel(x, mem, rel_bias, norm_mix, w_in, s5_a_re, s5_a_im, s5_b_re, s5_b_im, s5_c_re, s5_c_im, s5_log_dt, s5_d, w_glu, b_glu, g_out_attn, g_out_ssm, w_out, norm_xattn, norm_mem, w_xq, w_xkv, w_xo, norm_moe, w_router, b_router, w1, b1, w2, b2, norm_final):
    p = dict(norm_mix=norm_mix, w_in=w_in, s5_a_re=s5_a_re, s5_a_im=s5_a_im, s5_b_re=s5_b_re,
             s5_b_im=s5_b_im, s5_c_re=s5_c_re, s5_c_im=s5_c_im, s5_log_dt=s5_log_dt, s5_d=s5_d,
             w_glu=w_glu, b_glu=b_glu, g_out_attn=g_out_attn, g_out_ssm=g_out_ssm, w_out=w_out,
             norm_xattn=norm_xattn, norm_mem=norm_mem, w_xq=w_xq, w_xkv=w_xkv, w_xo=w_xo,
             norm_moe=norm_moe, w_router=w_router, b_router=b_router, w1=w1, b1=b1, w2=w2, b2=b2)
    biases = [_attn_bias(rel_bias, window, dil, perm)
              for (window, dil), perm in zip(WIN_DIL, (_PERM_D1, _PERM_D4, _PERM_D16))]
    outs = []
    for b in range(x.shape[0]):
        h = _to_span_layout(x[b])
        for l in range(DEPTH):
            h = _layer(h, mem[b], biases, p, l, norm_final)
        outs.append(h)
    return jnp.stack(outs)
```

```python
import functools
import math

import jax
import jax.numpy as jnp
import numpy as np
from jax import lax
from jax.experimental import pallas as pl
from jax.experimental.pallas import tpu as pltpu
from jax.experimental.pallas import tpu_sc as plsc

F32 = jnp.float32
BF16 = jnp.bfloat16

D_MODEL = 1024
DEPTH = 2
EPS = 1e-5
NEG_INF = -1e30
LOG2E = math.log2(math.e)
H_A = 8
HD_A = 64
D_A = H_A * HD_A
WIN_DIL = ((128, 1), (512, 4), (2048, 16))
BLK = 128
D_B = D_MODEL - D_A
S5_CH = 16
S5_G = D_B // S5_CH
S5_P = 64
D_IN = 3 * D_A + D_B
NUM_BUCKETS = 32
REL_MAX_DIST = 2048
H_X = 4
HD_X = 128
D_X = H_X * HD_X
N_EXPERTS = 32
TOP_K = 4
D_FF = D_MODEL
SWIGLU_ALPHA = 1.702
SWIGLU_LIMIT = 7.0

LANES = 128
SUBLANES = 8
NRES = WIN_DIL[-1][1]
SPAN = NRES * BLK
S5_CHUNK = NRES
S5_PAIRS = S5_G // 2
VMEM_LIMIT = 56 * 1024 * 1024

SC_CORES = 2
SC_SUBCORES = 16
SC_ROWS = 64

ROW_TILE = 512
ROW_CHAINS = 2
MID_CHAINS = 4
RANK_TILE = 1024
MOE_TILE = 512
MOE_ROW_PATHS = 4
MOE_W1_PARTS = 4
MOE_W2_PARTS = 2


def _params(sem):
    return pltpu.CompilerParams(dimension_semantics=sem, vmem_limit_bytes=VMEM_LIMIT)


def _rms(x, g):
    return x * lax.rsqrt(jnp.mean(x * x, axis=-1, keepdims=True) + EPS) * g


def _dot(a, b):
    return jnp.dot(a, b, preferred_element_type=F32)


def _dot_nt(a, b):
    return lax.dot_general(a, b, (((1,), (1,)), ((), ())), preferred_element_type=F32)


def _full(a):
    return pl.BlockSpec(a.shape, lambda *_: (0,) * a.ndim)


def _pack_rows(x):
    c = x.shape[1] // 2
    lo = lax.bitcast_convert_type(x[:, :c].astype(BF16).astype(F32), jnp.uint32)
    hi = lax.bitcast_convert_type(x[:, c:].astype(BF16).astype(F32), jnp.uint32)
    return lax.bitcast_convert_type(lax.shift_right_logical(lo, jnp.uint32(16)) | hi, jnp.int32)


def _unpack_rows(p):
    u = lax.bitcast_convert_type(p, jnp.uint32)
    lo = lax.bitcast_convert_type(lax.shift_left(u, jnp.uint32(16)), F32)
    hi = lax.bitcast_convert_type(u & jnp.uint32(0xFFFF0000), F32)
    return lo, hi


def _to_span_layout(x):
    s = x.shape[0]
    return x.reshape(s // SPAN, BLK, NRES, -1).transpose(0, 2, 1, 3).reshape(s, -1)


def _from_span_layout(x):
    s = x.shape[0]
    return x.reshape(s // SPAN, NRES, BLK, -1).transpose(0, 2, 1, 3).reshape(s, -1)


def _row_chains(n_rows):
    part = n_rows // ROW_CHAINS
    return [slice(c * part, (c + 1) * part) for c in range(ROW_CHAINS)]


def _inproj_body(h_ref, g_ref, w_ref, o_ref):
    for rows in _row_chains(h_ref.shape[0]):
        xn = _rms(h_ref[rows, :], g_ref[...]).astype(BF16)
        o_ref[rows, :] = _dot(xn, w_ref[...]).astype(BF16)


def _inproj(h, g, w):
    s, d = h.shape
    n = w.shape[1]
    tm = min(ROW_TILE, s)
    return pl.pallas_call(
        _inproj_body,
        out_shape=jax.ShapeDtypeStruct((s, n), BF16),
        grid=(s // tm,),
        in_specs=[pl.BlockSpec((tm, d), lambda i: (i, 0)),
                  pl.BlockSpec((1, d), lambda i: (0, 0)),
                  pl.BlockSpec((d, n), lambda i: (0, 0))],
        out_specs=pl.BlockSpec((tm, n), lambda i: (i, 0)),
        compiler_params=_params(("parallel",)),
        name="inproj",
    )(h, g.reshape(1, d), w)


def _combine_inproj_body(h_ref, y0_ref, y1_ref, y2_ref, y3_ref, gate_ref, g_ref, w_ref, h_out, o_ref):
    chains = _row_chains(h_ref.shape[0])
    hs = [_combined(h_ref.at[rows, :], [y.at[rows, :] for y in (y0_ref, y1_ref, y2_ref, y3_ref)],
                    gate_ref.at[rows, :]) for rows in chains]
    for rows, h in zip(chains, hs):
        h_out[rows, :] = h
    for rows, h in zip(chains, hs):
        o_ref[rows, :] = _dot(_rms(h, g_ref[...]).astype(BF16), w_ref[...]).astype(BF16)


def _combine_inproj(h, y, gates, g, w):
    s, d = h.shape
    n = w.shape[1]
    tm = min(ROW_TILE, s)
    nt = s // tm
    return pl.pallas_call(
        _combine_inproj_body,
        out_shape=[jax.ShapeDtypeStruct((s, d), F32), jax.ShapeDtypeStruct((s, n), BF16)],
        grid=(nt,),
        in_specs=[pl.BlockSpec((tm, d), lambda i: (i, 0))]
                 + [pl.BlockSpec((tm, d // 2), lambda i, k=k: (k * nt + i, 0)) for k in range(TOP_K)]
                 + [pl.BlockSpec((tm, TOP_K), lambda i: (i, 0)), pl.BlockSpec((1, d), lambda i: (0, 0)),
                    pl.BlockSpec((d, n), lambda i: (0, 0))],
        out_specs=[pl.BlockSpec((tm, d), lambda i: (i, 0)), pl.BlockSpec((tm, n), lambda i: (i, 0))],
        compiler_params=_params(("parallel",)),
        name="combine_inproj",
    )(h, y, y, y, y, gates, g.reshape(1, d), w)


def _t5_bucket(n):
    max_exact = NUM_BUCKETS // 2
    nf = jnp.maximum(n, 1).astype(F32)
    large = max_exact + (jnp.log(nf / max_exact) / math.log(REL_MAX_DIST / max_exact)
                         * (NUM_BUCKETS - max_exact)).astype(jnp.int32)
    large = jnp.minimum(large, NUM_BUCKETS - 1)
    return jnp.where(n < max_exact, n, large)


def _attn_bias(rel_bias, window, dil, perm):
    steps = window // dil
    perm = jnp.asarray(perm, jnp.int32)
    qi = perm[:, None]
    ki = jnp.concatenate([perm, BLK + perm])[None, :]
    dist = BLK + qi - ki
    in_win = (dist >= 0) & (dist <= steps)
    bucket = _t5_bucket(jnp.clip(dist, 0, steps) * dil)
    onehot = (bucket[:, :, None] == jnp.arange(NUM_BUCKETS, dtype=jnp.int32)).astype(F32)
    bias = jnp.einsum('qkb,bh->hqk', onehot, rel_bias.astype(F32), precision=lax.Precision.HIGHEST)
    bias = jnp.where(in_win[None], bias * LOG2E, NEG_INF)
    return bias.reshape(H_A // 2, 2 * BLK, 2 * BLK)


_PERM_D1 = [NRES * jl + r for r in range(NRES) for jl in range(BLK // NRES)]
_PERM_D4 = [4 * jl + i for i in range(4) for jl in range(BLK // 4)]
_PERM_D16 = list(range(BLK))


def _attn_body(q_ref, k_ref, v_ref, kp_ref, vp_ref, b1_ref, b4_ref, b16_ref, o_ref, acc, mst, lst):
    has_prev = pl.program_id(0) > 0
    lane = lax.broadcasted_iota(jnp.int32, (1, LANES), 1)
    lo = lane < HD_A
    mlo = lo.astype(BF16)
    mhi = (~lo).astype(BF16)
    col = lax.broadcasted_iota(jnp.int32, (2 * BLK, 2 * BLK), 1)
    ones = jnp.ones((2 * BLK, LANES), BF16)

    def tile(q2, kk, vv, bias, mask_prev):
        qs = jnp.concatenate([q2 * mlo, q2 * mhi], axis=0)
        s = _dot_nt(qs, kk) + bias
        if mask_prev:
            s = jnp.where(jnp.logical_or(has_prev, col >= BLK), s, NEG_INF)
        m = jnp.max(s, axis=-1, keepdims=True)
        e = jnp.exp2((s - m).astype(BF16))
        oa = _dot(e, jnp.concatenate([vv, ones], axis=1))
        o = oa[:, :LANES]
        l = oa[:, LANES:]
        return (jnp.where(lo, m[:BLK], m[BLK:]), jnp.where(lo, l[:BLK], l[BLK:]),
                jnp.where(lo, o[:BLK], o[BLK:]))

    def merge(prev, cur):
        mp, lp, ap = prev
        mc, lc, ac = cur
        mn = jnp.maximum(mp, mc)
        a = jnp.exp2(mp - mn)
        b = jnp.exp2(mc - mn)
        return mn, a * lp + b * lc, a * ap + b * ac

    def cat(xs):
        return jnp.concatenate(xs, axis=0)


    for r in range(NRES):
        rows = pl.ds(r * BLK, BLK)
        for hp in range(H_A // 2):
            lanes = slice(hp * LANES, (hp + 1) * LANES)
            kk = cat([kp_ref[rows, lanes], k_ref[rows, lanes]])
            vv = cat([vp_ref[rows, lanes], v_ref[rows, lanes]])
            m2, l2, o2 = tile(q_ref[rows, lanes], kk, vv, b16_ref[hp], True)
            mst[rows, lanes] = m2
            lst[rows, lanes] = l2
            acc[rows, lanes] = o2

    for r4 in range(4):
        for b in range(4):
            def chunk_rows(bb):
                return [pl.ds(4 * BLK * i + BLK * r4 + 32 * bb, 32) for i in range(4)]
            rows = chunk_rows(b)
            prows = chunk_rows(3 if b == 0 else b - 1)
            kprev, vprev = (kp_ref, vp_ref) if b == 0 else (k_ref, v_ref)
            for hp in range(H_A // 2):
                lanes = slice(hp * LANES, (hp + 1) * LANES)
                q2 = cat([q_ref[rr, lanes] for rr in rows])
                kk = cat([kprev[rr, lanes] for rr in prows] + [k_ref[rr, lanes] for rr in rows])
                vv = cat([vprev[rr, lanes] for rr in prows] + [v_ref[rr, lanes] for rr in rows])
                cur = tile(q2, kk, vv, b4_ref[hp], b == 0)
                prev = (cat([mst[rr, lanes] for rr in rows]), cat([lst[rr, lanes] for rr in rows]),
                        cat([acc[rr, lanes] for rr in rows]))
                mn, ln, an = merge(prev, cur)
                for i, rr in enumerate(rows):
                    part = slice(32 * i, 32 * (i + 1))
                    mst[rr, lanes] = mn[part]
                    lst[rr, lanes] = ln[part]
                    acc[rr, lanes] = an[part]

    def d1_pair(ap, kprev, vprev, prev_ap, mask_prev):
        def tiles(a_):
            return [pl.ds(BLK * r + 16 * a_, 16) for r in range(NRES)]
        cur_t = tiles(ap)
        prev_t = tiles(prev_ap)

        def halves(ref, ts, lanes):
            xs = [ref[t, lanes].astype(F32) for t in ts]
            return cat([x[:8] for x in xs]).astype(BF16), cat([x[8:] for x in xs]).astype(BF16)

        for hp in range(H_A // 2):
            lanes = slice(hp * LANES, (hp + 1) * LANES)
            q_e, q_o = halves(q_ref, cur_t, lanes)
            k_e, k_o = halves(k_ref, cur_t, lanes)
            v_e, v_o = halves(v_ref, cur_t, lanes)
            _, k_p = halves(kprev, prev_t, lanes)
            _, v_p = halves(vprev, prev_t, lanes)
            cur_e = tile(q_e, cat([k_p, k_e]), cat([v_p, v_e]), b1_ref[hp], mask_prev)
            cur_o = tile(q_o, cat([k_e, k_o]), cat([v_e, v_o]), b1_ref[hp], False)
            ms = [mst[t, lanes] for t in cur_t]
            ls = [lst[t, lanes] for t in cur_t]
            ac = [acc[t, lanes] for t in cur_t]
            outs = []
            for half, cur in ((0, cur_e), (1, cur_o)):
                part = slice(8 * half, 8 * half + 8)
                prev = (cat([x[part] for x in ms]), cat([x[part] for x in ls]), cat([x[part] for x in ac]))
                _, ln, an = merge(prev, cur)
                outs.append(an / ln)
            for r, t in enumerate(cur_t):
                part = slice(8 * r, 8 * r + 8)
                o_ref[t, lanes] = cat([outs[0][part], outs[1][part]]).astype(o_ref.dtype)

    d1_pair(0, kp_ref, vp_ref, BLK // 16 - 1, True)
    for ap in range(1, BLK // 16):
        d1_pair(ap, k_ref, v_ref, ap - 1, False)


def _dilated_attention(proj, biases):
    s = proj.shape[0]
    cur = lambda which: pl.BlockSpec((SPAN, D_A), lambda c: (c, which))
    prev = lambda which: pl.BlockSpec((SPAN, D_A), lambda c: (jnp.maximum(c - 1, 0), which))
    return pl.pallas_call(
        _attn_body,
        out_shape=jax.ShapeDtypeStruct((s, D_A), BF16),
        grid=(s // SPAN,),
        in_specs=[cur(0), cur(1), cur(2), prev(1), prev(2)] + [_full(b) for b in biases],
        out_specs=pl.BlockSpec((SPAN, D_A), lambda c: (c, 0)),
        scratch_shapes=[pltpu.VMEM((SPAN, D_A), F32)] * 3,
        compiler_params=_params(("arbitrary",)),
        name="attn",
    )(proj, proj, proj, proj, proj, *biases)


def _s5_operators(a_re, a_im, b_re, b_im, c_re, c_im, log_dt, d_skip):
    L = S5_CHUNK
    lam = lax.complex(a_re.astype(F32), a_im.astype(F32))
    dt = jnp.exp(log_dt.astype(F32))[:, None]
    a_bar = jnp.exp(lam * dt)
    b_bar = ((a_bar - 1.0) / lam)[..., None] * lax.complex(b_re.astype(F32), b_im.astype(F32))
    c = lax.complex(c_re.astype(F32), c_im.astype(F32))
    j = jnp.arange(L + 1, dtype=F32)
    log_a = lam * dt
    apow = jnp.exp(log_a[None] * j[:, None, None])
    kt = jnp.einsum('gdp,jgp,gpc->gcjd', c, apow[:L], b_bar).real
    skip = d_skip.astype(F32).reshape(S5_G, S5_CH, 1, 1) * jnp.eye(S5_CH)[None, :, None, :]
    kt = (kt + skip * (jnp.arange(L) == 0)[None, None, :, None]).reshape(S5_G, S5_CH, L * S5_CH)
    p = jnp.einsum('sgp,gpc->gscp', apow[:L][::-1], b_bar).reshape(S5_G, L * S5_CH, S5_P)
    ca = jnp.einsum('gdp,tgp->gptd', c, apow[1:L + 1]).reshape(S5_G, S5_P, L * S5_CH)
    a_l = apow[L]

    def pair_blocks(x):
        g, r, w = x.shape
        x = x.reshape(S5_PAIRS, 2, r, w)
        z = jnp.zeros_like(x[:, 0])
        top = jnp.concatenate([x[:, 0], z], axis=-1)
        bot = jnp.concatenate([z, x[:, 1]], axis=-1)
        return jnp.concatenate([top, bot], axis=1)

    p2 = jnp.concatenate([pair_blocks(p.real), pair_blocks(p.imag)], axis=-1)
    q2 = jnp.concatenate([pair_blocks(ca.real), pair_blocks(-ca.imag)], axis=1)
    a_lr = a_l.real.reshape(1, S5_G * S5_P)
    a_li = a_l.imag.reshape(1, S5_G * S5_P)
    return kt, p2.astype(BF16), q2.astype(BF16), a_lr, a_li


def _s5_body(u_ref, kt_ref, p_ref, q_ref, ar_ref, ai_ref, o_ref, m_ref, w_ref, y_ref,
             ere, eim, xre, xim, sre, sim):
    rows = BLK
    gw = S5_CHUNK * S5_CH
    pw = 2 * gw
    per_vreg = LANES // S5_CH
    chunk_of_lane = lax.broadcasted_iota(jnp.int32, (rows, LANES), 1) // S5_CH

    def chunk_transpose(arrs):
        arrs = list(arrs)
        for s in (4, 2, 1):
            upper = (chunk_of_lane & s) != 0
            nxt = list(arrs)
            for i in range(per_vreg):
                if i & s:
                    continue
                lo_a, hi_a = arrs[i], arrs[i + s]
                nxt[i] = jnp.where(upper, pltpu.roll(hi_a, s * S5_CH, axis=1), lo_a)
                nxt[i + s] = jnp.where(upper, hi_a, pltpu.roll(lo_a, LANES - s * S5_CH, axis=1))
            arrs = nxt
        return arrs

    @pl.when(pl.program_id(0) == 0)
    def _():
        sre[...] = jnp.zeros_like(sre)
        sim[...] = jnp.zeros_like(sim)
        lane = lax.broadcasted_iota(jnp.int32, (S5_CH, gw), 1)

        def build(g, _):
            kt = kt_ref[g]
            for s in range(S5_CHUNK):
                blk = kt if s == 0 else jnp.where(lane >= s * S5_CH, pltpu.roll(kt, s * S5_CH, axis=1), 0.0)
                m_ref[g, s * S5_CH:(s + 1) * S5_CH, :] = blk.astype(BF16)
            return 0

        lax.fori_loop(0, S5_G, build, 0)

    for b in range(D_B // LANES):
        for a in range(S5_CHUNK // per_vreg):
            srcs = [u_ref[(per_vreg * a + i) * rows:(per_vreg * a + i + 1) * rows,
                          b * LANES:(b + 1) * LANES].astype(F32) for i in range(per_vreg)]
            for gi, arr in enumerate(chunk_transpose(srcs)):
                g = per_vreg * b + gi
                w_ref[:, g * gw + a * LANES:g * gw + (a + 1) * LANES] = arr.astype(BF16)

    for pr in range(S5_PAIRS):
        e = _dot(w_ref[:, pr * pw:(pr + 1) * pw], p_ref[pr])
        ere[:, pr * LANES:(pr + 1) * LANES] = e[:, :LANES]
        eim[:, pr * LANES:(pr + 1) * LANES] = e[:, LANES:]

    ar = ar_ref[...]
    ai = ai_ref[...]

    def step(n, carry):
        xr, xi = carry
        xre[pl.ds(n, 1), :] = xr
        xim[pl.ds(n, 1), :] = xi
        nr = ar * xr - ai * xi + ere[pl.ds(n, 1), :]
        ni = ar * xi + ai * xr + eim[pl.ds(n, 1), :]
        return nr, ni

    xr, xi = lax.fori_loop(0, rows, step, (sre[...], sim[...]))
    sre[...] = xr
    sim[...] = xi

    for pr in range(S5_PAIRS):
        xin = jnp.concatenate([xre[:, pr * LANES:(pr + 1) * LANES],
                               xim[:, pr * LANES:(pr + 1) * LANES]], axis=-1).astype(BF16)
        yc = _dot(xin, q_ref[pr])
        for half in range(2):
            g = 2 * pr + half
            cols = slice(g * gw, (g + 1) * gw)
            y = _dot(w_ref[:, cols], m_ref[g]) + yc[:, half * gw:(half + 1) * gw]
            y_ref[:, cols] = 0.5 * y * (1.0 + lax.erf(y * (2.0 ** -0.5)))

    for b in range(D_B // LANES):
        for a in range(S5_CHUNK // per_vreg):
            srcs = [y_ref[:, (per_vreg * b + i) * gw + a * LANES:(per_vreg * b + i) * gw + (a + 1) * LANES]
                    for i in range(per_vreg)]
            for ri, arr in enumerate(chunk_transpose(srcs)):
                r = per_vreg * a + ri
                o_ref[r * rows:(r + 1) * rows, b * LANES:(b + 1) * LANES] = arr.astype(BF16)


def _s5_core(proj, ops):
    kt, p2, q2, a_lr, a_li = ops
    s = proj.shape[0]
    gw = S5_CHUNK * S5_CH
    wide = S5_G * gw
    nstate = S5_G * S5_P
    return pl.pallas_call(
        _s5_body,
        out_shape=jax.ShapeDtypeStruct((s, D_B), BF16),
        grid=(s // SPAN,),
        in_specs=[pl.BlockSpec((SPAN, D_B), lambda i: (i, 3 * D_A // D_B)),
                  _full(kt), _full(p2), _full(q2), _full(a_lr), _full(a_li)],
        out_specs=pl.BlockSpec((SPAN, D_B), lambda i: (i, 0)),
        scratch_shapes=[pltpu.VMEM((S5_G, gw, gw), BF16), pltpu.VMEM((BLK, wide), BF16),
                        pltpu.VMEM((BLK, wide), F32)]
                       + [pltpu.VMEM((BLK, nstate), F32)] * 4 + [pltpu.VMEM((1, nstate), F32)] * 2,
        compiler_params=_params(("arbitrary",)),
        name="s5",
    )(proj, kt, p2, q2, a_lr, a_li)


def _split_bf16(x):
    hi = x.astype(BF16)
    lo = (x - hi.astype(F32)).astype(BF16)
    return hi, lo


def _mid_body(h_ref, ya_ref, yb_ref, wglu_ref, bglu_ref, ga_ref, gb_ref, wout_ref, gx_ref, wq_ref,
              k_ref, v_ref, wo_ref, gm_ref, wr_ref, br_ref, h_out, xn_out, logit_out):
    part = h_ref.shape[0] // MID_CHAINS
    slices = [slice(c * part, (c + 1) * part) for c in range(MID_CHAINS)]
    chains = [_mid_rows(h_ref[rows, :], ya_ref[rows, :], yb_ref[rows, :], wglu_ref, bglu_ref, ga_ref, gb_ref,
                        wout_ref, gx_ref, wq_ref, k_ref, v_ref, wo_ref, gm_ref, wr_ref, br_ref)
              for rows in slices]
    for outs in zip(*chains):
        pass
    for rows, out in zip(slices, outs):
        for ref, val in zip((h_out, xn_out, logit_out), out):
            ref[rows, :] = val


def _mid_rows(h, ya, yb, wglu_ref, bglu_ref, ga_ref, gb_ref, wout_ref, gx_ref, wq_ref,
              k_ref, v_ref, wo_ref, gm_ref, wr_ref, br_ref):
    gate = jax.nn.sigmoid(_dot(yb, wglu_ref[...]) + bglu_ref[...])
    yb2 = yb.astype(F32) * gate
    na = _rms(ya.astype(F32), ga_ref[...]).astype(BF16)
    nb = _rms(yb2, gb_ref[...]).astype(BF16)
    h1 = h + _dot(na, wout_ref[0:D_A, :]) + _dot(nb, wout_ref[D_A:D_MODEL, :])
    yield None
    q = _dot(_rms(h1, gx_ref[...]).astype(BF16), wq_ref[...]).astype(BF16)
    yield None
    heads = []
    for hd in range(H_X):
        lanes = slice(hd * HD_X, (hd + 1) * HD_X)
        s = _dot_nt(q[:, lanes], k_ref[:, lanes])
        e = jnp.exp(s - jnp.max(s, axis=-1, keepdims=True))
        heads.append(_dot(e.astype(BF16), v_ref[:, lanes]) / jnp.sum(e, axis=-1, keepdims=True))
    o = jnp.concatenate(heads, axis=-1).astype(BF16)
    yield None
    h2 = h1 + _dot(o, wo_ref[...])
    yield None
    xn = _rms(h2, gm_ref[...])
    x_hi, x_lo = _split_bf16(xn)
    part = _dot(x_hi, wr_ref[...]) + _dot(x_lo, wr_ref[...])
    logits = part + pltpu.roll(part, LANES - N_EXPERTS, axis=1) + br_ref[...]
    yield h2, _pack_rows(xn), logits


def _mid(h, ya, yb, wglu, bglu, ga, gb, wout, gx, wq, kmem, vmem, wo, gm, wr2, br):
    s = h.shape[0]
    tm = min(MID_CHAINS * ROW_TILE // 2, s)
    row = lambda w: pl.BlockSpec((tm, w), lambda i: (i, 0))
    consts = [wglu, bglu, ga, gb, wout, gx, wq, kmem, vmem, wo, gm, wr2, br]
    return pl.pallas_call(
        _mid_body,
        out_shape=[jax.ShapeDtypeStruct((s, D_MODEL), F32), jax.ShapeDtypeStruct((s, D_MODEL // 2), jnp.int32),
                   jax.ShapeDtypeStruct((s, LANES), F32)],
        grid=(s // tm,),
        in_specs=[row(D_MODEL), row(D_A), row(D_B)] + [_full(a) for a in consts],
        out_specs=[row(D_MODEL), row(D_MODEL // 2), row(LANES)],
        compiler_params=_params(("parallel",)),
        name="mid",
    )(h, ya, yb, *consts)


def _memkv_body(mem_ref, g_ref, w_ref, o_ref):
    o_ref[...] = _dot(_rms(mem_ref[...], g_ref[...]).astype(BF16), w_ref[...]).astype(BF16)


def _memkv(mem, g, w):
    n, d = mem.shape
    return pl.pallas_call(
        _memkv_body,
        out_shape=jax.ShapeDtypeStruct((n, w.shape[1]), BF16),
        compiler_params=pltpu.CompilerParams(vmem_limit_bytes=VMEM_LIMIT),
        name="memkv",
    )(mem, g.reshape(1, d), w)


def _rank_body(logit_ref, tri_ref, idx_ref, gate_ref, rank_ref, cnt_ref, carry):
    @pl.when(pl.program_id(0) == 0)
    def _():
        carry[...] = jnp.zeros_like(carry)

    tm = logit_ref.shape[0]
    logits = jnp.transpose(logit_ref[...])[:N_EXPERTS, :]
    expert = lax.broadcasted_iota(jnp.int32, (N_EXPERTS, tm), 0)
    vals, idxs = [], []
    for _ in range(TOP_K):
        mx = jnp.max(logits, axis=0, keepdims=True)
        ix = jnp.min(jnp.where(logits == mx, expert, N_EXPERTS), axis=0, keepdims=True)
        vals.append(mx)
        idxs.append(ix)
        logits = jnp.where(expert == ix, -jnp.inf, logits)
    es = [jnp.exp(v - vals[0]) for v in vals]
    den = es[0] + es[1] + es[2] + es[3]
    hits = [expert == ix for ix in idxs]
    onehot = jnp.zeros((N_EXPERTS, tm), F32)
    for hit in hits:
        onehot = onehot + jnp.where(hit, 1.0, 0.0)
    nb = tm // LANES
    blocks = jnp.concatenate([onehot[:, b * LANES:(b + 1) * LANES] for b in range(nb)], axis=0)
    inc = _dot(blocks.astype(BF16), tri_ref[...])
    run = carry[...][:, 0:1]
    before = []
    for b in range(nb):
        inc_b = inc[b * N_EXPERTS:(b + 1) * N_EXPERTS, :]
        before.append(run + inc_b - onehot[:, b * LANES:(b + 1) * LANES])
        run = run + inc_b[:, LANES - 1:LANES]
    before = jnp.concatenate(before, axis=1)
    choice = lax.broadcasted_iota(jnp.int32, (SUBLANES, tm), 0)
    idx_t = jnp.full((SUBLANES, tm), N_EXPERTS, jnp.int32)
    gate_t = jnp.zeros((SUBLANES, tm), F32)
    rank_t = jnp.zeros((SUBLANES, tm), jnp.int32)
    for k in range(TOP_K):
        rk = jnp.sum(jnp.where(hits[k], before, 0.0), axis=0, keepdims=True)
        idx_t = jnp.where(choice == k, idxs[k], idx_t)
        gate_t = jnp.where(choice == k, es[k] / den, gate_t)
        rank_t = jnp.where(choice == k, rk.astype(jnp.int32), rank_t)
    idx_ref[...] = idx_t
    gate_ref[...] = gate_t
    rank_ref[...] = rank_t
    total = jnp.broadcast_to(run, carry.shape)
    carry[...] = total
    cnt_ref[...] = total.astype(jnp.int32)


def _rank(logits):
    t = logits.shape[0]
    tm = min(RANK_TILE, t)
    cols = pl.BlockSpec((SUBLANES, tm), lambda i: (0, i))
    tri = jnp.asarray(np.triu(np.ones((LANES, LANES), np.float32)), BF16)
    return pl.pallas_call(
        _rank_body,
        out_shape=[jax.ShapeDtypeStruct((SUBLANES, t), jnp.int32), jax.ShapeDtypeStruct((SUBLANES, t), F32),
                   jax.ShapeDtypeStruct((SUBLANES, t), jnp.int32),
                   jax.ShapeDtypeStruct((N_EXPERTS, LANES), jnp.int32)],
        grid=(t // tm,),
        in_specs=[pl.BlockSpec((tm, LANES), lambda i: (i, 0)), _full(tri)],
        out_specs=[cols, cols, cols, pl.BlockSpec((N_EXPERTS, LANES), lambda i: (0, 0))],
        scratch_shapes=[pltpu.VMEM((N_EXPERTS, LANES), F32)],
        compiler_params=_params(("arbitrary",)),
        name="rank",
    )(logits, tri)


def _route(logits, tm):
    t = logits.shape[0]
    tk = t * TOP_K
    idx, gates, rank, cnt = _rank(logits)
    counts = cnt[:, 0]
    padded = (counts + tm - 1) // tm * tm
    pend = jnp.cumsum(padded)
    pstart = pend - padded
    n_rows = tk + N_EXPERTS * tm
    n_tiles = n_rows // tm
    experts = jnp.arange(N_EXPERTS, dtype=jnp.int32)
    tile_first = jnp.arange(n_tiles, dtype=jnp.int32) * tm
    last_used = jnp.max(jnp.where(padded > 0, experts, 0))
    tile_e = jnp.minimum(jnp.sum(tile_first[:, None] >= pend[None, :], axis=1), last_used).astype(jnp.int32)
    tile_rows = jnp.clip(jnp.sum(jnp.where(tile_e[:, None] == experts[None, :],
                                           (pstart + counts)[None, :], 0), axis=1) - tile_first, 0, tm)
    tile_rows = jnp.where(tile_first < pend[-1], tile_rows, 0).astype(jnp.int32)
    group = jnp.cumsum(jnp.concatenate([jnp.zeros((1,), jnp.int32),
                                        (tile_e[1:] != tile_e[:-1]).astype(jnp.int32)]))
    tile_slot = (group % 2).astype(jnp.int32)
    later = (experts[None, :] > experts[:, None]) & (padded > 0)[None, :]
    next_e = jnp.min(jnp.where(later, experts[None, :], N_EXPERTS), axis=1)
    next_e = jnp.where(next_e < N_EXPERTS, next_e, -1).astype(jnp.int32)
    tile_next = jnp.sum(jnp.where(tile_e[:, None] == experts[None, :], next_e[None, :], 0), axis=1).astype(jnp.int32)
    base = jnp.sum(jnp.where(idx[:TOP_K, :, None] == experts, pstart, 0), axis=-1)
    dest = rank[:TOP_K] + base
    return gates[:TOP_K].T, dest, n_rows, tile_e, tile_rows, tile_slot, tile_next


def _sc_mesh():
    return plsc.VectorSubcoreMesh(core_axis_name="c", subcore_axis_name="s",
                                  num_cores=SC_CORES, num_subcores=SC_SUBCORES)


def _sc_worker():
    return lax.axis_index("s") * SC_CORES + lax.axis_index("c")


def _scatter_rows(x, dest, n_rows):
    t, d = x.shape
    per_worker = t // (SC_CORES * SC_SUBCORES)
    chunks = per_worker // SC_ROWS

    assert chunks % 2 == 0

    @functools.partial(
        pl.kernel, mesh=_sc_mesh(),
        out_type=jax.ShapeDtypeStruct((n_rows, d), x.dtype),
        scratch_types=[pltpu.VMEM((SC_ROWS, d), x.dtype)] * 2 + [pltpu.VMEM((SC_ROWS,), jnp.int32)] * (2 * TOP_K)
                      + [pltpu.SemaphoreType.DMA((2,)), pltpu.SemaphoreType.DMA((2, TOP_K))],
    )
    def scatter(x_hbm, *rest):
        dest_hbm, out_hbm = rest[:TOP_K], rest[TOP_K]
        rows_v = rest[TOP_K + 1:TOP_K + 3]
        idx_v = (rest[TOP_K + 3:2 * TOP_K + 3], rest[2 * TOP_K + 3:3 * TOP_K + 3])
        lsem, ssem = rest[3 * TOP_K + 3], rest[3 * TOP_K + 4]
        base = _sc_worker() * per_worker

        def rows_at(c):
            return pl.ds(pl.multiple_of(base + c * SC_ROWS, SC_ROWS), SC_ROWS)

        def loaded(c, s):
            return pltpu.make_async_copy(x_hbm.at[rows_at(c)], rows_v[s], lsem.at[s])

        def load(c, s):
            loaded(c, s).start()
            for k in range(TOP_K):
                pltpu.sync_copy(dest_hbm[k].at[rows_at(c)], idx_v[s][k])

        def scattered(s):
            return [pltpu.make_async_copy(rows_v[s], out_hbm.at[idx_v[s][k]], ssem.at[s, k]) for k in range(TOP_K)]

        load(0, 0)

        @pl.loop(0, chunks, step=2)
        def _(c0):
            for s in range(2):
                c = c0 + s
                loaded(c, s).wait()
                for cp in scattered(s):
                    cp.start()

                @pl.when(c >= 1)
                def _():
                    for cp in scattered(1 - s):
                        cp.wait()

                @pl.when(c + 1 < chunks)
                def _():
                    load(c + 1, 1 - s)

        for cp in scattered(1):
            cp.wait()

    return scatter(x, *[dest[k] for k in range(TOP_K)])


def _gather_rows(table, idx):
    n, d = table.shape
    b = idx.shape[0]
    per_worker = b // (SC_CORES * SC_SUBCORES)
    chunks = per_worker // SC_ROWS
    assert chunks % 2 == 0

    @functools.partial(
        pl.kernel, mesh=_sc_mesh(),
        out_type=jax.ShapeDtypeStruct((b, d), table.dtype),
        scratch_types=[pltpu.VMEM((chunks, SC_ROWS), jnp.int32)] + [pltpu.VMEM((SC_ROWS, d), table.dtype)] * 2
                      + [pltpu.SemaphoreType.DMA((2,)), pltpu.SemaphoreType.DMA((2,))],
    )
    def gather(table_hbm, idx_hbm, out_hbm, idx_v, rows0, rows1, gsem, wsem):
        rows_v = (rows0, rows1)
        worker = _sc_worker()
        base = worker * per_worker
        pltpu.sync_copy(idx_hbm.at[pl.ds(pl.multiple_of(worker * chunks, chunks), chunks)], idx_v)

        def rows_at(c):
            return pl.ds(pl.multiple_of(base + c * SC_ROWS, SC_ROWS), SC_ROWS)

        def fetched(c, s):
            return pltpu.make_async_copy(table_hbm.at[idx_v.at[c]], rows_v[s], gsem.at[s])

        def written(c, s):
            return pltpu.make_async_copy(rows_v[s], out_hbm.at[rows_at(c)], wsem.at[s])

        fetched(0, 0).start()

        @pl.loop(0, chunks, step=2)
        def _(c0):
            for s in range(2):
                c = c0 + s
                fetched(c, s).wait()
                written(c, s).start()

                @pl.when(c >= 1)
                def _():
                    written(c - 1, 1 - s).wait()

                @pl.when(c + 1 < chunks)
                def _():
                    fetched(c + 1, 1 - s).start()

        written(chunks - 1, 1).wait()

    return gather(table, idx.reshape(b // SC_ROWS, SC_ROWS))


def _moe_body(te_ref, tv_ref, sl_ref, nx_ref, x_ref, w1_hbm, b1_ref, w2_hbm, b2_ref, o_ref,
              w1f, w2f, w1b, w2b, sem, *, layer):
    i = pl.program_id(0)
    e = te_ref[i]
    slot = sl_ref[i]
    new_expert = (i == 0) | (e != te_ref[jnp.maximum(i - 1, 0)])

    def weight_copies(expert, s):
        rows1 = w1f.shape[1] // MOE_W1_PARTS
        rows2 = w2f.shape[1] // MOE_W2_PARTS
        c1 = [pltpu.make_async_copy(w1_hbm.at[layer, expert, pl.ds(q * rows1, rows1)],
                                    w1f.at[s, pl.ds(q * rows1, rows1)], sem.at[s, q])
              for q in range(MOE_W1_PARTS)]
        c2 = [pltpu.make_async_copy(w2_hbm.at[layer, expert, pl.ds(q * rows2, rows2)],
                                    w2f.at[s, pl.ds(q * rows2, rows2)], sem.at[s, MOE_W1_PARTS + q])
              for q in range(MOE_W2_PARTS)]
        return c1 + c2

    @pl.when(i == 0)
    def _():
        for c in weight_copies(e, slot):
            c.start()

    @pl.when(new_expert)
    def _():
        for c in weight_copies(e, slot):
            c.wait()
        nxt = nx_ref[i]

        @pl.when(nxt >= 0)
        def _():
            for c in weight_copies(nxt, 1 - slot):
                c.start()

        w1b[...] = w1f[slot].astype(BF16)
        w2b[...] = w2f[slot].astype(BF16)

    def expert(rows):
        row = lax.broadcasted_iota(jnp.int32, (rows, x_ref.shape[1]), 0)
        lo, hi = _unpack_rows(jnp.where(row < tv_ref[i], x_ref[0:rows, :], 0))
        x = jnp.concatenate([lo, hi], axis=-1).astype(BF16)
        hb = _dot(x, w1b[...]) + b1_ref[0]
        x_glu = jnp.minimum(hb[:, :D_FF], SWIGLU_LIMIT)
        x_lin = jnp.clip(hb[:, D_FF:], -SWIGLU_LIMIT, SWIGLU_LIMIT)
        act = x_glu * jax.nn.sigmoid(SWIGLU_ALPHA * x_glu) * (x_lin + 1.0)
        o_ref[0:rows, :] = _pack_rows(_dot(act.astype(BF16), w2b[...]) + b2_ref[0])

    step = x_ref.shape[0] // MOE_ROW_PATHS
    for path in range(1, MOE_ROW_PATHS + 1):
        rows = path * step

        @pl.when((tv_ref[i] > rows - step) & (tv_ref[i] <= rows))
        def _(rows=rows):
            expert(rows)
            if rows < x_ref.shape[0]:
                o_ref[rows:, :] = jnp.zeros((x_ref.shape[0] - rows, o_ref.shape[1]), o_ref.dtype)

    @pl.when(tv_ref[i] == 0)
    def _():
        o_ref[...] = jnp.zeros_like(o_ref)


def _moe_experts(xs, tile_e, tile_rows, tile_slot, tile_next, w1, b1, w2, b2, layer):
    n_rows = xs.shape[0]
    tm = MOE_TILE
    nl, ne, d, ff2 = w1.shape
    bias_map = lambda i, te, tv, sl, nx: (layer, te[i], 0, 0)
    grid_spec = pltpu.PrefetchScalarGridSpec(
        num_scalar_prefetch=4,
        grid=(n_rows // tm,),
        in_specs=[pl.BlockSpec((tm, d // 2), lambda i, *_: (i, 0)),
                  pl.BlockSpec(memory_space=pl.ANY),
                  pl.BlockSpec((None, 1, 1, ff2), bias_map),
                  pl.BlockSpec(memory_space=pl.ANY),
                  pl.BlockSpec((None, 1, 1, d), bias_map)],
        out_specs=pl.BlockSpec((tm, d // 2), lambda i, *_: (i, 0)),
        scratch_shapes=[pltpu.VMEM((2, d, ff2), F32), pltpu.VMEM((2, ff2 // 2, d), F32),
                        pltpu.VMEM((d, ff2), BF16), pltpu.VMEM((ff2 // 2, d), BF16),
                        pltpu.SemaphoreType.DMA((2, MOE_W1_PARTS + MOE_W2_PARTS))],
    )
    return pl.pallas_call(
        functools.partial(_moe_body, layer=layer),
        out_shape=jax.ShapeDtypeStruct((n_rows, d // 2), jnp.int32),
        grid_spec=grid_spec,
        compiler_params=_params(("arbitrary",)),
        name="moe",
    )(tile_e, tile_rows, tile_slot, tile_next, xs, w1, b1.reshape(nl, ne, 1, ff2), w2, b2.reshape(nl, ne, 1, d))


def _combined(h_ref, y_refs, gate_ref):
    gates = gate_ref[...]
    lo = jnp.zeros(y_refs[0].shape, F32)
    hi = jnp.zeros(y_refs[0].shape, F32)
    for k, y_ref in enumerate(y_refs):
        yl, yh = _unpack_rows(y_ref[...])
        lo = lo + yl * gates[:, k:k + 1]
        hi = hi + yh * gates[:, k:k + 1]
    return h_ref[...] + jnp.concatenate([lo, hi], axis=-1)


def _combine_final_body(h_ref, y0_ref, y1_ref, y2_ref, y3_ref, gate_ref, g_ref, out_hbm, buf, sem):
    i = pl.program_id(0)
    n = pl.num_programs(0)
    slot = i % 2
    per_step = h_ref.shape[0] // BLK
    steps_per_span = NRES // per_step

    def writes(step, s):
        span = step // steps_per_span
        r0 = (step % steps_per_span) * per_step
        return [pltpu.make_async_copy(buf.at[s, pl.ds(rr * BLK, BLK), :], out_hbm.at[span, :, r0 + rr, :],
                                      sem.at[s, rr]) for rr in range(per_step)]

    @pl.when(i >= 2)
    def _():
        for cp in writes(i - 2, slot):
            cp.wait()

    buf[slot] = _rms(_combined(h_ref, (y0_ref, y1_ref, y2_ref, y3_ref), gate_ref), g_ref[...])
    for cp in writes(i, slot):
        cp.start()

    @pl.when(i == n - 1)
    def _():
        for cp in writes(i, slot):
            cp.wait()

        @pl.when(i >= 1)
        def _():
            for cp in writes(i - 1, 1 - slot):
                cp.wait()


def _combine_final(h, y, gates, g):
    s, d = h.shape
    tm = min(ROW_TILE, s)
    nt = s // tm
    in_specs = ([pl.BlockSpec((tm, d), lambda i: (i, 0))]
                + [pl.BlockSpec((tm, d // 2), lambda i, k=k: (k * nt + i, 0)) for k in range(TOP_K)]
                + [pl.BlockSpec((tm, TOP_K), lambda i: (i, 0))])
    out = pl.pallas_call(
        _combine_final_body,
        out_shape=jax.ShapeDtypeStruct((s // SPAN, BLK, NRES, d), F32),
        grid=(nt,),
        in_specs=in_specs + [pl.BlockSpec((1, d), lambda i: (0, 0))],
        out_specs=pl.BlockSpec(memory_space=pl.ANY),
        scratch_shapes=[pltpu.VMEM((2, tm, d), F32), pltpu.SemaphoreType.DMA((2, tm // BLK))],
        compiler_params=_params(("arbitrary",)),
        name="combine_final",
    )(h, y, y, y, y, gates, g.reshape(1, d))
    return out.reshape(s, d)


def _layer(state, mem, biases, p, l, g_final):
    row = lambda a: a.reshape(1, -1).astype(F32)
    q_scale = jnp.concatenate([jnp.full((D_A,), HD_A ** -0.5 * LOG2E, F32), jnp.ones((D_IN - D_A,), F32)])
    w_in = (p["w_in"][l] * q_scale).astype(BF16)
    if l == 0:
        h, proj = state, _inproj(state, p["norm_mix"][l], w_in)
    else:
        h, proj = _combine_inproj(*state, p["norm_mix"][l], w_in)
    ya = _dilated_attention(proj, biases)
    ops = _s5_operators(p["s5_a_re"][l], p["s5_a_im"][l], p["s5_b_re"][l], p["s5_b_im"][l],
                        p["s5_c_re"][l], p["s5_c_im"][l], p["s5_log_dt"][l], p["s5_d"][l])
    yb = _s5_core(proj, ops)
    kv = _memkv(mem, p["norm_mem"][l], p["w_xkv"][l].astype(BF16))
    wr = p["w_router"][l].astype(F32)
    wr_hi = wr.astype(BF16)
    wr2 = jnp.pad(jnp.concatenate([wr_hi, (wr - wr_hi.astype(F32)).astype(BF16)], axis=1),
                  ((0, 0), (0, LANES - 2 * N_EXPERTS)))
    br = jnp.pad(p["b_router"][l].astype(F32), (0, LANES - N_EXPERTS)).reshape(1, LANES)
    h2, xn, logits = _mid(
        h, ya, yb, p["w_glu"][l].astype(BF16), row(p["b_glu"][l]), row(p["g_out_attn"][l]),
        row(p["g_out_ssm"][l]), p["w_out"][l].astype(BF16), row(p["norm_xattn"][l]),
        (p["w_xq"][l] * (HD_X ** -0.5)).astype(BF16), kv[:, :D_X], kv[:, D_X:],
        p["w_xo"][l].astype(BF16), row(p["norm_moe"][l]), wr2, br)
    gates, dest, n_rows, tile_e, tile_rows, tile_slot, tile_next = _route(logits, MOE_TILE)
    out = _moe_experts(_scatter_rows(xn, dest, n_rows), tile_e, tile_rows, tile_slot, tile_next,
                       p["w1"], p["b1"], p["w2"], p["b2"], l)
    y = _gather_rows(out, dest.reshape(-1))
    return _combine_final(h2, y, gates, g_final) if l == DEPTH - 1 else (h2, y, gates)


def kernel(x, mem, rel_bias, norm_mix, w_in, s5_a_re, s5_a_im, s5_b_re, s5_b_im, s5_c_re, s5_c_im, s5_log_dt, s5_d, w_glu, b_glu, g_out_attn, g_out_ssm, w_out, norm_xattn, norm_mem, w_xq, w_xkv, w_xo, norm_moe, w_router, b_router, w1, b1, w2, b2, norm_final):
    p = dict(norm_mix=norm_mix, w_in=w_in, s5_a_re=s5_a_re, s5_a_im=s5_a_im, s5_b_re=s5_b_re,
             s5_b_im=s5_b_im, s5_c_re=s5_c_re, s5_c_im=s5_c_im, s5_log_dt=s5_log_dt, s5_d=s5_d,
             w_glu=w_glu, b_glu=b_glu, g_out_attn=g_out_attn, g_out_ssm=g_out_ssm, w_out=w_out,
             norm_xattn=norm_xattn, norm_mem=norm_mem, w_xq=w_xq, w_xkv=w_xkv, w_xo=w_xo,
             norm_moe=norm_moe, w_router=w_router, b_router=b_router, w1=w1, b1=b1, w2=w2, b2=b2)
    biases = [_attn_bias(rel_bias, window, dil, perm)
              for (window, dil), perm in zip(WIN_DIL, (_PERM_D1, _PERM_D4, _PERM_D16))]
    outs = []
    for b in range(x.shape[0]):
        h = _to_span_layout(x[b])
        for l in range(DEPTH):
            h = _layer(h, mem[b], biases, p, l, norm_final)
        outs.append(h)
    return jnp.stack(outs)
```

```python
import functools
import math

import jax
import jax.numpy as jnp
import numpy as np
from jax import lax
from jax.experimental import pallas as pl
from jax.experimental.pallas import tpu as pltpu
from jax.experimental.pallas import tpu_sc as plsc

F32 = jnp.float32
BF16 = jnp.bfloat16

D_MODEL = 1024
DEPTH = 2
EPS = 1e-5
NEG_INF = -1e30
LOG2E = math.log2(math.e)
H_A = 8
HD_A = 64
D_A = H_A * HD_A
WIN_DIL = ((128, 1), (512, 4), (2048, 16))
BLK = 128
D_B = D_MODEL - D_A
S5_CH = 16
S5_G = D_B // S5_CH
S5_P = 64
D_IN = 3 * D_A + D_B
NUM_BUCKETS = 32
REL_MAX_DIST = 2048
H_X = 4
HD_X = 128
D_X = H_X * HD_X
N_EXPERTS = 32
TOP_K = 4
D_FF = D_MODEL
SWIGLU_ALPHA = 1.702
SWIGLU_LIMIT = 7.0

LANES = 128
SUBLANES = 8
NRES = WIN_DIL[-1][1]
SPAN = NRES * BLK
S5_CHUNK = NRES
S5_PAIRS = S5_G // 2
VMEM_LIMIT = 56 * 1024 * 1024

SC_CORES = 2
SC_SUBCORES = 16
SC_ROWS = 64

ROW_TILE = 512
ROW_CHAINS = 2
MID_CHAINS = 4
RANK_TILE = 1024
FINAL_PARTS = 2
MOE_TILE = 512
MOE_ROW_PATHS = 4
MOE_W1_PARTS = 4
MOE_W2_PARTS = 2


def _params(sem):
    return pltpu.CompilerParams(dimension_semantics=sem, vmem_limit_bytes=VMEM_LIMIT)


def _rms(x, g):
    return x * lax.rsqrt(jnp.mean(x * x, axis=-1, keepdims=True) + EPS) * g


def _dot(a, b):
    return jnp.dot(a, b, preferred_element_type=F32)


def _dot_nt(a, b):
    return lax.dot_general(a, b, (((1,), (1,)), ((), ())), preferred_element_type=F32)


def _full(a):
    return pl.BlockSpec(a.shape, lambda *_: (0,) * a.ndim)


def _pack_rows(x):
    c = x.shape[1] // 2
    lo = lax.bitcast_convert_type(x[:, :c].astype(BF16).astype(F32), jnp.uint32)
    hi = lax.bitcast_convert_type(x[:, c:].astype(BF16).astype(F32), jnp.uint32)
    return lax.bitcast_convert_type(lax.shift_right_logical(lo, jnp.uint32(16)) | hi, jnp.int32)


def _unpack_rows(p):
    u = lax.bitcast_convert_type(p, jnp.uint32)
    lo = lax.bitcast_convert_type(lax.shift_left(u, jnp.uint32(16)), F32)
    hi = lax.bitcast_convert_type(u & jnp.uint32(0xFFFF0000), F32)
    return lo, hi


def _to_span_layout(x):
    s = x.shape[0]
    return x.reshape(s // SPAN, BLK, NRES, -1).transpose(0, 2, 1, 3).reshape(s, -1)


def _from_span_layout(x):
    s = x.shape[0]
    return x.reshape(s // SPAN, NRES, BLK, -1).transpose(0, 2, 1, 3).reshape(s, -1)


def _row_chains(n_rows):
    part = n_rows // ROW_CHAINS
    return [slice(c * part, (c + 1) * part) for c in range(ROW_CHAINS)]


def _inproj_body(h_ref, g_ref, w_ref, o_ref):
    for rows in _row_chains(h_ref.shape[0]):
        xn = _rms(h_ref[rows, :], g_ref[...]).astype(BF16)
        o_ref[rows, :] = _dot(xn, w_ref[...]).astype(BF16)


def _inproj(h, g, w):
    s, d = h.shape
    n = w.shape[1]
    tm = min(ROW_TILE, s)
    return pl.pallas_call(
        _inproj_body,
        out_shape=jax.ShapeDtypeStruct((s, n), BF16),
        grid=(s // tm,),
        in_specs=[pl.BlockSpec((tm, d), lambda i: (i, 0)),
                  pl.BlockSpec((1, d), lambda i: (0, 0)),
                  pl.BlockSpec((d, n), lambda i: (0, 0))],
        out_specs=pl.BlockSpec((tm, n), lambda i: (i, 0)),
        compiler_params=_params(("parallel",)),
        name="inproj",
    )(h, g.reshape(1, d), w)


def _combine_inproj_body(h_ref, y0_ref, y1_ref, y2_ref, y3_ref, gate_ref, g_ref, w_ref, h_out, o_ref):
    chains = _row_chains(h_ref.shape[0])
    hs = [_combined(h_ref.at[rows, :], [y.at[rows, :] for y in (y0_ref, y1_ref, y2_ref, y3_ref)],
                    gate_ref.at[rows, :]) for rows in chains]
    for rows, h in zip(chains, hs):
        h_out[rows, :] = h
    for rows, h in zip(chains, hs):
        o_ref[rows, :] = _dot(_rms(h, g_ref[...]).astype(BF16), w_ref[...]).astype(BF16)


def _combine_inproj(h, y, gates, g, w):
    s, d = h.shape
    n = w.shape[1]
    tm = min(ROW_TILE, s)
    nt = s // tm
    return pl.pallas_call(
        _combine_inproj_body,
        out_shape=[jax.ShapeDtypeStruct((s, d), F32), jax.ShapeDtypeStruct((s, n), BF16)],
        grid=(nt,),
        in_specs=[pl.BlockSpec((tm, d), lambda i: (i, 0))]
                 + [pl.BlockSpec((tm, d // 2), lambda i, k=k: (k * nt + i, 0)) for k in range(TOP_K)]
                 + [pl.BlockSpec((tm, TOP_K), lambda i: (i, 0)), pl.BlockSpec((1, d), lambda i: (0, 0)),
                    pl.BlockSpec((d, n), lambda i: (0, 0))],
        out_specs=[pl.BlockSpec((tm, d), lambda i: (i, 0)), pl.BlockSpec((tm, n), lambda i: (i, 0))],
        compiler_params=_params(("parallel",)),
        name="combine_inproj",
    )(h, y, y, y, y, gates, g.reshape(1, d), w)


def _t5_bucket(n):
    max_exact = NUM_BUCKETS // 2
    nf = jnp.maximum(n, 1).astype(F32)
    large = max_exact + (jnp.log(nf / max_exact) / math.log(REL_MAX_DIST / max_exact)
                         * (NUM_BUCKETS - max_exact)).astype(jnp.int32)
    large = jnp.minimum(large, NUM_BUCKETS - 1)
    return jnp.where(n < max_exact, n, large)


def _attn_bias(rel_bias, window, dil, perm):
    steps = window // dil
    perm = jnp.asarray(perm, jnp.int32)
    qi = perm[:, None]
    ki = jnp.concatenate([perm, BLK + perm])[None, :]
    dist = BLK + qi - ki
    in_win = (dist >= 0) & (dist <= steps)
    bucket = _t5_bucket(jnp.clip(dist, 0, steps) * dil)
    onehot = (bucket[:, :, None] == jnp.arange(NUM_BUCKETS, dtype=jnp.int32)).astype(F32)
    bias = jnp.einsum('qkb,bh->hqk', onehot, rel_bias.astype(F32), precision=lax.Precision.HIGHEST)
    bias = jnp.where(in_win[None], bias * LOG2E, NEG_INF)
    return bias.reshape(H_A // 2, 2 * BLK, 2 * BLK)


_PERM_D1 = [NRES * jl + r for r in range(NRES) for jl in range(BLK // NRES)]
_PERM_D4 = [4 * jl + i for i in range(4) for jl in range(BLK // 4)]
_PERM_D16 = list(range(BLK))


def _attn_body(q_ref, k_ref, v_ref, kp_ref, vp_ref, b1_ref, b4_ref, b16_ref, o_ref, acc, mst, lst):
    has_prev = pl.program_id(0) > 0
    lane = lax.broadcasted_iota(jnp.int32, (1, LANES), 1)
    lo = lane < HD_A
    mlo = lo.astype(BF16)
    mhi = (~lo).astype(BF16)
    col = lax.broadcasted_iota(jnp.int32, (2 * BLK, 2 * BLK), 1)
    ones = jnp.ones((2 * BLK, LANES), BF16)

    def tile(q2, kk, vv, bias, mask_prev):
        qs = jnp.concatenate([q2 * mlo, q2 * mhi], axis=0)
        s = _dot_nt(qs, kk) + bias
        if mask_prev:
            s = jnp.where(jnp.logical_or(has_prev, col >= BLK), s, NEG_INF)
        m = jnp.max(s, axis=-1, keepdims=True)
        e = jnp.exp2((s - m).astype(BF16))
        oa = _dot(e, jnp.concatenate([vv, ones], axis=1))
        o = oa[:, :LANES]
        l = oa[:, LANES:]
        return (jnp.where(lo, m[:BLK], m[BLK:]), jnp.where(lo, l[:BLK], l[BLK:]),
                jnp.where(lo, o[:BLK], o[BLK:]))

    def merge(prev, cur):
        mp, lp, ap = prev
        mc, lc, ac = cur
        mn = jnp.maximum(mp, mc)
        a = jnp.exp2(mp - mn)
        b = jnp.exp2(mc - mn)
        return mn, a * lp + b * lc, a * ap + b * ac

    def cat(xs):
        return jnp.concatenate(xs, axis=0)


    for r in range(NRES):
        rows = pl.ds(r * BLK, BLK)
        for hp in range(H_A // 2):
            lanes = slice(hp * LANES, (hp + 1) * LANES)
            kk = cat([kp_ref[rows, lanes], k_ref[rows, lanes]])
            vv = cat([vp_ref[rows, lanes], v_ref[rows, lanes]])
            m2, l2, o2 = tile(q_ref[rows, lanes], kk, vv, b16_ref[hp], True)
            mst[rows, lanes] = m2
            lst[rows, lanes] = l2
            acc[rows, lanes] = o2

    for r4 in range(4):
        for b in range(4):
            def chunk_rows(bb):
                return [pl.ds(4 * BLK * i + BLK * r4 + 32 * bb, 32) for i in range(4)]
            rows = chunk_rows(b)
            prows = chunk_rows(3 if b == 0 else b - 1)
            kprev, vprev = (kp_ref, vp_ref) if b == 0 else (k_ref, v_ref)
            for hp in range(H_A // 2):
                lanes = slice(hp * LANES, (hp + 1) * LANES)
                q2 = cat([q_ref[rr, lanes] for rr in rows])
                kk = cat([kprev[rr, lanes] for rr in prows] + [k_ref[rr, lanes] for rr in rows])
                vv = cat([vprev[rr, lanes] for rr in prows] + [v_ref[rr, lanes] for rr in rows])
                cur = tile(q2, kk, vv, b4_ref[hp], b == 0)
                prev = (cat([mst[rr, lanes] for rr in rows]), cat([lst[rr, lanes] for rr in rows]),
                        cat([acc[rr, lanes] for rr in rows]))
                mn, ln, an = merge(prev, cur)
                for i, rr in enumerate(rows):
                    part = slice(32 * i, 32 * (i + 1))
                    mst[rr, lanes] = mn[part]
                    lst[rr, lanes] = ln[part]
                    acc[rr, lanes] = an[part]

    def d1_pair(ap, kprev, vprev, prev_ap, mask_prev):
        def tiles(a_):
            return [pl.ds(BLK * r + 16 * a_, 16) for r in range(NRES)]
        cur_t = tiles(ap)
        prev_t = tiles(prev_ap)

        def halves(ref, ts, lanes):
            xs = [ref[t, lanes].astype(F32) for t in ts]
            return cat([x[:8] for x in xs]).astype(BF16), cat([x[8:] for x in xs]).astype(BF16)

        for hp in range(H_A // 2):
            lanes = slice(hp * LANES, (hp + 1) * LANES)
            q_e, q_o = halves(q_ref, cur_t, lanes)
            k_e, k_o = halves(k_ref, cur_t, lanes)
            v_e, v_o = halves(v_ref, cur_t, lanes)
            _, k_p = halves(kprev, prev_t, lanes)
            _, v_p = halves(vprev, prev_t, lanes)
            cur_e = tile(q_e, cat([k_p, k_e]), cat([v_p, v_e]), b1_ref[hp], mask_prev)
            cur_o = tile(q_o, cat([k_e, k_o]), cat([v_e, v_o]), b1_ref[hp], False)
            ms = [mst[t, lanes] for t in cur_t]
            ls = [lst[t, lanes] for t in cur_t]
            ac = [acc[t, lanes] for t in cur_t]
            outs = []
            for half, cur in ((0, cur_e), (1, cur_o)):
                part = slice(8 * half, 8 * half + 8)
                prev = (cat([x[part] for x in ms]), cat([x[part] for x in ls]), cat([x[part] for x in ac]))
                _, ln, an = merge(prev, cur)
                outs.append(an / ln)
            for r, t in enumerate(cur_t):
                part = slice(8 * r, 8 * r + 8)
                o_ref[t, lanes] = cat([outs[0][part], outs[1][part]]).astype(o_ref.dtype)

    d1_pair(0, kp_ref, vp_ref, BLK // 16 - 1, True)
    for ap in range(1, BLK // 16):
        d1_pair(ap, k_ref, v_ref, ap - 1, False)


def _dilated_attention(proj, biases):
    s = proj.shape[0]
    cur = lambda which: pl.BlockSpec((SPAN, D_A), lambda c: (c, which))
    prev = lambda which: pl.BlockSpec((SPAN, D_A), lambda c: (jnp.maximum(c - 1, 0), which))
    return pl.pallas_call(
        _attn_body,
        out_shape=jax.ShapeDtypeStruct((s, D_A), BF16),
        grid=(s // SPAN,),
        in_specs=[cur(0), cur(1), cur(2), prev(1), prev(2)] + [_full(b) for b in biases],
        out_specs=pl.BlockSpec((SPAN, D_A), lambda c: (c, 0)),
        scratch_shapes=[pltpu.VMEM((SPAN, D_A), F32)] * 3,
        compiler_params=_params(("arbitrary",)),
        name="attn",
    )(proj, proj, proj, proj, proj, *biases)


def _s5_operators(a_re, a_im, b_re, b_im, c_re, c_im, log_dt, d_skip):
    L = S5_CHUNK
    lam = lax.complex(a_re.astype(F32), a_im.astype(F32))
    dt = jnp.exp(log_dt.astype(F32))[:, None]
    a_bar = jnp.exp(lam * dt)
    b_bar = ((a_bar - 1.0) / lam)[..., None] * lax.complex(b_re.astype(F32), b_im.astype(F32))
    c = lax.complex(c_re.astype(F32), c_im.astype(F32))
    j = jnp.arange(L + 1, dtype=F32)
    log_a = lam * dt
    apow = jnp.exp(log_a[None] * j[:, None, None])
    kt = jnp.einsum('gdp,jgp,gpc->gcjd', c, apow[:L], b_bar).real
    skip = d_skip.astype(F32).reshape(S5_G, S5_CH, 1, 1) * jnp.eye(S5_CH)[None, :, None, :]
    kt = (kt + skip * (jnp.arange(L) == 0)[None, None, :, None]).reshape(S5_G, S5_CH, L * S5_CH)
    p = jnp.einsum('sgp,gpc->gscp', apow[:L][::-1], b_bar).reshape(S5_G, L * S5_CH, S5_P)
    ca = jnp.einsum('gdp,tgp->gptd', c, apow[1:L + 1]).reshape(S5_G, S5_P, L * S5_CH)
    a_l = apow[L]

    def pair_blocks(x):
        g, r, w = x.shape
        x = x.reshape(S5_PAIRS, 2, r, w)
        z = jnp.zeros_like(x[:, 0])
        top = jnp.concatenate([x[:, 0], z], axis=-1)
        bot = jnp.concatenate([z, x[:, 1]], axis=-1)
        return jnp.concatenate([top, bot], axis=1)

    p2 = jnp.concatenate([pair_blocks(p.real), pair_blocks(p.imag)], axis=-1)
    q2 = jnp.concatenate([pair_blocks(ca.real), pair_blocks(-ca.imag)], axis=1)
    a_lr = a_l.real.reshape(1, S5_G * S5_P)
    a_li = a_l.imag.reshape(1, S5_G * S5_P)
    return kt, p2.astype(BF16), q2.astype(BF16), a_lr, a_li


def _s5_body(u_ref, kt_ref, p_ref, q_ref, ar_ref, ai_ref, o_ref, m_ref, w_ref, y_ref,
             ere, eim, xre, xim, sre, sim):
    rows = BLK
    gw = S5_CHUNK * S5_CH
    pw = 2 * gw
    per_vreg = LANES // S5_CH
    chunk_of_lane = lax.broadcasted_iota(jnp.int32, (rows, LANES), 1) // S5_CH

    def chunk_transpose(arrs):
        arrs = list(arrs)
        for s in (4, 2, 1):
            upper = (chunk_of_lane & s) != 0
            nxt = list(arrs)
            for i in range(per_vreg):
                if i & s:
                    continue
                lo_a, hi_a = arrs[i], arrs[i + s]
                nxt[i] = jnp.where(upper, pltpu.roll(hi_a, s * S5_CH, axis=1), lo_a)
                nxt[i + s] = jnp.where(upper, hi_a, pltpu.roll(lo_a, LANES - s * S5_CH, axis=1))
            arrs = nxt
        return arrs

    @pl.when(pl.program_id(0) == 0)
    def _():
        sre[...] = jnp.zeros_like(sre)
        sim[...] = jnp.zeros_like(sim)
        lane = lax.broadcasted_iota(jnp.int32, (S5_CH, gw), 1)

        def build(g, _):
            kt = kt_ref[g]
            for s in range(S5_CHUNK):
                blk = kt if s == 0 else jnp.where(lane >= s * S5_CH, pltpu.roll(kt, s * S5_CH, axis=1), 0.0)
                m_ref[g, s * S5_CH:(s + 1) * S5_CH, :] = blk.astype(BF16)
            return 0

        lax.fori_loop(0, S5_G, build, 0)

    for b in range(D_B // LANES):
        for a in range(S5_CHUNK // per_vreg):
            srcs = [u_ref[(per_vreg * a + i) * rows:(per_vreg * a + i + 1) * rows,
                          b * LANES:(b + 1) * LANES].astype(F32) for i in range(per_vreg)]
            for gi, arr in enumerate(chunk_transpose(srcs)):
                g = per_vreg * b + gi
                w_ref[:, g * gw + a * LANES:g * gw + (a + 1) * LANES] = arr.astype(BF16)

    for pr in range(S5_PAIRS):
        e = _dot(w_ref[:, pr * pw:(pr + 1) * pw], p_ref[pr])
        ere[:, pr * LANES:(pr + 1) * LANES] = e[:, :LANES]
        eim[:, pr * LANES:(pr + 1) * LANES] = e[:, LANES:]

    ar = ar_ref[...]
    ai = ai_ref[...]

    def step(n, carry):
        xr, xi = carry
        xre[pl.ds(n, 1), :] = xr
        xim[pl.ds(n, 1), :] = xi
        nr = ar * xr - ai * xi + ere[pl.ds(n, 1), :]
        ni = ar * xi + ai * xr + eim[pl.ds(n, 1), :]
        return nr, ni

    xr, xi = lax.fori_loop(0, rows, step, (sre[...], sim[...]))
    sre[...] = xr
    sim[...] = xi

    for pr in range(S5_PAIRS):
        xin = jnp.concatenate([xre[:, pr * LANES:(pr + 1) * LANES],
                               xim[:, pr * LANES:(pr + 1) * LANES]], axis=-1).astype(BF16)
        yc = _dot(xin, q_ref[pr])
        for half in range(2):
            g = 2 * pr + half
            cols = slice(g * gw, (g + 1) * gw)
            y = _dot(w_ref[:, cols], m_ref[g]) + yc[:, half * gw:(half + 1) * gw]
            y_ref[:, cols] = 0.5 * y * (1.0 + lax.erf(y * (2.0 ** -0.5)))

    for b in range(D_B // LANES):
        for a in range(S5_CHUNK // per_vreg):
            srcs = [y_ref[:, (per_vreg * b + i) * gw + a * LANES:(per_vreg * b + i) * gw + (a + 1) * LANES]
                    for i in range(per_vreg)]
            for ri, arr in enumerate(chunk_transpose(srcs)):
                r = per_vreg * a + ri
                o_ref[r * rows:(r + 1) * rows, b * LANES:(b + 1) * LANES] = arr.astype(BF16)


def _s5_core(proj, ops):
    kt, p2, q2, a_lr, a_li = ops
    s = proj.shape[0]
    gw = S5_CHUNK * S5_CH
    wide = S5_G * gw
    nstate = S5_G * S5_P
    return pl.pallas_call(
        _s5_body,
        out_shape=jax.ShapeDtypeStruct((s, D_B), BF16),
        grid=(s // SPAN,),
        in_specs=[pl.BlockSpec((SPAN, D_B), lambda i: (i, 3 * D_A // D_B)),
                  _full(kt), _full(p2), _full(q2), _full(a_lr), _full(a_li)],
        out_specs=pl.BlockSpec((SPAN, D_B), lambda i: (i, 0)),
        scratch_shapes=[pltpu.VMEM((S5_G, gw, gw), BF16), pltpu.VMEM((BLK, wide), BF16),
                        pltpu.VMEM((BLK, wide), F32)]
                       + [pltpu.VMEM((BLK, nstate), F32)] * 4 + [pltpu.VMEM((1, nstate), F32)] * 2,
        compiler_params=_params(("arbitrary",)),
        name="s5",
    )(proj, kt, p2, q2, a_lr, a_li)


def _split_bf16(x):
    hi = x.astype(BF16)
    lo = (x - hi.astype(F32)).astype(BF16)
    return hi, lo


def _mid_body(h_ref, ya_ref, yb_ref, wglu_ref, bglu_ref, ga_ref, gb_ref, wout_ref, gx_ref, wq_ref,
              k_ref, v_ref, wo_ref, gm_ref, wr_ref, br_ref, h_out, xn_out, logit_out):
    part = h_ref.shape[0] // MID_CHAINS
    slices = [slice(c * part, (c + 1) * part) for c in range(MID_CHAINS)]
    chains = [_mid_rows(h_ref[rows, :], ya_ref[rows, :], yb_ref[rows, :], wglu_ref, bglu_ref, ga_ref, gb_ref,
                        wout_ref, gx_ref, wq_ref, k_ref, v_ref, wo_ref, gm_ref, wr_ref, br_ref)
              for rows in slices]
    for outs in zip(*chains):
        pass
    for rows, out in zip(slices, outs):
        for ref, val in zip((h_out, xn_out, logit_out), out):
            ref[rows, :] = val


def _mid_rows(h, ya, yb, wglu_ref, bglu_ref, ga_ref, gb_ref, wout_ref, gx_ref, wq_ref,
              k_ref, v_ref, wo_ref, gm_ref, wr_ref, br_ref):
    gate = jax.nn.sigmoid(_dot(yb, wglu_ref[...]) + bglu_ref[...])
    yb2 = yb.astype(F32) * gate
    na = _rms(ya.astype(F32), ga_ref[...]).astype(BF16)
    nb = _rms(yb2, gb_ref[...]).astype(BF16)
    h1 = h + _dot(na, wout_ref[0:D_A, :]) + _dot(nb, wout_ref[D_A:D_MODEL, :])
    yield None
    q = _dot(_rms(h1, gx_ref[...]).astype(BF16), wq_ref[...]).astype(BF16)
    yield None
    heads = []
    for hd in range(H_X):
        lanes = slice(hd * HD_X, (hd + 1) * HD_X)
        s = _dot_nt(q[:, lanes], k_ref[:, lanes])
        e = jnp.exp(s - jnp.max(s, axis=-1, keepdims=True))
        heads.append(_dot(e.astype(BF16), v_ref[:, lanes]) / jnp.sum(e, axis=-1, keepdims=True))
    o = jnp.concatenate(heads, axis=-1).astype(BF16)
    yield None
    h2 = h1 + _dot(o, wo_ref[...])
    yield None
    xn = _rms(h2, gm_ref[...])
    x_hi, x_lo = _split_bf16(xn)
    part = _dot(x_hi, wr_ref[...]) + _dot(x_lo, wr_ref[...])
    logits = part + pltpu.roll(part, LANES - N_EXPERTS, axis=1) + br_ref[...]
    yield h2, _pack_rows(xn), logits


def _mid(h, ya, yb, wglu, bglu, ga, gb, wout, gx, wq, kmem, vmem, wo, gm, wr2, br):
    s = h.shape[0]
    tm = min(MID_CHAINS * ROW_TILE // 2, s)
    row = lambda w: pl.BlockSpec((tm, w), lambda i: (i, 0))
    consts = [wglu, bglu, ga, gb, wout, gx, wq, kmem, vmem, wo, gm, wr2, br]
    return pl.pallas_call(
        _mid_body,
        out_shape=[jax.ShapeDtypeStruct((s, D_MODEL), F32), jax.ShapeDtypeStruct((s, D_MODEL // 2), jnp.int32),
                   jax.ShapeDtypeStruct((s, LANES), F32)],
        grid=(s // tm,),
        in_specs=[row(D_MODEL), row(D_A), row(D_B)] + [_full(a) for a in consts],
        out_specs=[row(D_MODEL), row(D_MODEL // 2), row(LANES)],
        compiler_params=_params(("parallel",)),
        name="mid",
    )(h, ya, yb, *consts)


def _memkv_body(mem_ref, g_ref, w_ref, o_ref):
    o_ref[...] = _dot(_rms(mem_ref[...], g_ref[...]).astype(BF16), w_ref[...]).astype(BF16)


def _memkv(mem, g, w):
    n, d = mem.shape
    return pl.pallas_call(
        _memkv_body,
        out_shape=jax.ShapeDtypeStruct((n, w.shape[1]), BF16),
        compiler_params=pltpu.CompilerParams(vmem_limit_bytes=VMEM_LIMIT),
        name="memkv",
    )(mem, g.reshape(1, d), w)


def _rank_body(logit_ref, tri_ref, idx_ref, gate_ref, rank_ref, cnt_ref, carry):
    @pl.when(pl.program_id(0) == 0)
    def _():
        carry[...] = jnp.zeros_like(carry)

    tm = logit_ref.shape[0]
    logits = jnp.transpose(logit_ref[...])[:N_EXPERTS, :]
    expert = lax.broadcasted_iota(jnp.int32, (N_EXPERTS, tm), 0)
    vals, idxs = [], []
    for _ in range(TOP_K):
        mx = jnp.max(logits, axis=0, keepdims=True)
        ix = jnp.min(jnp.where(logits == mx, expert, N_EXPERTS), axis=0, keepdims=True)
        vals.append(mx)
        idxs.append(ix)
        logits = jnp.where(expert == ix, -jnp.inf, logits)
    es = [jnp.exp(v - vals[0]) for v in vals]
    den = es[0] + es[1] + es[2] + es[3]
    hits = [expert == ix for ix in idxs]
    onehot = jnp.zeros((N_EXPERTS, tm), F32)
    for hit in hits:
        onehot = onehot + jnp.where(hit, 1.0, 0.0)
    nb = tm // LANES
    blocks = jnp.concatenate([onehot[:, b * LANES:(b + 1) * LANES] for b in range(nb)], axis=0)
    inc = _dot(blocks.astype(BF16), tri_ref[...])
    run = carry[...][:, 0:1]
    before = []
    for b in range(nb):
        inc_b = inc[b * N_EXPERTS:(b + 1) * N_EXPERTS, :]
        before.append(run + inc_b - onehot[:, b * LANES:(b + 1) * LANES])
        run = run + inc_b[:, LANES - 1:LANES]
    before = jnp.concatenate(before, axis=1)
    choice = lax.broadcasted_iota(jnp.int32, (SUBLANES, tm), 0)
    idx_t = jnp.full((SUBLANES, tm), N_EXPERTS, jnp.int32)
    gate_t = jnp.zeros((SUBLANES, tm), F32)
    rank_t = jnp.zeros((SUBLANES, tm), jnp.int32)
    for k in range(TOP_K):
        rk = jnp.sum(jnp.where(hits[k], before, 0.0), axis=0, keepdims=True)
        idx_t = jnp.where(choice == k, idxs[k], idx_t)
        gate_t = jnp.where(choice == k, es[k] / den, gate_t)
        rank_t = jnp.where(choice == k, rk.astype(jnp.int32), rank_t)
    idx_ref[...] = idx_t
    gate_ref[...] = gate_t
    rank_ref[...] = rank_t
    total = jnp.broadcast_to(run, carry.shape)
    carry[...] = total
    cnt_ref[...] = total.astype(jnp.int32)


def _rank(logits):
    t = logits.shape[0]
    tm = min(RANK_TILE, t)
    cols = pl.BlockSpec((SUBLANES, tm), lambda i: (0, i))
    tri = jnp.asarray(np.triu(np.ones((LANES, LANES), np.float32)), BF16)
    return pl.pallas_call(
        _rank_body,
        out_shape=[jax.ShapeDtypeStruct((SUBLANES, t), jnp.int32), jax.ShapeDtypeStruct((SUBLANES, t), F32),
                   jax.ShapeDtypeStruct((SUBLANES, t), jnp.int32),
                   jax.ShapeDtypeStruct((N_EXPERTS, LANES), jnp.int32)],
        grid=(t // tm,),
        in_specs=[pl.BlockSpec((tm, LANES), lambda i: (i, 0)), _full(tri)],
        out_specs=[cols, cols, cols, pl.BlockSpec((N_EXPERTS, LANES), lambda i: (0, 0))],
        scratch_shapes=[pltpu.VMEM((N_EXPERTS, LANES), F32)],
        compiler_params=_params(("arbitrary",)),
        name="rank",
    )(logits, tri)


def _route(logits, tm):
    t = logits.shape[0]
    tk = t * TOP_K
    idx, gates, rank, cnt = _rank(logits)
    counts = cnt[:, 0]
    padded = (counts + tm - 1) // tm * tm
    pend = jnp.cumsum(padded)
    pstart = pend - padded
    n_rows = tk + N_EXPERTS * tm
    n_tiles = n_rows // tm
    experts = jnp.arange(N_EXPERTS, dtype=jnp.int32)
    tile_first = jnp.arange(n_tiles, dtype=jnp.int32) * tm
    last_used = jnp.max(jnp.where(padded > 0, experts, 0))
    tile_e = jnp.minimum(jnp.sum(tile_first[:, None] >= pend[None, :], axis=1), last_used).astype(jnp.int32)
    tile_rows = jnp.clip(jnp.sum(jnp.where(tile_e[:, None] == experts[None, :],
                                           (pstart + counts)[None, :], 0), axis=1) - tile_first, 0, tm)
    tile_rows = jnp.where(tile_first < pend[-1], tile_rows, 0).astype(jnp.int32)
    group = jnp.cumsum(jnp.concatenate([jnp.zeros((1,), jnp.int32),
                                        (tile_e[1:] != tile_e[:-1]).astype(jnp.int32)]))
    tile_slot = (group % 2).astype(jnp.int32)
    later = (experts[None, :] > experts[:, None]) & (padded > 0)[None, :]
    next_e = jnp.min(jnp.where(later, experts[None, :], N_EXPERTS), axis=1)
    next_e = jnp.where(next_e < N_EXPERTS, next_e, -1).astype(jnp.int32)
    tile_next = jnp.sum(jnp.where(tile_e[:, None] == experts[None, :], next_e[None, :], 0), axis=1).astype(jnp.int32)
    base = jnp.sum(jnp.where(idx[:TOP_K, :, None] == experts, pstart, 0), axis=-1)
    dest = rank[:TOP_K] + base
    return gates[:TOP_K].T, dest, n_rows, tile_e, tile_rows, tile_slot, tile_next


def _sc_mesh():
    return plsc.VectorSubcoreMesh(core_axis_name="c", subcore_axis_name="s",
                                  num_cores=SC_CORES, num_subcores=SC_SUBCORES)


def _sc_worker():
    return lax.axis_index("s") * SC_CORES + lax.axis_index("c")


def _scatter_rows(x, dest, n_rows):
    t, d = x.shape
    per_worker = t // (SC_CORES * SC_SUBCORES)
    chunks = per_worker // SC_ROWS

    assert chunks % 2 == 0

    @functools.partial(
        pl.kernel, mesh=_sc_mesh(),
        out_type=jax.ShapeDtypeStruct((n_rows, d), x.dtype),
        scratch_types=[pltpu.VMEM((SC_ROWS, d), x.dtype)] * 2 + [pltpu.VMEM((SC_ROWS,), jnp.int32)] * (2 * TOP_K)
                      + [pltpu.SemaphoreType.DMA((2,)), pltpu.SemaphoreType.DMA((2, TOP_K))],
    )
    def scatter(x_hbm, *rest):
        dest_hbm, out_hbm = rest[:TOP_K], rest[TOP_K]
        rows_v = rest[TOP_K + 1:TOP_K + 3]
        idx_v = (rest[TOP_K + 3:2 * TOP_K + 3], rest[2 * TOP_K + 3:3 * TOP_K + 3])
        lsem, ssem = rest[3 * TOP_K + 3], rest[3 * TOP_K + 4]
        base = _sc_worker() * per_worker

        def rows_at(c):
            return pl.ds(pl.multiple_of(base + c * SC_ROWS, SC_ROWS), SC_ROWS)

        def loaded(c, s):
            return pltpu.make_async_copy(x_hbm.at[rows_at(c)], rows_v[s], lsem.at[s])

        def load(c, s):
            loaded(c, s).start()
            for k in range(TOP_K):
                pltpu.sync_copy(dest_hbm[k].at[rows_at(c)], idx_v[s][k])

        def scattered(s):
            return [pltpu.make_async_copy(rows_v[s], out_hbm.at[idx_v[s][k]], ssem.at[s, k]) for k in range(TOP_K)]

        load(0, 0)

        @pl.loop(0, chunks, step=2)
        def _(c0):
            for s in range(2):
                c = c0 + s
                loaded(c, s).wait()
                for cp in scattered(s):
                    cp.start()

                @pl.when(c >= 1)
                def _():
                    for cp in scattered(1 - s):
                        cp.wait()

                @pl.when(c + 1 < chunks)
                def _():
                    load(c + 1, 1 - s)

        for cp in scattered(1):
            cp.wait()

    return scatter(x, *[dest[k] for k in range(TOP_K)])


def _gather_rows(table, idx):
    n, d = table.shape
    b = idx.shape[0]
    per_worker = b // (SC_CORES * SC_SUBCORES)
    chunks = per_worker // SC_ROWS
    assert chunks % 2 == 0

    @functools.partial(
        pl.kernel, mesh=_sc_mesh(),
        out_type=jax.ShapeDtypeStruct((b, d), table.dtype),
        scratch_types=[pltpu.VMEM((chunks, SC_ROWS), jnp.int32)] + [pltpu.VMEM((SC_ROWS, d), table.dtype)] * 2
                      + [pltpu.SemaphoreType.DMA((2,)), pltpu.SemaphoreType.DMA((2,))],
    )
    def gather(table_hbm, idx_hbm, out_hbm, idx_v, rows0, rows1, gsem, wsem):
        rows_v = (rows0, rows1)
        worker = _sc_worker()
        base = worker * per_worker
        pltpu.sync_copy(idx_hbm.at[pl.ds(pl.multiple_of(worker * chunks, chunks), chunks)], idx_v)

        def rows_at(c):
            return pl.ds(pl.multiple_of(base + c * SC_ROWS, SC_ROWS), SC_ROWS)

        def fetched(c, s):
            return pltpu.make_async_copy(table_hbm.at[idx_v.at[c]], rows_v[s], gsem.at[s])

        def written(c, s):
            return pltpu.make_async_copy(rows_v[s], out_hbm.at[rows_at(c)], wsem.at[s])

        fetched(0, 0).start()

        @pl.loop(0, chunks, step=2)
        def _(c0):
            for s in range(2):
                c = c0 + s
                fetched(c, s).wait()
                written(c, s).start()

                @pl.when(c >= 1)
                def _():
                    written(c - 1, 1 - s).wait()

                @pl.when(c + 1 < chunks)
                def _():
                    fetched(c + 1, 1 - s).start()

        written(chunks - 1, 1).wait()

    return gather(table, idx.reshape(b // SC_ROWS, SC_ROWS))


def _moe_body(te_ref, tv_ref, sl_ref, nx_ref, x_ref, w1_hbm, b1_ref, w2_hbm, b2_ref, o_ref,
              w1f, w2f, w1b, w2b, sem, *, layer):
    i = pl.program_id(0)
    e = te_ref[i]
    slot = sl_ref[i]
    new_expert = (i == 0) | (e != te_ref[jnp.maximum(i - 1, 0)])

    def weight_copies(expert, s):
        rows1 = w1f.shape[1] // MOE_W1_PARTS
        rows2 = w2f.shape[1] // MOE_W2_PARTS
        c1 = [pltpu.make_async_copy(w1_hbm.at[layer, expert, pl.ds(q * rows1, rows1)],
                                    w1f.at[s, pl.ds(q * rows1, rows1)], sem.at[s, q])
              for q in range(MOE_W1_PARTS)]
        c2 = [pltpu.make_async_copy(w2_hbm.at[layer, expert, pl.ds(q * rows2, rows2)],
                                    w2f.at[s, pl.ds(q * rows2, rows2)], sem.at[s, MOE_W1_PARTS + q])
              for q in range(MOE_W2_PARTS)]
        return c1 + c2

    @pl.when(i == 0)
    def _():
        for c in weight_copies(e, slot):
            c.start()

    @pl.when(new_expert)
    def _():
        for c in weight_copies(e, slot):
            c.wait()
        nxt = nx_ref[i]

        @pl.when(nxt >= 0)
        def _():
            for c in weight_copies(nxt, 1 - slot):
                c.start()

        w1b[...] = w1f[slot].astype(BF16)
        w2b[...] = w2f[slot].astype(BF16)

    def expert(rows):
        row = lax.broadcasted_iota(jnp.int32, (rows, x_ref.shape[1]), 0)
        lo, hi = _unpack_rows(jnp.where(row < tv_ref[i], x_ref[0:rows, :], 0))
        x = jnp.concatenate([lo, hi], axis=-1).astype(BF16)
        hb = _dot(x, w1b[...]) + b1_ref[0]
        x_glu = jnp.minimum(hb[:, :D_FF], SWIGLU_LIMIT)
        x_lin = jnp.clip(hb[:, D_FF:], -SWIGLU_LIMIT, SWIGLU_LIMIT)
        act = x_glu * jax.nn.sigmoid(SWIGLU_ALPHA * x_glu) * (x_lin + 1.0)
        o_ref[0:rows, :] = _pack_rows(_dot(act.astype(BF16), w2b[...]) + b2_ref[0])

    step = x_ref.shape[0] // MOE_ROW_PATHS
    for path in range(1, MOE_ROW_PATHS + 1):
        rows = path * step

        @pl.when((tv_ref[i] > rows - step) & (tv_ref[i] <= rows))
        def _(rows=rows):
            expert(rows)
            if rows < x_ref.shape[0]:
                o_ref[rows:, :] = jnp.zeros((x_ref.shape[0] - rows, o_ref.shape[1]), o_ref.dtype)

    @pl.when(tv_ref[i] == 0)
    def _():
        o_ref[...] = jnp.zeros_like(o_ref)


def _moe_experts(xs, tile_e, tile_rows, tile_slot, tile_next, w1, b1, w2, b2, layer):
    n_rows = xs.shape[0]
    tm = MOE_TILE
    nl, ne, d, ff2 = w1.shape
    bias_map = lambda i, te, tv, sl, nx: (layer, te[i], 0, 0)
    grid_spec = pltpu.PrefetchScalarGridSpec(
        num_scalar_prefetch=4,
        grid=(n_rows // tm,),
        in_specs=[pl.BlockSpec((tm, d // 2), lambda i, *_: (i, 0)),
                  pl.BlockSpec(memory_space=pl.ANY),
                  pl.BlockSpec((None, 1, 1, ff2), bias_map),
                  pl.BlockSpec(memory_space=pl.ANY),
                  pl.BlockSpec((None, 1, 1, d), bias_map)],
        out_specs=pl.BlockSpec((tm, d // 2), lambda i, *_: (i, 0)),
        scratch_shapes=[pltpu.VMEM((2, d, ff2), F32), pltpu.VMEM((2, ff2 // 2, d), F32),
                        pltpu.VMEM((d, ff2), BF16), pltpu.VMEM((ff2 // 2, d), BF16),
                        pltpu.SemaphoreType.DMA((2, MOE_W1_PARTS + MOE_W2_PARTS))],
    )
    return pl.pallas_call(
        functools.partial(_moe_body, layer=layer),
        out_shape=jax.ShapeDtypeStruct((n_rows, d // 2), jnp.int32),
        grid_spec=grid_spec,
        compiler_params=_params(("arbitrary",)),
        name="moe",
    )(tile_e, tile_rows, tile_slot, tile_next, xs, w1, b1.reshape(nl, ne, 1, ff2), w2, b2.reshape(nl, ne, 1, d))


def _combined(h_ref, y_refs, gate_ref):
    gates = gate_ref[...]
    lo = jnp.zeros(y_refs[0].shape, F32)
    hi = jnp.zeros(y_refs[0].shape, F32)
    for k, y_ref in enumerate(y_refs):
        yl, yh = _unpack_rows(y_ref[...])
        lo = lo + yl * gates[:, k:k + 1]
        hi = hi + yh * gates[:, k:k + 1]
    return h_ref[...] + jnp.concatenate([lo, hi], axis=-1)


def _combine_final_body(h_ref, y0_ref, y1_ref, y2_ref, y3_ref, gate_ref, g_ref, *rest, first_step):
    out_hbm, buf, sem = rest[-3:]
    i = pl.program_id(0)
    n = pl.num_programs(0)
    slot = i % 2
    per_step = h_ref.shape[0] // BLK
    steps_per_span = NRES // per_step

    def writes(step, s):
        span = (first_step + step) // steps_per_span
        r0 = ((first_step + step) % steps_per_span) * per_step
        return [pltpu.make_async_copy(buf.at[s, pl.ds(rr * BLK, BLK), :], out_hbm.at[span, :, r0 + rr, :],
                                      sem.at[s, rr]) for rr in range(per_step)]

    @pl.when(i >= 2)
    def _():
        for cp in writes(i - 2, slot):
            cp.wait()

    buf[slot] = _rms(_combined(h_ref, (y0_ref, y1_ref, y2_ref, y3_ref), gate_ref), g_ref[...])
    for cp in writes(i, slot):
        cp.start()

    @pl.when(i == n - 1)
    def _():
        for cp in writes(i, slot):
            cp.wait()

        @pl.when(i >= 1)
        def _():
            for cp in writes(i - 1, 1 - slot):
                cp.wait()


def _combine_final(h, ys, gates, g):
    s, d = h.shape
    tm = min(ROW_TILE, s)
    nt = s // tm // len(ys)
    out = None
    for part, y in enumerate(ys):
        first = part * nt
        in_specs = ([pl.BlockSpec((tm, d), lambda i, first=first: (first + i, 0))]
                    + [pl.BlockSpec((tm, d // 2), lambda i, k=k: (k * nt + i, 0)) for k in range(TOP_K)]
                    + [pl.BlockSpec((tm, TOP_K), lambda i, first=first: (first + i, 0)),
                       pl.BlockSpec((1, d), lambda i: (0, 0))])
        args = [h, y, y, y, y, gates, g.reshape(1, d)]
        aliases = {}
        if out is not None:
            in_specs.append(pl.BlockSpec(memory_space=pl.ANY))
            args.append(out)
            aliases = {len(args) - 1: 0}
        out = pl.pallas_call(
            functools.partial(_combine_final_body, first_step=first),
            out_shape=jax.ShapeDtypeStruct((s // SPAN, BLK, NRES, d), F32),
            grid=(nt,),
            in_specs=in_specs,
            out_specs=pl.BlockSpec(memory_space=pl.ANY),
            scratch_shapes=[pltpu.VMEM((2, tm, d), F32), pltpu.SemaphoreType.DMA((2, tm // BLK))],
            input_output_aliases=aliases,
            compiler_params=_params(("arbitrary",)),
            name="combine_final",
        )(*args)
    return out.reshape(s, d)


def _layer(state, mem, biases, p, l, g_final):
    row = lambda a: a.reshape(1, -1).astype(F32)
    q_scale = jnp.concatenate([jnp.full((D_A,), HD_A ** -0.5 * LOG2E, F32), jnp.ones((D_IN - D_A,), F32)])
    w_in = (p["w_in"][l] * q_scale).astype(BF16)
    if l == 0:
        h, proj = state, _inproj(state, p["norm_mix"][l], w_in)
    else:
        h, proj = _combine_inproj(*state, p["norm_mix"][l], w_in)
    ya = _dilated_attention(proj, biases)
    ops = _s5_operators(p["s5_a_re"][l], p["s5_a_im"][l], p["s5_b_re"][l], p["s5_b_im"][l],
                        p["s5_c_re"][l], p["s5_c_im"][l], p["s5_log_dt"][l], p["s5_d"][l])
    yb = _s5_core(proj, ops)
    kv = _memkv(mem, p["norm_mem"][l], p["w_xkv"][l].astype(BF16))
    wr = p["w_router"][l].astype(F32)
    wr_hi = wr.astype(BF16)
    wr2 = jnp.pad(jnp.concatenate([wr_hi, (wr - wr_hi.astype(F32)).astype(BF16)], axis=1),
                  ((0, 0), (0, LANES - 2 * N_EXPERTS)))
    br = jnp.pad(p["b_router"][l].astype(F32), (0, LANES - N_EXPERTS)).reshape(1, LANES)
    h2, xn, logits = _mid(
        h, ya, yb, p["w_glu"][l].astype(BF16), row(p["b_glu"][l]), row(p["g_out_attn"][l]),
        row(p["g_out_ssm"][l]), p["w_out"][l].astype(BF16), row(p["norm_xattn"][l]),
        (p["w_xq"][l] * (HD_X ** -0.5)).astype(BF16), kv[:, :D_X], kv[:, D_X:],
        p["w_xo"][l].astype(BF16), row(p["norm_moe"][l]), wr2, br)
    gates, dest, n_rows, tile_e, tile_rows, tile_slot, tile_next = _route(logits, MOE_TILE)
    out = _moe_experts(_scatter_rows(xn, dest, n_rows), tile_e, tile_rows, tile_slot, tile_next,
                       p["w1"], p["b1"], p["w2"], p["b2"], l)
    if l < DEPTH - 1:
        return h2, _gather_rows(out, dest.reshape(-1)), gates
    part = h2.shape[0] // FINAL_PARTS
    ys = [_gather_rows(out, dest[:, i * part:(i + 1) * part].reshape(-1)) for i in range(FINAL_PARTS)]
    return _combine_final(h2, ys, gates, g_final)


def kernel(x, mem, rel_bias, norm_mix, w_in, s5_a_re, s5_a_im, s5_b_re, s5_b_im, s5_c_re, s5_c_im, s5_log_dt, s5_d, w_glu, b_glu, g_out_attn, g_out_ssm, w_out, norm_xattn, norm_mem, w_xq, w_xkv, w_xo, norm_moe, w_router, b_router, w1, b1, w2, b2, norm_final):
    p = dict(norm_mix=norm_mix, w_in=w_in, s5_a_re=s5_a_re, s5_a_im=s5_a_im, s5_b_re=s5_b_re,
             s5_b_im=s5_b_im, s5_c_re=s5_c_re, s5_c_im=s5_c_im, s5_log_dt=s5_log_dt, s5_d=s5_d,
             w_glu=w_glu, b_glu=b_glu, g_out_attn=g_out_attn, g_out_ssm=g_out_ssm, w_out=w_out,
             norm_xattn=norm_xattn, norm_mem=norm_mem, w_xq=w_xq, w_xkv=w_xkv, w_xo=w_xo,
             norm_moe=norm_moe, w_router=w_router, b_router=b_router, w1=w1, b1=b1, w2=w2, b2=b2)
    biases = [_attn_bias(rel_bias, window, dil, perm)
              for (window, dil), perm in zip(WIN_DIL, (_PERM_D1, _PERM_D4, _PERM_D16))]
    outs = []
    for b in range(x.shape[0]):
        h = _to_span_layout(x[b])
        for l in range(DEPTH):
            h = _layer(h, mem[b], biases, p, l, norm_final)
        outs.append(h)
    return jnp.stack(outs)
```

```python
import functools
import math

import jax
import jax.numpy as jnp
import numpy as np
from jax import lax
from jax.experimental import pallas as pl
from jax.experimental.pallas import tpu as pltpu
from jax.experimental.pallas import tpu_sc as plsc

F32 = jnp.float32
BF16 = jnp.bfloat16

D_MODEL = 1024
DEPTH = 2
EPS = 1e-5
NEG_INF = -1e30
LOG2E = math.log2(math.e)
H_A = 8
HD_A = 64
D_A = H_A * HD_A
WIN_DIL = ((128, 1), (512, 4), (2048, 16))
BLK = 128
D_B = D_MODEL - D_A
S5_CH = 16
S5_G = D_B // S5_CH
S5_P = 64
D_IN = 3 * D_A + D_B
NUM_BUCKETS = 32
REL_MAX_DIST = 2048
H_X = 4
HD_X = 128
D_X = H_X * HD_X
N_EXPERTS = 32
TOP_K = 4
D_FF = D_MODEL
SWIGLU_ALPHA = 1.702
SWIGLU_LIMIT = 7.0

LANES = 128
SUBLANES = 8
NRES = WIN_DIL[-1][1]
SPAN = NRES * BLK
S5_CHUNK = NRES
S5_PAIRS = S5_G // 2
VMEM_LIMIT = 56 * 1024 * 1024

SC_CORES = 2
SC_SUBCORES = 16
SC_ROWS = 64

ROW_TILE = 512
ROW_CHAINS = 2
MID_CHAINS = 4
RANK_TILE = 1024
MOE_TILE = 512
MOE_ROW_PATHS = 4
MOE_W1_PARTS = 4
MOE_W2_PARTS = 2


def _params(sem):
    return pltpu.CompilerParams(dimension_semantics=sem, vmem_limit_bytes=VMEM_LIMIT)


def _rms(x, g):
    return x * lax.rsqrt(jnp.mean(x * x, axis=-1, keepdims=True) + EPS) * g


def _dot(a, b):
    return jnp.dot(a, b, preferred_element_type=F32)


def _dot_nt(a, b):
    return lax.dot_general(a, b, (((1,), (1,)), ((), ())), preferred_element_type=F32)


def _full(a):
    return pl.BlockSpec(a.shape, lambda *_: (0,) * a.ndim)


def _pack_rows(x):
    c = x.shape[1] // 2
    lo = lax.bitcast_convert_type(x[:, :c].astype(BF16).astype(F32), jnp.uint32)
    hi = lax.bitcast_convert_type(x[:, c:].astype(BF16).astype(F32), jnp.uint32)
    return lax.bitcast_convert_type(lax.shift_right_logical(lo, jnp.uint32(16)) | hi, jnp.int32)


def _unpack_rows(p):
    u = lax.bitcast_convert_type(p, jnp.uint32)
    lo = lax.bitcast_convert_type(lax.shift_left(u, jnp.uint32(16)), F32)
    hi = lax.bitcast_convert_type(u & jnp.uint32(0xFFFF0000), F32)
    return lo, hi


def _to_span_layout(x):
    s = x.shape[0]
    return x.reshape(s // SPAN, BLK, NRES, -1).transpose(0, 2, 1, 3).reshape(s, -1)


def _from_span_layout(x):
    s = x.shape[0]
    return x.reshape(s // SPAN, NRES, BLK, -1).transpose(0, 2, 1, 3).reshape(s, -1)


def _row_chains(n_rows):
    part = n_rows // ROW_CHAINS
    return [slice(c * part, (c + 1) * part) for c in range(ROW_CHAINS)]


def _inproj_body(h_ref, g_ref, w_ref, o_ref):
    for rows in _row_chains(h_ref.shape[0]):
        xn = _rms(h_ref[rows, :], g_ref[...]).astype(BF16)
        o_ref[rows, :] = _dot(xn, w_ref[...]).astype(BF16)


def _inproj(h, g, w):
    s, d = h.shape
    n = w.shape[1]
    tm = min(ROW_TILE, s)
    return pl.pallas_call(
        _inproj_body,
        out_shape=jax.ShapeDtypeStruct((s, n), BF16),
        grid=(s // tm,),
        in_specs=[pl.BlockSpec((tm, d), lambda i: (i, 0)),
                  pl.BlockSpec((1, d), lambda i: (0, 0)),
                  pl.BlockSpec((d, n), lambda i: (0, 0))],
        out_specs=pl.BlockSpec((tm, n), lambda i: (i, 0)),
        compiler_params=_params(("parallel",)),
        name="inproj",
    )(h, g.reshape(1, d), w)


def _combine_inproj_body(h_ref, y0_ref, y1_ref, y2_ref, y3_ref, gate_ref, g_ref, w_ref, h_out, o_ref):
    chains = _row_chains(h_ref.shape[0])
    hs = [_combined(h_ref.at[rows, :], [y.at[rows, :] for y in (y0_ref, y1_ref, y2_ref, y3_ref)],
                    gate_ref.at[rows, :]) for rows in chains]
    for rows, h in zip(chains, hs):
        h_out[rows, :] = h
    for rows, h in zip(chains, hs):
        o_ref[rows, :] = _dot(_rms(h, g_ref[...]).astype(BF16), w_ref[...]).astype(BF16)


def _combine_inproj(h, y, gates, g, w):
    s, d = h.shape
    n = w.shape[1]
    tm = min(ROW_TILE, s)
    nt = s // tm
    return pl.pallas_call(
        _combine_inproj_body,
        out_shape=[jax.ShapeDtypeStruct((s, d), F32), jax.ShapeDtypeStruct((s, n), BF16)],
        grid=(nt,),
        in_specs=[pl.BlockSpec((tm, d), lambda i: (i, 0))]
                 + [pl.BlockSpec((tm, d // 2), lambda i, k=k: (k * nt + i, 0)) for k in range(TOP_K)]
                 + [pl.BlockSpec((tm, TOP_K), lambda i: (i, 0)), pl.BlockSpec((1, d), lambda i: (0, 0)),
                    pl.BlockSpec((d, n), lambda i: (0, 0))],
        out_specs=[pl.BlockSpec((tm, d), lambda i: (i, 0)), pl.BlockSpec((tm, n), lambda i: (i, 0))],
        compiler_params=_params(("parallel",)),
        name="combine_inproj",
    )(h, y, y, y, y, gates, g.reshape(1, d), w)


def _t5_bucket(n):
    max_exact = NUM_BUCKETS // 2
    nf = jnp.maximum(n, 1).astype(F32)
    large = max_exact + (jnp.log(nf / max_exact) / math.log(REL_MAX_DIST / max_exact)
                         * (NUM_BUCKETS - max_exact)).astype(jnp.int32)
    large = jnp.minimum(large, NUM_BUCKETS - 1)
    return jnp.where(n < max_exact, n, large)


def _attn_bias(rel_bias, window, dil, perm):
    steps = window // dil
    perm = jnp.asarray(perm, jnp.int32)
    qi = perm[:, None]
    ki = jnp.concatenate([perm, BLK + perm])[None, :]
    dist = BLK + qi - ki
    in_win = (dist >= 0) & (dist <= steps)
    bucket = _t5_bucket(jnp.clip(dist, 0, steps) * dil)
    onehot = (bucket[:, :, None] == jnp.arange(NUM_BUCKETS, dtype=jnp.int32)).astype(F32)
    bias = jnp.einsum('qkb,bh->hqk', onehot, rel_bias.astype(F32), precision=lax.Precision.HIGHEST)
    bias = jnp.where(in_win[None], bias * LOG2E, NEG_INF)
    return bias.reshape(H_A // 2, 2 * BLK, 2 * BLK)


_PERM_D1 = [NRES * jl + r for r in range(NRES) for jl in range(BLK // NRES)]
_PERM_D4 = [4 * jl + i for i in range(4) for jl in range(BLK // 4)]
_PERM_D16 = list(range(BLK))


def _attn_body(q_ref, k_ref, v_ref, kp_ref, vp_ref, b1_ref, b4_ref, b16_ref, o_ref, acc, mst, lst):
    has_prev = pl.program_id(0) > 0
    lane = lax.broadcasted_iota(jnp.int32, (1, LANES), 1)
    lo = lane < HD_A
    mlo = lo.astype(BF16)
    mhi = (~lo).astype(BF16)
    col = lax.broadcasted_iota(jnp.int32, (2 * BLK, 2 * BLK), 1)
    ones = jnp.ones((2 * BLK, LANES), BF16)

    def tile(q2, kk, vv, bias, mask_prev):
        qs = jnp.concatenate([q2 * mlo, q2 * mhi], axis=0)
        s = _dot_nt(qs, kk) + bias
        if mask_prev:
            s = jnp.where(jnp.logical_or(has_prev, col >= BLK), s, NEG_INF)
        m = jnp.max(s, axis=-1, keepdims=True)
        e = jnp.exp2((s - m).astype(BF16))
        oa = _dot(e, jnp.concatenate([vv, ones], axis=1))
        o = oa[:, :LANES]
        l = oa[:, LANES:]
        return (jnp.where(lo, m[:BLK], m[BLK:]), jnp.where(lo, l[:BLK], l[BLK:]),
                jnp.where(lo, o[:BLK], o[BLK:]))

    def merge(prev, cur):
        mp, lp, ap = prev
        mc, lc, ac = cur
        mn = jnp.maximum(mp, mc)
        a = jnp.exp2(mp - mn)
        b = jnp.exp2(mc - mn)
        return mn, a * lp + b * lc, a * ap + b * ac

    def cat(xs):
        return jnp.concatenate(xs, axis=0)


    for r in range(NRES):
        rows = pl.ds(r * BLK, BLK)
        for hp in range(H_A // 2):
            lanes = slice(hp * LANES, (hp + 1) * LANES)
            kk = cat([kp_ref[rows, lanes], k_ref[rows, lanes]])
            vv = cat([vp_ref[rows, lanes], v_ref[rows, lanes]])
            m2, l2, o2 = tile(q_ref[rows, lanes], kk, vv, b16_ref[hp], True)
            mst[rows, lanes] = m2
            lst[rows, lanes] = l2
            acc[rows, lanes] = o2

    for r4 in range(4):
        for b in range(4):
            def chunk_rows(bb):
                return [pl.ds(4 * BLK * i + BLK * r4 + 32 * bb, 32) for i in range(4)]
            rows = chunk_rows(b)
            prows = chunk_rows(3 if b == 0 else b - 1)
            kprev, vprev = (kp_ref, vp_ref) if b == 0 else (k_ref, v_ref)
            for hp in range(H_A // 2):
                lanes = slice(hp * LANES, (hp + 1) * LANES)
                q2 = cat([q_ref[rr, lanes] for rr in rows])
                kk = cat([kprev[rr, lanes] for rr in prows] + [k_ref[rr, lanes] for rr in rows])
                vv = cat([vprev[rr, lanes] for rr in prows] + [v_ref[rr, lanes] for rr in rows])
                cur = tile(q2, kk, vv, b4_ref[hp], b == 0)
                prev = (cat([mst[rr, lanes] for rr in rows]), cat([lst[rr, lanes] for rr in rows]),
                        cat([acc[rr, lanes] for rr in rows]))
                mn, ln, an = merge(prev, cur)
                for i, rr in enumerate(rows):
                    part = slice(32 * i, 32 * (i + 1))
                    mst[rr, lanes] = mn[part]
                    lst[rr, lanes] = ln[part]
                    acc[rr, lanes] = an[part]

    def d1_pair(ap, kprev, vprev, prev_ap, mask_prev):
        def tiles(a_):
            return [pl.ds(BLK * r + 16 * a_, 16) for r in range(NRES)]
        cur_t = tiles(ap)
        prev_t = tiles(prev_ap)

        def halves(ref, ts, lanes):
            xs = [ref[t, lanes].astype(F32) for t in ts]
            return cat([x[:8] for x in xs]).astype(BF16), cat([x[8:] for x in xs]).astype(BF16)

        for hp in range(H_A // 2):
            lanes = slice(hp * LANES, (hp + 1) * LANES)
            q_e, q_o = halves(q_ref, cur_t, lanes)
            k_e, k_o = halves(k_ref, cur_t, lanes)
            v_e, v_o = halves(v_ref, cur_t, lanes)
            _, k_p = halves(kprev, prev_t, lanes)
            _, v_p = halves(vprev, prev_t, lanes)
            cur_e = tile(q_e, cat([k_p, k_e]), cat([v_p, v_e]), b1_ref[hp], mask_prev)
            cur_o = tile(q_o, cat([k_e, k_o]), cat([v_e, v_o]), b1_ref[hp], False)
            ms = [mst[t, lanes] for t in cur_t]
            ls = [lst[t, lanes] for t in cur_t]
            ac = [acc[t, lanes] for t in cur_t]
            outs = []
            for half, cur in ((0, cur_e), (1, cur_o)):
                part = slice(8 * half, 8 * half + 8)
                prev = (cat([x[part] for x in ms]), cat([x[part] for x in ls]), cat([x[part] for x in ac]))
                _, ln, an = merge(prev, cur)
                outs.append(an / ln)
            for r, t in enumerate(cur_t):
                part = slice(8 * r, 8 * r + 8)
                o_ref[t, lanes] = cat([outs[0][part], outs[1][part]]).astype(o_ref.dtype)

    d1_pair(0, kp_ref, vp_ref, BLK // 16 - 1, True)
    for ap in range(1, BLK // 16):
        d1_pair(ap, k_ref, v_ref, ap - 1, False)


def _dilated_attention(proj, biases):
    s = proj.shape[0]
    cur = lambda which: pl.BlockSpec((SPAN, D_A), lambda c: (c, which))
    prev = lambda which: pl.BlockSpec((SPAN, D_A), lambda c: (jnp.maximum(c - 1, 0), which))
    return pl.pallas_call(
        _attn_body,
        out_shape=jax.ShapeDtypeStruct((s, D_A), BF16),
        grid=(s // SPAN,),
        in_specs=[cur(0), cur(1), cur(2), prev(1), prev(2)] + [_full(b) for b in biases],
        out_specs=pl.BlockSpec((SPAN, D_A), lambda c: (c, 0)),
        scratch_shapes=[pltpu.VMEM((SPAN, D_A), F32)] * 3,
        compiler_params=_params(("arbitrary",)),
        name="attn",
    )(proj, proj, proj, proj, proj, *biases)


def _s5_operators(a_re, a_im, b_re, b_im, c_re, c_im, log_dt, d_skip):
    L = S5_CHUNK
    lam = lax.complex(a_re.astype(F32), a_im.astype(F32))
    dt = jnp.exp(log_dt.astype(F32))[:, None]
    a_bar = jnp.exp(lam * dt)
    b_bar = ((a_bar - 1.0) / lam)[..., None] * lax.complex(b_re.astype(F32), b_im.astype(F32))
    c = lax.complex(c_re.astype(F32), c_im.astype(F32))
    j = jnp.arange(L + 1, dtype=F32)
    log_a = lam * dt
    apow = jnp.exp(log_a[None] * j[:, None, None])
    kt = jnp.einsum('gdp,jgp,gpc->gcjd', c, apow[:L], b_bar).real
    skip = d_skip.astype(F32).reshape(S5_G, S5_CH, 1, 1) * jnp.eye(S5_CH)[None, :, None, :]
    kt = (kt + skip * (jnp.arange(L) == 0)[None, None, :, None]).reshape(S5_G, S5_CH, L * S5_CH)
    p = jnp.einsum('sgp,gpc->gscp', apow[:L][::-1], b_bar).reshape(S5_G, L * S5_CH, S5_P)
    ca = jnp.einsum('gdp,tgp->gptd', c, apow[1:L + 1]).reshape(S5_G, S5_P, L * S5_CH)
    a_l = apow[L]

    def pair_blocks(x):
        g, r, w = x.shape
        x = x.reshape(S5_PAIRS, 2, r, w)
        z = jnp.zeros_like(x[:, 0])
        top = jnp.concatenate([x[:, 0], z], axis=-1)
        bot = jnp.concatenate([z, x[:, 1]], axis=-1)
        return jnp.concatenate([top, bot], axis=1)

    p2 = jnp.concatenate([pair_blocks(p.real), pair_blocks(p.imag)], axis=-1)
    q2 = jnp.concatenate([pair_blocks(ca.real), pair_blocks(-ca.imag)], axis=1)
    a_lr = a_l.real.reshape(1, S5_G * S5_P)
    a_li = a_l.imag.reshape(1, S5_G * S5_P)
    return kt, p2.astype(BF16), q2.astype(BF16), a_lr, a_li


def _s5_body(u_ref, un_ref, kt_ref, p_ref, q_ref, ar_ref, ai_ref, o_ref, m_ref, w_ref, y_ref,
             ere, eim, xre, xim, sre, sim):
    rows = BLK
    gw = S5_CHUNK * S5_CH
    pw = 2 * gw
    per_vreg = LANES // S5_CH
    chunk_of_lane = lax.broadcasted_iota(jnp.int32, (rows, LANES), 1) // S5_CH

    def chunk_transpose(arrs):
        arrs = list(arrs)
        for s in (4, 2, 1):
            upper = (chunk_of_lane & s) != 0
            nxt = list(arrs)
            for i in range(per_vreg):
                if i & s:
                    continue
                lo_a, hi_a = arrs[i], arrs[i + s]
                nxt[i] = jnp.where(upper, pltpu.roll(hi_a, s * S5_CH, axis=1), lo_a)
                nxt[i + s] = jnp.where(upper, hi_a, pltpu.roll(lo_a, LANES - s * S5_CH, axis=1))
            arrs = nxt
        return arrs

    @pl.when(pl.program_id(0) == 0)
    def _():
        sre[...] = jnp.zeros_like(sre)
        sim[...] = jnp.zeros_like(sim)
        lane = lax.broadcasted_iota(jnp.int32, (S5_CH, gw), 1)

        def build(g, _):
            kt = kt_ref[g]
            for s in range(S5_CHUNK):
                blk = kt if s == 0 else jnp.where(lane >= s * S5_CH, pltpu.roll(kt, s * S5_CH, axis=1), 0.0)
                m_ref[g, s * S5_CH:(s + 1) * S5_CH, :] = blk.astype(BF16)
            return 0

        lax.fori_loop(0, S5_G, build, 0)

    def relayout_in(src_ref, slot):
        for b in range(D_B // LANES):
            for a in range(S5_CHUNK // per_vreg):
                srcs = [src_ref[(per_vreg * a + i) * rows:(per_vreg * a + i + 1) * rows,
                                b * LANES:(b + 1) * LANES].astype(F32) for i in range(per_vreg)]
                for gi, arr in enumerate(chunk_transpose(srcs)):
                    g = per_vreg * b + gi
                    w_ref[slot, :, g * gw + a * LANES:g * gw + (a + 1) * LANES] = arr.astype(BF16)

    cur = pl.program_id(0) % 2

    @pl.when(pl.program_id(0) == 0)
    def _():
        relayout_in(u_ref, 0)

    for pr in range(S5_PAIRS):
        e = _dot(w_ref[cur, :, pr * pw:(pr + 1) * pw], p_ref[pr])
        ere[:, pr * LANES:(pr + 1) * LANES] = e[:, :LANES]
        eim[:, pr * LANES:(pr + 1) * LANES] = e[:, LANES:]

    ar = ar_ref[...]
    ai = ai_ref[...]

    def step(n, carry):
        xr, xi = carry
        xre[pl.ds(n, 1), :] = xr
        xim[pl.ds(n, 1), :] = xi
        nr = ar * xr - ai * xi + ere[pl.ds(n, 1), :]
        ni = ar * xi + ai * xr + eim[pl.ds(n, 1), :]
        return nr, ni

    xr, xi = lax.fori_loop(0, rows, step, (sre[...], sim[...]))
    sre[...] = xr
    sim[...] = xi

    relayout_in(un_ref, 1 - cur)

    for pr in range(S5_PAIRS):
        xin = jnp.concatenate([xre[:, pr * LANES:(pr + 1) * LANES],
                               xim[:, pr * LANES:(pr + 1) * LANES]], axis=-1).astype(BF16)
        yc = _dot(xin, q_ref[pr])
        for half in range(2):
            g = 2 * pr + half
            cols = slice(g * gw, (g + 1) * gw)
            y = _dot(w_ref[cur, :, cols], m_ref[g]) + yc[:, half * gw:(half + 1) * gw]
            y_ref[:, cols] = 0.5 * y * (1.0 + lax.erf(y * (2.0 ** -0.5)))

    for b in range(D_B // LANES):
        for a in range(S5_CHUNK // per_vreg):
            srcs = [y_ref[:, (per_vreg * b + i) * gw + a * LANES:(per_vreg * b + i) * gw + (a + 1) * LANES]
                    for i in range(per_vreg)]
            for ri, arr in enumerate(chunk_transpose(srcs)):
                r = per_vreg * a + ri
                o_ref[r * rows:(r + 1) * rows, b * LANES:(b + 1) * LANES] = arr.astype(BF16)


def _s5_core(proj, ops):
    kt, p2, q2, a_lr, a_li = ops
    s = proj.shape[0]
    gw = S5_CHUNK * S5_CH
    wide = S5_G * gw
    nstate = S5_G * S5_P
    return pl.pallas_call(
        _s5_body,
        out_shape=jax.ShapeDtypeStruct((s, D_B), BF16),
        grid=(s // SPAN,),
        in_specs=[pl.BlockSpec((SPAN, D_B), lambda i: (i, 3 * D_A // D_B)),
                  pl.BlockSpec((SPAN, D_B), lambda i: (jnp.minimum(i + 1, s // SPAN - 1), 3 * D_A // D_B)),
                  _full(kt), _full(p2), _full(q2), _full(a_lr), _full(a_li)],
        out_specs=pl.BlockSpec((SPAN, D_B), lambda i: (i, 0)),
        scratch_shapes=[pltpu.VMEM((S5_G, gw, gw), BF16), pltpu.VMEM((2, BLK, wide), BF16),
                        pltpu.VMEM((BLK, wide), F32)]
                       + [pltpu.VMEM((BLK, nstate), F32)] * 4 + [pltpu.VMEM((1, nstate), F32)] * 2,
        compiler_params=_params(("arbitrary",)),
        name="s5",
    )(proj, proj, kt, p2, q2, a_lr, a_li)


def _split_bf16(x):
    hi = x.astype(BF16)
    lo = (x - hi.astype(F32)).astype(BF16)
    return hi, lo


def _mid_body(h_ref, ya_ref, yb_ref, wglu_ref, bglu_ref, ga_ref, gb_ref, wout_ref, gx_ref, wq_ref,
              k_ref, v_ref, wo_ref, gm_ref, wr_ref, br_ref, h_out, xn_out, logit_out):
    part = h_ref.shape[0] // MID_CHAINS
    slices = [slice(c * part, (c + 1) * part) for c in range(MID_CHAINS)]
    chains = [_mid_rows(h_ref[rows, :], ya_ref[rows, :], yb_ref[rows, :], wglu_ref, bglu_ref, ga_ref, gb_ref,
                        wout_ref, gx_ref, wq_ref, k_ref, v_ref, wo_ref, gm_ref, wr_ref, br_ref)
              for rows in slices]
    for outs in zip(*chains):
        pass
    for rows, out in zip(slices, outs):
        for ref, val in zip((h_out, xn_out, logit_out), out):
            ref[rows, :] = val


def _mid_rows(h, ya, yb, wglu_ref, bglu_ref, ga_ref, gb_ref, wout_ref, gx_ref, wq_ref,
              k_ref, v_ref, wo_ref, gm_ref, wr_ref, br_ref):
    gate = jax.nn.sigmoid(_dot(yb, wglu_ref[...]) + bglu_ref[...])
    yb2 = yb.astype(F32) * gate
    na = _rms(ya.astype(F32), ga_ref[...]).astype(BF16)
    nb = _rms(yb2, gb_ref[...]).astype(BF16)
    h1 = h + _dot(na, wout_ref[0:D_A, :]) + _dot(nb, wout_ref[D_A:D_MODEL, :])
    yield None
    q = _dot(_rms(h1, gx_ref[...]).astype(BF16), wq_ref[...]).astype(BF16)
    yield None
    heads = []
    for hd in range(H_X):
        lanes = slice(hd * HD_X, (hd + 1) * HD_X)
        s = _dot_nt(q[:, lanes], k_ref[:, lanes])
        e = jnp.exp(s - jnp.max(s, axis=-1, keepdims=True))
        heads.append(_dot(e.astype(BF16), v_ref[:, lanes]) / jnp.sum(e, axis=-1, keepdims=True))
    o = jnp.concatenate(heads, axis=-1).astype(BF16)
    yield None
    h2 = h1 + _dot(o, wo_ref[...])
    yield None
    xn = _rms(h2, gm_ref[...])
    x_hi, x_lo = _split_bf16(xn)
    part = _dot(x_hi, wr_ref[...]) + _dot(x_lo, wr_ref[...])
    logits = part + pltpu.roll(part, LANES - N_EXPERTS, axis=1) + br_ref[...]
    yield h2, _pack_rows(xn), logits


def _mid(h, ya, yb, wglu, bglu, ga, gb, wout, gx, wq, kmem, vmem, wo, gm, wr2, br):
    s = h.shape[0]
    tm = min(MID_CHAINS * ROW_TILE // 2, s)
    row = lambda w: pl.BlockSpec((tm, w), lambda i: (i, 0))
    consts = [wglu, bglu, ga, gb, wout, gx, wq, kmem, vmem, wo, gm, wr2, br]
    return pl.pallas_call(
        _mid_body,
        out_shape=[jax.ShapeDtypeStruct((s, D_MODEL), F32), jax.ShapeDtypeStruct((s, D_MODEL // 2), jnp.int32),
                   jax.ShapeDtypeStruct((s, LANES), F32)],
        grid=(s // tm,),
        in_specs=[row(D_MODEL), row(D_A), row(D_B)] + [_full(a) for a in consts],
        out_specs=[row(D_MODEL), row(D_MODEL // 2), row(LANES)],
        compiler_params=_params(("parallel",)),
        name="mid",
    )(h, ya, yb, *consts)


def _memkv_body(mem_ref, g_ref, w_ref, o_ref):
    o_ref[...] = _dot(_rms(mem_ref[...], g_ref[...]).astype(BF16), w_ref[...]).astype(BF16)


def _memkv(mem, g, w):
    n, d = mem.shape
    return pl.pallas_call(
        _memkv_body,
        out_shape=jax.ShapeDtypeStruct((n, w.shape[1]), BF16),
        compiler_params=pltpu.CompilerParams(vmem_limit_bytes=VMEM_LIMIT),
        name="memkv",
    )(mem, g.reshape(1, d), w)


def _rank_body(logit_ref, tri_ref, idx_ref, gate_ref, rank_ref, cnt_ref, carry):
    @pl.when(pl.program_id(0) == 0)
    def _():
        carry[...] = jnp.zeros_like(carry)

    tm = logit_ref.shape[0]
    logits = jnp.transpose(logit_ref[...])[:N_EXPERTS, :]
    expert = lax.broadcasted_iota(jnp.int32, (N_EXPERTS, tm), 0)
    vals, idxs = [], []
    for _ in range(TOP_K):
        mx = jnp.max(logits, axis=0, keepdims=True)
        ix = jnp.min(jnp.where(logits == mx, expert, N_EXPERTS), axis=0, keepdims=True)
        vals.append(mx)
        idxs.append(ix)
        logits = jnp.where(expert == ix, -jnp.inf, logits)
    es = [jnp.exp(v - vals[0]) for v in vals]
    den = es[0] + es[1] + es[2] + es[3]
    hits = [expert == ix for ix in idxs]
    onehot = jnp.zeros((N_EXPERTS, tm), F32)
    for hit in hits:
        onehot = onehot + jnp.where(hit, 1.0, 0.0)
    nb = tm // LANES
    blocks = jnp.concatenate([onehot[:, b * LANES:(b + 1) * LANES] for b in range(nb)], axis=0)
    inc = _dot(blocks.astype(BF16), tri_ref[...])
    run = carry[...][:, 0:1]
    before = []
    for b in range(nb):
        inc_b = inc[b * N_EXPERTS:(b + 1) * N_EXPERTS, :]
        before.append(run + inc_b - onehot[:, b * LANES:(b + 1) * LANES])
        run = run + inc_b[:, LANES - 1:LANES]
    before = jnp.concatenate(before, axis=1)
    choice = lax.broadcasted_iota(jnp.int32, (SUBLANES, tm), 0)
    idx_t = jnp.full((SUBLANES, tm), N_EXPERTS, jnp.int32)
    gate_t = jnp.zeros((SUBLANES, tm), F32)
    rank_t = jnp.zeros((SUBLANES, tm), jnp.int32)
    for k in range(TOP_K):
        rk = jnp.sum(jnp.where(hits[k], before, 0.0), axis=0, keepdims=True)
        idx_t = jnp.where(choice == k, idxs[k], idx_t)
        gate_t = jnp.where(choice == k, es[k] / den, gate_t)
        rank_t = jnp.where(choice == k, rk.astype(jnp.int32), rank_t)
    idx_ref[...] = idx_t
    gate_ref[...] = gate_t
    rank_ref[...] = rank_t
    total = jnp.broadcast_to(run, carry.shape)
    carry[...] = total
    cnt_ref[...] = total.astype(jnp.int32)


def _rank(logits):
    t = logits.shape[0]
    tm = min(RANK_TILE, t)
    cols = pl.BlockSpec((SUBLANES, tm), lambda i: (0, i))
    tri = jnp.asarray(np.triu(np.ones((LANES, LANES), np.float32)), BF16)
    return pl.pallas_call(
        _rank_body,
        out_shape=[jax.ShapeDtypeStruct((SUBLANES, t), jnp.int32), jax.ShapeDtypeStruct((SUBLANES, t), F32),
                   jax.ShapeDtypeStruct((SUBLANES, t), jnp.int32),
                   jax.ShapeDtypeStruct((N_EXPERTS, LANES), jnp.int32)],
        grid=(t // tm,),
        in_specs=[pl.BlockSpec((tm, LANES), lambda i: (i, 0)), _full(tri)],
        out_specs=[cols, cols, cols, pl.BlockSpec((N_EXPERTS, LANES), lambda i: (0, 0))],
        scratch_shapes=[pltpu.VMEM((N_EXPERTS, LANES), F32)],
        compiler_params=_params(("arbitrary",)),
        name="rank",
    )(logits, tri)


def _route(logits, tm):
    t = logits.shape[0]
    tk = t * TOP_K
    idx, gates, rank, cnt = _rank(logits)
    counts = cnt[:, 0]
    padded = (counts + tm - 1) // tm * tm
    pend = jnp.cumsum(padded)
    pstart = pend - padded
    n_rows = tk + N_EXPERTS * tm
    n_tiles = n_rows // tm
    experts = jnp.arange(N_EXPERTS, dtype=jnp.int32)
    tile_first = jnp.arange(n_tiles, dtype=jnp.int32) * tm
    last_used = jnp.max(jnp.where(padded > 0, experts, 0))
    tile_e = jnp.minimum(jnp.sum(tile_first[:, None] >= pend[None, :], axis=1), last_used).astype(jnp.int32)
    tile_rows = jnp.clip(jnp.sum(jnp.where(tile_e[:, None] == experts[None, :],
                                           (pstart + counts)[None, :], 0), axis=1) - tile_first, 0, tm)
    tile_rows = jnp.where(tile_first < pend[-1], tile_rows, 0).astype(jnp.int32)
    group = jnp.cumsum(jnp.concatenate([jnp.zeros((1,), jnp.int32),
                                        (tile_e[1:] != tile_e[:-1]).astype(jnp.int32)]))
    tile_slot = (group % 2).astype(jnp.int32)
    later = (experts[None, :] > experts[:, None]) & (padded > 0)[None, :]
    next_e = jnp.min(jnp.where(later, experts[None, :], N_EXPERTS), axis=1)
    next_e = jnp.where(next_e < N_EXPERTS, next_e, -1).astype(jnp.int32)
    tile_next = jnp.sum(jnp.where(tile_e[:, None] == experts[None, :], next_e[None, :], 0), axis=1).astype(jnp.int32)
    base = jnp.sum(jnp.where(idx[:TOP_K, :, None] == experts, pstart, 0), axis=-1)
    dest = rank[:TOP_K] + base
    return gates[:TOP_K].T, dest, n_rows, tile_e, tile_rows, tile_slot, tile_next


def _sc_mesh():
    return plsc.VectorSubcoreMesh(core_axis_name="c", subcore_axis_name="s",
                                  num_cores=SC_CORES, num_subcores=SC_SUBCORES)


def _sc_worker():
    return lax.axis_index("s") * SC_CORES + lax.axis_index("c")


def _scatter_rows(x, dest, n_rows):
    t, d = x.shape
    per_worker = t // (SC_CORES * SC_SUBCORES)
    chunks = per_worker // SC_ROWS

    assert chunks % 2 == 0

    @functools.partial(
        pl.kernel, mesh=_sc_mesh(),
        out_type=jax.ShapeDtypeStruct((n_rows, d), x.dtype),
        scratch_types=[pltpu.VMEM((SC_ROWS, d), x.dtype)] * 2 + [pltpu.VMEM((SC_ROWS,), jnp.int32)] * (2 * TOP_K)
                      + [pltpu.SemaphoreType.DMA((2,)), pltpu.SemaphoreType.DMA((2, TOP_K))],
    )
    def scatter(x_hbm, *rest):
        dest_hbm, out_hbm = rest[:TOP_K], rest[TOP_K]
        rows_v = rest[TOP_K + 1:TOP_K + 3]
        idx_v = (rest[TOP_K + 3:2 * TOP_K + 3], rest[2 * TOP_K + 3:3 * TOP_K + 3])
        lsem, ssem = rest[3 * TOP_K + 3], rest[3 * TOP_K + 4]
        base = _sc_worker() * per_worker

        def rows_at(c):
            return pl.ds(pl.multiple_of(base + c * SC_ROWS, SC_ROWS), SC_ROWS)

        def loaded(c, s):
            return pltpu.make_async_copy(x_hbm.at[rows_at(c)], rows_v[s], lsem.at[s])

        def load(c, s):
            loaded(c, s).start()
            for k in range(TOP_K):
                pltpu.sync_copy(dest_hbm[k].at[rows_at(c)], idx_v[s][k])

        def scattered(s):
            return [pltpu.make_async_copy(rows_v[s], out_hbm.at[idx_v[s][k]], ssem.at[s, k]) for k in range(TOP_K)]

        load(0, 0)

        @pl.loop(0, chunks, step=2)
        def _(c0):
            for s in range(2):
                c = c0 + s
                loaded(c, s).wait()
                for cp in scattered(s):
                    cp.start()

                @pl.when(c >= 1)
                def _():
                    for cp in scattered(1 - s):
                        cp.wait()

                @pl.when(c + 1 < chunks)
                def _():
                    load(c + 1, 1 - s)

        for cp in scattered(1):
            cp.wait()

    return scatter(x, *[dest[k] for k in range(TOP_K)])


def _gather_rows(table, idx):
    n, d = table.shape
    b = idx.shape[0]
    per_worker = b // (SC_CORES * SC_SUBCORES)
    chunks = per_worker // SC_ROWS
    assert chunks % 2 == 0

    @functools.partial(
        pl.kernel, mesh=_sc_mesh(),
        out_type=jax.ShapeDtypeStruct((b, d), table.dtype),
        scratch_types=[pltpu.VMEM((chunks, SC_ROWS), jnp.int32)] + [pltpu.VMEM((SC_ROWS, d), table.dtype)] * 2
                      + [pltpu.SemaphoreType.DMA((2,)), pltpu.SemaphoreType.DMA((2,))],
    )
    def gather(table_hbm, idx_hbm, out_hbm, idx_v, rows0, rows1, gsem, wsem):
        rows_v = (rows0, rows1)
        worker = _sc_worker()
        base = worker * per_worker
        pltpu.sync_copy(idx_hbm.at[pl.ds(pl.multiple_of(worker * chunks, chunks), chunks)], idx_v)

        def rows_at(c):
            return pl.ds(pl.multiple_of(base + c * SC_ROWS, SC_ROWS), SC_ROWS)

        def fetched(c, s):
            return pltpu.make_async_copy(table_hbm.at[idx_v.at[c]], rows_v[s], gsem.at[s])

        def written(c, s):
            return pltpu.make_async_copy(rows_v[s], out_hbm.at[rows_at(c)], wsem.at[s])

        fetched(0, 0).start()

        @pl.loop(0, chunks, step=2)
        def _(c0):
            for s in range(2):
                c = c0 + s
                fetched(c, s).wait()
                written(c, s).start()

                @pl.when(c >= 1)
                def _():
                    written(c - 1, 1 - s).wait()

                @pl.when(c + 1 < chunks)
                def _():
                    fetched(c + 1, 1 - s).start()

        written(chunks - 1, 1).wait()

    return gather(table, idx.reshape(b // SC_ROWS, SC_ROWS))


def _moe_body(te_ref, tv_ref, sl_ref, nx_ref, x_ref, w1_hbm, b1_ref, w2_hbm, b2_ref, o_ref,
              w1f, w2f, w1b, w2b, sem, *, layer):
    i = pl.program_id(0)
    e = te_ref[i]
    slot = sl_ref[i]
    new_expert = (i == 0) | (e != te_ref[jnp.maximum(i - 1, 0)])

    def weight_copies(expert, s):
        rows1 = w1f.shape[1] // MOE_W1_PARTS
        rows2 = w2f.shape[1] // MOE_W2_PARTS
        c1 = [pltpu.make_async_copy(w1_hbm.at[layer, expert, pl.ds(q * rows1, rows1)],
                                    w1f.at[s, pl.ds(q * rows1, rows1)], sem.at[s, q])
              for q in range(MOE_W1_PARTS)]
        c2 = [pltpu.make_async_copy(w2_hbm.at[layer, expert, pl.ds(q * rows2, rows2)],
                                    w2f.at[s, pl.ds(q * rows2, rows2)], sem.at[s, MOE_W1_PARTS + q])
              for q in range(MOE_W2_PARTS)]
        return c1 + c2

    @pl.when(i == 0)
    def _():
        for c in weight_copies(e, slot):
            c.start()

    @pl.when(new_expert)
    def _():
        for c in weight_copies(e, slot):
            c.wait()
        nxt = nx_ref[i]

        @pl.when(nxt >= 0)
        def _():
            for c in weight_copies(nxt, 1 - slot):
                c.start()

        w1b[...] = w1f[slot].astype(BF16)
        w2b[...] = w2f[slot].astype(BF16)

    def expert(rows):
        row = lax.broadcasted_iota(jnp.int32, (rows, x_ref.shape[1]), 0)
        lo, hi = _unpack_rows(jnp.where(row < tv_ref[i], x_ref[0:rows, :], 0))
        x = jnp.concatenate([lo, hi], axis=-1).astype(BF16)
        hb = _dot(x, w1b[...]) + b1_ref[0]
        x_glu = jnp.minimum(hb[:, :D_FF], SWIGLU_LIMIT)
        x_lin = jnp.clip(hb[:, D_FF:], -SWIGLU_LIMIT, SWIGLU_LIMIT)
        act = x_glu * jax.nn.sigmoid(SWIGLU_ALPHA * x_glu) * (x_lin + 1.0)
        o_ref[0:rows, :] = _pack_rows(_dot(act.astype(BF16), w2b[...]) + b2_ref[0])

    step = x_ref.shape[0] // MOE_ROW_PATHS
    for path in range(1, MOE_ROW_PATHS + 1):
        rows = path * step

        @pl.when((tv_ref[i] > rows - step) & (tv_ref[i] <= rows))
        def _(rows=rows):
            expert(rows)
            if rows < x_ref.shape[0]:
                o_ref[rows:, :] = jnp.zeros((x_ref.shape[0] - rows, o_ref.shape[1]), o_ref.dtype)

    @pl.when(tv_ref[i] == 0)
    def _():
        o_ref[...] = jnp.zeros_like(o_ref)


def _moe_experts(xs, tile_e, tile_rows, tile_slot, tile_next, w1, b1, w2, b2, layer):
    n_rows = xs.shape[0]
    tm = MOE_TILE
    nl, ne, d, ff2 = w1.shape
    bias_map = lambda i, te, tv, sl, nx: (layer, te[i], 0, 0)
    grid_spec = pltpu.PrefetchScalarGridSpec(
        num_scalar_prefetch=4,
        grid=(n_rows // tm,),
        in_specs=[pl.BlockSpec((tm, d // 2), lambda i, *_: (i, 0)),
                  pl.BlockSpec(memory_space=pl.ANY),
                  pl.BlockSpec((None, 1, 1, ff2), bias_map),
                  pl.BlockSpec(memory_space=pl.ANY),
                  pl.BlockSpec((None, 1, 1, d), bias_map)],
        out_specs=pl.BlockSpec((tm, d // 2), lambda i, *_: (i, 0)),
        scratch_shapes=[pltpu.VMEM((2, d, ff2), F32), pltpu.VMEM((2, ff2 // 2, d), F32),
                        pltpu.VMEM((d, ff2), BF16), pltpu.VMEM((ff2 // 2, d), BF16),
                        pltpu.SemaphoreType.DMA((2, MOE_W1_PARTS + MOE_W2_PARTS))],
    )
    return pl.pallas_call(
        functools.partial(_moe_body, layer=layer),
        out_shape=jax.ShapeDtypeStruct((n_rows, d // 2), jnp.int32),
        grid_spec=grid_spec,
        compiler_params=_params(("arbitrary",)),
        name="moe",
    )(tile_e, tile_rows, tile_slot, tile_next, xs, w1, b1.reshape(nl, ne, 1, ff2), w2, b2.reshape(nl, ne, 1, d))


def _combined(h_ref, y_refs, gate_ref):
    gates = gate_ref[...]
    lo = jnp.zeros(y_refs[0].shape, F32)
    hi = jnp.zeros(y_refs[0].shape, F32)
    for k, y_ref in enumerate(y_refs):
        yl, yh = _unpack_rows(y_ref[...])
        lo = lo + yl * gates[:, k:k + 1]
        hi = hi + yh * gates[:, k:k + 1]
    return h_ref[...] + jnp.concatenate([lo, hi], axis=-1)


def _combine_final_body(h_ref, y0_ref, y1_ref, y2_ref, y3_ref, gate_ref, g_ref, out_hbm, buf, sem):
    i = pl.program_id(0)
    n = pl.num_programs(0)
    slot = i % 2
    per_step = h_ref.shape[0] // BLK
    steps_per_span = NRES // per_step

    def writes(step, s):
        span = step // steps_per_span
        r0 = (step % steps_per_span) * per_step
        return [pltpu.make_async_copy(buf.at[s, pl.ds(rr * BLK, BLK), :], out_hbm.at[span, :, r0 + rr, :],
                                      sem.at[s, rr]) for rr in range(per_step)]

    @pl.when(i >= 2)
    def _():
        for cp in writes(i - 2, slot):
            cp.wait()

    buf[slot] = _rms(_combined(h_ref, (y0_ref, y1_ref, y2_ref, y3_ref), gate_ref), g_ref[...])
    for cp in writes(i, slot):
        cp.start()

    @pl.when(i == n - 1)
    def _():
        for cp in writes(i, slot):
            cp.wait()

        @pl.when(i >= 1)
        def _():
            for cp in writes(i - 1, 1 - slot):
                cp.wait()


def _combine_final(h, y, gates, g):
    s, d = h.shape
    tm = min(ROW_TILE, s)
    nt = s // tm
    in_specs = ([pl.BlockSpec((tm, d), lambda i: (i, 0))]
                + [pl.BlockSpec((tm, d // 2), lambda i, k=k: (k * nt + i, 0)) for k in range(TOP_K)]
                + [pl.BlockSpec((tm, TOP_K), lambda i: (i, 0))])
    out = pl.pallas_call(
        _combine_final_body,
        out_shape=jax.ShapeDtypeStruct((s // SPAN, BLK, NRES, d), F32),
        grid=(nt,),
        in_specs=in_specs + [pl.BlockSpec((1, d), lambda i: (0, 0))],
        out_specs=pl.BlockSpec(memory_space=pl.ANY),
        scratch_shapes=[pltpu.VMEM((2, tm, d), F32), pltpu.SemaphoreType.DMA((2, tm // BLK))],
        compiler_params=_params(("arbitrary",)),
        name="combine_final",
    )(h, y, y, y, y, gates, g.reshape(1, d))
    return out.reshape(s, d)


def _layer(state, mem, biases, p, l, g_final):
    row = lambda a: a.reshape(1, -1).astype(F32)
    q_scale = jnp.concatenate([jnp.full((D_A,), HD_A ** -0.5 * LOG2E, F32), jnp.ones((D_IN - D_A,), F32)])
    w_in = (p["w_in"][l] * q_scale).astype(BF16)
    if l == 0:
        h, proj = state, _inproj(state, p["norm_mix"][l], w_in)
    else:
        h, proj = _combine_inproj(*state, p["norm_mix"][l], w_in)
    ya = _dilated_attention(proj, biases)
    ops = _s5_operators(p["s5_a_re"][l], p["s5_a_im"][l], p["s5_b_re"][l], p["s5_b_im"][l],
                        p["s5_c_re"][l], p["s5_c_im"][l], p["s5_log_dt"][l], p["s5_d"][l])
    yb = _s5_core(proj, ops)
    kv = _memkv(mem, p["norm_mem"][l], p["w_xkv"][l].astype(BF16))
    wr = p["w_router"][l].astype(F32)
    wr_hi = wr.astype(BF16)
    wr2 = jnp.pad(jnp.concatenate([wr_hi, (wr - wr_hi.astype(F32)).astype(BF16)], axis=1),
                  ((0, 0), (0, LANES - 2 * N_EXPERTS)))
    br = jnp.pad(p["b_router"][l].astype(F32), (0, LANES - N_EXPERTS)).reshape(1, LANES)
    h2, xn, logits = _mid(
        h, ya, yb, p["w_glu"][l].astype(BF16), row(p["b_glu"][l]), row(p["g_out_attn"][l]),
        row(p["g_out_ssm"][l]), p["w_out"][l].astype(BF16), row(p["norm_xattn"][l]),
        (p["w_xq"][l] * (HD_X ** -0.5)).astype(BF16), kv[:, :D_X], kv[:, D_X:],
        p["w_xo"][l].astype(BF16), row(p["norm_moe"][l]), wr2, br)
    gates, dest, n_rows, tile_e, tile_rows, tile_slot, tile_next = _route(logits, MOE_TILE)
    out = _moe_experts(_scatter_rows(xn, dest, n_rows), tile_e, tile_rows, tile_slot, tile_next,
                       p["w1"], p["b1"], p["w2"], p["b2"], l)
    y = _gather_rows(out, dest.reshape(-1))
    return _combine_final(h2, y, gates, g_final) if l == DEPTH - 1 else (h2, y, gates)


def kernel(x, mem, rel_bias, norm_mix, w_in, s5_a_re, s5_a_im, s5_b_re, s5_b_im, s5_c_re, s5_c_im, s5_log_dt, s5_d, w_glu, b_glu, g_out_attn, g_out_ssm, w_out, norm_xattn, norm_mem, w_xq, w_xkv, w_xo, norm_moe, w_router, b_router, w1, b1, w2, b2, norm_final):
    p = dict(norm_mix=norm_mix, w_in=w_in, s5_a_re=s5_a_re, s5_a_im=s5_a_im, s5_b_re=s5_b_re,
             s5_b_im=s5_b_im, s5_c_re=s5_c_re, s5_c_im=s5_c_im, s5_log_dt=s5_log_dt, s5_d=s5_d,
             w_glu=w_glu, b_glu=b_glu, g_out_attn=g_out_attn, g_out_ssm=g_out_ssm, w_out=w_out,
             norm_xattn=norm_xattn, norm_mem=norm_mem, w_xq=w_xq, w_xkv=w_xkv, w_xo=w_xo,
             norm_moe=norm_moe, w_router=w_router, b_router=b_router, w1=w1, b1=b1, w2=w2, b2=b2)
    biases = [_attn_bias(rel_bias, window, dil, perm)
              for (window, dil), perm in zip(WIN_DIL, (_PERM_D1, _PERM_D4, _PERM_D16))]
    outs = []
    for b in range(x.shape[0]):
        h = _to_span_layout(x[b])
        for l in range(DEPTH):
            h = _layer(h, mem[b], biases, p, l, norm_final)
        outs.append(h)
    return jnp.stack(outs)
```

```python
import functools
import math

import jax
import jax.numpy as jnp
import numpy as np
from jax import lax
from jax.experimental import pallas as pl
from jax.experimental.pallas import tpu as pltpu
from jax.experimental.pallas import tpu_sc as plsc

F32 = jnp.float32
BF16 = jnp.bfloat16

D_MODEL = 1024
DEPTH = 2
EPS = 1e-5
NEG_INF = -1e30
LOG2E = math.log2(math.e)
H_A = 8
HD_A = 64
D_A = H_A * HD_A
WIN_DIL = ((128, 1), (512, 4), (2048, 16))
BLK = 128
D_B = D_MODEL - D_A
S5_CH = 16
S5_G = D_B // S5_CH
S5_P = 64
D_IN = 3 * D_A + D_B
NUM_BUCKETS = 32
REL_MAX_DIST = 2048
H_X = 4
HD_X = 128
D_X = H_X * HD_X
N_EXPERTS = 32
TOP_K = 4
D_FF = D_MODEL
SWIGLU_ALPHA = 1.702
SWIGLU_LIMIT = 7.0

LANES = 128
SUBLANES = 8
NRES = WIN_DIL[-1][1]
SPAN = NRES * BLK
S5_CHUNK = NRES
S5_PAIRS = S5_G // 2
VMEM_LIMIT = 56 * 1024 * 1024

SC_CORES = 2
SC_SUBCORES = 16
SC_ROWS = 64

S5_STAGGER = 4
ROW_TILE = 512
ROW_CHAINS = 2
MID_CHAINS = 4
RANK_TILE = 1024
MOE_TILE = 512
MOE_ROW_PATHS = 4
MOE_W1_PARTS = 4
MOE_W2_PARTS = 2


def _params(sem):
    return pltpu.CompilerParams(dimension_semantics=sem, vmem_limit_bytes=VMEM_LIMIT)


def _rms(x, g):
    return x * lax.rsqrt(jnp.mean(x * x, axis=-1, keepdims=True) + EPS) * g


def _dot(a, b):
    return jnp.dot(a, b, preferred_element_type=F32)


def _dot_nt(a, b):
    return lax.dot_general(a, b, (((1,), (1,)), ((), ())), preferred_element_type=F32)


def _full(a):
    return pl.BlockSpec(a.shape, lambda *_: (0,) * a.ndim)


def _pack_rows(x):
    c = x.shape[1] // 2
    lo = lax.bitcast_convert_type(x[:, :c].astype(BF16).astype(F32), jnp.uint32)
    hi = lax.bitcast_convert_type(x[:, c:].astype(BF16).astype(F32), jnp.uint32)
    return lax.bitcast_convert_type(lax.shift_right_logical(lo, jnp.uint32(16)) | hi, jnp.int32)


def _unpack_rows(p):
    u = lax.bitcast_convert_type(p, jnp.uint32)
    lo = lax.bitcast_convert_type(lax.shift_left(u, jnp.uint32(16)), F32)
    hi = lax.bitcast_convert_type(u & jnp.uint32(0xFFFF0000), F32)
    return lo, hi


def _to_span_layout(x):
    s = x.shape[0]
    return x.reshape(s // SPAN, BLK, NRES, -1).transpose(0, 2, 1, 3).reshape(s, -1)


def _from_span_layout(x):
    s = x.shape[0]
    return x.reshape(s // SPAN, NRES, BLK, -1).transpose(0, 2, 1, 3).reshape(s, -1)


def _row_chains(n_rows):
    part = n_rows // ROW_CHAINS
    return [slice(c * part, (c + 1) * part) for c in range(ROW_CHAINS)]


def _inproj_body(h_ref, g_ref, w_ref, o_ref):
    for rows in _row_chains(h_ref.shape[0]):
        xn = _rms(h_ref[rows, :], g_ref[...]).astype(BF16)
        o_ref[rows, :] = _dot(xn, w_ref[...]).astype(BF16)


def _inproj(h, g, w):
    s, d = h.shape
    n = w.shape[1]
    tm = min(ROW_TILE, s)
    return pl.pallas_call(
        _inproj_body,
        out_shape=jax.ShapeDtypeStruct((s, n), BF16),
        grid=(s // tm,),
        in_specs=[pl.BlockSpec((tm, d), lambda i: (i, 0)),
                  pl.BlockSpec((1, d), lambda i: (0, 0)),
                  pl.BlockSpec((d, n), lambda i: (0, 0))],
        out_specs=pl.BlockSpec((tm, n), lambda i: (i, 0)),
        compiler_params=_params(("parallel",)),
        name="inproj",
    )(h, g.reshape(1, d), w)


def _combine_inproj_body(h_ref, y0_ref, y1_ref, y2_ref, y3_ref, gate_ref, g_ref, w_ref, h_out, o_ref):
    chains = _row_chains(h_ref.shape[0])
    hs = [_combined(h_ref.at[rows, :], [y.at[rows, :] for y in (y0_ref, y1_ref, y2_ref, y3_ref)],
                    gate_ref.at[rows, :]) for rows in chains]
    for rows, h in zip(chains, hs):
        h_out[rows, :] = h
    for rows, h in zip(chains, hs):
        o_ref[rows, :] = _dot(_rms(h, g_ref[...]).astype(BF16), w_ref[...]).astype(BF16)


def _combine_inproj(h, y, gates, g, w):
    s, d = h.shape
    n = w.shape[1]
    tm = min(ROW_TILE, s)
    nt = s // tm
    return pl.pallas_call(
        _combine_inproj_body,
        out_shape=[jax.ShapeDtypeStruct((s, d), F32), jax.ShapeDtypeStruct((s, n), BF16)],
        grid=(nt,),
        in_specs=[pl.BlockSpec((tm, d), lambda i: (i, 0))]
                 + [pl.BlockSpec((tm, d // 2), lambda i, k=k: (k * nt + i, 0)) for k in range(TOP_K)]
                 + [pl.BlockSpec((tm, TOP_K), lambda i: (i, 0)), pl.BlockSpec((1, d), lambda i: (0, 0)),
                    pl.BlockSpec((d, n), lambda i: (0, 0))],
        out_specs=[pl.BlockSpec((tm, d), lambda i: (i, 0)), pl.BlockSpec((tm, n), lambda i: (i, 0))],
        compiler_params=_params(("parallel",)),
        name="combine_inproj",
    )(h, y, y, y, y, gates, g.reshape(1, d), w)


def _t5_bucket(n):
    max_exact = NUM_BUCKETS // 2
    nf = jnp.maximum(n, 1).astype(F32)
    large = max_exact + (jnp.log(nf / max_exact) / math.log(REL_MAX_DIST / max_exact)
                         * (NUM_BUCKETS - max_exact)).astype(jnp.int32)
    large = jnp.minimum(large, NUM_BUCKETS - 1)
    return jnp.where(n < max_exact, n, large)


def _attn_bias(rel_bias, window, dil, perm):
    steps = window // dil
    perm = jnp.asarray(perm, jnp.int32)
    qi = perm[:, None]
    ki = jnp.concatenate([perm, BLK + perm])[None, :]
    dist = BLK + qi - ki
    in_win = (dist >= 0) & (dist <= steps)
    bucket = _t5_bucket(jnp.clip(dist, 0, steps) * dil)
    onehot = (bucket[:, :, None] == jnp.arange(NUM_BUCKETS, dtype=jnp.int32)).astype(F32)
    bias = jnp.einsum('qkb,bh->hqk', onehot, rel_bias.astype(F32), precision=lax.Precision.HIGHEST)
    bias = jnp.where(in_win[None], bias * LOG2E, NEG_INF)
    return bias.reshape(H_A // 2, 2 * BLK, 2 * BLK)


_PERM_D1 = [NRES * jl + r for r in range(NRES) for jl in range(BLK // NRES)]
_PERM_D4 = [4 * jl + i for i in range(4) for jl in range(BLK // 4)]
_PERM_D16 = list(range(BLK))


def _attn_body(q_ref, k_ref, v_ref, kp_ref, vp_ref, b1_ref, b4_ref, b16_ref, o_ref, acc, mst, lst):
    has_prev = pl.program_id(0) > 0
    lane = lax.broadcasted_iota(jnp.int32, (1, LANES), 1)
    lo = lane < HD_A
    mlo = lo.astype(BF16)
    mhi = (~lo).astype(BF16)
    col = lax.broadcasted_iota(jnp.int32, (2 * BLK, 2 * BLK), 1)
    ones = jnp.ones((2 * BLK, LANES), BF16)

    def tile(q2, kk, vv, bias, mask_prev):
        qs = jnp.concatenate([q2 * mlo, q2 * mhi], axis=0)
        s = _dot_nt(qs, kk) + bias
        if mask_prev:
            s = jnp.where(jnp.logical_or(has_prev, col >= BLK), s, NEG_INF)
        m = jnp.max(s, axis=-1, keepdims=True)
        e = jnp.exp2((s - m).astype(BF16))
        oa = _dot(e, jnp.concatenate([vv, ones], axis=1))
        o = oa[:, :LANES]
        l = oa[:, LANES:]
        return (jnp.where(lo, m[:BLK], m[BLK:]), jnp.where(lo, l[:BLK], l[BLK:]),
                jnp.where(lo, o[:BLK], o[BLK:]))

    def merge(prev, cur):
        mp, lp, ap = prev
        mc, lc, ac = cur
        mn = jnp.maximum(mp, mc)
        a = jnp.exp2(mp - mn)
        b = jnp.exp2(mc - mn)
        return mn, a * lp + b * lc, a * ap + b * ac

    def cat(xs):
        return jnp.concatenate(xs, axis=0)


    for r in range(NRES):
        rows = pl.ds(r * BLK, BLK)
        for hp in range(H_A // 2):
            lanes = slice(hp * LANES, (hp + 1) * LANES)
            kk = cat([kp_ref[rows, lanes], k_ref[rows, lanes]])
            vv = cat([vp_ref[rows, lanes], v_ref[rows, lanes]])
            m2, l2, o2 = tile(q_ref[rows, lanes], kk, vv, b16_ref[hp], True)
            mst[rows, lanes] = m2
            lst[rows, lanes] = l2
            acc[rows, lanes] = o2

    for r4 in range(4):
        for b in range(4):
            def chunk_rows(bb):
                return [pl.ds(4 * BLK * i + BLK * r4 + 32 * bb, 32) for i in range(4)]
            rows = chunk_rows(b)
            prows = chunk_rows(3 if b == 0 else b - 1)
            kprev, vprev = (kp_ref, vp_ref) if b == 0 else (k_ref, v_ref)
            for hp in range(H_A // 2):
                lanes = slice(hp * LANES, (hp + 1) * LANES)
                q2 = cat([q_ref[rr, lanes] for rr in rows])
                kk = cat([kprev[rr, lanes] for rr in prows] + [k_ref[rr, lanes] for rr in rows])
                vv = cat([vprev[rr, lanes] for rr in prows] + [v_ref[rr, lanes] for rr in rows])
                cur = tile(q2, kk, vv, b4_ref[hp], b == 0)
                prev = (cat([mst[rr, lanes] for rr in rows]), cat([lst[rr, lanes] for rr in rows]),
                        cat([acc[rr, lanes] for rr in rows]))
                mn, ln, an = merge(prev, cur)
                for i, rr in enumerate(rows):
                    part = slice(32 * i, 32 * (i + 1))
                    mst[rr, lanes] = mn[part]
                    lst[rr, lanes] = ln[part]
                    acc[rr, lanes] = an[part]

    def d1_pair(ap, kprev, vprev, prev_ap, mask_prev):
        def tiles(a_):
            return [pl.ds(BLK * r + 16 * a_, 16) for r in range(NRES)]
        cur_t = tiles(ap)
        prev_t = tiles(prev_ap)

        def halves(ref, ts, lanes):
            xs = [ref[t, lanes].astype(F32) for t in ts]
            return cat([x[:8] for x in xs]).astype(BF16), cat([x[8:] for x in xs]).astype(BF16)

        for hp in range(H_A // 2):
            lanes = slice(hp * LANES, (hp + 1) * LANES)
            q_e, q_o = halves(q_ref, cur_t, lanes)
            k_e, k_o = halves(k_ref, cur_t, lanes)
            v_e, v_o = halves(v_ref, cur_t, lanes)
            _, k_p = halves(kprev, prev_t, lanes)
            _, v_p = halves(vprev, prev_t, lanes)
            cur_e = tile(q_e, cat([k_p, k_e]), cat([v_p, v_e]), b1_ref[hp], mask_prev)
            cur_o = tile(q_o, cat([k_e, k_o]), cat([v_e, v_o]), b1_ref[hp], False)
            ms = [mst[t, lanes] for t in cur_t]
            ls = [lst[t, lanes] for t in cur_t]
            ac = [acc[t, lanes] for t in cur_t]
            outs = []
            for half, cur in ((0, cur_e), (1, cur_o)):
                part = slice(8 * half, 8 * half + 8)
                prev = (cat([x[part] for x in ms]), cat([x[part] for x in ls]), cat([x[part] for x in ac]))
                _, ln, an = merge(prev, cur)
                outs.append(an / ln)
            for r, t in enumerate(cur_t):
                part = slice(8 * r, 8 * r + 8)
                o_ref[t, lanes] = cat([outs[0][part], outs[1][part]]).astype(o_ref.dtype)

    d1_pair(0, kp_ref, vp_ref, BLK // 16 - 1, True)
    for ap in range(1, BLK // 16):
        d1_pair(ap, k_ref, v_ref, ap - 1, False)


def _dilated_attention(proj, biases):
    s = proj.shape[0]
    cur = lambda which: pl.BlockSpec((SPAN, D_A), lambda c: (c, which))
    prev = lambda which: pl.BlockSpec((SPAN, D_A), lambda c: (jnp.maximum(c - 1, 0), which))
    return pl.pallas_call(
        _attn_body,
        out_shape=jax.ShapeDtypeStruct((s, D_A), BF16),
        grid=(s // SPAN,),
        in_specs=[cur(0), cur(1), cur(2), prev(1), prev(2)] + [_full(b) for b in biases],
        out_specs=pl.BlockSpec((SPAN, D_A), lambda c: (c, 0)),
        scratch_shapes=[pltpu.VMEM((SPAN, D_A), F32)] * 3,
        compiler_params=_params(("arbitrary",)),
        name="attn",
    )(proj, proj, proj, proj, proj, *biases)


def _s5_operators(a_re, a_im, b_re, b_im, c_re, c_im, log_dt, d_skip):
    L = S5_CHUNK
    lam = lax.complex(a_re.astype(F32), a_im.astype(F32))
    dt = jnp.exp(log_dt.astype(F32))[:, None]
    a_bar = jnp.exp(lam * dt)
    b_bar = ((a_bar - 1.0) / lam)[..., None] * lax.complex(b_re.astype(F32), b_im.astype(F32))
    c = lax.complex(c_re.astype(F32), c_im.astype(F32))
    j = jnp.arange(L + 1, dtype=F32)
    log_a = lam * dt
    apow = jnp.exp(log_a[None] * j[:, None, None])
    kt = jnp.einsum('gdp,jgp,gpc->gcjd', c, apow[:L], b_bar).real
    skip = d_skip.astype(F32).reshape(S5_G, S5_CH, 1, 1) * jnp.eye(S5_CH)[None, :, None, :]
    kt = (kt + skip * (jnp.arange(L) == 0)[None, None, :, None]).reshape(S5_G, S5_CH, L * S5_CH)
    p = jnp.einsum('sgp,gpc->gscp', apow[:L][::-1], b_bar).reshape(S5_G, L * S5_CH, S5_P)
    ca = jnp.einsum('gdp,tgp->gptd', c, apow[1:L + 1]).reshape(S5_G, S5_P, L * S5_CH)
    a_l = apow[L]

    def pair_blocks(x):
        g, r, w = x.shape
        x = x.reshape(S5_PAIRS, 2, r, w)
        z = jnp.zeros_like(x[:, 0])
        top = jnp.concatenate([x[:, 0], z], axis=-1)
        bot = jnp.concatenate([z, x[:, 1]], axis=-1)
        return jnp.concatenate([top, bot], axis=1)

    p2 = jnp.concatenate([pair_blocks(p.real), pair_blocks(p.imag)], axis=-1)
    q2 = jnp.concatenate([pair_blocks(ca.real), pair_blocks(-ca.imag)], axis=1)
    a_lr = a_l.real.reshape(1, S5_G * S5_P)
    a_li = a_l.imag.reshape(1, S5_G * S5_P)
    return kt, p2.astype(BF16), q2.astype(BF16), a_lr, a_li


def _s5_body(u_ref, kt_ref, p_ref, q_ref, ar_ref, ai_ref, o_ref, m_ref, w_ref, y_ref,
             ere, eim, xre, xim, sre, sim):
    rows = BLK
    gw = S5_CHUNK * S5_CH
    pw = 2 * gw
    per_vreg = LANES // S5_CH
    chunk_of_lane = lax.broadcasted_iota(jnp.int32, (rows, LANES), 1) // S5_CH

    def chunk_transpose(arrs):
        arrs = list(arrs)
        for s in (4, 2, 1):
            upper = (chunk_of_lane & s) != 0
            nxt = list(arrs)
            for i in range(per_vreg):
                if i & s:
                    continue
                lo_a, hi_a = arrs[i], arrs[i + s]
                nxt[i] = jnp.where(upper, pltpu.roll(hi_a, s * S5_CH, axis=1), lo_a)
                nxt[i + s] = jnp.where(upper, hi_a, pltpu.roll(lo_a, LANES - s * S5_CH, axis=1))
            arrs = nxt
        return arrs

    @pl.when(pl.program_id(0) == 0)
    def _():
        sre[...] = jnp.zeros_like(sre)
        sim[...] = jnp.zeros_like(sim)
        lane = lax.broadcasted_iota(jnp.int32, (S5_CH, gw), 1)

        def build(g, _):
            kt = kt_ref[g]
            for s in range(S5_CHUNK):
                blk = kt if s == 0 else jnp.where(lane >= s * S5_CH, pltpu.roll(kt, s * S5_CH, axis=1), 0.0)
                m_ref[g, s * S5_CH:(s + 1) * S5_CH, :] = blk.astype(BF16)
            return 0

        lax.fori_loop(0, S5_G, build, 0)

    pairs_per_block = per_vreg // 2
    state_w = pairs_per_block * LANES

    def lane_block(b):
        st = slice(b * state_w, (b + 1) * state_w)
        for a in range(S5_CHUNK // per_vreg):
            srcs = [u_ref[(per_vreg * a + i) * rows:(per_vreg * a + i + 1) * rows,
                          b * LANES:(b + 1) * LANES].astype(F32) for i in range(per_vreg)]
            for gi, arr in enumerate(chunk_transpose(srcs)):
                g = per_vreg * b + gi
                w_ref[:, g * gw + a * LANES:g * gw + (a + 1) * LANES] = arr.astype(BF16)
        yield
        for pr in range(pairs_per_block * b, pairs_per_block * (b + 1)):
            e = _dot(w_ref[:, pr * pw:(pr + 1) * pw], p_ref[pr])
            ere[:, pr * LANES:(pr + 1) * LANES] = e[:, :LANES]
            eim[:, pr * LANES:(pr + 1) * LANES] = e[:, LANES:]
        yield
        ar = ar_ref[:, st]
        ai = ai_ref[:, st]
        xr = sre[:, st]
        xi = sim[:, st]
        for n in range(rows):
            xre[n:n + 1, st] = xr
            xim[n:n + 1, st] = xi
            xr, xi = (ar * xr - ai * xi + ere[n:n + 1, st], ar * xi + ai * xr + eim[n:n + 1, st])
            if n % S5_CHUNK == S5_CHUNK - 1:
                yield
        sre[:, st] = xr
        sim[:, st] = xi
        for pr in range(pairs_per_block * b, pairs_per_block * (b + 1)):
            xin = jnp.concatenate([xre[:, pr * LANES:(pr + 1) * LANES],
                                   xim[:, pr * LANES:(pr + 1) * LANES]], axis=-1).astype(BF16)
            yc = _dot(xin, q_ref[pr])
            for half in range(2):
                g = 2 * pr + half
                cols = slice(g * gw, (g + 1) * gw)
                y = _dot(w_ref[:, cols], m_ref[g]) + yc[:, half * gw:(half + 1) * gw]
                y_ref[:, cols] = 0.5 * y * (1.0 + lax.erf(y * (2.0 ** -0.5)))
            yield
        for a in range(S5_CHUNK // per_vreg):
            srcs = [y_ref[:, (per_vreg * b + i) * gw + a * LANES:(per_vreg * b + i) * gw + (a + 1) * LANES]
                    for i in range(per_vreg)]
            for ri, arr in enumerate(chunk_transpose(srcs)):
                r = per_vreg * a + ri
                o_ref[r * rows:(r + 1) * rows, b * LANES:(b + 1) * LANES] = arr.astype(BF16)
        yield

    blocks = [lane_block(b) for b in range(D_B // LANES)]
    live = [True] * len(blocks)
    rnd = 0
    while any(live):
        for b, blk in enumerate(blocks):
            if live[b] and rnd >= S5_STAGGER * b:
                live[b] = next(blk, "done") != "done"
        rnd += 1


def _s5_core(proj, ops):
    kt, p2, q2, a_lr, a_li = ops
    s = proj.shape[0]
    gw = S5_CHUNK * S5_CH
    wide = S5_G * gw
    nstate = S5_G * S5_P
    return pl.pallas_call(
        _s5_body,
        out_shape=jax.ShapeDtypeStruct((s, D_B), BF16),
        grid=(s // SPAN,),
        in_specs=[pl.BlockSpec((SPAN, D_B), lambda i: (i, 3 * D_A // D_B)),
                  _full(kt), _full(p2), _full(q2), _full(a_lr), _full(a_li)],
        out_specs=pl.BlockSpec((SPAN, D_B), lambda i: (i, 0)),
        scratch_shapes=[pltpu.VMEM((S5_G, gw, gw), BF16), pltpu.VMEM((BLK, wide), BF16),
                        pltpu.VMEM((BLK, wide), F32)]
                       + [pltpu.VMEM((BLK, nstate), F32)] * 4 + [pltpu.VMEM((1, nstate), F32)] * 2,
        compiler_params=_params(("arbitrary",)),
        name="s5",
    )(proj, kt, p2, q2, a_lr, a_li)


def _split_bf16(x):
    hi = x.astype(BF16)
    lo = (x - hi.astype(F32)).astype(BF16)
    return hi, lo


def _mid_body(h_ref, ya_ref, yb_ref, wglu_ref, bglu_ref, ga_ref, gb_ref, wout_ref, gx_ref, wq_ref,
              k_ref, v_ref, wo_ref, gm_ref, wr_ref, br_ref, h_out, xn_out, logit_out):
    part = h_ref.shape[0] // MID_CHAINS
    slices = [slice(c * part, (c + 1) * part) for c in range(MID_CHAINS)]
    chains = [_mid_rows(h_ref[rows, :], ya_ref[rows, :], yb_ref[rows, :], wglu_ref, bglu_ref, ga_ref, gb_ref,
                        wout_ref, gx_ref, wq_ref, k_ref, v_ref, wo_ref, gm_ref, wr_ref, br_ref)
              for rows in slices]
    for outs in zip(*chains):
        pass
    for rows, out in zip(slices, outs):
        for ref, val in zip((h_out, xn_out, logit_out), out):
            ref[rows, :] = val


def _mid_rows(h, ya, yb, wglu_ref, bglu_ref, ga_ref, gb_ref, wout_ref, gx_ref, wq_ref,
              k_ref, v_ref, wo_ref, gm_ref, wr_ref, br_ref):
    gate = jax.nn.sigmoid(_dot(yb, wglu_ref[...]) + bglu_ref[...])
    yb2 = yb.astype(F32) * gate
    na = _rms(ya.astype(F32), ga_ref[...]).astype(BF16)
    nb = _rms(yb2, gb_ref[...]).astype(BF16)
    h1 = h + _dot(na, wout_ref[0:D_A, :]) + _dot(nb, wout_ref[D_A:D_MODEL, :])
    yield None
    q = _dot(_rms(h1, gx_ref[...]).astype(BF16), wq_ref[...]).astype(BF16)
    yield None
    heads = []
    for hd in range(H_X):
        lanes = slice(hd * HD_X, (hd + 1) * HD_X)
        s = _dot_nt(q[:, lanes], k_ref[:, lanes])
        e = jnp.exp(s - jnp.max(s, axis=-1, keepdims=True))
        heads.append(_dot(e.astype(BF16), v_ref[:, lanes]) / jnp.sum(e, axis=-1, keepdims=True))
    o = jnp.concatenate(heads, axis=-1).astype(BF16)
    yield None
    h2 = h1 + _dot(o, wo_ref[...])
    yield None
    xn = _rms(h2, gm_ref[...])
    x_hi, x_lo = _split_bf16(xn)
    part = _dot(x_hi, wr_ref[...]) + _dot(x_lo, wr_ref[...])
    logits = part + pltpu.roll(part, LANES - N_EXPERTS, axis=1) + br_ref[...]
    yield h2, _pack_rows(xn), logits


def _mid(h, ya, yb, wglu, bglu, ga, gb, wout, gx, wq, kmem, vmem, wo, gm, wr2, br):
    s = h.shape[0]
    tm = min(MID_CHAINS * ROW_TILE // 2, s)
    row = lambda w: pl.BlockSpec((tm, w), lambda i: (i, 0))
    consts = [wglu, bglu, ga, gb, wout, gx, wq, kmem, vmem, wo, gm, wr2, br]
    return pl.pallas_call(
        _mid_body,
        out_shape=[jax.ShapeDtypeStruct((s, D_MODEL), F32), jax.ShapeDtypeStruct((s, D_MODEL // 2), jnp.int32),
                   jax.ShapeDtypeStruct((s, LANES), F32)],
        grid=(s // tm,),
        in_specs=[row(D_MODEL), row(D_A), row(D_B)] + [_full(a) for a in consts],
        out_specs=[row(D_MODEL), row(D_MODEL // 2), row(LANES)],
        compiler_params=_params(("parallel",)),
        name="mid",
    )(h, ya, yb, *consts)


def _memkv_body(mem_ref, g_ref, w_ref, o_ref):
    o_ref[...] = _dot(_rms(mem_ref[...], g_ref[...]).astype(BF16), w_ref[...]).astype(BF16)


def _memkv(mem, g, w):
    n, d = mem.shape
    return pl.pallas_call(
        _memkv_body,
        out_shape=jax.ShapeDtypeStruct((n, w.shape[1]), BF16),
        compiler_params=pltpu.CompilerParams(vmem_limit_bytes=VMEM_LIMIT),
        name="memkv",
    )(mem, g.reshape(1, d), w)


def _rank_body(logit_ref, tri_ref, idx_ref, gate_ref, rank_ref, cnt_ref, carry):
    @pl.when(pl.program_id(0) == 0)
    def _():
        carry[...] = jnp.zeros_like(carry)

    tm = logit_ref.shape[0]
    logits = jnp.transpose(logit_ref[...])[:N_EXPERTS, :]
    expert = lax.broadcasted_iota(jnp.int32, (N_EXPERTS, tm), 0)
    vals, idxs = [], []
    for _ in range(TOP_K):
        mx = jnp.max(logits, axis=0, keepdims=True)
        ix = jnp.min(jnp.where(logits == mx, expert, N_EXPERTS), axis=0, keepdims=True)
        vals.append(mx)
        idxs.append(ix)
        logits = jnp.where(expert == ix, -jnp.inf, logits)
    es = [jnp.exp(v - vals[0]) for v in vals]
    den = es[0] + es[1] + es[2] + es[3]
    hits = [expert == ix for ix in idxs]
    onehot = jnp.zeros((N_EXPERTS, tm), F32)
    for hit in hits:
        onehot = onehot + jnp.where(hit, 1.0, 0.0)
    nb = tm // LANES
    blocks = jnp.concatenate([onehot[:, b * LANES:(b + 1) * LANES] for b in range(nb)], axis=0)
    inc = _dot(blocks.astype(BF16), tri_ref[...])
    run = carry[...][:, 0:1]
    before = []
    for b in range(nb):
        inc_b = inc[b * N_EXPERTS:(b + 1) * N_EXPERTS, :]
        before.append(run + inc_b - onehot[:, b * LANES:(b + 1) * LANES])
        run = run + inc_b[:, LANES - 1:LANES]
    before = jnp.concatenate(before, axis=1)
    choice = lax.broadcasted_iota(jnp.int32, (SUBLANES, tm), 0)
    idx_t = jnp.full((SUBLANES, tm), N_EXPERTS, jnp.int32)
    gate_t = jnp.zeros((SUBLANES, tm), F32)
    rank_t = jnp.zeros((SUBLANES, tm), jnp.int32)
    for k in range(TOP_K):
        rk = jnp.sum(jnp.where(hits[k], before, 0.0), axis=0, keepdims=True)
        idx_t = jnp.where(choice == k, idxs[k], idx_t)
        gate_t = jnp.where(choice == k, es[k] / den, gate_t)
        rank_t = jnp.where(choice == k, rk.astype(jnp.int32), rank_t)
    idx_ref[...] = idx_t
    gate_ref[...] = gate_t
    rank_ref[...] = rank_t
    total = jnp.broadcast_to(run, carry.shape)
    carry[...] = total
    cnt_ref[...] = total.astype(jnp.int32)


def _rank(logits):
    t = logits.shape[0]
    tm = min(RANK_TILE, t)
    cols = pl.BlockSpec((SUBLANES, tm), lambda i: (0, i))
    tri = jnp.asarray(np.triu(np.ones((LANES, LANES), np.float32)), BF16)
    return pl.pallas_call(
        _rank_body,
        out_shape=[jax.ShapeDtypeStruct((SUBLANES, t), jnp.int32), jax.ShapeDtypeStruct((SUBLANES, t), F32),
                   jax.ShapeDtypeStruct((SUBLANES, t), jnp.int32),
                   jax.ShapeDtypeStruct((N_EXPERTS, LANES), jnp.int32)],
        grid=(t // tm,),
        in_specs=[pl.BlockSpec((tm, LANES), lambda i: (i, 0)), _full(tri)],
        out_specs=[cols, cols, cols, pl.BlockSpec((N_EXPERTS, LANES), lambda i: (0, 0))],
        scratch_shapes=[pltpu.VMEM((N_EXPERTS, LANES), F32)],
        compiler_params=_params(("arbitrary",)),
        name="rank",
    )(logits, tri)


def _route(logits, tm):
    t = logits.shape[0]
    tk = t * TOP_K
    idx, gates, rank, cnt = _rank(logits)
    counts = cnt[:, 0]
    padded = (counts + tm - 1) // tm * tm
    pend = jnp.cumsum(padded)
    pstart = pend - padded
    n_rows = tk + N_EXPERTS * tm
    n_tiles = n_rows // tm
    experts = jnp.arange(N_EXPERTS, dtype=jnp.int32)
    tile_first = jnp.arange(n_tiles, dtype=jnp.int32) * tm
    last_used = jnp.max(jnp.where(padded > 0, experts, 0))
    tile_e = jnp.minimum(jnp.sum(tile_first[:, None] >= pend[None, :], axis=1), last_used).astype(jnp.int32)
    tile_rows = jnp.clip(jnp.sum(jnp.where(tile_e[:, None] == experts[None, :],
                                           (pstart + counts)[None, :], 0), axis=1) - tile_first, 0, tm)
    tile_rows = jnp.where(tile_first < pend[-1], tile_rows, 0).astype(jnp.int32)
    group = jnp.cumsum(jnp.concatenate([jnp.zeros((1,), jnp.int32),
                                        (tile_e[1:] != tile_e[:-1]).astype(jnp.int32)]))
    tile_slot = (group % 2).astype(jnp.int32)
    later = (experts[None, :] > experts[:, None]) & (padded > 0)[None, :]
    next_e = jnp.min(jnp.where(later, experts[None, :], N_EXPERTS), axis=1)
    next_e = jnp.where(next_e < N_EXPERTS, next_e, -1).astype(jnp.int32)
    tile_next = jnp.sum(jnp.where(tile_e[:, None] == experts[None, :], next_e[None, :], 0), axis=1).astype(jnp.int32)
    base = jnp.sum(jnp.where(idx[:TOP_K, :, None] == experts, pstart, 0), axis=-1)
    dest = rank[:TOP_K] + base
    return gates[:TOP_K].T, dest, n_rows, tile_e, tile_rows, tile_slot, tile_next


def _sc_mesh():
    return plsc.VectorSubcoreMesh(core_axis_name="c", subcore_axis_name="s",
                                  num_cores=SC_CORES, num_subcores=SC_SUBCORES)


def _sc_worker():
    return lax.axis_index("s") * SC_CORES + lax.axis_index("c")


def _scatter_rows(x, dest, n_rows):
    t, d = x.shape
    per_worker = t // (SC_CORES * SC_SUBCORES)
    chunks = per_worker // SC_ROWS

    assert chunks % 2 == 0

    @functools.partial(
        pl.kernel, mesh=_sc_mesh(),
        out_type=jax.ShapeDtypeStruct((n_rows, d), x.dtype),
        scratch_types=[pltpu.VMEM((SC_ROWS, d), x.dtype)] * 2 + [pltpu.VMEM((SC_ROWS,), jnp.int32)] * (2 * TOP_K)
                      + [pltpu.SemaphoreType.DMA((2,)), pltpu.SemaphoreType.DMA((2, TOP_K))],
    )
    def scatter(x_hbm, *rest):
        dest_hbm, out_hbm = rest[:TOP_K], rest[TOP_K]
        rows_v = rest[TOP_K + 1:TOP_K + 3]
        idx_v = (rest[TOP_K + 3:2 * TOP_K + 3], rest[2 * TOP_K + 3:3 * TOP_K + 3])
        lsem, ssem = rest[3 * TOP_K + 3], rest[3 * TOP_K + 4]
        base = _sc_worker() * per_worker

        def rows_at(c):
            return pl.ds(pl.multiple_of(base + c * SC_ROWS, SC_ROWS), SC_ROWS)

        def loaded(c, s):
            return pltpu.make_async_copy(x_hbm.at[rows_at(c)], rows_v[s], lsem.at[s])

        def load(c, s):
            loaded(c, s).start()
            for k in range(TOP_K):
                pltpu.sync_copy(dest_hbm[k].at[rows_at(c)], idx_v[s][k])

        def scattered(s):
            return [pltpu.make_async_copy(rows_v[s], out_hbm.at[idx_v[s][k]], ssem.at[s, k]) for k in range(TOP_K)]

        load(0, 0)

        @pl.loop(0, chunks, step=2)
        def _(c0):
            for s in range(2):
                c = c0 + s
                loaded(c, s).wait()
                for cp in scattered(s):
                    cp.start()

                @pl.when(c >= 1)
                def _():
                    for cp in scattered(1 - s):
                        cp.wait()

                @pl.when(c + 1 < chunks)
                def _():
                    load(c + 1, 1 - s)

        for cp in scattered(1):
            cp.wait()

    return scatter(x, *[dest[k] for k in range(TOP_K)])


def _gather_rows(table, idx):
    n, d = table.shape
    b = idx.shape[0]
    per_worker = b // (SC_CORES * SC_SUBCORES)
    chunks = per_worker // SC_ROWS
    assert chunks % 2 == 0

    @functools.partial(
        pl.kernel, mesh=_sc_mesh(),
        out_type=jax.ShapeDtypeStruct((b, d), table.dtype),
        scratch_types=[pltpu.VMEM((chunks, SC_ROWS), jnp.int32)] + [pltpu.VMEM((SC_ROWS, d), table.dtype)] * 2
                      + [pltpu.SemaphoreType.DMA((2,)), pltpu.SemaphoreType.DMA((2,))],
    )
    def gather(table_hbm, idx_hbm, out_hbm, idx_v, rows0, rows1, gsem, wsem):
        rows_v = (rows0, rows1)
        worker = _sc_worker()
        base = worker * per_worker
        pltpu.sync_copy(idx_hbm.at[pl.ds(pl.multiple_of(worker * chunks, chunks), chunks)], idx_v)

        def rows_at(c):
            return pl.ds(pl.multiple_of(base + c * SC_ROWS, SC_ROWS), SC_ROWS)

        def fetched(c, s):
            return pltpu.make_async_copy(table_hbm.at[idx_v.at[c]], rows_v[s], gsem.at[s])

        def written(c, s):
            return pltpu.make_async_copy(rows_v[s], out_hbm.at[rows_at(c)], wsem.at[s])

        fetched(0, 0).start()

        @pl.loop(0, chunks, step=2)
        def _(c0):
            for s in range(2):
                c = c0 + s
                fetched(c, s).wait()
                written(c, s).start()

                @pl.when(c >= 1)
                def _():
                    written(c - 1, 1 - s).wait()

                @pl.when(c + 1 < chunks)
                def _():
                    fetched(c + 1, 1 - s).start()

        written(chunks - 1, 1).wait()

    return gather(table, idx.reshape(b // SC_ROWS, SC_ROWS))


def _moe_body(te_ref, tv_ref, sl_ref, nx_ref, x_ref, w1_hbm, b1_ref, w2_hbm, b2_ref, o_ref,
              w1f, w2f, w1b, w2b, sem, *, layer):
    i = pl.program_id(0)
    e = te_ref[i]
    slot = sl_ref[i]
    new_expert = (i == 0) | (e != te_ref[jnp.maximum(i - 1, 0)])

    def weight_copies(expert, s):
        rows1 = w1f.shape[1] // MOE_W1_PARTS
        rows2 = w2f.shape[1] // MOE_W2_PARTS
        c1 = [pltpu.make_async_copy(w1_hbm.at[layer, expert, pl.ds(q * rows1, rows1)],
                                    w1f.at[s, pl.ds(q * rows1, rows1)], sem.at[s, q])
              for q in range(MOE_W1_PARTS)]
        c2 = [pltpu.make_async_copy(w2_hbm.at[layer, expert, pl.ds(q * rows2, rows2)],
                                    w2f.at[s, pl.ds(q * rows2, rows2)], sem.at[s, MOE_W1_PARTS + q])
              for q in range(MOE_W2_PARTS)]
        return c1 + c2

    @pl.when(i == 0)
    def _():
        for c in weight_copies(e, slot):
            c.start()

    @pl.when(new_expert)
    def _():
        for c in weight_copies(e, slot):
            c.wait()
        nxt = nx_ref[i]

        @pl.when(nxt >= 0)
        def _():
            for c in weight_copies(nxt, 1 - slot):
                c.start()

        w1b[...] = w1f[slot].astype(BF16)
        w2b[...] = w2f[slot].astype(BF16)

    def expert(rows):
        row = lax.broadcasted_iota(jnp.int32, (rows, x_ref.shape[1]), 0)
        lo, hi = _unpack_rows(jnp.where(row < tv_ref[i], x_ref[0:rows, :], 0))
        x = jnp.concatenate([lo, hi], axis=-1).astype(BF16)
        hb = _dot(x, w1b[...]) + b1_ref[0]
        x_glu = jnp.minimum(hb[:, :D_FF], SWIGLU_LIMIT)
        x_lin = jnp.clip(hb[:, D_FF:], -SWIGLU_LIMIT, SWIGLU_LIMIT)
        act = x_glu * jax.nn.sigmoid(SWIGLU_ALPHA * x_glu) * (x_lin + 1.0)
        o_ref[0:rows, :] = _pack_rows(_dot(act.astype(BF16), w2b[...]) + b2_ref[0])

    step = x_ref.shape[0] // MOE_ROW_PATHS
    for path in range(1, MOE_ROW_PATHS + 1):
        rows = path * step

        @pl.when((tv_ref[i] > rows - step) & (tv_ref[i] <= rows))
        def _(rows=rows):
            expert(rows)
            if rows < x_ref.shape[0]:
                o_ref[rows:, :] = jnp.zeros((x_ref.shape[0] - rows, o_ref.shape[1]), o_ref.dtype)

    @pl.when(tv_ref[i] == 0)
    def _():
        o_ref[...] = jnp.zeros_like(o_ref)


def _moe_experts(xs, tile_e, tile_rows, tile_slot, tile_next, w1, b1, w2, b2, layer):
    n_rows = xs.shape[0]
    tm = MOE_TILE
    nl, ne, d, ff2 = w1.shape
    bias_map = lambda i, te, tv, sl, nx: (layer, te[i], 0, 0)
    grid_spec = pltpu.PrefetchScalarGridSpec(
        num_scalar_prefetch=4,
        grid=(n_rows // tm,),
        in_specs=[pl.BlockSpec((tm, d // 2), lambda i, *_: (i, 0)),
                  pl.BlockSpec(memory_space=pl.ANY),
                  pl.BlockSpec((None, 1, 1, ff2), bias_map),
                  pl.BlockSpec(memory_space=pl.ANY),
                  pl.BlockSpec((None, 1, 1, d), bias_map)],
        out_specs=pl.BlockSpec((tm, d // 2), lambda i, *_: (i, 0)),
        scratch_shapes=[pltpu.VMEM((2, d, ff2), F32), pltpu.VMEM((2, ff2 // 2, d), F32),
                        pltpu.VMEM((d, ff2), BF16), pltpu.VMEM((ff2 // 2, d), BF16),
                        pltpu.SemaphoreType.DMA((2, MOE_W1_PARTS + MOE_W2_PARTS))],
    )
    return pl.pallas_call(
        functools.partial(_moe_body, layer=layer),
        out_shape=jax.ShapeDtypeStruct((n_rows, d // 2), jnp.int32),
        grid_spec=grid_spec,
        compiler_params=_params(("arbitrary",)),
        name="moe",
    )(tile_e, tile_rows, tile_slot, tile_next, xs, w1, b1.reshape(nl, ne, 1, ff2), w2, b2.reshape(nl, ne, 1, d))


def _combined(h_ref, y_refs, gate_ref):
    gates = gate_ref[...]
    lo = jnp.zeros(y_refs[0].shape, F32)
    hi = jnp.zeros(y_refs[0].shape, F32)
    for k, y_ref in enumerate(y_refs):
        yl, yh = _unpack_rows(y_ref[...])
        lo = lo + yl * gates[:, k:k + 1]
        hi = hi + yh * gates[:, k:k + 1]
    return h_ref[...] + jnp.concatenate([lo, hi], axis=-1)


def _combine_final_body(h_ref, y0_ref, y1_ref, y2_ref, y3_ref, gate_ref, g_ref, out_hbm, buf, sem):
    i = pl.program_id(0)
    n = pl.num_programs(0)
    slot = i % 2
    per_step = h_ref.shape[0] // BLK
    steps_per_span = NRES // per_step

    def writes(step, s):
        span = step // steps_per_span
        r0 = (step % steps_per_span) * per_step
        return [pltpu.make_async_copy(buf.at[s, pl.ds(rr * BLK, BLK), :], out_hbm.at[span, :, r0 + rr, :],
                                      sem.at[s, rr]) for rr in range(per_step)]

    @pl.when(i >= 2)
    def _():
        for cp in writes(i - 2, slot):
            cp.wait()

    buf[slot] = _rms(_combined(h_ref, (y0_ref, y1_ref, y2_ref, y3_ref), gate_ref), g_ref[...])
    for cp in writes(i, slot):
        cp.start()

    @pl.when(i == n - 1)
    def _():
        for cp in writes(i, slot):
            cp.wait()

        @pl.when(i >= 1)
        def _():
            for cp in writes(i - 1, 1 - slot):
                cp.wait()


def _combine_final(h, y, gates, g):
    s, d = h.shape
    tm = min(ROW_TILE, s)
    nt = s // tm
    in_specs = ([pl.BlockSpec((tm, d), lambda i: (i, 0))]
                + [pl.BlockSpec((tm, d // 2), lambda i, k=k: (k * nt + i, 0)) for k in range(TOP_K)]
                + [pl.BlockSpec((tm, TOP_K), lambda i: (i, 0))])
    out = pl.pallas_call(
        _combine_final_body,
        out_shape=jax.ShapeDtypeStruct((s // SPAN, BLK, NRES, d), F32),
        grid=(nt,),
        in_specs=in_specs + [pl.BlockSpec((1, d), lambda i: (0, 0))],
        out_specs=pl.BlockSpec(memory_space=pl.ANY),
        scratch_shapes=[pltpu.VMEM((2, tm, d), F32), pltpu.SemaphoreType.DMA((2, tm // BLK))],
        compiler_params=_params(("arbitrary",)),
        name="combine_final",
    )(h, y, y, y, y, gates, g.reshape(1, d))
    return out.reshape(s, d)


def _layer(state, mem, biases, p, l, g_final):
    row = lambda a: a.reshape(1, -1).astype(F32)
    q_scale = jnp.concatenate([jnp.full((D_A,), HD_A ** -0.5 * LOG2E, F32), jnp.ones((D_IN - D_A,), F32)])
    w_in = (p["w_in"][l] * q_scale).astype(BF16)
    if l == 0:
        h, proj = state, _inproj(state, p["norm_mix"][l], w_in)
    else:
        h, proj = _combine_inproj(*state, p["norm_mix"][l], w_in)
    ya = _dilated_attention(proj, biases)
    ops = _s5_operators(p["s5_a_re"][l], p["s5_a_im"][l], p["s5_b_re"][l], p["s5_b_im"][l],
                        p["s5_c_re"][l], p["s5_c_im"][l], p["s5_log_dt"][l], p["s5_d"][l])
    yb = _s5_core(proj, ops)
    kv = _memkv(mem, p["norm_mem"][l], p["w_xkv"][l].astype(BF16))
    wr = p["w_router"][l].astype(F32)
    wr_hi = wr.astype(BF16)
    wr2 = jnp.pad(jnp.concatenate([wr_hi, (wr - wr_hi.astype(F32)).astype(BF16)], axis=1),
                  ((0, 0), (0, LANES - 2 * N_EXPERTS)))
    br = jnp.pad(p["b_router"][l].astype(F32), (0, LANES - N_EXPERTS)).reshape(1, LANES)
    h2, xn, logits = _mid(
        h, ya, yb, p["w_glu"][l].astype(BF16), row(p["b_glu"][l]), row(p["g_out_attn"][l]),
        row(p["g_out_ssm"][l]), p["w_out"][l].astype(BF16), row(p["norm_xattn"][l]),
        (p["w_xq"][l] * (HD_X ** -0.5)).astype(BF16), kv[:, :D_X], kv[:, D_X:],
        p["w_xo"][l].astype(BF16), row(p["norm_moe"][l]), wr2, br)
    gates, dest, n_rows, tile_e, tile_rows, tile_slot, tile_next = _route(logits, MOE_TILE)
    out = _moe_experts(_scatter_rows(xn, dest, n_rows), tile_e, tile_rows, tile_slot, tile_next,
                       p["w1"], p["b1"], p["w2"], p["b2"], l)
    y = _gather_rows(out, dest.reshape(-1))
    return _combine_final(h2, y, gates, g_final) if l == DEPTH - 1 else (h2, y, gates)


def kernel(x, mem, rel_bias, norm_mix, w_in, s5_a_re, s5_a_im, s5_b_re, s5_b_im, s5_c_re, s5_c_im, s5_log_dt, s5_d, w_glu, b_glu, g_out_attn, g_out_ssm, w_out, norm_xattn, norm_mem, w_xq, w_xkv, w_xo, norm_moe, w_router, b_router, w1, b1, w2, b2, norm_final):
    p = dict(norm_mix=norm_mix, w_in=w_in, s5_a_re=s5_a_re, s5_a_im=s5_a_im, s5_b_re=s5_b_re,
             s5_b_im=s5_b_im, s5_c_re=s5_c_re, s5_c_im=s5_c_im, s5_log_dt=s5_log_dt, s5_d=s5_d,
             w_glu=w_glu, b_glu=b_glu, g_out_attn=g_out_attn, g_out_ssm=g_out_ssm, w_out=w_out,
             norm_xattn=norm_xattn, norm_mem=norm_mem, w_xq=w_xq, w_xkv=w_xkv, w_xo=w_xo,
             norm_moe=norm_moe, w_router=w_router, b_router=b_router, w1=w1, b1=b1, w2=w2, b2=b2)
    biases = [_attn_bias(rel_bias, window, dil, perm)
              for (window, dil), perm in zip(WIN_DIL, (_PERM_D1, _PERM_D4, _PERM_D16))]
    outs = []
    for b in range(x.shape[0]):
        h = _to_span_layout(x[b])
        for l in range(DEPTH):
            h = _layer(h, mem[b], biases, p, l, norm_final)
        outs.append(h)
    return jnp.stack(outs)
```

```python
import functools
import math

import jax
import jax.numpy as jnp
import numpy as np
from jax import lax
from jax.experimental import pallas as pl
from jax.experimental.pallas import tpu as pltpu
from jax.experimental.pallas import tpu_sc as plsc

F32 = jnp.float32
BF16 = jnp.bfloat16

D_MODEL = 1024
DEPTH = 2
EPS = 1e-5
NEG_INF = -1e30
LOG2E = math.log2(math.e)
H_A = 8
HD_A = 64
D_A = H_A * HD_A
WIN_DIL = ((128, 1), (512, 4), (2048, 16))
BLK = 128
D_B = D_MODEL - D_A
S5_CH = 16
S5_G = D_B // S5_CH
S5_P = 64
D_IN = 3 * D_A + D_B
NUM_BUCKETS = 32
REL_MAX_DIST = 2048
H_X = 4
HD_X = 128
D_X = H_X * HD_X
N_EXPERTS = 32
TOP_K = 4
D_FF = D_MODEL
SWIGLU_ALPHA = 1.702
SWIGLU_LIMIT = 7.0

LANES = 128
SUBLANES = 8
NRES = WIN_DIL[-1][1]
SPAN = NRES * BLK
S5_CHUNK = NRES
S5_PAIRS = S5_G // 2
VMEM_LIMIT = 56 * 1024 * 1024

SC_CORES = 2
SC_SUBCORES = 16
SC_ROWS = 64

S5_STAGGER = 4
ROW_TILE = 512
ROW_CHAINS = 2
MID_CHAINS = 4
RANK_TILE = 1024
MOE_TILE = 512
MOE_ROW_PATHS = 4
MOE_W1_PARTS = 4
MOE_W2_PARTS = 2


def _params(sem):
    return pltpu.CompilerParams(dimension_semantics=sem, vmem_limit_bytes=VMEM_LIMIT)


def _rms(x, g):
    return x * lax.rsqrt(jnp.mean(x * x, axis=-1, keepdims=True) + EPS) * g


def _dot(a, b):
    return jnp.dot(a, b, preferred_element_type=F32)


def _dot_nt(a, b):
    return lax.dot_general(a, b, (((1,), (1,)), ((), ())), preferred_element_type=F32)


def _full(a):
    return pl.BlockSpec(a.shape, lambda *_: (0,) * a.ndim)


def _pack_rows(x):
    c = x.shape[1] // 2
    lo = lax.bitcast_convert_type(x[:, :c].astype(BF16).astype(F32), jnp.uint32)
    hi = lax.bitcast_convert_type(x[:, c:].astype(BF16).astype(F32), jnp.uint32)
    return lax.bitcast_convert_type(lax.shift_right_logical(lo, jnp.uint32(16)) | hi, jnp.int32)


def _unpack_rows(p):
    u = lax.bitcast_convert_type(p, jnp.uint32)
    lo = lax.bitcast_convert_type(lax.shift_left(u, jnp.uint32(16)), F32)
    hi = lax.bitcast_convert_type(u & jnp.uint32(0xFFFF0000), F32)
    return lo, hi


def _to_span_layout(x):
    s = x.shape[0]
    return x.reshape(s // SPAN, BLK, NRES, -1).transpose(0, 2, 1, 3).reshape(s, -1)


def _from_span_layout(x):
    s = x.shape[0]
    return x.reshape(s // SPAN, NRES, BLK, -1).transpose(0, 2, 1, 3).reshape(s, -1)


def _row_chains(n_rows):
    part = n_rows // ROW_CHAINS
    return [slice(c * part, (c + 1) * part) for c in range(ROW_CHAINS)]


def _inproj_body(h_ref, g_ref, w_ref, o_ref):
    for rows in _row_chains(h_ref.shape[0]):
        xn = _rms(h_ref[rows, :], g_ref[...]).astype(BF16)
        o_ref[rows, :] = _dot(xn, w_ref[...]).astype(BF16)


def _inproj(h, g, w):
    s, d = h.shape
    n = w.shape[1]
    tm = min(ROW_TILE, s)
    return pl.pallas_call(
        _inproj_body,
        out_shape=jax.ShapeDtypeStruct((s, n), BF16),
        grid=(s // tm,),
        in_specs=[pl.BlockSpec((tm, d), lambda i: (i, 0)),
                  pl.BlockSpec((1, d), lambda i: (0, 0)),
                  pl.BlockSpec((d, n), lambda i: (0, 0))],
        out_specs=pl.BlockSpec((tm, n), lambda i: (i, 0)),
        compiler_params=_params(("parallel",)),
        name="inproj",
    )(h, g.reshape(1, d), w)


def _combine_inproj_body(h_ref, y0_ref, y1_ref, y2_ref, y3_ref, gate_ref, g_ref, w_ref, h_out, o_ref):
    chains = _row_chains(h_ref.shape[0])
    hs = [_combined(h_ref.at[rows, :], [y.at[rows, :] for y in (y0_ref, y1_ref, y2_ref, y3_ref)],
                    gate_ref.at[rows, :]) for rows in chains]
    for rows, h in zip(chains, hs):
        h_out[rows, :] = h
    for rows, h in zip(chains, hs):
        o_ref[rows, :] = _dot(_rms(h, g_ref[...]).astype(BF16), w_ref[...]).astype(BF16)


def _combine_inproj(h, y, gates, g, w):
    s, d = h.shape
    n = w.shape[1]
    tm = min(ROW_TILE, s)
    nt = s // tm
    return pl.pallas_call(
        _combine_inproj_body,
        out_shape=[jax.ShapeDtypeStruct((s, d), F32), jax.ShapeDtypeStruct((s, n), BF16)],
        grid=(nt,),
        in_specs=[pl.BlockSpec((tm, d), lambda i: (i, 0))]
                 + [pl.BlockSpec((tm, d // 2), lambda i, k=k: (k * nt + i, 0)) for k in range(TOP_K)]
                 + [pl.BlockSpec((tm, TOP_K), lambda i: (i, 0)), pl.BlockSpec((1, d), lambda i: (0, 0)),
                    pl.BlockSpec((d, n), lambda i: (0, 0))],
        out_specs=[pl.BlockSpec((tm, d), lambda i: (i, 0)), pl.BlockSpec((tm, n), lambda i: (i, 0))],
        compiler_params=_params(("parallel",)),
        name="combine_inproj",
    )(h, y, y, y, y, gates, g.reshape(1, d), w)


def _t5_bucket(n):
    max_exact = NUM_BUCKETS // 2
    nf = jnp.maximum(n, 1).astype(F32)
    large = max_exact + (jnp.log(nf / max_exact) / math.log(REL_MAX_DIST / max_exact)
                         * (NUM_BUCKETS - max_exact)).astype(jnp.int32)
    large = jnp.minimum(large, NUM_BUCKETS - 1)
    return jnp.where(n < max_exact, n, large)


def _attn_bias(rel_bias, window, dil, perm):
    steps = window // dil
    perm = jnp.asarray(perm, jnp.int32)
    qi = perm[:, None]
    ki = jnp.concatenate([perm, BLK + perm])[None, :]
    dist = BLK + qi - ki
    in_win = (dist >= 0) & (dist <= steps)
    bucket = _t5_bucket(jnp.clip(dist, 0, steps) * dil)
    onehot = (bucket[:, :, None] == jnp.arange(NUM_BUCKETS, dtype=jnp.int32)).astype(F32)
    bias = jnp.einsum('qkb,bh->hqk', onehot, rel_bias.astype(F32), precision=lax.Precision.HIGHEST)
    bias = jnp.where(in_win[None], bias * LOG2E, NEG_INF)
    return bias.reshape(H_A // 2, 2 * BLK, 2 * BLK)


_PERM_D1 = [NRES * jl + r for r in range(NRES) for jl in range(BLK // NRES)]
_PERM_D4 = [4 * jl + i for i in range(4) for jl in range(BLK // 4)]
_PERM_D16 = list(range(BLK))


def _attn_body(q_ref, k_ref, v_ref, kp_ref, vp_ref, b1_ref, b4_ref, b16_ref, o_ref, acc, mst, lst):
    has_prev = pl.program_id(0) > 0
    lane = lax.broadcasted_iota(jnp.int32, (1, LANES), 1)
    lo = lane < HD_A
    mlo = lo.astype(BF16)
    mhi = (~lo).astype(BF16)
    col = lax.broadcasted_iota(jnp.int32, (2 * BLK, 2 * BLK), 1)
    ones = jnp.ones((2 * BLK, LANES), BF16)

    def tile(q2, kk, vv, bias, mask_prev):
        qs = jnp.concatenate([q2 * mlo, q2 * mhi], axis=0)
        s = _dot_nt(qs, kk) + bias
        if mask_prev:
            s = jnp.where(jnp.logical_or(has_prev, col >= BLK), s, NEG_INF)
        m = jnp.max(s, axis=-1, keepdims=True)
        e = jnp.exp2((s - m).astype(BF16))
        oa = _dot(e, jnp.concatenate([vv, ones], axis=1))
        o = oa[:, :LANES]
        l = oa[:, LANES:]
        return (jnp.where(lo, m[:BLK], m[BLK:]), jnp.where(lo, l[:BLK], l[BLK:]),
                jnp.where(lo, o[:BLK], o[BLK:]))

    def merge(prev, cur):
        mp, lp, ap = prev
        mc, lc, ac = cur
        mn = jnp.maximum(mp, mc)
        a = jnp.exp2(mp - mn)
        b = jnp.exp2(mc - mn)
        return mn, a * lp + b * lc, a * ap + b * ac

    def cat(xs):
        return jnp.concatenate(xs, axis=0)


    for r in range(NRES):
        rows = pl.ds(r * BLK, BLK)
        for hp in range(H_A // 2):
            lanes = slice(hp * LANES, (hp + 1) * LANES)
            kk = cat([kp_ref[rows, lanes], k_ref[rows, lanes]])
            vv = cat([vp_ref[rows, lanes], v_ref[rows, lanes]])
            m2, l2, o2 = tile(q_ref[rows, lanes], kk, vv, b16_ref[hp], True)
            mst[rows, lanes] = m2
            lst[rows, lanes] = l2
            acc[rows, lanes] = o2

    for r4 in range(4):
        for b in range(4):
            def chunk_rows(bb):
                return [pl.ds(4 * BLK * i + BLK * r4 + 32 * bb, 32) for i in range(4)]
            rows = chunk_rows(b)
            prows = chunk_rows(3 if b == 0 else b - 1)
            kprev, vprev = (kp_ref, vp_ref) if b == 0 else (k_ref, v_ref)
            for hp in range(H_A // 2):
                lanes = slice(hp * LANES, (hp + 1) * LANES)
                q2 = cat([q_ref[rr, lanes] for rr in rows])
                kk = cat([kprev[rr, lanes] for rr in prows] + [k_ref[rr, lanes] for rr in rows])
                vv = cat([vprev[rr, lanes] for rr in prows] + [v_ref[rr, lanes] for rr in rows])
                cur = tile(q2, kk, vv, b4_ref[hp], b == 0)
                prev = (cat([mst[rr, lanes] for rr in rows]), cat([lst[rr, lanes] for rr in rows]),
                        cat([acc[rr, lanes] for rr in rows]))
                mn, ln, an = merge(prev, cur)
                for i, rr in enumerate(rows):
                    part = slice(32 * i, 32 * (i + 1))
                    mst[rr, lanes] = mn[part]
                    lst[rr, lanes] = ln[part]
                    acc[rr, lanes] = an[part]

    def d1_pair(ap, kprev, vprev, prev_ap, mask_prev):
        def tiles(a_):
            return [pl.ds(BLK * r + 16 * a_, 16) for r in range(NRES)]
        cur_t = tiles(ap)
        prev_t = tiles(prev_ap)

        def halves(ref, ts, lanes):
            xs = [ref[t, lanes].astype(F32) for t in ts]
            return cat([x[:8] for x in xs]).astype(BF16), cat([x[8:] for x in xs]).astype(BF16)

        for hp in range(H_A // 2):
            lanes = slice(hp * LANES, (hp + 1) * LANES)
            q_e, q_o = halves(q_ref, cur_t, lanes)
            k_e, k_o = halves(k_ref, cur_t, lanes)
            v_e, v_o = halves(v_ref, cur_t, lanes)
            _, k_p = halves(kprev, prev_t, lanes)
            _, v_p = halves(vprev, prev_t, lanes)
            cur_e = tile(q_e, cat([k_p, k_e]), cat([v_p, v_e]), b1_ref[hp], mask_prev)
            cur_o = tile(q_o, cat([k_e, k_o]), cat([v_e, v_o]), b1_ref[hp], False)
            ms = [mst[t, lanes] for t in cur_t]
            ls = [lst[t, lanes] for t in cur_t]
            ac = [acc[t, lanes] for t in cur_t]
            outs = []
            for half, cur in ((0, cur_e), (1, cur_o)):
                part = slice(8 * half, 8 * half + 8)
                prev = (cat([x[part] for x in ms]), cat([x[part] for x in ls]), cat([x[part] for x in ac]))
                _, ln, an = merge(prev, cur)
                outs.append(an / ln)
            for r, t in enumerate(cur_t):
                part = slice(8 * r, 8 * r + 8)
                o_ref[t, lanes] = cat([outs[0][part], outs[1][part]]).astype(o_ref.dtype)

    d1_pair(0, kp_ref, vp_ref, BLK // 16 - 1, True)
    for ap in range(1, BLK // 16):
        d1_pair(ap, k_ref, v_ref, ap - 1, False)


def _dilated_attention(proj, biases):
    s = proj.shape[0]
    cur = lambda which: pl.BlockSpec((SPAN, D_A), lambda c: (c, which))
    prev = lambda which: pl.BlockSpec((SPAN, D_A), lambda c: (jnp.maximum(c - 1, 0), which))
    return pl.pallas_call(
        _attn_body,
        out_shape=jax.ShapeDtypeStruct((s, D_A), BF16),
        grid=(s // SPAN,),
        in_specs=[cur(0), cur(1), cur(2), prev(1), prev(2)] + [_full(b) for b in biases],
        out_specs=pl.BlockSpec((SPAN, D_A), lambda c: (c, 0)),
        scratch_shapes=[pltpu.VMEM((SPAN, D_A), F32)] * 3,
        compiler_params=_params(("arbitrary",)),
        name="attn",
    )(proj, proj, proj, proj, proj, *biases)


def _s5_operators(a_re, a_im, b_re, b_im, c_re, c_im, log_dt, d_skip):
    L = S5_CHUNK
    lam = lax.complex(a_re.astype(F32), a_im.astype(F32))
    dt = jnp.exp(log_dt.astype(F32))[:, None]
    a_bar = jnp.exp(lam * dt)
    b_bar = ((a_bar - 1.0) / lam)[..., None] * lax.complex(b_re.astype(F32), b_im.astype(F32))
    c = lax.complex(c_re.astype(F32), c_im.astype(F32))
    j = jnp.arange(L + 1, dtype=F32)
    log_a = lam * dt
    apow = jnp.exp(log_a[None] * j[:, None, None])
    kt = jnp.einsum('gdp,jgp,gpc->gcjd', c, apow[:L], b_bar).real
    skip = d_skip.astype(F32).reshape(S5_G, S5_CH, 1, 1) * jnp.eye(S5_CH)[None, :, None, :]
    kt = (kt + skip * (jnp.arange(L) == 0)[None, None, :, None]).reshape(S5_G, S5_CH, L * S5_CH)
    p = jnp.einsum('sgp,gpc->gscp', apow[:L][::-1], b_bar).reshape(S5_G, L * S5_CH, S5_P)
    ca = jnp.einsum('gdp,tgp->gptd', c, apow[1:L + 1]).reshape(S5_G, S5_P, L * S5_CH)
    a_l = apow[L]

    def pair_blocks(x):
        g, r, w = x.shape
        x = x.reshape(S5_PAIRS, 2, r, w)
        z = jnp.zeros_like(x[:, 0])
        top = jnp.concatenate([x[:, 0], z], axis=-1)
        bot = jnp.concatenate([z, x[:, 1]], axis=-1)
        return jnp.concatenate([top, bot], axis=1)

    p2 = jnp.concatenate([pair_blocks(p.real), pair_blocks(p.imag)], axis=-1)
    q2 = jnp.concatenate([pair_blocks(ca.real), pair_blocks(-ca.imag)], axis=1)
    a_lr = a_l.real.reshape(1, S5_G * S5_P)
    a_li = a_l.imag.reshape(1, S5_G * S5_P)
    return kt, p2.astype(BF16), q2.astype(BF16), a_lr, a_li


def _s5_body(u_ref, kt_ref, p_ref, q_ref, ar_ref, ai_ref, o_ref, m_ref, w_ref, y_ref,
             ere, eim, xre, xim, sre, sim):
    rows = BLK
    gw = S5_CHUNK * S5_CH
    pw = 2 * gw
    per_vreg = LANES // S5_CH
    chunk_of_lane = lax.broadcasted_iota(jnp.int32, (rows, LANES), 1) // S5_CH

    def chunk_transpose(arrs):
        arrs = list(arrs)
        for s in (4, 2, 1):
            upper = (chunk_of_lane & s) != 0
            nxt = list(arrs)
            for i in range(per_vreg):
                if i & s:
                    continue
                lo_a, hi_a = arrs[i], arrs[i + s]
                nxt[i] = jnp.where(upper, pltpu.roll(hi_a, s * S5_CH, axis=1), lo_a)
                nxt[i + s] = jnp.where(upper, hi_a, pltpu.roll(lo_a, LANES - s * S5_CH, axis=1))
            arrs = nxt
        return arrs

    @pl.when(pl.program_id(0) == 0)
    def _():
        sre[...] = jnp.zeros_like(sre)
        sim[...] = jnp.zeros_like(sim)
        lane = lax.broadcasted_iota(jnp.int32, (S5_CH, gw), 1)

        def build(g, _):
            kt = kt_ref[g]
            for s in range(S5_CHUNK):
                blk = kt if s == 0 else jnp.where(lane >= s * S5_CH, pltpu.roll(kt, s * S5_CH, axis=1), 0.0)
                m_ref[g, s * S5_CH:(s + 1) * S5_CH, :] = blk.astype(BF16)
            return 0

        lax.fori_loop(0, S5_G, build, 0)

    pairs_per_block = per_vreg // 2
    state_w = pairs_per_block * LANES

    def lane_block(b):
        st = slice(b * state_w, (b + 1) * state_w)
        for a in range(S5_CHUNK // per_vreg):
            srcs = [u_ref[(per_vreg * a + i) * rows:(per_vreg * a + i + 1) * rows,
                          b * LANES:(b + 1) * LANES].astype(F32) for i in range(per_vreg)]
            for gi, arr in enumerate(chunk_transpose(srcs)):
                g = per_vreg * b + gi
                w_ref[:, g * gw + a * LANES:g * gw + (a + 1) * LANES] = arr.astype(BF16)
        yield
        for pr in range(pairs_per_block * b, pairs_per_block * (b + 1)):
            e = _dot(w_ref[:, pr * pw:(pr + 1) * pw], p_ref[pr])
            ere[:, pr * LANES:(pr + 1) * LANES] = e[:, :LANES]
            eim[:, pr * LANES:(pr + 1) * LANES] = e[:, LANES:]
        yield
        ar = ar_ref[:, st]
        ai = ai_ref[:, st]
        xr = sre[:, st]
        xi = sim[:, st]
        for n in range(rows):
            xre[n:n + 1, st] = xr
            xim[n:n + 1, st] = xi
            xr, xi = (ar * xr - ai * xi + ere[n:n + 1, st], ar * xi + ai * xr + eim[n:n + 1, st])
            if n % S5_CHUNK == S5_CHUNK - 1:
                yield
        sre[:, st] = xr
        sim[:, st] = xi
        for pr in range(pairs_per_block * b, pairs_per_block * (b + 1)):
            xin = jnp.concatenate([xre[:, pr * LANES:(pr + 1) * LANES],
                                   xim[:, pr * LANES:(pr + 1) * LANES]], axis=-1).astype(BF16)
            yc = _dot(xin, q_ref[pr])
            for half in range(2):
                g = 2 * pr + half
                cols = slice(g * gw, (g + 1) * gw)
                y = _dot(w_ref[:, cols], m_ref[g]) + yc[:, half * gw:(half + 1) * gw]
                y_ref[:, cols] = 0.5 * y * (1.0 + lax.erf(y * (2.0 ** -0.5)))
            yield
        for a in range(S5_CHUNK // per_vreg):
            srcs = [y_ref[:, (per_vreg * b + i) * gw + a * LANES:(per_vreg * b + i) * gw + (a + 1) * LANES]
                    for i in range(per_vreg)]
            for ri, arr in enumerate(chunk_transpose(srcs)):
                r = per_vreg * a + ri
                o_ref[r * rows:(r + 1) * rows, b * LANES:(b + 1) * LANES] = arr.astype(BF16)
        yield

    blocks = [lane_block(b) for b in range(D_B // LANES)]
    live = [True] * len(blocks)
    rnd = 0
    while any(live):
        for b, blk in enumerate(blocks):
            if live[b] and rnd >= S5_STAGGER * b:
                live[b] = next(blk, "done") != "done"
        rnd += 1


def _s5_core(proj, ops):
    kt, p2, q2, a_lr, a_li = ops
    s = proj.shape[0]
    gw = S5_CHUNK * S5_CH
    wide = S5_G * gw
    nstate = S5_G * S5_P
    return pl.pallas_call(
        _s5_body,
        out_shape=jax.ShapeDtypeStruct((s, D_B), BF16),
        grid=(s // SPAN,),
        in_specs=[pl.BlockSpec((SPAN, D_B), lambda i: (i, 3 * D_A // D_B)),
                  _full(kt), _full(p2), _full(q2), _full(a_lr), _full(a_li)],
        out_specs=pl.BlockSpec((SPAN, D_B), lambda i: (i, 0)),
        scratch_shapes=[pltpu.VMEM((S5_G, gw, gw), BF16), pltpu.VMEM((BLK, wide), BF16),
                        pltpu.VMEM((BLK, wide), F32)]
                       + [pltpu.VMEM((BLK, nstate), F32)] * 4 + [pltpu.VMEM((1, nstate), F32)] * 2,
        compiler_params=_params(("arbitrary",)),
        name="s5",
    )(proj, kt, p2, q2, a_lr, a_li)


def _split_bf16(x):
    hi = x.astype(BF16)
    lo = (x - hi.astype(F32)).astype(BF16)
    return hi, lo


def _mid_body(h_ref, ya_ref, yb_ref, wglu_ref, bglu_ref, ga_ref, gb_ref, wout_ref, gx_ref, wq_ref,
              k_ref, v_ref, wo_ref, gm_ref, wr_ref, br_ref, h_out, xn_out, logit_out):
    part = h_ref.shape[0] // MID_CHAINS
    slices = [slice(c * part, (c + 1) * part) for c in range(MID_CHAINS)]
    chains = [_mid_rows(h_ref[rows, :], ya_ref[rows, :], yb_ref[rows, :], wglu_ref, bglu_ref, ga_ref, gb_ref,
                        wout_ref, gx_ref, wq_ref, k_ref, v_ref, wo_ref, gm_ref, wr_ref, br_ref)
              for rows in slices]
    for outs in zip(*chains):
        pass
    for rows, out in zip(slices, outs):
        for ref, val in zip((h_out, xn_out, logit_out), out):
            ref[rows, :] = val


def _mid_rows(h, ya, yb, wglu_ref, bglu_ref, ga_ref, gb_ref, wout_ref, gx_ref, wq_ref,
              k_ref, v_ref, wo_ref, gm_ref, wr_ref, br_ref):
    gate = jax.nn.sigmoid(_dot(yb, wglu_ref[...]) + bglu_ref[...])
    yb2 = yb.astype(F32) * gate
    na = _rms(ya.astype(F32), ga_ref[...]).astype(BF16)
    nb = _rms(yb2, gb_ref[...]).astype(BF16)
    h1 = h + _dot(na, wout_ref[0:D_A, :]) + _dot(nb, wout_ref[D_A:D_MODEL, :])
    yield None
    q = _dot(_rms(h1, gx_ref[...]).astype(BF16), wq_ref[...]).astype(BF16)
    yield None
    heads = []
    for hd in range(H_X):
        lanes = slice(hd * HD_X, (hd + 1) * HD_X)
        s = _dot_nt(q[:, lanes], k_ref[:, lanes])
        e = jnp.exp(s - jnp.max(s, axis=-1, keepdims=True))
        heads.append(_dot(e.astype(BF16), v_ref[:, lanes]) / jnp.sum(e, axis=-1, keepdims=True))
    o = jnp.concatenate(heads, axis=-1).astype(BF16)
    yield None
    h2 = h1 + _dot(o, wo_ref[...])
    yield None
    xn = _rms(h2, gm_ref[...])
    x_hi, x_lo = _split_bf16(xn)
    part = _dot(x_hi, wr_ref[...]) + _dot(x_lo, wr_ref[...])
    logits = part + pltpu.roll(part, LANES - N_EXPERTS, axis=1) + br_ref[...]
    yield h2, _pack_rows(xn), logits


def _mid(h, ya, yb, wglu, bglu, ga, gb, wout, gx, wq, kmem, vmem, wo, gm, wr2, br):
    s = h.shape[0]
    tm = min(MID_CHAINS * ROW_TILE // 2, s)
    row = lambda w: pl.BlockSpec((tm, w), lambda i: (i, 0))
    consts = [wglu, bglu, ga, gb, wout, gx, wq, kmem, vmem, wo, gm, wr2, br]
    return pl.pallas_call(
        _mid_body,
        out_shape=[jax.ShapeDtypeStruct((s, D_MODEL), F32), jax.ShapeDtypeStruct((s, D_MODEL // 2), jnp.int32),
                   jax.ShapeDtypeStruct((s, LANES), F32)],
        grid=(s // tm,),
        in_specs=[row(D_MODEL), row(D_A), row(D_B)] + [_full(a) for a in consts],
        out_specs=[row(D_MODEL), row(D_MODEL // 2), row(LANES)],
        compiler_params=_params(("parallel",)),
        name="mid",
    )(h, ya, yb, *consts)


def _memkv_body(mem_ref, g_ref, w_ref, o_ref):
    o_ref[...] = _dot(_rms(mem_ref[...], g_ref[...]).astype(BF16), w_ref[...]).astype(BF16)


def _memkv(mem, g, w):
    n, d = mem.shape
    return pl.pallas_call(
        _memkv_body,
        out_shape=jax.ShapeDtypeStruct((n, w.shape[1]), BF16),
        compiler_params=pltpu.CompilerParams(vmem_limit_bytes=VMEM_LIMIT),
        name="memkv",
    )(mem, g.reshape(1, d), w)


def _rank_body(logit_ref, tri_ref, idx_ref, gate_ref, rank_ref, cnt_ref, carry):
    @pl.when(pl.program_id(0) == 0)
    def _():
        carry[...] = jnp.zeros_like(carry)

    tm = logit_ref.shape[0]
    logits = jnp.transpose(logit_ref[...])[:N_EXPERTS, :]
    expert = lax.broadcasted_iota(jnp.int32, (N_EXPERTS, tm), 0)
    vals, idxs = [], []
    for _ in range(TOP_K):
        mx = jnp.max(logits, axis=0, keepdims=True)
        ix = jnp.min(jnp.where(logits == mx, expert, N_EXPERTS), axis=0, keepdims=True)
        vals.append(mx)
        idxs.append(ix)
        logits = jnp.where(expert == ix, -jnp.inf, logits)
    es = [jnp.exp(v - vals[0]) for v in vals]
    den = es[0] + es[1] + es[2] + es[3]
    hits = [expert == ix for ix in idxs]
    onehot = jnp.zeros((N_EXPERTS, tm), F32)
    for hit in hits:
        onehot = onehot + jnp.where(hit, 1.0, 0.0)
    nb = tm // LANES
    blocks = jnp.concatenate([onehot[:, b * LANES:(b + 1) * LANES] for b in range(nb)], axis=0)
    inc = _dot(blocks.astype(BF16), tri_ref[...])
    run = carry[...][:, 0:1]
    before = []
    for b in range(nb):
        inc_b = inc[b * N_EXPERTS:(b + 1) * N_EXPERTS, :]
        before.append(run + inc_b - onehot[:, b * LANES:(b + 1) * LANES])
        run = run + inc_b[:, LANES - 1:LANES]
    before = jnp.concatenate(before, axis=1)
    choice = lax.broadcasted_iota(jnp.int32, (SUBLANES, tm), 0)
    idx_t = jnp.full((SUBLANES, tm), N_EXPERTS, jnp.int32)
    gate_t = jnp.zeros((SUBLANES, tm), F32)
    rank_t = jnp.zeros((SUBLANES, tm), jnp.int32)
    for k in range(TOP_K):
        rk = jnp.sum(jnp.where(hits[k], before, 0.0), axis=0, keepdims=True)
        idx_t = jnp.where(choice == k, idxs[k], idx_t)
        gate_t = jnp.where(choice == k, es[k] / den, gate_t)
        rank_t = jnp.where(choice == k, rk.astype(jnp.int32), rank_t)
    idx_ref[...] = idx_t
    gate_ref[...] = gate_t
    rank_ref[...] = rank_t
    total = jnp.broadcast_to(run, carry.shape)
    carry[...] = total
    cnt_ref[...] = total.astype(jnp.int32)


def _rank(logits):
    t = logits.shape[0]
    tm = min(RANK_TILE, t)
    cols = pl.BlockSpec((SUBLANES, tm), lambda i: (0, i))
    tri = jnp.asarray(np.triu(np.ones((LANES, LANES), np.float32)), BF16)
    return pl.pallas_call(
        _rank_body,
        out_shape=[jax.ShapeDtypeStruct((SUBLANES, t), jnp.int32), jax.ShapeDtypeStruct((SUBLANES, t), F32),
                   jax.ShapeDtypeStruct((SUBLANES, t), jnp.int32),
                   jax.ShapeDtypeStruct((N_EXPERTS, LANES), jnp.int32)],
        grid=(t // tm,),
        in_specs=[pl.BlockSpec((tm, LANES), lambda i: (i, 0)), _full(tri)],
        out_specs=[cols, cols, cols, pl.BlockSpec((N_EXPERTS, LANES), lambda i: (0, 0))],
        scratch_shapes=[pltpu.VMEM((N_EXPERTS, LANES), F32)],
        compiler_params=_params(("arbitrary",)),
        name="rank",
    )(logits, tri)


def _route(logits, tm):
    t = logits.shape[0]
    tk = t * TOP_K
    idx, gates, rank, cnt = _rank(logits)
    counts = cnt[:, 0]
    padded = (counts + tm - 1) // tm * tm
    pend = jnp.cumsum(padded)
    pstart = pend - padded
    n_rows = tk + N_EXPERTS * tm
    n_tiles = n_rows // tm
    experts = jnp.arange(N_EXPERTS, dtype=jnp.int32)
    tile_first = jnp.arange(n_tiles, dtype=jnp.int32) * tm
    last_used = jnp.max(jnp.where(padded > 0, experts, 0))
    tile_e = jnp.minimum(jnp.sum(tile_first[:, None] >= pend[None, :], axis=1), last_used).astype(jnp.int32)
    tile_rows = jnp.clip(jnp.sum(jnp.where(tile_e[:, None] == experts[None, :],
                                           (pstart + counts)[None, :], 0), axis=1) - tile_first, 0, tm)
    tile_rows = jnp.where(tile_first < pend[-1], tile_rows, 0).astype(jnp.int32)
    group = jnp.cumsum(jnp.concatenate([jnp.zeros((1,), jnp.int32),
                                        (tile_e[1:] != tile_e[:-1]).astype(jnp.int32)]))
    tile_slot = (group % 2).astype(jnp.int32)
    later = (experts[None, :] > experts[:, None]) & (padded > 0)[None, :]
    next_e = jnp.min(jnp.where(later, experts[None, :], N_EXPERTS), axis=1)
    next_e = jnp.where(next_e < N_EXPERTS, next_e, -1).astype(jnp.int32)
    tile_next = jnp.sum(jnp.where(tile_e[:, None] == experts[None, :], next_e[None, :], 0), axis=1).astype(jnp.int32)
    base = jnp.sum(jnp.where(idx[:TOP_K, :, None] == experts, pstart, 0), axis=-1)
    dest = rank[:TOP_K] + base
    return gates[:TOP_K].T, dest, n_rows, tile_e, tile_rows, tile_slot, tile_next


def _sc_mesh():
    return plsc.VectorSubcoreMesh(core_axis_name="c", subcore_axis_name="s",
                                  num_cores=SC_CORES, num_subcores=SC_SUBCORES)


def _sc_worker():
    return lax.axis_index("s") * SC_CORES + lax.axis_index("c")


def _scatter_rows(x, dest, n_rows):
    t, d = x.shape
    per_worker = t // (SC_CORES * SC_SUBCORES)
    chunks = per_worker // SC_ROWS

    assert chunks % 2 == 0

    @functools.partial(
        pl.kernel, mesh=_sc_mesh(),
        out_type=jax.ShapeDtypeStruct((n_rows, d), x.dtype),
        scratch_types=[pltpu.VMEM((SC_ROWS, d), x.dtype)] * 2 + [pltpu.VMEM((SC_ROWS,), jnp.int32)] * (2 * TOP_K)
                      + [pltpu.SemaphoreType.DMA((2,)), pltpu.SemaphoreType.DMA((2, TOP_K))],
    )
    def scatter(x_hbm, *rest):
        dest_hbm, out_hbm = rest[:TOP_K], rest[TOP_K]
        rows_v = rest[TOP_K + 1:TOP_K + 3]
        idx_v = (rest[TOP_K + 3:2 * TOP_K + 3], rest[2 * TOP_K + 3:3 * TOP_K + 3])
        lsem, ssem = rest[3 * TOP_K + 3], rest[3 * TOP_K + 4]
        base = _sc_worker() * per_worker

        def rows_at(c):
            return pl.ds(pl.multiple_of(base + c * SC_ROWS, SC_ROWS), SC_ROWS)

        def loaded(c, s):
            return pltpu.make_async_copy(x_hbm.at[rows_at(c)], rows_v[s], lsem.at[s])

        def load(c, s):
            loaded(c, s).start()
            for k in range(TOP_K):
                pltpu.sync_copy(dest_hbm[k].at[rows_at(c)], idx_v[s][k])

        def scattered(s):
            return [pltpu.make_async_copy(rows_v[s], out_hbm.at[idx_v[s][k]], ssem.at[s, k]) for k in range(TOP_K)]

        load(0, 0)

        @pl.loop(0, chunks, step=2)
        def _(c0):
            for s in range(2):
                c = c0 + s
                loaded(c, s).wait()
                for cp in scattered(s):
                    cp.start()

                @pl.when(c >= 1)
                def _():
                    for cp in scattered(1 - s):
                        cp.wait()

                @pl.when(c + 1 < chunks)
                def _():
                    load(c + 1, 1 - s)

        for cp in scattered(1):
            cp.wait()

    return scatter(x, *[dest[k] for k in range(TOP_K)])


def _gather_rows(table, idx):
    n, d = table.shape
    b = idx.shape[0]
    per_worker = b // (SC_CORES * SC_SUBCORES)
    chunks = per_worker // SC_ROWS
    assert chunks % 2 == 0

    @functools.partial(
        pl.kernel, mesh=_sc_mesh(),
        out_type=jax.ShapeDtypeStruct((b, d), table.dtype),
        scratch_types=[pltpu.VMEM((chunks, SC_ROWS), jnp.int32)] + [pltpu.VMEM((SC_ROWS, d), table.dtype)] * 2
                      + [pltpu.SemaphoreType.DMA((2,)), pltpu.SemaphoreType.DMA((2,))],
    )
    def gather(table_hbm, idx_hbm, out_hbm, idx_v, rows0, rows1, gsem, wsem):
        rows_v = (rows0, rows1)
        worker = _sc_worker()
        base = worker * per_worker
        pltpu.sync_copy(idx_hbm.at[pl.ds(pl.multiple_of(worker * chunks, chunks), chunks)], idx_v)

        def rows_at(c):
            return pl.ds(pl.multiple_of(base + c * SC_ROWS, SC_ROWS), SC_ROWS)

        def fetched(c, s):
            return pltpu.make_async_copy(table_hbm.at[idx_v.at[c]], rows_v[s], gsem.at[s])

        def written(c, s):
            return pltpu.make_async_copy(rows_v[s], out_hbm.at[rows_at(c)], wsem.at[s])

        fetched(0, 0).start()

        @pl.loop(0, chunks, step=2)
        def _(c0):
            for s in range(2):
                c = c0 + s
                fetched(c, s).wait()
                written(c, s).start()

                @pl.when(c >= 1)
                def _():
                    written(c - 1, 1 - s).wait()

                @pl.when(c + 1 < chunks)
                def _():
                    fetched(c + 1, 1 - s).start()

        written(chunks - 1, 1).wait()

    return gather(table, idx.reshape(b // SC_ROWS, SC_ROWS))


def _moe_body(te_ref, tv_ref, sl_ref, nx_ref, x_ref, w1_hbm, b1_ref, w2_hbm, b2_ref, o_ref,
              w1f, w2f, w1b, w2b, sem, *, layer):
    i = pl.program_id(0)
    e = te_ref[i]
    slot = sl_ref[i]
    new_expert = (i == 0) | (e != te_ref[jnp.maximum(i - 1, 0)])

    def weight_copies(expert, s):
        rows1 = w1f.shape[1] // MOE_W1_PARTS
        rows2 = w2f.shape[1] // MOE_W2_PARTS
        c1 = [pltpu.make_async_copy(w1_hbm.at[layer, expert, pl.ds(q * rows1, rows1)],
                                    w1f.at[s, pl.ds(q * rows1, rows1)], sem.at[s, q])
              for q in range(MOE_W1_PARTS)]
        c2 = [pltpu.make_async_copy(w2_hbm.at[layer, expert, pl.ds(q * rows2, rows2)],
                                    w2f.at[s, pl.ds(q * rows2, rows2)], sem.at[s, MOE_W1_PARTS + q])
              for q in range(MOE_W2_PARTS)]
        return c1 + c2

    def start_all(copies):
        for n, c in enumerate(copies):
            c.start(priority=n % 2)

    @pl.when(i == 0)
    def _():
        start_all(weight_copies(e, slot))

    @pl.when(new_expert)
    def _():
        for c in weight_copies(e, slot):
            c.wait()
        nxt = nx_ref[i]

        @pl.when(nxt >= 0)
        def _():
            start_all(weight_copies(nxt, 1 - slot))

        w1b[...] = w1f[slot].astype(BF16)
        w2b[...] = w2f[slot].astype(BF16)

    def expert(rows):
        row = lax.broadcasted_iota(jnp.int32, (rows, x_ref.shape[1]), 0)
        lo, hi = _unpack_rows(jnp.where(row < tv_ref[i], x_ref[0:rows, :], 0))
        x = jnp.concatenate([lo, hi], axis=-1).astype(BF16)
        hb = _dot(x, w1b[...]) + b1_ref[0]
        x_glu = jnp.minimum(hb[:, :D_FF], SWIGLU_LIMIT)
        x_lin = jnp.clip(hb[:, D_FF:], -SWIGLU_LIMIT, SWIGLU_LIMIT)
        act = x_glu * jax.nn.sigmoid(SWIGLU_ALPHA * x_glu) * (x_lin + 1.0)
        o_ref[0:rows, :] = _pack_rows(_dot(act.astype(BF16), w2b[...]) + b2_ref[0])

    step = x_ref.shape[0] // MOE_ROW_PATHS
    for path in range(1, MOE_ROW_PATHS + 1):
        rows = path * step

        @pl.when((tv_ref[i] > rows - step) & (tv_ref[i] <= rows))
        def _(rows=rows):
            expert(rows)
            if rows < x_ref.shape[0]:
                o_ref[rows:, :] = jnp.zeros((x_ref.shape[0] - rows, o_ref.shape[1]), o_ref.dtype)

    @pl.when(tv_ref[i] == 0)
    def _():
        o_ref[...] = jnp.zeros_like(o_ref)


def _moe_experts(xs, tile_e, tile_rows, tile_slot, tile_next, w1, b1, w2, b2, layer):
    n_rows = xs.shape[0]
    tm = MOE_TILE
    nl, ne, d, ff2 = w1.shape
    bias_map = lambda i, te, tv, sl, nx: (layer, te[i], 0, 0)
    grid_spec = pltpu.PrefetchScalarGridSpec(
        num_scalar_prefetch=4,
        grid=(n_rows // tm,),
        in_specs=[pl.BlockSpec((tm, d // 2), lambda i, *_: (i, 0)),
                  pl.BlockSpec(memory_space=pl.ANY),
                  pl.BlockSpec((None, 1, 1, ff2), bias_map),
                  pl.BlockSpec(memory_space=pl.ANY),
                  pl.BlockSpec((None, 1, 1, d), bias_map)],
        out_specs=pl.BlockSpec((tm, d // 2), lambda i, *_: (i, 0)),
        scratch_shapes=[pltpu.VMEM((2, d, ff2), F32), pltpu.VMEM((2, ff2 // 2, d), F32),
                        pltpu.VMEM((d, ff2), BF16), pltpu.VMEM((ff2 // 2, d), BF16),
                        pltpu.SemaphoreType.DMA((2, MOE_W1_PARTS + MOE_W2_PARTS))],
    )
    return pl.pallas_call(
        functools.partial(_moe_body, layer=layer),
        out_shape=jax.ShapeDtypeStruct((n_rows, d // 2), jnp.int32),
        grid_spec=grid_spec,
        compiler_params=_params(("arbitrary",)),
        name="moe",
    )(tile_e, tile_rows, tile_slot, tile_next, xs, w1, b1.reshape(nl, ne, 1, ff2), w2, b2.reshape(nl, ne, 1, d))


def _combined(h_ref, y_refs, gate_ref):
    gates = gate_ref[...]
    lo = jnp.zeros(y_refs[0].shape, F32)
    hi = jnp.zeros(y_refs[0].shape, F32)
    for k, y_ref in enumerate(y_refs):
        yl, yh = _unpack_rows(y_ref[...])
        lo = lo + yl * gates[:, k:k + 1]
        hi = hi + yh * gates[:, k:k + 1]
    return h_ref[...] + jnp.concatenate([lo, hi], axis=-1)


def _combine_final_body(h_ref, y0_ref, y1_ref, y2_ref, y3_ref, gate_ref, g_ref, out_hbm, buf, sem):
    i = pl.program_id(0)
    n = pl.num_programs(0)
    slot = i % 2
    per_step = h_ref.shape[0] // BLK
    steps_per_span = NRES // per_step

    def writes(step, s):
        span = step // steps_per_span
        r0 = (step % steps_per_span) * per_step
        return [pltpu.make_async_copy(buf.at[s, pl.ds(rr * BLK, BLK), :], out_hbm.at[span, :, r0 + rr, :],
                                      sem.at[s, rr]) for rr in range(per_step)]

    @pl.when(i >= 2)
    def _():
        for cp in writes(i - 2, slot):
            cp.wait()

    buf[slot] = _rms(_combined(h_ref, (y0_ref, y1_ref, y2_ref, y3_ref), gate_ref), g_ref[...])
    for cp in writes(i, slot):
        cp.start()

    @pl.when(i == n - 1)
    def _():
        for cp in writes(i, slot):
            cp.wait()

        @pl.when(i >= 1)
        def _():
            for cp in writes(i - 1, 1 - slot):
                cp.wait()


def _combine_final(h, y, gates, g):
    s, d = h.shape
    tm = min(ROW_TILE, s)
    nt = s // tm
    in_specs = ([pl.BlockSpec((tm, d), lambda i: (i, 0))]
                + [pl.BlockSpec((tm, d // 2), lambda i, k=k: (k * nt + i, 0)) for k in range(TOP_K)]
                + [pl.BlockSpec((tm, TOP_K), lambda i: (i, 0))])
    out = pl.pallas_call(
        _combine_final_body,
        out_shape=jax.ShapeDtypeStruct((s // SPAN, BLK, NRES, d), F32),
        grid=(nt,),
        in_specs=in_specs + [pl.BlockSpec((1, d), lambda i: (0, 0))],
        out_specs=pl.BlockSpec(memory_space=pl.ANY),
        scratch_shapes=[pltpu.VMEM((2, tm, d), F32), pltpu.SemaphoreType.DMA((2, tm // BLK))],
        compiler_params=_params(("arbitrary",)),
        name="combine_final",
    )(h, y, y, y, y, gates, g.reshape(1, d))
    return out.reshape(s, d)


def _layer(state, mem, biases, p, l, g_final):
    row = lambda a: a.reshape(1, -1).astype(F32)
    q_scale = jnp.concatenate([jnp.full((D_A,), HD_A ** -0.5 * LOG2E, F32), jnp.ones((D_IN - D_A,), F32)])
    w_in = (p["w_in"][l] * q_scale).astype(BF16)
    if l == 0:
        h, proj = state, _inproj(state, p["norm_mix"][l], w_in)
    else:
        h, proj = _combine_inproj(*state, p["norm_mix"][l], w_in)
    ya = _dilated_attention(proj, biases)
    ops = _s5_operators(p["s5_a_re"][l], p["s5_a_im"][l], p["s5_b_re"][l], p["s5_b_im"][l],
                        p["s5_c_re"][l], p["s5_c_im"][l], p["s5_log_dt"][l], p["s5_d"][l])
    yb = _s5_core(proj, ops)
    kv = _memkv(mem, p["norm_mem"][l], p["w_xkv"][l].astype(BF16))
    wr = p["w_router"][l].astype(F32)
    wr_hi = wr.astype(BF16)
    wr2 = jnp.pad(jnp.concatenate([wr_hi, (wr - wr_hi.astype(F32)).astype(BF16)], axis=1),
                  ((0, 0), (0, LANES - 2 * N_EXPERTS)))
    br = jnp.pad(p["b_router"][l].astype(F32), (0, LANES - N_EXPERTS)).reshape(1, LANES)
    h2, xn, logits = _mid(
        h, ya, yb, p["w_glu"][l].astype(BF16), row(p["b_glu"][l]), row(p["g_out_attn"][l]),
        row(p["g_out_ssm"][l]), p["w_out"][l].astype(BF16), row(p["norm_xattn"][l]),
        (p["w_xq"][l] * (HD_X ** -0.5)).astype(BF16), kv[:, :D_X], kv[:, D_X:],
        p["w_xo"][l].astype(BF16), row(p["norm_moe"][l]), wr2, br)
    gates, dest, n_rows, tile_e, tile_rows, tile_slot, tile_next = _route(logits, MOE_TILE)
    out = _moe_experts(_scatter_rows(xn, dest, n_rows), tile_e, tile_rows, tile_slot, tile_next,
                       p["w1"], p["b1"], p["w2"], p["b2"], l)
    y = _gather_rows(out, dest.reshape(-1))
    return _combine_final(h2, y, gates, g_final) if l == DEPTH - 1 else (h2, y, gates)


def kernel(x, mem, rel_bias, norm_mix, w_in, s5_a_re, s5_a_im, s5_b_re, s5_b_im, s5_c_re, s5_c_im, s5_log_dt, s5_d, w_glu, b_glu, g_out_attn, g_out_ssm, w_out, norm_xattn, norm_mem, w_xq, w_xkv, w_xo, norm_moe, w_router, b_router, w1, b1, w2, b2, norm_final):
    p = dict(norm_mix=norm_mix, w_in=w_in, s5_a_re=s5_a_re, s5_a_im=s5_a_im, s5_b_re=s5_b_re,
             s5_b_im=s5_b_im, s5_c_re=s5_c_re, s5_c_im=s5_c_im, s5_log_dt=s5_log_dt, s5_d=s5_d,
             w_glu=w_glu, b_glu=b_glu, g_out_attn=g_out_attn, g_out_ssm=g_out_ssm, w_out=w_out,
             norm_xattn=norm_xattn, norm_mem=norm_mem, w_xq=w_xq, w_xkv=w_xkv, w_xo=w_xo,
             norm_moe=norm_moe, w_router=w_router, b_router=b_router, w1=w1, b1=b1, w2=w2, b2=b2)
    biases = [_attn_bias(rel_bias, window, dil, perm)
              for (window, dil), perm in zip(WIN_DIL, (_PERM_D1, _PERM_D4, _PERM_D16))]
    outs = []
    for b in range(x.shape[0]):
        h = _to_span_layout(x[b])
        for l in range(DEPTH):
            h = _layer(h, mem[b], biases, p, l, norm_final)
        outs.append(h)
    return jnp.stack(outs)
```
